```python
import jax, jax.numpy as jnp
from jax import lax
import numpy as np

D_MODEL = 1024
BATCH = 8
SEQ = 8192
DEPTH = 2

HEAD_DIM = 64
ATTN_WIDTH = D_MODEL // 2
N_Q_HEADS = ATTN_WIDTH // HEAD_DIM
N_KV_HEADS = 2
Q_PER_KV = N_Q_HEADS // N_KV_HEADS
KV_WIDTH = N_KV_HEADS * HEAD_DIM
WINDOW = 128
BLOCK = 128
CONV_WIDTH = D_MODEL // 4
CONV_KERNEL = 31
LRU_WIDTH = D_MODEL // 4
LRU_HEADS = 4
LRU_HEAD_DIM = LRU_WIDTH // LRU_HEADS
LRU_CONV_KERNEL = 4
LRU_C = 8.0
MIX_WIDTH = ATTN_WIDTH + CONV_WIDTH + LRU_WIDTH
IN_SPLIT_SIZES = (ATTN_WIDTH, KV_WIDTH, KV_WIDTH, CONV_WIDTH, CONV_WIDTH, LRU_WIDTH, LRU_WIDTH)
IN_WIDTH = sum(IN_SPLIT_SIZES)
IN_SPLIT_IDX = [int(v) for v in np.cumsum(IN_SPLIT_SIZES)[:-1]]
D_FF = 4 * D_MODEL
RMS_EPS = 1e-6
LN_EPS = 1e-5
MASK_VALUE = -1e30

kernel_name = "hymba_style_swa_conformer_rglru_hybrid"


def rms_norm(x, g):
    xf = x.astype(jnp.float32)
    y = xf * lax.rsqrt(jnp.mean(xf * xf, axis=-1, keepdims=True) + RMS_EPS)
    return (y * g.astype(jnp.float32)).astype(x.dtype)


def layer_norm(x, g, b):
    xf = x.astype(jnp.float32)
    mu = jnp.mean(xf, axis=-1, keepdims=True)
    xc = xf - mu
    y = xc * lax.rsqrt(jnp.mean(xc * xc, axis=-1, keepdims=True) + LN_EPS)
    return (y * g.astype(jnp.float32) + b.astype(jnp.float32)).astype(x.dtype)


def causal_depthwise_conv(x, w, b):
    k_width, c = w.shape
    out = lax.conv_general_dilated(
        x, w[:, None, :].astype(x.dtype), window_strides=(1,), padding=[(k_width - 1, 0)],
        dimension_numbers=("NWC", "WIO", "NWC"), feature_group_count=c)
    return out + b.astype(x.dtype)


def sliding_window_attention(q, k, v, sinks):
    b, s, _ = q.shape
    nb = s // BLOCK
    q = q.reshape(b, nb, BLOCK, N_KV_HEADS, Q_PER_KV, HEAD_DIM)
    k = k.reshape(b, nb, BLOCK, N_KV_HEADS, HEAD_DIM)
    v = v.reshape(b, nb, BLOCK, N_KV_HEADS, HEAD_DIM)
    k_band = jnp.concatenate([jnp.concatenate([jnp.zeros_like(k[:, :1]), k[:, :-1]], axis=1), k], axis=2)
    v_band = jnp.concatenate([jnp.concatenate([jnp.zeros_like(v[:, :1]), v[:, :-1]], axis=1), v], axis=2)
    scores = jnp.einsum("bnqhgd,bnkhd->bnhgqk", q, k_band).astype(jnp.float32) * (HEAD_DIM ** -0.5)
    blk = jnp.arange(nb)[:, None]
    q_pos = blk * BLOCK + jnp.arange(BLOCK)[None, :]
    k_pos = (blk - 1) * BLOCK + jnp.arange(2 * BLOCK)[None, :]
    diff = q_pos[:, :, None] - k_pos[:, None, :]
    mask = (diff >= 0) & (diff < WINDOW) & (k_pos[:, None, :] >= 0)
    scores = jnp.where(mask[None, :, None, None], scores, MASK_VALUE)
    sink = sinks.astype(jnp.float32).reshape(N_KV_HEADS, Q_PER_KV)[None, None, :, :, None, None]
    m = jnp.maximum(jnp.max(scores, axis=-1, keepdims=True), sink)
    p = jnp.exp(scores - m)
    probs = p / (jnp.sum(p, axis=-1, keepdims=True) + jnp.exp(sink - m))
    out = jnp.einsum("bnhgqk,bnkhd->bnqhgd", probs.astype(v.dtype), v_band)
    return out.reshape(b, s, ATTN_WIDTH)


def conformer_conv(u_val, u_gate, dw_w, dw_b, ln_g, ln_b):
    u = u_val * jax.nn.sigmoid(u_gate)
    u = causal_depthwise_conv(u, dw_w, dw_b)
    u = layer_norm(u, ln_g, ln_b)
    return jax.nn.silu(u)


def _linear_recurrence_combine(c1, c2):
    a1, b1 = c1
    a2, b2 = c2
    return a1 * a2, a2 * b1 + b2


def rglru_branch(u_x, u_gate, conv_w, conv_b, wa, ba, wx, bx, lam):
    xc = causal_depthwise_conv(u_x, conv_w, conv_b)
    b, s, _ = xc.shape
    xh = xc.reshape(b, s, LRU_HEADS, LRU_HEAD_DIM)
    r = jax.nn.sigmoid(jnp.einsum("bshi,hij->bshj", xh, wa) + ba).reshape(b, s, LRU_WIDTH)
    i = jax.nn.sigmoid(jnp.einsum("bshi,hij->bshj", xh, wx) + bx).reshape(b, s, LRU_WIDTH)
    log_a = (-LRU_C * r.astype(jnp.float32)) * jax.nn.softplus(-lam.astype(jnp.float32))
    a = jnp.exp(log_a)
    gated_x = jnp.sqrt(-jnp.expm1(2.0 * log_a)) * (i * xc).astype(jnp.float32)
    _, h = lax.associative_scan(_linear_recurrence_combine, (a, gated_x), axis=1)
    return h.astype(u_x.dtype) * jax.nn.gelu(u_gate)


def _fwd_setup_inputs(seed: int = 0) -> dict:
    key = jax.random.key(seed)
    ks = jax.random.split(key, 24)
    f32 = jnp.float32

    def nrm(k, shape, scale):
        return jax.random.normal(k, shape, f32) * scale

    def gain(k, shape):
        return 1.0 + 0.02 * jax.random.normal(k, shape, f32)

    a0 = jax.random.uniform(ks[14], (DEPTH, LRU_WIDTH), f32, 0.9, 0.999)
    s0 = a0 ** (1.0 / LRU_C)
    lru_lambda = jnp.log(s0) - jnp.log1p(-s0)
    return {
        "x": jax.random.normal(ks[0], (BATCH, SEQ, D_MODEL), f32),
        "norm1": gain(ks[1], (DEPTH, D_MODEL)),
        "w_in": nrm(ks[2], (DEPTH, D_MODEL, IN_WIDTH), D_MODEL ** -0.5),
        "attn_sinks": nrm(ks[3], (DEPTH, N_Q_HEADS), 0.5),
        "conv_dw_w": nrm(ks[4], (DEPTH, CONV_KERNEL, CONV_WIDTH), CONV_KERNEL ** -0.5),
        "conv_dw_b": nrm(ks[5], (DEPTH, CONV_WIDTH), 0.01),
        "conv_ln_g": gain(ks[6], (DEPTH, CONV_WIDTH)),
        "conv_ln_b": nrm(ks[7], (DEPTH, CONV_WIDTH), 0.01),
        "lru_conv_w": nrm(ks[8], (DEPTH, LRU_CONV_KERNEL, LRU_WIDTH), LRU_CONV_KERNEL ** -0.5),
        "lru_conv_b": nrm(ks[9], (DEPTH, LRU_WIDTH), 0.01),
        "lru_wa": nrm(ks[10], (DEPTH, LRU_HEADS, LRU_HEAD_DIM, LRU_HEAD_DIM), LRU_HEAD_DIM ** -0.5),
        "lru_ba": nrm(ks[11], (DEPTH, LRU_HEADS, LRU_HEAD_DIM), 0.01),
        "lru_wx": nrm(ks[12], (DEPTH, LRU_HEADS, LRU_HEAD_DIM, LRU_HEAD_DIM), LRU_HEAD_DIM ** -0.5),
        "lru_bx": nrm(ks[13], (DEPTH, LRU_HEADS, LRU_HEAD_DIM), 0.01),
        "lru_lambda": lru_lambda,
        "mix_norm": gain(ks[15], (DEPTH, MIX_WIDTH)),
        "w_out": nrm(ks[16], (DEPTH, MIX_WIDTH, D_MODEL), MIX_WIDTH ** -0.5),
        "norm2": gain(ks[17], (DEPTH, D_MODEL)),
        "w_up": nrm(ks[18], (DEPTH, D_MODEL, D_FF), D_MODEL ** -0.5),
        "w_down": nrm(ks[19], (DEPTH, D_FF, D_MODEL), D_FF ** -0.5),
        "final_norm": gain(ks[20], (D_MODEL,)),
    }


def _fwd_reference(x, norm1, w_in, attn_sinks, conv_dw_w, conv_dw_b, conv_ln_g, conv_ln_b,
              lru_conv_w, lru_conv_b, lru_wa, lru_ba, lru_wx, lru_bx, lru_lambda,
              mix_norm, w_out, norm2, w_up, w_down, final_norm):
    h = x
    a_end = ATTN_WIDTH
    c_end = ATTN_WIDTH + CONV_WIDTH
    for l in range(DEPTH):
        hn = rms_norm(h, norm1[l])
        z = hn @ w_in[l]
        q, k, v, c_val, c_gate, r_x, r_gate = jnp.split(z, IN_SPLIT_IDX, axis=-1)
        y_attn = sliding_window_attention(q, k, v, attn_sinks[l])
        y_conv = conformer_conv(c_val, c_gate, conv_dw_w[l], conv_dw_b[l], conv_ln_g[l], conv_ln_b[l])
        y_lru = rglru_branch(r_x, r_gate, lru_conv_w[l], lru_conv_b[l], lru_wa[l], lru_ba[l],
                             lru_wx[l], lru_bx[l], lru_lambda[l])
        g = mix_norm[l]
        y = jnp.concatenate([rms_norm(y_attn, g[:a_end]),
                             rms_norm(y_conv, g[a_end:c_end]),
                             rms_norm(y_lru, g[c_end:])], axis=-1)
        h = h + y @ w_out[l]
        hn = rms_norm(h, norm2[l])
        h = h + jnp.square(jax.nn.relu(hn @ w_up[l])) @ w_down[l]
    return rms_norm(h, final_norm)


import jax as _jax
import jax.numpy as _jnp

TWIN_FORMAT = 'train_step'
FWD_PARAMS = ['x', 'norm1', 'w_in', 'attn_sinks', 'conv_dw_w', 'conv_dw_b', 'conv_ln_g', 'conv_ln_b', 'lru_conv_w', 'lru_conv_b', 'lru_wa', 'lru_ba', 'lru_wx', 'lru_bx', 'lru_lambda', 'mix_norm', 'w_out', 'norm2', 'w_up', 'w_down', 'final_norm']
TWIN_WEIGHTS = ['norm1', 'w_in', 'attn_sinks', 'conv_dw_w', 'conv_dw_b', 'conv_ln_g', 'conv_ln_b', 'lru_conv_w', 'lru_conv_b', 'lru_wa', 'lru_ba', 'lru_wx', 'lru_bx', 'lru_lambda', 'mix_norm', 'w_out', 'norm2', 'w_up', 'w_down', 'final_norm']
TWIN_DIFF_INPUT = 'x'
TWIN_INPUTS = ['x', 'norm1', 'w_in', 'attn_sinks', 'conv_dw_w', 'conv_dw_b', 'conv_ln_g', 'conv_ln_b', 'lru_conv_w', 'lru_conv_b', 'lru_wa', 'lru_ba', 'lru_wx', 'lru_bx', 'lru_lambda', 'mix_norm', 'w_out', 'norm2', 'w_up', 'w_down', 'final_norm', 'loss_target', 'm_norm1', 'm_w_in', 'm_attn_sinks', 'm_conv_dw_w', 'm_conv_dw_b', 'm_conv_ln_g', 'm_conv_ln_b', 'm_lru_conv_w', 'm_lru_conv_b', 'm_lru_wa', 'm_lru_ba', 'm_lru_wx', 'm_lru_bx', 'm_lru_lambda', 'm_mix_norm', 'm_w_out', 'm_norm2', 'm_w_up', 'm_w_down', 'm_final_norm', 'v_norm1', 'v_w_in', 'v_attn_sinks', 'v_conv_dw_w', 'v_conv_dw_b', 'v_conv_ln_g', 'v_conv_ln_b', 'v_lru_conv_w', 'v_lru_conv_b', 'v_lru_wa', 'v_lru_ba', 'v_lru_wx', 'v_lru_bx', 'v_lru_lambda', 'v_mix_norm', 'v_w_out', 'v_norm2', 'v_w_up', 'v_w_down', 'v_final_norm']
TWIN_OUTPUTS = ['loss', 'grad_x', 'grad_norm1', 'grad_w_in', 'grad_attn_sinks', 'grad_conv_dw_w', 'grad_conv_dw_b', 'grad_conv_ln_g', 'grad_conv_ln_b', 'grad_lru_conv_w', 'grad_lru_conv_b', 'grad_lru_wa', 'grad_lru_ba', 'grad_lru_wx', 'grad_lru_bx', 'grad_lru_lambda', 'grad_mix_norm', 'grad_w_out', 'grad_norm2', 'grad_w_up', 'grad_w_down', 'grad_final_norm', 'delta_norm1', 'delta_w_in', 'delta_attn_sinks', 'delta_conv_dw_w', 'delta_conv_dw_b', 'delta_conv_ln_g', 'delta_conv_ln_b', 'delta_lru_conv_w', 'delta_lru_conv_b', 'delta_lru_wa', 'delta_lru_ba', 'delta_lru_wx', 'delta_lru_bx', 'delta_lru_lambda', 'delta_mix_norm', 'delta_w_out', 'delta_norm2', 'delta_w_up', 'delta_w_down', 'delta_final_norm', 'new_m_norm1', 'new_m_w_in', 'new_m_attn_sinks', 'new_m_conv_dw_w', 'new_m_conv_dw_b', 'new_m_conv_ln_g', 'new_m_conv_ln_b', 'new_m_lru_conv_w', 'new_m_lru_conv_b', 'new_m_lru_wa', 'new_m_lru_ba', 'new_m_lru_wx', 'new_m_lru_bx', 'new_m_lru_lambda', 'new_m_mix_norm', 'new_m_w_out', 'new_m_norm2', 'new_m_w_up', 'new_m_w_down', 'new_m_final_norm', 'new_v_norm1', 'new_v_w_in', 'new_v_attn_sinks', 'new_v_conv_dw_w', 'new_v_conv_dw_b', 'new_v_conv_ln_g', 'new_v_conv_ln_b', 'new_v_lru_conv_w', 'new_v_lru_conv_b', 'new_v_lru_wa', 'new_v_lru_ba', 'new_v_lru_wx', 'new_v_lru_bx', 'new_v_lru_lambda', 'new_v_mix_norm', 'new_v_w_out', 'new_v_norm2', 'new_v_w_up', 'new_v_w_down', 'new_v_final_norm']
TWIN_LEAF_KINDS = {'loss': 'loss', 'grad_x': 'grad_x', 'grad_norm1': 'grad_w', 'grad_w_in': 'grad_w', 'grad_attn_sinks': 'grad_w', 'grad_conv_dw_w': 'grad_w', 'grad_conv_dw_b': 'grad_w', 'grad_conv_ln_g': 'grad_w', 'grad_conv_ln_b': 'grad_w', 'grad_lru_conv_w': 'grad_w', 'grad_lru_conv_b': 'grad_w', 'grad_lru_wa': 'grad_w', 'grad_lru_ba': 'grad_w', 'grad_lru_wx': 'grad_w', 'grad_lru_bx': 'grad_w', 'grad_lru_lambda': 'grad_w', 'grad_mix_norm': 'grad_w', 'grad_w_out': 'grad_w', 'grad_norm2': 'grad_w', 'grad_w_up': 'grad_w', 'grad_w_down': 'grad_w', 'grad_final_norm': 'grad_w', 'delta_norm1': 'delta_w', 'delta_w_in': 'delta_w', 'delta_attn_sinks': 'delta_w', 'delta_conv_dw_w': 'delta_w', 'delta_conv_dw_b': 'delta_w', 'delta_conv_ln_g': 'delta_w', 'delta_conv_ln_b': 'delta_w', 'delta_lru_conv_w': 'delta_w', 'delta_lru_conv_b': 'delta_w', 'delta_lru_wa': 'delta_w', 'delta_lru_ba': 'delta_w', 'delta_lru_wx': 'delta_w', 'delta_lru_bx': 'delta_w', 'delta_lru_lambda': 'delta_w', 'delta_mix_norm': 'delta_w', 'delta_w_out': 'delta_w', 'delta_norm2': 'delta_w', 'delta_w_up': 'delta_w', 'delta_w_down': 'delta_w', 'delta_final_norm': 'delta_w', 'new_m_norm1': 'new_m', 'new_m_w_in': 'new_m', 'new_m_attn_sinks': 'new_m', 'new_m_conv_dw_w': 'new_m', 'new_m_conv_dw_b': 'new_m', 'new_m_conv_ln_g': 'new_m', 'new_m_conv_ln_b': 'new_m', 'new_m_lru_conv_w': 'new_m', 'new_m_lru_conv_b': 'new_m', 'new_m_lru_wa': 'new_m', 'new_m_lru_ba': 'new_m', 'new_m_lru_wx': 'new_m', 'new_m_lru_bx': 'new_m', 'new_m_lru_lambda': 'new_m', 'new_m_mix_norm': 'new_m', 'new_m_w_out': 'new_m', 'new_m_norm2': 'new_m', 'new_m_w_up': 'new_m', 'new_m_w_down': 'new_m', 'new_m_final_norm': 'new_m', 'new_v_norm1': 'new_v', 'new_v_w_in': 'new_v', 'new_v_attn_sinks': 'new_v', 'new_v_conv_dw_w': 'new_v', 'new_v_conv_dw_b': 'new_v', 'new_v_conv_ln_g': 'new_v', 'new_v_conv_ln_b': 'new_v', 'new_v_lru_conv_w': 'new_v', 'new_v_lru_conv_b': 'new_v', 'new_v_lru_wa': 'new_v', 'new_v_lru_ba': 'new_v', 'new_v_lru_wx': 'new_v', 'new_v_lru_bx': 'new_v', 'new_v_lru_lambda': 'new_v', 'new_v_mix_norm': 'new_v', 'new_v_w_out': 'new_v', 'new_v_norm2': 'new_v', 'new_v_w_up': 'new_v', 'new_v_w_down': 'new_v', 'new_v_final_norm': 'new_v'}


def _forward(args):
    return _fwd_reference(*[args[k] for k in FWD_PARAMS])


def _output_shape():
    def fwd():
        inp = _fwd_setup_inputs(0)
        return _fwd_reference(*[inp[k] for k in FWD_PARAMS])
    out = _jax.eval_shape(fwd)
    return out.shape, out.dtype

N_MICROBATCH = 1
ADAM_LR = 0.001
ADAM_B1 = 0.9
ADAM_B2 = 0.999
ADAM_EPS = 1e-08
ADAM_WD = 0.01
ADAM_STEP = 10
PER_EXAMPLE_BATCH_AXIS = {'x': 0, 'loss_target': 0}
SHARED_INPUTS = []
_WEIGHT_DTYPES = {'norm1': _jnp.float32, 'w_in': _jnp.float32, 'attn_sinks': _jnp.float32, 'conv_dw_w': _jnp.float32, 'conv_dw_b': _jnp.float32, 'conv_ln_g': _jnp.float32, 'conv_ln_b': _jnp.float32, 'lru_conv_w': _jnp.float32, 'lru_conv_b': _jnp.float32, 'lru_wa': _jnp.float32, 'lru_ba': _jnp.float32, 'lru_wx': _jnp.float32, 'lru_bx': _jnp.float32, 'lru_lambda': _jnp.float32, 'mix_norm': _jnp.float32, 'w_out': _jnp.float32, 'norm2': _jnp.float32, 'w_up': _jnp.float32, 'w_down': _jnp.float32, 'final_norm': _jnp.float32}
MOMENT_SCALE = {'norm1': 2.399985e-01, 'w_in': 1.930717e-01, 'attn_sinks': 7.598335e-02, 'conv_dw_w': 1.746015e-01, 'conv_dw_b': 6.267649e-01, 'conv_ln_g': 3.026262e-01, 'conv_ln_b': 3.599895e-01, 'lru_conv_w': 1.890323e-01, 'lru_conv_b': 1.738187e+00, 'lru_wa': 5.825352e-02, 'lru_ba': 5.340382e-02, 'lru_wx': 1.055949e-01, 'lru_bx': 6.644241e-02, 'lru_lambda': 1.028703e-01, 'mix_norm': 1.884407e-01, 'w_out': 2.037857e-01, 'norm2': 1.903821e-01, 'w_up': 9.144566e-02, 'w_down': 2.296904e-01, 'final_norm': 6.575712e+01}


def _to_microbatches(a, axis):
    t = _jnp.moveaxis(a, axis, 0)
    t = t.reshape((N_MICROBATCH, t.shape[0] // N_MICROBATCH) + t.shape[1:])
    return _jnp.moveaxis(t, 1, axis + 1)


def setup_inputs(seed: int = 0) -> dict:
    inp = _fwd_setup_inputs(seed)
    key = _jax.random.fold_in(_jax.random.key(seed), 7919)
    shape, _ = _output_shape()
    out = dict(inp)
    out["loss_target"] = _jax.random.normal(_jax.random.fold_in(key, 0), shape, _jnp.float32)
    for i, name in enumerate(TWIN_WEIGHTS):
        w = inp[name].astype(_jnp.float32)
        if MOMENT_SCALE is None:
            s = _jnp.sqrt(_jnp.mean(_jnp.square(w)) + 1e-30)
        else:
            s = MOMENT_SCALE[name]
        km, kv = _jax.random.split(_jax.random.fold_in(key, i + 1))
        out[name] = w
        out["m_" + name] = s * _jax.random.normal(km, w.shape, _jnp.float32)
        out["v_" + name] = (s * s) * _jax.random.uniform(kv, w.shape, _jnp.float32, 0.5, 1.5)
    if N_MICROBATCH > 1:
        for name, axis in PER_EXAMPLE_BATCH_AXIS.items():
            out[name] = _to_microbatches(out[name], axis)
    return {'x': out['x'], 'norm1': out['norm1'], 'w_in': out['w_in'], 'attn_sinks': out['attn_sinks'], 'conv_dw_w': out['conv_dw_w'], 'conv_dw_b': out['conv_dw_b'], 'conv_ln_g': out['conv_ln_g'], 'conv_ln_b': out['conv_ln_b'], 'lru_conv_w': out['lru_conv_w'], 'lru_conv_b': out['lru_conv_b'], 'lru_wa': out['lru_wa'], 'lru_ba': out['lru_ba'], 'lru_wx': out['lru_wx'], 'lru_bx': out['lru_bx'], 'lru_lambda': out['lru_lambda'], 'mix_norm': out['mix_norm'], 'w_out': out['w_out'], 'norm2': out['norm2'], 'w_up': out['w_up'], 'w_down': out['w_down'], 'final_norm': out['final_norm'], 'loss_target': out['loss_target'], 'm_norm1': out['m_norm1'], 'm_w_in': out['m_w_in'], 'm_attn_sinks': out['m_attn_sinks'], 'm_conv_dw_w': out['m_conv_dw_w'], 'm_conv_dw_b': out['m_conv_dw_b'], 'm_conv_ln_g': out['m_conv_ln_g'], 'm_conv_ln_b': out['m_conv_ln_b'], 'm_lru_conv_w': out['m_lru_conv_w'], 'm_lru_conv_b': out['m_lru_conv_b'], 'm_lru_wa': out['m_lru_wa'], 'm_lru_ba': out['m_lru_ba'], 'm_lru_wx': out['m_lru_wx'], 'm_lru_bx': out['m_lru_bx'], 'm_lru_lambda': out['m_lru_lambda'], 'm_mix_norm': out['m_mix_norm'], 'm_w_out': out['m_w_out'], 'm_norm2': out['m_norm2'], 'm_w_up': out['m_w_up'], 'm_w_down': out['m_w_down'], 'm_final_norm': out['m_final_norm'], 'v_norm1': out['v_norm1'], 'v_w_in': out['v_w_in'], 'v_attn_sinks': out['v_attn_sinks'], 'v_conv_dw_w': out['v_conv_dw_w'], 'v_conv_dw_b': out['v_conv_dw_b'], 'v_conv_ln_g': out['v_conv_ln_g'], 'v_conv_ln_b': out['v_conv_ln_b'], 'v_lru_conv_w': out['v_lru_conv_w'], 'v_lru_conv_b': out['v_lru_conv_b'], 'v_lru_wa': out['v_lru_wa'], 'v_lru_ba': out['v_lru_ba'], 'v_lru_wx': out['v_lru_wx'], 'v_lru_bx': out['v_lru_bx'], 'v_lru_lambda': out['v_lru_lambda'], 'v_mix_norm': out['v_mix_norm'], 'v_w_out': out['v_w_out'], 'v_norm2': out['v_norm2'], 'v_w_up': out['v_w_up'], 'v_w_down': out['v_w_down'], 'v_final_norm': out['v_final_norm']}


def _loss(weights, diff, rest, loss_target):
    with _jax.named_scope("forward"):
        args = {**rest, TWIN_DIFF_INPUT: diff, **{k: w.astype(_WEIGHT_DTYPES[k]) for k, w in weights.items()}}
        y = _forward(args)
    with _jax.named_scope("loss_head"):
        err = _jnp.square(y.astype(_jnp.float32) - loss_target)
        return 0.5 * _jnp.sum(_jnp.mean(err, axis=-1)) if err.ndim else 0.5 * err


def _adamw(w, g, m, v):
    m = ADAM_B1 * m + (1.0 - ADAM_B1) * g
    v = ADAM_B2 * v + (1.0 - ADAM_B2) * _jnp.square(g)
    m_hat = m / (1.0 - ADAM_B1 ** ADAM_STEP)
    v_hat = v / (1.0 - ADAM_B2 ** ADAM_STEP)
    delta = -ADAM_LR * (m_hat / (_jnp.sqrt(v_hat) + ADAM_EPS) + ADAM_WD * w)
    return delta, m, v


def reference(x, norm1, w_in, attn_sinks, conv_dw_w, conv_dw_b, conv_ln_g, conv_ln_b, lru_conv_w, lru_conv_b, lru_wa, lru_ba, lru_wx, lru_bx, lru_lambda, mix_norm, w_out, norm2, w_up, w_down, final_norm, loss_target, m_norm1, m_w_in, m_attn_sinks, m_conv_dw_w, m_conv_dw_b, m_conv_ln_g, m_conv_ln_b, m_lru_conv_w, m_lru_conv_b, m_lru_wa, m_lru_ba, m_lru_wx, m_lru_bx, m_lru_lambda, m_mix_norm, m_w_out, m_norm2, m_w_up, m_w_down, m_final_norm, v_norm1, v_w_in, v_attn_sinks, v_conv_dw_w, v_conv_dw_b, v_conv_ln_g, v_conv_ln_b, v_lru_conv_w, v_lru_conv_b, v_lru_wa, v_lru_ba, v_lru_wx, v_lru_bx, v_lru_lambda, v_mix_norm, v_w_out, v_norm2, v_w_up, v_w_down, v_final_norm):
    given = dict(x=x, norm1=norm1, w_in=w_in, attn_sinks=attn_sinks, conv_dw_w=conv_dw_w, conv_dw_b=conv_dw_b, conv_ln_g=conv_ln_g, conv_ln_b=conv_ln_b, lru_conv_w=lru_conv_w, lru_conv_b=lru_conv_b, lru_wa=lru_wa, lru_ba=lru_ba, lru_wx=lru_wx, lru_bx=lru_bx, lru_lambda=lru_lambda, mix_norm=mix_norm, w_out=w_out, norm2=norm2, w_up=w_up, w_down=w_down, final_norm=final_norm, loss_target=loss_target, m_norm1=m_norm1, m_w_in=m_w_in, m_attn_sinks=m_attn_sinks, m_conv_dw_w=m_conv_dw_w, m_conv_dw_b=m_conv_dw_b, m_conv_ln_g=m_conv_ln_g, m_conv_ln_b=m_conv_ln_b, m_lru_conv_w=m_lru_conv_w, m_lru_conv_b=m_lru_conv_b, m_lru_wa=m_lru_wa, m_lru_ba=m_lru_ba, m_lru_wx=m_lru_wx, m_lru_bx=m_lru_bx, m_lru_lambda=m_lru_lambda, m_mix_norm=m_mix_norm, m_w_out=m_w_out, m_norm2=m_norm2, m_w_up=m_w_up, m_w_down=m_w_down, m_final_norm=m_final_norm, v_norm1=v_norm1, v_w_in=v_w_in, v_attn_sinks=v_attn_sinks, v_conv_dw_w=v_conv_dw_w, v_conv_dw_b=v_conv_dw_b, v_conv_ln_g=v_conv_ln_g, v_conv_ln_b=v_conv_ln_b, v_lru_conv_w=v_lru_conv_w, v_lru_conv_b=v_lru_conv_b, v_lru_wa=v_lru_wa, v_lru_ba=v_lru_ba, v_lru_wx=v_lru_wx, v_lru_bx=v_lru_bx, v_lru_lambda=v_lru_lambda, v_mix_norm=v_mix_norm, v_w_out=v_w_out, v_norm2=v_norm2, v_w_up=v_w_up, v_w_down=v_w_down, v_final_norm=v_final_norm)
    weights = {n: given[n] for n in TWIN_WEIGHTS}
    shared = {n: given[n] for n in SHARED_INPUTS}
    per_example = {n: given[n] for n in ['x']}
    grad_fn = _jax.value_and_grad(_loss, argnums=(0, 1))

    def one_microbatch(ex, loss_target):
        ex = dict(ex)
        diff = ex.pop(TWIN_DIFF_INPUT)
        return grad_fn(weights, diff, {**shared, **ex}, loss_target)

    if N_MICROBATCH == 1:
        loss, (grad_w, grad_x) = one_microbatch(per_example, given["loss_target"])
    else:
        def body(carry, xs):
            loss_sum, grad_sum = carry
            l_k, (gw_k, gx_k) = one_microbatch(xs[0], xs[1])
            with _jax.named_scope("update"):
                return (loss_sum + l_k, _jax.tree.map(_jnp.add, grad_sum, gw_k)), gx_k

        init = (_jnp.zeros((), _jnp.float32), _jax.tree.map(_jnp.zeros_like, weights))
        (loss, grad_w), grad_x = _jax.lax.scan(body, init, (per_example, given["loss_target"]))
    with _jax.named_scope("update"):
        delta_w, new_m, new_v = {}, {}, {}
        for n in TWIN_WEIGHTS:
            delta_w[n], new_m[n], new_v[n] = _adamw(weights[n], grad_w[n], given["m_" + n], given["v_" + n])
    return (loss, grad_x, *[grad_w[n] for n in TWIN_WEIGHTS], *[delta_w[n] for n in TWIN_WEIGHTS],
            *[new_m[n] for n in TWIN_WEIGHTS], *[new_v[n] for n in TWIN_WEIGHTS])
```

```python
import functools
import math

import jax
import jax.numpy as jnp
from jax import lax
from jax.experimental import pallas as pl
from jax.experimental.pallas import tpu as pltpu

f32 = jnp.float32
bf16 = jnp.bfloat16
SDS = jax.ShapeDtypeStruct

D_MODEL = 1024
DEPTH = 2
ATTN_W = 512
KV_W = 128
HEAD_DIM = 64
N_HEADS = 8
BLOCK = 128
CONV_W = 256
CONV_K = 31
LRU_W = 256
LRU_K = 4
LRU_HEADS = 4
LRU_C = 8.0
IN_W = 1792
D_FF = 4096
N_SHARD = 4
FF_CHUNK = D_FF // N_SHARD
RMS_EPS = 1e-6
LN_EPS = 1e-5
MASK_VALUE = -1e30
HALO = 32
LANES = 128
VMEM_LIMIT = 56 * 1024 * 1024

ADAM_LR = 0.001
ADAM_B1 = 0.9
ADAM_B2 = 0.999
ADAM_EPS = 1e-08
ADAM_WD = 0.01
ADAM_STEP = 10

MESH = pl.DeviceIdType.MESH


def _dot(a, b):
    return jnp.dot(a, b, preferred_element_type=f32)


def _dot_nt(a, b):
    return lax.dot_general(a, b, (((1,), (1,)), ((), ())), preferred_element_type=f32)


def _dot_tn(a, b):
    return lax.dot_general(a, b, (((0,), (0,)), ((), ())), preferred_element_type=f32)


def _rms_fwd(x, g):
    r = lax.rsqrt(jnp.mean(x * x, axis=-1, keepdims=True) + RMS_EPS)
    return x * r * g, r


def _rms_bwd(dy, x, r, g):
    t = dy * g
    dx = r * t - x * (r * r * r) * jnp.mean(t * x, axis=-1, keepdims=True)
    dg = jnp.sum(dy * x * r, axis=0, keepdims=True)
    return dx, dg


def _sigmoid(x):
    return jax.nn.sigmoid(x)


_GELU_K = math.sqrt(2.0 / math.pi)


def _gelu(x):
    t = jnp.tanh(_GELU_K * (x + 0.044715 * x * x * x))
    return 0.5 * x * (1.0 + t), t


def _gelu_grad(x, t):
    return 0.5 * (1.0 + t) + 0.5 * x * (1.0 - t * t) * _GELU_K * (1.0 + 3.0 * 0.044715 * x * x)


def _log1p(x):
    return jnp.where(x < 1e-4, x - 0.5 * x * x, jnp.log(1.0 + x))


def _softplus(x):
    return jnp.maximum(x, 0.0) + _log1p(jnp.exp(-jnp.abs(x)))


def _neg_expm1(x):
    series = -x * (1.0 + 0.5 * x * (1.0 + x * (1.0 / 3.0) * (1.0 + 0.25 * x)))
    return jnp.where(x > -0.01, series, 1.0 - jnp.exp(x))


def _conv_taps(xpad, w, k_width):
    acc = None
    for k in range(k_width):
        sh = (k_width - 1) - k
        xs = xpad if sh == 0 else pltpu.roll(xpad, sh, 0)
        term = xs[HALO:] * w[k:k + 1, :]
        acc = term if acc is None else acc + term
    return acc


def _conv_taps_bwd(dpad, upad, w, k_width, t_rows):
    n = t_rows + HALO
    d_in = None
    dw_rows = []
    d_out = dpad[:t_rows]
    for k in range(k_width):
        sh = (k_width - 1) - k
        ds = dpad if sh == 0 else pltpu.roll(dpad, n - sh, 0)
        term = ds[:t_rows] * w[k:k + 1, :]
        d_in = term if d_in is None else d_in + term
        us = upad if sh == 0 else pltpu.roll(upad, sh, 0)
        dw_rows.append(jnp.sum(d_out * us[HALO:], axis=0, keepdims=True))
    return d_in, dw_rows


def _scan_fwd(a, b):
    t_rows = a.shape[0]
    row = lax.broadcasted_iota(jnp.int32, a.shape, 0)
    d = 1
    while d < t_rows:
        a_sh = jnp.where(row < d, 1.0, pltpu.roll(a, d, 0))
        b_sh = jnp.where(row < d, 0.0, pltpu.roll(b, d, 0))
        b = a * b_sh + b
        a = a * a_sh
        d *= 2
    return a, b


def _scan_bwd(a, b):
    t_rows = a.shape[0]
    row = lax.broadcasted_iota(jnp.int32, a.shape, 0)
    d = 1
    while d < t_rows:
        a_sh = jnp.where(row >= t_rows - d, 1.0, pltpu.roll(a, t_rows - d, 0))
        b_sh = jnp.where(row >= t_rows - d, 0.0, pltpu.roll(b, t_rows - d, 0))
        b = b + a * b_sh
        a = a * a_sh
        d *= 2
    return b


def _full(shape):
    nd = len(shape)
    return pl.BlockSpec(shape, lambda *_: (0,) * nd)


def _params(sem, vmem=None):
    return pltpu.CompilerParams(dimension_semantics=sem, vmem_limit_bytes=vmem)


def _tile(s):
    return min(512, s)


def inproj_fwd(h, g1, w_in):
    s = h.shape[0]
    tm = _tile(s)

    def body(h_ref, g_ref, w_ref, z_ref):
        hn, _ = _rms_fwd(h_ref[...], g_ref[...])
        z_ref[...] = _dot(hn.astype(bf16), w_ref[...]).astype(bf16)

    return pl.pallas_call(
        body, name="inproj_fwd", grid=(s // tm,),
        in_specs=[pl.BlockSpec((tm, D_MODEL), lambda i: (i, 0)), _full((1, D_MODEL)), _full((D_MODEL, IN_W))],
        out_specs=pl.BlockSpec((tm, IN_W), lambda i: (i, 0)),
        out_shape=SDS((s, IN_W), bf16),
        compiler_params=_params(("parallel",), VMEM_LIMIT),
    )(h, g1, w_in)


def _attn_common(n, q, kvc, kvp):
    kb = jnp.concatenate([kvp[:, :KV_W], kvc[:, :KV_W]], axis=0)
    vb = jnp.concatenate([kvp[:, KV_W:], kvc[:, KV_W:]], axis=0)
    lane = lax.broadcasted_iota(jnp.int32, kb.shape, 1)
    kb_sw = pltpu.roll(kb, HEAD_DIM, 1)
    vb_sw = pltpu.roll(vb, HEAD_DIM, 1)
    kx = [jnp.where(lane < HEAD_DIM, kb, kb_sw), jnp.where(lane >= HEAD_DIM, kb, kb_sw)]
    vx = [jnp.where(lane < HEAD_DIM, vb, vb_sw), jnp.where(lane >= HEAD_DIM, vb, vb_sw)]
    row = lax.broadcasted_iota(jnp.int32, (2 * BLOCK, 2 * BLOCK), 0)
    col = lax.broadcasted_iota(jnp.int32, (2 * BLOCK, 2 * BLOCK), 1)
    qi = jnp.bitwise_and(row, BLOCK - 1)
    mask = (col > qi) & (col <= qi + BLOCK) & ((n > 0) | (col >= BLOCK))
    return kx, vx, mask


def _two_heads(x, mlo):
    zero = jnp.zeros_like(x)
    return jnp.concatenate([jnp.where(mlo, x, zero), jnp.where(mlo, zero, x)], axis=0)


def _attn_probs(q2, kx_h, mask, sink_col):
    s = _dot_nt(q2, kx_h) * (HEAD_DIM ** -0.5)
    s = jnp.where(mask, s, MASK_VALUE)
    m = jnp.maximum(jnp.max(s, axis=-1, keepdims=True), sink_col)
    p = jnp.exp(s - m)
    e_sink = jnp.exp(sink_col - m)
    inv = 1.0 / (jnp.sum(p, axis=-1, keepdims=True) + e_sink)
    return p * inv, e_sink * inv


def attn_fwd(z, sinks_b, g_a):
    s = z.shape[0]
    nb = s // BLOCK

    def body(q_ref, kvc_ref, kvp_ref, sk_ref, g_ref, o_ref, y_ref):
        n = pl.program_id(0)
        q = q_ref[...]
        kx, vx, mask = _attn_common(n, q, kvc_ref[...], kvp_ref[...])
        mlo = lax.broadcasted_iota(jnp.int32, (BLOCK, LANES), 1) < HEAD_DIM
        rowc = lax.broadcasted_iota(jnp.int32, (2 * BLOCK, 1), 0)
        outs = []
        for j in range(N_HEADS // 2):
            q2 = _two_heads(q[:, j * LANES:(j + 1) * LANES], mlo)
            sink_col = jnp.where(rowc < BLOCK, sk_ref[2 * j:2 * j + 1, 0:1], sk_ref[2 * j + 1:2 * j + 2, 0:1])
            pr, _ = _attn_probs(q2, kx[j // 2], mask, sink_col)
            o2 = _dot(pr.astype(bf16), vx[j // 2])
            outs.append(jnp.where(mlo, o2[:BLOCK], o2[BLOCK:]))
        o = jnp.concatenate(outs, axis=1)
        o_ref[...] = o.astype(bf16)
        y, _ = _rms_fwd(o, g_ref[...])
        y_ref[...] = y.astype(bf16)

    return pl.pallas_call(
        body, name="attn_fwd", grid=(nb,),
        in_specs=[pl.BlockSpec((BLOCK, ATTN_W), lambda n: (n, 0)),
                  pl.BlockSpec((BLOCK, 2 * KV_W), lambda n: (n, 2)),
                  pl.BlockSpec((BLOCK, 2 * KV_W), lambda n: (jnp.maximum(n - 1, 0), 2)),
                  _full((N_HEADS, LANES)), _full((1, ATTN_W))],
        out_specs=[pl.BlockSpec((BLOCK, ATTN_W), lambda n: (n, 0)), pl.BlockSpec((BLOCK, ATTN_W), lambda n: (n, 0))],
        out_shape=[SDS((s, ATTN_W), bf16), SDS((s, ATTN_W), bf16)],
        compiler_params=_params(("parallel",)),
    )(z, z, z, sinks_b, g_a)


def _lru_gates(xc, wa, ba, wx, bx, lam):
    xcb = xc.astype(bf16)
    r = _sigmoid(_dot(xcb, wa) + ba)
    ig = _sigmoid(_dot(xcb, wx) + bx)
    sp = _softplus(-lam)
    la = (-LRU_C * r) * sp
    a = jnp.exp(la)
    mult = jnp.sqrt(_neg_expm1(2.0 * la))
    return r, ig, sp, la, a, mult


def branch_fwd(z, p):
    s = z.shape[0]
    tm = _tile(s)
    hb = tm // HALO

    def body(cv_ref, cg_ref, rx_ref, rg_ref, cvh_ref, cgh_ref, rxh_ref,
             cw_ref, cb_ref, lng_ref, lnb_ref, lw_ref, lb_ref, wa_ref, ba_ref, wx_ref, bx_ref, lam_ref, gc_ref, gl_ref,
             conv_ref, hst_ref, nc_ref, nl_ref, carry_ref):
        i = pl.program_id(0)
        first = i == 0

        @pl.when(first)
        def _():
            carry_ref[...] = jnp.zeros_like(carry_ref)

        cval = cv_ref[...].astype(f32)
        u = cval * _sigmoid(cg_ref[...].astype(f32))
        hu = jnp.where(first, 0.0, cvh_ref[...].astype(f32) * _sigmoid(cgh_ref[...].astype(f32)))
        conv = _conv_taps(jnp.concatenate([hu, u], axis=0), cw_ref[...], CONV_K) + cb_ref[...]
        conv_ref[...] = conv
        mu = jnp.mean(conv, axis=-1, keepdims=True)
        xm = conv - mu
        rstd = lax.rsqrt(jnp.mean(xm * xm, axis=-1, keepdims=True) + LN_EPS)
        ln = xm * rstd * lng_ref[...] + lnb_ref[...]
        yc = ln * _sigmoid(ln)
        nc, _ = _rms_fwd(yc, gc_ref[...])
        nc_ref[...] = nc.astype(bf16)

        rx = rx_ref[...].astype(f32)
        hrx = jnp.where(first, 0.0, rxh_ref[...].astype(f32))
        xc = _conv_taps(jnp.concatenate([hrx, rx], axis=0), lw_ref[...], LRU_K) + lb_ref[...]
        r, ig, sp, la, a, mult = _lru_gates(xc, wa_ref[...], ba_ref[...], wx_ref[...], bx_ref[...], lam_ref[...])
        gx = mult * (ig * xc)
        a_cum, h_loc = _scan_fwd(a, gx)
        hs = a_cum * carry_ref[0:1, :] + h_loc
        carry_ref[...] = jnp.broadcast_to(hs[tm - 1:tm, :], carry_ref.shape)
        hst_ref[...] = hs
        gl, _ = _gelu(rg_ref[...].astype(f32))
        nl, _ = _rms_fwd(hs * gl, gl_ref[...])
        nl_ref[...] = nl.astype(bf16)

    def col(c):
        return pl.BlockSpec((tm, CONV_W), lambda i: (i, c))

    def halo(c):
        return pl.BlockSpec((HALO, CONV_W), lambda i: (jnp.maximum(i * hb - 1, 0), c))

    small = [p["cw"], p["cb"], p["lng"], p["lnb"], p["lw"], p["lb"], p["wa"], p["ba"], p["wx"], p["bx"], p["lam"],
             p["gc"], p["gl"]]
    row = pl.BlockSpec((tm, CONV_W), lambda i: (i, 0))
    return pl.pallas_call(
        body, name="branch_fwd", grid=(s // tm,),
        in_specs=[col(3), col(4), col(5), col(6), halo(3), halo(4), halo(5)] + [_full(a.shape) for a in small],
        out_specs=[row, row, row, row],
        out_shape=[SDS((s, CONV_W), f32), SDS((s, LRU_W), f32), SDS((s, CONV_W), bf16), SDS((s, LRU_W), bf16)],
        scratch_shapes=[pltpu.VMEM((8, LRU_W), f32)],
        compiler_params=_params(("arbitrary",)),
    )(z, z, z, z, z, z, z, *small)


def outproj_fwd(ya, yc, yl, h, w_out, g2):
    s = h.shape[0]
    tm = _tile(s)

    def body(ya_ref, yc_ref, yl_ref, h_ref, w_ref, g_ref, h1_ref, hn_ref):
        y = jnp.concatenate([ya_ref[...], yc_ref[...], yl_ref[...]], axis=1)
        h1 = h_ref[...] + _dot(y, w_ref[...])
        h1_ref[...] = h1
        hn, _ = _rms_fwd(h1, g_ref[...])
        hn_ref[...] = hn.astype(bf16)

    return pl.pallas_call(
        body, name="outproj_fwd", grid=(s // tm,),
        in_specs=[pl.BlockSpec((tm, ATTN_W), lambda i: (i, 0)), pl.BlockSpec((tm, CONV_W), lambda i: (i, 0)),
                  pl.BlockSpec((tm, LRU_W), lambda i: (i, 0)), pl.BlockSpec((tm, D_MODEL), lambda i: (i, 0)),
                  _full((D_MODEL, D_MODEL)), _full((1, D_MODEL))],
        out_specs=[pl.BlockSpec((tm, D_MODEL), lambda i: (i, 0)), pl.BlockSpec((tm, D_MODEL), lambda i: (i, 0))],
        out_shape=[SDS((s, D_MODEL), f32), SDS((s, D_MODEL), bf16)],
        compiler_params=_params(("parallel",), VMEM_LIMIT),
    )(ya, yc, yl, h, w_out, g2)


def mlp_fwd(hn2, h1, w_up, w_dn):
    s = h1.shape[0]
    tm = min(256, s)

    def body(x_ref, h_ref, wu_ref, wd_ref, up_ref, h2_ref):
        x = x_ref[...]
        acc = h_ref[...]
        for c in range(N_SHARD):
            u = _dot(x, wu_ref[c])
            up_ref[:, c * FF_CHUNK:(c + 1) * FF_CHUNK] = u.astype(bf16)
            act = jnp.square(jnp.maximum(u, 0.0)).astype(bf16)
            acc = acc + _dot(act, wd_ref[c])
        h2_ref[...] = acc

    return pl.pallas_call(
        body, name="mlp_fwd", grid=(s // tm,),
        in_specs=[pl.BlockSpec((tm, D_MODEL), lambda i: (i, 0)), pl.BlockSpec((tm, D_MODEL), lambda i: (i, 0)),
                  _full((N_SHARD, D_MODEL, FF_CHUNK)), _full((N_SHARD, FF_CHUNK, D_MODEL))],
        out_specs=[pl.BlockSpec((tm, D_FF), lambda i: (i, 0)), pl.BlockSpec((tm, D_MODEL), lambda i: (i, 0))],
        out_shape=[SDS((s, D_FF), bf16), SDS((s, D_MODEL), f32)],
        compiler_params=_params(("parallel",), VMEM_LIMIT),
    )(hn2, h1, w_up, w_dn)


def final_loss(h, tgt, gf):
    s = h.shape[0]
    tm = _tile(s)

    def body(h_ref, t_ref, g_ref, dh_ref, loss_ref, dg_ref):
        i = pl.program_id(0)

        @pl.when(i == 0)
        def _():
            loss_ref[...] = jnp.zeros_like(loss_ref)
            dg_ref[...] = jnp.zeros_like(dg_ref)

        x = h_ref[...]
        g = g_ref[...]
        y, r = _rms_fwd(x, g)
        err = y - t_ref[...]
        part = 0.5 * jnp.sum(jnp.mean(err * err, axis=-1, keepdims=True), axis=0, keepdims=True)
        loss_ref[...] += jnp.broadcast_to(part, loss_ref.shape)
        dx, dg = _rms_bwd(err * (1.0 / D_MODEL), x, r, g)
        dh_ref[...] = dx
        dg_ref[...] += dg

    return pl.pallas_call(
        body, name="final_loss", grid=(s // tm,),
        in_specs=[pl.BlockSpec((tm, D_MODEL), lambda i: (i, 0)), pl.BlockSpec((tm, D_MODEL), lambda i: (i, 0)),
                  _full((1, D_MODEL))],
        out_specs=[pl.BlockSpec((tm, D_MODEL), lambda i: (i, 0)), _full((8, LANES)), _full((1, D_MODEL))],
        out_shape=[SDS((s, D_MODEL), f32), SDS((8, LANES), f32), SDS((1, D_MODEL), f32)],
        compiler_params=_params(("arbitrary",)),
    )(h, tgt, gf)


def mlp_bwd_act(dh, up, h1, g2, w_up, w_dn):
    s = dh.shape[0]
    tm = min(256, s)

    def body(dh_ref, up_ref, h1_ref, g_ref, wu_ref, wd_ref, dup_ref, dh1_ref, dg_ref):
        i = pl.program_id(0)

        @pl.when(i == 0)
        def _():
            dg_ref[...] = jnp.zeros_like(dg_ref)

        dh = dh_ref[...]
        dhb = dh.astype(bf16)
        d_hn = jnp.zeros((tm, D_MODEL), f32)
        for c in range(N_SHARD):
            d_act = _dot_nt(dhb, wd_ref[c])
            u = up_ref[:, c * FF_CHUNK:(c + 1) * FF_CHUNK].astype(f32)
            d_u = (d_act * (2.0 * jnp.maximum(u, 0.0))).astype(bf16)
            dup_ref[:, c * FF_CHUNK:(c + 1) * FF_CHUNK] = d_u
            d_hn = d_hn + _dot_nt(d_u, wu_ref[c])
        x = h1_ref[...]
        g = g_ref[...]
        _, r = _rms_fwd(x, g)
        dx, dg = _rms_bwd(d_hn, x, r, g)
        dh1_ref[...] = dh + dx
        dg_ref[...] += dg

    return pl.pallas_call(
        body, name="mlp_bwd_act", grid=(s // tm,),
        in_specs=[pl.BlockSpec((tm, D_MODEL), lambda i: (i, 0)), pl.BlockSpec((tm, D_FF), lambda i: (i, 0)),
                  pl.BlockSpec((tm, D_MODEL), lambda i: (i, 0)), _full((1, D_MODEL)),
                  _full((N_SHARD, D_MODEL, FF_CHUNK)), _full((N_SHARD, FF_CHUNK, D_MODEL))],
        out_specs=[pl.BlockSpec((tm, D_FF), lambda i: (i, 0)), pl.BlockSpec((tm, D_MODEL), lambda i: (i, 0)),
                   _full((1, D_MODEL))],
        out_shape=[SDS((s, D_FF), bf16), SDS((s, D_MODEL), f32), SDS((1, D_MODEL), f32)],
        compiler_params=_params(("arbitrary",), VMEM_LIMIT),
    )(dh, up, h1, g2, w_up, w_dn)


def mlp_bwd_w(hn2, d_up, up, dh):
    s = dh.shape[0]
    tk = _tile(s)

    def body(x_ref, du_ref, up_ref, dh_ref, dwu_ref, dwd_ref):
        k = pl.program_id(1)

        @pl.when(k == 0)
        def _():
            dwu_ref[...] = jnp.zeros_like(dwu_ref)
            dwd_ref[...] = jnp.zeros_like(dwd_ref)

        dwu_ref[0] += _dot_tn(x_ref[...], du_ref[...])
        act = jnp.square(jnp.maximum(up_ref[...].astype(f32), 0.0)).astype(bf16)
        dwd_ref[0] += _dot_tn(act, dh_ref[...].astype(bf16))

    return pl.pallas_call(
        body, name="mlp_bwd_w", grid=(N_SHARD, s // tk),
        in_specs=[pl.BlockSpec((tk, D_MODEL), lambda c, k: (k, 0)), pl.BlockSpec((tk, FF_CHUNK), lambda c, k: (k, c)),
                  pl.BlockSpec((tk, FF_CHUNK), lambda c, k: (k, c)), pl.BlockSpec((tk, D_MODEL), lambda c, k: (k, 0))],
        out_specs=[pl.BlockSpec((1, D_MODEL, FF_CHUNK), lambda c, k: (c, 0, 0)),
                   pl.BlockSpec((1, FF_CHUNK, D_MODEL), lambda c, k: (c, 0, 0))],
        out_shape=[SDS((N_SHARD, D_MODEL, FF_CHUNK), f32), SDS((N_SHARD, FF_CHUNK, D_MODEL), f32)],
        compiler_params=_params(("parallel", "arbitrary"), VMEM_LIMIT),
    )(hn2, d_up, up, dh)


def outproj_bwd(dh1, ya, yc, yl, w_out):
    s = dh1.shape[0]
    tm = _tile(s)

    def body(dh_ref, ya_ref, yc_ref, yl_ref, w_ref, dy_ref, dw_ref):
        i = pl.program_id(0)

        @pl.when(i == 0)
        def _():
            dw_ref[...] = jnp.zeros_like(dw_ref)

        dhb = dh_ref[...].astype(bf16)
        dy_ref[...] = _dot_nt(dhb, w_ref[...])
        y = jnp.concatenate([ya_ref[...], yc_ref[...], yl_ref[...]], axis=1)
        dw_ref[...] += _dot_tn(y, dhb)

    return pl.pallas_call(
        body, name="outproj_bwd", grid=(s // tm,),
        in_specs=[pl.BlockSpec((tm, D_MODEL), lambda i: (i, 0)), pl.BlockSpec((tm, ATTN_W), lambda i: (i, 0)),
                  pl.BlockSpec((tm, CONV_W), lambda i: (i, 0)), pl.BlockSpec((tm, LRU_W), lambda i: (i, 0)),
                  _full((D_MODEL, D_MODEL))],
        out_specs=[pl.BlockSpec((tm, D_MODEL), lambda i: (i, 0)), _full((D_MODEL, D_MODEL))],
        out_shape=[SDS((s, D_MODEL), f32), SDS((D_MODEL, D_MODEL), f32)],
        compiler_params=_params(("arbitrary",), VMEM_LIMIT),
    )(dh1, ya, yc, yl, w_out)


def attn_bwd(z, o, dy, sinks_b, g_a):
    s = z.shape[0]
    nb = s // BLOCK

    def body(q_ref, kvc_ref, kvp_ref, o_ref, dy_ref, sk_ref, g_ref, dq_ref, dkv_ref, dsk_ref, dg_ref, pend_ref):
        n = pl.program_id(0)

        @pl.when(n == 0)
        def _():
            pend_ref[...] = jnp.zeros_like(pend_ref)
            dsk_ref[...] = jnp.zeros_like(dsk_ref)
            dg_ref[...] = jnp.zeros_like(dg_ref)

        @pl.when(n < nb)
        def _():
            q = q_ref[...]
            kx, vx, mask = _attn_common(n, q, kvc_ref[...], kvp_ref[...])
            mlo = lax.broadcasted_iota(jnp.int32, (BLOCK, LANES), 1) < HEAD_DIM
            rowc = lax.broadcasted_iota(jnp.int32, (2 * BLOCK, 1), 0)
            o_f = o_ref[...].astype(f32)
            g = g_ref[...]
            _, r = _rms_fwd(o_f, g)
            d_o, dg = _rms_bwd(dy_ref[...], o_f, r, g)
            dg_ref[...] += dg
            tk = [jnp.zeros((2 * BLOCK, LANES), f32) for _ in range(2)]
            tv = [jnp.zeros((2 * BLOCK, LANES), f32) for _ in range(2)]
            dqs = []
            scale = HEAD_DIM ** -0.5
            for j in range(N_HEADS // 2):
                hk = j // 2
                q2 = _two_heads(q[:, j * LANES:(j + 1) * LANES], mlo)
                do2 = _two_heads(d_o[:, j * LANES:(j + 1) * LANES], mlo).astype(bf16)
                sink_col = jnp.where(rowc < BLOCK, sk_ref[2 * j:2 * j + 1, 0:1], sk_ref[2 * j + 1:2 * j + 2, 0:1])
                pr, p_sink = _attn_probs(q2, kx[hk], mask, sink_col)
                d_p = _dot_nt(do2, vx[hk])
                d_row = jnp.sum(pr * d_p, axis=-1, keepdims=True)
                d_s = (pr * (d_p - d_row)).astype(bf16)
                d_sink = -(p_sink * d_row)
                dsk_ref[2 * j:2 * j + 1, :] += jnp.broadcast_to(jnp.sum(d_sink[:BLOCK], axis=0, keepdims=True), (1, LANES))
                dsk_ref[2 * j + 1:2 * j + 2, :] += jnp.broadcast_to(jnp.sum(d_sink[BLOCK:], axis=0, keepdims=True), (1, LANES))
                dq2 = _dot(d_s, kx[hk]) * scale
                dqs.append(jnp.where(mlo, dq2[:BLOCK], dq2[BLOCK:]))
                tk[hk] = tk[hk] + _dot_tn(d_s, q2) * scale
                tv[hk] = tv[hk] + _dot_tn(pr.astype(bf16), do2)
            lane = lax.broadcasted_iota(jnp.int32, (2 * BLOCK, LANES), 1)
            fk = [t + pltpu.roll(t, HEAD_DIM, 1) for t in tk]
            fv = [t + pltpu.roll(t, HEAD_DIM, 1) for t in tv]
            band = jnp.concatenate([jnp.where(lane < HEAD_DIM, fk[0], fk[1]), jnp.where(lane < HEAD_DIM, fv[0], fv[1])], axis=1)
            dkv_ref[...] = (pend_ref[...] + band[:BLOCK]).astype(bf16)
            pend_ref[...] = band[BLOCK:]
            dq_ref[...] = jnp.concatenate(dqs, axis=1).astype(bf16)

        @pl.when(n == nb)
        def _():
            dkv_ref[...] = pend_ref[...].astype(bf16)

    def cur(n):
        return jnp.minimum(n, nb - 1)

    return pl.pallas_call(
        body, name="attn_bwd", grid=(nb + 1,),
        in_specs=[pl.BlockSpec((BLOCK, ATTN_W), lambda n: (cur(n), 0)),
                  pl.BlockSpec((BLOCK, 2 * KV_W), lambda n: (cur(n), 2)),
                  pl.BlockSpec((BLOCK, 2 * KV_W), lambda n: (jnp.maximum(cur(n) - 1, 0), 2)),
                  pl.BlockSpec((BLOCK, ATTN_W), lambda n: (cur(n), 0)),
                  pl.BlockSpec((BLOCK, ATTN_W), lambda n: (cur(n), 0)),
                  _full((N_HEADS, LANES)), _full((1, ATTN_W))],
        out_specs=[pl.BlockSpec((BLOCK, ATTN_W), lambda n: (cur(n), 0)),
                   pl.BlockSpec((BLOCK, 2 * KV_W), lambda n: (jnp.maximum(n - 1, 0), 0)),
                   _full((N_HEADS, LANES)), _full((1, ATTN_W))],
        out_shape=[SDS((s, ATTN_W), bf16), SDS((s, 2 * KV_W), bf16), SDS((N_HEADS, LANES), f32), SDS((1, ATTN_W), f32)],
        scratch_shapes=[pltpu.VMEM((BLOCK, 2 * KV_W), f32)],
        compiler_params=_params(("arbitrary",)),
    )(z, z, z, o, dy, sinks_b, g_a)


_V_GC, _V_LNG, _V_LNB, _V_CB, _V_GL, _V_BA, _V_BX, _V_LAM, _V_LB = range(9)
_V_ROWS = 16


def branch_bwd_a(z, conv, hst, dy, p):
    s = z.shape[0]
    tm = _tile(s)
    nt = s // tm
    hb = tm // HALO
    h8 = tm // 8

    def body(conv_ref, dyc_ref, dyl_ref, rx_ref, rxh_ref, rg_ref, hst_ref, hsth_ref,
             lng_ref, lnb_ref, lw_ref, lb_ref, wa_ref, ba_ref, wx_ref, bx_ref, lam_ref, gc_ref, gl_ref,
             dconv_ref, dxc_ref, drg_ref, vec_ref, dwa_ref, dwx_ref, carry_ref):
        i = pl.program_id(0)
        ti = nt - 1 - i

        @pl.when(i == 0)
        def _():
            carry_ref[...] = jnp.zeros_like(carry_ref)
            vec_ref[...] = jnp.zeros_like(vec_ref)
            dwa_ref[...] = jnp.zeros_like(dwa_ref)
            dwx_ref[...] = jnp.zeros_like(dwx_ref)

        conv = conv_ref[...]
        mu = jnp.mean(conv, axis=-1, keepdims=True)
        xm = conv - mu
        rstd = lax.rsqrt(jnp.mean(xm * xm, axis=-1, keepdims=True) + LN_EPS)
        xhat = xm * rstd
        lng = lng_ref[...]
        ln = xhat * lng + lnb_ref[...]
        sg = _sigmoid(ln)
        yc = ln * sg
        gc = gc_ref[...]
        _, rc = _rms_fwd(yc, gc)
        d_yc, d_gc = _rms_bwd(dyc_ref[...], yc, rc, gc)
        d_ln = d_yc * (sg * (1.0 + ln * (1.0 - sg)))
        d_xhat = d_ln * lng
        d_conv = rstd * (d_xhat - jnp.mean(d_xhat, axis=-1, keepdims=True)
                         - xhat * jnp.mean(d_xhat * xhat, axis=-1, keepdims=True))
        dconv_ref[...] = d_conv

        rx = rx_ref[...].astype(f32)
        hrx = jnp.where(ti == 0, 0.0, rxh_ref[...].astype(f32))
        xc = _conv_taps(jnp.concatenate([hrx, rx], axis=0), lw_ref[...], LRU_K) + lb_ref[...]
        wa = wa_ref[...]
        wx = wx_ref[...]
        lam = lam_ref[...]
        r, ig, sp, la, a, mult = _lru_gates(xc, wa, ba_ref[...], wx, bx_ref[...], lam)
        hs = hst_ref[...]
        row = lax.broadcasted_iota(jnp.int32, hs.shape, 0)
        h_before = jnp.where(ti == 0, 0.0, hsth_ref[7:8, :])
        h_prev = jnp.where(row == 0, h_before, pltpu.roll(hs, 1, 0))
        rg = rg_ref[...].astype(f32)
        gl, tg = _gelu(rg)
        out = hs * gl
        gmix = gl_ref[...]
        _, rl = _rms_fwd(out, gmix)
        d_out, d_gl = _rms_bwd(dyl_ref[...], out, rl, gmix)
        drg_ref[...] = (d_out * hs * _gelu_grad(rg, tg)).astype(bf16)
        d_h = d_out * gl
        last = row == tm - 1
        a_next = jnp.where(last, 0.0, pltpu.roll(a, tm - 1, 0))
        lmb = _scan_bwd(a_next, d_h + jnp.where(last, carry_ref[0:1, :], 0.0))
        carry_ref[...] = jnp.broadcast_to(a[0:1, :] * lmb[0:1, :], carry_ref.shape)
        d_a = lmb * h_prev
        d_mult = lmb * (ig * xc)
        d_ig = lmb * (mult * xc)
        d_la = d_a * a - d_mult * (a * a) / jnp.maximum(mult, 1e-30)
        d_pa = (d_la * (-LRU_C * sp)) * (r * (1.0 - r))
        d_px = d_ig * (ig * (1.0 - ig))
        d_pab = d_pa.astype(bf16)
        d_pxb = d_px.astype(bf16)
        d_xc = lmb * (mult * ig) + _dot_nt(d_pab, wa) + _dot_nt(d_pxb, wx)
        dxc_ref[...] = d_xc
        xcb = xc.astype(bf16)
        dwa_ref[...] += _dot_tn(xcb, d_pab)
        dwx_ref[...] += _dot_tn(xcb, d_pxb)
        d_lam = jnp.sum(d_la * (-LRU_C * r), axis=0, keepdims=True) * (-_sigmoid(-lam))

        def colsum(v):
            return jnp.sum(v, axis=0, keepdims=True)

        rows = [None] * _V_ROWS
        rows[_V_GC] = d_gc
        rows[_V_LNG] = colsum(d_ln * xhat)
        rows[_V_LNB] = colsum(d_ln)
        rows[_V_CB] = colsum(d_conv)
        rows[_V_GL] = d_gl
        rows[_V_BA] = colsum(d_pa)
        rows[_V_BX] = colsum(d_px)
        rows[_V_LAM] = d_lam
        rows[_V_LB] = colsum(d_xc)
        zero = jnp.zeros((1, CONV_W), f32)
        vec_ref[...] += jnp.concatenate([zero if v is None else v for v in rows], axis=0)

    def rev(c):
        return pl.BlockSpec((tm, CONV_W), lambda i: (nt - 1 - i, c))

    small = [p["lng"], p["lnb"], p["lw"], p["lb"], p["wa"], p["ba"], p["wx"], p["bx"], p["lam"], p["gc"], p["gl"]]
    return pl.pallas_call(
        body, name="branch_bwd_a", grid=(nt,),
        in_specs=[rev(0), rev(2), rev(3), rev(5),
                  pl.BlockSpec((HALO, CONV_W), lambda i: (jnp.maximum((nt - 1 - i) * hb - 1, 0), 5)),
                  rev(6), rev(0),
                  pl.BlockSpec((8, LRU_W), lambda i: (jnp.maximum((nt - 1 - i) * h8 - 1, 0), 0))]
                 + [_full(a.shape) for a in small],
        out_specs=[rev(0), rev(0), rev(0), _full((_V_ROWS, CONV_W)), _full((LRU_W, LRU_W)), _full((LRU_W, LRU_W))],
        out_shape=[SDS((s, CONV_W), f32), SDS((s, LRU_W), f32), SDS((s, LRU_W), bf16), SDS((_V_ROWS, CONV_W), f32),
                   SDS((LRU_W, LRU_W), f32), SDS((LRU_W, LRU_W), f32)],
        scratch_shapes=[pltpu.VMEM((8, LRU_W), f32)],
        compiler_params=_params(("arbitrary",)),
    )(conv, dy, dy, z, z, z, hst, hst, *small)


def branch_bwd_b(z, d_conv, d_xc, p):
    s = z.shape[0]
    tm = _tile(s)
    nt = s // tm
    hb = tm // HALO

    def body(cv_ref, cg_ref, cvh_ref, cgh_ref, rx_ref, rxh_ref, dc_ref, dch_ref, dx_ref, dxh_ref, cw_ref, lw_ref,
             dzc_ref, dzr_ref, dcw_ref, dlw_ref):
        i = pl.program_id(0)

        @pl.when(i == 0)
        def _():
            dcw_ref[...] = jnp.zeros_like(dcw_ref)
            dlw_ref[...] = jnp.zeros_like(dlw_ref)

        first = i == 0
        last = i == nt - 1
        cval = cv_ref[...].astype(f32)
        sg = _sigmoid(cg_ref[...].astype(f32))
        u = cval * sg
        hu = jnp.where(first, 0.0, cvh_ref[...].astype(f32) * _sigmoid(cgh_ref[...].astype(f32)))
        dpad = jnp.concatenate([dc_ref[...], jnp.where(last, 0.0, dch_ref[...])], axis=0)
        d_u, dw_rows = _conv_taps_bwd(dpad, jnp.concatenate([hu, u], axis=0), cw_ref[...], CONV_K, tm)
        dcw_ref[...] += jnp.concatenate(dw_rows + [jnp.zeros((HALO - CONV_K, CONV_W), f32)], axis=0)
        dzc_ref[...] = jnp.concatenate([d_u * sg, d_u * cval * sg * (1.0 - sg)], axis=1).astype(bf16)

        rx = rx_ref[...].astype(f32)
        hrx = jnp.where(first, 0.0, rxh_ref[...].astype(f32))
        dxpad = jnp.concatenate([dx_ref[...], jnp.where(last, 0.0, dxh_ref[...])], axis=0)
        d_rx, dlw_rows = _conv_taps_bwd(dxpad, jnp.concatenate([hrx, rx], axis=0), lw_ref[...], LRU_K, tm)
        dlw_ref[...] += jnp.concatenate(dlw_rows + [jnp.zeros((8 - LRU_K, LRU_W), f32)], axis=0)
        dzr_ref[...] = d_rx.astype(bf16)

    def col(c):
        return pl.BlockSpec((tm, CONV_W), lambda i: (i, c))

    def prev(c):
        return pl.BlockSpec((HALO, CONV_W), lambda i: (jnp.maximum(i * hb - 1, 0), c))

    nxt = pl.BlockSpec((HALO, CONV_W), lambda i: (jnp.minimum((i + 1) * hb, nt * hb - 1), 0))
    return pl.pallas_call(
        body, name="branch_bwd_b", grid=(nt,),
        in_specs=[col(3), col(4), prev(3), prev(4), col(5), prev(5), col(0), nxt, col(0), nxt,
                  _full(p["cw"].shape), _full(p["lw"].shape)],
        out_specs=[pl.BlockSpec((tm, 2 * CONV_W), lambda i: (i, 0)), pl.BlockSpec((tm, LRU_W), lambda i: (i, 0)),
                   _full((HALO, CONV_W)), _full((8, LRU_W))],
        out_shape=[SDS((s, 2 * CONV_W), bf16), SDS((s, LRU_W), bf16), SDS((HALO, CONV_W), f32), SDS((8, LRU_W), f32)],
        compiler_params=_params(("arbitrary",)),
    )(z, z, z, z, z, z, d_conv, d_conv, d_xc, d_xc, p["cw"], p["lw"])


def inproj_bwd(dq, dkv, dzc, dzr, drg, h, g1, w_in, dh1):
    s = h.shape[0]
    tm = _tile(s)

    def body(dq_ref, dkv_ref, dzc_ref, dzr_ref, drg_ref, h_ref, g_ref, w_ref, dh1_ref, dh_ref, dw_ref, dg_ref):
        i = pl.program_id(0)

        @pl.when(i == 0)
        def _():
            dw_ref[...] = jnp.zeros_like(dw_ref)
            dg_ref[...] = jnp.zeros_like(dg_ref)

        dz = jnp.concatenate([dq_ref[...], dkv_ref[...], dzc_ref[...], dzr_ref[...], drg_ref[...]], axis=1)
        x = h_ref[...]
        g = g_ref[...]
        hn, r = _rms_fwd(x, g)
        d_hn = _dot_nt(dz, w_ref[...])
        dw_ref[...] += _dot_tn(hn.astype(bf16), dz)
        dx, dg = _rms_bwd(d_hn, x, r, g)
        dh_ref[...] = dh1_ref[...] + dx
        dg_ref[...] += dg

    def rowb(w):
        return pl.BlockSpec((tm, w), lambda i: (i, 0))

    return pl.pallas_call(
        body, name="inproj_bwd", grid=(s // tm,),
        in_specs=[rowb(ATTN_W), rowb(2 * KV_W), rowb(2 * CONV_W), rowb(LRU_W), rowb(LRU_W), rowb(D_MODEL),
                  _full((1, D_MODEL)), _full((D_MODEL, IN_W)), rowb(D_MODEL)],
        out_specs=[rowb(D_MODEL), _full((D_MODEL, IN_W)), _full((1, D_MODEL))],
        out_shape=[SDS((s, D_MODEL), f32), SDS((D_MODEL, IN_W), f32), SDS((1, D_MODEL), f32)],
        compiler_params=_params(("arbitrary",), VMEM_LIMIT),
    )(dq, dkv, dzc, dzr, drg, h, g1, w_in, dh1)


def _block_diag(w):
    out = jnp.zeros((LRU_W, LRU_W), w.dtype)
    hd = LRU_W // LRU_HEADS
    for hh in range(LRU_HEADS):
        out = out.at[hh * hd:(hh + 1) * hd, hh * hd:(hh + 1) * hd].set(w[hh])
    return out


def _diag_blocks(w):
    hd = LRU_W // LRU_HEADS
    return jnp.stack([w[hh * hd:(hh + 1) * hd, hh * hd:(hh + 1) * hd] for hh in range(LRU_HEADS)])


def _layer_params(sp, l):
    mix = sp["mix_norm"][l]
    return dict(
        g1=sp["norm1"][l][None, :], g2=sp["norm2"][l][None, :],
        sinks=jnp.broadcast_to(sp["attn_sinks"][l][:, None], (N_HEADS, LANES)),
        ga=mix[None, :ATTN_W], gc=mix[None, ATTN_W:ATTN_W + CONV_W], gl=mix[None, ATTN_W + CONV_W:],
        cw=jnp.pad(sp["conv_dw_w"][l], ((0, HALO - CONV_K), (0, 0))), cb=sp["conv_dw_b"][l][None, :],
        lng=sp["conv_ln_g"][l][None, :], lnb=sp["conv_ln_b"][l][None, :],
        lw=jnp.pad(sp["lru_conv_w"][l], ((0, 8 - LRU_K), (0, 0))), lb=sp["lru_conv_b"][l][None, :],
        wa=_block_diag(sp["lru_wa"][l]).astype(bf16), ba=sp["lru_ba"][l].reshape(1, LRU_W),
        wx=_block_diag(sp["lru_wx"][l]).astype(bf16), bx=sp["lru_bx"][l].reshape(1, LRU_W),
        lam=sp["lru_lambda"][l][None, :],
    )


def local_step(x, tgt, big, sp):
    lp = [_layer_params(sp, l) for l in range(DEPTH)]
    saved = []
    h = x
    for l in range(DEPTH):
        p, w = lp[l], big[l]
        z = inproj_fwd(h, p["g1"], w["w_in"])
        o, ya = attn_fwd(z, p["sinks"], p["ga"])
        conv, hst, yc, yl = branch_fwd(z, p)
        h1, hn2 = outproj_fwd(ya, yc, yl, h, w["w_out"], p["g2"])
        up, h2 = mlp_fwd(hn2, h1, w["w_up"], w["w_dn"])
        saved.append(dict(h=h, z=z, o=o, ya=ya, conv=conv, hst=hst, yc=yc, yl=yl, h1=h1, hn2=hn2, up=up))
        h = h2
    dh, loss, d_gf = final_loss(h, tgt, sp["final_norm"][None, :])
    big_g = [None] * DEPTH
    small_g = [None] * DEPTH
    for l in reversed(range(DEPTH)):
        p, w, sv = lp[l], big[l], saved[l]
        d_up, dh1, d_g2 = mlp_bwd_act(dh, sv["up"], sv["h1"], p["g2"], w["w_up"], w["w_dn"])
        dw_up, dw_dn = mlp_bwd_w(sv["hn2"], d_up, sv["up"], dh)
        dy, dw_out = outproj_bwd(dh1, sv["ya"], sv["yc"], sv["yl"], w["w_out"])
        dq, dkv, d_sk, d_ga = attn_bwd(sv["z"], sv["o"], dy, p["sinks"], p["ga"])
        d_conv, d_xc, d_rg, vec, dwa, dwx = branch_bwd_a(sv["z"], sv["conv"], sv["hst"], dy, p)
        dzc, dzr, dcw, dlw = branch_bwd_b(sv["z"], d_conv, d_xc, p)
        dh, dw_in, d_g1 = inproj_bwd(dq, dkv, dzc, dzr, d_rg, sv["h"], p["g1"], w["w_in"], dh1)
        big_g[l] = dict(w_in=dw_in, w_out=dw_out, w_up=dw_up, w_dn=dw_dn)
        hd = LRU_W // LRU_HEADS
        small_g[l] = dict(
            norm1=d_g1[0], attn_sinks=d_sk[:, 0], conv_dw_w=dcw[:CONV_K], conv_dw_b=vec[_V_CB],
            conv_ln_g=vec[_V_LNG], conv_ln_b=vec[_V_LNB], lru_conv_w=dlw[:LRU_K], lru_conv_b=vec[_V_LB],
            lru_wa=_diag_blocks(dwa), lru_ba=vec[_V_BA].reshape(LRU_HEADS, hd),
            lru_wx=_diag_blocks(dwx), lru_bx=vec[_V_BX].reshape(LRU_HEADS, hd), lru_lambda=vec[_V_LAM],
            mix_norm=jnp.concatenate([d_ga[0], vec[_V_GC], vec[_V_GL]]), norm2=d_g2[0],
        )
    return loss, dh, big_g, small_g, d_gf[0]


_HBM = pl.BlockSpec(memory_space=pl.ANY)


def _place():
    x, y, c = lax.axis_index("x"), lax.axis_index("y"), lax.axis_index("c")
    chips = [(1 - x, y), (x, 1 - y), (1 - x, 1 - y)]
    return x, y, c, chips


def gather_weights(shards):
    nw = len(shards)

    def body(*refs):
        ins, outs = refs[:nw], refs[nw:2 * nw]
        ici_send, ici_recv, d2d_send, d2d_recv, loc_sem = refs[2 * nw:]
        x, y, c, chips = _place()
        s_me = 2 * x + y
        sib = (x, y, 1 - c)

        local = []
        for w in range(nw):
            for l in range(DEPTH):
                cp = pltpu.make_async_copy(ins[w].at[l], outs[w].at[l, s_me], loc_sem.at[DEPTH * w + l])
                cp.start()
                local.append(cp)

        def ici(j, w):
            cx, cy = chips[j]
            return pltpu.make_async_remote_copy(
                src_ref=ins[w].at[c], dst_ref=outs[w].at[c, s_me], send_sem=ici_send.at[nw * j + w],
                recv_sem=ici_recv.at[nw * j + w], device_id=(cx, cy, c), device_id_type=MESH)

        def landed(j, w, layer):
            cx, cy = chips[j]
            return outs[w].at[layer, 2 * cx + cy]

        def d2d(j, w):
            return pltpu.make_async_remote_copy(
                src_ref=landed(j, w, c), dst_ref=landed(j, w, c), send_sem=d2d_send.at[nw * j + w],
                recv_sem=d2d_recv.at[nw * j + w], device_id=sib, device_id_type=MESH)

        first = [ici(j, w) for j in range(3) for w in range(nw)]
        for cp in first:
            cp.start()
        passed = []
        for j in range(3):
            for w in range(nw):
                pltpu.make_async_remote_copy(
                    src_ref=ins[w].at[c], dst_ref=landed(j, w, c), send_sem=ici_send.at[nw * j + w],
                    recv_sem=ici_recv.at[nw * j + w], device_id=sib, device_id_type=MESH).wait_recv()
                cp = d2d(j, w)
                cp.start()
                passed.append(cp)
        for j in range(3):
            for w in range(nw):
                pltpu.make_async_remote_copy(
                    src_ref=ins[w].at[c], dst_ref=landed(j, w, 1 - c), send_sem=d2d_send.at[nw * j + w],
                    recv_sem=d2d_recv.at[nw * j + w], device_id=sib, device_id_type=MESH).wait_recv()
        for cp in first + passed:
            cp.wait_send()
        for cp in local:
            cp.wait()

    out_shape = [SDS((DEPTH, N_SHARD) + a.shape[1:], a.dtype) for a in shards]
    return pl.pallas_call(
        body, name="gather_weights", in_specs=[_HBM] * nw, out_specs=[_HBM] * nw, out_shape=out_shape,
        scratch_shapes=[pltpu.SemaphoreType.DMA((3 * nw,)), pltpu.SemaphoreType.DMA((3 * nw,)),
                        pltpu.SemaphoreType.DMA((3 * nw,)), pltpu.SemaphoreType.DMA((3 * nw,)),
                        pltpu.SemaphoreType.DMA((DEPTH * nw,))],
    )(*shards)


def pair_exchange(parts):
    nw = len(parts[0])

    def body(*refs):
        ins = [refs[l * nw:(l + 1) * nw] for l in range(DEPTH)]
        outs = refs[DEPTH * nw:(DEPTH + 1) * nw]
        send_sem, recv_sem = refs[(DEPTH + 1) * nw:]
        x, y, c, _ = _place()
        copies = []
        for l in range(DEPTH):
            for w in range(nw):
                half = ins[l][w].shape[1] // 2
                cp = pltpu.make_async_remote_copy(
                    src_ref=ins[l][w].at[:, pl.ds(pl.multiple_of((1 - c) * half, 8), half), :], dst_ref=outs[w].at[l],
                    send_sem=send_sem.at[nw * l + w], recv_sem=recv_sem.at[nw * l + w],
                    device_id=(x, y, 1 - c), device_id_type=MESH)
                cp.start()
                copies.append(cp)
        for cp in copies:
            cp.wait()

    flat = [a for l in range(DEPTH) for a in parts[l]]
    out_shape = [SDS((DEPTH, N_SHARD, a.shape[1] // 2, a.shape[2]), f32) for a in parts[0]]
    return pl.pallas_call(
        body, name="pair_exchange", in_specs=[_HBM] * len(flat), out_specs=[_HBM] * nw, out_shape=out_shape,
        scratch_shapes=[pltpu.SemaphoreType.DMA((DEPTH * nw,)), pltpu.SemaphoreType.DMA((DEPTH * nw,))],
    )(*flat)


def chip_sum(g0, g1, recv, c_idx):
    _, r, cdim = g0.shape
    half = r // 2
    tr = min(half, 256)
    nh = half // tr

    def body(c_ref, g0_ref, g1_ref, r_ref, o32_ref, o16_ref):
        l = pl.program_id(0)
        mine = jnp.where(l == 0, g0_ref[...], g1_ref[...])
        tot = mine[0] + r_ref[0, 0]
        o32_ref[0, 0] = tot
        o16_ref[0, 0] = tot.astype(bf16)

    def layer_map(layer):
        def index(l, s, i, c_ref):
            on = l == layer
            return (jnp.where(on, s, 0), jnp.where(on, c_ref[0] * nh + i, 0), 0)
        return index

    blk4 = pl.BlockSpec((1, 1, tr, cdim), lambda l, s, i, c_ref: (l, s, i, 0))
    return pl.pallas_call(
        body, name="chip_sum",
        grid_spec=pltpu.PrefetchScalarGridSpec(
            num_scalar_prefetch=1, grid=(DEPTH, N_SHARD, nh),
            in_specs=[pl.BlockSpec((1, tr, cdim), layer_map(0)), pl.BlockSpec((1, tr, cdim), layer_map(1)), blk4],
            out_specs=[blk4, blk4]),
        out_shape=[SDS((DEPTH, N_SHARD, half, cdim), f32), SDS((DEPTH, N_SHARD, half, cdim), bf16)],
        compiler_params=_params(("arbitrary", "arbitrary", "arbitrary")),
    )(c_idx, g0, g1, recv)


def shard_exchange(sums16):
    nw = len(sums16)

    def body(*refs):
        ins, outs = refs[:nw], refs[nw:2 * nw]
        send_sem, recv_sem = refs[2 * nw:]
        x, y, c, chips = _place()
        copies = []
        for j, (cx, cy) in enumerate(chips):
            for l in range(DEPTH):
                for w in range(nw):
                    k = (j * DEPTH + l) * nw + w
                    cp = pltpu.make_async_remote_copy(
                        src_ref=ins[w].at[l, 2 * cx + cy], dst_ref=outs[w].at[l, j], send_sem=send_sem.at[k],
                        recv_sem=recv_sem.at[k], device_id=(cx, cy, c), device_id_type=MESH)
                    cp.start()
                    copies.append(cp)
        for cp in copies:
            cp.wait()

    out_shape = [SDS((DEPTH, 3) + a.shape[2:], bf16) for a in sums16]
    return pl.pallas_call(
        body, name="shard_exchange", in_specs=[_HBM] * nw, out_specs=[_HBM] * nw, out_shape=out_shape,
        scratch_shapes=[pltpu.SemaphoreType.DMA((3 * DEPTH * nw,)), pltpu.SemaphoreType.DMA((3 * DEPTH * nw,))],
    )(*sums16)


def shard_sum(sum32, recv16, idx):
    _, _, half, cdim = sum32.shape
    tr = min(half, 256)

    def body(idx_ref, a_ref, r0_ref, r1_ref, r2_ref, o_ref):
        o_ref[0] = ((a_ref[0, 0] + r0_ref[0, 0].astype(f32)) + r1_ref[0, 0].astype(f32)) + r2_ref[0, 0].astype(f32)

    def rel(j):
        return pl.BlockSpec((1, 1, tr, cdim), lambda l, i, idx_ref: (l, j, i, 0))

    return pl.pallas_call(
        body, name="shard_sum",
        grid_spec=pltpu.PrefetchScalarGridSpec(
            num_scalar_prefetch=1, grid=(DEPTH, half // tr),
            in_specs=[pl.BlockSpec((1, 1, tr, cdim), lambda l, i, idx_ref: (l, idx_ref[1], i, 0)), rel(0), rel(1), rel(2)],
            out_specs=pl.BlockSpec((1, tr, cdim), lambda l, i, idx_ref: (l, i, 0))),
        out_shape=SDS((DEPTH, half, cdim), f32),
        compiler_params=_params(("arbitrary", "arbitrary")),
    )(idx, sum32, recv16, recv16, recv16)


def halves_exchange(tots):
    nw = len(tots)

    def body(*refs):
        ins, outs = refs[:nw], refs[nw:2 * nw]
        send_sem, recv_sem, loc_sem = refs[2 * nw:]
        x, y, c, _ = _place()
        copies, local = [], []
        for w in range(nw):
            half = ins[w].shape[1]
            rows = pl.ds(pl.multiple_of(c * half, 8), half)
            lc = pltpu.make_async_copy(ins[w], outs[w].at[:, rows, :], loc_sem.at[w])
            lc.start()
            local.append(lc)
            cp = pltpu.make_async_remote_copy(
                src_ref=ins[w], dst_ref=outs[w].at[:, rows, :], send_sem=send_sem.at[w], recv_sem=recv_sem.at[w],
                device_id=(x, y, 1 - c), device_id_type=MESH)
            cp.start()
            copies.append(cp)
        for w in range(nw):
            half = ins[w].shape[1]
            other = pl.ds(pl.multiple_of((1 - c) * half, 8), half)
            pltpu.make_async_remote_copy(
                src_ref=ins[w], dst_ref=outs[w].at[:, other, :], send_sem=send_sem.at[w], recv_sem=recv_sem.at[w],
                device_id=(x, y, 1 - c), device_id_type=MESH).wait_recv()
        for cp in copies:
            cp.wait_send()
        for lc in local:
            lc.wait()

    out_shape = [SDS((DEPTH, 2 * a.shape[1], a.shape[2]), f32) for a in tots]
    return pl.pallas_call(
        body, name="halves_exchange", in_specs=[_HBM] * nw, out_specs=[_HBM] * nw, out_shape=out_shape,
        scratch_shapes=[pltpu.SemaphoreType.DMA((nw,)), pltpu.SemaphoreType.DMA((nw,)), pltpu.SemaphoreType.DMA((nw,))],
    )(*tots)


N_DEV = 8


def small_allreduce(vec):
    r = vec.shape[0]

    def body(v_ref, o_ref, all_ref, send_sems, recv_sems):
        x, y, c, chips = _place()
        me, sib = (x, y, c), (x, y, 1 - c)

        def rows(px, py, pc):
            return all_ref.at[4 * px + 2 * py + pc]

        def copy(k, block, to, src=None):
            return pltpu.make_async_remote_copy(
                src_ref=rows(*block) if src is None else src, dst_ref=rows(*block), send_sem=send_sems.at[k],
                recv_sem=recv_sems.at[k], device_id=to, device_id_type=MESH)

        first = [copy(0, me, sib, src=v_ref)]
        first += [copy(1 + j, me, (*chip, c), src=v_ref) for j, chip in enumerate(chips)]
        for cp in first:
            cp.start()
        rows(*me)[...] = v_ref[...]
        passed = [copy(4 + j, (*chip, c), sib) for j, chip in enumerate(chips)]
        for j, chip in enumerate(chips):
            copy(1 + j, (*chip, c), me).wait_recv()
            passed[j].start()
        copy(0, sib, me).wait_recv()
        for j, chip in enumerate(chips):
            copy(4 + j, (*chip, 1 - c), me).wait_recv()
        for cp in first + passed:
            cp.wait_send()
        acc = all_ref[0]
        for d in range(1, N_DEV):
            acc = acc + all_ref[d]
        o_ref[...] = acc

    return pl.pallas_call(
        body, name="small_allreduce",
        in_specs=[pl.BlockSpec(memory_space=pltpu.VMEM)], out_specs=pl.BlockSpec(memory_space=pltpu.VMEM),
        out_shape=SDS((r, LANES), f32),
        scratch_shapes=[pltpu.VMEM((N_DEV, r, LANES), f32), pltpu.SemaphoreType.DMA((7,)), pltpu.SemaphoreType.DMA((7,))],
    )(vec)


def _adamw_math(w, g, m, v):
    m = ADAM_B1 * m + (1.0 - ADAM_B1) * g
    v = ADAM_B2 * v + (1.0 - ADAM_B2) * (g * g)
    m_hat = m / (1.0 - ADAM_B1 ** ADAM_STEP)
    v_hat = v / (1.0 - ADAM_B2 ** ADAM_STEP)
    delta = -ADAM_LR * (m_hat / (jnp.sqrt(v_hat) + ADAM_EPS) + ADAM_WD * w)
    return delta, m, v


def adamw_big(w, g, m, v):
    _, r, cdim = w.shape
    tr = min(r, 256)

    def body(w_ref, g_ref, m_ref, v_ref, d_ref, mo_ref, vo_ref):
        d, mm, vv = _adamw_math(w_ref[...], g_ref[...], m_ref[...], v_ref[...])
        d_ref[...] = d
        mo_ref[...] = mm
        vo_ref[...] = vv

    blk = pl.BlockSpec((1, tr, cdim), lambda l, i: (l, i, 0))
    return pl.pallas_call(
        body, name="adamw_big", grid=(DEPTH, r // tr), in_specs=[blk] * 4, out_specs=[blk] * 3,
        out_shape=[SDS(w.shape, f32)] * 3, compiler_params=_params(("parallel", "parallel")),
    )(w, g, m, v)


def adamw_small(ws, gs, ms, vs):
    n = len(ws)

    def body(*refs):
        w_r, g_r, m_r, v_r = refs[:n], refs[n:2 * n], refs[2 * n:3 * n], refs[3 * n:4 * n]
        d_o, m_o, v_o = refs[4 * n:5 * n], refs[5 * n:6 * n], refs[6 * n:7 * n]
        for k in range(n):
            d, mm, vv = _adamw_math(w_r[k][...], g_r[k][...], m_r[k][...], v_r[k][...])
            d_o[k][...] = d
            m_o[k][...] = mm
            v_o[k][...] = vv

    vm = pl.BlockSpec(memory_space=pltpu.VMEM)
    shapes = [SDS(a.shape, f32) for a in ws]
    outs = pl.pallas_call(
        body, name="adamw_small", in_specs=[vm] * (4 * n), out_specs=[vm] * (3 * n), out_shape=shapes * 3,
    )(*ws, *gs, *ms, *vs)
    return outs[:n], outs[n:2 * n], outs[2 * n:]


_BIG = ("w_in", "w_out", "w_up", "w_down")
_WEIGHTS = ("norm1", "w_in", "attn_sinks", "conv_dw_w", "conv_dw_b", "conv_ln_g", "conv_ln_b", "lru_conv_w", "lru_conv_b",
            "lru_wa", "lru_ba", "lru_wx", "lru_bx", "lru_lambda", "mix_norm", "w_out", "norm2", "w_up", "w_down", "final_norm")
_SMALL = tuple(n for n in _WEIGHTS if n not in _BIG)
_SMALL_FULL_SHAPE = dict(
    norm1=(DEPTH, D_MODEL), attn_sinks=(DEPTH, N_HEADS), conv_dw_w=(DEPTH, CONV_K, CONV_W), conv_dw_b=(DEPTH, CONV_W),
    conv_ln_g=(DEPTH, CONV_W), conv_ln_b=(DEPTH, CONV_W), lru_conv_w=(DEPTH, LRU_K, LRU_W), lru_conv_b=(DEPTH, LRU_W),
    lru_wa=(DEPTH, LRU_HEADS, 64, 64), lru_ba=(DEPTH, LRU_HEADS, 64), lru_wx=(DEPTH, LRU_HEADS, 64, 64),
    lru_bx=(DEPTH, LRU_HEADS, 64), lru_lambda=(DEPTH, LRU_W), mix_norm=(DEPTH, D_MODEL), norm2=(DEPTH, D_MODEL),
    final_norm=(D_MODEL,))
_CHANNEL_SHARDED = ("conv_dw_w", "lru_conv_w")


def _pad_lanes(n):
    return -(-n // LANES) * LANES


def _pack(named):
    flat = []
    for a in named:
        a = a.reshape(-1)
        flat.append(jnp.pad(a, (0, _pad_lanes(a.shape[0]) - a.shape[0])))
    v = jnp.concatenate(flat)
    rows = -(-v.shape[0] // (8 * LANES)) * 8
    return jnp.pad(v, (0, rows * LANES - v.shape[0])).reshape(rows, LANES)


def _unpack(vec, shapes):
    flat = vec.reshape(-1)
    out, off = [], 0
    for shp in shapes:
        n = math.prod(shp)
        out.append(flat[off:off + n].reshape(shp))
        off += _pad_lanes(n)
    return out


def _as2d(a):
    return a.reshape(-1, a.shape[-1]) if a.ndim > 1 else a.reshape(1, -1)


def kernel(x, norm1, w_in, attn_sinks, conv_dw_w, conv_dw_b, conv_ln_g, conv_ln_b, lru_conv_w, lru_conv_b, lru_wa, lru_ba, lru_wx, lru_bx, lru_lambda, mix_norm, w_out, norm2, w_up, w_down, final_norm, loss_target, m_norm1, m_w_in, m_attn_sinks, m_conv_dw_w, m_conv_dw_b, m_conv_ln_g, m_conv_ln_b, m_lru_conv_w, m_lru_conv_b, m_lru_wa, m_lru_ba, m_lru_wx, m_lru_bx, m_lru_lambda, m_mix_norm, m_w_out, m_norm2, m_w_up, m_w_down, m_final_norm, v_norm1, v_w_in, v_attn_sinks, v_conv_dw_w, v_conv_dw_b, v_conv_ln_g, v_conv_ln_b, v_lru_conv_w, v_lru_conv_b, v_lru_wa, v_lru_ba, v_lru_wx, v_lru_bx, v_lru_lambda, v_mix_norm, v_w_out, v_norm2, v_w_up, v_w_down, v_final_norm):
    wts = dict(norm1=norm1, w_in=w_in, attn_sinks=attn_sinks, conv_dw_w=conv_dw_w, conv_dw_b=conv_dw_b, conv_ln_g=conv_ln_g,
               conv_ln_b=conv_ln_b, lru_conv_w=lru_conv_w, lru_conv_b=lru_conv_b, lru_wa=lru_wa, lru_ba=lru_ba, lru_wx=lru_wx,
               lru_bx=lru_bx, lru_lambda=lru_lambda, mix_norm=mix_norm, w_out=w_out, norm2=norm2, w_up=w_up, w_down=w_down,
               final_norm=final_norm)
    mom = dict(norm1=m_norm1, w_in=m_w_in, attn_sinks=m_attn_sinks, conv_dw_w=m_conv_dw_w, conv_dw_b=m_conv_dw_b,
               conv_ln_g=m_conv_ln_g, conv_ln_b=m_conv_ln_b, lru_conv_w=m_lru_conv_w, lru_conv_b=m_lru_conv_b, lru_wa=m_lru_wa,
               lru_ba=m_lru_ba, lru_wx=m_lru_wx, lru_bx=m_lru_bx, lru_lambda=m_lru_lambda, mix_norm=m_mix_norm, w_out=m_w_out,
               norm2=m_norm2, w_up=m_w_up, w_down=m_w_down, final_norm=m_final_norm)
    var = dict(norm1=v_norm1, w_in=v_w_in, attn_sinks=v_attn_sinks, conv_dw_w=v_conv_dw_w, conv_dw_b=v_conv_dw_b,
               conv_ln_g=v_conv_ln_g, conv_ln_b=v_conv_ln_b, lru_conv_w=v_lru_conv_w, lru_conv_b=v_lru_conv_b, lru_wa=v_lru_wa,
               lru_ba=v_lru_ba, lru_wx=v_lru_wx, lru_bx=v_lru_bx, lru_lambda=v_lru_lambda, mix_norm=v_mix_norm, w_out=v_w_out,
               norm2=v_norm2, w_up=v_w_up, w_down=v_w_down, final_norm=v_final_norm)

    c_idx = lax.axis_index("c").astype(jnp.int32)
    s_idx = (2 * lax.axis_index("x") + lax.axis_index("y")).astype(jnp.int32)
    idx = jnp.stack([c_idx, s_idx])

    g_in, g_out, g_up, g_dn, g_cw, g_lw = gather_weights(
        [w_in.astype(bf16), w_out.astype(bf16), w_up.astype(bf16), w_down.astype(bf16), conv_dw_w, lru_conv_w])
    big = [dict(w_in=g_in[l].transpose(1, 0, 2).reshape(D_MODEL, IN_W), w_out=g_out[l].reshape(D_MODEL, D_MODEL),
                w_up=g_up[l], w_dn=g_dn[l]) for l in range(DEPTH)]
    sp = {n: wts[n] for n in _SMALL}
    sp["conv_dw_w"] = g_cw.transpose(0, 2, 1, 3).reshape(DEPTH, CONV_K, CONV_W)
    sp["lru_conv_w"] = g_lw.transpose(0, 2, 1, 3).reshape(DEPTH, LRU_K, LRU_W)

    loss_blk, grad_x, big_g, small_g, d_gf = local_step(x[0], loss_target[0], big, sp)

    parts = [[big_g[l]["w_in"].reshape(D_MODEL, N_SHARD, IN_W // N_SHARD).transpose(1, 0, 2),
              big_g[l]["w_out"].reshape(N_SHARD, D_MODEL // N_SHARD, D_MODEL), big_g[l]["w_up"], big_g[l]["w_dn"]]
             for l in range(DEPTH)]
    recv = pair_exchange(parts)
    sums = [chip_sum(parts[0][w], parts[1][w], recv[w], idx) for w in range(4)]
    got = shard_exchange([s16 for _, s16 in sums])
    tots = [shard_sum(sums[w][0], got[w], idx) for w in range(4)]
    grads_big = dict(zip(_BIG, halves_exchange(tots)))

    stacked = [jnp.stack([small_g[l][n] for l in range(DEPTH)]) for n in _SMALL if n != "final_norm"]
    packed = _pack(stacked + [d_gf, loss_blk[0, 0:1]])
    summed = small_allreduce(packed)
    names = [n for n in _SMALL if n != "final_norm"] + ["final_norm"]
    unpacked = _unpack(summed, [_SMALL_FULL_SHAPE[n] for n in names] + [(1,)])
    loss = unpacked[-1][0]
    grads = dict(zip(names, unpacked[:-1]))
    for n in _CHANNEL_SHARDED:
        width = wts[n].shape[-1]
        grads[n] = lax.dynamic_slice_in_dim(grads[n], s_idx * width, width, axis=2)
    grads.update(grads_big)

    delta, new_m, new_v = {}, {}, {}
    for n in _BIG:
        delta[n], new_m[n], new_v[n] = adamw_big(wts[n], grads[n], mom[n], var[n])
    sm = list(_SMALL)
    d_s, m_s, v_s = adamw_small([_as2d(wts[n]) for n in sm], [_as2d(grads[n]) for n in sm],
                                [_as2d(mom[n]) for n in sm], [_as2d(var[n]) for n in sm])
    for k, n in enumerate(sm):
        delta[n], new_m[n], new_v[n] = (a.reshape(wts[n].shape) for a in (d_s[k], m_s[k], v_s[k]))

    return (loss, grad_x[None], *[grads[n] for n in _WEIGHTS], *[delta[n] for n in _WEIGHTS],
            *[new_m[n] for n in _WEIGHTS], *[new_v[n] for n in _WEIGHTS])
```

```python
import functools
import math

import jax
import jax.numpy as jnp
from jax import lax
from jax.experimental import pallas as pl
from jax.experimental.pallas import tpu as pltpu

f32 = jnp.float32
bf16 = jnp.bfloat16
SDS = jax.ShapeDtypeStruct

D_MODEL = 1024
DEPTH = 2
ATTN_W = 512
KV_W = 128
HEAD_DIM = 64
N_HEADS = 8
BLOCK = 128
CONV_W = 256
CONV_K = 31
LRU_W = 256
LRU_K = 4
LRU_HEADS = 4
LRU_C = 8.0
IN_W = 1792
D_FF = 4096
N_SHARD = 4
FF_CHUNK = D_FF // N_SHARD
RMS_EPS = 1e-6
LN_EPS = 1e-5
MASK_VALUE = -1e30
HALO = 32
LANES = 128
VMEM_LIMIT = 56 * 1024 * 1024

ADAM_LR = 0.001
ADAM_B1 = 0.9
ADAM_B2 = 0.999
ADAM_EPS = 1e-08
ADAM_WD = 0.01
ADAM_STEP = 10

MESH = pl.DeviceIdType.MESH


def _dot(a, b):
    return jnp.dot(a, b, preferred_element_type=f32)


def _dot_nt(a, b):
    return lax.dot_general(a, b, (((1,), (1,)), ((), ())), preferred_element_type=f32)


def _dot_tn(a, b):
    return lax.dot_general(a, b, (((0,), (0,)), ((), ())), preferred_element_type=f32)


def _rms_fwd(x, g):
    r = lax.rsqrt(jnp.mean(x * x, axis=-1, keepdims=True) + RMS_EPS)
    return x * r * g, r


def _rms_bwd(dy, x, r, g):
    t = dy * g
    dx = r * t - x * (r * r * r) * jnp.mean(t * x, axis=-1, keepdims=True)
    dg = jnp.sum(dy * x * r, axis=0, keepdims=True)
    return dx, dg


def _sigmoid(x):
    return jax.nn.sigmoid(x)


_GELU_K = math.sqrt(2.0 / math.pi)


def _gelu(x):
    t = jnp.tanh(_GELU_K * (x + 0.044715 * x * x * x))
    return 0.5 * x * (1.0 + t), t


def _gelu_grad(x, t):
    return 0.5 * (1.0 + t) + 0.5 * x * (1.0 - t * t) * _GELU_K * (1.0 + 3.0 * 0.044715 * x * x)


def _log1p(x):
    return jnp.where(x < 1e-4, x - 0.5 * x * x, jnp.log(1.0 + x))


def _softplus(x):
    return jnp.maximum(x, 0.0) + _log1p(jnp.exp(-jnp.abs(x)))


def _neg_expm1(x):
    series = -x * (1.0 + 0.5 * x * (1.0 + x * (1.0 / 3.0) * (1.0 + 0.25 * x)))
    return jnp.where(x > -0.01, series, 1.0 - jnp.exp(x))


def _conv_taps(xpad, w, k_width):
    acc = None
    for k in range(k_width):
        sh = (k_width - 1) - k
        xs = xpad if sh == 0 else pltpu.roll(xpad, sh, 0)
        term = xs[HALO:] * w[k:k + 1, :]
        acc = term if acc is None else acc + term
    return acc


def _conv_taps_bwd(dpad, upad, w, k_width, t_rows):
    n = t_rows + HALO
    d_in = None
    dw_rows = []
    d_out = dpad[:t_rows]
    for k in range(k_width):
        sh = (k_width - 1) - k
        ds = dpad if sh == 0 else pltpu.roll(dpad, n - sh, 0)
        term = ds[:t_rows] * w[k:k + 1, :]
        d_in = term if d_in is None else d_in + term
        us = upad if sh == 0 else pltpu.roll(upad, sh, 0)
        dw_rows.append(jnp.sum(d_out * us[HALO:], axis=0, keepdims=True))
    return d_in, dw_rows


def _scan_fwd(a, b):
    t_rows = a.shape[0]
    row = lax.broadcasted_iota(jnp.int32, a.shape, 0)
    d = 1
    while d < t_rows:
        a_sh = jnp.where(row < d, 1.0, pltpu.roll(a, d, 0))
        b_sh = jnp.where(row < d, 0.0, pltpu.roll(b, d, 0))
        b = a * b_sh + b
        a = a * a_sh
        d *= 2
    return a, b


def _scan_bwd(a, b):
    t_rows = a.shape[0]
    row = lax.broadcasted_iota(jnp.int32, a.shape, 0)
    d = 1
    while d < t_rows:
        a_sh = jnp.where(row >= t_rows - d, 1.0, pltpu.roll(a, t_rows - d, 0))
        b_sh = jnp.where(row >= t_rows - d, 0.0, pltpu.roll(b, t_rows - d, 0))
        b = b + a * b_sh
        a = a * a_sh
        d *= 2
    return b


def _full(shape):
    nd = len(shape)
    return pl.BlockSpec(shape, lambda *_: (0,) * nd)


def _params(sem, vmem=None):
    return pltpu.CompilerParams(dimension_semantics=sem, vmem_limit_bytes=vmem)


def _tile(s):
    return min(512, s)


def inproj_fwd(h, g1, w_in):
    s = h.shape[0]
    tm = _tile(s)

    def body(h_ref, g_ref, w_ref, z_ref):
        hn, _ = _rms_fwd(h_ref[...], g_ref[...])
        z_ref[...] = _dot(hn.astype(bf16), w_ref[...]).astype(bf16)

    return pl.pallas_call(
        body, name="inproj_fwd", grid=(s // tm,),
        in_specs=[pl.BlockSpec((tm, D_MODEL), lambda i: (i, 0)), _full((1, D_MODEL)), _full((D_MODEL, IN_W))],
        out_specs=pl.BlockSpec((tm, IN_W), lambda i: (i, 0)),
        out_shape=SDS((s, IN_W), bf16),
        compiler_params=_params(("parallel",), VMEM_LIMIT),
    )(h, g1, w_in)


ATT_ROWS = N_HEADS * BLOCK


def _attn_bias():
    qi = jnp.arange(BLOCK)[:, None]
    col = jnp.arange(2 * BLOCK)[None, :]
    band = (col > qi) & (col <= qi + BLOCK)
    first = band & (col >= BLOCK)
    return jnp.where(jnp.stack([first, band]), 0.0, MASK_VALUE).astype(f32)


def _attn_band(kvc, kvp):
    kb = jnp.concatenate([kvp[:, :KV_W], kvc[:, :KV_W]], axis=0)
    vb = jnp.concatenate([kvp[:, KV_W:], kvc[:, KV_W:]], axis=0)
    lane = lax.broadcasted_iota(jnp.int32, kb.shape, 1)
    kb_sw = pltpu.roll(kb, HEAD_DIM, 1)
    vb_sw = pltpu.roll(vb, HEAD_DIM, 1)
    kx = [jnp.where(lane < HEAD_DIM, kb, kb_sw), jnp.where(lane >= HEAD_DIM, kb, kb_sw)]
    vx = [jnp.where(lane < HEAD_DIM, vb, vb_sw), jnp.where(lane >= HEAD_DIM, vb, vb_sw)]
    return kx, vx


def _stack_heads(x, mlo):
    zero = jnp.zeros((BLOCK, LANES), x.dtype)
    out = []
    for hk in range(2):
        parts = []
        for j in (2 * hk, 2 * hk + 1):
            xj = x[:, j * LANES:(j + 1) * LANES]
            parts += [jnp.where(mlo, xj, zero), jnp.where(mlo, zero, xj)]
        out.append(jnp.concatenate(parts, axis=0))
    return out


def _unstack_heads(y, mlo):
    cols = []
    for hk in range(2):
        for t in range(2):
            base = 2 * t * BLOCK
            cols.append(jnp.where(mlo, y[hk][base:base + BLOCK], y[hk][base + BLOCK:base + 2 * BLOCK]))
    return jnp.concatenate(cols, axis=1)


def _attn_probs(q4, kx, bias, sink_col):
    s = jnp.concatenate([_dot_nt(q4[0], kx[0]), _dot_nt(q4[1], kx[1])], axis=0)
    s = (s.reshape(N_HEADS, BLOCK, 2 * BLOCK) + bias[None]).reshape(ATT_ROWS, 2 * BLOCK)
    m = jnp.maximum(jnp.max(s, axis=-1, keepdims=True), sink_col)
    p = jnp.exp(s - m)
    e_sink = jnp.exp(sink_col - m)
    inv = 1.0 / (jnp.sum(p, axis=-1, keepdims=True) + e_sink)
    return p * inv, e_sink * inv


def attn_fwd(z, sink_col, bias, g_a):
    s = z.shape[0]
    nb = s // BLOCK

    def body(q_ref, kvc_ref, kvp_ref, sk_ref, b_ref, g_ref, o_ref, y_ref):
        n = pl.program_id(0)
        kx, vx = _attn_band(kvc_ref[...], kvp_ref[...])
        mlo = lax.broadcasted_iota(jnp.int32, (BLOCK, LANES), 1) < HEAD_DIM
        q4 = _stack_heads(q_ref[...] * (HEAD_DIM ** -0.5), mlo)
        pr, _ = _attn_probs(q4, kx, b_ref[jnp.minimum(n, 1)], sk_ref[...])
        prb = pr.astype(bf16)
        half = ATT_ROWS // 2
        o = _unstack_heads([_dot(prb[:half], vx[0]), _dot(prb[half:], vx[1])], mlo)
        o_ref[...] = o.astype(bf16)
        y, _ = _rms_fwd(o, g_ref[...])
        y_ref[...] = y.astype(bf16)

    return pl.pallas_call(
        body, name="attn_fwd", grid=(nb,),
        in_specs=[pl.BlockSpec((BLOCK, ATTN_W), lambda n: (n, 0)),
                  pl.BlockSpec((BLOCK, 2 * KV_W), lambda n: (n, 2)),
                  pl.BlockSpec((BLOCK, 2 * KV_W), lambda n: (jnp.maximum(n - 1, 0), 2)),
                  _full((ATT_ROWS, 1)), _full((2, BLOCK, 2 * BLOCK)), _full((1, ATTN_W))],
        out_specs=[pl.BlockSpec((BLOCK, ATTN_W), lambda n: (n, 0)), pl.BlockSpec((BLOCK, ATTN_W), lambda n: (n, 0))],
        out_shape=[SDS((s, ATTN_W), bf16), SDS((s, ATTN_W), bf16)],
        compiler_params=_params(("parallel",)),
    )(z, z, z, sink_col, bias, g_a)


def _lru_gates(xc, wa, ba, wx, bx, lam):
    xcb = xc.astype(bf16)
    r = _sigmoid(_dot(xcb, wa) + ba)
    ig = _sigmoid(_dot(xcb, wx) + bx)
    sp = _softplus(-lam)
    la = (-LRU_C * r) * sp
    a = jnp.exp(la)
    mult = jnp.sqrt(_neg_expm1(2.0 * la))
    return r, ig, sp, la, a, mult


def branch_fwd(z, p):
    s = z.shape[0]
    tm = _tile(s)
    hb = tm // HALO

    def body(cv_ref, cg_ref, rx_ref, rg_ref, cvh_ref, cgh_ref, rxh_ref,
             cw_ref, cb_ref, lng_ref, lnb_ref, lw_ref, lb_ref, wa_ref, ba_ref, wx_ref, bx_ref, lam_ref, gc_ref, gl_ref,
             conv_ref, hst_ref, nc_ref, nl_ref, carry_ref):
        i = pl.program_id(0)
        first = i == 0

        @pl.when(first)
        def _():
            carry_ref[...] = jnp.zeros_like(carry_ref)

        cval = cv_ref[...].astype(f32)
        u = cval * _sigmoid(cg_ref[...].astype(f32))
        hu = jnp.where(first, 0.0, cvh_ref[...].astype(f32) * _sigmoid(cgh_ref[...].astype(f32)))
        conv = _conv_taps(jnp.concatenate([hu, u], axis=0), cw_ref[...], CONV_K) + cb_ref[...]
        conv_ref[...] = conv
        mu = jnp.mean(conv, axis=-1, keepdims=True)
        xm = conv - mu
        rstd = lax.rsqrt(jnp.mean(xm * xm, axis=-1, keepdims=True) + LN_EPS)
        ln = xm * rstd * lng_ref[...] + lnb_ref[...]
        yc = ln * _sigmoid(ln)
        nc, _ = _rms_fwd(yc, gc_ref[...])
        nc_ref[...] = nc.astype(bf16)

        rx = rx_ref[...].astype(f32)
        hrx = jnp.where(first, 0.0, rxh_ref[...].astype(f32))
        xc = _conv_taps(jnp.concatenate([hrx, rx], axis=0), lw_ref[...], LRU_K) + lb_ref[...]
        r, ig, sp, la, a, mult = _lru_gates(xc, wa_ref[...], ba_ref[...], wx_ref[...], bx_ref[...], lam_ref[...])
        gx = mult * (ig * xc)
        a_cum, h_loc = _scan_fwd(a, gx)
        hs = a_cum * carry_ref[0:1, :] + h_loc
        carry_ref[...] = jnp.broadcast_to(hs[tm - 1:tm, :], carry_ref.shape)
        hst_ref[...] = hs
        gl, _ = _gelu(rg_ref[...].astype(f32))
        nl, _ = _rms_fwd(hs * gl, gl_ref[...])
        nl_ref[...] = nl.astype(bf16)

    def col(c):
        return pl.BlockSpec((tm, CONV_W), lambda i: (i, c))

    def halo(c):
        return pl.BlockSpec((HALO, CONV_W), lambda i: (jnp.maximum(i * hb - 1, 0), c))

    small = [p["cw"], p["cb"], p["lng"], p["lnb"], p["lw"], p["lb"], p["wa"], p["ba"], p["wx"], p["bx"], p["lam"],
             p["gc"], p["gl"]]
    row = pl.BlockSpec((tm, CONV_W), lambda i: (i, 0))
    return pl.pallas_call(
        body, name="branch_fwd", grid=(s // tm,),
        in_specs=[col(3), col(4), col(5), col(6), halo(3), halo(4), halo(5)] + [_full(a.shape) for a in small],
        out_specs=[row, row, row, row],
        out_shape=[SDS((s, CONV_W), f32), SDS((s, LRU_W), f32), SDS((s, CONV_W), bf16), SDS((s, LRU_W), bf16)],
        scratch_shapes=[pltpu.VMEM((8, LRU_W), f32)],
        compiler_params=_params(("arbitrary",)),
    )(z, z, z, z, z, z, z, *small)


def outproj_fwd(ya, yc, yl, h, w_out, g2):
    s = h.shape[0]
    tm = _tile(s)

    def body(ya_ref, yc_ref, yl_ref, h_ref, w_ref, g_ref, h1_ref, hn_ref):
        y = jnp.concatenate([ya_ref[...], yc_ref[...], yl_ref[...]], axis=1)
        h1 = h_ref[...] + _dot(y, w_ref[...])
        h1_ref[...] = h1
        hn, _ = _rms_fwd(h1, g_ref[...])
        hn_ref[...] = hn.astype(bf16)

    return pl.pallas_call(
        body, name="outproj_fwd", grid=(s // tm,),
        in_specs=[pl.BlockSpec((tm, ATTN_W), lambda i: (i, 0)), pl.BlockSpec((tm, CONV_W), lambda i: (i, 0)),
                  pl.BlockSpec((tm, LRU_W), lambda i: (i, 0)), pl.BlockSpec((tm, D_MODEL), lambda i: (i, 0)),
                  _full((D_MODEL, D_MODEL)), _full((1, D_MODEL))],
        out_specs=[pl.BlockSpec((tm, D_MODEL), lambda i: (i, 0)), pl.BlockSpec((tm, D_MODEL), lambda i: (i, 0))],
        out_shape=[SDS((s, D_MODEL), f32), SDS((s, D_MODEL), bf16)],
        compiler_params=_params(("parallel",), VMEM_LIMIT),
    )(ya, yc, yl, h, w_out, g2)


def mlp_fwd(hn2, h1, w_up, w_dn):
    s = h1.shape[0]
    tm = min(256, s)

    def body(x_ref, h_ref, wu_ref, wd_ref, up_ref, h2_ref):
        x = x_ref[...]
        acc = h_ref[...]
        for c in range(N_SHARD):
            u = _dot(x, wu_ref[c])
            up_ref[:, c * FF_CHUNK:(c + 1) * FF_CHUNK] = u.astype(bf16)
            act = jnp.square(jnp.maximum(u, 0.0)).astype(bf16)
            acc = acc + _dot(act, wd_ref[c])
        h2_ref[...] = acc

    return pl.pallas_call(
        body, name="mlp_fwd", grid=(s // tm,),
        in_specs=[pl.BlockSpec((tm, D_MODEL), lambda i: (i, 0)), pl.BlockSpec((tm, D_MODEL), lambda i: (i, 0)),
                  _full((N_SHARD, D_MODEL, FF_CHUNK)), _full((N_SHARD, FF_CHUNK, D_MODEL))],
        out_specs=[pl.BlockSpec((tm, D_FF), lambda i: (i, 0)), pl.BlockSpec((tm, D_MODEL), lambda i: (i, 0))],
        out_shape=[SDS((s, D_FF), bf16), SDS((s, D_MODEL), f32)],
        compiler_params=_params(("parallel",), VMEM_LIMIT),
    )(hn2, h1, w_up, w_dn)


def final_loss(h, tgt, gf):
    s = h.shape[0]
    tm = _tile(s)

    def body(h_ref, t_ref, g_ref, dh_ref, loss_ref, dg_ref):
        i = pl.program_id(0)

        @pl.when(i == 0)
        def _():
            loss_ref[...] = jnp.zeros_like(loss_ref)
            dg_ref[...] = jnp.zeros_like(dg_ref)

        x = h_ref[...]
        g = g_ref[...]
        y, r = _rms_fwd(x, g)
        err = y - t_ref[...]
        part = 0.5 * jnp.sum(jnp.mean(err * err, axis=-1, keepdims=True), axis=0, keepdims=True)
        loss_ref[...] += jnp.broadcast_to(part, loss_ref.shape)
        dx, dg = _rms_bwd(err * (1.0 / D_MODEL), x, r, g)
        dh_ref[...] = dx
        dg_ref[...] += dg

    return pl.pallas_call(
        body, name="final_loss", grid=(s // tm,),
        in_specs=[pl.BlockSpec((tm, D_MODEL), lambda i: (i, 0)), pl.BlockSpec((tm, D_MODEL), lambda i: (i, 0)),
                  _full((1, D_MODEL))],
        out_specs=[pl.BlockSpec((tm, D_MODEL), lambda i: (i, 0)), _full((8, LANES)), _full((1, D_MODEL))],
        out_shape=[SDS((s, D_MODEL), f32), SDS((8, LANES), f32), SDS((1, D_MODEL), f32)],
        compiler_params=_params(("arbitrary",)),
    )(h, tgt, gf)


def mlp_bwd_act(dh, up, h1, g2, w_up, w_dn):
    s = dh.shape[0]
    tm = min(256, s)

    def body(dh_ref, up_ref, h1_ref, g_ref, wu_ref, wd_ref, dup_ref, dh1_ref, dg_ref):
        i = pl.program_id(0)

        @pl.when(i == 0)
        def _():
            dg_ref[...] = jnp.zeros_like(dg_ref)

        dh = dh_ref[...]
        dhb = dh.astype(bf16)
        d_hn = jnp.zeros((tm, D_MODEL), f32)
        for c in range(N_SHARD):
            d_act = _dot_nt(dhb, wd_ref[c])
            u = up_ref[:, c * FF_CHUNK:(c + 1) * FF_CHUNK].astype(f32)
            d_u = (d_act * (2.0 * jnp.maximum(u, 0.0))).astype(bf16)
            dup_ref[:, c * FF_CHUNK:(c + 1) * FF_CHUNK] = d_u
            d_hn = d_hn + _dot_nt(d_u, wu_ref[c])
        x = h1_ref[...]
        g = g_ref[...]
        _, r = _rms_fwd(x, g)
        dx, dg = _rms_bwd(d_hn, x, r, g)
        dh1_ref[...] = dh + dx
        dg_ref[...] += dg

    return pl.pallas_call(
        body, name="mlp_bwd_act", grid=(s // tm,),
        in_specs=[pl.BlockSpec((tm, D_MODEL), lambda i: (i, 0)), pl.BlockSpec((tm, D_FF), lambda i: (i, 0)),
                  pl.BlockSpec((tm, D_MODEL), lambda i: (i, 0)), _full((1, D_MODEL)),
                  _full((N_SHARD, D_MODEL, FF_CHUNK)), _full((N_SHARD, FF_CHUNK, D_MODEL))],
        out_specs=[pl.BlockSpec((tm, D_FF), lambda i: (i, 0)), pl.BlockSpec((tm, D_MODEL), lambda i: (i, 0)),
                   _full((1, D_MODEL))],
        out_shape=[SDS((s, D_FF), bf16), SDS((s, D_MODEL), f32), SDS((1, D_MODEL), f32)],
        compiler_params=_params(("arbitrary",), VMEM_LIMIT),
    )(dh, up, h1, g2, w_up, w_dn)


def mlp_bwd_w(hn2, d_up, up, dh):
    s = dh.shape[0]
    tk = _tile(s)

    def body(x_ref, du_ref, up_ref, dh_ref, dwu_ref, dwd_ref):
        k = pl.program_id(1)

        @pl.when(k == 0)
        def _():
            dwu_ref[...] = jnp.zeros_like(dwu_ref)
            dwd_ref[...] = jnp.zeros_like(dwd_ref)

        dwu_ref[0] += _dot_tn(x_ref[...], du_ref[...])
        act = jnp.square(jnp.maximum(up_ref[...].astype(f32), 0.0)).astype(bf16)
        dwd_ref[0] += _dot_tn(act, dh_ref[...].astype(bf16))

    return pl.pallas_call(
        body, name="mlp_bwd_w", grid=(N_SHARD, s // tk),
        in_specs=[pl.BlockSpec((tk, D_MODEL), lambda c, k: (k, 0)), pl.BlockSpec((tk, FF_CHUNK), lambda c, k: (k, c)),
                  pl.BlockSpec((tk, FF_CHUNK), lambda c, k: (k, c)), pl.BlockSpec((tk, D_MODEL), lambda c, k: (k, 0))],
        out_specs=[pl.BlockSpec((1, D_MODEL, FF_CHUNK), lambda c, k: (c, 0, 0)),
                   pl.BlockSpec((1, FF_CHUNK, D_MODEL), lambda c, k: (c, 0, 0))],
        out_shape=[SDS((N_SHARD, D_MODEL, FF_CHUNK), f32), SDS((N_SHARD, FF_CHUNK, D_MODEL), f32)],
        compiler_params=_params(("parallel", "arbitrary"), VMEM_LIMIT),
    )(hn2, d_up, up, dh)


def outproj_bwd(dh1, ya, yc, yl, w_out):
    s = dh1.shape[0]
    tm = _tile(s)

    def body(dh_ref, ya_ref, yc_ref, yl_ref, w_ref, dy_ref, dw_ref):
        i = pl.program_id(0)

        @pl.when(i == 0)
        def _():
            dw_ref[...] = jnp.zeros_like(dw_ref)

        dhb = dh_ref[...].astype(bf16)
        dy_ref[...] = _dot_nt(dhb, w_ref[...])
        y = jnp.concatenate([ya_ref[...], yc_ref[...], yl_ref[...]], axis=1)
        dw_ref[...] += _dot_tn(y, dhb)

    return pl.pallas_call(
        body, name="outproj_bwd", grid=(s // tm,),
        in_specs=[pl.BlockSpec((tm, D_MODEL), lambda i: (i, 0)), pl.BlockSpec((tm, ATTN_W), lambda i: (i, 0)),
                  pl.BlockSpec((tm, CONV_W), lambda i: (i, 0)), pl.BlockSpec((tm, LRU_W), lambda i: (i, 0)),
                  _full((D_MODEL, D_MODEL))],
        out_specs=[pl.BlockSpec((tm, D_MODEL), lambda i: (i, 0)), _full((D_MODEL, D_MODEL))],
        out_shape=[SDS((s, D_MODEL), f32), SDS((D_MODEL, D_MODEL), f32)],
        compiler_params=_params(("arbitrary",), VMEM_LIMIT),
    )(dh1, ya, yc, yl, w_out)


def attn_bwd(z, o, dy, sink_col, bias, g_a):
    s = z.shape[0]
    nb = s // BLOCK

    def body(q_ref, kvc_ref, kvp_ref, o_ref, dy_ref, sk_ref, b_ref, g_ref, dq_ref, dkv_ref, dsk_ref, dg_ref,
             pend_ref, dsk_acc):
        n = pl.program_id(0)

        @pl.when(n == 0)
        def _():
            pend_ref[...] = jnp.zeros_like(pend_ref)
            dsk_acc[...] = jnp.zeros_like(dsk_acc)
            dg_ref[...] = jnp.zeros_like(dg_ref)

        @pl.when(n < nb)
        def _():
            kx, vx = _attn_band(kvc_ref[...], kvp_ref[...])
            mlo = lax.broadcasted_iota(jnp.int32, (BLOCK, LANES), 1) < HEAD_DIM
            scale = HEAD_DIM ** -0.5
            q4 = _stack_heads(q_ref[...] * scale, mlo)
            o_f = o_ref[...].astype(f32)
            g = g_ref[...]
            _, r = _rms_fwd(o_f, g)
            d_o, dg = _rms_bwd(dy_ref[...], o_f, r, g)
            dg_ref[...] += dg
            do4 = _stack_heads(d_o.astype(bf16), mlo)
            pr, p_sink = _attn_probs(q4, kx, b_ref[jnp.minimum(n, 1)], sk_ref[...])
            d_p = jnp.concatenate([_dot_nt(do4[0], vx[0]), _dot_nt(do4[1], vx[1])], axis=0)
            d_row = jnp.sum(pr * d_p, axis=-1, keepdims=True)
            d_s = (pr * (d_p - d_row)).astype(bf16)
            dsk_acc[...] -= p_sink * d_row
            prb = pr.astype(bf16)
            half = ATT_ROWS // 2
            dq4 = [_dot(d_s[:half], kx[0]), _dot(d_s[half:], kx[1])]
            dq_ref[...] = (_unstack_heads(dq4, mlo) * scale).astype(bf16)
            tk = [_dot_tn(d_s[:half], q4[0]), _dot_tn(d_s[half:], q4[1])]
            tv = [_dot_tn(prb[:half], do4[0]), _dot_tn(prb[half:], do4[1])]
            lane = lax.broadcasted_iota(jnp.int32, (2 * BLOCK, LANES), 1)
            fk = [t + pltpu.roll(t, HEAD_DIM, 1) for t in tk]
            fv = [t + pltpu.roll(t, HEAD_DIM, 1) for t in tv]
            band = jnp.concatenate([jnp.where(lane < HEAD_DIM, fk[0], fk[1]), jnp.where(lane < HEAD_DIM, fv[0], fv[1])], axis=1)
            dkv_ref[...] = (pend_ref[...] + band[:BLOCK]).astype(bf16)
            pend_ref[...] = band[BLOCK:]

        @pl.when(n == nb)
        def _():
            dkv_ref[...] = pend_ref[...].astype(bf16)
            for hh in range(N_HEADS):
                tot = jnp.sum(dsk_acc[hh * BLOCK:(hh + 1) * BLOCK, :], axis=0, keepdims=True)
                dsk_ref[hh:hh + 1, :] = jnp.broadcast_to(tot, (1, LANES))

    def cur(n):
        return jnp.minimum(n, nb - 1)

    return pl.pallas_call(
        body, name="attn_bwd", grid=(nb + 1,),
        in_specs=[pl.BlockSpec((BLOCK, ATTN_W), lambda n: (cur(n), 0)),
                  pl.BlockSpec((BLOCK, 2 * KV_W), lambda n: (cur(n), 2)),
                  pl.BlockSpec((BLOCK, 2 * KV_W), lambda n: (jnp.maximum(cur(n) - 1, 0), 2)),
                  pl.BlockSpec((BLOCK, ATTN_W), lambda n: (cur(n), 0)),
                  pl.BlockSpec((BLOCK, ATTN_W), lambda n: (cur(n), 0)),
                  _full((ATT_ROWS, 1)), _full((2, BLOCK, 2 * BLOCK)), _full((1, ATTN_W))],
        out_specs=[pl.BlockSpec((BLOCK, ATTN_W), lambda n: (cur(n), 0)),
                   pl.BlockSpec((BLOCK, 2 * KV_W), lambda n: (jnp.maximum(n - 1, 0), 0)),
                   _full((N_HEADS, LANES)), _full((1, ATTN_W))],
        out_shape=[SDS((s, ATTN_W), bf16), SDS((s, 2 * KV_W), bf16), SDS((N_HEADS, LANES), f32), SDS((1, ATTN_W), f32)],
        scratch_shapes=[pltpu.VMEM((BLOCK, 2 * KV_W), f32), pltpu.VMEM((ATT_ROWS, 1), f32)],
        compiler_params=_params(("arbitrary",)),
    )(z, z, z, o, dy, sink_col, bias, g_a)


_V_GC, _V_LNG, _V_LNB, _V_CB, _V_GL, _V_BA, _V_BX, _V_LAM, _V_LB = range(9)
_V_ROWS = 16


def branch_bwd_a(z, conv, hst, dy, p):
    s = z.shape[0]
    tm = _tile(s)
    nt = s // tm
    hb = tm // HALO
    h8 = tm // 8

    def body(conv_ref, dyc_ref, dyl_ref, rx_ref, rxh_ref, rg_ref, hst_ref, hsth_ref,
             lng_ref, lnb_ref, lw_ref, lb_ref, wa_ref, ba_ref, wx_ref, bx_ref, lam_ref, gc_ref, gl_ref,
             dconv_ref, dxc_ref, drg_ref, vec_ref, dwa_ref, dwx_ref, carry_ref):
        i = pl.program_id(0)
        ti = nt - 1 - i

        @pl.when(i == 0)
        def _():
            carry_ref[...] = jnp.zeros_like(carry_ref)
            vec_ref[...] = jnp.zeros_like(vec_ref)
            dwa_ref[...] = jnp.zeros_like(dwa_ref)
            dwx_ref[...] = jnp.zeros_like(dwx_ref)

        conv = conv_ref[...]
        mu = jnp.mean(conv, axis=-1, keepdims=True)
        xm = conv - mu
        rstd = lax.rsqrt(jnp.mean(xm * xm, axis=-1, keepdims=True) + LN_EPS)
        xhat = xm * rstd
        lng = lng_ref[...]
        ln = xhat * lng + lnb_ref[...]
        sg = _sigmoid(ln)
        yc = ln * sg
        gc = gc_ref[...]
        _, rc = _rms_fwd(yc, gc)
        d_yc, d_gc = _rms_bwd(dyc_ref[...], yc, rc, gc)
        d_ln = d_yc * (sg * (1.0 + ln * (1.0 - sg)))
        d_xhat = d_ln * lng
        d_conv = rstd * (d_xhat - jnp.mean(d_xhat, axis=-1, keepdims=True)
                         - xhat * jnp.mean(d_xhat * xhat, axis=-1, keepdims=True))
        dconv_ref[...] = d_conv

        rx = rx_ref[...].astype(f32)
        hrx = jnp.where(ti == 0, 0.0, rxh_ref[...].astype(f32))
        xc = _conv_taps(jnp.concatenate([hrx, rx], axis=0), lw_ref[...], LRU_K) + lb_ref[...]
        wa = wa_ref[...]
        wx = wx_ref[...]
        lam = lam_ref[...]
        r, ig, sp, la, a, mult = _lru_gates(xc, wa, ba_ref[...], wx, bx_ref[...], lam)
        hs = hst_ref[...]
        row = lax.broadcasted_iota(jnp.int32, hs.shape, 0)
        h_before = jnp.where(ti == 0, 0.0, hsth_ref[7:8, :])
        h_prev = jnp.where(row == 0, h_before, pltpu.roll(hs, 1, 0))
        rg = rg_ref[...].astype(f32)
        gl, tg = _gelu(rg)
        out = hs * gl
        gmix = gl_ref[...]
        _, rl = _rms_fwd(out, gmix)
        d_out, d_gl = _rms_bwd(dyl_ref[...], out, rl, gmix)
        drg_ref[...] = (d_out * hs * _gelu_grad(rg, tg)).astype(bf16)
        d_h = d_out * gl
        last = row == tm - 1
        a_next = jnp.where(last, 0.0, pltpu.roll(a, tm - 1, 0))
        lmb = _scan_bwd(a_next, d_h + jnp.where(last, carry_ref[0:1, :], 0.0))
        carry_ref[...] = jnp.broadcast_to(a[0:1, :] * lmb[0:1, :], carry_ref.shape)
        d_a = lmb * h_prev
        d_mult = lmb * (ig * xc)
        d_ig = lmb * (mult * xc)
        d_la = d_a * a - d_mult * (a * a) / jnp.maximum(mult, 1e-30)
        d_pa = (d_la * (-LRU_C * sp)) * (r * (1.0 - r))
        d_px = d_ig * (ig * (1.0 - ig))
        d_pab = d_pa.astype(bf16)
        d_pxb = d_px.astype(bf16)
        d_xc = lmb * (mult * ig) + _dot_nt(d_pab, wa) + _dot_nt(d_pxb, wx)
        dxc_ref[...] = d_xc
        xcb = xc.astype(bf16)
        dwa_ref[...] += _dot_tn(xcb, d_pab)
        dwx_ref[...] += _dot_tn(xcb, d_pxb)
        d_lam = jnp.sum(d_la * (-LRU_C * r), axis=0, keepdims=True) * (-_sigmoid(-lam))

        def colsum(v):
            return jnp.sum(v, axis=0, keepdims=True)

        rows = [None] * _V_ROWS
        rows[_V_GC] = d_gc
        rows[_V_LNG] = colsum(d_ln * xhat)
        rows[_V_LNB] = colsum(d_ln)
        rows[_V_CB] = colsum(d_conv)
        rows[_V_GL] = d_gl
        rows[_V_BA] = colsum(d_pa)
        rows[_V_BX] = colsum(d_px)
        rows[_V_LAM] = d_lam
        rows[_V_LB] = colsum(d_xc)
        zero = jnp.zeros((1, CONV_W), f32)
        vec_ref[...] += jnp.concatenate([zero if v is None else v for v in rows], axis=0)

    def rev(c):
        return pl.BlockSpec((tm, CONV_W), lambda i: (nt - 1 - i, c))

    small = [p["lng"], p["lnb"], p["lw"], p["lb"], p["wa"], p["ba"], p["wx"], p["bx"], p["lam"], p["gc"], p["gl"]]
    return pl.pallas_call(
        body, name="branch_bwd_a", grid=(nt,),
        in_specs=[rev(0), rev(2), rev(3), rev(5),
                  pl.BlockSpec((HALO, CONV_W), lambda i: (jnp.maximum((nt - 1 - i) * hb - 1, 0), 5)),
                  rev(6), rev(0),
                  pl.BlockSpec((8, LRU_W), lambda i: (jnp.maximum((nt - 1 - i) * h8 - 1, 0), 0))]
                 + [_full(a.shape) for a in small],
        out_specs=[rev(0), rev(0), rev(0), _full((_V_ROWS, CONV_W)), _full((LRU_W, LRU_W)), _full((LRU_W, LRU_W))],
        out_shape=[SDS((s, CONV_W), f32), SDS((s, LRU_W), f32), SDS((s, LRU_W), bf16), SDS((_V_ROWS, CONV_W), f32),
                   SDS((LRU_W, LRU_W), f32), SDS((LRU_W, LRU_W), f32)],
        scratch_shapes=[pltpu.VMEM((8, LRU_W), f32)],
        compiler_params=_params(("arbitrary",)),
    )(conv, dy, dy, z, z, z, hst, hst, *small)


def branch_bwd_b(z, d_conv, d_xc, p):
    s = z.shape[0]
    tm = _tile(s)
    nt = s // tm
    hb = tm // HALO

    def body(cv_ref, cg_ref, cvh_ref, cgh_ref, rx_ref, rxh_ref, dc_ref, dch_ref, dx_ref, dxh_ref, cw_ref, lw_ref,
             dzc_ref, dzr_ref, dcw_ref, dlw_ref):
        i = pl.program_id(0)

        @pl.when(i == 0)
        def _():
            dcw_ref[...] = jnp.zeros_like(dcw_ref)
            dlw_ref[...] = jnp.zeros_like(dlw_ref)

        first = i == 0
        last = i == nt - 1
        cval = cv_ref[...].astype(f32)
        sg = _sigmoid(cg_ref[...].astype(f32))
        u = cval * sg
        hu = jnp.where(first, 0.0, cvh_ref[...].astype(f32) * _sigmoid(cgh_ref[...].astype(f32)))
        dpad = jnp.concatenate([dc_ref[...], jnp.where(last, 0.0, dch_ref[...])], axis=0)
        d_u, dw_rows = _conv_taps_bwd(dpad, jnp.concatenate([hu, u], axis=0), cw_ref[...], CONV_K, tm)
        dcw_ref[...] += jnp.concatenate(dw_rows + [jnp.zeros((HALO - CONV_K, CONV_W), f32)], axis=0)
        dzc_ref[...] = jnp.concatenate([d_u * sg, d_u * cval * sg * (1.0 - sg)], axis=1).astype(bf16)

        rx = rx_ref[...].astype(f32)
        hrx = jnp.where(first, 0.0, rxh_ref[...].astype(f32))
        dxpad = jnp.concatenate([dx_ref[...], jnp.where(last, 0.0, dxh_ref[...])], axis=0)
        d_rx, dlw_rows = _conv_taps_bwd(dxpad, jnp.concatenate([hrx, rx], axis=0), lw_ref[...], LRU_K, tm)
        dlw_ref[...] += jnp.concatenate(dlw_rows + [jnp.zeros((8 - LRU_K, LRU_W), f32)], axis=0)
        dzr_ref[...] = d_rx.astype(bf16)

    def col(c):
        return pl.BlockSpec((tm, CONV_W), lambda i: (i, c))

    def prev(c):
        return pl.BlockSpec((HALO, CONV_W), lambda i: (jnp.maximum(i * hb - 1, 0), c))

    nxt = pl.BlockSpec((HALO, CONV_W), lambda i: (jnp.minimum((i + 1) * hb, nt * hb - 1), 0))
    return pl.pallas_call(
        body, name="branch_bwd_b", grid=(nt,),
        in_specs=[col(3), col(4), prev(3), prev(4), col(5), prev(5), col(0), nxt, col(0), nxt,
                  _full(p["cw"].shape), _full(p["lw"].shape)],
        out_specs=[pl.BlockSpec((tm, 2 * CONV_W), lambda i: (i, 0)), pl.BlockSpec((tm, LRU_W), lambda i: (i, 0)),
                   _full((HALO, CONV_W)), _full((8, LRU_W))],
        out_shape=[SDS((s, 2 * CONV_W), bf16), SDS((s, LRU_W), bf16), SDS((HALO, CONV_W), f32), SDS((8, LRU_W), f32)],
        compiler_params=_params(("arbitrary",)),
    )(z, z, z, z, z, z, d_conv, d_conv, d_xc, d_xc, p["cw"], p["lw"])


def inproj_bwd(dq, dkv, dzc, dzr, drg, h, g1, w_in, dh1):
    s = h.shape[0]
    tm = _tile(s)

    def body(dq_ref, dkv_ref, dzc_ref, dzr_ref, drg_ref, h_ref, g_ref, w_ref, dh1_ref, dh_ref, dw_ref, dg_ref):
        i = pl.program_id(0)

        @pl.when(i == 0)
        def _():
            dw_ref[...] = jnp.zeros_like(dw_ref)
            dg_ref[...] = jnp.zeros_like(dg_ref)

        dz = jnp.concatenate([dq_ref[...], dkv_ref[...], dzc_ref[...], dzr_ref[...], drg_ref[...]], axis=1)
        x = h_ref[...]
        g = g_ref[...]
        hn, r = _rms_fwd(x, g)
        d_hn = _dot_nt(dz, w_ref[...])
        dw_ref[...] += _dot_tn(hn.astype(bf16), dz)
        dx, dg = _rms_bwd(d_hn, x, r, g)
        dh_ref[...] = dh1_ref[...] + dx
        dg_ref[...] += dg

    def rowb(w):
        return pl.BlockSpec((tm, w), lambda i: (i, 0))

    return pl.pallas_call(
        body, name="inproj_bwd", grid=(s // tm,),
        in_specs=[rowb(ATTN_W), rowb(2 * KV_W), rowb(2 * CONV_W), rowb(LRU_W), rowb(LRU_W), rowb(D_MODEL),
                  _full((1, D_MODEL)), _full((D_MODEL, IN_W)), rowb(D_MODEL)],
        out_specs=[rowb(D_MODEL), _full((D_MODEL, IN_W)), _full((1, D_MODEL))],
        out_shape=[SDS((s, D_MODEL), f32), SDS((D_MODEL, IN_W), f32), SDS((1, D_MODEL), f32)],
        compiler_params=_params(("arbitrary",), VMEM_LIMIT),
    )(dq, dkv, dzc, dzr, drg, h, g1, w_in, dh1)


def _block_diag(w):
    out = jnp.zeros((LRU_W, LRU_W), w.dtype)
    hd = LRU_W // LRU_HEADS
    for hh in range(LRU_HEADS):
        out = out.at[hh * hd:(hh + 1) * hd, hh * hd:(hh + 1) * hd].set(w[hh])
    return out


def _diag_blocks(w):
    hd = LRU_W // LRU_HEADS
    return jnp.stack([w[hh * hd:(hh + 1) * hd, hh * hd:(hh + 1) * hd] for hh in range(LRU_HEADS)])


def _layer_params(sp, l):
    mix = sp["mix_norm"][l]
    return dict(
        g1=sp["norm1"][l][None, :], g2=sp["norm2"][l][None, :],
        sinks=jnp.repeat(sp["attn_sinks"][l], BLOCK)[:, None],
        ga=mix[None, :ATTN_W], gc=mix[None, ATTN_W:ATTN_W + CONV_W], gl=mix[None, ATTN_W + CONV_W:],
        cw=jnp.pad(sp["conv_dw_w"][l], ((0, HALO - CONV_K), (0, 0))), cb=sp["conv_dw_b"][l][None, :],
        lng=sp["conv_ln_g"][l][None, :], lnb=sp["conv_ln_b"][l][None, :],
        lw=jnp.pad(sp["lru_conv_w"][l], ((0, 8 - LRU_K), (0, 0))), lb=sp["lru_conv_b"][l][None, :],
        wa=_block_diag(sp["lru_wa"][l]).astype(bf16), ba=sp["lru_ba"][l].reshape(1, LRU_W),
        wx=_block_diag(sp["lru_wx"][l]).astype(bf16), bx=sp["lru_bx"][l].reshape(1, LRU_W),
        lam=sp["lru_lambda"][l][None, :],
    )


def local_step(x, tgt, big, sp):
    lp = [_layer_params(sp, l) for l in range(DEPTH)]
    bias = _attn_bias()
    saved = []
    h = x
    for l in range(DEPTH):
        p, w = lp[l], big[l]
        z = inproj_fwd(h, p["g1"], w["w_in"])
        o, ya = attn_fwd(z, p["sinks"], bias, p["ga"])
        conv, hst, yc, yl = branch_fwd(z, p)
        h1, hn2 = outproj_fwd(ya, yc, yl, h, w["w_out"], p["g2"])
        up, h2 = mlp_fwd(hn2, h1, w["w_up"], w["w_dn"])
        saved.append(dict(h=h, z=z, o=o, ya=ya, conv=conv, hst=hst, yc=yc, yl=yl, h1=h1, hn2=hn2, up=up))
        h = h2
    dh, loss, d_gf = final_loss(h, tgt, sp["final_norm"][None, :])
    big_g = [None] * DEPTH
    small_g = [None] * DEPTH
    for l in reversed(range(DEPTH)):
        p, w, sv = lp[l], big[l], saved[l]
        d_up, dh1, d_g2 = mlp_bwd_act(dh, sv["up"], sv["h1"], p["g2"], w["w_up"], w["w_dn"])
        dw_up, dw_dn = mlp_bwd_w(sv["hn2"], d_up, sv["up"], dh)
        dy, dw_out = outproj_bwd(dh1, sv["ya"], sv["yc"], sv["yl"], w["w_out"])
        dq, dkv, d_sk, d_ga = attn_bwd(sv["z"], sv["o"], dy, p["sinks"], bias, p["ga"])
        d_conv, d_xc, d_rg, vec, dwa, dwx = branch_bwd_a(sv["z"], sv["conv"], sv["hst"], dy, p)
        dzc, dzr, dcw, dlw = branch_bwd_b(sv["z"], d_conv, d_xc, p)
        dh, dw_in, d_g1 = inproj_bwd(dq, dkv, dzc, dzr, d_rg, sv["h"], p["g1"], w["w_in"], dh1)
        big_g[l] = dict(w_in=dw_in, w_out=dw_out, w_up=dw_up, w_dn=dw_dn)
        hd = LRU_W // LRU_HEADS
        small_g[l] = dict(
            norm1=d_g1[0], attn_sinks=d_sk[:, 0], conv_dw_w=dcw[:CONV_K], conv_dw_b=vec[_V_CB],
            conv_ln_g=vec[_V_LNG], conv_ln_b=vec[_V_LNB], lru_conv_w=dlw[:LRU_K], lru_conv_b=vec[_V_LB],
            lru_wa=_diag_blocks(dwa), lru_ba=vec[_V_BA].reshape(LRU_HEADS, hd),
            lru_wx=_diag_blocks(dwx), lru_bx=vec[_V_BX].reshape(LRU_HEADS, hd), lru_lambda=vec[_V_LAM],
            mix_norm=jnp.concatenate([d_ga[0], vec[_V_GC], vec[_V_GL]]), norm2=d_g2[0],
        )
    return loss, dh, big_g, small_g, d_gf[0]


_HBM = pl.BlockSpec(memory_space=pl.ANY)


def _place():
    x, y, c = lax.axis_index("x"), lax.axis_index("y"), lax.axis_index("c")
    chips = [(1 - x, y), (x, 1 - y), (1 - x, 1 - y)]
    return x, y, c, chips


def place_shard(a, idx, dtype):
    _, r, cdim = a.shape
    tr = min(r, 512)

    def body(idx_ref, a_ref, o_ref):
        o_ref[0, 0] = a_ref[0].astype(dtype)

    return pl.pallas_call(
        body, name="place_shard",
        grid_spec=pltpu.PrefetchScalarGridSpec(
            num_scalar_prefetch=1, grid=(DEPTH, r // tr),
            in_specs=[pl.BlockSpec((1, tr, cdim), lambda l, i, idx_ref: (l, i, 0))],
            out_specs=pl.BlockSpec((1, 1, tr, cdim), lambda l, i, idx_ref: (l, idx_ref[1], i, 0))),
        out_shape=SDS((DEPTH, N_SHARD, r, cdim), dtype),
        compiler_params=_params(("arbitrary", "arbitrary")),
    )(idx, a)


def gather_weights(slots):
    nw = len(slots)

    def body(*refs):
        bufs = refs[nw:2 * nw]
        ici_send, ici_recv, d2d_send, d2d_recv = refs[2 * nw:]
        x, y, c, chips = _place()
        s_me = 2 * x + y
        sib = (x, y, 1 - c)

        def slot(j, w, layer):
            cx, cy = chips[j]
            return bufs[w].at[layer, 2 * cx + cy]

        def ici(j, w):
            cx, cy = chips[j]
            mine = bufs[w].at[c, s_me]
            return pltpu.make_async_remote_copy(
                src_ref=mine, dst_ref=mine, send_sem=ici_send.at[nw * j + w], recv_sem=ici_recv.at[nw * j + w],
                device_id=(cx, cy, c), device_id_type=MESH)

        def d2d(j, w, layer):
            return pltpu.make_async_remote_copy(
                src_ref=slot(j, w, layer), dst_ref=slot(j, w, layer), send_sem=d2d_send.at[nw * j + w],
                recv_sem=d2d_recv.at[nw * j + w], device_id=sib, device_id_type=MESH)

        first = [ici(j, w) for j in range(3) for w in range(nw)]
        for cp in first:
            cp.start()
        passed = []
        for j in range(3):
            for w in range(nw):
                pltpu.make_async_remote_copy(
                    src_ref=slot(j, w, c), dst_ref=slot(j, w, c), send_sem=ici_send.at[nw * j + w],
                    recv_sem=ici_recv.at[nw * j + w], device_id=sib, device_id_type=MESH).wait_recv()
                cp = d2d(j, w, c)
                cp.start()
                passed.append(cp)
        for j in range(3):
            for w in range(nw):
                d2d(j, w, 1 - c).wait_recv()
        for cp in first + passed:
            cp.wait_send()

    out_shape = [SDS(a.shape, a.dtype) for a in slots]
    return pl.pallas_call(
        body, name="gather_weights", in_specs=[_HBM] * nw, out_specs=[_HBM] * nw, out_shape=out_shape,
        input_output_aliases={w: w for w in range(nw)},
        scratch_shapes=[pltpu.SemaphoreType.DMA((3 * nw,)), pltpu.SemaphoreType.DMA((3 * nw,)),
                        pltpu.SemaphoreType.DMA((3 * nw,)), pltpu.SemaphoreType.DMA((3 * nw,))],
    )(*slots)


def pair_exchange(parts):
    nw = len(parts[0])

    def body(*refs):
        ins = [refs[l * nw:(l + 1) * nw] for l in range(DEPTH)]
        outs = refs[DEPTH * nw:(DEPTH + 1) * nw]
        send_sem, recv_sem = refs[(DEPTH + 1) * nw:]
        x, y, c, _ = _place()
        copies = []
        for l in range(DEPTH):
            for w in range(nw):
                half = ins[l][w].shape[1] // 2
                cp = pltpu.make_async_remote_copy(
                    src_ref=ins[l][w].at[:, pl.ds(pl.multiple_of((1 - c) * half, 8), half), :], dst_ref=outs[w].at[l],
                    send_sem=send_sem.at[nw * l + w], recv_sem=recv_sem.at[nw * l + w],
                    device_id=(x, y, 1 - c), device_id_type=MESH)
                cp.start()
                copies.append(cp)
        for cp in copies:
            cp.wait()

    flat = [a for l in range(DEPTH) for a in parts[l]]
    out_shape = [SDS((DEPTH, N_SHARD, a.shape[1] // 2, a.shape[2]), f32) for a in parts[0]]
    return pl.pallas_call(
        body, name="pair_exchange", in_specs=[_HBM] * len(flat), out_specs=[_HBM] * nw, out_shape=out_shape,
        scratch_shapes=[pltpu.SemaphoreType.DMA((DEPTH * nw,)), pltpu.SemaphoreType.DMA((DEPTH * nw,))],
    )(*flat)


def chip_sum(g0, g1, recv, c_idx):
    _, r, cdim = g0.shape
    half = r // 2
    tr = min(half, 256)
    nh = half // tr

    def body(c_ref, g0_ref, g1_ref, r_ref, o32_ref, o16_ref):
        l = pl.program_id(0)
        mine = jnp.where(l == 0, g0_ref[...], g1_ref[...])
        tot = mine[0] + r_ref[0, 0]
        o32_ref[0, 0] = tot
        o16_ref[0, 0] = tot.astype(bf16)

    def layer_map(layer):
        def index(l, s, i, c_ref):
            on = l == layer
            return (jnp.where(on, s, 0), jnp.where(on, c_ref[0] * nh + i, 0), 0)
        return index

    blk4 = pl.BlockSpec((1, 1, tr, cdim), lambda l, s, i, c_ref: (l, s, i, 0))
    return pl.pallas_call(
        body, name="chip_sum",
        grid_spec=pltpu.PrefetchScalarGridSpec(
            num_scalar_prefetch=1, grid=(DEPTH, N_SHARD, nh),
            in_specs=[pl.BlockSpec((1, tr, cdim), layer_map(0)), pl.BlockSpec((1, tr, cdim), layer_map(1)), blk4],
            out_specs=[blk4, blk4]),
        out_shape=[SDS((DEPTH, N_SHARD, half, cdim), f32), SDS((DEPTH, N_SHARD, half, cdim), bf16)],
        compiler_params=_params(("arbitrary", "arbitrary", "arbitrary")),
    )(c_idx, g0, g1, recv)


def shard_exchange(sums16):
    nw = len(sums16)

    def body(*refs):
        ins, outs = refs[:nw], refs[nw:2 * nw]
        send_sem, recv_sem = refs[2 * nw:]
        x, y, c, chips = _place()
        copies = []
        for j, (cx, cy) in enumerate(chips):
            for l in range(DEPTH):
                for w in range(nw):
                    k = (j * DEPTH + l) * nw + w
                    cp = pltpu.make_async_remote_copy(
                        src_ref=ins[w].at[l, 2 * cx + cy], dst_ref=outs[w].at[l, j], send_sem=send_sem.at[k],
                        recv_sem=recv_sem.at[k], device_id=(cx, cy, c), device_id_type=MESH)
                    cp.start()
                    copies.append(cp)
        for cp in copies:
            cp.wait()

    out_shape = [SDS((DEPTH, 3) + a.shape[2:], bf16) for a in sums16]
    return pl.pallas_call(
        body, name="shard_exchange", in_specs=[_HBM] * nw, out_specs=[_HBM] * nw, out_shape=out_shape,
        scratch_shapes=[pltpu.SemaphoreType.DMA((3 * DEPTH * nw,)), pltpu.SemaphoreType.DMA((3 * DEPTH * nw,))],
    )(*sums16)


def shard_sum(sum32, recv16, idx):
    _, _, half, cdim = sum32.shape
    tr = min(half, 256)

    def body(idx_ref, a_ref, r0_ref, r1_ref, r2_ref, o_ref):
        o_ref[0, 0] = ((a_ref[0, 0] + r0_ref[0, 0].astype(f32)) + r1_ref[0, 0].astype(f32)) + r2_ref[0, 0].astype(f32)

    def rel(j):
        return pl.BlockSpec((1, 1, tr, cdim), lambda l, i, idx_ref: (l, j, i, 0))

    return pl.pallas_call(
        body, name="shard_sum",
        grid_spec=pltpu.PrefetchScalarGridSpec(
            num_scalar_prefetch=1, grid=(DEPTH, half // tr),
            in_specs=[pl.BlockSpec((1, 1, tr, cdim), lambda l, i, idx_ref: (l, idx_ref[1], i, 0)), rel(0), rel(1), rel(2)],
            out_specs=pl.BlockSpec((1, 1, tr, cdim), lambda l, i, idx_ref: (l, idx_ref[0], i, 0))),
        out_shape=SDS((DEPTH, 2, half, cdim), f32),
        compiler_params=_params(("arbitrary", "arbitrary")),
    )(idx, sum32, recv16, recv16, recv16)


def halves_exchange(tots):
    nw = len(tots)

    def body(*refs):
        bufs = refs[nw:2 * nw]
        send_sem, recv_sem = refs[2 * nw:]
        x, y, c, _ = _place()

        def copy(w, l, half_idx):
            return pltpu.make_async_remote_copy(
                src_ref=bufs[w].at[l, half_idx], dst_ref=bufs[w].at[l, half_idx], send_sem=send_sem.at[DEPTH * w + l],
                recv_sem=recv_sem.at[DEPTH * w + l], device_id=(x, y, 1 - c), device_id_type=MESH)

        sends = [copy(w, l, c) for w in range(nw) for l in range(DEPTH)]
        for cp in sends:
            cp.start()
        for w in range(nw):
            for l in range(DEPTH):
                copy(w, l, 1 - c).wait_recv()
        for cp in sends:
            cp.wait_send()

    return pl.pallas_call(
        body, name="halves_exchange", in_specs=[_HBM] * nw, out_specs=[_HBM] * nw,
        out_shape=[SDS(a.shape, f32) for a in tots], input_output_aliases={w: w for w in range(nw)},
        scratch_shapes=[pltpu.SemaphoreType.DMA((DEPTH * nw,)), pltpu.SemaphoreType.DMA((DEPTH * nw,))],
    )(*tots)


N_DEV = 8


def small_allreduce(vec):
    r = vec.shape[0]

    def body(v_ref, o_ref, all_ref, send_sems, recv_sems):
        x, y, c, chips = _place()
        me, sib = (x, y, c), (x, y, 1 - c)

        def rows(px, py, pc):
            return all_ref.at[4 * px + 2 * py + pc]

        def copy(k, block, to, src=None):
            return pltpu.make_async_remote_copy(
                src_ref=rows(*block) if src is None else src, dst_ref=rows(*block), send_sem=send_sems.at[k],
                recv_sem=recv_sems.at[k], device_id=to, device_id_type=MESH)

        first = [copy(0, me, sib, src=v_ref)]
        first += [copy(1 + j, me, (*chip, c), src=v_ref) for j, chip in enumerate(chips)]
        for cp in first:
            cp.start()
        rows(*me)[...] = v_ref[...]
        passed = [copy(4 + j, (*chip, c), sib) for j, chip in enumerate(chips)]
        for j, chip in enumerate(chips):
            copy(1 + j, (*chip, c), me).wait_recv()
            passed[j].start()
        copy(0, sib, me).wait_recv()
        for j, chip in enumerate(chips):
            copy(4 + j, (*chip, 1 - c), me).wait_recv()
        for cp in first + passed:
            cp.wait_send()
        acc = all_ref[0]
        for d in range(1, N_DEV):
            acc = acc + all_ref[d]
        o_ref[...] = acc

    return pl.pallas_call(
        body, name="small_allreduce",
        in_specs=[pl.BlockSpec(memory_space=pltpu.VMEM)], out_specs=pl.BlockSpec(memory_space=pltpu.VMEM),
        out_shape=SDS((r, LANES), f32),
        scratch_shapes=[pltpu.VMEM((N_DEV, r, LANES), f32), pltpu.SemaphoreType.DMA((7,)), pltpu.SemaphoreType.DMA((7,))],
    )(vec)


def _adamw_math(w, g, m, v):
    m = ADAM_B1 * m + (1.0 - ADAM_B1) * g
    v = ADAM_B2 * v + (1.0 - ADAM_B2) * (g * g)
    m_hat = m / (1.0 - ADAM_B1 ** ADAM_STEP)
    v_hat = v / (1.0 - ADAM_B2 ** ADAM_STEP)
    delta = -ADAM_LR * (m_hat / (jnp.sqrt(v_hat) + ADAM_EPS) + ADAM_WD * w)
    return delta, m, v


def adamw_big(w, g, m, v):
    _, r, cdim = w.shape
    tr = min(r, 256)

    def body(w_ref, g_ref, m_ref, v_ref, d_ref, mo_ref, vo_ref):
        d, mm, vv = _adamw_math(w_ref[...], g_ref[...], m_ref[...], v_ref[...])
        d_ref[...] = d
        mo_ref[...] = mm
        vo_ref[...] = vv

    blk = pl.BlockSpec((1, tr, cdim), lambda l, i: (l, i, 0))
    return pl.pallas_call(
        body, name="adamw_big", grid=(DEPTH, r // tr), in_specs=[blk] * 4, out_specs=[blk] * 3,
        out_shape=[SDS(w.shape, f32)] * 3, compiler_params=_params(("parallel", "parallel")),
    )(w, g, m, v)


def adamw_small(ws, gs, ms, vs):
    n = len(ws)

    def body(*refs):
        w_r, g_r, m_r, v_r = refs[:n], refs[n:2 * n], refs[2 * n:3 * n], refs[3 * n:4 * n]
        d_o, m_o, v_o = refs[4 * n:5 * n], refs[5 * n:6 * n], refs[6 * n:7 * n]
        for k in range(n):
            d, mm, vv = _adamw_math(w_r[k][...], g_r[k][...], m_r[k][...], v_r[k][...])
            d_o[k][...] = d
            m_o[k][...] = mm
            v_o[k][...] = vv

    vm = pl.BlockSpec(memory_space=pltpu.VMEM)
    shapes = [SDS(a.shape, f32) for a in ws]
    outs = pl.pallas_call(
        body, name="adamw_small", in_specs=[vm] * (4 * n), out_specs=[vm] * (3 * n), out_shape=shapes * 3,
    )(*ws, *gs, *ms, *vs)
    return outs[:n], outs[n:2 * n], outs[2 * n:]


_BIG = ("w_in", "w_out", "w_up", "w_down")
_WEIGHTS = ("norm1", "w_in", "attn_sinks", "conv_dw_w", "conv_dw_b", "conv_ln_g", "conv_ln_b", "lru_conv_w", "lru_conv_b",
            "lru_wa", "lru_ba", "lru_wx", "lru_bx", "lru_lambda", "mix_norm", "w_out", "norm2", "w_up", "w_down", "final_norm")
_SMALL = tuple(n for n in _WEIGHTS if n not in _BIG)
_SMALL_FULL_SHAPE = dict(
    norm1=(DEPTH, D_MODEL), attn_sinks=(DEPTH, N_HEADS), conv_dw_w=(DEPTH, CONV_K, CONV_W), conv_dw_b=(DEPTH, CONV_W),
    conv_ln_g=(DEPTH, CONV_W), conv_ln_b=(DEPTH, CONV_W), lru_conv_w=(DEPTH, LRU_K, LRU_W), lru_conv_b=(DEPTH, LRU_W),
    lru_wa=(DEPTH, LRU_HEADS, 64, 64), lru_ba=(DEPTH, LRU_HEADS, 64), lru_wx=(DEPTH, LRU_HEADS, 64, 64),
    lru_bx=(DEPTH, LRU_HEADS, 64), lru_lambda=(DEPTH, LRU_W), mix_norm=(DEPTH, D_MODEL), norm2=(DEPTH, D_MODEL),
    final_norm=(D_MODEL,))
_CHANNEL_SHARDED = ("conv_dw_w", "lru_conv_w")


def _pad_lanes(n):
    return -(-n // LANES) * LANES


def _pack(named):
    flat = []
    for a in named:
        a = a.reshape(-1)
        flat.append(jnp.pad(a, (0, _pad_lanes(a.shape[0]) - a.shape[0])))
    v = jnp.concatenate(flat)
    rows = -(-v.shape[0] // (8 * LANES)) * 8
    return jnp.pad(v, (0, rows * LANES - v.shape[0])).reshape(rows, LANES)


def _unpack(vec, shapes):
    flat = vec.reshape(-1)
    out, off = [], 0
    for shp in shapes:
        n = math.prod(shp)
        out.append(flat[off:off + n].reshape(shp))
        off += _pad_lanes(n)
    return out


def _as2d(a):
    return a.reshape(-1, a.shape[-1]) if a.ndim > 1 else a.reshape(1, -1)


def kernel(x, norm1, w_in, attn_sinks, conv_dw_w, conv_dw_b, conv_ln_g, conv_ln_b, lru_conv_w, lru_conv_b, lru_wa, lru_ba, lru_wx, lru_bx, lru_lambda, mix_norm, w_out, norm2, w_up, w_down, final_norm, loss_target, m_norm1, m_w_in, m_attn_sinks, m_conv_dw_w, m_conv_dw_b, m_conv_ln_g, m_conv_ln_b, m_lru_conv_w, m_lru_conv_b, m_lru_wa, m_lru_ba, m_lru_wx, m_lru_bx, m_lru_lambda, m_mix_norm, m_w_out, m_norm2, m_w_up, m_w_down, m_final_norm, v_norm1, v_w_in, v_attn_sinks, v_conv_dw_w, v_conv_dw_b, v_conv_ln_g, v_conv_ln_b, v_lru_conv_w, v_lru_conv_b, v_lru_wa, v_lru_ba, v_lru_wx, v_lru_bx, v_lru_lambda, v_mix_norm, v_w_out, v_norm2, v_w_up, v_w_down, v_final_norm):
    wts = dict(norm1=norm1, w_in=w_in, attn_sinks=attn_sinks, conv_dw_w=conv_dw_w, conv_dw_b=conv_dw_b, conv_ln_g=conv_ln_g,
               conv_ln_b=conv_ln_b, lru_conv_w=lru_conv_w, lru_conv_b=lru_conv_b, lru_wa=lru_wa, lru_ba=lru_ba, lru_wx=lru_wx,
               lru_bx=lru_bx, lru_lambda=lru_lambda, mix_norm=mix_norm, w_out=w_out, norm2=norm2, w_up=w_up, w_down=w_down,
               final_norm=final_norm)
    mom = dict(norm1=m_norm1, w_in=m_w_in, attn_sinks=m_attn_sinks, conv_dw_w=m_conv_dw_w, conv_dw_b=m_conv_dw_b,
               conv_ln_g=m_conv_ln_g, conv_ln_b=m_conv_ln_b, lru_conv_w=m_lru_conv_w, lru_conv_b=m_lru_conv_b, lru_wa=m_lru_wa,
               lru_ba=m_lru_ba, lru_wx=m_lru_wx, lru_bx=m_lru_bx, lru_lambda=m_lru_lambda, mix_norm=m_mix_norm, w_out=m_w_out,
               norm2=m_norm2, w_up=m_w_up, w_down=m_w_down, final_norm=m_final_norm)
    var = dict(norm1=v_norm1, w_in=v_w_in, attn_sinks=v_attn_sinks, conv_dw_w=v_conv_dw_w, conv_dw_b=v_conv_dw_b,
               conv_ln_g=v_conv_ln_g, conv_ln_b=v_conv_ln_b, lru_conv_w=v_lru_conv_w, lru_conv_b=v_lru_conv_b, lru_wa=v_lru_wa,
               lru_ba=v_lru_ba, lru_wx=v_lru_wx, lru_bx=v_lru_bx, lru_lambda=v_lru_lambda, mix_norm=v_mix_norm, w_out=v_w_out,
               norm2=v_norm2, w_up=v_w_up, w_down=v_w_down, final_norm=v_final_norm)

    c_idx = lax.axis_index("c").astype(jnp.int32)
    s_idx = (2 * lax.axis_index("x") + lax.axis_index("y")).astype(jnp.int32)
    idx = jnp.stack([c_idx, s_idx])

    g_in, g_out, g_up, g_dn, g_cw, g_lw = gather_weights(
        [place_shard(w_in, idx, bf16), place_shard(w_out, idx, bf16), place_shard(w_up, idx, bf16),
         place_shard(w_down, idx, bf16), place_shard(conv_dw_w, idx, f32), place_shard(lru_conv_w, idx, f32)])
    big = [dict(w_in=g_in[l].transpose(1, 0, 2).reshape(D_MODEL, IN_W), w_out=g_out[l].reshape(D_MODEL, D_MODEL),
                w_up=g_up[l], w_dn=g_dn[l]) for l in range(DEPTH)]
    sp = {n: wts[n] for n in _SMALL}
    sp["conv_dw_w"] = g_cw.transpose(0, 2, 1, 3).reshape(DEPTH, CONV_K, CONV_W)
    sp["lru_conv_w"] = g_lw.transpose(0, 2, 1, 3).reshape(DEPTH, LRU_K, LRU_W)

    loss_blk, grad_x, big_g, small_g, d_gf = local_step(x[0], loss_target[0], big, sp)

    parts = [[big_g[l]["w_in"].reshape(D_MODEL, N_SHARD, IN_W // N_SHARD).transpose(1, 0, 2),
              big_g[l]["w_out"].reshape(N_SHARD, D_MODEL // N_SHARD, D_MODEL), big_g[l]["w_up"], big_g[l]["w_dn"]]
             for l in range(DEPTH)]
    recv = pair_exchange(parts)
    sums = [chip_sum(parts[0][w], parts[1][w], recv[w], idx) for w in range(4)]
    got = shard_exchange([s16 for _, s16 in sums])
    tots = [shard_sum(sums[w][0], got[w], idx) for w in range(4)]
    grads_big = {n: a.reshape(wts[n].shape) for n, a in zip(_BIG, halves_exchange(tots))}

    stacked = [jnp.stack([small_g[l][n] for l in range(DEPTH)]) for n in _SMALL if n != "final_norm"]
    packed = _pack(stacked + [d_gf, loss_blk[0, 0:1]])
    summed = small_allreduce(packed)
    names = [n for n in _SMALL if n != "final_norm"] + ["final_norm"]
    unpacked = _unpack(summed, [_SMALL_FULL_SHAPE[n] for n in names] + [(1,)])
    loss = unpacked[-1][0]
    grads = dict(zip(names, unpacked[:-1]))
    for n in _CHANNEL_SHARDED:
        width = wts[n].shape[-1]
        grads[n] = lax.dynamic_slice_in_dim(grads[n], s_idx * width, width, axis=2)
    grads.update(grads_big)

    delta, new_m, new_v = {}, {}, {}
    for n in _BIG:
        delta[n], new_m[n], new_v[n] = adamw_big(wts[n], grads[n], mom[n], var[n])
    sm = list(_SMALL)
    d_s, m_s, v_s = adamw_small([_as2d(wts[n]) for n in sm], [_as2d(grads[n]) for n in sm],
                                [_as2d(mom[n]) for n in sm], [_as2d(var[n]) for n in sm])
    for k, n in enumerate(sm):
        delta[n], new_m[n], new_v[n] = (a.reshape(wts[n].shape) for a in (d_s[k], m_s[k], v_s[k]))

    return (loss, grad_x[None], *[grads[n] for n in _WEIGHTS], *[delta[n] for n in _WEIGHTS],
            *[new_m[n] for n in _WEIGHTS], *[new_v[n] for n in _WEIGHTS])
```

```python
import functools
import math

import jax
import jax.numpy as jnp
from jax import lax
from jax.experimental import pallas as pl
from jax.experimental.pallas import tpu as pltpu

f32 = jnp.float32
bf16 = jnp.bfloat16
SDS = jax.ShapeDtypeStruct

D_MODEL = 1024
DEPTH = 2
ATTN_W = 512
KV_W = 128
HEAD_DIM = 64
N_HEADS = 8
BLOCK = 128
CONV_W = 256
CONV_K = 31
LRU_W = 256
LRU_K = 4
LRU_HEADS = 4
LRU_C = 8.0
IN_W = 1792
D_FF = 4096
N_SHARD = 4
FF_CHUNK = D_FF // N_SHARD
RMS_EPS = 1e-6
LN_EPS = 1e-5
MASK_VALUE = -1e30
HALO = 32
LANES = 128
VMEM_LIMIT = 56 * 1024 * 1024

ADAM_LR = 0.001
ADAM_B1 = 0.9
ADAM_B2 = 0.999
ADAM_EPS = 1e-08
ADAM_WD = 0.01
ADAM_STEP = 10

MESH = pl.DeviceIdType.MESH


def _dot(a, b):
    return jnp.dot(a, b, preferred_element_type=f32)


def _dot_nt(a, b):
    return lax.dot_general(a, b, (((1,), (1,)), ((), ())), preferred_element_type=f32)


def _dot_tn(a, b):
    return lax.dot_general(a, b, (((0,), (0,)), ((), ())), preferred_element_type=f32)


def _rms_fwd(x, g):
    r = lax.rsqrt(jnp.mean(x * x, axis=-1, keepdims=True) + RMS_EPS)
    return x * r * g, r


def _rms_bwd(dy, x, r, g):
    t = dy * g
    dx = r * t - x * (r * r * r) * jnp.mean(t * x, axis=-1, keepdims=True)
    dg = jnp.sum(dy * x * r, axis=0, keepdims=True)
    return dx, dg


def _sigmoid(x):
    return jax.nn.sigmoid(x)


_GELU_K = math.sqrt(2.0 / math.pi)


def _gelu(x):
    t = jnp.tanh(_GELU_K * (x + 0.044715 * x * x * x))
    return 0.5 * x * (1.0 + t), t


def _gelu_grad(x, t):
    return 0.5 * (1.0 + t) + 0.5 * x * (1.0 - t * t) * _GELU_K * (1.0 + 3.0 * 0.044715 * x * x)


def _log1p(x):
    return jnp.where(x < 1e-4, x - 0.5 * x * x, jnp.log(1.0 + x))


def _softplus(x):
    return jnp.maximum(x, 0.0) + _log1p(jnp.exp(-jnp.abs(x)))


def _neg_expm1(x):
    series = -x * (1.0 + 0.5 * x * (1.0 + x * (1.0 / 3.0) * (1.0 + 0.25 * x)))
    return jnp.where(x > -0.01, series, 1.0 - jnp.exp(x))


def _sublane_rolls(x, count, forward):
    n = x.shape[0]
    return [x if b == 0 else pltpu.roll(x, b if forward else n - b, 0) for b in range(count)]


def _conv_taps(xpad, w, k_width):
    t_rows = xpad.shape[0] - HALO
    rolled = _sublane_rolls(xpad, min(k_width, 8), forward=True)
    acc = None
    for k in range(k_width):
        hi, lo = divmod((k_width - 1) - k, 8)
        term = rolled[lo][HALO - 8 * hi:HALO - 8 * hi + t_rows] * w[k:k + 1, :]
        acc = term if acc is None else acc + term
    return acc


def _conv_taps_bwd(dpad, upad, w, k_width, t_rows):
    n_lo = min(k_width, 8)
    d_rolled = _sublane_rolls(dpad, n_lo, forward=False)
    u_rolled = _sublane_rolls(upad, n_lo, forward=True)
    d_in = None
    dw_rows = []
    d_out = dpad[:t_rows]
    for k in range(k_width):
        hi, lo = divmod((k_width - 1) - k, 8)
        term = d_rolled[lo][8 * hi:8 * hi + t_rows] * w[k:k + 1, :]
        d_in = term if d_in is None else d_in + term
        us = u_rolled[lo][HALO - 8 * hi:HALO - 8 * hi + t_rows]
        dw_rows.append(jnp.sum(d_out * us, axis=0, keepdims=True))
    return d_in, dw_rows


def _scan_fwd(a, b):
    t_rows = a.shape[0]
    row = lax.broadcasted_iota(jnp.int32, a.shape, 0)
    d = 1
    while d < t_rows:
        a_sh = jnp.where(row < d, 1.0, pltpu.roll(a, d, 0))
        b_sh = jnp.where(row < d, 0.0, pltpu.roll(b, d, 0))
        b = a * b_sh + b
        a = a * a_sh
        d *= 2
    return a, b


def _scan_bwd(a, b):
    t_rows = a.shape[0]
    row = lax.broadcasted_iota(jnp.int32, a.shape, 0)
    d = 1
    while d < t_rows:
        a_sh = jnp.where(row >= t_rows - d, 1.0, pltpu.roll(a, t_rows - d, 0))
        b_sh = jnp.where(row >= t_rows - d, 0.0, pltpu.roll(b, t_rows - d, 0))
        b = b + a * b_sh
        a = a * a_sh
        d *= 2
    return b


def _full(shape):
    nd = len(shape)
    return pl.BlockSpec(shape, lambda *_: (0,) * nd)


def _params(sem, vmem=None):
    return pltpu.CompilerParams(dimension_semantics=sem, vmem_limit_bytes=vmem)


def _tile(s):
    return min(512, s)


def inproj_fwd(h, g1, w_in):
    s = h.shape[0]
    tm = _tile(s)

    def body(h_ref, g_ref, w_ref, z_ref):
        hn, _ = _rms_fwd(h_ref[...], g_ref[...])
        z_ref[...] = _dot(hn.astype(bf16), w_ref[...]).astype(bf16)

    return pl.pallas_call(
        body, name="inproj_fwd", grid=(s // tm,),
        in_specs=[pl.BlockSpec((tm, D_MODEL), lambda i: (i, 0)), _full((1, D_MODEL)), _full((D_MODEL, IN_W))],
        out_specs=pl.BlockSpec((tm, IN_W), lambda i: (i, 0)),
        out_shape=SDS((s, IN_W), bf16),
        compiler_params=_params(("parallel",), VMEM_LIMIT),
    )(h, g1, w_in)


ATT_ROWS = N_HEADS * BLOCK


def _attn_bias():
    qi = jnp.arange(BLOCK)[:, None]
    col = jnp.arange(2 * BLOCK)[None, :]
    band = (col > qi) & (col <= qi + BLOCK)
    first = band & (col >= BLOCK)
    return jnp.where(jnp.stack([first, band]), 0.0, MASK_VALUE).astype(f32)


def _attn_band(kvc, kvp):
    kb = jnp.concatenate([kvp[:, :KV_W], kvc[:, :KV_W]], axis=0)
    vb = jnp.concatenate([kvp[:, KV_W:], kvc[:, KV_W:]], axis=0)
    lane = lax.broadcasted_iota(jnp.int32, kb.shape, 1)
    kb_sw = pltpu.roll(kb, HEAD_DIM, 1)
    vb_sw = pltpu.roll(vb, HEAD_DIM, 1)
    kx = [jnp.where(lane < HEAD_DIM, kb, kb_sw), jnp.where(lane >= HEAD_DIM, kb, kb_sw)]
    vx = [jnp.where(lane < HEAD_DIM, vb, vb_sw), jnp.where(lane >= HEAD_DIM, vb, vb_sw)]
    return kx, vx


def _stack_heads(x, mlo):
    zero = jnp.zeros((BLOCK, LANES), x.dtype)
    out = []
    for hk in range(2):
        parts = []
        for j in (2 * hk, 2 * hk + 1):
            xj = x[:, j * LANES:(j + 1) * LANES]
            parts += [jnp.where(mlo, xj, zero), jnp.where(mlo, zero, xj)]
        out.append(jnp.concatenate(parts, axis=0))
    return out


def _unstack_heads(y, mlo):
    cols = []
    for hk in range(2):
        for t in range(2):
            base = 2 * t * BLOCK
            cols.append(jnp.where(mlo, y[hk][base:base + BLOCK], y[hk][base + BLOCK:base + 2 * BLOCK]))
    return jnp.concatenate(cols, axis=1)


def _attn_probs(q4, kx, bias, sink_col):
    s = jnp.concatenate([_dot_nt(q4[0], kx[0]), _dot_nt(q4[1], kx[1])], axis=0)
    s = (s.reshape(N_HEADS, BLOCK, 2 * BLOCK) + bias[None]).reshape(ATT_ROWS, 2 * BLOCK)
    m = jnp.maximum(jnp.max(s, axis=-1, keepdims=True), sink_col)
    p = jnp.exp(s - m)
    e_sink = jnp.exp(sink_col - m)
    inv = 1.0 / (jnp.sum(p, axis=-1, keepdims=True) + e_sink)
    return p * inv, e_sink * inv


def attn_fwd(z, sink_col, bias, g_a, pieces=None):
    s = z.shape[0]
    nb = s // BLOCK

    def body(q_ref, kvc_ref, kvp_ref, sk_ref, b_ref, g_ref, o_ref, y_ref):
        n = pl.program_id(0)
        kx, vx = _attn_band(kvc_ref[...], kvp_ref[...])
        mlo = lax.broadcasted_iota(jnp.int32, (BLOCK, LANES), 1) < HEAD_DIM
        q4 = _stack_heads(q_ref[...] * (HEAD_DIM ** -0.5), mlo)
        pr, _ = _attn_probs(q4, kx, b_ref[jnp.minimum(n, 1)], sk_ref[...])
        prb = pr.astype(bf16)
        half = ATT_ROWS // 2
        o = _unstack_heads([_dot(prb[:half], vx[0]), _dot(prb[half:], vx[1])], mlo)
        o_ref[...] = o.astype(bf16)
        y, _ = _rms_fwd(o, g_ref[...])
        y_ref[...] = y.astype(bf16)

    return _pcall(
        body, pieces, name="attn_fwd", grid=(nb,),
        in_specs=[pl.BlockSpec((BLOCK, ATTN_W), lambda n: (n, 0)),
                  pl.BlockSpec((BLOCK, 2 * KV_W), lambda n: (n, 2)),
                  pl.BlockSpec((BLOCK, 2 * KV_W), lambda n: (jnp.maximum(n - 1, 0), 2)),
                  _full((ATT_ROWS, 1)), _full((2, BLOCK, 2 * BLOCK)), _full((1, ATTN_W))],
        out_specs=[pl.BlockSpec((BLOCK, ATTN_W), lambda n: (n, 0)), pl.BlockSpec((BLOCK, ATTN_W), lambda n: (n, 0))],
        out_shape=[SDS((s, ATTN_W), bf16), SDS((s, ATTN_W), bf16)],
        operands=[z, z, z, sink_col, bias, g_a], sem=("parallel",))


def _lru_gates(xc, wa, ba, wx, bx, lam):
    xcb = xc.astype(bf16)
    r = _sigmoid(_dot(xcb, wa) + ba)
    ig = _sigmoid(_dot(xcb, wx) + bx)
    sp = _softplus(-lam)
    la = (-LRU_C * r) * sp
    a = jnp.exp(la)
    mult = jnp.sqrt(_neg_expm1(2.0 * la))
    return r, ig, sp, la, a, mult


def branch_fwd(z, p, pieces=None):
    s = z.shape[0]
    tm = _tile(s)
    hb = tm // HALO

    def body(cv_ref, cg_ref, rx_ref, rg_ref, cvh_ref, cgh_ref, rxh_ref,
             cw_ref, cb_ref, lng_ref, lnb_ref, lw_ref, lb_ref, wa_ref, ba_ref, wx_ref, bx_ref, lam_ref, gc_ref, gl_ref,
             conv_ref, hst_ref, nc_ref, nl_ref, carry_ref):
        i = pl.program_id(0)
        first = i == 0

        @pl.when(first)
        def _():
            carry_ref[...] = jnp.zeros_like(carry_ref)

        cval = cv_ref[...].astype(f32)
        u = cval * _sigmoid(cg_ref[...].astype(f32))
        hu = jnp.where(first, 0.0, cvh_ref[...].astype(f32) * _sigmoid(cgh_ref[...].astype(f32)))
        conv = _conv_taps(jnp.concatenate([hu, u], axis=0), cw_ref[...], CONV_K) + cb_ref[...]
        conv_ref[...] = conv
        mu = jnp.mean(conv, axis=-1, keepdims=True)
        xm = conv - mu
        rstd = lax.rsqrt(jnp.mean(xm * xm, axis=-1, keepdims=True) + LN_EPS)
        ln = xm * rstd * lng_ref[...] + lnb_ref[...]
        yc = ln * _sigmoid(ln)
        nc, _ = _rms_fwd(yc, gc_ref[...])
        nc_ref[...] = nc.astype(bf16)

        rx = rx_ref[...].astype(f32)
        hrx = jnp.where(first, 0.0, rxh_ref[...].astype(f32))
        xc = _conv_taps(jnp.concatenate([hrx, rx], axis=0), lw_ref[...], LRU_K) + lb_ref[...]
        r, ig, sp, la, a, mult = _lru_gates(xc, wa_ref[...], ba_ref[...], wx_ref[...], bx_ref[...], lam_ref[...])
        gx = mult * (ig * xc)
        a_cum, h_loc = _scan_fwd(a, gx)
        hs = a_cum * carry_ref[0:1, :] + h_loc
        carry_ref[...] = jnp.broadcast_to(hs[tm - 1:tm, :], carry_ref.shape)
        hst_ref[...] = hs
        gl, _ = _gelu(rg_ref[...].astype(f32))
        nl, _ = _rms_fwd(hs * gl, gl_ref[...])
        nl_ref[...] = nl.astype(bf16)

    def col(c):
        return pl.BlockSpec((tm, CONV_W), lambda i: (i, c))

    def halo(c):
        return pl.BlockSpec((HALO, CONV_W), lambda i: (jnp.maximum(i * hb - 1, 0), c))

    small = [p["cw"], p["cb"], p["lng"], p["lnb"], p["lw"], p["lb"], p["wa"], p["ba"], p["wx"], p["bx"], p["lam"],
             p["gc"], p["gl"]]
    row = pl.BlockSpec((tm, CONV_W), lambda i: (i, 0))
    return _pcall(
        body, pieces, name="branch_fwd", grid=(s // tm,),
        in_specs=[col(3), col(4), col(5), col(6), halo(3), halo(4), halo(5)] + [_full(a.shape) for a in small],
        out_specs=[row, row, row, row],
        out_shape=[SDS((s, CONV_W), f32), SDS((s, LRU_W), f32), SDS((s, CONV_W), bf16), SDS((s, LRU_W), bf16)],
        scratch_shapes=[pltpu.VMEM((8, LRU_W), f32)],
        operands=[z, z, z, z, z, z, z, *small], sem=("arbitrary",))


def outproj_fwd(ya, yc, yl, h, w_out, g2):
    s = h.shape[0]
    tm = _tile(s)

    def body(ya_ref, yc_ref, yl_ref, h_ref, w_ref, g_ref, h1_ref, hn_ref):
        y = jnp.concatenate([ya_ref[...], yc_ref[...], yl_ref[...]], axis=1)
        h1 = h_ref[...] + _dot(y, w_ref[...])
        h1_ref[...] = h1
        hn, _ = _rms_fwd(h1, g_ref[...])
        hn_ref[...] = hn.astype(bf16)

    return pl.pallas_call(
        body, name="outproj_fwd", grid=(s // tm,),
        in_specs=[pl.BlockSpec((tm, ATTN_W), lambda i: (i, 0)), pl.BlockSpec((tm, CONV_W), lambda i: (i, 0)),
                  pl.BlockSpec((tm, LRU_W), lambda i: (i, 0)), pl.BlockSpec((tm, D_MODEL), lambda i: (i, 0)),
                  _full((D_MODEL, D_MODEL)), _full((1, D_MODEL))],
        out_specs=[pl.BlockSpec((tm, D_MODEL), lambda i: (i, 0)), pl.BlockSpec((tm, D_MODEL), lambda i: (i, 0))],
        out_shape=[SDS((s, D_MODEL), f32), SDS((s, D_MODEL), bf16)],
        compiler_params=_params(("parallel",), VMEM_LIMIT),
    )(ya, yc, yl, h, w_out, g2)


def mlp_fwd(hn2, h1, w_up, w_dn, pieces=None):
    s = h1.shape[0]
    tm = min(256, s)

    def body(x_ref, h_ref, wu_ref, wd_ref, up_ref, h2_ref):
        x = x_ref[...]
        acc = h_ref[...]
        for c in range(N_SHARD):
            u = _dot(x, wu_ref[c])
            up_ref[:, c * FF_CHUNK:(c + 1) * FF_CHUNK] = u.astype(bf16)
            act = jnp.square(jnp.maximum(u, 0.0)).astype(bf16)
            acc = acc + _dot(act, wd_ref[c])
        h2_ref[...] = acc

    return _pcall(
        body, pieces, name="mlp_fwd", grid=(s // tm,),
        in_specs=[pl.BlockSpec((tm, D_MODEL), lambda i: (i, 0)), pl.BlockSpec((tm, D_MODEL), lambda i: (i, 0)),
                  _full((N_SHARD, D_MODEL, FF_CHUNK)), _full((N_SHARD, FF_CHUNK, D_MODEL))],
        out_specs=[pl.BlockSpec((tm, D_FF), lambda i: (i, 0)), pl.BlockSpec((tm, D_MODEL), lambda i: (i, 0))],
        out_shape=[SDS((s, D_FF), bf16), SDS((s, D_MODEL), f32)],
        operands=[hn2, h1, w_up, w_dn], sem=("parallel",), vmem=VMEM_LIMIT)


def final_loss(h, tgt, gf):
    s = h.shape[0]
    tm = _tile(s)

    def body(h_ref, t_ref, g_ref, dh_ref, loss_ref, dg_ref):
        i = pl.program_id(0)

        @pl.when(i == 0)
        def _():
            loss_ref[...] = jnp.zeros_like(loss_ref)
            dg_ref[...] = jnp.zeros_like(dg_ref)

        x = h_ref[...]
        g = g_ref[...]
        y, r = _rms_fwd(x, g)
        err = y - t_ref[...]
        part = 0.5 * jnp.sum(jnp.mean(err * err, axis=-1, keepdims=True), axis=0, keepdims=True)
        loss_ref[...] += jnp.broadcast_to(part, loss_ref.shape)
        dx, dg = _rms_bwd(err * (1.0 / D_MODEL), x, r, g)
        dh_ref[...] = dx
        dg_ref[...] += dg

    return pl.pallas_call(
        body, name="final_loss", grid=(s // tm,),
        in_specs=[pl.BlockSpec((tm, D_MODEL), lambda i: (i, 0)), pl.BlockSpec((tm, D_MODEL), lambda i: (i, 0)),
                  _full((1, D_MODEL))],
        out_specs=[pl.BlockSpec((tm, D_MODEL), lambda i: (i, 0)), _full((8, LANES)), _full((1, D_MODEL))],
        out_shape=[SDS((s, D_MODEL), f32), SDS((8, LANES), f32), SDS((1, D_MODEL), f32)],
        compiler_params=_params(("arbitrary",)),
    )(h, tgt, gf)


def mlp_bwd_act(dh, up, h1, g2, w_up, w_dn, pieces=None):
    s = dh.shape[0]
    tm = min(256, s)

    def body(dh_ref, up_ref, h1_ref, g_ref, wu_ref, wd_ref, dup_ref, dh1_ref, dg_ref):
        i = pl.program_id(0)

        @pl.when(i == 0)
        def _():
            dg_ref[...] = jnp.zeros_like(dg_ref)

        dh = dh_ref[...]
        dhb = dh.astype(bf16)
        d_hn = jnp.zeros((tm, D_MODEL), f32)
        for c in range(N_SHARD):
            d_act = _dot_nt(dhb, wd_ref[c])
            u = up_ref[:, c * FF_CHUNK:(c + 1) * FF_CHUNK].astype(f32)
            d_u = (d_act * (2.0 * jnp.maximum(u, 0.0))).astype(bf16)
            dup_ref[:, c * FF_CHUNK:(c + 1) * FF_CHUNK] = d_u
            d_hn = d_hn + _dot_nt(d_u, wu_ref[c])
        x = h1_ref[...]
        g = g_ref[...]
        _, r = _rms_fwd(x, g)
        dx, dg = _rms_bwd(d_hn, x, r, g)
        dh1_ref[...] = dh + dx
        dg_ref[...] += dg

    return _pcall(
        body, pieces, name="mlp_bwd_act", grid=(s // tm,),
        in_specs=[pl.BlockSpec((tm, D_MODEL), lambda i: (i, 0)), pl.BlockSpec((tm, D_FF), lambda i: (i, 0)),
                  pl.BlockSpec((tm, D_MODEL), lambda i: (i, 0)), _full((1, D_MODEL)),
                  _full((N_SHARD, D_MODEL, FF_CHUNK)), _full((N_SHARD, FF_CHUNK, D_MODEL))],
        out_specs=[pl.BlockSpec((tm, D_FF), lambda i: (i, 0)), pl.BlockSpec((tm, D_MODEL), lambda i: (i, 0)),
                   _full((1, D_MODEL))],
        out_shape=[SDS((s, D_FF), bf16), SDS((s, D_MODEL), f32), SDS((1, D_MODEL), f32)],
        operands=[dh, up, h1, g2, w_up, w_dn], sem=("arbitrary",), vmem=VMEM_LIMIT)


def mlp_bwd_w(hn2, d_up, up, dh, pieces=None):
    s = dh.shape[0]
    tk = _tile(s)

    def body(x_ref, du_ref, up_ref, dh_ref, dwu_ref, dwd_ref):
        k = pl.program_id(1)

        @pl.when(k == 0)
        def _():
            dwu_ref[...] = jnp.zeros_like(dwu_ref)
            dwd_ref[...] = jnp.zeros_like(dwd_ref)

        dwu_ref[0] += _dot_tn(x_ref[...], du_ref[...])
        act = jnp.square(jnp.maximum(up_ref[...].astype(f32), 0.0)).astype(bf16)
        dwd_ref[0] += _dot_tn(act, dh_ref[...].astype(bf16))

    return _pcall(
        body, pieces, name="mlp_bwd_w", grid=(N_SHARD, s // tk),
        in_specs=[pl.BlockSpec((tk, D_MODEL), lambda c, k: (k, 0)), pl.BlockSpec((tk, FF_CHUNK), lambda c, k: (k, c)),
                  pl.BlockSpec((tk, FF_CHUNK), lambda c, k: (k, c)), pl.BlockSpec((tk, D_MODEL), lambda c, k: (k, 0))],
        out_specs=[pl.BlockSpec((1, D_MODEL, FF_CHUNK), lambda c, k: (c, 0, 0)),
                   pl.BlockSpec((1, FF_CHUNK, D_MODEL), lambda c, k: (c, 0, 0))],
        out_shape=[SDS((N_SHARD, D_MODEL, FF_CHUNK), f32), SDS((N_SHARD, FF_CHUNK, D_MODEL), f32)],
        operands=[hn2, d_up, up, dh], sem=("parallel", "arbitrary"), vmem=VMEM_LIMIT)


def outproj_bwd(dh1, ya, yc, yl, w_out):
    s = dh1.shape[0]
    tm = _tile(s)

    def body(dh_ref, ya_ref, yc_ref, yl_ref, w_ref, dy_ref, dw_ref):
        i = pl.program_id(0)

        @pl.when(i == 0)
        def _():
            dw_ref[...] = jnp.zeros_like(dw_ref)

        dhb = dh_ref[...].astype(bf16)
        dy_ref[...] = _dot_nt(dhb, w_ref[...])
        y = jnp.concatenate([ya_ref[...], yc_ref[...], yl_ref[...]], axis=1)
        dw_ref[...] += _dot_tn(y, dhb)

    return pl.pallas_call(
        body, name="outproj_bwd", grid=(s // tm,),
        in_specs=[pl.BlockSpec((tm, D_MODEL), lambda i: (i, 0)), pl.BlockSpec((tm, ATTN_W), lambda i: (i, 0)),
                  pl.BlockSpec((tm, CONV_W), lambda i: (i, 0)), pl.BlockSpec((tm, LRU_W), lambda i: (i, 0)),
                  _full((D_MODEL, D_MODEL))],
        out_specs=[pl.BlockSpec((tm, D_MODEL), lambda i: (i, 0)), _full((D_MODEL, D_MODEL))],
        out_shape=[SDS((s, D_MODEL), f32), SDS((D_MODEL, D_MODEL), f32)],
        compiler_params=_params(("arbitrary",), VMEM_LIMIT),
    )(dh1, ya, yc, yl, w_out)


def attn_bwd(z, o, dy, sink_col, bias, g_a, pieces=None):
    s = z.shape[0]
    nb = s // BLOCK

    def body(q_ref, kvc_ref, kvp_ref, o_ref, dy_ref, sk_ref, b_ref, g_ref, dq_ref, dkv_ref, dsk_ref, dg_ref,
             pend_ref, dsk_acc):
        n = pl.program_id(0)

        @pl.when(n == 0)
        def _():
            pend_ref[...] = jnp.zeros_like(pend_ref)
            dsk_acc[...] = jnp.zeros_like(dsk_acc)
            dg_ref[...] = jnp.zeros_like(dg_ref)

        @pl.when(n < nb)
        def _():
            kx, vx = _attn_band(kvc_ref[...], kvp_ref[...])
            mlo = lax.broadcasted_iota(jnp.int32, (BLOCK, LANES), 1) < HEAD_DIM
            scale = HEAD_DIM ** -0.5
            q4 = _stack_heads(q_ref[...] * scale, mlo)
            o_f = o_ref[...].astype(f32)
            g = g_ref[...]
            _, r = _rms_fwd(o_f, g)
            d_o, dg = _rms_bwd(dy_ref[...], o_f, r, g)
            dg_ref[...] += dg
            do4 = _stack_heads(d_o.astype(bf16), mlo)
            pr, p_sink = _attn_probs(q4, kx, b_ref[jnp.minimum(n, 1)], sk_ref[...])
            d_p = jnp.concatenate([_dot_nt(do4[0], vx[0]), _dot_nt(do4[1], vx[1])], axis=0)
            d_row = jnp.sum(pr * d_p, axis=-1, keepdims=True)
            d_s = (pr * (d_p - d_row)).astype(bf16)
            dsk_acc[...] -= p_sink * d_row
            prb = pr.astype(bf16)
            half = ATT_ROWS // 2
            dq4 = [_dot(d_s[:half], kx[0]), _dot(d_s[half:], kx[1])]
            dq_ref[...] = (_unstack_heads(dq4, mlo) * scale).astype(bf16)
            tk = [_dot_tn(d_s[:half], q4[0]), _dot_tn(d_s[half:], q4[1])]
            tv = [_dot_tn(prb[:half], do4[0]), _dot_tn(prb[half:], do4[1])]
            lane = lax.broadcasted_iota(jnp.int32, (2 * BLOCK, LANES), 1)
            fk = [t + pltpu.roll(t, HEAD_DIM, 1) for t in tk]
            fv = [t + pltpu.roll(t, HEAD_DIM, 1) for t in tv]
            band = jnp.concatenate([jnp.where(lane < HEAD_DIM, fk[0], fk[1]), jnp.where(lane < HEAD_DIM, fv[0], fv[1])], axis=1)
            dkv_ref[...] = (pend_ref[...] + band[:BLOCK]).astype(bf16)
            pend_ref[...] = band[BLOCK:]

        @pl.when(n == nb)
        def _():
            dkv_ref[...] = pend_ref[...].astype(bf16)
            for hh in range(N_HEADS):
                tot = jnp.sum(dsk_acc[hh * BLOCK:(hh + 1) * BLOCK, :], axis=0, keepdims=True)
                dsk_ref[hh:hh + 1, :] = jnp.broadcast_to(tot, (1, LANES))

    def cur(n):
        return jnp.minimum(n, nb - 1)

    return _pcall(
        body, pieces, name="attn_bwd", grid=(nb + 1,),
        in_specs=[pl.BlockSpec((BLOCK, ATTN_W), lambda n: (cur(n), 0)),
                  pl.BlockSpec((BLOCK, 2 * KV_W), lambda n: (cur(n), 2)),
                  pl.BlockSpec((BLOCK, 2 * KV_W), lambda n: (jnp.maximum(cur(n) - 1, 0), 2)),
                  pl.BlockSpec((BLOCK, ATTN_W), lambda n: (cur(n), 0)),
                  pl.BlockSpec((BLOCK, ATTN_W), lambda n: (cur(n), 0)),
                  _full((ATT_ROWS, 1)), _full((2, BLOCK, 2 * BLOCK)), _full((1, ATTN_W))],
        out_specs=[pl.BlockSpec((BLOCK, ATTN_W), lambda n: (cur(n), 0)),
                   pl.BlockSpec((BLOCK, 2 * KV_W), lambda n: (jnp.maximum(n - 1, 0), 0)),
                   _full((N_HEADS, LANES)), _full((1, ATTN_W))],
        out_shape=[SDS((s, ATTN_W), bf16), SDS((s, 2 * KV_W), bf16), SDS((N_HEADS, LANES), f32), SDS((1, ATTN_W), f32)],
        scratch_shapes=[pltpu.VMEM((BLOCK, 2 * KV_W), f32), pltpu.VMEM((ATT_ROWS, 1), f32)],
        operands=[z, z, z, o, dy, sink_col, bias, g_a], sem=("arbitrary",))


_V_GC, _V_LNG, _V_LNB, _V_CB, _V_GL, _V_BA, _V_BX, _V_LAM, _V_LB = range(9)
_V_ROWS = 16


def branch_bwd_a(z, conv, hst, dy, p, pieces=None):
    s = z.shape[0]
    tm = _tile(s)
    nt = s // tm
    hb = tm // HALO
    h8 = tm // 8

    def body(conv_ref, dyc_ref, dyl_ref, rx_ref, rxh_ref, rg_ref, hst_ref, hsth_ref,
             lng_ref, lnb_ref, lw_ref, lb_ref, wa_ref, ba_ref, wx_ref, bx_ref, lam_ref, gc_ref, gl_ref,
             dconv_ref, dxc_ref, drg_ref, vec_ref, dwa_ref, dwx_ref, carry_ref):
        i = pl.program_id(0)
        ti = nt - 1 - i

        @pl.when(i == 0)
        def _():
            carry_ref[...] = jnp.zeros_like(carry_ref)
            vec_ref[...] = jnp.zeros_like(vec_ref)
            dwa_ref[...] = jnp.zeros_like(dwa_ref)
            dwx_ref[...] = jnp.zeros_like(dwx_ref)

        conv = conv_ref[...]
        mu = jnp.mean(conv, axis=-1, keepdims=True)
        xm = conv - mu
        rstd = lax.rsqrt(jnp.mean(xm * xm, axis=-1, keepdims=True) + LN_EPS)
        xhat = xm * rstd
        lng = lng_ref[...]
        ln = xhat * lng + lnb_ref[...]
        sg = _sigmoid(ln)
        yc = ln * sg
        gc = gc_ref[...]
        _, rc = _rms_fwd(yc, gc)
        d_yc, d_gc = _rms_bwd(dyc_ref[...], yc, rc, gc)
        d_ln = d_yc * (sg * (1.0 + ln * (1.0 - sg)))
        d_xhat = d_ln * lng
        d_conv = rstd * (d_xhat - jnp.mean(d_xhat, axis=-1, keepdims=True)
                         - xhat * jnp.mean(d_xhat * xhat, axis=-1, keepdims=True))
        dconv_ref[...] = d_conv

        rx = rx_ref[...].astype(f32)
        hrx = jnp.where(ti == 0, 0.0, rxh_ref[...].astype(f32))
        xc = _conv_taps(jnp.concatenate([hrx, rx], axis=0), lw_ref[...], LRU_K) + lb_ref[...]
        wa = wa_ref[...]
        wx = wx_ref[...]
        lam = lam_ref[...]
        r, ig, sp, la, a, mult = _lru_gates(xc, wa, ba_ref[...], wx, bx_ref[...], lam)
        hs = hst_ref[...]
        row = lax.broadcasted_iota(jnp.int32, hs.shape, 0)
        h_before = jnp.where(ti == 0, 0.0, hsth_ref[7:8, :])
        h_prev = jnp.where(row == 0, h_before, pltpu.roll(hs, 1, 0))
        rg = rg_ref[...].astype(f32)
        gl, tg = _gelu(rg)
        out = hs * gl
        gmix = gl_ref[...]
        _, rl = _rms_fwd(out, gmix)
        d_out, d_gl = _rms_bwd(dyl_ref[...], out, rl, gmix)
        drg_ref[...] = (d_out * hs * _gelu_grad(rg, tg)).astype(bf16)
        d_h = d_out * gl
        last = row == tm - 1
        a_next = jnp.where(last, 0.0, pltpu.roll(a, tm - 1, 0))
        lmb = _scan_bwd(a_next, d_h + jnp.where(last, carry_ref[0:1, :], 0.0))
        carry_ref[...] = jnp.broadcast_to(a[0:1, :] * lmb[0:1, :], carry_ref.shape)
        d_a = lmb * h_prev
        d_mult = lmb * (ig * xc)
        d_ig = lmb * (mult * xc)
        d_la = d_a * a - d_mult * (a * a) / jnp.maximum(mult, 1e-30)
        d_pa = (d_la * (-LRU_C * sp)) * (r * (1.0 - r))
        d_px = d_ig * (ig * (1.0 - ig))
        d_pab = d_pa.astype(bf16)
        d_pxb = d_px.astype(bf16)
        d_xc = lmb * (mult * ig) + _dot_nt(d_pab, wa) + _dot_nt(d_pxb, wx)
        dxc_ref[...] = d_xc
        xcb = xc.astype(bf16)
        dwa_ref[...] += _dot_tn(xcb, d_pab)
        dwx_ref[...] += _dot_tn(xcb, d_pxb)
        d_lam = jnp.sum(d_la * (-LRU_C * r), axis=0, keepdims=True) * (-_sigmoid(-lam))

        def colsum(v):
            return jnp.sum(v, axis=0, keepdims=True)

        rows = [None] * _V_ROWS
        rows[_V_GC] = d_gc
        rows[_V_LNG] = colsum(d_ln * xhat)
        rows[_V_LNB] = colsum(d_ln)
        rows[_V_CB] = colsum(d_conv)
        rows[_V_GL] = d_gl
        rows[_V_BA] = colsum(d_pa)
        rows[_V_BX] = colsum(d_px)
        rows[_V_LAM] = d_lam
        rows[_V_LB] = colsum(d_xc)
        zero = jnp.zeros((1, CONV_W), f32)
        vec_ref[...] += jnp.concatenate([zero if v is None else v for v in rows], axis=0)

    def rev(c):
        return pl.BlockSpec((tm, CONV_W), lambda i: (nt - 1 - i, c))

    small = [p["lng"], p["lnb"], p["lw"], p["lb"], p["wa"], p["ba"], p["wx"], p["bx"], p["lam"], p["gc"], p["gl"]]
    return _pcall(
        body, pieces, name="branch_bwd_a", grid=(nt,),
        in_specs=[rev(0), rev(2), rev(3), rev(5),
                  pl.BlockSpec((HALO, CONV_W), lambda i: (jnp.maximum((nt - 1 - i) * hb - 1, 0), 5)),
                  rev(6), rev(0),
                  pl.BlockSpec((8, LRU_W), lambda i: (jnp.maximum((nt - 1 - i) * h8 - 1, 0), 0))]
                 + [_full(a.shape) for a in small],
        out_specs=[rev(0), rev(0), rev(0), _full((_V_ROWS, CONV_W)), _full((LRU_W, LRU_W)), _full((LRU_W, LRU_W))],
        out_shape=[SDS((s, CONV_W), f32), SDS((s, LRU_W), f32), SDS((s, LRU_W), bf16), SDS((_V_ROWS, CONV_W), f32),
                   SDS((LRU_W, LRU_W), f32), SDS((LRU_W, LRU_W), f32)],
        scratch_shapes=[pltpu.VMEM((8, LRU_W), f32)],
        operands=[conv, dy, dy, z, z, z, hst, hst, *small], sem=("arbitrary",))


def branch_bwd_b(z, d_conv, d_xc, p, pieces=None):
    s = z.shape[0]
    tm = _tile(s)
    nt = s // tm
    hb = tm // HALO

    def body(cv_ref, cg_ref, cvh_ref, cgh_ref, rx_ref, rxh_ref, dc_ref, dch_ref, dx_ref, dxh_ref, cw_ref, lw_ref,
             dzc_ref, dzr_ref, dcw_ref, dlw_ref):
        i = pl.program_id(0)

        @pl.when(i == 0)
        def _():
            dcw_ref[...] = jnp.zeros_like(dcw_ref)
            dlw_ref[...] = jnp.zeros_like(dlw_ref)

        first = i == 0
        last = i == nt - 1
        cval = cv_ref[...].astype(f32)
        sg = _sigmoid(cg_ref[...].astype(f32))
        u = cval * sg
        hu = jnp.where(first, 0.0, cvh_ref[...].astype(f32) * _sigmoid(cgh_ref[...].astype(f32)))
        dpad = jnp.concatenate([dc_ref[...], jnp.where(last, 0.0, dch_ref[...])], axis=0)
        d_u, dw_rows = _conv_taps_bwd(dpad, jnp.concatenate([hu, u], axis=0), cw_ref[...], CONV_K, tm)
        dcw_ref[...] += jnp.concatenate(dw_rows + [jnp.zeros((HALO - CONV_K, CONV_W), f32)], axis=0)
        dzc_ref[...] = jnp.concatenate([d_u * sg, d_u * cval * sg * (1.0 - sg)], axis=1).astype(bf16)

        rx = rx_ref[...].astype(f32)
        hrx = jnp.where(first, 0.0, rxh_ref[...].astype(f32))
        dxpad = jnp.concatenate([dx_ref[...], jnp.where(last, 0.0, dxh_ref[...])], axis=0)
        d_rx, dlw_rows = _conv_taps_bwd(dxpad, jnp.concatenate([hrx, rx], axis=0), lw_ref[...], LRU_K, tm)
        dlw_ref[...] += jnp.concatenate(dlw_rows + [jnp.zeros((8 - LRU_K, LRU_W), f32)], axis=0)
        dzr_ref[...] = d_rx.astype(bf16)

    def col(c):
        return pl.BlockSpec((tm, CONV_W), lambda i: (i, c))

    def prev(c):
        return pl.BlockSpec((HALO, CONV_W), lambda i: (jnp.maximum(i * hb - 1, 0), c))

    nxt = pl.BlockSpec((HALO, CONV_W), lambda i: (jnp.minimum((i + 1) * hb, nt * hb - 1), 0))
    return _pcall(
        body, pieces, name="branch_bwd_b", grid=(nt,),
        in_specs=[col(3), col(4), prev(3), prev(4), col(5), prev(5), col(0), nxt, col(0), nxt,
                  _full(p["cw"].shape), _full(p["lw"].shape)],
        out_specs=[pl.BlockSpec((tm, 2 * CONV_W), lambda i: (i, 0)), pl.BlockSpec((tm, LRU_W), lambda i: (i, 0)),
                   _full((HALO, CONV_W)), _full((8, LRU_W))],
        out_shape=[SDS((s, 2 * CONV_W), bf16), SDS((s, LRU_W), bf16), SDS((HALO, CONV_W), f32), SDS((8, LRU_W), f32)],
        operands=[z, z, z, z, z, z, d_conv, d_conv, d_xc, d_xc, p["cw"], p["lw"]], sem=("arbitrary",))


def inproj_bwd(dq, dkv, dzc, dzr, drg, h, g1, w_in, dh1, pieces=None):
    s = h.shape[0]
    tm = _tile(s)

    def body(dq_ref, dkv_ref, dzc_ref, dzr_ref, drg_ref, h_ref, g_ref, w_ref, dh1_ref, dh_ref, dw_ref, dg_ref):
        i = pl.program_id(0)

        @pl.when(i == 0)
        def _():
            dw_ref[...] = jnp.zeros_like(dw_ref)
            dg_ref[...] = jnp.zeros_like(dg_ref)

        dz = jnp.concatenate([dq_ref[...], dkv_ref[...], dzc_ref[...], dzr_ref[...], drg_ref[...]], axis=1)
        x = h_ref[...]
        g = g_ref[...]
        hn, r = _rms_fwd(x, g)
        d_hn = _dot_nt(dz, w_ref[...])
        dw_ref[...] += _dot_tn(hn.astype(bf16), dz)
        dx, dg = _rms_bwd(d_hn, x, r, g)
        dh_ref[...] = dh1_ref[...] + dx
        dg_ref[...] += dg

    def rowb(w):
        return pl.BlockSpec((tm, w), lambda i: (i, 0))

    return _pcall(
        body, pieces, name="inproj_bwd", grid=(s // tm,),
        in_specs=[rowb(ATTN_W), rowb(2 * KV_W), rowb(2 * CONV_W), rowb(LRU_W), rowb(LRU_W), rowb(D_MODEL),
                  _full((1, D_MODEL)), _full((D_MODEL, IN_W)), rowb(D_MODEL)],
        out_specs=[rowb(D_MODEL), _full((D_MODEL, IN_W)), _full((1, D_MODEL))],
        out_shape=[SDS((s, D_MODEL), f32), SDS((D_MODEL, IN_W), f32), SDS((1, D_MODEL), f32)],
        operands=[dq, dkv, dzc, dzr, drg, h, g1, w_in, dh1], sem=("arbitrary",), vmem=VMEM_LIMIT)


def _block_diag(w):
    out = jnp.zeros((LRU_W, LRU_W), w.dtype)
    hd = LRU_W // LRU_HEADS
    for hh in range(LRU_HEADS):
        out = out.at[hh * hd:(hh + 1) * hd, hh * hd:(hh + 1) * hd].set(w[hh])
    return out


def _diag_blocks(w):
    hd = LRU_W // LRU_HEADS
    return jnp.stack([w[hh * hd:(hh + 1) * hd, hh * hd:(hh + 1) * hd] for hh in range(LRU_HEADS)])


def _layer_params(sp, l):
    mix = sp["mix_norm"][l]
    return dict(
        g1=sp["norm1"][l][None, :], g2=sp["norm2"][l][None, :],
        sinks=jnp.repeat(sp["attn_sinks"][l], BLOCK)[:, None],
        ga=mix[None, :ATTN_W], gc=mix[None, ATTN_W:ATTN_W + CONV_W], gl=mix[None, ATTN_W + CONV_W:],
        cw=jnp.pad(sp["conv_dw_w"][l], ((0, HALO - CONV_K), (0, 0))), cb=sp["conv_dw_b"][l][None, :],
        lng=sp["conv_ln_g"][l][None, :], lnb=sp["conv_ln_b"][l][None, :],
        lw=jnp.pad(sp["lru_conv_w"][l], ((0, 8 - LRU_K), (0, 0))), lb=sp["lru_conv_b"][l][None, :],
        wa=_block_diag(sp["lru_wa"][l]).astype(bf16), ba=sp["lru_ba"][l].reshape(1, LRU_W),
        wx=_block_diag(sp["lru_wx"][l]).astype(bf16), bx=sp["lru_bx"][l].reshape(1, LRU_W),
        lam=sp["lru_lambda"][l][None, :],
    )


def local_step(x, tgt, big, sp):
    return train_local(x, tgt, sp, LocalWeights(big))


class LocalWeights:
    def __init__(self, big):
        self.big = big
        self.grads = [dict() for _ in range(DEPTH)]

    def weight(self, name, l):
        return self.big[l][name]

    def host(self, point, l):
        return None

    def grad(self, name, l, g):
        self.grads[l][name] = g

    def big_grads(self):
        return self.grads


def train_local(x, tgt, sp, cs):
    lp = [_layer_params(sp, l) for l in range(DEPTH)]
    bias = _attn_bias()
    saved = []
    h = x
    for l in range(DEPTH):
        p = lp[l]
        z = inproj_fwd(h, p["g1"], cs.weight("w_in", l))
        o, ya = attn_fwd(z, p["sinks"], bias, p["ga"], cs.host("attn_fwd", l))
        conv, hst, yc, yl = branch_fwd(z, p, cs.host("branch_fwd", l))
        h1, hn2 = outproj_fwd(ya, yc, yl, h, cs.weight("w_out", l), p["g2"])
        up, h2 = mlp_fwd(hn2, h1, cs.weight("w_up", l), cs.weight("w_dn", l), cs.host("mlp_fwd", l))
        saved.append(dict(h=h, z=z, o=o, ya=ya, conv=conv, hst=hst, yc=yc, yl=yl, h1=h1, hn2=hn2, up=up))
        h = h2
    dh, loss, d_gf = final_loss(h, tgt, sp["final_norm"][None, :])
    small_g = [None] * DEPTH
    for l in reversed(range(DEPTH)):
        p, sv = lp[l], saved[l]
        w_up, w_dn = cs.weight("w_up", l), cs.weight("w_dn", l)
        d_up, dh1, d_g2 = mlp_bwd_act(dh, sv["up"], sv["h1"], p["g2"], w_up, w_dn, cs.host("mlp_bwd_act", l))
        dw_up, dw_dn = mlp_bwd_w(sv["hn2"], d_up, sv["up"], dh, cs.host("mlp_bwd_w", l))
        cs.grad("w_up", l, dw_up)
        cs.grad("w_dn", l, dw_dn)
        dy, dw_out = outproj_bwd(dh1, sv["ya"], sv["yc"], sv["yl"], cs.weight("w_out", l))
        cs.grad("w_out", l, dw_out)
        dq, dkv, d_sk, d_ga = attn_bwd(sv["z"], sv["o"], dy, p["sinks"], bias, p["ga"], cs.host("attn_bwd", l))
        d_conv, d_xc, d_rg, vec, dwa, dwx = branch_bwd_a(sv["z"], sv["conv"], sv["hst"], dy, p, cs.host("branch_bwd_a", l))
        dzc, dzr, dcw, dlw = branch_bwd_b(sv["z"], d_conv, d_xc, p, cs.host("branch_bwd_b", l))
        dh, dw_in, d_g1 = inproj_bwd(dq, dkv, dzc, dzr, d_rg, sv["h"], p["g1"], cs.weight("w_in", l), dh1,
                                     cs.host("inproj_bwd", l))
        cs.grad("w_in", l, dw_in)
        hd = LRU_W // LRU_HEADS
        small_g[l] = dict(
            norm1=d_g1[0], attn_sinks=d_sk[:, 0], conv_dw_w=dcw[:CONV_K], conv_dw_b=vec[_V_CB],
            conv_ln_g=vec[_V_LNG], conv_ln_b=vec[_V_LNB], lru_conv_w=dlw[:LRU_K], lru_conv_b=vec[_V_LB],
            lru_wa=_diag_blocks(dwa), lru_ba=vec[_V_BA].reshape(LRU_HEADS, hd),
            lru_wx=_diag_blocks(dwx), lru_bx=vec[_V_BX].reshape(LRU_HEADS, hd), lru_lambda=vec[_V_LAM],
            mix_norm=jnp.concatenate([d_ga[0], vec[_V_GC], vec[_V_GL]]), norm2=d_g2[0],
        )
    return loss, dh, cs.big_grads(), small_g, d_gf[0]


_HBM = pl.BlockSpec(memory_space=pl.ANY)


def _place():
    x, y, c = lax.axis_index("x"), lax.axis_index("y"), lax.axis_index("c")
    chips = [(1 - x, y), (x, 1 - y), (1 - x, 1 - y)]
    return x, y, c, chips


class Comm:
    def __init__(self, ins, out_shape, aliases, sems, start, finish, done):
        self.ins, self.out_shape, self.aliases, self.sems = list(ins), list(out_shape), dict(aliases), list(sems)
        self.start, self.finish, self.done = start, finish, done


def _pcall(body, pieces, *, name, grid, in_specs, out_specs, out_shape, operands, scratch_shapes=(), sem, vmem=None):
    in_specs, out_specs, out_shape, scratch_shapes = list(in_specs), list(out_specs), list(out_shape), list(scratch_shapes)
    if not pieces:
        return pl.pallas_call(body, name=name, grid=grid, in_specs=in_specs, out_specs=out_specs, out_shape=out_shape,
                              scratch_shapes=scratch_shapes, compiler_params=_params(sem, vmem))(*operands)
    n_in, n_out, n_scr = len(in_specs), len(out_specs), len(scratch_shapes)
    c_ins = [a for p in pieces for a in p.ins]
    c_outs = [s for p in pieces for s in p.out_shape]
    c_sems = [n for p in pieces for n in p.sems]
    aliases, spans, ki, ko, ks = {}, [], 0, 0, 0
    for p in pieces:
        spans.append((ki, ko, ks))
        for a, b in p.aliases.items():
            aliases[n_in + ki + a] = n_out + ko + b
        ki, ko, ks = ki + len(p.ins), ko + len(p.out_shape), ks + len(p.sems)

    def hosted(*refs):
        ins, cin = refs[:n_in], refs[n_in:n_in + ki]
        outs, cout = refs[n_in + ki:n_in + ki + n_out], refs[n_in + ki + n_out:n_in + ki + n_out + ko]
        scr, csem = refs[n_in + ki + n_out + ko:n_in + ki + n_out + ko + n_scr], refs[n_in + ki + n_out + ko + n_scr:]
        first = functools.reduce(jnp.logical_and, [pl.program_id(d) == 0 for d in range(len(grid))])
        last = functools.reduce(jnp.logical_and, [pl.program_id(d) == grid[d] - 1 for d in range(len(grid))])

        def each(which):
            for p, (a, b, s) in zip(pieces, spans):
                getattr(p, which)(cin[a:a + len(p.ins)], cout[b:b + len(p.out_shape)], csem[s:s + len(p.sems)])

        @pl.when(first)
        def _():
            each("start")

        body(*ins, *outs, *scr)

        @pl.when(last)
        def _():
            each("finish")

    res = pl.pallas_call(
        hosted, name=name + "_host", grid=grid, in_specs=in_specs + [_HBM] * ki, out_specs=out_specs + [_HBM] * ko,
        out_shape=out_shape + c_outs, scratch_shapes=scratch_shapes + [pltpu.SemaphoreType.DMA((n,)) for n in c_sems],
        input_output_aliases=aliases, compiler_params=_params(("arbitrary",) * len(grid), vmem),
    )(*operands, *c_ins)
    for p, (a, b, s) in zip(pieces, spans):
        p.done(res[n_out + b:n_out + b + len(p.out_shape)])
    return res[:n_out]


def standalone(pieces, name):
    ki = sum(len(p.ins) for p in pieces)
    ko = sum(len(p.out_shape) for p in pieces)
    spans, a, b, s = [], 0, 0, 0
    aliases = {}
    for p in pieces:
        spans.append((a, b, s))
        for i, o in p.aliases.items():
            aliases[a + i] = b + o
        a, b, s = a + len(p.ins), b + len(p.out_shape), s + len(p.sems)

    def body(*refs):
        cin, cout, csem = refs[:ki], refs[ki:ki + ko], refs[ki + ko:]
        for which in ("start", "finish"):
            for p, (a, b, s) in zip(pieces, spans):
                getattr(p, which)(cin[a:a + len(p.ins)], cout[b:b + len(p.out_shape)], csem[s:s + len(p.sems)])

    res = pl.pallas_call(
        body, name=name, in_specs=[_HBM] * ki, out_specs=[_HBM] * ko, out_shape=[s for p in pieces for s in p.out_shape],
        scratch_shapes=[pltpu.SemaphoreType.DMA((n,)) for p in pieces for n in p.sems], input_output_aliases=aliases,
    )(*[a for p in pieces for a in p.ins])
    for p, (a, b, s) in zip(pieces, spans):
        p.done(res[b:b + len(p.out_shape)])


def _rows_half(ref, which, rows):
    return ref.at[pl.ds(pl.multiple_of(which * rows, 8), rows)]


def gather_ici_piece(bufs, done):
    n = len(bufs)

    def copies(cout):
        x, y, c, chips = _place()
        out = []
        for j, (cx, cy) in enumerate(chips):
            for w in range(n):
                half = bufs[w].shape[1] // 2
                out.append((j, w, _rows_half(cout[w].at[2 * x + y], c, half), _rows_half(cout[w].at[2 * cx + cy], c, half),
                            (cx, cy, c)))
        return out

    def start(cin, cout, sems):
        for j, w, mine, _, to in copies(cout):
            pltpu.make_async_remote_copy(src_ref=mine, dst_ref=mine, send_sem=sems[0].at[n * j + w],
                                         recv_sem=sems[1].at[n * j + w], device_id=to, device_id_type=MESH).start()

    def finish(cin, cout, sems):
        for j, w, mine, landed, to in copies(cout):
            pltpu.make_async_remote_copy(src_ref=mine, dst_ref=landed, send_sem=sems[0].at[n * j + w],
                                         recv_sem=sems[1].at[n * j + w], device_id=to, device_id_type=MESH).wait()

    return Comm(bufs, [SDS(b.shape, b.dtype) for b in bufs], {w: w for w in range(n)}, [3 * n, 3 * n], start, finish, done)


def gather_full_piece(bufs, done):
    n = len(bufs)

    def copies(cout):
        x, y, c, chips = _place()
        return [(n * j + w, cout[w].at[2 * x + y], cout[w].at[2 * cx + cy], (cx, cy, c))
                for j, (cx, cy) in enumerate(chips) for w in range(n)]

    def start(cin, cout, sems):
        for k, mine, _, to in copies(cout):
            pltpu.make_async_remote_copy(src_ref=mine, dst_ref=mine, send_sem=sems[0].at[k], recv_sem=sems[1].at[k],
                                         device_id=to, device_id_type=MESH).start()

    def finish(cin, cout, sems):
        for k, mine, landed, to in copies(cout):
            pltpu.make_async_remote_copy(src_ref=mine, dst_ref=landed, send_sem=sems[0].at[k], recv_sem=sems[1].at[k],
                                         device_id=to, device_id_type=MESH).wait()

    return Comm(bufs, [SDS(b.shape, b.dtype) for b in bufs], {w: w for w in range(n)}, [3 * n, 3 * n], start, finish, done)


def gather_d2d_piece(bufs, done):
    n = len(bufs)

    def copies(cout):
        x, y, c, chips = _place()
        out = []
        for j, (cx, cy) in enumerate(chips):
            for w in range(n):
                half = bufs[w].shape[1] // 2
                slot = cout[w].at[2 * cx + cy]
                out.append((n * j + w, _rows_half(slot, c, half), _rows_half(slot, 1 - c, half), (x, y, 1 - c)))
        return out

    def start(cin, cout, sems):
        for k, mine, _, to in copies(cout):
            pltpu.make_async_remote_copy(src_ref=mine, dst_ref=mine, send_sem=sems[0].at[k], recv_sem=sems[1].at[k],
                                         device_id=to, device_id_type=MESH).start()

    def finish(cin, cout, sems):
        for k, mine, theirs, to in copies(cout):
            pltpu.make_async_remote_copy(src_ref=mine, dst_ref=theirs, send_sem=sems[0].at[k], recv_sem=sems[1].at[k],
                                         device_id=to, device_id_type=MESH).wait()

    return Comm(bufs, [SDS(b.shape, b.dtype) for b in bufs], {w: w for w in range(n)}, [3 * n, 3 * n], start, finish, done)


def pair_piece(parts, done):
    n = len(parts)

    def copies(cin, cout):
        x, y, c, _ = _place()
        out = []
        for w in range(n):
            half = parts[w].shape[1] // 2
            out.append((w, cin[w].at[:, pl.ds(pl.multiple_of((1 - c) * half, 8), half), :], cout[w], (x, y, 1 - c)))
        return out

    def start(cin, cout, sems):
        for w, src, dst, to in copies(cin, cout):
            pltpu.make_async_remote_copy(src_ref=src, dst_ref=dst, send_sem=sems[0].at[w], recv_sem=sems[1].at[w],
                                         device_id=to, device_id_type=MESH).start()

    def finish(cin, cout, sems):
        for w, src, dst, to in copies(cin, cout):
            pltpu.make_async_remote_copy(src_ref=src, dst_ref=dst, send_sem=sems[0].at[w], recv_sem=sems[1].at[w],
                                         device_id=to, device_id_type=MESH).wait()

    return Comm(parts, [SDS((N_SHARD, a.shape[1] // 2, a.shape[2]), f32) for a in parts], {}, [n, n], start, finish, done)


def shard_piece(sums16, done):
    n = len(sums16)

    def copies(cin, cout):
        x, y, c, chips = _place()
        return [(n * j + w, cin[w].at[2 * cx + cy], cout[w].at[j], (cx, cy, c))
                for j, (cx, cy) in enumerate(chips) for w in range(n)]

    def start(cin, cout, sems):
        for k, src, dst, to in copies(cin, cout):
            pltpu.make_async_remote_copy(src_ref=src, dst_ref=dst, send_sem=sems[0].at[k], recv_sem=sems[1].at[k],
                                         device_id=to, device_id_type=MESH).start()

    def finish(cin, cout, sems):
        for k, src, dst, to in copies(cin, cout):
            pltpu.make_async_remote_copy(src_ref=src, dst_ref=dst, send_sem=sems[0].at[k], recv_sem=sems[1].at[k],
                                         device_id=to, device_id_type=MESH).wait()

    return Comm(sums16, [SDS((3,) + a.shape[1:], bf16) for a in sums16], {}, [3 * n, 3 * n], start, finish, done)


def place_shard(a, layer, idx, dtype):
    _, r, cdim = a.shape
    tr = min(r, 512)

    def body(idx_ref, a_ref, o_ref):
        o_ref[0] = a_ref[0].astype(dtype)

    return pl.pallas_call(
        body, name="place_shard",
        grid_spec=pltpu.PrefetchScalarGridSpec(
            num_scalar_prefetch=1, grid=(r // tr,),
            in_specs=[pl.BlockSpec((1, tr, cdim), lambda i, idx_ref: (layer, i, 0))],
            out_specs=pl.BlockSpec((1, tr, cdim), lambda i, idx_ref: (idx_ref[1], i, 0))),
        out_shape=SDS((N_SHARD, r, cdim), dtype),
        compiler_params=_params(("arbitrary",)),
    )(idx, a)


_KEYS = ("w_in", "w_out", "w_up", "w_dn")


class MeshWeights:
    def __init__(self, w_in, w_out, w_up, w_down, conv_dw_w, lru_conv_w, idx):
        self.idx = idx
        src = dict(w_in=w_in, w_out=w_out, w_up=w_up, w_dn=w_down)
        self.buf = {(n, l): place_shard(src[n], l, idx, bf16) for n in _KEYS for l in range(DEPTH)}
        self.conv = {(n, l): place_shard(a, l, idx, f32)
                     for n, a in (("cw", conv_dw_w), ("lw", lru_conv_w)) for l in range(DEPTH)}
        self.cache, self.parts, self.sum32, self.sum16, self.got = {}, {}, {}, {}, {}
        first, small = [("w_in", 0)], list(self.conv)

        def store_conv(outs):
            self.conv.update(zip(small, outs))

        standalone([self._gather(gather_ici_piece, first), gather_full_piece([self.conv[k] for k in small], store_conv)],
                   "gather_first_ici")
        standalone([self._gather(gather_d2d_piece, first)], "gather_first_d2d")

    def _gather(self, piece, keys):
        def done(outs):
            self.buf.update(zip(keys, outs))
        return piece([self.buf[k] for k in keys], done)

    def _pair(self, keys):
        def done(outs):
            for k, recv in zip(keys, outs):
                self.sum32[k], self.sum16[k] = chip_sum(self.parts[k], recv, self.idx)
        return pair_piece([self.parts[k] for k in keys], done)

    def _shard(self, keys):
        def done(outs):
            self.got.update(zip(keys, outs))
        return shard_piece([self.sum16[k] for k in keys], done)

    def conv_weights(self):
        out = []
        for n in ("cw", "lw"):
            a = jnp.stack([self.conv[(n, l)] for l in range(DEPTH)])
            out.append(a.transpose(0, 2, 1, 3).reshape(DEPTH, a.shape[2], N_SHARD * a.shape[3]))
        return out

    def weight(self, name, l):
        if (name, l) not in self.cache:
            b = self.buf[(name, l)]
            if name == "w_in":
                b = b.transpose(1, 0, 2).reshape(D_MODEL, IN_W)
            elif name == "w_out":
                b = b.reshape(D_MODEL, D_MODEL)
            self.cache[(name, l)] = b
        return self.cache[(name, l)]

    def host(self, point, l):
        rest0 = [("w_out", 0), ("w_up", 0), ("w_dn", 0), ("w_in", 1)]
        rest1 = [("w_out", 1), ("w_up", 1), ("w_dn", 1)]
        plan = {
            ("attn_fwd", 0): lambda: [self._gather(gather_ici_piece, rest0)],
            ("branch_fwd", 0): lambda: [self._gather(gather_d2d_piece, rest0)],
            ("mlp_fwd", 0): lambda: [self._gather(gather_ici_piece, rest1)],
            ("attn_fwd", 1): lambda: [self._gather(gather_d2d_piece, rest1)],
            ("attn_bwd", 1): lambda: [self._pair([("w_up", 1), ("w_dn", 1)])],
            ("inproj_bwd", 1): lambda: [self._pair([("w_out", 1)])],
            ("mlp_bwd_act", 0): lambda: [self._shard([("w_up", 1), ("w_dn", 1)])],
            ("mlp_bwd_w", 0): lambda: [self._pair([("w_in", 1)])],
            ("attn_bwd", 0): lambda: [self._shard([("w_in", 1), ("w_out", 1)]), self._pair([("w_up", 0), ("w_dn", 0)])],
            ("branch_bwd_a", 0): lambda: [self._shard([("w_up", 0)])],
            ("branch_bwd_b", 0): lambda: [self._shard([("w_dn", 0)])],
        }
        make = plan.get((point, l))
        return make() if make else None

    def grad(self, name, l, g):
        if name == "w_in":
            g = g.reshape(D_MODEL, N_SHARD, IN_W // N_SHARD).transpose(1, 0, 2)
        elif name == "w_out":
            g = g.reshape(N_SHARD, D_MODEL // N_SHARD, D_MODEL)
        self.parts[(name, l)] = g

    def big_grads(self):
        last = [("w_in", 0), ("w_out", 0)]
        standalone([self._pair(last)], "pair_last")
        standalone([self._shard(last)], "shard_last")
        tots = []
        for n in _KEYS:
            t = None
            for l in reversed(range(DEPTH)):
                t = shard_sum(self.sum32[(n, l)], self.got[(n, l)], self.idx, l, t)
            tots.append(t)
        return tots


def chip_sum(g, recv, idx):
    _, r, cdim = g.shape
    half = r // 2
    tr = min(half, 512)
    nh = half // tr

    def body(idx_ref, g_ref, r_ref, o32_ref, o16_ref):
        tot = g_ref[...] + r_ref[...]
        o32_ref[...] = tot
        o16_ref[...] = tot.astype(bf16)

    blk = pl.BlockSpec((1, tr, cdim), lambda s, i, idx_ref: (s, i, 0))
    return pl.pallas_call(
        body, name="chip_sum",
        grid_spec=pltpu.PrefetchScalarGridSpec(
            num_scalar_prefetch=1, grid=(N_SHARD, nh),
            in_specs=[pl.BlockSpec((1, tr, cdim), lambda s, i, idx_ref: (s, idx_ref[0] * nh + i, 0)), blk],
            out_specs=[blk, blk]),
        out_shape=[SDS((N_SHARD, half, cdim), f32), SDS((N_SHARD, half, cdim), bf16)],
        compiler_params=_params(("arbitrary", "arbitrary")),
    )(idx, g, recv)


def shard_sum(sum32, got16, idx, layer, prev):
    _, half, cdim = sum32.shape
    tr = min(half, 512)

    def body(idx_ref, a_ref, r0_ref, r1_ref, r2_ref, *rest):
        o_ref = rest[-1]
        o_ref[0, 0] = ((a_ref[0] + r0_ref[0].astype(f32)) + r1_ref[0].astype(f32)) + r2_ref[0].astype(f32)

    def rel(j):
        return pl.BlockSpec((1, tr, cdim), lambda i, idx_ref: (j, i, 0))

    in_specs = [pl.BlockSpec((1, tr, cdim), lambda i, idx_ref: (idx_ref[1], i, 0)), rel(0), rel(1), rel(2)]
    operands = [idx, sum32, got16, got16, got16]
    aliases = {}
    if prev is not None:
        in_specs.append(_HBM)
        operands.append(prev)
        aliases = {5: 0}
    return pl.pallas_call(
        body, name="shard_sum",
        grid_spec=pltpu.PrefetchScalarGridSpec(
            num_scalar_prefetch=1, grid=(half // tr,), in_specs=in_specs,
            out_specs=pl.BlockSpec((1, 1, tr, cdim), lambda i, idx_ref: (layer, idx_ref[0], i, 0))),
        out_shape=SDS((DEPTH, 2, half, cdim), f32), input_output_aliases=aliases,
        compiler_params=_params(("arbitrary",)),
    )(*operands)


def halves_exchange(tots):
    nw = len(tots)

    def body(*refs):
        bufs = refs[nw:2 * nw]
        send_sem, recv_sem = refs[2 * nw:]
        x, y, c, _ = _place()

        def copy(w, l, half_idx):
            return pltpu.make_async_remote_copy(
                src_ref=bufs[w].at[l, half_idx], dst_ref=bufs[w].at[l, half_idx], send_sem=send_sem.at[DEPTH * w + l],
                recv_sem=recv_sem.at[DEPTH * w + l], device_id=(x, y, 1 - c), device_id_type=MESH)

        sends = [copy(w, l, c) for w in range(nw) for l in range(DEPTH)]
        for cp in sends:
            cp.start()
        for w in range(nw):
            for l in range(DEPTH):
                copy(w, l, 1 - c).wait_recv()
        for cp in sends:
            cp.wait_send()

    return pl.pallas_call(
        body, name="halves_exchange", in_specs=[_HBM] * nw, out_specs=[_HBM] * nw,
        out_shape=[SDS(a.shape, f32) for a in tots], input_output_aliases={w: w for w in range(nw)},
        scratch_shapes=[pltpu.SemaphoreType.DMA((DEPTH * nw,)), pltpu.SemaphoreType.DMA((DEPTH * nw,))],
    )(*tots)


N_DEV = 8


def small_allreduce(vec):
    r = vec.shape[0]

    def body(v_ref, o_ref, all_ref, send_sems, recv_sems):
        x, y, c, chips = _place()
        me, sib = (x, y, c), (x, y, 1 - c)

        def rows(px, py, pc):
            return all_ref.at[4 * px + 2 * py + pc]

        def copy(k, block, to, src=None):
            return pltpu.make_async_remote_copy(
                src_ref=rows(*block) if src is None else src, dst_ref=rows(*block), send_sem=send_sems.at[k],
                recv_sem=recv_sems.at[k], device_id=to, device_id_type=MESH)

        first = [copy(0, me, sib, src=v_ref)]
        first += [copy(1 + j, me, (*chip, c), src=v_ref) for j, chip in enumerate(chips)]
        for cp in first:
            cp.start()
        rows(*me)[...] = v_ref[...]
        passed = [copy(4 + j, (*chip, c), sib) for j, chip in enumerate(chips)]
        for j, chip in enumerate(chips):
            copy(1 + j, (*chip, c), me).wait_recv()
            passed[j].start()
        copy(0, sib, me).wait_recv()
        for j, chip in enumerate(chips):
            copy(4 + j, (*chip, 1 - c), me).wait_recv()
        for cp in first + passed:
            cp.wait_send()
        acc = all_ref[0]
        for d in range(1, N_DEV):
            acc = acc + all_ref[d]
        o_ref[...] = acc

    return pl.pallas_call(
        body, name="small_allreduce",
        in_specs=[pl.BlockSpec(memory_space=pltpu.VMEM)], out_specs=pl.BlockSpec(memory_space=pltpu.VMEM),
        out_shape=SDS((r, LANES), f32),
        scratch_shapes=[pltpu.VMEM((N_DEV, r, LANES), f32), pltpu.SemaphoreType.DMA((7,)), pltpu.SemaphoreType.DMA((7,))],
    )(vec)


def _adamw_math(w, g, m, v):
    m = ADAM_B1 * m + (1.0 - ADAM_B1) * g
    v = ADAM_B2 * v + (1.0 - ADAM_B2) * (g * g)
    m_hat = m / (1.0 - ADAM_B1 ** ADAM_STEP)
    v_hat = v / (1.0 - ADAM_B2 ** ADAM_STEP)
    delta = -ADAM_LR * (m_hat / (jnp.sqrt(v_hat) + ADAM_EPS) + ADAM_WD * w)
    return delta, m, v


def adamw_big(w, g, m, v):
    _, r, cdim = w.shape
    tr = min(r, 256)

    def body(w_ref, g_ref, m_ref, v_ref, d_ref, mo_ref, vo_ref):
        d, mm, vv = _adamw_math(w_ref[...], g_ref[...], m_ref[...], v_ref[...])
        d_ref[...] = d
        mo_ref[...] = mm
        vo_ref[...] = vv

    blk = pl.BlockSpec((1, tr, cdim), lambda l, i: (l, i, 0))
    return pl.pallas_call(
        body, name="adamw_big", grid=(DEPTH, r // tr), in_specs=[blk] * 4, out_specs=[blk] * 3,
        out_shape=[SDS(w.shape, f32)] * 3, compiler_params=_params(("parallel", "parallel")),
    )(w, g, m, v)


def adamw_small(ws, gs, ms, vs):
    n = len(ws)

    def body(*refs):
        w_r, g_r, m_r, v_r = refs[:n], refs[n:2 * n], refs[2 * n:3 * n], refs[3 * n:4 * n]
        d_o, m_o, v_o = refs[4 * n:5 * n], refs[5 * n:6 * n], refs[6 * n:7 * n]
        for k in range(n):
            d, mm, vv = _adamw_math(w_r[k][...], g_r[k][...], m_r[k][...], v_r[k][...])
            d_o[k][...] = d
            m_o[k][...] = mm
            v_o[k][...] = vv

    vm = pl.BlockSpec(memory_space=pltpu.VMEM)
    shapes = [SDS(a.shape, f32) for a in ws]
    outs = pl.pallas_call(
        body, name="adamw_small", in_specs=[vm] * (4 * n), out_specs=[vm] * (3 * n), out_shape=shapes * 3,
    )(*ws, *gs, *ms, *vs)
    return outs[:n], outs[n:2 * n], outs[2 * n:]


_BIG = ("w_in", "w_out", "w_up", "w_down")
_WEIGHTS = ("norm1", "w_in", "attn_sinks", "conv_dw_w", "conv_dw_b", "conv_ln_g", "conv_ln_b", "lru_conv_w", "lru_conv_b",
            "lru_wa", "lru_ba", "lru_wx", "lru_bx", "lru_lambda", "mix_norm", "w_out", "norm2", "w_up", "w_down", "final_norm")
_SMALL = tuple(n for n in _WEIGHTS if n not in _BIG)
_SMALL_FULL_SHAPE = dict(
    norm1=(DEPTH, D_MODEL), attn_sinks=(DEPTH, N_HEADS), conv_dw_w=(DEPTH, CONV_K, CONV_W), conv_dw_b=(DEPTH, CONV_W),
    conv_ln_g=(DEPTH, CONV_W), conv_ln_b=(DEPTH, CONV_W), lru_conv_w=(DEPTH, LRU_K, LRU_W), lru_conv_b=(DEPTH, LRU_W),
    lru_wa=(DEPTH, LRU_HEADS, 64, 64), lru_ba=(DEPTH, LRU_HEADS, 64), lru_wx=(DEPTH, LRU_HEADS, 64, 64),
    lru_bx=(DEPTH, LRU_HEADS, 64), lru_lambda=(DEPTH, LRU_W), mix_norm=(DEPTH, D_MODEL), norm2=(DEPTH, D_MODEL),
    final_norm=(D_MODEL,))
_CHANNEL_SHARDED = ("conv_dw_w", "lru_conv_w")


def _pad_lanes(n):
    return -(-n // LANES) * LANES


def _pack(named):
    flat = []
    for a in named:
        a = a.reshape(-1)
        flat.append(jnp.pad(a, (0, _pad_lanes(a.shape[0]) - a.shape[0])))
    v = jnp.concatenate(flat)
    rows = -(-v.shape[0] // (8 * LANES)) * 8
    return jnp.pad(v, (0, rows * LANES - v.shape[0])).reshape(rows, LANES)


def _unpack(vec, shapes):
    flat = vec.reshape(-1)
    out, off = [], 0
    for shp in shapes:
        n = math.prod(shp)
        out.append(flat[off:off + n].reshape(shp))
        off += _pad_lanes(n)
    return out


def _as2d(a):
    return a.reshape(-1, a.shape[-1]) if a.ndim > 1 else a.reshape(1, -1)


def kernel(x, norm1, w_in, attn_sinks, conv_dw_w, conv_dw_b, conv_ln_g, conv_ln_b, lru_conv_w, lru_conv_b, lru_wa, lru_ba, lru_wx, lru_bx, lru_lambda, mix_norm, w_out, norm2, w_up, w_down, final_norm, loss_target, m_norm1, m_w_in, m_attn_sinks, m_conv_dw_w, m_conv_dw_b, m_conv_ln_g, m_conv_ln_b, m_lru_conv_w, m_lru_conv_b, m_lru_wa, m_lru_ba, m_lru_wx, m_lru_bx, m_lru_lambda, m_mix_norm, m_w_out, m_norm2, m_w_up, m_w_down, m_final_norm, v_norm1, v_w_in, v_attn_sinks, v_conv_dw_w, v_conv_dw_b, v_conv_ln_g, v_conv_ln_b, v_lru_conv_w, v_lru_conv_b, v_lru_wa, v_lru_ba, v_lru_wx, v_lru_bx, v_lru_lambda, v_mix_norm, v_w_out, v_norm2, v_w_up, v_w_down, v_final_norm):
    wts = dict(norm1=norm1, w_in=w_in, attn_sinks=attn_sinks, conv_dw_w=conv_dw_w, conv_dw_b=conv_dw_b, conv_ln_g=conv_ln_g,
               conv_ln_b=conv_ln_b, lru_conv_w=lru_conv_w, lru_conv_b=lru_conv_b, lru_wa=lru_wa, lru_ba=lru_ba, lru_wx=lru_wx,
               lru_bx=lru_bx, lru_lambda=lru_lambda, mix_norm=mix_norm, w_out=w_out, norm2=norm2, w_up=w_up, w_down=w_down,
               final_norm=final_norm)
    mom = dict(norm1=m_norm1, w_in=m_w_in, attn_sinks=m_attn_sinks, conv_dw_w=m_conv_dw_w, conv_dw_b=m_conv_dw_b,
               conv_ln_g=m_conv_ln_g, conv_ln_b=m_conv_ln_b, lru_conv_w=m_lru_conv_w, lru_conv_b=m_lru_conv_b, lru_wa=m_lru_wa,
               lru_ba=m_lru_ba, lru_wx=m_lru_wx, lru_bx=m_lru_bx, lru_lambda=m_lru_lambda, mix_norm=m_mix_norm, w_out=m_w_out,
               norm2=m_norm2, w_up=m_w_up, w_down=m_w_down, final_norm=m_final_norm)
    var = dict(norm1=v_norm1, w_in=v_w_in, attn_sinks=v_attn_sinks, conv_dw_w=v_conv_dw_w, conv_dw_b=v_conv_dw_b,
               conv_ln_g=v_conv_ln_g, conv_ln_b=v_conv_ln_b, lru_conv_w=v_lru_conv_w, lru_conv_b=v_lru_conv_b, lru_wa=v_lru_wa,
               lru_ba=v_lru_ba, lru_wx=v_lru_wx, lru_bx=v_lru_bx, lru_lambda=v_lru_lambda, mix_norm=v_mix_norm, w_out=v_w_out,
               norm2=v_norm2, w_up=v_w_up, w_down=v_w_down, final_norm=v_final_norm)

    c_idx = lax.axis_index("c").astype(jnp.int32)
    s_idx = (2 * lax.axis_index("x") + lax.axis_index("y")).astype(jnp.int32)
    idx = jnp.stack([c_idx, s_idx])

    cs = MeshWeights(w_in, w_out, w_up, w_down, conv_dw_w, lru_conv_w, idx)
    sp = {n: wts[n] for n in _SMALL}
    sp["conv_dw_w"], sp["lru_conv_w"] = cs.conv_weights()
    loss_blk, grad_x, tots, small_g, d_gf = train_local(x[0], loss_target[0], sp, cs)
    grads_big = {n: a.reshape(wts[n].shape) for n, a in zip(_BIG, halves_exchange(tots))}

    stacked = [jnp.stack([small_g[l][n] for l in range(DEPTH)]) for n in _SMALL if n != "final_norm"]
    packed = _pack(stacked + [d_gf, loss_blk[0, 0:1]])
    summed = small_allreduce(packed)
    names = [n for n in _SMALL if n != "final_norm"] + ["final_norm"]
    unpacked = _unpack(summed, [_SMALL_FULL_SHAPE[n] for n in names] + [(1,)])
    loss = unpacked[-1][0]
    grads = dict(zip(names, unpacked[:-1]))
    for n in _CHANNEL_SHARDED:
        width = wts[n].shape[-1]
        grads[n] = lax.dynamic_slice_in_dim(grads[n], s_idx * width, width, axis=2)
    grads.update(grads_big)

    delta, new_m, new_v = {}, {}, {}
    for n in _BIG:
        delta[n], new_m[n], new_v[n] = adamw_big(wts[n], grads[n], mom[n], var[n])
    sm = list(_SMALL)
    d_s, m_s, v_s = adamw_small([_as2d(wts[n]) for n in sm], [_as2d(grads[n]) for n in sm],
                                [_as2d(mom[n]) for n in sm], [_as2d(var[n]) for n in sm])
    for k, n in enumerate(sm):
        delta[n], new_m[n], new_v[n] = (a.reshape(wts[n].shape) for a in (d_s[k], m_s[k], v_s[k]))

    return (loss, grad_x[None], *[grads[n] for n in _WEIGHTS], *[delta[n] for n in _WEIGHTS],
            *[new_m[n] for n in _WEIGHTS], *[new_v[n] for n in _WEIGHTS])
```

```python
import functools
import math

import jax
import jax.numpy as jnp
from jax import lax
from jax.experimental import pallas as pl
from jax.experimental.pallas import tpu as pltpu

f32 = jnp.float32
bf16 = jnp.bfloat16
SDS = jax.ShapeDtypeStruct

D_MODEL = 1024
DEPTH = 2
ATTN_W = 512
KV_W = 128
HEAD_DIM = 64
N_HEADS = 8
BLOCK = 128
CONV_W = 256
CONV_K = 31
LRU_W = 256
LRU_K = 4
LRU_HEADS = 4
LRU_C = 8.0
IN_W = 1792
D_FF = 4096
N_SHARD = 4
FF_CHUNK = D_FF // N_SHARD
RMS_EPS = 1e-6
LN_EPS = 1e-5
MASK_VALUE = -1e30
HALO = 32
LANES = 128
VMEM_LIMIT = 56 * 1024 * 1024

ADAM_LR = 0.001
ADAM_B1 = 0.9
ADAM_B2 = 0.999
ADAM_EPS = 1e-08
ADAM_WD = 0.01
ADAM_STEP = 10

MESH = pl.DeviceIdType.MESH


def _dot(a, b):
    return jnp.dot(a, b, preferred_element_type=f32)


def _dot_nt(a, b):
    return lax.dot_general(a, b, (((1,), (1,)), ((), ())), preferred_element_type=f32)


def _dot_tn(a, b):
    return lax.dot_general(a, b, (((0,), (0,)), ((), ())), preferred_element_type=f32)


def _rms_fwd(x, g):
    r = lax.rsqrt(jnp.mean(x * x, axis=-1, keepdims=True) + RMS_EPS)
    return x * r * g, r


def _rms_bwd(dy, x, r, g):
    t = dy * g
    dx = r * t - x * (r * r * r) * jnp.mean(t * x, axis=-1, keepdims=True)
    dg = jnp.sum(dy * x * r, axis=0, keepdims=True)
    return dx, dg


def _sigmoid(x):
    return jax.nn.sigmoid(x)


_GELU_K = math.sqrt(2.0 / math.pi)


def _gelu(x):
    t = jnp.tanh(_GELU_K * (x + 0.044715 * x * x * x))
    return 0.5 * x * (1.0 + t), t


def _gelu_grad(x, t):
    return 0.5 * (1.0 + t) + 0.5 * x * (1.0 - t * t) * _GELU_K * (1.0 + 3.0 * 0.044715 * x * x)


def _log1p(x):
    return jnp.where(x < 1e-4, x - 0.5 * x * x, jnp.log(1.0 + x))


def _softplus(x):
    return jnp.maximum(x, 0.0) + _log1p(jnp.exp(-jnp.abs(x)))


def _neg_expm1(x):
    series = -x * (1.0 + 0.5 * x * (1.0 + x * (1.0 / 3.0) * (1.0 + 0.25 * x)))
    return jnp.where(x > -0.01, series, 1.0 - jnp.exp(x))


def _sublane_rolls(x, count, forward):
    n = x.shape[0]
    return [x if b == 0 else pltpu.roll(x, b if forward else n - b, 0) for b in range(count)]


def _conv_taps(xpad, w, k_width):
    t_rows = xpad.shape[0] - HALO
    rolled = _sublane_rolls(xpad, min(k_width, 8), forward=True)
    acc = None
    for k in range(k_width):
        hi, lo = divmod((k_width - 1) - k, 8)
        term = rolled[lo][HALO - 8 * hi:HALO - 8 * hi + t_rows] * w[k:k + 1, :]
        acc = term if acc is None else acc + term
    return acc


def _conv_taps_bwd(dpad, upad, w, k_width, t_rows):
    n_lo = min(k_width, 8)
    d_rolled = _sublane_rolls(dpad, n_lo, forward=False)
    u_rolled = _sublane_rolls(upad, n_lo, forward=True)
    d_in = None
    dw_rows = []
    d_out = dpad[:t_rows]
    for k in range(k_width):
        hi, lo = divmod((k_width - 1) - k, 8)
        term = d_rolled[lo][8 * hi:8 * hi + t_rows] * w[k:k + 1, :]
        d_in = term if d_in is None else d_in + term
        us = u_rolled[lo][HALO - 8 * hi:HALO - 8 * hi + t_rows]
        dw_rows.append(jnp.sum(d_out * us, axis=0, keepdims=True))
    return d_in, dw_rows


def _scan_fwd(a, b):
    t_rows = a.shape[0]
    row = lax.broadcasted_iota(jnp.int32, a.shape, 0)
    d = 1
    while d < t_rows:
        a_sh = jnp.where(row < d, 1.0, pltpu.roll(a, d, 0))
        b_sh = jnp.where(row < d, 0.0, pltpu.roll(b, d, 0))
        b = a * b_sh + b
        a = a * a_sh
        d *= 2
    return a, b


def _scan_bwd(a, b):
    t_rows = a.shape[0]
    row = lax.broadcasted_iota(jnp.int32, a.shape, 0)
    d = 1
    while d < t_rows:
        a_sh = jnp.where(row >= t_rows - d, 1.0, pltpu.roll(a, t_rows - d, 0))
        b_sh = jnp.where(row >= t_rows - d, 0.0, pltpu.roll(b, t_rows - d, 0))
        b = b + a * b_sh
        a = a * a_sh
        d *= 2
    return b


def _full(shape):
    nd = len(shape)
    return pl.BlockSpec(shape, lambda *_: (0,) * nd)


def _params(sem, vmem=None):
    return pltpu.CompilerParams(dimension_semantics=sem, vmem_limit_bytes=vmem)


def _tile(s):
    return min(512, s)


def inproj_fwd(h, g1, w_in):
    s = h.shape[0]
    tm = _tile(s)

    def body(h_ref, g_ref, w_ref, z_ref):
        hn, _ = _rms_fwd(h_ref[...], g_ref[...])
        z_ref[...] = _dot(hn.astype(bf16), w_ref[...]).astype(bf16)

    return pl.pallas_call(
        body, name="inproj_fwd", grid=(s // tm,),
        in_specs=[pl.BlockSpec((tm, D_MODEL), lambda i: (i, 0)), _full((1, D_MODEL)), _full((D_MODEL, IN_W))],
        out_specs=pl.BlockSpec((tm, IN_W), lambda i: (i, 0)),
        out_shape=SDS((s, IN_W), bf16),
        compiler_params=_params(("parallel",), VMEM_LIMIT),
    )(h, g1, w_in)


ATT_ROWS = N_HEADS * BLOCK


def _attn_bias():
    qi = jnp.arange(BLOCK)[None, :]
    key = jnp.arange(2 * BLOCK)[:, None]
    band = (key > qi) & (key <= qi + BLOCK)
    first = band & (key >= BLOCK)
    mask = jnp.where(jnp.stack([first, band]), 0.0, MASK_VALUE).astype(f32)
    return jnp.tile(mask, (1, 1, N_HEADS // 2))


def _attn_band(kvc, kvp):
    kb = jnp.concatenate([kvp[:, :KV_W], kvc[:, :KV_W]], axis=0)
    vb = jnp.concatenate([kvp[:, KV_W:], kvc[:, KV_W:]], axis=0)
    lane = lax.broadcasted_iota(jnp.int32, kb.shape, 1)
    kb_sw = pltpu.roll(kb, HEAD_DIM, 1)
    vb_sw = pltpu.roll(vb, HEAD_DIM, 1)
    kx = [jnp.where(lane < HEAD_DIM, kb, kb_sw), jnp.where(lane >= HEAD_DIM, kb, kb_sw)]
    vx = [jnp.where(lane < HEAD_DIM, vb, vb_sw), jnp.where(lane >= HEAD_DIM, vb, vb_sw)]
    return kx, vx


def _stack_heads(x, mlo):
    zero = jnp.zeros((BLOCK, LANES), x.dtype)
    out = []
    for hk in range(2):
        parts = []
        for j in (2 * hk, 2 * hk + 1):
            xj = x[:, j * LANES:(j + 1) * LANES]
            parts += [jnp.where(mlo, xj, zero), jnp.where(mlo, zero, xj)]
        out.append(jnp.concatenate(parts, axis=0))
    return out


def _unstack_heads(y, mlo):
    cols = []
    for hk in range(2):
        for t in range(2):
            base = 2 * t * BLOCK
            cols.append(jnp.where(mlo, y[hk][base:base + BLOCK], y[hk][base + BLOCK:base + 2 * BLOCK]))
    return jnp.concatenate(cols, axis=1)


def _attn_probs(q4, kx, bias_t, sink_row):
    out = []
    half = ATT_ROWS // 2
    for hk in range(2):
        s = _dot_nt(kx[hk], q4[hk]) + bias_t
        sink = sink_row[:, hk * half:(hk + 1) * half]
        m = jnp.maximum(jnp.max(s, axis=0, keepdims=True), sink)
        p = jnp.exp(s - m)
        e_sink = jnp.exp(sink - m)
        inv = 1.0 / (jnp.sum(p, axis=0, keepdims=True) + e_sink)
        out.append((p * inv, e_sink * inv))
    return out


def attn_fwd(z, sink_col, bias, g_a, pieces=None):
    s = z.shape[0]
    nb = s // BLOCK

    def body(q_ref, kvc_ref, kvp_ref, sk_ref, b_ref, g_ref, o_ref, y_ref):
        n = pl.program_id(0)
        kx, vx = _attn_band(kvc_ref[...], kvp_ref[...])
        mlo = lax.broadcasted_iota(jnp.int32, (BLOCK, LANES), 1) < HEAD_DIM
        q4 = _stack_heads(q_ref[...] * (HEAD_DIM ** -0.5), mlo)
        probs = _attn_probs(q4, kx, b_ref[jnp.minimum(n, 1)], sk_ref[...])
        o = _unstack_heads([_dot_tn(probs[hk][0].astype(bf16), vx[hk]) for hk in range(2)], mlo)
        o_ref[...] = o.astype(bf16)
        y, _ = _rms_fwd(o, g_ref[...])
        y_ref[...] = y.astype(bf16)

    return _pcall(
        body, pieces, name="attn_fwd", grid=(nb,),
        in_specs=[pl.BlockSpec((BLOCK, ATTN_W), lambda n: (n, 0)),
                  pl.BlockSpec((BLOCK, 2 * KV_W), lambda n: (n, 2)),
                  pl.BlockSpec((BLOCK, 2 * KV_W), lambda n: (jnp.maximum(n - 1, 0), 2)),
                  _full((1, ATT_ROWS)), _full((2, 2 * BLOCK, ATT_ROWS // 2)), _full((1, ATTN_W))],
        out_specs=[pl.BlockSpec((BLOCK, ATTN_W), lambda n: (n, 0)), pl.BlockSpec((BLOCK, ATTN_W), lambda n: (n, 0))],
        out_shape=[SDS((s, ATTN_W), bf16), SDS((s, ATTN_W), bf16)],
        operands=[z, z, z, sink_col, bias, g_a], sem=("parallel",))


def _lru_gates(xc, wa, ba, wx, bx, lam):
    xcb = xc.astype(bf16)
    r = _sigmoid(_dot(xcb, wa) + ba)
    ig = _sigmoid(_dot(xcb, wx) + bx)
    sp = _softplus(-lam)
    la = (-LRU_C * r) * sp
    a = jnp.exp(la)
    mult = jnp.sqrt(_neg_expm1(2.0 * la))
    return r, ig, sp, la, a, mult


def branch_fwd(z, p, pieces=None):
    s = z.shape[0]
    tm = _tile(s)
    hb = tm // HALO

    def body(cv_ref, cg_ref, rx_ref, rg_ref, cvh_ref, cgh_ref, rxh_ref,
             cw_ref, cb_ref, lng_ref, lnb_ref, lw_ref, lb_ref, wa_ref, ba_ref, wx_ref, bx_ref, lam_ref, gc_ref, gl_ref,
             conv_ref, hst_ref, nc_ref, nl_ref, carry_ref):
        i = pl.program_id(0)
        first = i == 0

        @pl.when(first)
        def _():
            carry_ref[...] = jnp.zeros_like(carry_ref)

        cval = cv_ref[...].astype(f32)
        u = cval * _sigmoid(cg_ref[...].astype(f32))
        hu = jnp.where(first, 0.0, cvh_ref[...].astype(f32) * _sigmoid(cgh_ref[...].astype(f32)))
        conv = _conv_taps(jnp.concatenate([hu, u], axis=0), cw_ref[...], CONV_K) + cb_ref[...]
        conv_ref[...] = conv
        mu = jnp.mean(conv, axis=-1, keepdims=True)
        xm = conv - mu
        rstd = lax.rsqrt(jnp.mean(xm * xm, axis=-1, keepdims=True) + LN_EPS)
        ln = xm * rstd * lng_ref[...] + lnb_ref[...]
        yc = ln * _sigmoid(ln)
        nc, _ = _rms_fwd(yc, gc_ref[...])
        nc_ref[...] = nc.astype(bf16)

        rx = rx_ref[...].astype(f32)
        hrx = jnp.where(first, 0.0, rxh_ref[...].astype(f32))
        xc = _conv_taps(jnp.concatenate([hrx, rx], axis=0), lw_ref[...], LRU_K) + lb_ref[...]
        r, ig, sp, la, a, mult = _lru_gates(xc, wa_ref[...], ba_ref[...], wx_ref[...], bx_ref[...], lam_ref[...])
        gx = mult * (ig * xc)
        a_cum, h_loc = _scan_fwd(a, gx)
        hs = a_cum * carry_ref[0:1, :] + h_loc
        carry_ref[...] = jnp.broadcast_to(hs[tm - 1:tm, :], carry_ref.shape)
        hst_ref[...] = hs
        gl, _ = _gelu(rg_ref[...].astype(f32))
        nl, _ = _rms_fwd(hs * gl, gl_ref[...])
        nl_ref[...] = nl.astype(bf16)

    def col(c):
        return pl.BlockSpec((tm, CONV_W), lambda i: (i, c))

    def halo(c):
        return pl.BlockSpec((HALO, CONV_W), lambda i: (jnp.maximum(i * hb - 1, 0), c))

    small = [p["cw"], p["cb"], p["lng"], p["lnb"], p["lw"], p["lb"], p["wa"], p["ba"], p["wx"], p["bx"], p["lam"],
             p["gc"], p["gl"]]
    row = pl.BlockSpec((tm, CONV_W), lambda i: (i, 0))
    return _pcall(
        body, pieces, name="branch_fwd", grid=(s // tm,),
        in_specs=[col(3), col(4), col(5), col(6), halo(3), halo(4), halo(5)] + [_full(a.shape) for a in small],
        out_specs=[row, row, row, row],
        out_shape=[SDS((s, CONV_W), f32), SDS((s, LRU_W), f32), SDS((s, CONV_W), bf16), SDS((s, LRU_W), bf16)],
        scratch_shapes=[pltpu.VMEM((8, LRU_W), f32)],
        operands=[z, z, z, z, z, z, z, *small], sem=("arbitrary",))


def outproj_fwd(ya, yc, yl, h, w_out, g2):
    s = h.shape[0]
    tm = _tile(s)

    def body(ya_ref, yc_ref, yl_ref, h_ref, w_ref, g_ref, h1_ref, hn_ref):
        y = jnp.concatenate([ya_ref[...], yc_ref[...], yl_ref[...]], axis=1)
        h1 = h_ref[...] + _dot(y, w_ref[...])
        h1_ref[...] = h1
        hn, _ = _rms_fwd(h1, g_ref[...])
        hn_ref[...] = hn.astype(bf16)

    return pl.pallas_call(
        body, name="outproj_fwd", grid=(s // tm,),
        in_specs=[pl.BlockSpec((tm, ATTN_W), lambda i: (i, 0)), pl.BlockSpec((tm, CONV_W), lambda i: (i, 0)),
                  pl.BlockSpec((tm, LRU_W), lambda i: (i, 0)), pl.BlockSpec((tm, D_MODEL), lambda i: (i, 0)),
                  _full((D_MODEL, D_MODEL)), _full((1, D_MODEL))],
        out_specs=[pl.BlockSpec((tm, D_MODEL), lambda i: (i, 0)), pl.BlockSpec((tm, D_MODEL), lambda i: (i, 0))],
        out_shape=[SDS((s, D_MODEL), f32), SDS((s, D_MODEL), bf16)],
        compiler_params=_params(("parallel",), VMEM_LIMIT),
    )(ya, yc, yl, h, w_out, g2)


def mlp_fwd(hn2, h1, w_up, w_dn, pieces=None):
    s = h1.shape[0]
    tm = min(256, s)

    def body(x_ref, h_ref, wu_ref, wd_ref, up_ref, h2_ref):
        x = x_ref[...]
        acc = h_ref[...]
        for c in range(N_SHARD):
            u = _dot(x, wu_ref[c])
            up_ref[:, c * FF_CHUNK:(c + 1) * FF_CHUNK] = u.astype(bf16)
            act = jnp.square(jnp.maximum(u, 0.0)).astype(bf16)
            acc = acc + _dot(act, wd_ref[c])
        h2_ref[...] = acc

    return _pcall(
        body, pieces, name="mlp_fwd", grid=(s // tm,),
        in_specs=[pl.BlockSpec((tm, D_MODEL), lambda i: (i, 0)), pl.BlockSpec((tm, D_MODEL), lambda i: (i, 0)),
                  _full((N_SHARD, D_MODEL, FF_CHUNK)), _full((N_SHARD, FF_CHUNK, D_MODEL))],
        out_specs=[pl.BlockSpec((tm, D_FF), lambda i: (i, 0)), pl.BlockSpec((tm, D_MODEL), lambda i: (i, 0))],
        out_shape=[SDS((s, D_FF), bf16), SDS((s, D_MODEL), f32)],
        operands=[hn2, h1, w_up, w_dn], sem=("parallel",), vmem=VMEM_LIMIT)


def final_loss(h, tgt, gf):
    s = h.shape[0]
    tm = _tile(s)

    def body(h_ref, t_ref, g_ref, dh_ref, loss_ref, dg_ref):
        i = pl.program_id(0)

        @pl.when(i == 0)
        def _():
            loss_ref[...] = jnp.zeros_like(loss_ref)
            dg_ref[...] = jnp.zeros_like(dg_ref)

        x = h_ref[...]
        g = g_ref[...]
        y, r = _rms_fwd(x, g)
        err = y - t_ref[...]
        part = 0.5 * jnp.sum(jnp.mean(err * err, axis=-1, keepdims=True), axis=0, keepdims=True)
        loss_ref[...] += jnp.broadcast_to(part, loss_ref.shape)
        dx, dg = _rms_bwd(err * (1.0 / D_MODEL), x, r, g)
        dh_ref[...] = dx
        dg_ref[...] += dg

    return pl.pallas_call(
        body, name="final_loss", grid=(s // tm,),
        in_specs=[pl.BlockSpec((tm, D_MODEL), lambda i: (i, 0)), pl.BlockSpec((tm, D_MODEL), lambda i: (i, 0)),
                  _full((1, D_MODEL))],
        out_specs=[pl.BlockSpec((tm, D_MODEL), lambda i: (i, 0)), _full((8, LANES)), _full((1, D_MODEL))],
        out_shape=[SDS((s, D_MODEL), f32), SDS((8, LANES), f32), SDS((1, D_MODEL), f32)],
        compiler_params=_params(("arbitrary",)),
    )(h, tgt, gf)


def mlp_bwd_act(dh, up, h1, g2, w_up, w_dn, pieces=None):
    s = dh.shape[0]
    tm = min(256, s)

    def body(dh_ref, up_ref, h1_ref, g_ref, wu_ref, wd_ref, dup_ref, dh1_ref, dg_ref):
        i = pl.program_id(0)

        @pl.when(i == 0)
        def _():
            dg_ref[...] = jnp.zeros_like(dg_ref)

        dh = dh_ref[...]
        dhb = dh.astype(bf16)
        d_hn = jnp.zeros((tm, D_MODEL), f32)
        for c in range(N_SHARD):
            d_act = _dot_nt(dhb, wd_ref[c])
            u = up_ref[:, c * FF_CHUNK:(c + 1) * FF_CHUNK].astype(f32)
            d_u = (d_act * (2.0 * jnp.maximum(u, 0.0))).astype(bf16)
            dup_ref[:, c * FF_CHUNK:(c + 1) * FF_CHUNK] = d_u
            d_hn = d_hn + _dot_nt(d_u, wu_ref[c])
        x = h1_ref[...]
        g = g_ref[...]
        _, r = _rms_fwd(x, g)
        dx, dg = _rms_bwd(d_hn, x, r, g)
        dh1_ref[...] = dh + dx
        dg_ref[...] += dg

    return _pcall(
        body, pieces, name="mlp_bwd_act", grid=(s // tm,),
        in_specs=[pl.BlockSpec((tm, D_MODEL), lambda i: (i, 0)), pl.BlockSpec((tm, D_FF), lambda i: (i, 0)),
                  pl.BlockSpec((tm, D_MODEL), lambda i: (i, 0)), _full((1, D_MODEL)),
                  _full((N_SHARD, D_MODEL, FF_CHUNK)), _full((N_SHARD, FF_CHUNK, D_MODEL))],
        out_specs=[pl.BlockSpec((tm, D_FF), lambda i: (i, 0)), pl.BlockSpec((tm, D_MODEL), lambda i: (i, 0)),
                   _full((1, D_MODEL))],
        out_shape=[SDS((s, D_FF), bf16), SDS((s, D_MODEL), f32), SDS((1, D_MODEL), f32)],
        operands=[dh, up, h1, g2, w_up, w_dn], sem=("arbitrary",), vmem=VMEM_LIMIT)


def mlp_bwd_w(hn2, d_up, up, dh, pieces=None):
    s = dh.shape[0]
    tk = _tile(s)

    def body(x_ref, du_ref, up_ref, dh_ref, dwu_ref, dwd_ref):
        k = pl.program_id(1)

        @pl.when(k == 0)
        def _():
            dwu_ref[...] = jnp.zeros_like(dwu_ref)
            dwd_ref[...] = jnp.zeros_like(dwd_ref)

        dwu_ref[0] += _dot_tn(x_ref[...], du_ref[...])
        act = jnp.square(jnp.maximum(up_ref[...].astype(f32), 0.0)).astype(bf16)
        dwd_ref[0] += _dot_tn(act, dh_ref[...].astype(bf16))

    return _pcall(
        body, pieces, name="mlp_bwd_w", grid=(N_SHARD, s // tk),
        in_specs=[pl.BlockSpec((tk, D_MODEL), lambda c, k: (k, 0)), pl.BlockSpec((tk, FF_CHUNK), lambda c, k: (k, c)),
                  pl.BlockSpec((tk, FF_CHUNK), lambda c, k: (k, c)), pl.BlockSpec((tk, D_MODEL), lambda c, k: (k, 0))],
        out_specs=[pl.BlockSpec((1, D_MODEL, FF_CHUNK), lambda c, k: (c, 0, 0)),
                   pl.BlockSpec((1, FF_CHUNK, D_MODEL), lambda c, k: (c, 0, 0))],
        out_shape=[SDS((N_SHARD, D_MODEL, FF_CHUNK), f32), SDS((N_SHARD, FF_CHUNK, D_MODEL), f32)],
        operands=[hn2, d_up, up, dh], sem=("parallel", "arbitrary"), vmem=VMEM_LIMIT)


def outproj_bwd(dh1, ya, yc, yl, w_out):
    s = dh1.shape[0]
    tm = _tile(s)

    def body(dh_ref, ya_ref, yc_ref, yl_ref, w_ref, dy_ref, dw_ref):
        i = pl.program_id(0)

        @pl.when(i == 0)
        def _():
            dw_ref[...] = jnp.zeros_like(dw_ref)

        dhb = dh_ref[...].astype(bf16)
        dy_ref[...] = _dot_nt(dhb, w_ref[...])
        y = jnp.concatenate([ya_ref[...], yc_ref[...], yl_ref[...]], axis=1)
        dw_ref[...] += _dot_tn(y, dhb)

    return pl.pallas_call(
        body, name="outproj_bwd", grid=(s // tm,),
        in_specs=[pl.BlockSpec((tm, D_MODEL), lambda i: (i, 0)), pl.BlockSpec((tm, ATTN_W), lambda i: (i, 0)),
                  pl.BlockSpec((tm, CONV_W), lambda i: (i, 0)), pl.BlockSpec((tm, LRU_W), lambda i: (i, 0)),
                  _full((D_MODEL, D_MODEL))],
        out_specs=[pl.BlockSpec((tm, D_MODEL), lambda i: (i, 0)), _full((D_MODEL, D_MODEL))],
        out_shape=[SDS((s, D_MODEL), f32), SDS((D_MODEL, D_MODEL), f32)],
        compiler_params=_params(("arbitrary",), VMEM_LIMIT),
    )(dh1, ya, yc, yl, w_out)


def attn_bwd(z, o, dy, sink_col, bias, g_a, pieces=None):
    s = z.shape[0]
    nb = s // BLOCK

    def body(q_ref, kvc_ref, kvp_ref, o_ref, dy_ref, sk_ref, b_ref, g_ref, dq_ref, dkv_ref, dsk_ref, dg_ref,
             pend_ref, dsk_acc):
        n = pl.program_id(0)

        @pl.when(n == 0)
        def _():
            pend_ref[...] = jnp.zeros_like(pend_ref)
            dsk_acc[...] = jnp.zeros_like(dsk_acc)
            dg_ref[...] = jnp.zeros_like(dg_ref)

        @pl.when(n < nb)
        def _():
            kx, vx = _attn_band(kvc_ref[...], kvp_ref[...])
            mlo = lax.broadcasted_iota(jnp.int32, (BLOCK, LANES), 1) < HEAD_DIM
            scale = HEAD_DIM ** -0.5
            q4 = _stack_heads(q_ref[...] * scale, mlo)
            o_f = o_ref[...].astype(f32)
            g = g_ref[...]
            _, r = _rms_fwd(o_f, g)
            d_o, dg = _rms_bwd(dy_ref[...], o_f, r, g)
            dg_ref[...] += dg
            do4 = _stack_heads(d_o.astype(bf16), mlo)
            probs = _attn_probs(q4, kx, b_ref[jnp.minimum(n, 1)], sk_ref[...])
            half = ATT_ROWS // 2
            dq4, tk, tv = [], [], []
            for hk in range(2):
                pr, p_sink = probs[hk]
                d_p = _dot_nt(vx[hk], do4[hk])
                d_row = jnp.sum(pr * d_p, axis=0, keepdims=True)
                d_s = (pr * (d_p - d_row)).astype(bf16)
                dsk_acc[:, hk * half:(hk + 1) * half] -= p_sink * d_row
                dq4.append(_dot_tn(d_s, kx[hk]))
                tk.append(_dot(d_s, q4[hk]))
                tv.append(_dot(pr.astype(bf16), do4[hk]))
            dq_ref[...] = (_unstack_heads(dq4, mlo) * scale).astype(bf16)
            lane = lax.broadcasted_iota(jnp.int32, (2 * BLOCK, LANES), 1)
            fk = [t + pltpu.roll(t, HEAD_DIM, 1) for t in tk]
            fv = [t + pltpu.roll(t, HEAD_DIM, 1) for t in tv]
            band = jnp.concatenate([jnp.where(lane < HEAD_DIM, fk[0], fk[1]), jnp.where(lane < HEAD_DIM, fv[0], fv[1])], axis=1)
            dkv_ref[...] = (pend_ref[...] + band[:BLOCK]).astype(bf16)
            pend_ref[...] = band[BLOCK:]

        @pl.when(n == nb)
        def _():
            dkv_ref[...] = pend_ref[...].astype(bf16)
            for hh in range(N_HEADS):
                tot = jnp.sum(dsk_acc[:, hh * BLOCK:(hh + 1) * BLOCK], axis=1, keepdims=True)
                dsk_ref[hh:hh + 1, :] = jnp.broadcast_to(tot, (1, LANES))

    def cur(n):
        return jnp.minimum(n, nb - 1)

    return _pcall(
        body, pieces, name="attn_bwd", grid=(nb + 1,),
        in_specs=[pl.BlockSpec((BLOCK, ATTN_W), lambda n: (cur(n), 0)),
                  pl.BlockSpec((BLOCK, 2 * KV_W), lambda n: (cur(n), 2)),
                  pl.BlockSpec((BLOCK, 2 * KV_W), lambda n: (jnp.maximum(cur(n) - 1, 0), 2)),
                  pl.BlockSpec((BLOCK, ATTN_W), lambda n: (cur(n), 0)),
                  pl.BlockSpec((BLOCK, ATTN_W), lambda n: (cur(n), 0)),
                  _full((1, ATT_ROWS)), _full((2, 2 * BLOCK, ATT_ROWS // 2)), _full((1, ATTN_W))],
        out_specs=[pl.BlockSpec((BLOCK, ATTN_W), lambda n: (cur(n), 0)),
                   pl.BlockSpec((BLOCK, 2 * KV_W), lambda n: (jnp.maximum(n - 1, 0), 0)),
                   _full((N_HEADS, LANES)), _full((1, ATTN_W))],
        out_shape=[SDS((s, ATTN_W), bf16), SDS((s, 2 * KV_W), bf16), SDS((N_HEADS, LANES), f32), SDS((1, ATTN_W), f32)],
        scratch_shapes=[pltpu.VMEM((BLOCK, 2 * KV_W), f32), pltpu.VMEM((1, ATT_ROWS), f32)],
        operands=[z, z, z, o, dy, sink_col, bias, g_a], sem=("arbitrary",))


_V_GC, _V_LNG, _V_LNB, _V_CB, _V_GL, _V_BA, _V_BX, _V_LAM, _V_LB = range(9)
_V_ROWS = 16


def branch_bwd_a(z, conv, hst, dy, p, pieces=None):
    s = z.shape[0]
    tm = _tile(s)
    nt = s // tm
    hb = tm // HALO
    h8 = tm // 8

    def body(conv_ref, dyc_ref, dyl_ref, rx_ref, rxh_ref, rg_ref, hst_ref, hsth_ref,
             lng_ref, lnb_ref, lw_ref, lb_ref, wa_ref, ba_ref, wx_ref, bx_ref, lam_ref, gc_ref, gl_ref,
             dconv_ref, dxc_ref, drg_ref, vec_ref, dwa_ref, dwx_ref, carry_ref):
        i = pl.program_id(0)
        ti = nt - 1 - i

        @pl.when(i == 0)
        def _():
            carry_ref[...] = jnp.zeros_like(carry_ref)
            vec_ref[...] = jnp.zeros_like(vec_ref)
            dwa_ref[...] = jnp.zeros_like(dwa_ref)
            dwx_ref[...] = jnp.zeros_like(dwx_ref)

        conv = conv_ref[...]
        mu = jnp.mean(conv, axis=-1, keepdims=True)
        xm = conv - mu
        rstd = lax.rsqrt(jnp.mean(xm * xm, axis=-1, keepdims=True) + LN_EPS)
        xhat = xm * rstd
        lng = lng_ref[...]
        ln = xhat * lng + lnb_ref[...]
        sg = _sigmoid(ln)
        yc = ln * sg
        gc = gc_ref[...]
        _, rc = _rms_fwd(yc, gc)
        d_yc, d_gc = _rms_bwd(dyc_ref[...], yc, rc, gc)
        d_ln = d_yc * (sg * (1.0 + ln * (1.0 - sg)))
        d_xhat = d_ln * lng
        d_conv = rstd * (d_xhat - jnp.mean(d_xhat, axis=-1, keepdims=True)
                         - xhat * jnp.mean(d_xhat * xhat, axis=-1, keepdims=True))
        dconv_ref[...] = d_conv

        rx = rx_ref[...].astype(f32)
        hrx = jnp.where(ti == 0, 0.0, rxh_ref[...].astype(f32))
        xc = _conv_taps(jnp.concatenate([hrx, rx], axis=0), lw_ref[...], LRU_K) + lb_ref[...]
        wa = wa_ref[...]
        wx = wx_ref[...]
        lam = lam_ref[...]
        r, ig, sp, la, a, mult = _lru_gates(xc, wa, ba_ref[...], wx, bx_ref[...], lam)
        hs = hst_ref[...]
        row = lax.broadcasted_iota(jnp.int32, hs.shape, 0)
        h_before = jnp.where(ti == 0, 0.0, hsth_ref[7:8, :])
        h_prev = jnp.where(row == 0, h_before, pltpu.roll(hs, 1, 0))
        rg = rg_ref[...].astype(f32)
        gl, tg = _gelu(rg)
        out = hs * gl
        gmix = gl_ref[...]
        _, rl = _rms_fwd(out, gmix)
        d_out, d_gl = _rms_bwd(dyl_ref[...], out, rl, gmix)
        drg_ref[...] = (d_out * hs * _gelu_grad(rg, tg)).astype(bf16)
        d_h = d_out * gl
        last = row == tm - 1
        a_next = jnp.where(last, 0.0, pltpu.roll(a, tm - 1, 0))
        lmb = _scan_bwd(a_next, d_h + jnp.where(last, carry_ref[0:1, :], 0.0))
        carry_ref[...] = jnp.broadcast_to(a[0:1, :] * lmb[0:1, :], carry_ref.shape)
        d_a = lmb * h_prev
        d_mult = lmb * (ig * xc)
        d_ig = lmb * (mult * xc)
        d_la = d_a * a - d_mult * (a * a) / jnp.maximum(mult, 1e-30)
        d_pa = (d_la * (-LRU_C * sp)) * (r * (1.0 - r))
        d_px = d_ig * (ig * (1.0 - ig))
        d_pab = d_pa.astype(bf16)
        d_pxb = d_px.astype(bf16)
        d_xc = lmb * (mult * ig) + _dot_nt(d_pab, wa) + _dot_nt(d_pxb, wx)
        dxc_ref[...] = d_xc
        xcb = xc.astype(bf16)
        dwa_ref[...] += _dot_tn(xcb, d_pab)
        dwx_ref[...] += _dot_tn(xcb, d_pxb)
        d_lam = jnp.sum(d_la * (-LRU_C * r), axis=0, keepdims=True) * (-_sigmoid(-lam))

        def colsum(v):
            return jnp.sum(v, axis=0, keepdims=True)

        rows = [None] * _V_ROWS
        rows[_V_GC] = d_gc
        rows[_V_LNG] = colsum(d_ln * xhat)
        rows[_V_LNB] = colsum(d_ln)
        rows[_V_CB] = colsum(d_conv)
        rows[_V_GL] = d_gl
        rows[_V_BA] = colsum(d_pa)
        rows[_V_BX] = colsum(d_px)
        rows[_V_LAM] = d_lam
        rows[_V_LB] = colsum(d_xc)
        zero = jnp.zeros((1, CONV_W), f32)
        vec_ref[...] += jnp.concatenate([zero if v is None else v for v in rows], axis=0)

    def rev(c):
        return pl.BlockSpec((tm, CONV_W), lambda i: (nt - 1 - i, c))

    small = [p["lng"], p["lnb"], p["lw"], p["lb"], p["wa"], p["ba"], p["wx"], p["bx"], p["lam"], p["gc"], p["gl"]]
    return _pcall(
        body, pieces, name="branch_bwd_a", grid=(nt,),
        in_specs=[rev(0), rev(2), rev(3), rev(5),
                  pl.BlockSpec((HALO, CONV_W), lambda i: (jnp.maximum((nt - 1 - i) * hb - 1, 0), 5)),
                  rev(6), rev(0),
                  pl.BlockSpec((8, LRU_W), lambda i: (jnp.maximum((nt - 1 - i) * h8 - 1, 0), 0))]
                 + [_full(a.shape) for a in small],
        out_specs=[rev(0), rev(0), rev(0), _full((_V_ROWS, CONV_W)), _full((LRU_W, LRU_W)), _full((LRU_W, LRU_W))],
        out_shape=[SDS((s, CONV_W), f32), SDS((s, LRU_W), f32), SDS((s, LRU_W), bf16), SDS((_V_ROWS, CONV_W), f32),
                   SDS((LRU_W, LRU_W), f32), SDS((LRU_W, LRU_W), f32)],
        scratch_shapes=[pltpu.VMEM((8, LRU_W), f32)],
        operands=[conv, dy, dy, z, z, z, hst, hst, *small], sem=("arbitrary",))


def branch_bwd_b(z, d_conv, d_xc, p, pieces=None):
    s = z.shape[0]
    tm = _tile(s)
    nt = s // tm
    hb = tm // HALO

    def body(cv_ref, cg_ref, cvh_ref, cgh_ref, rx_ref, rxh_ref, dc_ref, dch_ref, dx_ref, dxh_ref, cw_ref, lw_ref,
             dzc_ref, dzr_ref, dcw_ref, dlw_ref):
        i = pl.program_id(0)

        @pl.when(i == 0)
        def _():
            dcw_ref[...] = jnp.zeros_like(dcw_ref)
            dlw_ref[...] = jnp.zeros_like(dlw_ref)

        first = i == 0
        last = i == nt - 1
        cval = cv_ref[...].astype(f32)
        sg = _sigmoid(cg_ref[...].astype(f32))
        u = cval * sg
        hu = jnp.where(first, 0.0, cvh_ref[...].astype(f32) * _sigmoid(cgh_ref[...].astype(f32)))
        dpad = jnp.concatenate([dc_ref[...], jnp.where(last, 0.0, dch_ref[...])], axis=0)
        d_u, dw_rows = _conv_taps_bwd(dpad, jnp.concatenate([hu, u], axis=0), cw_ref[...], CONV_K, tm)
        dcw_ref[...] += jnp.concatenate(dw_rows + [jnp.zeros((HALO - CONV_K, CONV_W), f32)], axis=0)
        dzc_ref[...] = jnp.concatenate([d_u * sg, d_u * cval * sg * (1.0 - sg)], axis=1).astype(bf16)

        rx = rx_ref[...].astype(f32)
        hrx = jnp.where(first, 0.0, rxh_ref[...].astype(f32))
        dxpad = jnp.concatenate([dx_ref[...], jnp.where(last, 0.0, dxh_ref[...])], axis=0)
        d_rx, dlw_rows = _conv_taps_bwd(dxpad, jnp.concatenate([hrx, rx], axis=0), lw_ref[...], LRU_K, tm)
        dlw_ref[...] += jnp.concatenate(dlw_rows + [jnp.zeros((8 - LRU_K, LRU_W), f32)], axis=0)
        dzr_ref[...] = d_rx.astype(bf16)

    def col(c):
        return pl.BlockSpec((tm, CONV_W), lambda i: (i, c))

    def prev(c):
        return pl.BlockSpec((HALO, CONV_W), lambda i: (jnp.maximum(i * hb - 1, 0), c))

    nxt = pl.BlockSpec((HALO, CONV_W), lambda i: (jnp.minimum((i + 1) * hb, nt * hb - 1), 0))
    return _pcall(
        body, pieces, name="branch_bwd_b", grid=(nt,),
        in_specs=[col(3), col(4), prev(3), prev(4), col(5), prev(5), col(0), nxt, col(0), nxt,
                  _full(p["cw"].shape), _full(p["lw"].shape)],
        out_specs=[pl.BlockSpec((tm, 2 * CONV_W), lambda i: (i, 0)), pl.BlockSpec((tm, LRU_W), lambda i: (i, 0)),
                   _full((HALO, CONV_W)), _full((8, LRU_W))],
        out_shape=[SDS((s, 2 * CONV_W), bf16), SDS((s, LRU_W), bf16), SDS((HALO, CONV_W), f32), SDS((8, LRU_W), f32)],
        operands=[z, z, z, z, z, z, d_conv, d_conv, d_xc, d_xc, p["cw"], p["lw"]], sem=("arbitrary",))


def inproj_bwd(dq, dkv, dzc, dzr, drg, h, g1, w_in, dh1, pieces=None):
    s = h.shape[0]
    tm = _tile(s)

    def body(dq_ref, dkv_ref, dzc_ref, dzr_ref, drg_ref, h_ref, g_ref, w_ref, dh1_ref, dh_ref, dw_ref, dg_ref):
        i = pl.program_id(0)

        @pl.when(i == 0)
        def _():
            dw_ref[...] = jnp.zeros_like(dw_ref)
            dg_ref[...] = jnp.zeros_like(dg_ref)

        dz = jnp.concatenate([dq_ref[...], dkv_ref[...], dzc_ref[...], dzr_ref[...], drg_ref[...]], axis=1)
        x = h_ref[...]
        g = g_ref[...]
        hn, r = _rms_fwd(x, g)
        d_hn = _dot_nt(dz, w_ref[...])
        dw_ref[...] += _dot_tn(hn.astype(bf16), dz)
        dx, dg = _rms_bwd(d_hn, x, r, g)
        dh_ref[...] = dh1_ref[...] + dx
        dg_ref[...] += dg

    def rowb(w):
        return pl.BlockSpec((tm, w), lambda i: (i, 0))

    return _pcall(
        body, pieces, name="inproj_bwd", grid=(s // tm,),
        in_specs=[rowb(ATTN_W), rowb(2 * KV_W), rowb(2 * CONV_W), rowb(LRU_W), rowb(LRU_W), rowb(D_MODEL),
                  _full((1, D_MODEL)), _full((D_MODEL, IN_W)), rowb(D_MODEL)],
        out_specs=[rowb(D_MODEL), _full((D_MODEL, IN_W)), _full((1, D_MODEL))],
        out_shape=[SDS((s, D_MODEL), f32), SDS((D_MODEL, IN_W), f32), SDS((1, D_MODEL), f32)],
        operands=[dq, dkv, dzc, dzr, drg, h, g1, w_in, dh1], sem=("arbitrary",), vmem=VMEM_LIMIT)


def _block_diag(w):
    out = jnp.zeros((LRU_W, LRU_W), w.dtype)
    hd = LRU_W // LRU_HEADS
    for hh in range(LRU_HEADS):
        out = out.at[hh * hd:(hh + 1) * hd, hh * hd:(hh + 1) * hd].set(w[hh])
    return out


def _diag_blocks(w):
    hd = LRU_W // LRU_HEADS
    return jnp.stack([w[hh * hd:(hh + 1) * hd, hh * hd:(hh + 1) * hd] for hh in range(LRU_HEADS)])


def _layer_params(sp, l):
    mix = sp["mix_norm"][l]
    return dict(
        g1=sp["norm1"][l][None, :], g2=sp["norm2"][l][None, :],
        sinks=jnp.repeat(sp["attn_sinks"][l], BLOCK)[None, :],
        ga=mix[None, :ATTN_W], gc=mix[None, ATTN_W:ATTN_W + CONV_W], gl=mix[None, ATTN_W + CONV_W:],
        cw=jnp.pad(sp["conv_dw_w"][l], ((0, HALO - CONV_K), (0, 0))), cb=sp["conv_dw_b"][l][None, :],
        lng=sp["conv_ln_g"][l][None, :], lnb=sp["conv_ln_b"][l][None, :],
        lw=jnp.pad(sp["lru_conv_w"][l], ((0, 8 - LRU_K), (0, 0))), lb=sp["lru_conv_b"][l][None, :],
        wa=_block_diag(sp["lru_wa"][l]).astype(bf16), ba=sp["lru_ba"][l].reshape(1, LRU_W),
        wx=_block_diag(sp["lru_wx"][l]).astype(bf16), bx=sp["lru_bx"][l].reshape(1, LRU_W),
        lam=sp["lru_lambda"][l][None, :],
    )


def local_step(x, tgt, big, sp):
    return train_local(x, tgt, sp, LocalWeights(big))


class LocalWeights:
    def __init__(self, big):
        self.big = big
        self.grads = [dict() for _ in range(DEPTH)]

    def weight(self, name, l):
        return self.big[l][name]

    def host(self, point, l):
        return None

    def grad(self, name, l, g):
        self.grads[l][name] = g

    def big_grads(self):
        return self.grads


def train_local(x, tgt, sp, cs):
    lp = [_layer_params(sp, l) for l in range(DEPTH)]
    bias = _attn_bias()
    saved = []
    h = x
    for l in range(DEPTH):
        p = lp[l]
        z = inproj_fwd(h, p["g1"], cs.weight("w_in", l))
        o, ya = attn_fwd(z, p["sinks"], bias, p["ga"], cs.host("attn_fwd", l))
        conv, hst, yc, yl = branch_fwd(z, p, cs.host("branch_fwd", l))
        h1, hn2 = outproj_fwd(ya, yc, yl, h, cs.weight("w_out", l), p["g2"])
        up, h2 = mlp_fwd(hn2, h1, cs.weight("w_up", l), cs.weight("w_dn", l), cs.host("mlp_fwd", l))
        saved.append(dict(h=h, z=z, o=o, ya=ya, conv=conv, hst=hst, yc=yc, yl=yl, h1=h1, hn2=hn2, up=up))
        h = h2
    dh, loss, d_gf = final_loss(h, tgt, sp["final_norm"][None, :])
    small_g = [None] * DEPTH
    for l in reversed(range(DEPTH)):
        p, sv = lp[l], saved[l]
        w_up, w_dn = cs.weight("w_up", l), cs.weight("w_dn", l)
        d_up, dh1, d_g2 = mlp_bwd_act(dh, sv["up"], sv["h1"], p["g2"], w_up, w_dn, cs.host("mlp_bwd_act", l))
        dw_up, dw_dn = mlp_bwd_w(sv["hn2"], d_up, sv["up"], dh, cs.host("mlp_bwd_w", l))
        cs.grad("w_up", l, dw_up)
        cs.grad("w_dn", l, dw_dn)
        dy, dw_out = outproj_bwd(dh1, sv["ya"], sv["yc"], sv["yl"], cs.weight("w_out", l))
        cs.grad("w_out", l, dw_out)
        dq, dkv, d_sk, d_ga = attn_bwd(sv["z"], sv["o"], dy, p["sinks"], bias, p["ga"], cs.host("attn_bwd", l))
        d_conv, d_xc, d_rg, vec, dwa, dwx = branch_bwd_a(sv["z"], sv["conv"], sv["hst"], dy, p, cs.host("branch_bwd_a", l))
        dzc, dzr, dcw, dlw = branch_bwd_b(sv["z"], d_conv, d_xc, p, cs.host("branch_bwd_b", l))
        dh, dw_in, d_g1 = inproj_bwd(dq, dkv, dzc, dzr, d_rg, sv["h"], p["g1"], cs.weight("w_in", l), dh1,
                                     cs.host("inproj_bwd", l))
        cs.grad("w_in", l, dw_in)
        hd = LRU_W // LRU_HEADS
        small_g[l] = dict(
            norm1=d_g1[0], attn_sinks=d_sk[:, 0], conv_dw_w=dcw[:CONV_K], conv_dw_b=vec[_V_CB],
            conv_ln_g=vec[_V_LNG], conv_ln_b=vec[_V_LNB], lru_conv_w=dlw[:LRU_K], lru_conv_b=vec[_V_LB],
            lru_wa=_diag_blocks(dwa), lru_ba=vec[_V_BA].reshape(LRU_HEADS, hd),
            lru_wx=_diag_blocks(dwx), lru_bx=vec[_V_BX].reshape(LRU_HEADS, hd), lru_lambda=vec[_V_LAM],
            mix_norm=jnp.concatenate([d_ga[0], vec[_V_GC], vec[_V_GL]]), norm2=d_g2[0],
        )
    return loss, dh, cs.big_grads(), small_g, d_gf[0]


_HBM = pl.BlockSpec(memory_space=pl.ANY)


def _place():
    x, y, c = lax.axis_index("x"), lax.axis_index("y"), lax.axis_index("c")
    chips = [(1 - x, y), (x, 1 - y), (1 - x, 1 - y)]
    return x, y, c, chips


class Comm:
    def __init__(self, ins, out_shape, aliases, sems, start, finish, done):
        self.ins, self.out_shape, self.aliases, self.sems = list(ins), list(out_shape), dict(aliases), list(sems)
        self.start, self.finish, self.done = start, finish, done


def _pcall(body, pieces, *, name, grid, in_specs, out_specs, out_shape, operands, scratch_shapes=(), sem, vmem=None):
    in_specs, out_specs, out_shape, scratch_shapes = list(in_specs), list(out_specs), list(out_shape), list(scratch_shapes)
    if not pieces:
        return pl.pallas_call(body, name=name, grid=grid, in_specs=in_specs, out_specs=out_specs, out_shape=out_shape,
                              scratch_shapes=scratch_shapes, compiler_params=_params(sem, vmem))(*operands)
    n_in, n_out, n_scr = len(in_specs), len(out_specs), len(scratch_shapes)
    c_ins = [a for p in pieces for a in p.ins]
    c_outs = [s for p in pieces for s in p.out_shape]
    c_sems = [n for p in pieces for n in p.sems]
    aliases, spans, ki, ko, ks = {}, [], 0, 0, 0
    for p in pieces:
        spans.append((ki, ko, ks))
        for a, b in p.aliases.items():
            aliases[n_in + ki + a] = n_out + ko + b
        ki, ko, ks = ki + len(p.ins), ko + len(p.out_shape), ks + len(p.sems)

    def hosted(*refs):
        ins, cin = refs[:n_in], refs[n_in:n_in + ki]
        outs, cout = refs[n_in + ki:n_in + ki + n_out], refs[n_in + ki + n_out:n_in + ki + n_out + ko]
        scr, csem = refs[n_in + ki + n_out + ko:n_in + ki + n_out + ko + n_scr], refs[n_in + ki + n_out + ko + n_scr:]
        first = functools.reduce(jnp.logical_and, [pl.program_id(d) == 0 for d in range(len(grid))])
        last = functools.reduce(jnp.logical_and, [pl.program_id(d) == grid[d] - 1 for d in range(len(grid))])

        def each(which):
            for p, (a, b, s) in zip(pieces, spans):
                getattr(p, which)(cin[a:a + len(p.ins)], cout[b:b + len(p.out_shape)], csem[s:s + len(p.sems)])

        @pl.when(first)
        def _():
            each("start")

        body(*ins, *outs, *scr)

        @pl.when(last)
        def _():
            each("finish")

    res = pl.pallas_call(
        hosted, name=name + "_host", grid=grid, in_specs=in_specs + [_HBM] * ki, out_specs=out_specs + [_HBM] * ko,
        out_shape=out_shape + c_outs, scratch_shapes=scratch_shapes + [pltpu.SemaphoreType.DMA((n,)) for n in c_sems],
        input_output_aliases=aliases, compiler_params=_params(("arbitrary",) * len(grid), vmem),
    )(*operands, *c_ins)
    for p, (a, b, s) in zip(pieces, spans):
        p.done(res[n_out + b:n_out + b + len(p.out_shape)])
    return res[:n_out]


def standalone(pieces, name):
    ki = sum(len(p.ins) for p in pieces)
    ko = sum(len(p.out_shape) for p in pieces)
    spans, a, b, s = [], 0, 0, 0
    aliases = {}
    for p in pieces:
        spans.append((a, b, s))
        for i, o in p.aliases.items():
            aliases[a + i] = b + o
        a, b, s = a + len(p.ins), b + len(p.out_shape), s + len(p.sems)

    def body(*refs):
        cin, cout, csem = refs[:ki], refs[ki:ki + ko], refs[ki + ko:]
        for which in ("start", "finish"):
            for p, (a, b, s) in zip(pieces, spans):
                getattr(p, which)(cin[a:a + len(p.ins)], cout[b:b + len(p.out_shape)], csem[s:s + len(p.sems)])

    res = pl.pallas_call(
        body, name=name, in_specs=[_HBM] * ki, out_specs=[_HBM] * ko, out_shape=[s for p in pieces for s in p.out_shape],
        scratch_shapes=[pltpu.SemaphoreType.DMA((n,)) for p in pieces for n in p.sems], input_output_aliases=aliases,
    )(*[a for p in pieces for a in p.ins])
    for p, (a, b, s) in zip(pieces, spans):
        p.done(res[b:b + len(p.out_shape)])


def _rows_half(ref, which, rows):
    return ref.at[pl.ds(pl.multiple_of(which * rows, 8), rows)]


def gather_ici_piece(bufs, done):
    n = len(bufs)

    def copies(cout):
        x, y, c, chips = _place()
        out = []
        for j, (cx, cy) in enumerate(chips):
            for w in range(n):
                half = bufs[w].shape[1] // 2
                out.append((j, w, _rows_half(cout[w].at[2 * x + y], c, half), _rows_half(cout[w].at[2 * cx + cy], c, half),
                            (cx, cy, c)))
        return out

    def start(cin, cout, sems):
        for j, w, mine, _, to in copies(cout):
            pltpu.make_async_remote_copy(src_ref=mine, dst_ref=mine, send_sem=sems[0].at[n * j + w],
                                         recv_sem=sems[1].at[n * j + w], device_id=to, device_id_type=MESH).start()

    def finish(cin, cout, sems):
        for j, w, mine, landed, to in copies(cout):
            pltpu.make_async_remote_copy(src_ref=mine, dst_ref=landed, send_sem=sems[0].at[n * j + w],
                                         recv_sem=sems[1].at[n * j + w], device_id=to, device_id_type=MESH).wait()

    return Comm(bufs, [SDS(b.shape, b.dtype) for b in bufs], {w: w for w in range(n)}, [3 * n, 3 * n], start, finish, done)


def gather_full_piece(bufs, done):
    n = len(bufs)

    def copies(cout):
        x, y, c, chips = _place()
        return [(n * j + w, cout[w].at[2 * x + y], cout[w].at[2 * cx + cy], (cx, cy, c))
                for j, (cx, cy) in enumerate(chips) for w in range(n)]

    def start(cin, cout, sems):
        for k, mine, _, to in copies(cout):
            pltpu.make_async_remote_copy(src_ref=mine, dst_ref=mine, send_sem=sems[0].at[k], recv_sem=sems[1].at[k],
                                         device_id=to, device_id_type=MESH).start()

    def finish(cin, cout, sems):
        for k, mine, landed, to in copies(cout):
            pltpu.make_async_remote_copy(src_ref=mine, dst_ref=landed, send_sem=sems[0].at[k], recv_sem=sems[1].at[k],
                                         device_id=to, device_id_type=MESH).wait()

    return Comm(bufs, [SDS(b.shape, b.dtype) for b in bufs], {w: w for w in range(n)}, [3 * n, 3 * n], start, finish, done)


def gather_d2d_piece(bufs, done):
    n = len(bufs)

    def copies(cout):
        x, y, c, chips = _place()
        out = []
        for j, (cx, cy) in enumerate(chips):
            for w in range(n):
                half = bufs[w].shape[1] // 2
                slot = cout[w].at[2 * cx + cy]
                out.append((n * j + w, _rows_half(slot, c, half), _rows_half(slot, 1 - c, half), (x, y, 1 - c)))
        return out

    def start(cin, cout, sems):
        for k, mine, _, to in copies(cout):
            pltpu.make_async_remote_copy(src_ref=mine, dst_ref=mine, send_sem=sems[0].at[k], recv_sem=sems[1].at[k],
                                         device_id=to, device_id_type=MESH).start()

    def finish(cin, cout, sems):
        for k, mine, theirs, to in copies(cout):
            pltpu.make_async_remote_copy(src_ref=mine, dst_ref=theirs, send_sem=sems[0].at[k], recv_sem=sems[1].at[k],
                                         device_id=to, device_id_type=MESH).wait()

    return Comm(bufs, [SDS(b.shape, b.dtype) for b in bufs], {w: w for w in range(n)}, [3 * n, 3 * n], start, finish, done)


def pair_piece(parts, done):
    n = len(parts)

    def copies(cin, cout):
        x, y, c, _ = _place()
        out = []
        for w in range(n):
            half = parts[w].shape[1] // 2
            out.append((w, cin[w].at[:, pl.ds(pl.multiple_of((1 - c) * half, 8), half), :], cout[w], (x, y, 1 - c)))
        return out

    def start(cin, cout, sems):
        for w, src, dst, to in copies(cin, cout):
            pltpu.make_async_remote_copy(src_ref=src, dst_ref=dst, send_sem=sems[0].at[w], recv_sem=sems[1].at[w],
                                         device_id=to, device_id_type=MESH).start()

    def finish(cin, cout, sems):
        for w, src, dst, to in copies(cin, cout):
            pltpu.make_async_remote_copy(src_ref=src, dst_ref=dst, send_sem=sems[0].at[w], recv_sem=sems[1].at[w],
                                         device_id=to, device_id_type=MESH).wait()

    return Comm(parts, [SDS((N_SHARD, a.shape[1] // 2, a.shape[2]), f32) for a in parts], {}, [n, n], start, finish, done)


def shard_piece(sums16, done):
    n = len(sums16)

    def copies(cin, cout):
        x, y, c, chips = _place()
        return [(n * j + w, cin[w].at[2 * cx + cy], cout[w].at[j], (cx, cy, c))
                for j, (cx, cy) in enumerate(chips) for w in range(n)]

    def start(cin, cout, sems):
        for k, src, dst, to in copies(cin, cout):
            pltpu.make_async_remote_copy(src_ref=src, dst_ref=dst, send_sem=sems[0].at[k], recv_sem=sems[1].at[k],
                                         device_id=to, device_id_type=MESH).start()

    def finish(cin, cout, sems):
        for k, src, dst, to in copies(cin, cout):
            pltpu.make_async_remote_copy(src_ref=src, dst_ref=dst, send_sem=sems[0].at[k], recv_sem=sems[1].at[k],
                                         device_id=to, device_id_type=MESH).wait()

    return Comm(sums16, [SDS((3,) + a.shape[1:], bf16) for a in sums16], {}, [3 * n, 3 * n], start, finish, done)


def place_shard(a, layer, idx, dtype):
    _, r, cdim = a.shape
    tr = min(r, 512)

    def body(idx_ref, a_ref, o_ref):
        o_ref[0] = a_ref[0].astype(dtype)

    return pl.pallas_call(
        body, name="place_shard",
        grid_spec=pltpu.PrefetchScalarGridSpec(
            num_scalar_prefetch=1, grid=(r // tr,),
            in_specs=[pl.BlockSpec((1, tr, cdim), lambda i, idx_ref: (layer, i, 0))],
            out_specs=pl.BlockSpec((1, tr, cdim), lambda i, idx_ref: (idx_ref[1], i, 0))),
        out_shape=SDS((N_SHARD, r, cdim), dtype),
        compiler_params=_params(("arbitrary",)),
    )(idx, a)


_KEYS = ("w_in", "w_out", "w_up", "w_dn")


class MeshWeights:
    def __init__(self, w_in, w_out, w_up, w_down, conv_dw_w, lru_conv_w, idx):
        self.idx = idx
        src = dict(w_in=w_in, w_out=w_out, w_up=w_up, w_dn=w_down)
        self.buf = {(n, l): place_shard(src[n], l, idx, bf16) for n in _KEYS for l in range(DEPTH)}
        self.conv = {(n, l): place_shard(a, l, idx, f32)
                     for n, a in (("cw", conv_dw_w), ("lw", lru_conv_w)) for l in range(DEPTH)}
        self.cache, self.parts, self.sum32, self.sum16, self.got = {}, {}, {}, {}, {}
        first, small = [("w_in", 0)], list(self.conv)

        def store_conv(outs):
            self.conv.update(zip(small, outs))

        standalone([self._gather(gather_ici_piece, first), gather_full_piece([self.conv[k] for k in small], store_conv)],
                   "gather_first_ici")
        standalone([self._gather(gather_d2d_piece, first)], "gather_first_d2d")

    def _gather(self, piece, keys):
        def done(outs):
            self.buf.update(zip(keys, outs))
        return piece([self.buf[k] for k in keys], done)

    def _pair(self, keys):
        def done(outs):
            for k, recv in zip(keys, outs):
                self.sum32[k], self.sum16[k] = chip_sum(self.parts[k], recv, self.idx)
        return pair_piece([self.parts[k] for k in keys], done)

    def _shard(self, keys):
        def done(outs):
            self.got.update(zip(keys, outs))
        return shard_piece([self.sum16[k] for k in keys], done)

    def conv_weights(self):
        out = []
        for n in ("cw", "lw"):
            a = jnp.stack([self.conv[(n, l)] for l in range(DEPTH)])
            out.append(a.transpose(0, 2, 1, 3).reshape(DEPTH, a.shape[2], N_SHARD * a.shape[3]))
        return out

    def weight(self, name, l):
        if (name, l) not in self.cache:
            b = self.buf[(name, l)]
            if name == "w_in":
                b = b.transpose(1, 0, 2).reshape(D_MODEL, IN_W)
            elif name == "w_out":
                b = b.reshape(D_MODEL, D_MODEL)
            self.cache[(name, l)] = b
        return self.cache[(name, l)]

    def host(self, point, l):
        rest0 = [("w_out", 0), ("w_up", 0), ("w_dn", 0), ("w_in", 1)]
        rest1 = [("w_out", 1), ("w_up", 1), ("w_dn", 1)]
        plan = {
            ("attn_fwd", 0): lambda: [self._gather(gather_ici_piece, rest0)],
            ("branch_fwd", 0): lambda: [self._gather(gather_d2d_piece, rest0)],
            ("mlp_fwd", 0): lambda: [self._gather(gather_ici_piece, rest1)],
            ("attn_fwd", 1): lambda: [self._gather(gather_d2d_piece, rest1)],
            ("attn_bwd", 1): lambda: [self._pair([("w_up", 1), ("w_dn", 1)])],
            ("inproj_bwd", 1): lambda: [self._pair([("w_out", 1)])],
            ("mlp_bwd_act", 0): lambda: [self._shard([("w_up", 1), ("w_dn", 1)])],
            ("mlp_bwd_w", 0): lambda: [self._pair([("w_in", 1)])],
            ("attn_bwd", 0): lambda: [self._shard([("w_in", 1), ("w_out", 1)]), self._pair([("w_up", 0), ("w_dn", 0)])],
            ("branch_bwd_a", 0): lambda: [self._shard([("w_up", 0)])],
            ("branch_bwd_b", 0): lambda: [self._shard([("w_dn", 0)])],
        }
        make = plan.get((point, l))
        return make() if make else None

    def grad(self, name, l, g):
        if name == "w_in":
            g = g.reshape(D_MODEL, N_SHARD, IN_W // N_SHARD).transpose(1, 0, 2)
        elif name == "w_out":
            g = g.reshape(N_SHARD, D_MODEL // N_SHARD, D_MODEL)
        self.parts[(name, l)] = g

    def big_grads(self):
        last = [("w_in", 0), ("w_out", 0)]
        standalone([self._pair(last)], "pair_last")
        standalone([self._shard(last)], "shard_last")
        tots = []
        for n in _KEYS:
            t = None
            for l in reversed(range(DEPTH)):
                t = shard_sum(self.sum32[(n, l)], self.got[(n, l)], self.idx, l, t)
            tots.append(t)
        return tots


def chip_sum(g, recv, idx):
    _, r, cdim = g.shape
    half = r // 2
    tr = min(half, 512)
    nh = half // tr

    def body(idx_ref, g_ref, r_ref, o32_ref, o16_ref):
        tot = g_ref[...] + r_ref[...]
        o32_ref[...] = tot
        o16_ref[...] = tot.astype(bf16)

    blk = pl.BlockSpec((1, tr, cdim), lambda s, i, idx_ref: (s, i, 0))
    return pl.pallas_call(
        body, name="chip_sum",
        grid_spec=pltpu.PrefetchScalarGridSpec(
            num_scalar_prefetch=1, grid=(N_SHARD, nh),
            in_specs=[pl.BlockSpec((1, tr, cdim), lambda s, i, idx_ref: (s, idx_ref[0] * nh + i, 0)), blk],
            out_specs=[blk, blk]),
        out_shape=[SDS((N_SHARD, half, cdim), f32), SDS((N_SHARD, half, cdim), bf16)],
        compiler_params=_params(("arbitrary", "arbitrary")),
    )(idx, g, recv)


def shard_sum(sum32, got16, idx, layer, prev):
    _, half, cdim = sum32.shape
    tr = min(half, 512)

    def body(idx_ref, a_ref, r0_ref, r1_ref, r2_ref, *rest):
        o_ref = rest[-1]
        o_ref[0, 0] = ((a_ref[0] + r0_ref[0].astype(f32)) + r1_ref[0].astype(f32)) + r2_ref[0].astype(f32)

    def rel(j):
        return pl.BlockSpec((1, tr, cdim), lambda i, idx_ref: (j, i, 0))

    in_specs = [pl.BlockSpec((1, tr, cdim), lambda i, idx_ref: (idx_ref[1], i, 0)), rel(0), rel(1), rel(2)]
    operands = [idx, sum32, got16, got16, got16]
    aliases = {}
    if prev is not None:
        in_specs.append(_HBM)
        operands.append(prev)
        aliases = {5: 0}
    return pl.pallas_call(
        body, name="shard_sum",
        grid_spec=pltpu.PrefetchScalarGridSpec(
            num_scalar_prefetch=1, grid=(half // tr,), in_specs=in_specs,
            out_specs=pl.BlockSpec((1, 1, tr, cdim), lambda i, idx_ref: (layer, idx_ref[0], i, 0))),
        out_shape=SDS((DEPTH, 2, half, cdim), f32), input_output_aliases=aliases,
        compiler_params=_params(("arbitrary",)),
    )(*operands)


def halves_exchange(tots):
    nw = len(tots)

    def body(*refs):
        bufs = refs[nw:2 * nw]
        send_sem, recv_sem = refs[2 * nw:]
        x, y, c, _ = _place()

        def copy(w, l, half_idx):
            return pltpu.make_async_remote_copy(
                src_ref=bufs[w].at[l, half_idx], dst_ref=bufs[w].at[l, half_idx], send_sem=send_sem.at[DEPTH * w + l],
                recv_sem=recv_sem.at[DEPTH * w + l], device_id=(x, y, 1 - c), device_id_type=MESH)

        sends = [copy(w, l, c) for w in range(nw) for l in range(DEPTH)]
        for cp in sends:
            cp.start()
        for w in range(nw):
            for l in range(DEPTH):
                copy(w, l, 1 - c).wait_recv()
        for cp in sends:
            cp.wait_send()

    return pl.pallas_call(
        body, name="halves_exchange", in_specs=[_HBM] * nw, out_specs=[_HBM] * nw,
        out_shape=[SDS(a.shape, f32) for a in tots], input_output_aliases={w: w for w in range(nw)},
        scratch_shapes=[pltpu.SemaphoreType.DMA((DEPTH * nw,)), pltpu.SemaphoreType.DMA((DEPTH * nw,))],
    )(*tots)


N_DEV = 8


def small_allreduce(vec):
    r = vec.shape[0]

    def body(v_ref, o_ref, all_ref, send_sems, recv_sems):
        x, y, c, chips = _place()
        me, sib = (x, y, c), (x, y, 1 - c)

        def rows(px, py, pc):
            return all_ref.at[4 * px + 2 * py + pc]

        def copy(k, block, to, src=None):
            return pltpu.make_async_remote_copy(
                src_ref=rows(*block) if src is None else src, dst_ref=rows(*block), send_sem=send_sems.at[k],
                recv_sem=recv_sems.at[k], device_id=to, device_id_type=MESH)

        first = [copy(0, me, sib, src=v_ref)]
        first += [copy(1 + j, me, (*chip, c), src=v_ref) for j, chip in enumerate(chips)]
        for cp in first:
            cp.start()
        rows(*me)[...] = v_ref[...]
        passed = [copy(4 + j, (*chip, c), sib) for j, chip in enumerate(chips)]
        for j, chip in enumerate(chips):
            copy(1 + j, (*chip, c), me).wait_recv()
            passed[j].start()
        copy(0, sib, me).wait_recv()
        for j, chip in enumerate(chips):
            copy(4 + j, (*chip, 1 - c), me).wait_recv()
        for cp in first + passed:
            cp.wait_send()
        acc = all_ref[0]
        for d in range(1, N_DEV):
            acc = acc + all_ref[d]
        o_ref[...] = acc

    return pl.pallas_call(
        body, name="small_allreduce",
        in_specs=[pl.BlockSpec(memory_space=pltpu.VMEM)], out_specs=pl.BlockSpec(memory_space=pltpu.VMEM),
        out_shape=SDS((r, LANES), f32),
        scratch_shapes=[pltpu.VMEM((N_DEV, r, LANES), f32), pltpu.SemaphoreType.DMA((7,)), pltpu.SemaphoreType.DMA((7,))],
    )(vec)


def _adamw_math(w, g, m, v):
    m = ADAM_B1 * m + (1.0 - ADAM_B1) * g
    v = ADAM_B2 * v + (1.0 - ADAM_B2) * (g * g)
    m_hat = m / (1.0 - ADAM_B1 ** ADAM_STEP)
    v_hat = v / (1.0 - ADAM_B2 ** ADAM_STEP)
    delta = -ADAM_LR * (m_hat / (jnp.sqrt(v_hat) + ADAM_EPS) + ADAM_WD * w)
    return delta, m, v


def adamw_big(w, g, m, v):
    _, r, cdim = w.shape
    tr = min(r, 256)

    def body(w_ref, g_ref, m_ref, v_ref, d_ref, mo_ref, vo_ref):
        d, mm, vv = _adamw_math(w_ref[...], g_ref[...], m_ref[...], v_ref[...])
        d_ref[...] = d
        mo_ref[...] = mm
        vo_ref[...] = vv

    blk = pl.BlockSpec((1, tr, cdim), lambda l, i: (l, i, 0))
    return pl.pallas_call(
        body, name="adamw_big", grid=(DEPTH, r // tr), in_specs=[blk] * 4, out_specs=[blk] * 3,
        out_shape=[SDS(w.shape, f32)] * 3, compiler_params=_params(("parallel", "parallel")),
    )(w, g, m, v)


def adamw_small(ws, gs, ms, vs):
    n = len(ws)

    def body(*refs):
        w_r, g_r, m_r, v_r = refs[:n], refs[n:2 * n], refs[2 * n:3 * n], refs[3 * n:4 * n]
        d_o, m_o, v_o = refs[4 * n:5 * n], refs[5 * n:6 * n], refs[6 * n:7 * n]
        for k in range(n):
            d, mm, vv = _adamw_math(w_r[k][...], g_r[k][...], m_r[k][...], v_r[k][...])
            d_o[k][...] = d
            m_o[k][...] = mm
            v_o[k][...] = vv

    vm = pl.BlockSpec(memory_space=pltpu.VMEM)
    shapes = [SDS(a.shape, f32) for a in ws]
    outs = pl.pallas_call(
        body, name="adamw_small", in_specs=[vm] * (4 * n), out_specs=[vm] * (3 * n), out_shape=shapes * 3,
    )(*ws, *gs, *ms, *vs)
    return outs[:n], outs[n:2 * n], outs[2 * n:]


_BIG = ("w_in", "w_out", "w_up", "w_down")
_WEIGHTS = ("norm1", "w_in", "attn_sinks", "conv_dw_w", "conv_dw_b", "conv_ln_g", "conv_ln_b", "lru_conv_w", "lru_conv_b",
            "lru_wa", "lru_ba", "lru_wx", "lru_bx", "lru_lambda", "mix_norm", "w_out", "norm2", "w_up", "w_down", "final_norm")
_SMALL = tuple(n for n in _WEIGHTS if n not in _BIG)
_SMALL_FULL_SHAPE = dict(
    norm1=(DEPTH, D_MODEL), attn_sinks=(DEPTH, N_HEADS), conv_dw_w=(DEPTH, CONV_K, CONV_W), conv_dw_b=(DEPTH, CONV_W),
    conv_ln_g=(DEPTH, CONV_W), conv_ln_b=(DEPTH, CONV_W), lru_conv_w=(DEPTH, LRU_K, LRU_W), lru_conv_b=(DEPTH, LRU_W),
    lru_wa=(DEPTH, LRU_HEADS, 64, 64), lru_ba=(DEPTH, LRU_HEADS, 64), lru_wx=(DEPTH, LRU_HEADS, 64, 64),
    lru_bx=(DEPTH, LRU_HEADS, 64), lru_lambda=(DEPTH, LRU_W), mix_norm=(DEPTH, D_MODEL), norm2=(DEPTH, D_MODEL),
    final_norm=(D_MODEL,))
_CHANNEL_SHARDED = ("conv_dw_w", "lru_conv_w")


def _pad_lanes(n):
    return -(-n // LANES) * LANES


def _pack(named):
    flat = []
    for a in named:
        a = a.reshape(-1)
        flat.append(jnp.pad(a, (0, _pad_lanes(a.shape[0]) - a.shape[0])))
    v = jnp.concatenate(flat)
    rows = -(-v.shape[0] // (8 * LANES)) * 8
    return jnp.pad(v, (0, rows * LANES - v.shape[0])).reshape(rows, LANES)


def _unpack(vec, shapes):
    flat = vec.reshape(-1)
    out, off = [], 0
    for shp in shapes:
        n = math.prod(shp)
        out.append(flat[off:off + n].reshape(shp))
        off += _pad_lanes(n)
    return out


def _as2d(a):
    return a.reshape(-1, a.shape[-1]) if a.ndim > 1 else a.reshape(1, -1)


def kernel(x, norm1, w_in, attn_sinks, conv_dw_w, conv_dw_b, conv_ln_g, conv_ln_b, lru_conv_w, lru_conv_b, lru_wa, lru_ba, lru_wx, lru_bx, lru_lambda, mix_norm, w_out, norm2, w_up, w_down, final_norm, loss_target, m_norm1, m_w_in, m_attn_sinks, m_conv_dw_w, m_conv_dw_b, m_conv_ln_g, m_conv_ln_b, m_lru_conv_w, m_lru_conv_b, m_lru_wa, m_lru_ba, m_lru_wx, m_lru_bx, m_lru_lambda, m_mix_norm, m_w_out, m_norm2, m_w_up, m_w_down, m_final_norm, v_norm1, v_w_in, v_attn_sinks, v_conv_dw_w, v_conv_dw_b, v_conv_ln_g, v_conv_ln_b, v_lru_conv_w, v_lru_conv_b, v_lru_wa, v_lru_ba, v_lru_wx, v_lru_bx, v_lru_lambda, v_mix_norm, v_w_out, v_norm2, v_w_up, v_w_down, v_final_norm):
    wts = dict(norm1=norm1, w_in=w_in, attn_sinks=attn_sinks, conv_dw_w=conv_dw_w, conv_dw_b=conv_dw_b, conv_ln_g=conv_ln_g,
               conv_ln_b=conv_ln_b, lru_conv_w=lru_conv_w, lru_conv_b=lru_conv_b, lru_wa=lru_wa, lru_ba=lru_ba, lru_wx=lru_wx,
               lru_bx=lru_bx, lru_lambda=lru_lambda, mix_norm=mix_norm, w_out=w_out, norm2=norm2, w_up=w_up, w_down=w_down,
               final_norm=final_norm)
    mom = dict(norm1=m_norm1, w_in=m_w_in, attn_sinks=m_attn_sinks, conv_dw_w=m_conv_dw_w, conv_dw_b=m_conv_dw_b,
               conv_ln_g=m_conv_ln_g, conv_ln_b=m_conv_ln_b, lru_conv_w=m_lru_conv_w, lru_conv_b=m_lru_conv_b, lru_wa=m_lru_wa,
               lru_ba=m_lru_ba, lru_wx=m_lru_wx, lru_bx=m_lru_bx, lru_lambda=m_lru_lambda, mix_norm=m_mix_norm, w_out=m_w_out,
               norm2=m_norm2, w_up=m_w_up, w_down=m_w_down, final_norm=m_final_norm)
    var = dict(norm1=v_norm1, w_in=v_w_in, attn_sinks=v_attn_sinks, conv_dw_w=v_conv_dw_w, conv_dw_b=v_conv_dw_b,
               conv_ln_g=v_conv_ln_g, conv_ln_b=v_conv_ln_b, lru_conv_w=v_lru_conv_w, lru_conv_b=v_lru_conv_b, lru_wa=v_lru_wa,
               lru_ba=v_lru_ba, lru_wx=v_lru_wx, lru_bx=v_lru_bx, lru_lambda=v_lru_lambda, mix_norm=v_mix_norm, w_out=v_w_out,
               norm2=v_norm2, w_up=v_w_up, w_down=v_w_down, final_norm=v_final_norm)

    c_idx = lax.axis_index("c").astype(jnp.int32)
    s_idx = (2 * lax.axis_index("x") + lax.axis_index("y")).astype(jnp.int32)
    idx = jnp.stack([c_idx, s_idx])

    cs = MeshWeights(w_in, w_out, w_up, w_down, conv_dw_w, lru_conv_w, idx)
    sp = {n: wts[n] for n in _SMALL}
    sp["conv_dw_w"], sp["lru_conv_w"] = cs.conv_weights()
    loss_blk, grad_x, tots, small_g, d_gf = train_local(x[0], loss_target[0], sp, cs)
    grads_big = {n: a.reshape(wts[n].shape) for n, a in zip(_BIG, halves_exchange(tots))}

    stacked = [jnp.stack([small_g[l][n] for l in range(DEPTH)]) for n in _SMALL if n != "final_norm"]
    packed = _pack(stacked + [d_gf, loss_blk[0, 0:1]])
    summed = small_allreduce(packed)
    names = [n for n in _SMALL if n != "final_norm"] + ["final_norm"]
    unpacked = _unpack(summed, [_SMALL_FULL_SHAPE[n] for n in names] + [(1,)])
    loss = unpacked[-1][0]
    grads = dict(zip(names, unpacked[:-1]))
    for n in _CHANNEL_SHARDED:
        width = wts[n].shape[-1]
        grads[n] = lax.dynamic_slice_in_dim(grads[n], s_idx * width, width, axis=2)
    grads.update(grads_big)

    delta, new_m, new_v = {}, {}, {}
    for n in _BIG:
        delta[n], new_m[n], new_v[n] = adamw_big(wts[n], grads[n], mom[n], var[n])
    sm = list(_SMALL)
    d_s, m_s, v_s = adamw_small([_as2d(wts[n]) for n in sm], [_as2d(grads[n]) for n in sm],
                                [_as2d(mom[n]) for n in sm], [_as2d(var[n]) for n in sm])
    for k, n in enumerate(sm):
        delta[n], new_m[n], new_v[n] = (a.reshape(wts[n].shape) for a in (d_s[k], m_s[k], v_s[k]))

    return (loss, grad_x[None], *[grads[n] for n in _WEIGHTS], *[delta[n] for n in _WEIGHTS],
            *[new_m[n] for n in _WEIGHTS], *[new_v[n] for n in _WEIGHTS])
```

```python
import functools
import math

import jax
import jax.numpy as jnp
from jax import lax
from jax.experimental import pallas as pl
from jax.experimental.pallas import tpu as pltpu

f32 = jnp.float32
bf16 = jnp.bfloat16
SDS = jax.ShapeDtypeStruct

D_MODEL = 1024
DEPTH = 2
ATTN_W = 512
KV_W = 128
HEAD_DIM = 64
N_HEADS = 8
BLOCK = 128
CONV_W = 256
CONV_K = 31
LRU_W = 256
LRU_K = 4
LRU_HEADS = 4
LRU_C = 8.0
IN_W = 1792
D_FF = 4096
N_SHARD = 4
FF_CHUNK = D_FF // N_SHARD
RMS_EPS = 1e-6
LN_EPS = 1e-5
MASK_VALUE = -1e30
HALO = 32
LANES = 128
VMEM_LIMIT = 56 * 1024 * 1024

ADAM_LR = 0.001
ADAM_B1 = 0.9
ADAM_B2 = 0.999
ADAM_EPS = 1e-08
ADAM_WD = 0.01
ADAM_STEP = 10

MESH = pl.DeviceIdType.MESH


def _dot(a, b):
    return jnp.dot(a, b, preferred_element_type=f32)


def _dot_nt(a, b):
    return lax.dot_general(a, b, (((1,), (1,)), ((), ())), preferred_element_type=f32)


def _dot_tn(a, b):
    return lax.dot_general(a, b, (((0,), (0,)), ((), ())), preferred_element_type=f32)


def _rms_fwd(x, g):
    r = lax.rsqrt(jnp.mean(x * x, axis=-1, keepdims=True) + RMS_EPS)
    return x * r * g, r


def _rms_bwd(dy, x, r, g):
    t = dy * g
    dx = r * t - x * (r * r * r) * jnp.mean(t * x, axis=-1, keepdims=True)
    dg = jnp.sum(dy * x * r, axis=0, keepdims=True)
    return dx, dg


def _sigmoid(x):
    return jax.nn.sigmoid(x)


_GELU_K = math.sqrt(2.0 / math.pi)


def _gelu(x):
    t = jnp.tanh(_GELU_K * (x + 0.044715 * x * x * x))
    return 0.5 * x * (1.0 + t), t


def _gelu_grad(x, t):
    return 0.5 * (1.0 + t) + 0.5 * x * (1.0 - t * t) * _GELU_K * (1.0 + 3.0 * 0.044715 * x * x)


def _log1p(x):
    return jnp.where(x < 1e-4, x - 0.5 * x * x, jnp.log(1.0 + x))


def _softplus(x):
    return jnp.maximum(x, 0.0) + _log1p(jnp.exp(-jnp.abs(x)))


def _neg_expm1(x):
    series = -x * (1.0 + 0.5 * x * (1.0 + x * (1.0 / 3.0) * (1.0 + 0.25 * x)))
    return jnp.where(x > -0.01, series, 1.0 - jnp.exp(x))


def _sublane_rolls(x, count, forward):
    n = x.shape[0]
    return [x if b == 0 else pltpu.roll(x, b if forward else n - b, 0) for b in range(count)]


def _conv_taps(xpad, w, k_width):
    t_rows = xpad.shape[0] - HALO
    rolled = _sublane_rolls(xpad, min(k_width, 8), forward=True)
    acc = None
    for k in range(k_width):
        hi, lo = divmod((k_width - 1) - k, 8)
        term = rolled[lo][HALO - 8 * hi:HALO - 8 * hi + t_rows] * w[k:k + 1, :]
        acc = term if acc is None else acc + term
    return acc


def _conv_taps_bwd(dpad, upad, w, k_width, t_rows):
    n_lo = min(k_width, 8)
    d_rolled = _sublane_rolls(dpad, n_lo, forward=False)
    u_rolled = _sublane_rolls(upad, n_lo, forward=True)
    d_in = None
    dw_rows = []
    d_out = dpad[:t_rows]
    for k in range(k_width):
        hi, lo = divmod((k_width - 1) - k, 8)
        term = d_rolled[lo][8 * hi:8 * hi + t_rows] * w[k:k + 1, :]
        d_in = term if d_in is None else d_in + term
        us = u_rolled[lo][HALO - 8 * hi:HALO - 8 * hi + t_rows]
        dw_rows.append(jnp.sum(d_out * us, axis=0, keepdims=True))
    return d_in, dw_rows


def _scan_fwd(a, b):
    t_rows = a.shape[0]
    row = lax.broadcasted_iota(jnp.int32, a.shape, 0)
    d = 1
    while d < t_rows:
        a_sh = jnp.where(row < d, 1.0, pltpu.roll(a, d, 0))
        b_sh = jnp.where(row < d, 0.0, pltpu.roll(b, d, 0))
        b = a * b_sh + b
        a = a * a_sh
        d *= 2
    return a, b


def _scan_bwd(a, b):
    t_rows = a.shape[0]
    row = lax.broadcasted_iota(jnp.int32, a.shape, 0)
    d = 1
    while d < t_rows:
        a_sh = jnp.where(row >= t_rows - d, 1.0, pltpu.roll(a, t_rows - d, 0))
        b_sh = jnp.where(row >= t_rows - d, 0.0, pltpu.roll(b, t_rows - d, 0))
        b = b + a * b_sh
        a = a * a_sh
        d *= 2
    return b


def _full(shape, single=False):
    nd = len(shape)
    if single:
        return pl.BlockSpec(shape, lambda *_: (0,) * nd, pipeline_mode=pl.Buffered(1))
    return pl.BlockSpec(shape, lambda *_: (0,) * nd)


def _params(sem, vmem=None):
    return pltpu.CompilerParams(dimension_semantics=sem, vmem_limit_bytes=vmem)


def _tile(s):
    return min(512, s)


def inproj_fwd(h, g1, w_in, pieces=None):
    s = h.shape[0]
    tm = _tile(s)

    def body(h_ref, g_ref, w_ref, z_ref):
        hn, _ = _rms_fwd(h_ref[...], g_ref[...])
        z_ref[...] = _dot(hn.astype(bf16), w_ref[...]).astype(bf16)

    return _pcall(
        body, pieces, name="inproj_fwd", grid=(s // tm,),
        in_specs=[pl.BlockSpec((tm, D_MODEL), lambda i: (i, 0)), _full((1, D_MODEL)), _full((D_MODEL, IN_W))],
        out_specs=[pl.BlockSpec((tm, IN_W), lambda i: (i, 0))],
        out_shape=[SDS((s, IN_W), bf16)],
        operands=[h, g1, w_in], sem=("parallel",), vmem=VMEM_LIMIT)[0]


ATT_ROWS = N_HEADS * BLOCK


def _attn_bias():
    qi = jnp.arange(BLOCK)[None, :]
    key = jnp.arange(2 * BLOCK)[:, None]
    band = (key > qi) & (key <= qi + BLOCK)
    first = band & (key >= BLOCK)
    mask = jnp.where(jnp.stack([first, band]), 0.0, MASK_VALUE).astype(f32)
    return jnp.tile(mask, (1, 1, N_HEADS // 2))


def _attn_band(kvc, kvp):
    kb = jnp.concatenate([kvp[:, :KV_W], kvc[:, :KV_W]], axis=0)
    vb = jnp.concatenate([kvp[:, KV_W:], kvc[:, KV_W:]], axis=0)
    lane = lax.broadcasted_iota(jnp.int32, kb.shape, 1)
    kb_sw = pltpu.roll(kb, HEAD_DIM, 1)
    vb_sw = pltpu.roll(vb, HEAD_DIM, 1)
    kx = [jnp.where(lane < HEAD_DIM, kb, kb_sw), jnp.where(lane >= HEAD_DIM, kb, kb_sw)]
    vx = [jnp.where(lane < HEAD_DIM, vb, vb_sw), jnp.where(lane >= HEAD_DIM, vb, vb_sw)]
    return kx, vx


def _stack_heads(x, mlo):
    zero = jnp.zeros((BLOCK, LANES), x.dtype)
    out = []
    for hk in range(2):
        parts = []
        for j in (2 * hk, 2 * hk + 1):
            xj = x[:, j * LANES:(j + 1) * LANES]
            parts += [jnp.where(mlo, xj, zero), jnp.where(mlo, zero, xj)]
        out.append(jnp.concatenate(parts, axis=0))
    return out


def _unstack_heads(y, mlo):
    cols = []
    for hk in range(2):
        for t in range(2):
            base = 2 * t * BLOCK
            cols.append(jnp.where(mlo, y[hk][base:base + BLOCK], y[hk][base + BLOCK:base + 2 * BLOCK]))
    return jnp.concatenate(cols, axis=1)


def _attn_probs(q4, kx, bias_t, sink_row):
    out = []
    half = ATT_ROWS // 2
    for hk in range(2):
        s = _dot_nt(kx[hk], q4[hk]) + bias_t
        sink = sink_row[:, hk * half:(hk + 1) * half]
        m = jnp.maximum(jnp.max(s, axis=0, keepdims=True), sink)
        p = jnp.exp(s - m)
        e_sink = jnp.exp(sink - m)
        inv = 1.0 / (jnp.sum(p, axis=0, keepdims=True) + e_sink)
        out.append((p * inv, e_sink * inv))
    return out


def attn_fwd(z, sink_col, bias, g_a, pieces=None):
    s = z.shape[0]
    nb = s // BLOCK

    def body(q_ref, kvc_ref, kvp_ref, sk_ref, b_ref, g_ref, o_ref, y_ref):
        n = pl.program_id(0)
        kx, vx = _attn_band(kvc_ref[...], kvp_ref[...])
        mlo = lax.broadcasted_iota(jnp.int32, (BLOCK, LANES), 1) < HEAD_DIM
        q4 = _stack_heads(q_ref[...] * (HEAD_DIM ** -0.5), mlo)
        probs = _attn_probs(q4, kx, b_ref[jnp.minimum(n, 1)], sk_ref[...])
        o = _unstack_heads([_dot_tn(probs[hk][0].astype(bf16), vx[hk]) for hk in range(2)], mlo)
        o_ref[...] = o.astype(bf16)
        y, _ = _rms_fwd(o, g_ref[...])
        y_ref[...] = y.astype(bf16)

    return _pcall(
        body, pieces, name="attn_fwd", grid=(nb,),
        in_specs=[pl.BlockSpec((BLOCK, ATTN_W), lambda n: (n, 0)),
                  pl.BlockSpec((BLOCK, 2 * KV_W), lambda n: (n, 2)),
                  pl.BlockSpec((BLOCK, 2 * KV_W), lambda n: (jnp.maximum(n - 1, 0), 2)),
                  _full((1, ATT_ROWS)), _full((2, 2 * BLOCK, ATT_ROWS // 2)), _full((1, ATTN_W))],
        out_specs=[pl.BlockSpec((BLOCK, ATTN_W), lambda n: (n, 0)), pl.BlockSpec((BLOCK, ATTN_W), lambda n: (n, 0))],
        out_shape=[SDS((s, ATTN_W), bf16), SDS((s, ATTN_W), bf16)],
        operands=[z, z, z, sink_col, bias, g_a], sem=("parallel",))


def _lru_gates(xc, wa, ba, wx, bx, lam):
    xcb = xc.astype(bf16)
    r = _sigmoid(_dot(xcb, wa) + ba)
    ig = _sigmoid(_dot(xcb, wx) + bx)
    sp = _softplus(-lam)
    la = (-LRU_C * r) * sp
    a = jnp.exp(la)
    mult = jnp.sqrt(_neg_expm1(2.0 * la))
    return r, ig, sp, la, a, mult


def branch_fwd(z, p, pieces=None):
    s = z.shape[0]
    tm = _tile(s)
    hb = tm // HALO

    def body(cv_ref, cg_ref, rx_ref, rg_ref, cvh_ref, cgh_ref, rxh_ref,
             cw_ref, cb_ref, lng_ref, lnb_ref, lw_ref, lb_ref, wa_ref, ba_ref, wx_ref, bx_ref, lam_ref, gc_ref, gl_ref,
             conv_ref, hst_ref, nc_ref, nl_ref, carry_ref):
        i = pl.program_id(0)
        first = i == 0

        @pl.when(first)
        def _():
            carry_ref[...] = jnp.zeros_like(carry_ref)

        cval = cv_ref[...].astype(f32)
        u = cval * _sigmoid(cg_ref[...].astype(f32))
        hu = jnp.where(first, 0.0, cvh_ref[...].astype(f32) * _sigmoid(cgh_ref[...].astype(f32)))
        conv = _conv_taps(jnp.concatenate([hu, u], axis=0), cw_ref[...], CONV_K) + cb_ref[...]
        conv_ref[...] = conv
        mu = jnp.mean(conv, axis=-1, keepdims=True)
        xm = conv - mu
        rstd = lax.rsqrt(jnp.mean(xm * xm, axis=-1, keepdims=True) + LN_EPS)
        ln = xm * rstd * lng_ref[...] + lnb_ref[...]
        yc = ln * _sigmoid(ln)
        nc, _ = _rms_fwd(yc, gc_ref[...])
        nc_ref[...] = nc.astype(bf16)

        rx = rx_ref[...].astype(f32)
        hrx = jnp.where(first, 0.0, rxh_ref[...].astype(f32))
        xc = _conv_taps(jnp.concatenate([hrx, rx], axis=0), lw_ref[...], LRU_K) + lb_ref[...]
        r, ig, sp, la, a, mult = _lru_gates(xc, wa_ref[...], ba_ref[...], wx_ref[...], bx_ref[...], lam_ref[...])
        gx = mult * (ig * xc)
        a_cum, h_loc = _scan_fwd(a, gx)
        hs = a_cum * carry_ref[0:1, :] + h_loc
        carry_ref[...] = jnp.broadcast_to(hs[tm - 1:tm, :], carry_ref.shape)
        hst_ref[...] = hs
        gl, _ = _gelu(rg_ref[...].astype(f32))
        nl, _ = _rms_fwd(hs * gl, gl_ref[...])
        nl_ref[...] = nl.astype(bf16)

    def col(c):
        return pl.BlockSpec((tm, CONV_W), lambda i: (i, c))

    def halo(c):
        return pl.BlockSpec((HALO, CONV_W), lambda i: (jnp.maximum(i * hb - 1, 0), c))

    small = [p["cw"], p["cb"], p["lng"], p["lnb"], p["lw"], p["lb"], p["wa"], p["ba"], p["wx"], p["bx"], p["lam"],
             p["gc"], p["gl"]]
    row = pl.BlockSpec((tm, CONV_W), lambda i: (i, 0))
    return _pcall(
        body, pieces, name="branch_fwd", grid=(s // tm,),
        in_specs=[col(3), col(4), col(5), col(6), halo(3), halo(4), halo(5)] + [_full(a.shape) for a in small],
        out_specs=[row, row, row, row],
        out_shape=[SDS((s, CONV_W), f32), SDS((s, LRU_W), f32), SDS((s, CONV_W), bf16), SDS((s, LRU_W), bf16)],
        scratch_shapes=[pltpu.VMEM((8, LRU_W), f32)],
        operands=[z, z, z, z, z, z, z, *small], sem=("arbitrary",))


def outproj_fwd(ya, yc, yl, h, w_out, g2, pieces=None):
    s = h.shape[0]
    tm = _tile(s)

    def body(ya_ref, yc_ref, yl_ref, h_ref, w_ref, g_ref, h1_ref, hn_ref):
        y = jnp.concatenate([ya_ref[...], yc_ref[...], yl_ref[...]], axis=1)
        h1 = h_ref[...] + _dot(y, w_ref[...])
        h1_ref[...] = h1
        hn, _ = _rms_fwd(h1, g_ref[...])
        hn_ref[...] = hn.astype(bf16)

    return _pcall(
        body, pieces, name="outproj_fwd", grid=(s // tm,),
        in_specs=[pl.BlockSpec((tm, ATTN_W), lambda i: (i, 0)), pl.BlockSpec((tm, CONV_W), lambda i: (i, 0)),
                  pl.BlockSpec((tm, LRU_W), lambda i: (i, 0)), pl.BlockSpec((tm, D_MODEL), lambda i: (i, 0)),
                  _full((D_MODEL, D_MODEL)), _full((1, D_MODEL))],
        out_specs=[pl.BlockSpec((tm, D_MODEL), lambda i: (i, 0)), pl.BlockSpec((tm, D_MODEL), lambda i: (i, 0))],
        out_shape=[SDS((s, D_MODEL), f32), SDS((s, D_MODEL), bf16)],
        operands=[ya, yc, yl, h, w_out, g2], sem=("parallel",), vmem=VMEM_LIMIT)


def mlp_fwd(hn2, h1, w_up, w_dn, pieces=None):
    s = h1.shape[0]
    tm = _tile(s)

    def body(x_ref, h_ref, wu_ref, wd_ref, up_ref, h2_ref):
        x = x_ref[...]
        acc = h_ref[...]
        for c in range(N_SHARD):
            u = _dot(x, wu_ref[c])
            up_ref[:, c * FF_CHUNK:(c + 1) * FF_CHUNK] = u.astype(bf16)
            act = jnp.square(jnp.maximum(u, 0.0)).astype(bf16)
            acc = acc + _dot(act, wd_ref[c])
        h2_ref[...] = acc

    return _pcall(
        body, pieces, name="mlp_fwd", grid=(s // tm,),
        in_specs=[pl.BlockSpec((tm, D_MODEL), lambda i: (i, 0)), pl.BlockSpec((tm, D_MODEL), lambda i: (i, 0)),
                  _full((N_SHARD, D_MODEL, FF_CHUNK), single=True), _full((N_SHARD, FF_CHUNK, D_MODEL), single=True)],
        out_specs=[pl.BlockSpec((tm, D_FF), lambda i: (i, 0)), pl.BlockSpec((tm, D_MODEL), lambda i: (i, 0))],
        out_shape=[SDS((s, D_FF), bf16), SDS((s, D_MODEL), f32)],
        operands=[hn2, h1, w_up, w_dn], sem=("parallel",), vmem=VMEM_LIMIT)


def final_loss(h, tgt, gf):
    s = h.shape[0]
    tm = _tile(s)

    def body(h_ref, t_ref, g_ref, dh_ref, loss_ref, dg_ref):
        i = pl.program_id(0)

        @pl.when(i == 0)
        def _():
            loss_ref[...] = jnp.zeros_like(loss_ref)
            dg_ref[...] = jnp.zeros_like(dg_ref)

        x = h_ref[...]
        g = g_ref[...]
        y, r = _rms_fwd(x, g)
        err = y - t_ref[...]
        part = 0.5 * jnp.sum(jnp.mean(err * err, axis=-1, keepdims=True), axis=0, keepdims=True)
        loss_ref[...] += jnp.broadcast_to(part, loss_ref.shape)
        dx, dg = _rms_bwd(err * (1.0 / D_MODEL), x, r, g)
        dh_ref[...] = dx
        dg_ref[...] += dg

    return pl.pallas_call(
        body, name="final_loss", grid=(s // tm,),
        in_specs=[pl.BlockSpec((tm, D_MODEL), lambda i: (i, 0)), pl.BlockSpec((tm, D_MODEL), lambda i: (i, 0)),
                  _full((1, D_MODEL))],
        out_specs=[pl.BlockSpec((tm, D_MODEL), lambda i: (i, 0)), _full((8, LANES)), _full((1, D_MODEL))],
        out_shape=[SDS((s, D_MODEL), f32), SDS((8, LANES), f32), SDS((1, D_MODEL), f32)],
        compiler_params=_params(("arbitrary",)),
    )(h, tgt, gf)


def mlp_bwd_act(dh, up, h1, g2, w_up, w_dn, pieces=None):
    s = dh.shape[0]
    tm = _tile(s)

    def body(dh_ref, up_ref, h1_ref, g_ref, wu_ref, wd_ref, dup_ref, dh1_ref, dg_ref):
        i = pl.program_id(0)

        @pl.when(i == 0)
        def _():
            dg_ref[...] = jnp.zeros_like(dg_ref)

        dh = dh_ref[...]
        dhb = dh.astype(bf16)
        d_hn = jnp.zeros((tm, D_MODEL), f32)
        for c in range(N_SHARD):
            d_act = _dot_nt(dhb, wd_ref[c])
            u = up_ref[:, c * FF_CHUNK:(c + 1) * FF_CHUNK].astype(f32)
            d_u = (d_act * (2.0 * jnp.maximum(u, 0.0))).astype(bf16)
            dup_ref[:, c * FF_CHUNK:(c + 1) * FF_CHUNK] = d_u
            d_hn = d_hn + _dot_nt(d_u, wu_ref[c])
        x = h1_ref[...]
        g = g_ref[...]
        _, r = _rms_fwd(x, g)
        dx, dg = _rms_bwd(d_hn, x, r, g)
        dh1_ref[...] = dh + dx
        dg_ref[...] += dg

    return _pcall(
        body, pieces, name="mlp_bwd_act", grid=(s // tm,),
        in_specs=[pl.BlockSpec((tm, D_MODEL), lambda i: (i, 0)), pl.BlockSpec((tm, D_FF), lambda i: (i, 0)),
                  pl.BlockSpec((tm, D_MODEL), lambda i: (i, 0)), _full((1, D_MODEL)),
                  _full((N_SHARD, D_MODEL, FF_CHUNK), single=True), _full((N_SHARD, FF_CHUNK, D_MODEL), single=True)],
        out_specs=[pl.BlockSpec((tm, D_FF), lambda i: (i, 0)), pl.BlockSpec((tm, D_MODEL), lambda i: (i, 0)),
                   _full((1, D_MODEL))],
        out_shape=[SDS((s, D_FF), bf16), SDS((s, D_MODEL), f32), SDS((1, D_MODEL), f32)],
        operands=[dh, up, h1, g2, w_up, w_dn], sem=("arbitrary",), vmem=VMEM_LIMIT)


def mlp_bwd_w(hn2, d_up, up, dh, pieces=None):
    s = dh.shape[0]
    tk = _tile(s)

    def body(x_ref, du_ref, up_ref, dh_ref, dwu_ref, dwd_ref):
        k = pl.program_id(1)

        @pl.when(k == 0)
        def _():
            dwu_ref[...] = jnp.zeros_like(dwu_ref)
            dwd_ref[...] = jnp.zeros_like(dwd_ref)

        dwu_ref[0] += _dot_tn(x_ref[...], du_ref[...])
        act = jnp.square(jnp.maximum(up_ref[...].astype(f32), 0.0)).astype(bf16)
        dwd_ref[0] += _dot_tn(act, dh_ref[...].astype(bf16))

    return _pcall(
        body, pieces, name="mlp_bwd_w", grid=(N_SHARD, s // tk),
        in_specs=[pl.BlockSpec((tk, D_MODEL), lambda c, k: (k, 0)), pl.BlockSpec((tk, FF_CHUNK), lambda c, k: (k, c)),
                  pl.BlockSpec((tk, FF_CHUNK), lambda c, k: (k, c)), pl.BlockSpec((tk, D_MODEL), lambda c, k: (k, 0))],
        out_specs=[pl.BlockSpec((1, D_MODEL, FF_CHUNK), lambda c, k: (c, 0, 0)),
                   pl.BlockSpec((1, FF_CHUNK, D_MODEL), lambda c, k: (c, 0, 0))],
        out_shape=[SDS((N_SHARD, D_MODEL, FF_CHUNK), f32), SDS((N_SHARD, FF_CHUNK, D_MODEL), f32)],
        operands=[hn2, d_up, up, dh], sem=("parallel", "arbitrary"), vmem=VMEM_LIMIT)


def outproj_bwd(dh1, ya, yc, yl, w_out):
    s = dh1.shape[0]
    tm = _tile(s)

    def body(dh_ref, ya_ref, yc_ref, yl_ref, w_ref, dy_ref, dw_ref):
        i = pl.program_id(0)

        @pl.when(i == 0)
        def _():
            dw_ref[...] = jnp.zeros_like(dw_ref)

        dhb = dh_ref[...].astype(bf16)
        dy_ref[...] = _dot_nt(dhb, w_ref[...])
        y = jnp.concatenate([ya_ref[...], yc_ref[...], yl_ref[...]], axis=1)
        dw_ref[...] += _dot_tn(y, dhb)

    return pl.pallas_call(
        body, name="outproj_bwd", grid=(s // tm,),
        in_specs=[pl.BlockSpec((tm, D_MODEL), lambda i: (i, 0)), pl.BlockSpec((tm, ATTN_W), lambda i: (i, 0)),
                  pl.BlockSpec((tm, CONV_W), lambda i: (i, 0)), pl.BlockSpec((tm, LRU_W), lambda i: (i, 0)),
                  _full((D_MODEL, D_MODEL))],
        out_specs=[pl.BlockSpec((tm, D_MODEL), lambda i: (i, 0)), _full((D_MODEL, D_MODEL))],
        out_shape=[SDS((s, D_MODEL), f32), SDS((D_MODEL, D_MODEL), f32)],
        compiler_params=_params(("arbitrary",), VMEM_LIMIT),
    )(dh1, ya, yc, yl, w_out)


def attn_bwd(z, o, dy, sink_col, bias, g_a, pieces=None):
    s = z.shape[0]
    nb = s // BLOCK

    def body(q_ref, kvc_ref, kvp_ref, o_ref, dy_ref, sk_ref, b_ref, g_ref, dq_ref, dkv_ref, dsk_ref, dg_ref,
             pend_ref, dsk_acc):
        n = pl.program_id(0)

        @pl.when(n == 0)
        def _():
            pend_ref[...] = jnp.zeros_like(pend_ref)
            dsk_acc[...] = jnp.zeros_like(dsk_acc)
            dg_ref[...] = jnp.zeros_like(dg_ref)

        @pl.when(n < nb)
        def _():
            kx, vx = _attn_band(kvc_ref[...], kvp_ref[...])
            mlo = lax.broadcasted_iota(jnp.int32, (BLOCK, LANES), 1) < HEAD_DIM
            scale = HEAD_DIM ** -0.5
            q4 = _stack_heads(q_ref[...] * scale, mlo)
            o_f = o_ref[...].astype(f32)
            g = g_ref[...]
            _, r = _rms_fwd(o_f, g)
            d_o, dg = _rms_bwd(dy_ref[...], o_f, r, g)
            dg_ref[...] += dg
            do4 = _stack_heads(d_o.astype(bf16), mlo)
            probs = _attn_probs(q4, kx, b_ref[jnp.minimum(n, 1)], sk_ref[...])
            half = ATT_ROWS // 2
            dq4, tk, tv = [], [], []
            for hk in range(2):
                pr, p_sink = probs[hk]
                d_p = _dot_nt(vx[hk], do4[hk])
                d_row = jnp.sum(pr * d_p, axis=0, keepdims=True)
                d_s = (pr * (d_p - d_row)).astype(bf16)
                dsk_acc[:, hk * half:(hk + 1) * half] -= p_sink * d_row
                dq4.append(_dot_tn(d_s, kx[hk]))
                tk.append(_dot(d_s, q4[hk]))
                tv.append(_dot(pr.astype(bf16), do4[hk]))
            dq_ref[...] = (_unstack_heads(dq4, mlo) * scale).astype(bf16)
            lane = lax.broadcasted_iota(jnp.int32, (2 * BLOCK, LANES), 1)
            fk = [t + pltpu.roll(t, HEAD_DIM, 1) for t in tk]
            fv = [t + pltpu.roll(t, HEAD_DIM, 1) for t in tv]
            band = jnp.concatenate([jnp.where(lane < HEAD_DIM, fk[0], fk[1]), jnp.where(lane < HEAD_DIM, fv[0], fv[1])], axis=1)
            dkv_ref[...] = (pend_ref[...] + band[:BLOCK]).astype(bf16)
            pend_ref[...] = band[BLOCK:]

        @pl.when(n == nb)
        def _():
            dkv_ref[...] = pend_ref[...].astype(bf16)
            for hh in range(N_HEADS):
                tot = jnp.sum(dsk_acc[:, hh * BLOCK:(hh + 1) * BLOCK], axis=1, keepdims=True)
                dsk_ref[hh:hh + 1, :] = jnp.broadcast_to(tot, (1, LANES))

    def cur(n):
        return jnp.minimum(n, nb - 1)

    return _pcall(
        body, pieces, name="attn_bwd", grid=(nb + 1,),
        in_specs=[pl.BlockSpec((BLOCK, ATTN_W), lambda n: (cur(n), 0)),
                  pl.BlockSpec((BLOCK, 2 * KV_W), lambda n: (cur(n), 2)),
                  pl.BlockSpec((BLOCK, 2 * KV_W), lambda n: (jnp.maximum(cur(n) - 1, 0), 2)),
                  pl.BlockSpec((BLOCK, ATTN_W), lambda n: (cur(n), 0)),
                  pl.BlockSpec((BLOCK, ATTN_W), lambda n: (cur(n), 0)),
                  _full((1, ATT_ROWS)), _full((2, 2 * BLOCK, ATT_ROWS // 2)), _full((1, ATTN_W))],
        out_specs=[pl.BlockSpec((BLOCK, ATTN_W), lambda n: (cur(n), 0)),
                   pl.BlockSpec((BLOCK, 2 * KV_W), lambda n: (jnp.maximum(n - 1, 0), 0)),
                   _full((N_HEADS, LANES)), _full((1, ATTN_W))],
        out_shape=[SDS((s, ATTN_W), bf16), SDS((s, 2 * KV_W), bf16), SDS((N_HEADS, LANES), f32), SDS((1, ATTN_W), f32)],
        scratch_shapes=[pltpu.VMEM((BLOCK, 2 * KV_W), f32), pltpu.VMEM((1, ATT_ROWS), f32)],
        operands=[z, z, z, o, dy, sink_col, bias, g_a], sem=("arbitrary",))


_V_GC, _V_LNG, _V_LNB, _V_CB, _V_GL, _V_BA, _V_BX, _V_LAM, _V_LB = range(9)
_V_ROWS = 16


def branch_bwd_a(z, conv, hst, dy, p, pieces=None):
    s = z.shape[0]
    tm = _tile(s)
    nt = s // tm
    hb = tm // HALO
    h8 = tm // 8

    def body(conv_ref, dyc_ref, dyl_ref, rx_ref, rxh_ref, rg_ref, hst_ref, hsth_ref,
             lng_ref, lnb_ref, lw_ref, lb_ref, wa_ref, ba_ref, wx_ref, bx_ref, lam_ref, gc_ref, gl_ref,
             dconv_ref, dxc_ref, drg_ref, vec_ref, dwa_ref, dwx_ref, carry_ref):
        i = pl.program_id(0)
        ti = nt - 1 - i

        @pl.when(i == 0)
        def _():
            carry_ref[...] = jnp.zeros_like(carry_ref)
            vec_ref[...] = jnp.zeros_like(vec_ref)
            dwa_ref[...] = jnp.zeros_like(dwa_ref)
            dwx_ref[...] = jnp.zeros_like(dwx_ref)

        conv = conv_ref[...]
        mu = jnp.mean(conv, axis=-1, keepdims=True)
        xm = conv - mu
        rstd = lax.rsqrt(jnp.mean(xm * xm, axis=-1, keepdims=True) + LN_EPS)
        xhat = xm * rstd
        lng = lng_ref[...]
        ln = xhat * lng + lnb_ref[...]
        sg = _sigmoid(ln)
        yc = ln * sg
        gc = gc_ref[...]
        _, rc = _rms_fwd(yc, gc)
        d_yc, d_gc = _rms_bwd(dyc_ref[...], yc, rc, gc)
        d_ln = d_yc * (sg * (1.0 + ln * (1.0 - sg)))
        d_xhat = d_ln * lng
        d_conv = rstd * (d_xhat - jnp.mean(d_xhat, axis=-1, keepdims=True)
                         - xhat * jnp.mean(d_xhat * xhat, axis=-1, keepdims=True))
        dconv_ref[...] = d_conv

        rx = rx_ref[...].astype(f32)
        hrx = jnp.where(ti == 0, 0.0, rxh_ref[...].astype(f32))
        xc = _conv_taps(jnp.concatenate([hrx, rx], axis=0), lw_ref[...], LRU_K) + lb_ref[...]
        wa = wa_ref[...]
        wx = wx_ref[...]
        lam = lam_ref[...]
        r, ig, sp, la, a, mult = _lru_gates(xc, wa, ba_ref[...], wx, bx_ref[...], lam)
        hs = hst_ref[...]
        row = lax.broadcasted_iota(jnp.int32, hs.shape, 0)
        h_before = jnp.where(ti == 0, 0.0, hsth_ref[7:8, :])
        h_prev = jnp.where(row == 0, h_before, pltpu.roll(hs, 1, 0))
        rg = rg_ref[...].astype(f32)
        gl, tg = _gelu(rg)
        out = hs * gl
        gmix = gl_ref[...]
        _, rl = _rms_fwd(out, gmix)
        d_out, d_gl = _rms_bwd(dyl_ref[...], out, rl, gmix)
        drg_ref[...] = (d_out * hs * _gelu_grad(rg, tg)).astype(bf16)
        d_h = d_out * gl
        last = row == tm - 1
        a_next = jnp.where(last, 0.0, pltpu.roll(a, tm - 1, 0))
        lmb = _scan_bwd(a_next, d_h + jnp.where(last, carry_ref[0:1, :], 0.0))
        carry_ref[...] = jnp.broadcast_to(a[0:1, :] * lmb[0:1, :], carry_ref.shape)
        d_a = lmb * h_prev
        d_mult = lmb * (ig * xc)
        d_ig = lmb * (mult * xc)
        d_la = d_a * a - d_mult * (a * a) / jnp.maximum(mult, 1e-30)
        d_pa = (d_la * (-LRU_C * sp)) * (r * (1.0 - r))
        d_px = d_ig * (ig * (1.0 - ig))
        d_pab = d_pa.astype(bf16)
        d_pxb = d_px.astype(bf16)
        d_xc = lmb * (mult * ig) + _dot_nt(d_pab, wa) + _dot_nt(d_pxb, wx)
        dxc_ref[...] = d_xc
        xcb = xc.astype(bf16)
        dwa_ref[...] += _dot_tn(xcb, d_pab)
        dwx_ref[...] += _dot_tn(xcb, d_pxb)
        d_lam = jnp.sum(d_la * (-LRU_C * r), axis=0, keepdims=True) * (-_sigmoid(-lam))

        def colsum(v):
            return jnp.sum(v, axis=0, keepdims=True)

        rows = [None] * _V_ROWS
        rows[_V_GC] = d_gc
        rows[_V_LNG] = colsum(d_ln * xhat)
        rows[_V_LNB] = colsum(d_ln)
        rows[_V_CB] = colsum(d_conv)
        rows[_V_GL] = d_gl
        rows[_V_BA] = colsum(d_pa)
        rows[_V_BX] = colsum(d_px)
        rows[_V_LAM] = d_lam
        rows[_V_LB] = colsum(d_xc)
        zero = jnp.zeros((1, CONV_W), f32)
        vec_ref[...] += jnp.concatenate([zero if v is None else v for v in rows], axis=0)

    def rev(c):
        return pl.BlockSpec((tm, CONV_W), lambda i: (nt - 1 - i, c))

    small = [p["lng"], p["lnb"], p["lw"], p["lb"], p["wa"], p["ba"], p["wx"], p["bx"], p["lam"], p["gc"], p["gl"]]
    return _pcall(
        body, pieces, name="branch_bwd_a", grid=(nt,),
        in_specs=[rev(0), rev(2), rev(3), rev(5),
                  pl.BlockSpec((HALO, CONV_W), lambda i: (jnp.maximum((nt - 1 - i) * hb - 1, 0), 5)),
                  rev(6), rev(0),
                  pl.BlockSpec((8, LRU_W), lambda i: (jnp.maximum((nt - 1 - i) * h8 - 1, 0), 0))]
                 + [_full(a.shape) for a in small],
        out_specs=[rev(0), rev(0), rev(0), _full((_V_ROWS, CONV_W)), _full((LRU_W, LRU_W)), _full((LRU_W, LRU_W))],
        out_shape=[SDS((s, CONV_W), f32), SDS((s, LRU_W), f32), SDS((s, LRU_W), bf16), SDS((_V_ROWS, CONV_W), f32),
                   SDS((LRU_W, LRU_W), f32), SDS((LRU_W, LRU_W), f32)],
        scratch_shapes=[pltpu.VMEM((8, LRU_W), f32)],
        operands=[conv, dy, dy, z, z, z, hst, hst, *small], sem=("arbitrary",))


def branch_bwd_b(z, d_conv, d_xc, p, pieces=None):
    s = z.shape[0]
    tm = _tile(s)
    nt = s // tm
    hb = tm // HALO

    def body(cv_ref, cg_ref, cvh_ref, cgh_ref, rx_ref, rxh_ref, dc_ref, dch_ref, dx_ref, dxh_ref, cw_ref, lw_ref,
             dzc_ref, dzr_ref, dcw_ref, dlw_ref):
        i = pl.program_id(0)

        @pl.when(i == 0)
        def _():
            dcw_ref[...] = jnp.zeros_like(dcw_ref)
            dlw_ref[...] = jnp.zeros_like(dlw_ref)

        first = i == 0
        last = i == nt - 1
        cval = cv_ref[...].astype(f32)
        sg = _sigmoid(cg_ref[...].astype(f32))
        u = cval * sg
        hu = jnp.where(first, 0.0, cvh_ref[...].astype(f32) * _sigmoid(cgh_ref[...].astype(f32)))
        dpad = jnp.concatenate([dc_ref[...], jnp.where(last, 0.0, dch_ref[...])], axis=0)
        d_u, dw_rows = _conv_taps_bwd(dpad, jnp.concatenate([hu, u], axis=0), cw_ref[...], CONV_K, tm)
        dcw_ref[...] += jnp.concatenate(dw_rows + [jnp.zeros((HALO - CONV_K, CONV_W), f32)], axis=0)
        dzc_ref[...] = jnp.concatenate([d_u * sg, d_u * cval * sg * (1.0 - sg)], axis=1).astype(bf16)

        rx = rx_ref[...].astype(f32)
        hrx = jnp.where(first, 0.0, rxh_ref[...].astype(f32))
        dxpad = jnp.concatenate([dx_ref[...], jnp.where(last, 0.0, dxh_ref[...])], axis=0)
        d_rx, dlw_rows = _conv_taps_bwd(dxpad, jnp.concatenate([hrx, rx], axis=0), lw_ref[...], LRU_K, tm)
        dlw_ref[...] += jnp.concatenate(dlw_rows + [jnp.zeros((8 - LRU_K, LRU_W), f32)], axis=0)
        dzr_ref[...] = d_rx.astype(bf16)

    def col(c):
        return pl.BlockSpec((tm, CONV_W), lambda i: (i, c))

    def prev(c):
        return pl.BlockSpec((HALO, CONV_W), lambda i: (jnp.maximum(i * hb - 1, 0), c))

    nxt = pl.BlockSpec((HALO, CONV_W), lambda i: (jnp.minimum((i + 1) * hb, nt * hb - 1), 0))
    return _pcall(
        body, pieces, name="branch_bwd_b", grid=(nt,),
        in_specs=[col(3), col(4), prev(3), prev(4), col(5), prev(5), col(0), nxt, col(0), nxt,
                  _full(p["cw"].shape), _full(p["lw"].shape)],
        out_specs=[pl.BlockSpec((tm, 2 * CONV_W), lambda i: (i, 0)), pl.BlockSpec((tm, LRU_W), lambda i: (i, 0)),
                   _full((HALO, CONV_W)), _full((8, LRU_W))],
        out_shape=[SDS((s, 2 * CONV_W), bf16), SDS((s, LRU_W), bf16), SDS((HALO, CONV_W), f32), SDS((8, LRU_W), f32)],
        operands=[z, z, z, z, z, z, d_conv, d_conv, d_xc, d_xc, p["cw"], p["lw"]], sem=("arbitrary",))


def inproj_bwd(dq, dkv, dzc, dzr, drg, h, g1, w_in, dh1, pieces=None):
    s = h.shape[0]
    tm = _tile(s)

    def body(dq_ref, dkv_ref, dzc_ref, dzr_ref, drg_ref, h_ref, g_ref, w_ref, dh1_ref, dh_ref, dw_ref, dg_ref):
        i = pl.program_id(0)

        @pl.when(i == 0)
        def _():
            dw_ref[...] = jnp.zeros_like(dw_ref)
            dg_ref[...] = jnp.zeros_like(dg_ref)

        dz = jnp.concatenate([dq_ref[...], dkv_ref[...], dzc_ref[...], dzr_ref[...], drg_ref[...]], axis=1)
        x = h_ref[...]
        g = g_ref[...]
        hn, r = _rms_fwd(x, g)
        d_hn = _dot_nt(dz, w_ref[...])
        dw_ref[...] += _dot_tn(hn.astype(bf16), dz)
        dx, dg = _rms_bwd(d_hn, x, r, g)
        dh_ref[...] = dh1_ref[...] + dx
        dg_ref[...] += dg

    def rowb(w):
        return pl.BlockSpec((tm, w), lambda i: (i, 0))

    return _pcall(
        body, pieces, name="inproj_bwd", grid=(s // tm,),
        in_specs=[rowb(ATTN_W), rowb(2 * KV_W), rowb(2 * CONV_W), rowb(LRU_W), rowb(LRU_W), rowb(D_MODEL),
                  _full((1, D_MODEL)), _full((D_MODEL, IN_W)), rowb(D_MODEL)],
        out_specs=[rowb(D_MODEL), _full((D_MODEL, IN_W)), _full((1, D_MODEL))],
        out_shape=[SDS((s, D_MODEL), f32), SDS((D_MODEL, IN_W), f32), SDS((1, D_MODEL), f32)],
        operands=[dq, dkv, dzc, dzr, drg, h, g1, w_in, dh1], sem=("arbitrary",), vmem=VMEM_LIMIT)


def _block_diag(w):
    out = jnp.zeros((LRU_W, LRU_W), w.dtype)
    hd = LRU_W // LRU_HEADS
    for hh in range(LRU_HEADS):
        out = out.at[hh * hd:(hh + 1) * hd, hh * hd:(hh + 1) * hd].set(w[hh])
    return out


def _diag_blocks(w):
    hd = LRU_W // LRU_HEADS
    return jnp.stack([w[hh * hd:(hh + 1) * hd, hh * hd:(hh + 1) * hd] for hh in range(LRU_HEADS)])


def _layer_params(sp, l):
    mix = sp["mix_norm"][l]
    return dict(
        g1=sp["norm1"][l][None, :], g2=sp["norm2"][l][None, :],
        sinks=jnp.repeat(sp["attn_sinks"][l], BLOCK)[None, :],
        ga=mix[None, :ATTN_W], gc=mix[None, ATTN_W:ATTN_W + CONV_W], gl=mix[None, ATTN_W + CONV_W:],
        cw=jnp.pad(sp["conv_dw_w"][l], ((0, HALO - CONV_K), (0, 0))), cb=sp["conv_dw_b"][l][None, :],
        lng=sp["conv_ln_g"][l][None, :], lnb=sp["conv_ln_b"][l][None, :],
        lw=jnp.pad(sp["lru_conv_w"][l], ((0, 8 - LRU_K), (0, 0))), lb=sp["lru_conv_b"][l][None, :],
        wa=_block_diag(sp["lru_wa"][l]).astype(bf16), ba=sp["lru_ba"][l].reshape(1, LRU_W),
        wx=_block_diag(sp["lru_wx"][l]).astype(bf16), bx=sp["lru_bx"][l].reshape(1, LRU_W),
        lam=sp["lru_lambda"][l][None, :],
    )


def local_step(x, tgt, big, sp):
    return train_local(x, tgt, sp, LocalWeights(big))


class LocalWeights:
    def __init__(self, big):
        self.big = big
        self.grads = [dict() for _ in range(DEPTH)]

    def weight(self, name, l):
        return self.big[l][name]

    def host(self, point, l):
        return None

    def grad(self, name, l, g):
        self.grads[l][name] = g

    def big_grads(self):
        return self.grads


def train_local(x, tgt, sp, cs):
    lp = [_layer_params(sp, l) for l in range(DEPTH)]
    bias = _attn_bias()
    saved = []
    h = x
    for l in range(DEPTH):
        p = lp[l]
        z = inproj_fwd(h, p["g1"], cs.weight("w_in", l), cs.host("inproj_fwd", l))
        o, ya = attn_fwd(z, p["sinks"], bias, p["ga"], cs.host("attn_fwd", l))
        conv, hst, yc, yl = branch_fwd(z, p, cs.host("branch_fwd", l))
        h1, hn2 = outproj_fwd(ya, yc, yl, h, cs.weight("w_out", l), p["g2"], cs.host("outproj_fwd", l))
        up, h2 = mlp_fwd(hn2, h1, cs.weight("w_up", l), cs.weight("w_dn", l), cs.host("mlp_fwd", l))
        saved.append(dict(h=h, z=z, o=o, ya=ya, conv=conv, hst=hst, yc=yc, yl=yl, h1=h1, hn2=hn2, up=up))
        h = h2
    dh, loss, d_gf = final_loss(h, tgt, sp["final_norm"][None, :])
    small_g = [None] * DEPTH
    for l in reversed(range(DEPTH)):
        p, sv = lp[l], saved[l]
        w_up, w_dn = cs.weight("w_up", l), cs.weight("w_dn", l)
        d_up, dh1, d_g2 = mlp_bwd_act(dh, sv["up"], sv["h1"], p["g2"], w_up, w_dn, cs.host("mlp_bwd_act", l))
        dw_up, dw_dn = mlp_bwd_w(sv["hn2"], d_up, sv["up"], dh, cs.host("mlp_bwd_w", l))
        cs.grad("w_up", l, dw_up)
        cs.grad("w_dn", l, dw_dn)
        dy, dw_out = outproj_bwd(dh1, sv["ya"], sv["yc"], sv["yl"], cs.weight("w_out", l))
        cs.grad("w_out", l, dw_out)
        dq, dkv, d_sk, d_ga = attn_bwd(sv["z"], sv["o"], dy, p["sinks"], bias, p["ga"], cs.host("attn_bwd", l))
        d_conv, d_xc, d_rg, vec, dwa, dwx = branch_bwd_a(sv["z"], sv["conv"], sv["hst"], dy, p, cs.host("branch_bwd_a", l))
        dzc, dzr, dcw, dlw = branch_bwd_b(sv["z"], d_conv, d_xc, p, cs.host("branch_bwd_b", l))
        dh, dw_in, d_g1 = inproj_bwd(dq, dkv, dzc, dzr, d_rg, sv["h"], p["g1"], cs.weight("w_in", l), dh1,
                                     cs.host("inproj_bwd", l))
        cs.grad("w_in", l, dw_in)
        hd = LRU_W // LRU_HEADS
        small_g[l] = dict(
            norm1=d_g1[0], attn_sinks=d_sk[:, 0], conv_dw_w=dcw[:CONV_K], conv_dw_b=vec[_V_CB],
            conv_ln_g=vec[_V_LNG], conv_ln_b=vec[_V_LNB], lru_conv_w=dlw[:LRU_K], lru_conv_b=vec[_V_LB],
            lru_wa=_diag_blocks(dwa), lru_ba=vec[_V_BA].reshape(LRU_HEADS, hd),
            lru_wx=_diag_blocks(dwx), lru_bx=vec[_V_BX].reshape(LRU_HEADS, hd), lru_lambda=vec[_V_LAM],
            mix_norm=jnp.concatenate([d_ga[0], vec[_V_GC], vec[_V_GL]]), norm2=d_g2[0],
        )
    return loss, dh, cs.big_grads(), small_g, d_gf[0]


_HBM = pl.BlockSpec(memory_space=pl.ANY)


def _place():
    x, y, c = lax.axis_index("x"), lax.axis_index("y"), lax.axis_index("c")
    chips = [(1 - x, y), (x, 1 - y), (1 - x, 1 - y)]
    return x, y, c, chips


class Comm:
    def __init__(self, ins, out_shape, aliases, sems, start, finish, done):
        self.ins, self.out_shape, self.aliases, self.sems = list(ins), list(out_shape), dict(aliases), list(sems)
        self.start, self.finish, self.done = start, finish, done


def _pcall(body, pieces, *, name, grid, in_specs, out_specs, out_shape, operands, scratch_shapes=(), sem, vmem=None):
    in_specs, out_specs, out_shape, scratch_shapes = list(in_specs), list(out_specs), list(out_shape), list(scratch_shapes)
    if not pieces:
        return pl.pallas_call(body, name=name, grid=grid, in_specs=in_specs, out_specs=out_specs, out_shape=out_shape,
                              scratch_shapes=scratch_shapes, compiler_params=_params(sem, vmem))(*operands)
    n_in, n_out, n_scr = len(in_specs), len(out_specs), len(scratch_shapes)
    c_ins = [a for p in pieces for a in p.ins]
    c_outs = [s for p in pieces for s in p.out_shape]
    c_sems = [n for p in pieces for n in p.sems]
    aliases, spans, ki, ko, ks = {}, [], 0, 0, 0
    for p in pieces:
        spans.append((ki, ko, ks))
        for a, b in p.aliases.items():
            aliases[n_in + ki + a] = n_out + ko + b
        ki, ko, ks = ki + len(p.ins), ko + len(p.out_shape), ks + len(p.sems)

    def hosted(*refs):
        ins, cin = refs[:n_in], refs[n_in:n_in + ki]
        outs, cout = refs[n_in + ki:n_in + ki + n_out], refs[n_in + ki + n_out:n_in + ki + n_out + ko]
        scr, csem = refs[n_in + ki + n_out + ko:n_in + ki + n_out + ko + n_scr], refs[n_in + ki + n_out + ko + n_scr:]
        first = functools.reduce(jnp.logical_and, [pl.program_id(d) == 0 for d in range(len(grid))])
        last = functools.reduce(jnp.logical_and, [pl.program_id(d) == grid[d] - 1 for d in range(len(grid))])

        def each(which):
            for p, (a, b, s) in zip(pieces, spans):
                getattr(p, which)(cin[a:a + len(p.ins)], cout[b:b + len(p.out_shape)], csem[s:s + len(p.sems)])

        @pl.when(first)
        def _():
            each("start")

        body(*ins, *outs, *scr)

        @pl.when(last)
        def _():
            each("finish")

    res = pl.pallas_call(
        hosted, name=name + "_host", grid=grid, in_specs=in_specs + [_HBM] * ki, out_specs=out_specs + [_HBM] * ko,
        out_shape=out_shape + c_outs, scratch_shapes=scratch_shapes + [pltpu.SemaphoreType.DMA((n,)) for n in c_sems],
        input_output_aliases=aliases, compiler_params=_params(("arbitrary",) * len(grid), vmem),
    )(*operands, *c_ins)
    for p, (a, b, s) in zip(pieces, spans):
        p.done(res[n_out + b:n_out + b + len(p.out_shape)])
    return res[:n_out]


def standalone(pieces, name):
    ki = sum(len(p.ins) for p in pieces)
    ko = sum(len(p.out_shape) for p in pieces)
    spans, a, b, s = [], 0, 0, 0
    aliases = {}
    for p in pieces:
        spans.append((a, b, s))
        for i, o in p.aliases.items():
            aliases[a + i] = b + o
        a, b, s = a + len(p.ins), b + len(p.out_shape), s + len(p.sems)

    def body(*refs):
        cin, cout, csem = refs[:ki], refs[ki:ki + ko], refs[ki + ko:]
        for which in ("start", "finish"):
            for p, (a, b, s) in zip(pieces, spans):
                getattr(p, which)(cin[a:a + len(p.ins)], cout[b:b + len(p.out_shape)], csem[s:s + len(p.sems)])

    res = pl.pallas_call(
        body, name=name, in_specs=[_HBM] * ki, out_specs=[_HBM] * ko, out_shape=[s for p in pieces for s in p.out_shape],
        scratch_shapes=[pltpu.SemaphoreType.DMA((n,)) for p in pieces for n in p.sems], input_output_aliases=aliases,
    )(*[a for p in pieces for a in p.ins])
    for p, (a, b, s) in zip(pieces, spans):
        p.done(res[b:b + len(p.out_shape)])


def _rows_half(ref, which, rows):
    return ref.at[pl.ds(pl.multiple_of(which * rows, 8), rows)]


def gather_ici_piece(bufs, done):
    n = len(bufs)

    def copies(cout):
        x, y, c, chips = _place()
        out = []
        for j, (cx, cy) in enumerate(chips):
            for w in range(n):
                half = bufs[w].shape[1] // 2
                out.append((j, w, _rows_half(cout[w].at[2 * x + y], c, half), _rows_half(cout[w].at[2 * cx + cy], c, half),
                            (cx, cy, c)))
        return out

    def start(cin, cout, sems):
        for j, w, mine, _, to in copies(cout):
            pltpu.make_async_remote_copy(src_ref=mine, dst_ref=mine, send_sem=sems[0].at[n * j + w],
                                         recv_sem=sems[1].at[n * j + w], device_id=to, device_id_type=MESH).start()

    def finish(cin, cout, sems):
        for j, w, mine, landed, to in copies(cout):
            pltpu.make_async_remote_copy(src_ref=mine, dst_ref=landed, send_sem=sems[0].at[n * j + w],
                                         recv_sem=sems[1].at[n * j + w], device_id=to, device_id_type=MESH).wait()

    return Comm(bufs, [SDS(b.shape, b.dtype) for b in bufs], {w: w for w in range(n)}, [3 * n, 3 * n], start, finish, done)


def gather_full_piece(bufs, done):
    n = len(bufs)

    def copies(cout):
        x, y, c, chips = _place()
        return [(n * j + w, cout[w].at[2 * x + y], cout[w].at[2 * cx + cy], (cx, cy, c))
                for j, (cx, cy) in enumerate(chips) for w in range(n)]

    def start(cin, cout, sems):
        for k, mine, _, to in copies(cout):
            pltpu.make_async_remote_copy(src_ref=mine, dst_ref=mine, send_sem=sems[0].at[k], recv_sem=sems[1].at[k],
                                         device_id=to, device_id_type=MESH).start()

    def finish(cin, cout, sems):
        for k, mine, landed, to in copies(cout):
            pltpu.make_async_remote_copy(src_ref=mine, dst_ref=landed, send_sem=sems[0].at[k], recv_sem=sems[1].at[k],
                                         device_id=to, device_id_type=MESH).wait()

    return Comm(bufs, [SDS(b.shape, b.dtype) for b in bufs], {w: w for w in range(n)}, [3 * n, 3 * n], start, finish, done)


def gather_d2d_piece(bufs, done):
    n = len(bufs)

    def copies(cout):
        x, y, c, chips = _place()
        out = []
        for j, (cx, cy) in enumerate(chips):
            for w in range(n):
                half = bufs[w].shape[1] // 2
                slot = cout[w].at[2 * cx + cy]
                out.append((n * j + w, _rows_half(slot, c, half), _rows_half(slot, 1 - c, half), (x, y, 1 - c)))
        return out

    def start(cin, cout, sems):
        for k, mine, _, to in copies(cout):
            pltpu.make_async_remote_copy(src_ref=mine, dst_ref=mine, send_sem=sems[0].at[k], recv_sem=sems[1].at[k],
                                         device_id=to, device_id_type=MESH).start()

    def finish(cin, cout, sems):
        for k, mine, theirs, to in copies(cout):
            pltpu.make_async_remote_copy(src_ref=mine, dst_ref=theirs, send_sem=sems[0].at[k], recv_sem=sems[1].at[k],
                                         device_id=to, device_id_type=MESH).wait()

    return Comm(bufs, [SDS(b.shape, b.dtype) for b in bufs], {w: w for w in range(n)}, [3 * n, 3 * n], start, finish, done)


def pair_piece(parts, done):
    n = len(parts)

    def copies(cin, cout):
        x, y, c, _ = _place()
        out = []
        for w in range(n):
            half = parts[w].shape[1] // 2
            out.append((w, cin[w].at[:, pl.ds(pl.multiple_of((1 - c) * half, 8), half), :], cout[w], (x, y, 1 - c)))
        return out

    def start(cin, cout, sems):
        for w, src, dst, to in copies(cin, cout):
            pltpu.make_async_remote_copy(src_ref=src, dst_ref=dst, send_sem=sems[0].at[w], recv_sem=sems[1].at[w],
                                         device_id=to, device_id_type=MESH).start()

    def finish(cin, cout, sems):
        for w, src, dst, to in copies(cin, cout):
            pltpu.make_async_remote_copy(src_ref=src, dst_ref=dst, send_sem=sems[0].at[w], recv_sem=sems[1].at[w],
                                         device_id=to, device_id_type=MESH).wait()

    return Comm(parts, [SDS((N_SHARD, a.shape[1] // 2, a.shape[2]), f32) for a in parts], {}, [n, n], start, finish, done)


def shard_piece(sums16, done):
    n = len(sums16)

    def copies(cin, cout):
        x, y, c, chips = _place()
        return [(n * j + w, cin[w].at[2 * cx + cy], cout[w].at[j], (cx, cy, c))
                for j, (cx, cy) in enumerate(chips) for w in range(n)]

    def start(cin, cout, sems):
        for k, src, dst, to in copies(cin, cout):
            pltpu.make_async_remote_copy(src_ref=src, dst_ref=dst, send_sem=sems[0].at[k], recv_sem=sems[1].at[k],
                                         device_id=to, device_id_type=MESH).start()

    def finish(cin, cout, sems):
        for k, src, dst, to in copies(cin, cout):
            pltpu.make_async_remote_copy(src_ref=src, dst_ref=dst, send_sem=sems[0].at[k], recv_sem=sems[1].at[k],
                                         device_id=to, device_id_type=MESH).wait()

    return Comm(sums16, [SDS((3,) + a.shape[1:], bf16) for a in sums16], {}, [3 * n, 3 * n], start, finish, done)


def place_shard(a, layer, idx, dtype):
    _, r, cdim = a.shape
    tr = min(r, 512)

    def body(idx_ref, a_ref, o_ref):
        o_ref[0] = a_ref[0].astype(dtype)

    return pl.pallas_call(
        body, name="place_shard",
        grid_spec=pltpu.PrefetchScalarGridSpec(
            num_scalar_prefetch=1, grid=(r // tr,),
            in_specs=[pl.BlockSpec((1, tr, cdim), lambda i, idx_ref: (layer, i, 0))],
            out_specs=pl.BlockSpec((1, tr, cdim), lambda i, idx_ref: (idx_ref[1], i, 0))),
        out_shape=SDS((N_SHARD, r, cdim), dtype),
        compiler_params=_params(("arbitrary",)),
    )(idx, a)


_KEYS = ("w_in", "w_out", "w_up", "w_dn")


class MeshWeights:
    def __init__(self, w_in, w_out, w_up, w_down, conv_dw_w, lru_conv_w, idx):
        self.idx = idx
        src = dict(w_in=w_in, w_out=w_out, w_up=w_up, w_dn=w_down)
        self.buf = {(n, l): place_shard(src[n], l, idx, bf16) for n in _KEYS for l in range(DEPTH)}
        self.conv = {(n, l): place_shard(a, l, idx, f32)
                     for n, a in (("cw", conv_dw_w), ("lw", lru_conv_w)) for l in range(DEPTH)}
        self.cache, self.parts, self.sum32, self.sum16, self.got = {}, {}, {}, {}, {}
        first, small = [("w_in", 0)], list(self.conv)

        def store_conv(outs):
            self.conv.update(zip(small, outs))

        standalone([self._gather(gather_ici_piece, first), gather_full_piece([self.conv[k] for k in small], store_conv)],
                   "gather_first_ici")
        standalone([self._gather(gather_d2d_piece, first)], "gather_first_d2d")

    def _gather(self, piece, keys):
        def done(outs):
            self.buf.update(zip(keys, outs))
        return piece([self.buf[k] for k in keys], done)

    def _pair(self, keys):
        def done(outs):
            for k, recv in zip(keys, outs):
                self.sum32[k], self.sum16[k] = chip_sum(self.parts[k], recv, self.idx)
        return pair_piece([self.parts[k] for k in keys], done)

    def _shard(self, keys):
        def done(outs):
            self.got.update(zip(keys, outs))
        return shard_piece([self.sum16[k] for k in keys], done)

    def conv_weights(self):
        out = []
        for n in ("cw", "lw"):
            a = jnp.stack([self.conv[(n, l)] for l in range(DEPTH)])
            out.append(a.transpose(0, 2, 1, 3).reshape(DEPTH, a.shape[2], N_SHARD * a.shape[3]))
        return out

    def weight(self, name, l):
        if (name, l) not in self.cache:
            b = self.buf[(name, l)]
            if name == "w_in":
                b = b.transpose(1, 0, 2).reshape(D_MODEL, IN_W)
            elif name == "w_out":
                b = b.reshape(D_MODEL, D_MODEL)
            self.cache[(name, l)] = b
        return self.cache[(name, l)]

    def host(self, point, l):
        ici, d2d = gather_ici_piece, gather_d2d_piece
        rest1 = [("w_out", 1), ("w_up", 1), ("w_dn", 1)]
        plan = {
            ("inproj_fwd", 0): lambda: [self._gather(ici, [("w_out", 0)])],
            ("attn_fwd", 0): lambda: [self._gather(ici, [("w_up", 0)]), self._gather(d2d, [("w_out", 0)])],
            ("branch_fwd", 0): lambda: [self._gather(ici, [("w_dn", 0)]), self._gather(d2d, [("w_up", 0)])],
            ("outproj_fwd", 0): lambda: [self._gather(d2d, [("w_dn", 0)]), self._gather(ici, [("w_in", 1)])],
            ("mlp_fwd", 0): lambda: [self._gather(ici, rest1), self._gather(d2d, [("w_in", 1)])],
            ("attn_fwd", 1): lambda: [self._gather(d2d, rest1)],
            ("attn_bwd", 1): lambda: [self._pair([("w_up", 1), ("w_dn", 1)])],
            ("inproj_bwd", 1): lambda: [self._pair([("w_out", 1)])],
            ("mlp_bwd_act", 0): lambda: [self._shard([("w_up", 1), ("w_dn", 1)])],
            ("mlp_bwd_w", 0): lambda: [self._pair([("w_in", 1)])],
            ("attn_bwd", 0): lambda: [self._shard([("w_in", 1), ("w_out", 1)]),
                                      self._pair([("w_up", 0), ("w_dn", 0), ("w_out", 0)])],
            ("branch_bwd_a", 0): lambda: [self._shard([("w_up", 0)])],
            ("branch_bwd_b", 0): lambda: [self._shard([("w_dn", 0), ("w_out", 0)])],
        }
        make = plan.get((point, l))
        return make() if make else None

    def grad(self, name, l, g):
        if name == "w_in":
            g = g.reshape(D_MODEL, N_SHARD, IN_W // N_SHARD).transpose(1, 0, 2)
        elif name == "w_out":
            g = g.reshape(N_SHARD, D_MODEL // N_SHARD, D_MODEL)
        self.parts[(name, l)] = g

    def big_grads(self):
        last = [("w_in", 0)]
        standalone([self._pair(last)], "pair_last")
        standalone([self._shard(last)], "shard_last")
        tots = []
        for n in _KEYS:
            t = None
            for l in reversed(range(DEPTH)):
                t = shard_sum(self.sum32[(n, l)], self.got[(n, l)], self.idx, l, t)
            tots.append(t)
        return tots


def chip_sum(g, recv, idx):
    _, r, cdim = g.shape
    half = r // 2
    tr = min(half, 512)
    nh = half // tr

    def body(idx_ref, g_ref, r_ref, o32_ref, o16_ref):
        tot = g_ref[...] + r_ref[...]
        o32_ref[...] = tot
        o16_ref[...] = tot.astype(bf16)

    blk = pl.BlockSpec((1, tr, cdim), lambda s, i, idx_ref: (s, i, 0))
    return pl.pallas_call(
        body, name="chip_sum",
        grid_spec=pltpu.PrefetchScalarGridSpec(
            num_scalar_prefetch=1, grid=(N_SHARD, nh),
            in_specs=[pl.BlockSpec((1, tr, cdim), lambda s, i, idx_ref: (s, idx_ref[0] * nh + i, 0)), blk],
            out_specs=[blk, blk]),
        out_shape=[SDS((N_SHARD, half, cdim), f32), SDS((N_SHARD, half, cdim), bf16)],
        compiler_params=_params(("arbitrary", "arbitrary")),
    )(idx, g, recv)


def shard_sum(sum32, got16, idx, layer, prev):
    _, half, cdim = sum32.shape
    tr = min(half, 512)

    def body(idx_ref, a_ref, r0_ref, r1_ref, r2_ref, *rest):
        o_ref = rest[-1]
        o_ref[0, 0] = ((a_ref[0] + r0_ref[0].astype(f32)) + r1_ref[0].astype(f32)) + r2_ref[0].astype(f32)

    def rel(j):
        return pl.BlockSpec((1, tr, cdim), lambda i, idx_ref: (j, i, 0))

    in_specs = [pl.BlockSpec((1, tr, cdim), lambda i, idx_ref: (idx_ref[1], i, 0)), rel(0), rel(1), rel(2)]
    operands = [idx, sum32, got16, got16, got16]
    aliases = {}
    if prev is not None:
        in_specs.append(_HBM)
        operands.append(prev)
        aliases = {5: 0}
    return pl.pallas_call(
        body, name="shard_sum",
        grid_spec=pltpu.PrefetchScalarGridSpec(
            num_scalar_prefetch=1, grid=(half // tr,), in_specs=in_specs,
            out_specs=pl.BlockSpec((1, 1, tr, cdim), lambda i, idx_ref: (layer, idx_ref[0], i, 0))),
        out_shape=SDS((DEPTH, 2, half, cdim), f32), input_output_aliases=aliases,
        compiler_params=_params(("arbitrary",)),
    )(*operands)


def halves_exchange(tots):
    nw = len(tots)

    def body(*refs):
        bufs = refs[nw:2 * nw]
        send_sem, recv_sem = refs[2 * nw:]
        x, y, c, _ = _place()

        def copy(w, l, half_idx):
            return pltpu.make_async_remote_copy(
                src_ref=bufs[w].at[l, half_idx], dst_ref=bufs[w].at[l, half_idx], send_sem=send_sem.at[DEPTH * w + l],
                recv_sem=recv_sem.at[DEPTH * w + l], device_id=(x, y, 1 - c), device_id_type=MESH)

        sends = [copy(w, l, c) for w in range(nw) for l in range(DEPTH)]
        for cp in sends:
            cp.start()
        for w in range(nw):
            for l in range(DEPTH):
                copy(w, l, 1 - c).wait_recv()
        for cp in sends:
            cp.wait_send()

    return pl.pallas_call(
        body, name="halves_exchange", in_specs=[_HBM] * nw, out_specs=[_HBM] * nw,
        out_shape=[SDS(a.shape, f32) for a in tots], input_output_aliases={w: w for w in range(nw)},
        scratch_shapes=[pltpu.SemaphoreType.DMA((DEPTH * nw,)), pltpu.SemaphoreType.DMA((DEPTH * nw,))],
    )(*tots)


N_DEV = 8


def small_allreduce(vec):
    r = vec.shape[0]

    def body(v_ref, o_ref, all_ref, send_sems, recv_sems):
        x, y, c, chips = _place()
        me, sib = (x, y, c), (x, y, 1 - c)

        def rows(px, py, pc):
            return all_ref.at[4 * px + 2 * py + pc]

        def copy(k, block, to, src=None):
            return pltpu.make_async_remote_copy(
                src_ref=rows(*block) if src is None else src, dst_ref=rows(*block), send_sem=send_sems.at[k],
                recv_sem=recv_sems.at[k], device_id=to, device_id_type=MESH)

        first = [copy(0, me, sib, src=v_ref)]
        first += [copy(1 + j, me, (*chip, c), src=v_ref) for j, chip in enumerate(chips)]
        for cp in first:
            cp.start()
        rows(*me)[...] = v_ref[...]
        passed = [copy(4 + j, (*chip, c), sib) for j, chip in enumerate(chips)]
        for j, chip in enumerate(chips):
            copy(1 + j, (*chip, c), me).wait_recv()
            passed[j].start()
        copy(0, sib, me).wait_recv()
        for j, chip in enumerate(chips):
            copy(4 + j, (*chip, 1 - c), me).wait_recv()
        for cp in first + passed:
            cp.wait_send()
        acc = all_ref[0]
        for d in range(1, N_DEV):
            acc = acc + all_ref[d]
        o_ref[...] = acc

    return pl.pallas_call(
        body, name="small_allreduce",
        in_specs=[pl.BlockSpec(memory_space=pltpu.VMEM)], out_specs=pl.BlockSpec(memory_space=pltpu.VMEM),
        out_shape=SDS((r, LANES), f32),
        scratch_shapes=[pltpu.VMEM((N_DEV, r, LANES), f32), pltpu.SemaphoreType.DMA((7,)), pltpu.SemaphoreType.DMA((7,))],
    )(vec)


def _adamw_math(w, g, m, v):
    m = ADAM_B1 * m + (1.0 - ADAM_B1) * g
    v = ADAM_B2 * v + (1.0 - ADAM_B2) * (g * g)
    m_hat = m / (1.0 - ADAM_B1 ** ADAM_STEP)
    v_hat = v / (1.0 - ADAM_B2 ** ADAM_STEP)
    delta = -ADAM_LR * (m_hat / (jnp.sqrt(v_hat) + ADAM_EPS) + ADAM_WD * w)
    return delta, m, v


def adamw_big(w, g, m, v):
    _, r, cdim = w.shape
    tr = min(r, 256)

    def body(w_ref, g_ref, m_ref, v_ref, go_ref, d_ref, mo_ref, vo_ref):
        g = g_ref[...]
        d, mm, vv = _adamw_math(w_ref[...], g, m_ref[...], v_ref[...])
        go_ref[...] = g
        d_ref[...] = d
        mo_ref[...] = mm
        vo_ref[...] = vv

    blk = pl.BlockSpec((1, tr, cdim), lambda l, i: (l, i, 0))
    return pl.pallas_call(
        body, name="adamw_big", grid=(DEPTH, r // tr), in_specs=[blk] * 4, out_specs=[blk] * 4,
        out_shape=[SDS(w.shape, f32)] * 4, compiler_params=_params(("parallel", "parallel")),
    )(w, g, m, v)


def adamw_small(ws, gs, ms, vs):
    n = len(ws)

    def body(*refs):
        w_r, g_r, m_r, v_r = refs[:n], refs[n:2 * n], refs[2 * n:3 * n], refs[3 * n:4 * n]
        d_o, m_o, v_o = refs[4 * n:5 * n], refs[5 * n:6 * n], refs[6 * n:7 * n]
        for k in range(n):
            d, mm, vv = _adamw_math(w_r[k][...], g_r[k][...], m_r[k][...], v_r[k][...])
            d_o[k][...] = d
            m_o[k][...] = mm
            v_o[k][...] = vv

    vm = pl.BlockSpec(memory_space=pltpu.VMEM)
    shapes = [SDS(a.shape, f32) for a in ws]
    outs = pl.pallas_call(
        body, name="adamw_small", in_specs=[vm] * (4 * n), out_specs=[vm] * (3 * n), out_shape=shapes * 3,
    )(*ws, *gs, *ms, *vs)
    return outs[:n], outs[n:2 * n], outs[2 * n:]


_BIG = ("w_in", "w_out", "w_up", "w_down")
_WEIGHTS = ("norm1", "w_in", "attn_sinks", "conv_dw_w", "conv_dw_b", "conv_ln_g", "conv_ln_b", "lru_conv_w", "lru_conv_b",
            "lru_wa", "lru_ba", "lru_wx", "lru_bx", "lru_lambda", "mix_norm", "w_out", "norm2", "w_up", "w_down", "final_norm")
_SMALL = tuple(n for n in _WEIGHTS if n not in _BIG)
_SMALL_FULL_SHAPE = dict(
    norm1=(DEPTH, D_MODEL), attn_sinks=(DEPTH, N_HEADS), conv_dw_w=(DEPTH, CONV_K, CONV_W), conv_dw_b=(DEPTH, CONV_W),
    conv_ln_g=(DEPTH, CONV_W), conv_ln_b=(DEPTH, CONV_W), lru_conv_w=(DEPTH, LRU_K, LRU_W), lru_conv_b=(DEPTH, LRU_W),
    lru_wa=(DEPTH, LRU_HEADS, 64, 64), lru_ba=(DEPTH, LRU_HEADS, 64), lru_wx=(DEPTH, LRU_HEADS, 64, 64),
    lru_bx=(DEPTH, LRU_HEADS, 64), lru_lambda=(DEPTH, LRU_W), mix_norm=(DEPTH, D_MODEL), norm2=(DEPTH, D_MODEL),
    final_norm=(D_MODEL,))
_CHANNEL_SHARDED = ("conv_dw_w", "lru_conv_w")


def _pad_lanes(n):
    return -(-n // LANES) * LANES


def _pack(named):
    flat = []
    for a in named:
        a = a.reshape(-1)
        flat.append(jnp.pad(a, (0, _pad_lanes(a.shape[0]) - a.shape[0])))
    v = jnp.concatenate(flat)
    rows = -(-v.shape[0] // (8 * LANES)) * 8
    return jnp.pad(v, (0, rows * LANES - v.shape[0])).reshape(rows, LANES)


def _unpack(vec, shapes):
    flat = vec.reshape(-1)
    out, off = [], 0
    for shp in shapes:
        n = math.prod(shp)
        out.append(flat[off:off + n].reshape(shp))
        off += _pad_lanes(n)
    return out


def _as2d(a):
    return a.reshape(-1, a.shape[-1]) if a.ndim > 1 else a.reshape(1, -1)


def kernel(x, norm1, w_in, attn_sinks, conv_dw_w, conv_dw_b, conv_ln_g, conv_ln_b, lru_conv_w, lru_conv_b, lru_wa, lru_ba, lru_wx, lru_bx, lru_lambda, mix_norm, w_out, norm2, w_up, w_down, final_norm, loss_target, m_norm1, m_w_in, m_attn_sinks, m_conv_dw_w, m_conv_dw_b, m_conv_ln_g, m_conv_ln_b, m_lru_conv_w, m_lru_conv_b, m_lru_wa, m_lru_ba, m_lru_wx, m_lru_bx, m_lru_lambda, m_mix_norm, m_w_out, m_norm2, m_w_up, m_w_down, m_final_norm, v_norm1, v_w_in, v_attn_sinks, v_conv_dw_w, v_conv_dw_b, v_conv_ln_g, v_conv_ln_b, v_lru_conv_w, v_lru_conv_b, v_lru_wa, v_lru_ba, v_lru_wx, v_lru_bx, v_lru_lambda, v_mix_norm, v_w_out, v_norm2, v_w_up, v_w_down, v_final_norm):
    wts = dict(norm1=norm1, w_in=w_in, attn_sinks=attn_sinks, conv_dw_w=conv_dw_w, conv_dw_b=conv_dw_b, conv_ln_g=conv_ln_g,
               conv_ln_b=conv_ln_b, lru_conv_w=lru_conv_w, lru_conv_b=lru_conv_b, lru_wa=lru_wa, lru_ba=lru_ba, lru_wx=lru_wx,
               lru_bx=lru_bx, lru_lambda=lru_lambda, mix_norm=mix_norm, w_out=w_out, norm2=norm2, w_up=w_up, w_down=w_down,
               final_norm=final_norm)
    mom = dict(norm1=m_norm1, w_in=m_w_in, attn_sinks=m_attn_sinks, conv_dw_w=m_conv_dw_w, conv_dw_b=m_conv_dw_b,
               conv_ln_g=m_conv_ln_g, conv_ln_b=m_conv_ln_b, lru_conv_w=m_lru_conv_w, lru_conv_b=m_lru_conv_b, lru_wa=m_lru_wa,
               lru_ba=m_lru_ba, lru_wx=m_lru_wx, lru_bx=m_lru_bx, lru_lambda=m_lru_lambda, mix_norm=m_mix_norm, w_out=m_w_out,
               norm2=m_norm2, w_up=m_w_up, w_down=m_w_down, final_norm=m_final_norm)
    var = dict(norm1=v_norm1, w_in=v_w_in, attn_sinks=v_attn_sinks, conv_dw_w=v_conv_dw_w, conv_dw_b=v_conv_dw_b,
               conv_ln_g=v_conv_ln_g, conv_ln_b=v_conv_ln_b, lru_conv_w=v_lru_conv_w, lru_conv_b=v_lru_conv_b, lru_wa=v_lru_wa,
               lru_ba=v_lru_ba, lru_wx=v_lru_wx, lru_bx=v_lru_bx, lru_lambda=v_lru_lambda, mix_norm=v_mix_norm, w_out=v_w_out,
               norm2=v_norm2, w_up=v_w_up, w_down=v_w_down, final_norm=v_final_norm)

    c_idx = lax.axis_index("c").astype(jnp.int32)
    s_idx = (2 * lax.axis_index("x") + lax.axis_index("y")).astype(jnp.int32)
    idx = jnp.stack([c_idx, s_idx])

    cs = MeshWeights(w_in, w_out, w_up, w_down, conv_dw_w, lru_conv_w, idx)
    sp = {n: wts[n] for n in _SMALL}
    sp["conv_dw_w"], sp["lru_conv_w"] = cs.conv_weights()
    loss_blk, grad_x, tots, small_g, d_gf = train_local(x[0], loss_target[0], sp, cs)
    grads_big = {n: a.reshape(wts[n].shape) for n, a in zip(_BIG, halves_exchange(tots))}

    stacked = [jnp.stack([small_g[l][n] for l in range(DEPTH)]) for n in _SMALL if n != "final_norm"]
    packed = _pack(stacked + [d_gf, loss_blk[0, 0:1]])
    summed = small_allreduce(packed)
    names = [n for n in _SMALL if n != "final_norm"] + ["final_norm"]
    unpacked = _unpack(summed, [_SMALL_FULL_SHAPE[n] for n in names] + [(1,)])
    loss = unpacked[-1][0]
    grads = dict(zip(names, unpacked[:-1]))
    for n in _CHANNEL_SHARDED:
        width = wts[n].shape[-1]
        grads[n] = lax.dynamic_slice_in_dim(grads[n], s_idx * width, width, axis=2)
    grads.update(grads_big)

    delta, new_m, new_v = {}, {}, {}
    for n in _BIG:
        grads[n], delta[n], new_m[n], new_v[n] = adamw_big(wts[n], grads[n], mom[n], var[n])
    sm = list(_SMALL)
    d_s, m_s, v_s = adamw_small([_as2d(wts[n]) for n in sm], [_as2d(grads[n]) for n in sm],
                                [_as2d(mom[n]) for n in sm], [_as2d(var[n]) for n in sm])
    for k, n in enumerate(sm):
        delta[n], new_m[n], new_v[n] = (a.reshape(wts[n].shape) for a in (d_s[k], m_s[k], v_s[k]))

    return (loss, grad_x[None], *[grads[n] for n in _WEIGHTS], *[delta[n] for n in _WEIGHTS],
            *[new_m[n] for n in _WEIGHTS], *[new_v[n] for n in _WEIGHTS])
```

```python
import functools
import math

import jax
import jax.numpy as jnp
from jax import lax
from jax.experimental import pallas as pl
from jax.experimental.pallas import tpu as pltpu

f32 = jnp.float32
bf16 = jnp.bfloat16
SDS = jax.ShapeDtypeStruct

D_MODEL = 1024
DEPTH = 2
ATTN_W = 512
KV_W = 128
HEAD_DIM = 64
N_HEADS = 8
BLOCK = 128
CONV_W = 256
CONV_K = 31
LRU_W = 256
LRU_K = 4
LRU_HEADS = 4
LRU_C = 8.0
IN_W = 1792
D_FF = 4096
N_SHARD = 4
FF_CHUNK = D_FF // N_SHARD
RMS_EPS = 1e-6
LN_EPS = 1e-5
MASK_VALUE = -1e30
HALO = 32
LANES = 128
VMEM_LIMIT = 56 * 1024 * 1024

ADAM_LR = 0.001
ADAM_B1 = 0.9
ADAM_B2 = 0.999
ADAM_EPS = 1e-08
ADAM_WD = 0.01
ADAM_STEP = 10

MESH = pl.DeviceIdType.MESH


def _dot(a, b):
    return jnp.dot(a, b, preferred_element_type=f32)


def _dot_nt(a, b):
    return lax.dot_general(a, b, (((1,), (1,)), ((), ())), preferred_element_type=f32)


def _dot_tn(a, b):
    return lax.dot_general(a, b, (((0,), (0,)), ((), ())), preferred_element_type=f32)


def _rms_fwd(x, g):
    r = lax.rsqrt(jnp.mean(x * x, axis=-1, keepdims=True) + RMS_EPS)
    return x * r * g, r


def _rms_bwd(dy, x, r, g):
    t = dy * g
    dx = r * t - x * (r * r * r) * jnp.mean(t * x, axis=-1, keepdims=True)
    dg = jnp.sum(dy * x * r, axis=0, keepdims=True)
    return dx, dg


def _sigmoid(x):
    return jax.nn.sigmoid(x)


_GELU_K = math.sqrt(2.0 / math.pi)


def _gelu(x):
    t = jnp.tanh(_GELU_K * (x + 0.044715 * x * x * x))
    return 0.5 * x * (1.0 + t), t


def _gelu_grad(x, t):
    return 0.5 * (1.0 + t) + 0.5 * x * (1.0 - t * t) * _GELU_K * (1.0 + 3.0 * 0.044715 * x * x)


def _log1p(x):
    return jnp.where(x < 1e-4, x - 0.5 * x * x, jnp.log(1.0 + x))


def _softplus(x):
    return jnp.maximum(x, 0.0) + _log1p(jnp.exp(-jnp.abs(x)))


def _neg_expm1(x):
    series = -x * (1.0 + 0.5 * x * (1.0 + x * (1.0 / 3.0) * (1.0 + 0.25 * x)))
    return jnp.where(x > -0.01, series, 1.0 - jnp.exp(x))


def _sublane_rolls(x, count, forward):
    n = x.shape[0]
    return [x if b == 0 else pltpu.roll(x, b if forward else n - b, 0) for b in range(count)]


def _conv_taps(xpad, w, k_width):
    t_rows = xpad.shape[0] - HALO
    rolled = _sublane_rolls(xpad, min(k_width, 8), forward=True)
    acc = None
    for k in range(k_width):
        hi, lo = divmod((k_width - 1) - k, 8)
        term = rolled[lo][HALO - 8 * hi:HALO - 8 * hi + t_rows] * w[k:k + 1, :]
        acc = term if acc is None else acc + term
    return acc


def _conv_taps_bwd(dpad, upad, w, k_width, t_rows):
    n_lo = min(k_width, 8)
    d_rolled = _sublane_rolls(dpad, n_lo, forward=False)
    u_rolled = _sublane_rolls(upad, n_lo, forward=True)
    d_in = None
    dw_rows = []
    d_out = dpad[:t_rows]
    for k in range(k_width):
        hi, lo = divmod((k_width - 1) - k, 8)
        term = d_rolled[lo][8 * hi:8 * hi + t_rows] * w[k:k + 1, :]
        d_in = term if d_in is None else d_in + term
        us = u_rolled[lo][HALO - 8 * hi:HALO - 8 * hi + t_rows]
        dw_rows.append(jnp.sum(d_out * us, axis=0, keepdims=True))
    return d_in, dw_rows


SUBLANES = 8


def _scan_fwd(a, b, h0):
    t_rows = a.shape[0]
    sub = jnp.bitwise_and(lax.broadcasted_iota(jnp.int32, a.shape, 0), SUBLANES - 1)
    for d in (1, 2, 4):
        a_sh = jnp.where(sub < d, 1.0, pltpu.roll(a, d, 0))
        b_sh = jnp.where(sub < d, 0.0, pltpu.roll(b, d, 0))
        b = a * b_sh + b
        a = a * a_sh
    out, carry = [], h0
    for g in range(t_rows // SUBLANES):
        rows = slice(g * SUBLANES, (g + 1) * SUBLANES)
        hg = a[rows] * carry + b[rows]
        out.append(hg)
        carry = hg[SUBLANES - 1:SUBLANES]
    return jnp.concatenate(out, axis=0)


def _scan_bwd(a, b, l_end):
    t_rows = a.shape[0]
    sub = jnp.bitwise_and(lax.broadcasted_iota(jnp.int32, a.shape, 0), SUBLANES - 1)
    for d in (1, 2, 4):
        a_sh = jnp.where(sub >= SUBLANES - d, 1.0, pltpu.roll(a, t_rows - d, 0))
        b_sh = jnp.where(sub >= SUBLANES - d, 0.0, pltpu.roll(b, t_rows - d, 0))
        b = b + a * b_sh
        a = a * a_sh
    out, carry = [], l_end
    for g in reversed(range(t_rows // SUBLANES)):
        rows = slice(g * SUBLANES, (g + 1) * SUBLANES)
        lg = b[rows] + a[rows] * carry
        out.append(lg)
        carry = lg[0:1]
    return jnp.concatenate(out[::-1], axis=0)


def _full(shape, single=False):
    nd = len(shape)
    if single:
        return pl.BlockSpec(shape, lambda *_: (0,) * nd, pipeline_mode=pl.Buffered(1))
    return pl.BlockSpec(shape, lambda *_: (0,) * nd)


def _params(sem, vmem=None):
    return pltpu.CompilerParams(dimension_semantics=sem, vmem_limit_bytes=vmem)


def _tile(s):
    return min(512, s)


def inproj_fwd(h, g1, w_in, pieces=None):
    s = h.shape[0]
    tm = _tile(s)

    def body(h_ref, g_ref, w_ref, z_ref):
        hn, _ = _rms_fwd(h_ref[...], g_ref[...])
        z_ref[...] = _dot(hn.astype(bf16), w_ref[...]).astype(bf16)

    return _pcall(
        body, pieces, name="inproj_fwd", grid=(s // tm,),
        in_specs=[pl.BlockSpec((tm, D_MODEL), lambda i: (i, 0)), _full((1, D_MODEL)), _full((D_MODEL, IN_W))],
        out_specs=[pl.BlockSpec((tm, IN_W), lambda i: (i, 0))],
        out_shape=[SDS((s, IN_W), bf16)],
        operands=[h, g1, w_in], sem=("parallel",), vmem=VMEM_LIMIT)[0]


ATT_ROWS = N_HEADS * BLOCK
ATT_BLOCKS_PER_STEP = 4


def _attn_bias():
    qi = jnp.arange(BLOCK)[None, :]
    key = jnp.arange(2 * BLOCK)[:, None]
    band = (key > qi) & (key <= qi + BLOCK)
    first = band & (key >= BLOCK)
    mask = jnp.where(jnp.stack([first, band]), 0.0, MASK_VALUE).astype(f32)
    return jnp.tile(mask, (1, 1, N_HEADS // 2))


def _attn_band(kvc, kvp):
    kb = jnp.concatenate([kvp[:, :KV_W], kvc[:, :KV_W]], axis=0)
    vb = jnp.concatenate([kvp[:, KV_W:], kvc[:, KV_W:]], axis=0)
    lane = lax.broadcasted_iota(jnp.int32, kb.shape, 1)
    kb_sw = pltpu.roll(kb, HEAD_DIM, 1)
    vb_sw = pltpu.roll(vb, HEAD_DIM, 1)
    kx = [jnp.where(lane < HEAD_DIM, kb, kb_sw), jnp.where(lane >= HEAD_DIM, kb, kb_sw)]
    vx = [jnp.where(lane < HEAD_DIM, vb, vb_sw), jnp.where(lane >= HEAD_DIM, vb, vb_sw)]
    return kx, vx


def _stack_heads(x, mlo):
    zero = jnp.zeros((BLOCK, LANES), x.dtype)
    out = []
    for hk in range(2):
        parts = []
        for j in (2 * hk, 2 * hk + 1):
            xj = x[:, j * LANES:(j + 1) * LANES]
            parts += [jnp.where(mlo, xj, zero), jnp.where(mlo, zero, xj)]
        out.append(jnp.concatenate(parts, axis=0))
    return out


def _unstack_heads(y, mlo):
    cols = []
    for hk in range(2):
        for t in range(2):
            base = 2 * t * BLOCK
            cols.append(jnp.where(mlo, y[hk][base:base + BLOCK], y[hk][base + BLOCK:base + 2 * BLOCK]))
    return jnp.concatenate(cols, axis=1)


def _attn_probs(q4, kx, bias_t, sink_row):
    out = []
    half = ATT_ROWS // 2
    for hk in range(2):
        s = _dot_nt(kx[hk], q4[hk]) + bias_t
        sink = sink_row[:, hk * half:(hk + 1) * half]
        m = jnp.maximum(jnp.max(s, axis=0, keepdims=True), sink)
        p = jnp.exp(s - m)
        e_sink = jnp.exp(sink - m)
        inv = 1.0 / (jnp.sum(p, axis=0, keepdims=True) + e_sink)
        out.append((p * inv, e_sink * inv))
    return out


def attn_fwd(z, sink_col, bias, g_a, pieces=None):
    s = z.shape[0]
    per = min(ATT_BLOCKS_PER_STEP, s // BLOCK)
    tq = per * BLOCK

    def body(q_ref, kv_ref, kvp_ref, sk_ref, b_ref, g_ref, o_ref, y_ref):
        n = pl.program_id(0)
        mlo = lax.broadcasted_iota(jnp.int32, (BLOCK, LANES), 1) < HEAD_DIM
        for b in range(per):
            rows = slice(b * BLOCK, (b + 1) * BLOCK)
            kvp = kvp_ref[...] if b == 0 else kv_ref[(b - 1) * BLOCK:b * BLOCK, :]
            bias_b = b_ref[jnp.minimum(n, 1)] if b == 0 else b_ref[1]
            kx, vx = _attn_band(kv_ref[rows, :], kvp)
            q4 = _stack_heads(q_ref[rows, :] * (HEAD_DIM ** -0.5), mlo)
            probs = _attn_probs(q4, kx, bias_b, sk_ref[...])
            o = _unstack_heads([_dot_tn(probs[hk][0].astype(bf16), vx[hk]) for hk in range(2)], mlo)
            o_ref[rows, :] = o.astype(bf16)
            y, _ = _rms_fwd(o, g_ref[...])
            y_ref[rows, :] = y.astype(bf16)

    return _pcall(
        body, pieces, name="attn_fwd", grid=(s // tq,),
        in_specs=[pl.BlockSpec((tq, ATTN_W), lambda n: (n, 0)),
                  pl.BlockSpec((tq, 2 * KV_W), lambda n: (n, 2)),
                  pl.BlockSpec((BLOCK, 2 * KV_W), lambda n: (jnp.maximum(n * per - 1, 0), 2)),
                  _full((1, ATT_ROWS)), _full((2, 2 * BLOCK, ATT_ROWS // 2)), _full((1, ATTN_W))],
        out_specs=[pl.BlockSpec((tq, ATTN_W), lambda n: (n, 0)), pl.BlockSpec((tq, ATTN_W), lambda n: (n, 0))],
        out_shape=[SDS((s, ATTN_W), bf16), SDS((s, ATTN_W), bf16)],
        operands=[z, z, z, sink_col, bias, g_a], sem=("parallel",))


def _lru_gates(xc, wa, ba, wx, bx, lam):
    xcb = xc.astype(bf16)
    r = _sigmoid(_dot(xcb, wa) + ba)
    ig = _sigmoid(_dot(xcb, wx) + bx)
    sp = _softplus(-lam)
    la = (-LRU_C * r) * sp
    a = jnp.exp(la)
    mult = jnp.sqrt(_neg_expm1(2.0 * la))
    return r, ig, sp, la, a, mult


def branch_fwd(z, p, pieces=None):
    s = z.shape[0]
    tm = _tile(s)
    hb = tm // HALO

    def body(cv_ref, cg_ref, rx_ref, rg_ref, cvh_ref, cgh_ref, rxh_ref,
             cw_ref, cb_ref, lng_ref, lnb_ref, lw_ref, lb_ref, wa_ref, ba_ref, wx_ref, bx_ref, lam_ref, gc_ref, gl_ref,
             conv_ref, hst_ref, nc_ref, nl_ref, carry_ref):
        i = pl.program_id(0)
        first = i == 0

        @pl.when(first)
        def _():
            carry_ref[...] = jnp.zeros_like(carry_ref)

        cval = cv_ref[...].astype(f32)
        u = cval * _sigmoid(cg_ref[...].astype(f32))
        hu = jnp.where(first, 0.0, cvh_ref[...].astype(f32) * _sigmoid(cgh_ref[...].astype(f32)))
        conv = _conv_taps(jnp.concatenate([hu, u], axis=0), cw_ref[...], CONV_K) + cb_ref[...]
        conv_ref[...] = conv
        mu = jnp.mean(conv, axis=-1, keepdims=True)
        xm = conv - mu
        rstd = lax.rsqrt(jnp.mean(xm * xm, axis=-1, keepdims=True) + LN_EPS)
        ln = xm * rstd * lng_ref[...] + lnb_ref[...]
        yc = ln * _sigmoid(ln)
        nc, _ = _rms_fwd(yc, gc_ref[...])
        nc_ref[...] = nc.astype(bf16)

        rx = rx_ref[...].astype(f32)
        hrx = jnp.where(first, 0.0, rxh_ref[...].astype(f32))
        xc = _conv_taps(jnp.concatenate([hrx, rx], axis=0), lw_ref[...], LRU_K) + lb_ref[...]
        r, ig, sp, la, a, mult = _lru_gates(xc, wa_ref[...], ba_ref[...], wx_ref[...], bx_ref[...], lam_ref[...])
        gx = mult * (ig * xc)
        hs = _scan_fwd(a, gx, carry_ref[0:1, :])
        carry_ref[...] = jnp.broadcast_to(hs[tm - 1:tm, :], carry_ref.shape)
        hst_ref[...] = hs
        gl, _ = _gelu(rg_ref[...].astype(f32))
        nl, _ = _rms_fwd(hs * gl, gl_ref[...])
        nl_ref[...] = nl.astype(bf16)

    def col(c):
        return pl.BlockSpec((tm, CONV_W), lambda i: (i, c))

    def halo(c):
        return pl.BlockSpec((HALO, CONV_W), lambda i: (jnp.maximum(i * hb - 1, 0), c))

    small = [p["cw"], p["cb"], p["lng"], p["lnb"], p["lw"], p["lb"], p["wa"], p["ba"], p["wx"], p["bx"], p["lam"],
             p["gc"], p["gl"]]
    row = pl.BlockSpec((tm, CONV_W), lambda i: (i, 0))
    return _pcall(
        body, pieces, name="branch_fwd", grid=(s // tm,),
        in_specs=[col(3), col(4), col(5), col(6), halo(3), halo(4), halo(5)] + [_full(a.shape) for a in small],
        out_specs=[row, row, row, row],
        out_shape=[SDS((s, CONV_W), f32), SDS((s, LRU_W), f32), SDS((s, CONV_W), bf16), SDS((s, LRU_W), bf16)],
        scratch_shapes=[pltpu.VMEM((8, LRU_W), f32)],
        operands=[z, z, z, z, z, z, z, *small], sem=("arbitrary",))


def outproj_fwd(ya, yc, yl, h, w_out, g2, pieces=None):
    s = h.shape[0]
    tm = _tile(s)

    def body(ya_ref, yc_ref, yl_ref, h_ref, w_ref, g_ref, h1_ref, hn_ref):
        y = jnp.concatenate([ya_ref[...], yc_ref[...], yl_ref[...]], axis=1)
        h1 = h_ref[...] + _dot(y, w_ref[...])
        h1_ref[...] = h1
        hn, _ = _rms_fwd(h1, g_ref[...])
        hn_ref[...] = hn.astype(bf16)

    return _pcall(
        body, pieces, name="outproj_fwd", grid=(s // tm,),
        in_specs=[pl.BlockSpec((tm, ATTN_W), lambda i: (i, 0)), pl.BlockSpec((tm, CONV_W), lambda i: (i, 0)),
                  pl.BlockSpec((tm, LRU_W), lambda i: (i, 0)), pl.BlockSpec((tm, D_MODEL), lambda i: (i, 0)),
                  _full((D_MODEL, D_MODEL)), _full((1, D_MODEL))],
        out_specs=[pl.BlockSpec((tm, D_MODEL), lambda i: (i, 0)), pl.BlockSpec((tm, D_MODEL), lambda i: (i, 0))],
        out_shape=[SDS((s, D_MODEL), f32), SDS((s, D_MODEL), bf16)],
        operands=[ya, yc, yl, h, w_out, g2], sem=("parallel",), vmem=VMEM_LIMIT)


def mlp_fwd(hn2, h1, w_up, w_dn, pieces=None):
    s = h1.shape[0]
    tm = _tile(s)

    def body(x_ref, h_ref, wu_ref, wd_ref, up_ref, h2_ref):
        x = x_ref[...]
        acc = h_ref[...]
        for c in range(N_SHARD):
            u = _dot(x, wu_ref[c])
            up_ref[:, c * FF_CHUNK:(c + 1) * FF_CHUNK] = u.astype(bf16)
            act = jnp.square(jnp.maximum(u, 0.0)).astype(bf16)
            acc = acc + _dot(act, wd_ref[c])
        h2_ref[...] = acc

    return _pcall(
        body, pieces, name="mlp_fwd", grid=(s // tm,),
        in_specs=[pl.BlockSpec((tm, D_MODEL), lambda i: (i, 0)), pl.BlockSpec((tm, D_MODEL), lambda i: (i, 0)),
                  _full((N_SHARD, D_MODEL, FF_CHUNK), single=True), _full((N_SHARD, FF_CHUNK, D_MODEL), single=True)],
        out_specs=[pl.BlockSpec((tm, D_FF), lambda i: (i, 0)), pl.BlockSpec((tm, D_MODEL), lambda i: (i, 0))],
        out_shape=[SDS((s, D_FF), bf16), SDS((s, D_MODEL), f32)],
        operands=[hn2, h1, w_up, w_dn], sem=("parallel",), vmem=VMEM_LIMIT)


def final_loss(h, tgt, gf):
    s = h.shape[0]
    tm = _tile(s)

    def body(h_ref, t_ref, g_ref, dh_ref, loss_ref, dg_ref):
        i = pl.program_id(0)

        @pl.when(i == 0)
        def _():
            loss_ref[...] = jnp.zeros_like(loss_ref)
            dg_ref[...] = jnp.zeros_like(dg_ref)

        x = h_ref[...]
        g = g_ref[...]
        y, r = _rms_fwd(x, g)
        err = y - t_ref[...]
        part = 0.5 * jnp.sum(jnp.mean(err * err, axis=-1, keepdims=True), axis=0, keepdims=True)
        loss_ref[...] += jnp.broadcast_to(part, loss_ref.shape)
        dx, dg = _rms_bwd(err * (1.0 / D_MODEL), x, r, g)
        dh_ref[...] = dx
        dg_ref[...] += dg

    return pl.pallas_call(
        body, name="final_loss", grid=(s // tm,),
        in_specs=[pl.BlockSpec((tm, D_MODEL), lambda i: (i, 0)), pl.BlockSpec((tm, D_MODEL), lambda i: (i, 0)),
                  _full((1, D_MODEL))],
        out_specs=[pl.BlockSpec((tm, D_MODEL), lambda i: (i, 0)), _full((8, LANES)), _full((1, D_MODEL))],
        out_shape=[SDS((s, D_MODEL), f32), SDS((8, LANES), f32), SDS((1, D_MODEL), f32)],
        compiler_params=_params(("arbitrary",)),
    )(h, tgt, gf)


def mlp_bwd_act(dh, up, h1, g2, w_up, w_dn, pieces=None):
    s = dh.shape[0]
    tm = _tile(s)

    def body(dh_ref, up_ref, h1_ref, g_ref, wu_ref, wd_ref, dup_ref, dh1_ref, dg_ref):
        i = pl.program_id(0)

        @pl.when(i == 0)
        def _():
            dg_ref[...] = jnp.zeros_like(dg_ref)

        dh = dh_ref[...]
        dhb = dh.astype(bf16)
        d_hn = jnp.zeros((tm, D_MODEL), f32)
        for c in range(N_SHARD):
            d_act = _dot_nt(dhb, wd_ref[c])
            u = up_ref[:, c * FF_CHUNK:(c + 1) * FF_CHUNK].astype(f32)
            d_u = (d_act * (2.0 * jnp.maximum(u, 0.0))).astype(bf16)
            dup_ref[:, c * FF_CHUNK:(c + 1) * FF_CHUNK] = d_u
            d_hn = d_hn + _dot_nt(d_u, wu_ref[c])
        x = h1_ref[...]
        g = g_ref[...]
        _, r = _rms_fwd(x, g)
        dx, dg = _rms_bwd(d_hn, x, r, g)
        dh1_ref[...] = dh + dx
        dg_ref[...] += dg

    return _pcall(
        body, pieces, name="mlp_bwd_act", grid=(s // tm,),
        in_specs=[pl.BlockSpec((tm, D_MODEL), lambda i: (i, 0)), pl.BlockSpec((tm, D_FF), lambda i: (i, 0)),
                  pl.BlockSpec((tm, D_MODEL), lambda i: (i, 0)), _full((1, D_MODEL)),
                  _full((N_SHARD, D_MODEL, FF_CHUNK), single=True), _full((N_SHARD, FF_CHUNK, D_MODEL), single=True)],
        out_specs=[pl.BlockSpec((tm, D_FF), lambda i: (i, 0)), pl.BlockSpec((tm, D_MODEL), lambda i: (i, 0)),
                   _full((1, D_MODEL))],
        out_shape=[SDS((s, D_FF), bf16), SDS((s, D_MODEL), f32), SDS((1, D_MODEL), f32)],
        operands=[dh, up, h1, g2, w_up, w_dn], sem=("arbitrary",), vmem=VMEM_LIMIT)


def mlp_bwd_w(hn2, d_up, up, dh, pieces=None):
    s = dh.shape[0]
    tk = min(1024, s)

    def body(x_ref, du_ref, up_ref, dh_ref, dwu_ref, dwd_ref):
        k = pl.program_id(1)

        @pl.when(k == 0)
        def _():
            dwu_ref[...] = jnp.zeros_like(dwu_ref)
            dwd_ref[...] = jnp.zeros_like(dwd_ref)

        dwu_ref[0] += _dot_tn(x_ref[...], du_ref[...])
        act = jnp.square(jnp.maximum(up_ref[...].astype(f32), 0.0)).astype(bf16)
        dwd_ref[0] += _dot_tn(act, dh_ref[...].astype(bf16))

    return _pcall(
        body, pieces, name="mlp_bwd_w", grid=(N_SHARD, s // tk),
        in_specs=[pl.BlockSpec((tk, D_MODEL), lambda c, k: (k, 0)), pl.BlockSpec((tk, FF_CHUNK), lambda c, k: (k, c)),
                  pl.BlockSpec((tk, FF_CHUNK), lambda c, k: (k, c)), pl.BlockSpec((tk, D_MODEL), lambda c, k: (k, 0))],
        out_specs=[pl.BlockSpec((1, D_MODEL, FF_CHUNK), lambda c, k: (c, 0, 0)),
                   pl.BlockSpec((1, FF_CHUNK, D_MODEL), lambda c, k: (c, 0, 0))],
        out_shape=[SDS((N_SHARD, D_MODEL, FF_CHUNK), f32), SDS((N_SHARD, FF_CHUNK, D_MODEL), f32)],
        operands=[hn2, d_up, up, dh], sem=("parallel", "arbitrary"), vmem=VMEM_LIMIT)


def outproj_bwd(dh1, ya, yc, yl, w_out):
    s = dh1.shape[0]
    tm = _tile(s)

    def body(dh_ref, ya_ref, yc_ref, yl_ref, w_ref, dy_ref, dw_ref):
        i = pl.program_id(0)

        @pl.when(i == 0)
        def _():
            dw_ref[...] = jnp.zeros_like(dw_ref)

        dhb = dh_ref[...].astype(bf16)
        dy_ref[...] = _dot_nt(dhb, w_ref[...])
        y = jnp.concatenate([ya_ref[...], yc_ref[...], yl_ref[...]], axis=1)
        dw_ref[...] += _dot_tn(y, dhb)

    return pl.pallas_call(
        body, name="outproj_bwd", grid=(s // tm,),
        in_specs=[pl.BlockSpec((tm, D_MODEL), lambda i: (i, 0)), pl.BlockSpec((tm, ATTN_W), lambda i: (i, 0)),
                  pl.BlockSpec((tm, CONV_W), lambda i: (i, 0)), pl.BlockSpec((tm, LRU_W), lambda i: (i, 0)),
                  _full((D_MODEL, D_MODEL))],
        out_specs=[pl.BlockSpec((tm, D_MODEL), lambda i: (i, 0)), _full((D_MODEL, D_MODEL))],
        out_shape=[SDS((s, D_MODEL), f32), SDS((D_MODEL, D_MODEL), f32)],
        compiler_params=_params(("arbitrary",), VMEM_LIMIT),
    )(dh1, ya, yc, yl, w_out)


def attn_bwd(z, o, dy, sink_col, bias, g_a, pieces=None):
    s = z.shape[0]
    per = min(ATT_BLOCKS_PER_STEP, s // BLOCK)
    tq = per * BLOCK
    nt = s // tq

    def body(q_ref, kv_ref, kvp_ref, o_ref, dy_ref, sk_ref, b_ref, g_ref, dq_ref, dkv_ref, dsk_ref, dg_ref,
             carry_ref, dsk_acc):
        i = pl.program_id(0)
        t = nt - 1 - i

        @pl.when(i == 0)
        def _():
            carry_ref[...] = jnp.zeros_like(carry_ref)
            dsk_acc[...] = jnp.zeros_like(dsk_acc)
            dg_ref[...] = jnp.zeros_like(dg_ref)

        mlo = lax.broadcasted_iota(jnp.int32, (BLOCK, LANES), 1) < HEAD_DIM
        lane = lax.broadcasted_iota(jnp.int32, (2 * BLOCK, LANES), 1)
        scale = HEAD_DIM ** -0.5
        half = ATT_ROWS // 2
        g = g_ref[...]
        bands = []
        for b in range(per):
            rows = slice(b * BLOCK, (b + 1) * BLOCK)
            kvp = kvp_ref[...] if b == 0 else kv_ref[(b - 1) * BLOCK:b * BLOCK, :]
            bias_b = b_ref[jnp.minimum(t, 1)] if b == 0 else b_ref[1]
            kx, vx = _attn_band(kv_ref[rows, :], kvp)
            q4 = _stack_heads(q_ref[rows, :] * scale, mlo)
            o_f = o_ref[rows, :].astype(f32)
            _, r = _rms_fwd(o_f, g)
            d_o, dg = _rms_bwd(dy_ref[rows, :], o_f, r, g)
            dg_ref[...] += dg
            do4 = _stack_heads(d_o.astype(bf16), mlo)
            probs = _attn_probs(q4, kx, bias_b, sk_ref[...])
            dq4, tk, tv = [], [], []
            for hk in range(2):
                pr, p_sink = probs[hk]
                d_p = _dot_nt(vx[hk], do4[hk])
                d_row = jnp.sum(pr * d_p, axis=0, keepdims=True)
                d_s = (pr * (d_p - d_row)).astype(bf16)
                dsk_acc[:, hk * half:(hk + 1) * half] -= p_sink * d_row
                dq4.append(_dot_tn(d_s, kx[hk]))
                tk.append(_dot(d_s, q4[hk]))
                tv.append(_dot(pr.astype(bf16), do4[hk]))
            dq_ref[rows, :] = (_unstack_heads(dq4, mlo) * scale).astype(bf16)
            fk = [x + pltpu.roll(x, HEAD_DIM, 1) for x in tk]
            fv = [x + pltpu.roll(x, HEAD_DIM, 1) for x in tv]
            bands.append(jnp.concatenate([jnp.where(lane < HEAD_DIM, fk[0], fk[1]),
                                          jnp.where(lane < HEAD_DIM, fv[0], fv[1])], axis=1))
        for b in range(per):
            after = bands[b + 1][:BLOCK] if b + 1 < per else carry_ref[...]
            dkv_ref[b * BLOCK:(b + 1) * BLOCK, :] = (bands[b][BLOCK:] + after).astype(bf16)
        carry_ref[...] = bands[0][:BLOCK]

        @pl.when(i == nt - 1)
        def _():
            for hh in range(N_HEADS):
                tot = jnp.sum(dsk_acc[:, hh * BLOCK:(hh + 1) * BLOCK], axis=1, keepdims=True)
                dsk_ref[hh:hh + 1, :] = jnp.broadcast_to(tot, (1, LANES))

    def rev(width, col):
        return pl.BlockSpec((tq, width), lambda i: (nt - 1 - i, col))

    return _pcall(
        body, pieces, name="attn_bwd", grid=(nt,),
        in_specs=[rev(ATTN_W, 0), rev(2 * KV_W, 2),
                  pl.BlockSpec((BLOCK, 2 * KV_W), lambda i: (jnp.maximum((nt - 1 - i) * per - 1, 0), 2)),
                  rev(ATTN_W, 0), rev(ATTN_W, 0),
                  _full((1, ATT_ROWS)), _full((2, 2 * BLOCK, ATT_ROWS // 2)), _full((1, ATTN_W))],
        out_specs=[rev(ATTN_W, 0), rev(2 * KV_W, 0), _full((N_HEADS, LANES)), _full((1, ATTN_W))],
        out_shape=[SDS((s, ATTN_W), bf16), SDS((s, 2 * KV_W), bf16), SDS((N_HEADS, LANES), f32), SDS((1, ATTN_W), f32)],
        scratch_shapes=[pltpu.VMEM((BLOCK, 2 * KV_W), f32), pltpu.VMEM((1, ATT_ROWS), f32)],
        operands=[z, z, z, o, dy, sink_col, bias, g_a], sem=("arbitrary",))


_V_GC, _V_LNG, _V_LNB, _V_CB, _V_GL, _V_BA, _V_BX, _V_LAM, _V_LB = range(9)
_V_ROWS = 16


def branch_bwd_a(z, conv, hst, dy, p, pieces=None):
    s = z.shape[0]
    tm = _tile(s)
    nt = s // tm
    hb = tm // HALO
    h8 = tm // 8

    def body(conv_ref, dyc_ref, dyl_ref, rx_ref, rxh_ref, rg_ref, hst_ref, hsth_ref,
             lng_ref, lnb_ref, lw_ref, lb_ref, wa_ref, ba_ref, wx_ref, bx_ref, lam_ref, gc_ref, gl_ref,
             dconv_ref, dxc_ref, drg_ref, vec_ref, dwa_ref, dwx_ref, carry_ref):
        i = pl.program_id(0)
        ti = nt - 1 - i

        @pl.when(i == 0)
        def _():
            carry_ref[...] = jnp.zeros_like(carry_ref)
            vec_ref[...] = jnp.zeros_like(vec_ref)
            dwa_ref[...] = jnp.zeros_like(dwa_ref)
            dwx_ref[...] = jnp.zeros_like(dwx_ref)

        conv = conv_ref[...]
        mu = jnp.mean(conv, axis=-1, keepdims=True)
        xm = conv - mu
        rstd = lax.rsqrt(jnp.mean(xm * xm, axis=-1, keepdims=True) + LN_EPS)
        xhat = xm * rstd
        lng = lng_ref[...]
        ln = xhat * lng + lnb_ref[...]
        sg = _sigmoid(ln)
        yc = ln * sg
        gc = gc_ref[...]
        _, rc = _rms_fwd(yc, gc)
        d_yc, d_gc = _rms_bwd(dyc_ref[...], yc, rc, gc)
        d_ln = d_yc * (sg * (1.0 + ln * (1.0 - sg)))
        d_xhat = d_ln * lng
        d_conv = rstd * (d_xhat - jnp.mean(d_xhat, axis=-1, keepdims=True)
                         - xhat * jnp.mean(d_xhat * xhat, axis=-1, keepdims=True))
        dconv_ref[...] = d_conv

        rx = rx_ref[...].astype(f32)
        hrx = jnp.where(ti == 0, 0.0, rxh_ref[...].astype(f32))
        xc = _conv_taps(jnp.concatenate([hrx, rx], axis=0), lw_ref[...], LRU_K) + lb_ref[...]
        wa = wa_ref[...]
        wx = wx_ref[...]
        lam = lam_ref[...]
        r, ig, sp, la, a, mult = _lru_gates(xc, wa, ba_ref[...], wx, bx_ref[...], lam)
        hs = hst_ref[...]
        row = lax.broadcasted_iota(jnp.int32, hs.shape, 0)
        h_before = jnp.where(ti == 0, 0.0, hsth_ref[7:8, :])
        h_prev = jnp.where(row == 0, h_before, pltpu.roll(hs, 1, 0))
        rg = rg_ref[...].astype(f32)
        gl, tg = _gelu(rg)
        out = hs * gl
        gmix = gl_ref[...]
        _, rl = _rms_fwd(out, gmix)
        d_out, d_gl = _rms_bwd(dyl_ref[...], out, rl, gmix)
        drg_ref[...] = (d_out * hs * _gelu_grad(rg, tg)).astype(bf16)
        d_h = d_out * gl
        last = row == tm - 1
        a_next = jnp.where(last, 1.0, pltpu.roll(a, tm - 1, 0))
        lmb = _scan_bwd(a_next, d_h, carry_ref[0:1, :])
        carry_ref[...] = jnp.broadcast_to(a[0:1, :] * lmb[0:1, :], carry_ref.shape)
        d_a = lmb * h_prev
        d_mult = lmb * (ig * xc)
        d_ig = lmb * (mult * xc)
        d_la = d_a * a - d_mult * (a * a) / jnp.maximum(mult, 1e-30)
        d_pa = (d_la * (-LRU_C * sp)) * (r * (1.0 - r))
        d_px = d_ig * (ig * (1.0 - ig))
        d_pab = d_pa.astype(bf16)
        d_pxb = d_px.astype(bf16)
        d_xc = lmb * (mult * ig) + _dot_nt(d_pab, wa) + _dot_nt(d_pxb, wx)
        dxc_ref[...] = d_xc
        xcb = xc.astype(bf16)
        dwa_ref[...] += _dot_tn(xcb, d_pab)
        dwx_ref[...] += _dot_tn(xcb, d_pxb)
        d_lam = jnp.sum(d_la * (-LRU_C * r), axis=0, keepdims=True) * (-_sigmoid(-lam))

        def colsum(v):
            return jnp.sum(v, axis=0, keepdims=True)

        rows = [None] * _V_ROWS
        rows[_V_GC] = d_gc
        rows[_V_LNG] = colsum(d_ln * xhat)
        rows[_V_LNB] = colsum(d_ln)
        rows[_V_CB] = colsum(d_conv)
        rows[_V_GL] = d_gl
        rows[_V_BA] = colsum(d_pa)
        rows[_V_BX] = colsum(d_px)
        rows[_V_LAM] = d_lam
        rows[_V_LB] = colsum(d_xc)
        zero = jnp.zeros((1, CONV_W), f32)
        vec_ref[...] += jnp.concatenate([zero if v is None else v for v in rows], axis=0)

    def rev(c):
        return pl.BlockSpec((tm, CONV_W), lambda i: (nt - 1 - i, c))

    small = [p["lng"], p["lnb"], p["lw"], p["lb"], p["wa"], p["ba"], p["wx"], p["bx"], p["lam"], p["gc"], p["gl"]]
    return _pcall(
        body, pieces, name="branch_bwd_a", grid=(nt,),
        in_specs=[rev(0), rev(2), rev(3), rev(5),
                  pl.BlockSpec((HALO, CONV_W), lambda i: (jnp.maximum((nt - 1 - i) * hb - 1, 0), 5)),
                  rev(6), rev(0),
                  pl.BlockSpec((8, LRU_W), lambda i: (jnp.maximum((nt - 1 - i) * h8 - 1, 0), 0))]
                 + [_full(a.shape) for a in small],
        out_specs=[rev(0), rev(0), rev(0), _full((_V_ROWS, CONV_W)), _full((LRU_W, LRU_W)), _full((LRU_W, LRU_W))],
        out_shape=[SDS((s, CONV_W), f32), SDS((s, LRU_W), f32), SDS((s, LRU_W), bf16), SDS((_V_ROWS, CONV_W), f32),
                   SDS((LRU_W, LRU_W), f32), SDS((LRU_W, LRU_W), f32)],
        scratch_shapes=[pltpu.VMEM((8, LRU_W), f32)],
        operands=[conv, dy, dy, z, z, z, hst, hst, *small], sem=("arbitrary",))


def branch_bwd_b(z, d_conv, d_xc, p, pieces=None):
    s = z.shape[0]
    tm = _tile(s)
    nt = s // tm
    hb = tm // HALO

    def body(cv_ref, cg_ref, cvh_ref, cgh_ref, rx_ref, rxh_ref, dc_ref, dch_ref, dx_ref, dxh_ref, cw_ref, lw_ref,
             dzc_ref, dzr_ref, dcw_ref, dlw_ref):
        i = pl.program_id(0)

        @pl.when(i == 0)
        def _():
            dcw_ref[...] = jnp.zeros_like(dcw_ref)
            dlw_ref[...] = jnp.zeros_like(dlw_ref)

        first = i == 0
        last = i == nt - 1
        cval = cv_ref[...].astype(f32)
        sg = _sigmoid(cg_ref[...].astype(f32))
        u = cval * sg
        hu = jnp.where(first, 0.0, cvh_ref[...].astype(f32) * _sigmoid(cgh_ref[...].astype(f32)))
        dpad = jnp.concatenate([dc_ref[...], jnp.where(last, 0.0, dch_ref[...])], axis=0)
        d_u, dw_rows = _conv_taps_bwd(dpad, jnp.concatenate([hu, u], axis=0), cw_ref[...], CONV_K, tm)
        dcw_ref[...] += jnp.concatenate(dw_rows + [jnp.zeros((HALO - CONV_K, CONV_W), f32)], axis=0)
        dzc_ref[...] = jnp.concatenate([d_u * sg, d_u * cval * sg * (1.0 - sg)], axis=1).astype(bf16)

        rx = rx_ref[...].astype(f32)
        hrx = jnp.where(first, 0.0, rxh_ref[...].astype(f32))
        dxpad = jnp.concatenate([dx_ref[...], jnp.where(last, 0.0, dxh_ref[...])], axis=0)
        d_rx, dlw_rows = _conv_taps_bwd(dxpad, jnp.concatenate([hrx, rx], axis=0), lw_ref[...], LRU_K, tm)
        dlw_ref[...] += jnp.concatenate(dlw_rows + [jnp.zeros((8 - LRU_K, LRU_W), f32)], axis=0)
        dzr_ref[...] = d_rx.astype(bf16)

    def col(c):
        return pl.BlockSpec((tm, CONV_W), lambda i: (i, c))

    def prev(c):
        return pl.BlockSpec((HALO, CONV_W), lambda i: (jnp.maximum(i * hb - 1, 0), c))

    nxt = pl.BlockSpec((HALO, CONV_W), lambda i: (jnp.minimum((i + 1) * hb, nt * hb - 1), 0))
    return _pcall(
        body, pieces, name="branch_bwd_b", grid=(nt,),
        in_specs=[col(3), col(4), prev(3), prev(4), col(5), prev(5), col(0), nxt, col(0), nxt,
                  _full(p["cw"].shape), _full(p["lw"].shape)],
        out_specs=[pl.BlockSpec((tm, 2 * CONV_W), lambda i: (i, 0)), pl.BlockSpec((tm, LRU_W), lambda i: (i, 0)),
                   _full((HALO, CONV_W)), _full((8, LRU_W))],
        out_shape=[SDS((s, 2 * CONV_W), bf16), SDS((s, LRU_W), bf16), SDS((HALO, CONV_W), f32), SDS((8, LRU_W), f32)],
        operands=[z, z, z, z, z, z, d_conv, d_conv, d_xc, d_xc, p["cw"], p["lw"]], sem=("arbitrary",))


def inproj_bwd(dq, dkv, dzc, dzr, drg, h, g1, w_in, dh1, pieces=None):
    s = h.shape[0]
    tm = _tile(s)

    def body(dq_ref, dkv_ref, dzc_ref, dzr_ref, drg_ref, h_ref, g_ref, w_ref, dh1_ref, dh_ref, dw_ref, dg_ref):
        i = pl.program_id(0)

        @pl.when(i == 0)
        def _():
            dw_ref[...] = jnp.zeros_like(dw_ref)
            dg_ref[...] = jnp.zeros_like(dg_ref)

        dz = jnp.concatenate([dq_ref[...], dkv_ref[...], dzc_ref[...], dzr_ref[...], drg_ref[...]], axis=1)
        x = h_ref[...]
        g = g_ref[...]
        hn, r = _rms_fwd(x, g)
        d_hn = _dot_nt(dz, w_ref[...])
        dw_ref[...] += _dot_tn(hn.astype(bf16), dz)
        dx, dg = _rms_bwd(d_hn, x, r, g)
        dh_ref[...] = dh1_ref[...] + dx
        dg_ref[...] += dg

    def rowb(w):
        return pl.BlockSpec((tm, w), lambda i: (i, 0))

    return _pcall(
        body, pieces, name="inproj_bwd", grid=(s // tm,),
        in_specs=[rowb(ATTN_W), rowb(2 * KV_W), rowb(2 * CONV_W), rowb(LRU_W), rowb(LRU_W), rowb(D_MODEL),
                  _full((1, D_MODEL)), _full((D_MODEL, IN_W)), rowb(D_MODEL)],
        out_specs=[rowb(D_MODEL), _full((D_MODEL, IN_W)), _full((1, D_MODEL))],
        out_shape=[SDS((s, D_MODEL), f32), SDS((D_MODEL, IN_W), f32), SDS((1, D_MODEL), f32)],
        operands=[dq, dkv, dzc, dzr, drg, h, g1, w_in, dh1], sem=("arbitrary",), vmem=VMEM_LIMIT)


def _block_diag(w):
    out = jnp.zeros((LRU_W, LRU_W), w.dtype)
    hd = LRU_W // LRU_HEADS
    for hh in range(LRU_HEADS):
        out = out.at[hh * hd:(hh + 1) * hd, hh * hd:(hh + 1) * hd].set(w[hh])
    return out


def _diag_blocks(w):
    hd = LRU_W // LRU_HEADS
    return jnp.stack([w[hh * hd:(hh + 1) * hd, hh * hd:(hh + 1) * hd] for hh in range(LRU_HEADS)])


def _layer_params(sp, l):
    mix = sp["mix_norm"][l]
    return dict(
        g1=sp["norm1"][l][None, :], g2=sp["norm2"][l][None, :],
        sinks=jnp.repeat(sp["attn_sinks"][l], BLOCK)[None, :],
        ga=mix[None, :ATTN_W], gc=mix[None, ATTN_W:ATTN_W + CONV_W], gl=mix[None, ATTN_W + CONV_W:],
        cw=jnp.pad(sp["conv_dw_w"][l], ((0, HALO - CONV_K), (0, 0))), cb=sp["conv_dw_b"][l][None, :],
        lng=sp["conv_ln_g"][l][None, :], lnb=sp["conv_ln_b"][l][None, :],
        lw=jnp.pad(sp["lru_conv_w"][l], ((0, 8 - LRU_K), (0, 0))), lb=sp["lru_conv_b"][l][None, :],
        wa=_block_diag(sp["lru_wa"][l]).astype(bf16), ba=sp["lru_ba"][l].reshape(1, LRU_W),
        wx=_block_diag(sp["lru_wx"][l]).astype(bf16), bx=sp["lru_bx"][l].reshape(1, LRU_W),
        lam=sp["lru_lambda"][l][None, :],
    )


def local_step(x, tgt, big, sp):
    return train_local(x, tgt, sp, LocalWeights(big))


class LocalWeights:
    def __init__(self, big):
        self.big = big
        self.grads = [dict() for _ in range(DEPTH)]

    def weight(self, name, l):
        return self.big[l][name]

    def host(self, point, l):
        return None

    def grad(self, name, l, g):
        self.grads[l][name] = g

    def big_grads(self):
        return self.grads


def train_local(x, tgt, sp, cs):
    lp = [_layer_params(sp, l) for l in range(DEPTH)]
    bias = _attn_bias()
    saved = []
    h = x
    for l in range(DEPTH):
        p = lp[l]
        z = inproj_fwd(h, p["g1"], cs.weight("w_in", l), cs.host("inproj_fwd", l))
        o, ya = attn_fwd(z, p["sinks"], bias, p["ga"], cs.host("attn_fwd", l))
        conv, hst, yc, yl = branch_fwd(z, p, cs.host("branch_fwd", l))
        h1, hn2 = outproj_fwd(ya, yc, yl, h, cs.weight("w_out", l), p["g2"], cs.host("outproj_fwd", l))
        up, h2 = mlp_fwd(hn2, h1, cs.weight("w_up", l), cs.weight("w_dn", l), cs.host("mlp_fwd", l))
        saved.append(dict(h=h, z=z, o=o, ya=ya, conv=conv, hst=hst, yc=yc, yl=yl, h1=h1, hn2=hn2, up=up))
        h = h2
    dh, loss, d_gf = final_loss(h, tgt, sp["final_norm"][None, :])
    small_g = [None] * DEPTH
    for l in reversed(range(DEPTH)):
        p, sv = lp[l], saved[l]
        w_up, w_dn = cs.weight("w_up", l), cs.weight("w_dn", l)
        d_up, dh1, d_g2 = mlp_bwd_act(dh, sv["up"], sv["h1"], p["g2"], w_up, w_dn, cs.host("mlp_bwd_act", l))
        dw_up, dw_dn = mlp_bwd_w(sv["hn2"], d_up, sv["up"], dh, cs.host("mlp_bwd_w", l))
        cs.grad("w_up", l, dw_up)
        cs.grad("w_dn", l, dw_dn)
        dy, dw_out = outproj_bwd(dh1, sv["ya"], sv["yc"], sv["yl"], cs.weight("w_out", l))
        cs.grad("w_out", l, dw_out)
        dq, dkv, d_sk, d_ga = attn_bwd(sv["z"], sv["o"], dy, p["sinks"], bias, p["ga"], cs.host("attn_bwd", l))
        d_conv, d_xc, d_rg, vec, dwa, dwx = branch_bwd_a(sv["z"], sv["conv"], sv["hst"], dy, p, cs.host("branch_bwd_a", l))
        dzc, dzr, dcw, dlw = branch_bwd_b(sv["z"], d_conv, d_xc, p, cs.host("branch_bwd_b", l))
        dh, dw_in, d_g1 = inproj_bwd(dq, dkv, dzc, dzr, d_rg, sv["h"], p["g1"], cs.weight("w_in", l), dh1,
                                     cs.host("inproj_bwd", l))
        cs.grad("w_in", l, dw_in)
        hd = LRU_W // LRU_HEADS
        small_g[l] = dict(
            norm1=d_g1[0], attn_sinks=d_sk[:, 0], conv_dw_w=dcw[:CONV_K], conv_dw_b=vec[_V_CB],
            conv_ln_g=vec[_V_LNG], conv_ln_b=vec[_V_LNB], lru_conv_w=dlw[:LRU_K], lru_conv_b=vec[_V_LB],
            lru_wa=_diag_blocks(dwa), lru_ba=vec[_V_BA].reshape(LRU_HEADS, hd),
            lru_wx=_diag_blocks(dwx), lru_bx=vec[_V_BX].reshape(LRU_HEADS, hd), lru_lambda=vec[_V_LAM],
            mix_norm=jnp.concatenate([d_ga[0], vec[_V_GC], vec[_V_GL]]), norm2=d_g2[0],
        )
    return loss, dh, cs.big_grads(), small_g, d_gf[0]


_HBM = pl.BlockSpec(memory_space=pl.ANY)


def _place():
    x, y, c = lax.axis_index("x"), lax.axis_index("y"), lax.axis_index("c")
    chips = [(1 - x, y), (x, 1 - y), (1 - x, 1 - y)]
    return x, y, c, chips


class Comm:
    def __init__(self, ins, out_shape, aliases, sems, start, finish, done):
        self.ins, self.out_shape, self.aliases, self.sems = list(ins), list(out_shape), dict(aliases), list(sems)
        self.start, self.finish, self.done = start, finish, done


def _pcall(body, pieces, *, name, grid, in_specs, out_specs, out_shape, operands, scratch_shapes=(), sem, vmem=None):
    in_specs, out_specs, out_shape, scratch_shapes = list(in_specs), list(out_specs), list(out_shape), list(scratch_shapes)
    if not pieces:
        return pl.pallas_call(body, name=name, grid=grid, in_specs=in_specs, out_specs=out_specs, out_shape=out_shape,
                              scratch_shapes=scratch_shapes, compiler_params=_params(sem, vmem))(*operands)
    n_in, n_out, n_scr = len(in_specs), len(out_specs), len(scratch_shapes)
    c_ins = [a for p in pieces for a in p.ins]
    c_outs = [s for p in pieces for s in p.out_shape]
    c_sems = [n for p in pieces for n in p.sems]
    aliases, spans, ki, ko, ks = {}, [], 0, 0, 0
    for p in pieces:
        spans.append((ki, ko, ks))
        for a, b in p.aliases.items():
            aliases[n_in + ki + a] = n_out + ko + b
        ki, ko, ks = ki + len(p.ins), ko + len(p.out_shape), ks + len(p.sems)

    def hosted(*refs):
        ins, cin = refs[:n_in], refs[n_in:n_in + ki]
        outs, cout = refs[n_in + ki:n_in + ki + n_out], refs[n_in + ki + n_out:n_in + ki + n_out + ko]
        scr, csem = refs[n_in + ki + n_out + ko:n_in + ki + n_out + ko + n_scr], refs[n_in + ki + n_out + ko + n_scr:]
        first = functools.reduce(jnp.logical_and, [pl.program_id(d) == 0 for d in range(len(grid))])
        last = functools.reduce(jnp.logical_and, [pl.program_id(d) == grid[d] - 1 for d in range(len(grid))])

        def each(which):
            for p, (a, b, s) in zip(pieces, spans):
                getattr(p, which)(cin[a:a + len(p.ins)], cout[b:b + len(p.out_shape)], csem[s:s + len(p.sems)])

        @pl.when(first)
        def _():
            each("start")

        body(*ins, *outs, *scr)

        @pl.when(last)
        def _():
            each("finish")

    res = pl.pallas_call(
        hosted, name=name + "_host", grid=grid, in_specs=in_specs + [_HBM] * ki, out_specs=out_specs + [_HBM] * ko,
        out_shape=out_shape + c_outs, scratch_shapes=scratch_shapes + [pltpu.SemaphoreType.DMA((n,)) for n in c_sems],
        input_output_aliases=aliases, compiler_params=_params(("arbitrary",) * len(grid), vmem),
    )(*operands, *c_ins)
    for p, (a, b, s) in zip(pieces, spans):
        p.done(res[n_out + b:n_out + b + len(p.out_shape)])
    return res[:n_out]


def standalone(pieces, name):
    ki = sum(len(p.ins) for p in pieces)
    ko = sum(len(p.out_shape) for p in pieces)
    spans, a, b, s = [], 0, 0, 0
    aliases = {}
    for p in pieces:
        spans.append((a, b, s))
        for i, o in p.aliases.items():
            aliases[a + i] = b + o
        a, b, s = a + len(p.ins), b + len(p.out_shape), s + len(p.sems)

    def body(*refs):
        cin, cout, csem = refs[:ki], refs[ki:ki + ko], refs[ki + ko:]
        for which in ("start", "finish"):
            for p, (a, b, s) in zip(pieces, spans):
                getattr(p, which)(cin[a:a + len(p.ins)], cout[b:b + len(p.out_shape)], csem[s:s + len(p.sems)])

    res = pl.pallas_call(
        body, name=name, in_specs=[_HBM] * ki, out_specs=[_HBM] * ko, out_shape=[s for p in pieces for s in p.out_shape],
        scratch_shapes=[pltpu.SemaphoreType.DMA((n,)) for p in pieces for n in p.sems], input_output_aliases=aliases,
    )(*[a for p in pieces for a in p.ins])
    for p, (a, b, s) in zip(pieces, spans):
        p.done(res[b:b + len(p.out_shape)])


def _rows_half(ref, which, rows):
    return ref.at[pl.ds(pl.multiple_of(which * rows, 8), rows)]


def gather_ici_piece(bufs, done):
    n = len(bufs)

    def copies(cout):
        x, y, c, chips = _place()
        out = []
        for j, (cx, cy) in enumerate(chips):
            for w in range(n):
                half = bufs[w].shape[1] // 2
                out.append((j, w, _rows_half(cout[w].at[2 * x + y], c, half), _rows_half(cout[w].at[2 * cx + cy], c, half),
                            (cx, cy, c)))
        return out

    def start(cin, cout, sems):
        for j, w, mine, _, to in copies(cout):
            pltpu.make_async_remote_copy(src_ref=mine, dst_ref=mine, send_sem=sems[0].at[n * j + w],
                                         recv_sem=sems[1].at[n * j + w], device_id=to, device_id_type=MESH).start()

    def finish(cin, cout, sems):
        for j, w, mine, landed, to in copies(cout):
            pltpu.make_async_remote_copy(src_ref=mine, dst_ref=landed, send_sem=sems[0].at[n * j + w],
                                         recv_sem=sems[1].at[n * j + w], device_id=to, device_id_type=MESH).wait()

    return Comm(bufs, [SDS(b.shape, b.dtype) for b in bufs], {w: w for w in range(n)}, [3 * n, 3 * n], start, finish, done)


def gather_full_piece(bufs, done):
    n = len(bufs)

    def copies(cout):
        x, y, c, chips = _place()
        return [(n * j + w, cout[w].at[2 * x + y], cout[w].at[2 * cx + cy], (cx, cy, c))
                for j, (cx, cy) in enumerate(chips) for w in range(n)]

    def start(cin, cout, sems):
        for k, mine, _, to in copies(cout):
            pltpu.make_async_remote_copy(src_ref=mine, dst_ref=mine, send_sem=sems[0].at[k], recv_sem=sems[1].at[k],
                                         device_id=to, device_id_type=MESH).start()

    def finish(cin, cout, sems):
        for k, mine, landed, to in copies(cout):
            pltpu.make_async_remote_copy(src_ref=mine, dst_ref=landed, send_sem=sems[0].at[k], recv_sem=sems[1].at[k],
                                         device_id=to, device_id_type=MESH).wait()

    return Comm(bufs, [SDS(b.shape, b.dtype) for b in bufs], {w: w for w in range(n)}, [3 * n, 3 * n], start, finish, done)


def gather_d2d_piece(bufs, done):
    n = len(bufs)

    def copies(cout):
        x, y, c, chips = _place()
        out = []
        for j, (cx, cy) in enumerate(chips):
            for w in range(n):
                half = bufs[w].shape[1] // 2
                slot = cout[w].at[2 * cx + cy]
                out.append((n * j + w, _rows_half(slot, c, half), _rows_half(slot, 1 - c, half), (x, y, 1 - c)))
        return out

    def start(cin, cout, sems):
        for k, mine, _, to in copies(cout):
            pltpu.make_async_remote_copy(src_ref=mine, dst_ref=mine, send_sem=sems[0].at[k], recv_sem=sems[1].at[k],
                                         device_id=to, device_id_type=MESH).start()

    def finish(cin, cout, sems):
        for k, mine, theirs, to in copies(cout):
            pltpu.make_async_remote_copy(src_ref=mine, dst_ref=theirs, send_sem=sems[0].at[k], recv_sem=sems[1].at[k],
                                         device_id=to, device_id_type=MESH).wait()

    return Comm(bufs, [SDS(b.shape, b.dtype) for b in bufs], {w: w for w in range(n)}, [3 * n, 3 * n], start, finish, done)


def pair_piece(parts, done):
    n = len(parts)

    def copies(cin, cout):
        x, y, c, _ = _place()
        out = []
        for w in range(n):
            half = parts[w].shape[1] // 2
            out.append((w, cin[w].at[:, pl.ds(pl.multiple_of((1 - c) * half, 8), half), :], cout[w], (x, y, 1 - c)))
        return out

    def start(cin, cout, sems):
        for w, src, dst, to in copies(cin, cout):
            pltpu.make_async_remote_copy(src_ref=src, dst_ref=dst, send_sem=sems[0].at[w], recv_sem=sems[1].at[w],
                                         device_id=to, device_id_type=MESH).start()

    def finish(cin, cout, sems):
        for w, src, dst, to in copies(cin, cout):
            pltpu.make_async_remote_copy(src_ref=src, dst_ref=dst, send_sem=sems[0].at[w], recv_sem=sems[1].at[w],
                                         device_id=to, device_id_type=MESH).wait()

    return Comm(parts, [SDS((N_SHARD, a.shape[1] // 2, a.shape[2]), f32) for a in parts], {}, [n, n], start, finish, done)


def shard_piece(sums16, done):
    n = len(sums16)

    def copies(cin, cout):
        x, y, c, chips = _place()
        return [(n * j + w, cin[w].at[2 * cx + cy], cout[w].at[j], (cx, cy, c))
                for j, (cx, cy) in enumerate(chips) for w in range(n)]

    def start(cin, cout, sems):
        for k, src, dst, to in copies(cin, cout):
            pltpu.make_async_remote_copy(src_ref=src, dst_ref=dst, send_sem=sems[0].at[k], recv_sem=sems[1].at[k],
                                         device_id=to, device_id_type=MESH).start()

    def finish(cin, cout, sems):
        for k, src, dst, to in copies(cin, cout):
            pltpu.make_async_remote_copy(src_ref=src, dst_ref=dst, send_sem=sems[0].at[k], recv_sem=sems[1].at[k],
                                         device_id=to, device_id_type=MESH).wait()

    return Comm(sums16, [SDS((3,) + a.shape[1:], bf16) for a in sums16], {}, [3 * n, 3 * n], start, finish, done)


def place_shard(a, layer, idx, dtype):
    _, r, cdim = a.shape
    tr = min(r, 512)

    def body(idx_ref, a_ref, o_ref):
        o_ref[0] = a_ref[0].astype(dtype)

    return pl.pallas_call(
        body, name="place_shard",
        grid_spec=pltpu.PrefetchScalarGridSpec(
            num_scalar_prefetch=1, grid=(r // tr,),
            in_specs=[pl.BlockSpec((1, tr, cdim), lambda i, idx_ref: (layer, i, 0))],
            out_specs=pl.BlockSpec((1, tr, cdim), lambda i, idx_ref: (idx_ref[1], i, 0))),
        out_shape=SDS((N_SHARD, r, cdim), dtype),
        compiler_params=_params(("arbitrary",)),
    )(idx, a)


_KEYS = ("w_in", "w_out", "w_up", "w_dn")


class MeshWeights:
    def __init__(self, w_in, w_out, w_up, w_down, conv_dw_w, lru_conv_w, idx):
        self.idx = idx
        src = dict(w_in=w_in, w_out=w_out, w_up=w_up, w_dn=w_down)
        self.buf = {(n, l): place_shard(src[n], l, idx, bf16) for n in _KEYS for l in range(DEPTH)}
        self.conv = {(n, l): place_shard(a, l, idx, f32)
                     for n, a in (("cw", conv_dw_w), ("lw", lru_conv_w)) for l in range(DEPTH)}
        self.cache, self.parts, self.sum32, self.sum16, self.got = {}, {}, {}, {}, {}
        first, small = [("w_in", 0)], list(self.conv)

        def store_conv(outs):
            self.conv.update(zip(small, outs))

        standalone([self._gather(gather_ici_piece, first), gather_full_piece([self.conv[k] for k in small], store_conv)],
                   "gather_first_ici")
        standalone([self._gather(gather_d2d_piece, first)], "gather_first_d2d")

    def _gather(self, piece, keys):
        def done(outs):
            self.buf.update(zip(keys, outs))
        return piece([self.buf[k] for k in keys], done)

    def _pair(self, keys):
        def done(outs):
            for k, recv in zip(keys, outs):
                self.sum32[k], self.sum16[k] = chip_sum(self.parts[k], recv, self.idx)
        return pair_piece([self.parts[k] for k in keys], done)

    def _shard(self, keys):
        def done(outs):
            self.got.update(zip(keys, outs))
        return shard_piece([self.sum16[k] for k in keys], done)

    def conv_weights(self):
        out = []
        for n in ("cw", "lw"):
            a = jnp.stack([self.conv[(n, l)] for l in range(DEPTH)])
            out.append(a.transpose(0, 2, 1, 3).reshape(DEPTH, a.shape[2], N_SHARD * a.shape[3]))
        return out

    def weight(self, name, l):
        if (name, l) not in self.cache:
            b = self.buf[(name, l)]
            if name == "w_in":
                b = b.transpose(1, 0, 2).reshape(D_MODEL, IN_W)
            elif name == "w_out":
                b = b.reshape(D_MODEL, D_MODEL)
            self.cache[(name, l)] = b
        return self.cache[(name, l)]

    def host(self, point, l):
        ici, d2d = gather_ici_piece, gather_d2d_piece
        rest1 = [("w_out", 1), ("w_up", 1), ("w_dn", 1)]
        plan = {
            ("inproj_fwd", 0): lambda: [self._gather(ici, [("w_out", 0)])],
            ("attn_fwd", 0): lambda: [self._gather(ici, [("w_up", 0)]), self._gather(d2d, [("w_out", 0)])],
            ("branch_fwd", 0): lambda: [self._gather(ici, [("w_dn", 0)]), self._gather(d2d, [("w_up", 0)])],
            ("outproj_fwd", 0): lambda: [self._gather(d2d, [("w_dn", 0)]), self._gather(ici, [("w_in", 1)])],
            ("mlp_fwd", 0): lambda: [self._gather(ici, rest1), self._gather(d2d, [("w_in", 1)])],
            ("attn_fwd", 1): lambda: [self._gather(d2d, rest1)],
            ("attn_bwd", 1): lambda: [self._pair([("w_up", 1), ("w_dn", 1)])],
            ("inproj_bwd", 1): lambda: [self._pair([("w_out", 1)])],
            ("mlp_bwd_act", 0): lambda: [self._shard([("w_up", 1), ("w_dn", 1)])],
            ("mlp_bwd_w", 0): lambda: [self._pair([("w_in", 1)])],
            ("attn_bwd", 0): lambda: [self._shard([("w_in", 1), ("w_out", 1)]),
                                      self._pair([("w_up", 0), ("w_dn", 0), ("w_out", 0)])],
            ("branch_bwd_a", 0): lambda: [self._shard([("w_up", 0)])],
            ("branch_bwd_b", 0): lambda: [self._shard([("w_dn", 0), ("w_out", 0)])],
        }
        make = plan.get((point, l))
        return make() if make else None

    def grad(self, name, l, g):
        if name == "w_in":
            g = g.reshape(D_MODEL, N_SHARD, IN_W // N_SHARD).transpose(1, 0, 2)
        elif name == "w_out":
            g = g.reshape(N_SHARD, D_MODEL // N_SHARD, D_MODEL)
        self.parts[(name, l)] = g

    def big_grads(self):
        last = [("w_in", 0)]
        standalone([self._pair(last)], "pair_last")
        standalone([self._shard(last)], "shard_last")
        tots = []
        for n in _KEYS:
            t = None
            for l in reversed(range(DEPTH)):
                t = shard_sum(self.sum32[(n, l)], self.got[(n, l)], self.idx, l, t)
            tots.append(t)
        return tots


def chip_sum(g, recv, idx):
    _, r, cdim = g.shape
    half = r // 2
    tr = min(half, 512)
    nh = half // tr

    def body(idx_ref, g_ref, r_ref, o32_ref, o16_ref):
        tot = g_ref[...] + r_ref[...]
        o32_ref[...] = tot
        o16_ref[...] = tot.astype(bf16)

    blk = pl.BlockSpec((1, tr, cdim), lambda s, i, idx_ref: (s, i, 0))
    return pl.pallas_call(
        body, name="chip_sum",
        grid_spec=pltpu.PrefetchScalarGridSpec(
            num_scalar_prefetch=1, grid=(N_SHARD, nh),
            in_specs=[pl.BlockSpec((1, tr, cdim), lambda s, i, idx_ref: (s, idx_ref[0] * nh + i, 0)), blk],
            out_specs=[blk, blk]),
        out_shape=[SDS((N_SHARD, half, cdim), f32), SDS((N_SHARD, half, cdim), bf16)],
        compiler_params=_params(("arbitrary", "arbitrary")),
    )(idx, g, recv)


def shard_sum(sum32, got16, idx, layer, prev):
    _, half, cdim = sum32.shape
    tr = min(half, 512)

    def body(idx_ref, a_ref, r0_ref, r1_ref, r2_ref, *rest):
        o_ref = rest[-1]
        o_ref[0, 0] = ((a_ref[0] + r0_ref[0].astype(f32)) + r1_ref[0].astype(f32)) + r2_ref[0].astype(f32)

    def rel(j):
        return pl.BlockSpec((1, tr, cdim), lambda i, idx_ref: (j, i, 0))

    in_specs = [pl.BlockSpec((1, tr, cdim), lambda i, idx_ref: (idx_ref[1], i, 0)), rel(0), rel(1), rel(2)]
    operands = [idx, sum32, got16, got16, got16]
    aliases = {}
    if prev is not None:
        in_specs.append(_HBM)
        operands.append(prev)
        aliases = {5: 0}
    return pl.pallas_call(
        body, name="shard_sum",
        grid_spec=pltpu.PrefetchScalarGridSpec(
            num_scalar_prefetch=1, grid=(half // tr,), in_specs=in_specs,
            out_specs=pl.BlockSpec((1, 1, tr, cdim), lambda i, idx_ref: (layer, idx_ref[0], i, 0))),
        out_shape=SDS((DEPTH, 2, half, cdim), f32), input_output_aliases=aliases,
        compiler_params=_params(("arbitrary",)),
    )(*operands)


def halves_exchange(tots):
    nw = len(tots)

    def body(*refs):
        bufs = refs[nw:2 * nw]
        send_sem, recv_sem = refs[2 * nw:]
        x, y, c, _ = _place()

        def copy(w, l, half_idx):
            return pltpu.make_async_remote_copy(
                src_ref=bufs[w].at[l, half_idx], dst_ref=bufs[w].at[l, half_idx], send_sem=send_sem.at[DEPTH * w + l],
                recv_sem=recv_sem.at[DEPTH * w + l], device_id=(x, y, 1 - c), device_id_type=MESH)

        sends = [copy(w, l, c) for w in range(nw) for l in range(DEPTH)]
        for cp in sends:
            cp.start()
        for w in range(nw):
            for l in range(DEPTH):
                copy(w, l, 1 - c).wait_recv()
        for cp in sends:
            cp.wait_send()

    return pl.pallas_call(
        body, name="halves_exchange", in_specs=[_HBM] * nw, out_specs=[_HBM] * nw,
        out_shape=[SDS(a.shape, f32) for a in tots], input_output_aliases={w: w for w in range(nw)},
        scratch_shapes=[pltpu.SemaphoreType.DMA((DEPTH * nw,)), pltpu.SemaphoreType.DMA((DEPTH * nw,))],
    )(*tots)


N_DEV = 8


def small_allreduce(vec):
    r = vec.shape[0]

    def body(v_ref, o_ref, all_ref, send_sems, recv_sems):
        x, y, c, chips = _place()
        me, sib = (x, y, c), (x, y, 1 - c)

        def rows(px, py, pc):
            return all_ref.at[4 * px + 2 * py + pc]

        def copy(k, block, to, src=None):
            return pltpu.make_async_remote_copy(
                src_ref=rows(*block) if src is None else src, dst_ref=rows(*block), send_sem=send_sems.at[k],
                recv_sem=recv_sems.at[k], device_id=to, device_id_type=MESH)

        first = [copy(0, me, sib, src=v_ref)]
        first += [copy(1 + j, me, (*chip, c), src=v_ref) for j, chip in enumerate(chips)]
        for cp in first:
            cp.start()
        rows(*me)[...] = v_ref[...]
        passed = [copy(4 + j, (*chip, c), sib) for j, chip in enumerate(chips)]
        for j, chip in enumerate(chips):
            copy(1 + j, (*chip, c), me).wait_recv()
            passed[j].start()
        copy(0, sib, me).wait_recv()
        for j, chip in enumerate(chips):
            copy(4 + j, (*chip, 1 - c), me).wait_recv()
        for cp in first + passed:
            cp.wait_send()
        acc = all_ref[0]
        for d in range(1, N_DEV):
            acc = acc + all_ref[d]
        o_ref[...] = acc

    return pl.pallas_call(
        body, name="small_allreduce",
        in_specs=[pl.BlockSpec(memory_space=pltpu.VMEM)], out_specs=pl.BlockSpec(memory_space=pltpu.VMEM),
        out_shape=SDS((r, LANES), f32),
        scratch_shapes=[pltpu.VMEM((N_DEV, r, LANES), f32), pltpu.SemaphoreType.DMA((7,)), pltpu.SemaphoreType.DMA((7,))],
    )(vec)


def _adamw_math(w, g, m, v):
    m = ADAM_B1 * m + (1.0 - ADAM_B1) * g
    v = ADAM_B2 * v + (1.0 - ADAM_B2) * (g * g)
    m_hat = m / (1.0 - ADAM_B1 ** ADAM_STEP)
    v_hat = v / (1.0 - ADAM_B2 ** ADAM_STEP)
    delta = -ADAM_LR * (m_hat / (jnp.sqrt(v_hat) + ADAM_EPS) + ADAM_WD * w)
    return delta, m, v


def adamw_big(w, g, m, v):
    _, r, cdim = w.shape
    tr = min(r, 256)

    def body(w_ref, g_ref, m_ref, v_ref, go_ref, d_ref, mo_ref, vo_ref):
        g = g_ref[...]
        d, mm, vv = _adamw_math(w_ref[...], g, m_ref[...], v_ref[...])
        go_ref[...] = g
        d_ref[...] = d
        mo_ref[...] = mm
        vo_ref[...] = vv

    blk = pl.BlockSpec((1, tr, cdim), lambda l, i: (l, i, 0))
    return pl.pallas_call(
        body, name="adamw_big", grid=(DEPTH, r // tr), in_specs=[blk] * 4, out_specs=[blk] * 4,
        out_shape=[SDS(w.shape, f32)] * 4, compiler_params=_params(("parallel", "parallel")),
    )(w, g, m, v)


def adamw_small(ws, gs, ms, vs):
    n = len(ws)

    def body(*refs):
        w_r, g_r, m_r, v_r = refs[:n], refs[n:2 * n], refs[2 * n:3 * n], refs[3 * n:4 * n]
        d_o, m_o, v_o = refs[4 * n:5 * n], refs[5 * n:6 * n], refs[6 * n:7 * n]
        for k in range(n):
            d, mm, vv = _adamw_math(w_r[k][...], g_r[k][...], m_r[k][...], v_r[k][...])
            d_o[k][...] = d
            m_o[k][...] = mm
            v_o[k][...] = vv

    vm = pl.BlockSpec(memory_space=pltpu.VMEM)
    shapes = [SDS(a.shape, f32) for a in ws]
    outs = pl.pallas_call(
        body, name="adamw_small", in_specs=[vm] * (4 * n), out_specs=[vm] * (3 * n), out_shape=shapes * 3,
    )(*ws, *gs, *ms, *vs)
    return outs[:n], outs[n:2 * n], outs[2 * n:]


_BIG = ("w_in", "w_out", "w_up", "w_down")
_WEIGHTS = ("norm1", "w_in", "attn_sinks", "conv_dw_w", "conv_dw_b", "conv_ln_g", "conv_ln_b", "lru_conv_w", "lru_conv_b",
            "lru_wa", "lru_ba", "lru_wx", "lru_bx", "lru_lambda", "mix_norm", "w_out", "norm2", "w_up", "w_down", "final_norm")
_SMALL = tuple(n for n in _WEIGHTS if n not in _BIG)
_SMALL_FULL_SHAPE = dict(
    norm1=(DEPTH, D_MODEL), attn_sinks=(DEPTH, N_HEADS), conv_dw_w=(DEPTH, CONV_K, CONV_W), conv_dw_b=(DEPTH, CONV_W),
    conv_ln_g=(DEPTH, CONV_W), conv_ln_b=(DEPTH, CONV_W), lru_conv_w=(DEPTH, LRU_K, LRU_W), lru_conv_b=(DEPTH, LRU_W),
    lru_wa=(DEPTH, LRU_HEADS, 64, 64), lru_ba=(DEPTH, LRU_HEADS, 64), lru_wx=(DEPTH, LRU_HEADS, 64, 64),
    lru_bx=(DEPTH, LRU_HEADS, 64), lru_lambda=(DEPTH, LRU_W), mix_norm=(DEPTH, D_MODEL), norm2=(DEPTH, D_MODEL),
    final_norm=(D_MODEL,))
_CHANNEL_SHARDED = ("conv_dw_w", "lru_conv_w")


def _pad_lanes(n):
    return -(-n // LANES) * LANES


def _pack(named):
    flat = []
    for a in named:
        a = a.reshape(-1)
        flat.append(jnp.pad(a, (0, _pad_lanes(a.shape[0]) - a.shape[0])))
    v = jnp.concatenate(flat)
    rows = -(-v.shape[0] // (8 * LANES)) * 8
    return jnp.pad(v, (0, rows * LANES - v.shape[0])).reshape(rows, LANES)


def _unpack(vec, shapes):
    flat = vec.reshape(-1)
    out, off = [], 0
    for shp in shapes:
        n = math.prod(shp)
        out.append(flat[off:off + n].reshape(shp))
        off += _pad_lanes(n)
    return out


def _as2d(a):
    return a.reshape(-1, a.shape[-1]) if a.ndim > 1 else a.reshape(1, -1)


def kernel(x, norm1, w_in, attn_sinks, conv_dw_w, conv_dw_b, conv_ln_g, conv_ln_b, lru_conv_w, lru_conv_b, lru_wa, lru_ba, lru_wx, lru_bx, lru_lambda, mix_norm, w_out, norm2, w_up, w_down, final_norm, loss_target, m_norm1, m_w_in, m_attn_sinks, m_conv_dw_w, m_conv_dw_b, m_conv_ln_g, m_conv_ln_b, m_lru_conv_w, m_lru_conv_b, m_lru_wa, m_lru_ba, m_lru_wx, m_lru_bx, m_lru_lambda, m_mix_norm, m_w_out, m_norm2, m_w_up, m_w_down, m_final_norm, v_norm1, v_w_in, v_attn_sinks, v_conv_dw_w, v_conv_dw_b, v_conv_ln_g, v_conv_ln_b, v_lru_conv_w, v_lru_conv_b, v_lru_wa, v_lru_ba, v_lru_wx, v_lru_bx, v_lru_lambda, v_mix_norm, v_w_out, v_norm2, v_w_up, v_w_down, v_final_norm):
    wts = dict(norm1=norm1, w_in=w_in, attn_sinks=attn_sinks, conv_dw_w=conv_dw_w, conv_dw_b=conv_dw_b, conv_ln_g=conv_ln_g,
               conv_ln_b=conv_ln_b, lru_conv_w=lru_conv_w, lru_conv_b=lru_conv_b, lru_wa=lru_wa, lru_ba=lru_ba, lru_wx=lru_wx,
               lru_bx=lru_bx, lru_lambda=lru_lambda, mix_norm=mix_norm, w_out=w_out, norm2=norm2, w_up=w_up, w_down=w_down,
               final_norm=final_norm)
    mom = dict(norm1=m_norm1, w_in=m_w_in, attn_sinks=m_attn_sinks, conv_dw_w=m_conv_dw_w, conv_dw_b=m_conv_dw_b,
               conv_ln_g=m_conv_ln_g, conv_ln_b=m_conv_ln_b, lru_conv_w=m_lru_conv_w, lru_conv_b=m_lru_conv_b, lru_wa=m_lru_wa,
               lru_ba=m_lru_ba, lru_wx=m_lru_wx, lru_bx=m_lru_bx, lru_lambda=m_lru_lambda, mix_norm=m_mix_norm, w_out=m_w_out,
               norm2=m_norm2, w_up=m_w_up, w_down=m_w_down, final_norm=m_final_norm)
    var = dict(norm1=v_norm1, w_in=v_w_in, attn_sinks=v_attn_sinks, conv_dw_w=v_conv_dw_w, conv_dw_b=v_conv_dw_b,
               conv_ln_g=v_conv_ln_g, conv_ln_b=v_conv_ln_b, lru_conv_w=v_lru_conv_w, lru_conv_b=v_lru_conv_b, lru_wa=v_lru_wa,
               lru_ba=v_lru_ba, lru_wx=v_lru_wx, lru_bx=v_lru_bx, lru_lambda=v_lru_lambda, mix_norm=v_mix_norm, w_out=v_w_out,
               norm2=v_norm2, w_up=v_w_up, w_down=v_w_down, final_norm=v_final_norm)

    c_idx = lax.axis_index("c").astype(jnp.int32)
    s_idx = (2 * lax.axis_index("x") + lax.axis_index("y")).astype(jnp.int32)
    idx = jnp.stack([c_idx, s_idx])

    cs = MeshWeights(w_in, w_out, w_up, w_down, conv_dw_w, lru_conv_w, idx)
    sp = {n: wts[n] for n in _SMALL}
    sp["conv_dw_w"], sp["lru_conv_w"] = cs.conv_weights()
    loss_blk, grad_x, tots, small_g, d_gf = train_local(x[0], loss_target[0], sp, cs)
    grads_big = {n: a.reshape(wts[n].shape) for n, a in zip(_BIG, halves_exchange(tots))}

    stacked = [jnp.stack([small_g[l][n] for l in range(DEPTH)]) for n in _SMALL if n != "final_norm"]
    packed = _pack(stacked + [d_gf, loss_blk[0, 0:1]])
    summed = small_allreduce(packed)
    names = [n for n in _SMALL if n != "final_norm"] + ["final_norm"]
    unpacked = _unpack(summed, [_SMALL_FULL_SHAPE[n] for n in names] + [(1,)])
    loss = unpacked[-1][0]
    grads = dict(zip(names, unpacked[:-1]))
    for n in _CHANNEL_SHARDED:
        width = wts[n].shape[-1]
        grads[n] = lax.dynamic_slice_in_dim(grads[n], s_idx * width, width, axis=2)
    grads.update(grads_big)

    delta, new_m, new_v = {}, {}, {}
    for n in _BIG:
        grads[n], delta[n], new_m[n], new_v[n] = adamw_big(wts[n], grads[n], mom[n], var[n])
    sm = list(_SMALL)
    d_s, m_s, v_s = adamw_small([_as2d(wts[n]) for n in sm], [_as2d(grads[n]) for n in sm],
                                [_as2d(mom[n]) for n in sm], [_as2d(var[n]) for n in sm])
    for k, n in enumerate(sm):
        delta[n], new_m[n], new_v[n] = (a.reshape(wts[n].shape) for a in (d_s[k], m_s[k], v_s[k]))

    return (loss, grad_x[None], *[grads[n] for n in _WEIGHTS], *[delta[n] for n in _WEIGHTS],
            *[new_m[n] for n in _WEIGHTS], *[new_v[n] for n in _WEIGHTS])
```

```python
import functools
import math

import jax
import jax.numpy as jnp
from jax import lax
from jax.experimental import pallas as pl
from jax.experimental.pallas import tpu as pltpu

f32 = jnp.float32
bf16 = jnp.bfloat16
SDS = jax.ShapeDtypeStruct

D_MODEL = 1024
DEPTH = 2
ATTN_W = 512
KV_W = 128
HEAD_DIM = 64
N_HEADS = 8
BLOCK = 128
CONV_W = 256
CONV_K = 31
LRU_W = 256
LRU_K = 4
LRU_HEADS = 4
LRU_C = 8.0
IN_W = 1792
D_FF = 4096
N_SHARD = 4
FF_CHUNK = D_FF // N_SHARD
RMS_EPS = 1e-6
LN_EPS = 1e-5
MASK_VALUE = -1e30
HALO = 32
LANES = 128
VMEM_LIMIT = 56 * 1024 * 1024

ADAM_LR = 0.001
ADAM_B1 = 0.9
ADAM_B2 = 0.999
ADAM_EPS = 1e-08
ADAM_WD = 0.01
ADAM_STEP = 10

MESH = pl.DeviceIdType.MESH


def _dot(a, b):
    return jnp.dot(a, b, preferred_element_type=f32)


def _dot_nt(a, b):
    return lax.dot_general(a, b, (((1,), (1,)), ((), ())), preferred_element_type=f32)


def _dot_tn(a, b):
    return lax.dot_general(a, b, (((0,), (0,)), ((), ())), preferred_element_type=f32)


def _rms_fwd(x, g):
    r = lax.rsqrt(jnp.mean(x * x, axis=-1, keepdims=True) + RMS_EPS)
    return x * r * g, r


def _rms_bwd(dy, x, r, g):
    t = dy * g
    dx = r * t - x * (r * r * r) * jnp.mean(t * x, axis=-1, keepdims=True)
    dg = jnp.sum(dy * x * r, axis=0, keepdims=True)
    return dx, dg


def _sigmoid(x):
    return jax.nn.sigmoid(x)


_GELU_K = math.sqrt(2.0 / math.pi)


def _gelu(x):
    t = jnp.tanh(_GELU_K * (x + 0.044715 * x * x * x))
    return 0.5 * x * (1.0 + t), t


def _gelu_grad(x, t):
    return 0.5 * (1.0 + t) + 0.5 * x * (1.0 - t * t) * _GELU_K * (1.0 + 3.0 * 0.044715 * x * x)


def _log1p(x):
    return jnp.where(x < 1e-4, x - 0.5 * x * x, jnp.log(1.0 + x))


def _softplus(x):
    return jnp.maximum(x, 0.0) + _log1p(jnp.exp(-jnp.abs(x)))


def _neg_expm1(x):
    series = -x * (1.0 + 0.5 * x * (1.0 + x * (1.0 / 3.0) * (1.0 + 0.25 * x)))
    return jnp.where(x > -0.01, series, 1.0 - jnp.exp(x))


def _sublane_rolls(x, count, forward):
    n = x.shape[0]
    return [x if b == 0 else pltpu.roll(x, b if forward else n - b, 0) for b in range(count)]


def _conv_taps(xpad, w, k_width):
    t_rows = xpad.shape[0] - HALO
    rolled = _sublane_rolls(xpad, min(k_width, 8), forward=True)
    acc = None
    for k in range(k_width):
        hi, lo = divmod((k_width - 1) - k, 8)
        term = rolled[lo][HALO - 8 * hi:HALO - 8 * hi + t_rows] * w[k:k + 1, :]
        acc = term if acc is None else acc + term
    return acc


def _conv_taps_bwd(dpad, upad, w, k_width, t_rows):
    n_lo = min(k_width, 8)
    d_rolled = _sublane_rolls(dpad, n_lo, forward=False)
    u_rolled = _sublane_rolls(upad, n_lo, forward=True)
    d_in = None
    dw_rows = []
    d_out = dpad[:t_rows]
    for k in range(k_width):
        hi, lo = divmod((k_width - 1) - k, 8)
        term = d_rolled[lo][8 * hi:8 * hi + t_rows] * w[k:k + 1, :]
        d_in = term if d_in is None else d_in + term
        us = u_rolled[lo][HALO - 8 * hi:HALO - 8 * hi + t_rows]
        dw_rows.append(jnp.sum(d_out * us, axis=0, keepdims=True))
    return d_in, dw_rows


SUBLANES = 8


def _scan_fwd(a, b, h0):
    t_rows = a.shape[0]
    sub = jnp.bitwise_and(lax.broadcasted_iota(jnp.int32, a.shape, 0), SUBLANES - 1)
    for d in (1, 2, 4):
        a_sh = jnp.where(sub < d, 1.0, pltpu.roll(a, d, 0))
        b_sh = jnp.where(sub < d, 0.0, pltpu.roll(b, d, 0))
        b = a * b_sh + b
        a = a * a_sh
    out, carry = [], h0
    for g in range(t_rows // SUBLANES):
        rows = slice(g * SUBLANES, (g + 1) * SUBLANES)
        hg = a[rows] * carry + b[rows]
        out.append(hg)
        carry = hg[SUBLANES - 1:SUBLANES]
    return jnp.concatenate(out, axis=0)


def _scan_bwd(a, b, l_end):
    t_rows = a.shape[0]
    sub = jnp.bitwise_and(lax.broadcasted_iota(jnp.int32, a.shape, 0), SUBLANES - 1)
    for d in (1, 2, 4):
        a_sh = jnp.where(sub >= SUBLANES - d, 1.0, pltpu.roll(a, t_rows - d, 0))
        b_sh = jnp.where(sub >= SUBLANES - d, 0.0, pltpu.roll(b, t_rows - d, 0))
        b = b + a * b_sh
        a = a * a_sh
    out, carry = [], l_end
    for g in reversed(range(t_rows // SUBLANES)):
        rows = slice(g * SUBLANES, (g + 1) * SUBLANES)
        lg = b[rows] + a[rows] * carry
        out.append(lg)
        carry = lg[0:1]
    return jnp.concatenate(out[::-1], axis=0)


def _full(shape, single=False):
    nd = len(shape)
    if single:
        return pl.BlockSpec(shape, lambda *_: (0,) * nd, pipeline_mode=pl.Buffered(1))
    return pl.BlockSpec(shape, lambda *_: (0,) * nd)


def _params(sem, vmem=None):
    return pltpu.CompilerParams(dimension_semantics=sem, vmem_limit_bytes=vmem)


def _tile(s):
    return min(512, s)


def inproj_fwd(h, g1, w_in, pieces=None):
    s = h.shape[0]
    tm = _tile(s)

    def body(h_ref, g_ref, w_ref, z_ref):
        hn, _ = _rms_fwd(h_ref[...], g_ref[...])
        z_ref[...] = _dot(hn.astype(bf16), w_ref[...]).astype(bf16)

    return _pcall(
        body, pieces, name="inproj_fwd", grid=(s // tm,),
        in_specs=[pl.BlockSpec((tm, D_MODEL), lambda i: (i, 0)), _full((1, D_MODEL)), _full((D_MODEL, IN_W))],
        out_specs=[pl.BlockSpec((tm, IN_W), lambda i: (i, 0))],
        out_shape=[SDS((s, IN_W), bf16)],
        operands=[h, g1, w_in], sem=("parallel",), vmem=VMEM_LIMIT)[0]


ATT_ROWS = N_HEADS * BLOCK
ATT_BLOCKS_PER_STEP = 8


def _attn_bias():
    qi = jnp.arange(BLOCK)[None, :]
    key = jnp.arange(2 * BLOCK)[:, None]
    band = (key > qi) & (key <= qi + BLOCK)
    first = band & (key >= BLOCK)
    mask = jnp.where(jnp.stack([first, band]), 0.0, MASK_VALUE).astype(f32)
    return jnp.tile(mask, (1, 1, N_HEADS // 2))


def _attn_band(kvc, kvp):
    kb = jnp.concatenate([kvp[:, :KV_W], kvc[:, :KV_W]], axis=0)
    vb = jnp.concatenate([kvp[:, KV_W:], kvc[:, KV_W:]], axis=0)
    lane = lax.broadcasted_iota(jnp.int32, kb.shape, 1)
    kb_sw = pltpu.roll(kb, HEAD_DIM, 1)
    vb_sw = pltpu.roll(vb, HEAD_DIM, 1)
    kx = [jnp.where(lane < HEAD_DIM, kb, kb_sw), jnp.where(lane >= HEAD_DIM, kb, kb_sw)]
    vx = [jnp.where(lane < HEAD_DIM, vb, vb_sw), jnp.where(lane >= HEAD_DIM, vb, vb_sw)]
    return kx, vx


def _stack_heads(x, mlo):
    zero = jnp.zeros((BLOCK, LANES), x.dtype)
    out = []
    for hk in range(2):
        parts = []
        for j in (2 * hk, 2 * hk + 1):
            xj = x[:, j * LANES:(j + 1) * LANES]
            parts += [jnp.where(mlo, xj, zero), jnp.where(mlo, zero, xj)]
        out.append(jnp.concatenate(parts, axis=0))
    return out


def _unstack_heads(y, mlo):
    cols = []
    for hk in range(2):
        for t in range(2):
            base = 2 * t * BLOCK
            cols.append(jnp.where(mlo, y[hk][base:base + BLOCK], y[hk][base + BLOCK:base + 2 * BLOCK]))
    return jnp.concatenate(cols, axis=1)


def _attn_probs(q4, kx, bias_t, sink_row):
    out = []
    half = ATT_ROWS // 2
    for hk in range(2):
        s = _dot_nt(kx[hk], q4[hk]) + bias_t
        sink = sink_row[:, hk * half:(hk + 1) * half]
        m = jnp.maximum(jnp.max(s, axis=0, keepdims=True), sink)
        p = jnp.exp(s - m)
        e_sink = jnp.exp(sink - m)
        inv = 1.0 / (jnp.sum(p, axis=0, keepdims=True) + e_sink)
        out.append((p * inv, e_sink * inv))
    return out


def attn_fwd(z, sink_col, bias, g_a, pieces=None):
    s = z.shape[0]
    per = min(ATT_BLOCKS_PER_STEP, s // BLOCK)
    tq = per * BLOCK

    def body(q_ref, kv_ref, kvp_ref, sk_ref, b_ref, g_ref, o_ref, y_ref):
        n = pl.program_id(0)
        mlo = lax.broadcasted_iota(jnp.int32, (BLOCK, LANES), 1) < HEAD_DIM
        for b in range(per):
            rows = slice(b * BLOCK, (b + 1) * BLOCK)
            kvp = kvp_ref[...] if b == 0 else kv_ref[(b - 1) * BLOCK:b * BLOCK, :]
            bias_b = b_ref[jnp.minimum(n, 1)] if b == 0 else b_ref[1]
            kx, vx = _attn_band(kv_ref[rows, :], kvp)
            q4 = _stack_heads(q_ref[rows, :] * (HEAD_DIM ** -0.5), mlo)
            probs = _attn_probs(q4, kx, bias_b, sk_ref[...])
            o = _unstack_heads([_dot_tn(probs[hk][0].astype(bf16), vx[hk]) for hk in range(2)], mlo)
            o_ref[rows, :] = o.astype(bf16)
            y, _ = _rms_fwd(o, g_ref[...])
            y_ref[rows, :] = y.astype(bf16)

    return _pcall(
        body, pieces, name="attn_fwd", grid=(s // tq,),
        in_specs=[pl.BlockSpec((tq, ATTN_W), lambda n: (n, 0)),
                  pl.BlockSpec((tq, 2 * KV_W), lambda n: (n, 2)),
                  pl.BlockSpec((BLOCK, 2 * KV_W), lambda n: (jnp.maximum(n * per - 1, 0), 2)),
                  _full((1, ATT_ROWS)), _full((2, 2 * BLOCK, ATT_ROWS // 2)), _full((1, ATTN_W))],
        out_specs=[pl.BlockSpec((tq, ATTN_W), lambda n: (n, 0)), pl.BlockSpec((tq, ATTN_W), lambda n: (n, 0))],
        out_shape=[SDS((s, ATTN_W), bf16), SDS((s, ATTN_W), bf16)],
        operands=[z, z, z, sink_col, bias, g_a], sem=("parallel",))


def _lru_gates(xc, wa, ba, wx, bx, lam):
    xcb = xc.astype(bf16)
    r = _sigmoid(_dot(xcb, wa) + ba)
    ig = _sigmoid(_dot(xcb, wx) + bx)
    sp = _softplus(-lam)
    la = (-LRU_C * r) * sp
    a = jnp.exp(la)
    mult = jnp.sqrt(_neg_expm1(2.0 * la))
    return r, ig, sp, la, a, mult


def branch_fwd(z, p, pieces=None):
    s = z.shape[0]
    tm = _tile(s)
    hb = tm // HALO

    def body(cv_ref, cg_ref, rx_ref, rg_ref, cvh_ref, cgh_ref, rxh_ref,
             cw_ref, cb_ref, lng_ref, lnb_ref, lw_ref, lb_ref, wa_ref, ba_ref, wx_ref, bx_ref, lam_ref, gc_ref, gl_ref,
             conv_ref, hst_ref, nc_ref, nl_ref, carry_ref):
        i = pl.program_id(0)
        first = i == 0

        @pl.when(first)
        def _():
            carry_ref[...] = jnp.zeros_like(carry_ref)

        cval = cv_ref[...].astype(f32)
        u = cval * _sigmoid(cg_ref[...].astype(f32))
        hu = jnp.where(first, 0.0, cvh_ref[...].astype(f32) * _sigmoid(cgh_ref[...].astype(f32)))
        conv = _conv_taps(jnp.concatenate([hu, u], axis=0), cw_ref[...], CONV_K) + cb_ref[...]
        conv_ref[...] = conv
        mu = jnp.mean(conv, axis=-1, keepdims=True)
        xm = conv - mu
        rstd = lax.rsqrt(jnp.mean(xm * xm, axis=-1, keepdims=True) + LN_EPS)
        ln = xm * rstd * lng_ref[...] + lnb_ref[...]
        yc = ln * _sigmoid(ln)
        nc, _ = _rms_fwd(yc, gc_ref[...])
        nc_ref[...] = nc.astype(bf16)

        rx = rx_ref[...].astype(f32)
        hrx = jnp.where(first, 0.0, rxh_ref[...].astype(f32))
        xc = _conv_taps(jnp.concatenate([hrx, rx], axis=0), lw_ref[...], LRU_K) + lb_ref[...]
        r, ig, sp, la, a, mult = _lru_gates(xc, wa_ref[...], ba_ref[...], wx_ref[...], bx_ref[...], lam_ref[...])
        gx = mult * (ig * xc)
        hs = _scan_fwd(a, gx, carry_ref[0:1, :])
        carry_ref[...] = jnp.broadcast_to(hs[tm - 1:tm, :], carry_ref.shape)
        hst_ref[...] = hs
        gl, _ = _gelu(rg_ref[...].astype(f32))
        nl, _ = _rms_fwd(hs * gl, gl_ref[...])
        nl_ref[...] = nl.astype(bf16)

    def col(c):
        return pl.BlockSpec((tm, CONV_W), lambda i: (i, c))

    def halo(c):
        return pl.BlockSpec((HALO, CONV_W), lambda i: (jnp.maximum(i * hb - 1, 0), c))

    small = [p["cw"], p["cb"], p["lng"], p["lnb"], p["lw"], p["lb"], p["wa"], p["ba"], p["wx"], p["bx"], p["lam"],
             p["gc"], p["gl"]]
    row = pl.BlockSpec((tm, CONV_W), lambda i: (i, 0))
    return _pcall(
        body, pieces, name="branch_fwd", grid=(s // tm,),
        in_specs=[col(3), col(4), col(5), col(6), halo(3), halo(4), halo(5)] + [_full(a.shape) for a in small],
        out_specs=[row, row, row, row],
        out_shape=[SDS((s, CONV_W), f32), SDS((s, LRU_W), f32), SDS((s, CONV_W), bf16), SDS((s, LRU_W), bf16)],
        scratch_shapes=[pltpu.VMEM((8, LRU_W), f32)],
        operands=[z, z, z, z, z, z, z, *small], sem=("arbitrary",))


def outproj_fwd(ya, yc, yl, h, w_out, g2, pieces=None):
    s = h.shape[0]
    tm = _tile(s)

    def body(ya_ref, yc_ref, yl_ref, h_ref, w_ref, g_ref, h1_ref, hn_ref):
        y = jnp.concatenate([ya_ref[...], yc_ref[...], yl_ref[...]], axis=1)
        h1 = h_ref[...] + _dot(y, w_ref[...])
        h1_ref[...] = h1
        hn, _ = _rms_fwd(h1, g_ref[...])
        hn_ref[...] = hn.astype(bf16)

    return _pcall(
        body, pieces, name="outproj_fwd", grid=(s // tm,),
        in_specs=[pl.BlockSpec((tm, ATTN_W), lambda i: (i, 0)), pl.BlockSpec((tm, CONV_W), lambda i: (i, 0)),
                  pl.BlockSpec((tm, LRU_W), lambda i: (i, 0)), pl.BlockSpec((tm, D_MODEL), lambda i: (i, 0)),
                  _full((D_MODEL, D_MODEL)), _full((1, D_MODEL))],
        out_specs=[pl.BlockSpec((tm, D_MODEL), lambda i: (i, 0)), pl.BlockSpec((tm, D_MODEL), lambda i: (i, 0))],
        out_shape=[SDS((s, D_MODEL), f32), SDS((s, D_MODEL), bf16)],
        operands=[ya, yc, yl, h, w_out, g2], sem=("parallel",), vmem=VMEM_LIMIT)


def mlp_fwd(hn2, h1, w_up, w_dn, pieces=None):
    s = h1.shape[0]
    tm = _tile(s)

    def body(x_ref, h_ref, wu_ref, wd_ref, up_ref, h2_ref):
        h2_ref[...] = _mlp_chunks(x_ref[...], h_ref[...], wu_ref, wd_ref, up_ref)

    return _pcall(
        body, pieces, name="mlp_fwd", grid=(s // tm,),
        in_specs=[pl.BlockSpec((tm, D_MODEL), lambda i: (i, 0)), pl.BlockSpec((tm, D_MODEL), lambda i: (i, 0)),
                  _full((N_SHARD, D_MODEL, FF_CHUNK), single=True), _full((N_SHARD, FF_CHUNK, D_MODEL), single=True)],
        out_specs=[pl.BlockSpec((tm, D_FF), lambda i: (i, 0)), pl.BlockSpec((tm, D_MODEL), lambda i: (i, 0))],
        out_shape=[SDS((s, D_FF), bf16), SDS((s, D_MODEL), f32)],
        operands=[hn2, h1, w_up, w_dn], sem=("parallel",), vmem=VMEM_LIMIT)


def _mlp_chunks(x, acc, wu_ref, wd_ref, up_ref):
    for c in range(N_SHARD):
        u = _dot(x, wu_ref[c])
        up_ref[:, c * FF_CHUNK:(c + 1) * FF_CHUNK] = u.astype(bf16)
        act = jnp.square(jnp.maximum(u, 0.0)).astype(bf16)
        acc = acc + _dot(act, wd_ref[c])
    return acc


def _final_tile(x, tgt, g, dh_ref, loss_ref, dg_ref):
    y, r = _rms_fwd(x, g)
    err = y - tgt
    part = 0.5 * jnp.sum(jnp.mean(err * err, axis=-1, keepdims=True), axis=0, keepdims=True)
    loss_ref[...] += jnp.broadcast_to(part, loss_ref.shape)
    dx, dg = _rms_bwd(err * (1.0 / D_MODEL), x, r, g)
    dh_ref[...] = dx
    dg_ref[...] += dg


def last_layer_fwd(ya, yc, yl, h, w_out, g2, w_up, w_dn, tgt, gf):
    s = h.shape[0]
    tm = min(256, s)

    def body(ya_ref, yc_ref, yl_ref, h_ref, wo_ref, g2_ref, wu_ref, wd_ref, t_ref, gf_ref,
             h1_ref, hn_ref, up_ref, dh_ref, loss_ref, dg_ref):
        i = pl.program_id(0)

        @pl.when(i == 0)
        def _():
            loss_ref[...] = jnp.zeros_like(loss_ref)
            dg_ref[...] = jnp.zeros_like(dg_ref)

        y = jnp.concatenate([ya_ref[...], yc_ref[...], yl_ref[...]], axis=1)
        h1 = h_ref[...] + _dot(y, wo_ref[...])
        h1_ref[...] = h1
        hn, _ = _rms_fwd(h1, g2_ref[...])
        hn = hn.astype(bf16)
        hn_ref[...] = hn
        h2 = _mlp_chunks(hn, h1, wu_ref, wd_ref, up_ref)
        _final_tile(h2, t_ref[...], gf_ref[...], dh_ref, loss_ref, dg_ref)

    def rowb(w):
        return pl.BlockSpec((tm, w), lambda i: (i, 0))

    return pl.pallas_call(
        body, name="last_layer_fwd", grid=(s // tm,),
        in_specs=[rowb(ATTN_W), rowb(CONV_W), rowb(LRU_W), rowb(D_MODEL), _full((D_MODEL, D_MODEL), single=True),
                  _full((1, D_MODEL)), _full((N_SHARD, D_MODEL, FF_CHUNK), single=True),
                  _full((N_SHARD, FF_CHUNK, D_MODEL), single=True), rowb(D_MODEL), _full((1, D_MODEL))],
        out_specs=[rowb(D_MODEL), rowb(D_MODEL), rowb(D_FF), rowb(D_MODEL), _full((8, LANES)), _full((1, D_MODEL))],
        out_shape=[SDS((s, D_MODEL), f32), SDS((s, D_MODEL), bf16), SDS((s, D_FF), bf16), SDS((s, D_MODEL), f32),
                   SDS((8, LANES), f32), SDS((1, D_MODEL), f32)],
        compiler_params=_params(("arbitrary",), VMEM_LIMIT),
    )(ya, yc, yl, h, w_out, g2, w_up, w_dn, tgt, gf)


def mlp_bwd_act(dh, up, h1, g2, w_up, w_dn, pieces=None):
    s = dh.shape[0]
    tm = _tile(s)

    def body(dh_ref, up_ref, h1_ref, g_ref, wu_ref, wd_ref, dup_ref, dh1_ref, dg_ref):
        i = pl.program_id(0)

        @pl.when(i == 0)
        def _():
            dg_ref[...] = jnp.zeros_like(dg_ref)

        dh = dh_ref[...]
        dhb = dh.astype(bf16)
        d_hn = jnp.zeros((tm, D_MODEL), f32)
        for c in range(N_SHARD):
            d_act = _dot_nt(dhb, wd_ref[c])
            u = up_ref[:, c * FF_CHUNK:(c + 1) * FF_CHUNK].astype(f32)
            d_u = (d_act * (2.0 * jnp.maximum(u, 0.0))).astype(bf16)
            dup_ref[:, c * FF_CHUNK:(c + 1) * FF_CHUNK] = d_u
            d_hn = d_hn + _dot_nt(d_u, wu_ref[c])
        x = h1_ref[...]
        g = g_ref[...]
        _, r = _rms_fwd(x, g)
        dx, dg = _rms_bwd(d_hn, x, r, g)
        dh1_ref[...] = dh + dx
        dg_ref[...] += dg

    return _pcall(
        body, pieces, name="mlp_bwd_act", grid=(s // tm,),
        in_specs=[pl.BlockSpec((tm, D_MODEL), lambda i: (i, 0)), pl.BlockSpec((tm, D_FF), lambda i: (i, 0)),
                  pl.BlockSpec((tm, D_MODEL), lambda i: (i, 0)), _full((1, D_MODEL)),
                  _full((N_SHARD, D_MODEL, FF_CHUNK), single=True), _full((N_SHARD, FF_CHUNK, D_MODEL), single=True)],
        out_specs=[pl.BlockSpec((tm, D_FF), lambda i: (i, 0)), pl.BlockSpec((tm, D_MODEL), lambda i: (i, 0)),
                   _full((1, D_MODEL))],
        out_shape=[SDS((s, D_FF), bf16), SDS((s, D_MODEL), f32), SDS((1, D_MODEL), f32)],
        operands=[dh, up, h1, g2, w_up, w_dn], sem=("arbitrary",), vmem=VMEM_LIMIT)


def mlp_bwd_w(hn2, d_up, up, dh, pieces=None):
    s = dh.shape[0]
    tk = min(1024, s)

    def body(x_ref, du_ref, up_ref, dh_ref, dwu_ref, dwd_ref):
        k = pl.program_id(1)

        @pl.when(k == 0)
        def _():
            dwu_ref[...] = jnp.zeros_like(dwu_ref)
            dwd_ref[...] = jnp.zeros_like(dwd_ref)

        dwu_ref[0] += _dot_tn(x_ref[...], du_ref[...])
        act = jnp.square(jnp.maximum(up_ref[...].astype(f32), 0.0)).astype(bf16)
        dwd_ref[0] += _dot_tn(act, dh_ref[...].astype(bf16))

    return _pcall(
        body, pieces, name="mlp_bwd_w", grid=(N_SHARD, s // tk),
        in_specs=[pl.BlockSpec((tk, D_MODEL), lambda c, k: (k, 0)), pl.BlockSpec((tk, FF_CHUNK), lambda c, k: (k, c)),
                  pl.BlockSpec((tk, FF_CHUNK), lambda c, k: (k, c)), pl.BlockSpec((tk, D_MODEL), lambda c, k: (k, 0))],
        out_specs=[pl.BlockSpec((1, D_MODEL, FF_CHUNK), lambda c, k: (c, 0, 0)),
                   pl.BlockSpec((1, FF_CHUNK, D_MODEL), lambda c, k: (c, 0, 0))],
        out_shape=[SDS((N_SHARD, D_MODEL, FF_CHUNK), f32), SDS((N_SHARD, FF_CHUNK, D_MODEL), f32)],
        operands=[hn2, d_up, up, dh], sem=("parallel", "arbitrary"), vmem=VMEM_LIMIT)


def outproj_bwd(dh1, ya, yc, yl, w_out):
    s = dh1.shape[0]
    tm = _tile(s)

    def body(dh_ref, ya_ref, yc_ref, yl_ref, w_ref, dy_ref, dw_ref):
        i = pl.program_id(0)

        @pl.when(i == 0)
        def _():
            dw_ref[...] = jnp.zeros_like(dw_ref)

        dhb = dh_ref[...].astype(bf16)
        dy_ref[...] = _dot_nt(dhb, w_ref[...])
        y = jnp.concatenate([ya_ref[...], yc_ref[...], yl_ref[...]], axis=1)
        dw_ref[...] += _dot_tn(y, dhb)

    return pl.pallas_call(
        body, name="outproj_bwd", grid=(s // tm,),
        in_specs=[pl.BlockSpec((tm, D_MODEL), lambda i: (i, 0)), pl.BlockSpec((tm, ATTN_W), lambda i: (i, 0)),
                  pl.BlockSpec((tm, CONV_W), lambda i: (i, 0)), pl.BlockSpec((tm, LRU_W), lambda i: (i, 0)),
                  _full((D_MODEL, D_MODEL))],
        out_specs=[pl.BlockSpec((tm, D_MODEL), lambda i: (i, 0)), _full((D_MODEL, D_MODEL))],
        out_shape=[SDS((s, D_MODEL), f32), SDS((D_MODEL, D_MODEL), f32)],
        compiler_params=_params(("arbitrary",), VMEM_LIMIT),
    )(dh1, ya, yc, yl, w_out)


def attn_bwd(z, o, dy, sink_col, bias, g_a, pieces=None):
    s = z.shape[0]
    per = min(ATT_BLOCKS_PER_STEP, s // BLOCK)
    tq = per * BLOCK
    nt = s // tq

    def body(q_ref, kv_ref, kvp_ref, o_ref, dy_ref, sk_ref, b_ref, g_ref, dq_ref, dkv_ref, dsk_ref, dg_ref,
             carry_ref, dsk_acc):
        i = pl.program_id(0)
        t = nt - 1 - i

        @pl.when(i == 0)
        def _():
            carry_ref[...] = jnp.zeros_like(carry_ref)
            dsk_acc[...] = jnp.zeros_like(dsk_acc)
            dg_ref[...] = jnp.zeros_like(dg_ref)

        mlo = lax.broadcasted_iota(jnp.int32, (BLOCK, LANES), 1) < HEAD_DIM
        lane = lax.broadcasted_iota(jnp.int32, (2 * BLOCK, LANES), 1)
        scale = HEAD_DIM ** -0.5
        half = ATT_ROWS // 2
        g = g_ref[...]
        bands = []
        for b in range(per):
            rows = slice(b * BLOCK, (b + 1) * BLOCK)
            kvp = kvp_ref[...] if b == 0 else kv_ref[(b - 1) * BLOCK:b * BLOCK, :]
            bias_b = b_ref[jnp.minimum(t, 1)] if b == 0 else b_ref[1]
            kx, vx = _attn_band(kv_ref[rows, :], kvp)
            q4 = _stack_heads(q_ref[rows, :] * scale, mlo)
            o_f = o_ref[rows, :].astype(f32)
            _, r = _rms_fwd(o_f, g)
            d_o, dg = _rms_bwd(dy_ref[rows, :], o_f, r, g)
            dg_ref[...] += dg
            do4 = _stack_heads(d_o.astype(bf16), mlo)
            probs = _attn_probs(q4, kx, bias_b, sk_ref[...])
            dq4, tk, tv = [], [], []
            for hk in range(2):
                pr, p_sink = probs[hk]
                d_p = _dot_nt(vx[hk], do4[hk])
                d_row = jnp.sum(pr * d_p, axis=0, keepdims=True)
                d_s = (pr * (d_p - d_row)).astype(bf16)
                dsk_acc[:, hk * half:(hk + 1) * half] -= p_sink * d_row
                dq4.append(_dot_tn(d_s, kx[hk]))
                tk.append(_dot(d_s, q4[hk]))
                tv.append(_dot(pr.astype(bf16), do4[hk]))
            dq_ref[rows, :] = (_unstack_heads(dq4, mlo) * scale).astype(bf16)
            fk = [x + pltpu.roll(x, HEAD_DIM, 1) for x in tk]
            fv = [x + pltpu.roll(x, HEAD_DIM, 1) for x in tv]
            bands.append(jnp.concatenate([jnp.where(lane < HEAD_DIM, fk[0], fk[1]),
                                          jnp.where(lane < HEAD_DIM, fv[0], fv[1])], axis=1))
        for b in range(per):
            after = bands[b + 1][:BLOCK] if b + 1 < per else carry_ref[...]
            dkv_ref[b * BLOCK:(b + 1) * BLOCK, :] = (bands[b][BLOCK:] + after).astype(bf16)
        carry_ref[...] = bands[0][:BLOCK]

        @pl.when(i == nt - 1)
        def _():
            for hh in range(N_HEADS):
                tot = jnp.sum(dsk_acc[:, hh * BLOCK:(hh + 1) * BLOCK], axis=1, keepdims=True)
                dsk_ref[hh:hh + 1, :] = jnp.broadcast_to(tot, (1, LANES))

    def rev(width, col):
        return pl.BlockSpec((tq, width), lambda i: (nt - 1 - i, col))

    return _pcall(
        body, pieces, name="attn_bwd", grid=(nt,),
        in_specs=[rev(ATTN_W, 0), rev(2 * KV_W, 2),
                  pl.BlockSpec((BLOCK, 2 * KV_W), lambda i: (jnp.maximum((nt - 1 - i) * per - 1, 0), 2)),
                  rev(ATTN_W, 0), rev(ATTN_W, 0),
                  _full((1, ATT_ROWS)), _full((2, 2 * BLOCK, ATT_ROWS // 2)), _full((1, ATTN_W))],
        out_specs=[rev(ATTN_W, 0), rev(2 * KV_W, 0), _full((N_HEADS, LANES)), _full((1, ATTN_W))],
        out_shape=[SDS((s, ATTN_W), bf16), SDS((s, 2 * KV_W), bf16), SDS((N_HEADS, LANES), f32), SDS((1, ATTN_W), f32)],
        scratch_shapes=[pltpu.VMEM((BLOCK, 2 * KV_W), f32), pltpu.VMEM((1, ATT_ROWS), f32)],
        operands=[z, z, z, o, dy, sink_col, bias, g_a], sem=("arbitrary",))


_V_GC, _V_LNG, _V_LNB, _V_CB, _V_GL, _V_BA, _V_BX, _V_LAM, _V_LB = range(9)
_V_ROWS = 16


def branch_bwd_a(z, conv, hst, dy, p, pieces=None):
    s = z.shape[0]
    tm = _tile(s)
    nt = s // tm
    hb = tm // HALO
    h8 = tm // 8

    def body(conv_ref, dyc_ref, dyl_ref, rx_ref, rxh_ref, rg_ref, hst_ref, hsth_ref,
             lng_ref, lnb_ref, lw_ref, lb_ref, wa_ref, ba_ref, wx_ref, bx_ref, lam_ref, gc_ref, gl_ref,
             dconv_ref, dxc_ref, drg_ref, vec_ref, dwa_ref, dwx_ref, carry_ref):
        i = pl.program_id(0)
        ti = nt - 1 - i

        @pl.when(i == 0)
        def _():
            carry_ref[...] = jnp.zeros_like(carry_ref)
            vec_ref[...] = jnp.zeros_like(vec_ref)
            dwa_ref[...] = jnp.zeros_like(dwa_ref)
            dwx_ref[...] = jnp.zeros_like(dwx_ref)

        conv = conv_ref[...]
        mu = jnp.mean(conv, axis=-1, keepdims=True)
        xm = conv - mu
        rstd = lax.rsqrt(jnp.mean(xm * xm, axis=-1, keepdims=True) + LN_EPS)
        xhat = xm * rstd
        lng = lng_ref[...]
        ln = xhat * lng + lnb_ref[...]
        sg = _sigmoid(ln)
        yc = ln * sg
        gc = gc_ref[...]
        _, rc = _rms_fwd(yc, gc)
        d_yc, d_gc = _rms_bwd(dyc_ref[...], yc, rc, gc)
        d_ln = d_yc * (sg * (1.0 + ln * (1.0 - sg)))
        d_xhat = d_ln * lng
        d_conv = rstd * (d_xhat - jnp.mean(d_xhat, axis=-1, keepdims=True)
                         - xhat * jnp.mean(d_xhat * xhat, axis=-1, keepdims=True))
        dconv_ref[...] = d_conv

        rx = rx_ref[...].astype(f32)
        hrx = jnp.where(ti == 0, 0.0, rxh_ref[...].astype(f32))
        xc = _conv_taps(jnp.concatenate([hrx, rx], axis=0), lw_ref[...], LRU_K) + lb_ref[...]
        wa = wa_ref[...]
        wx = wx_ref[...]
        lam = lam_ref[...]
        r, ig, sp, la, a, mult = _lru_gates(xc, wa, ba_ref[...], wx, bx_ref[...], lam)
        hs = hst_ref[...]
        row = lax.broadcasted_iota(jnp.int32, hs.shape, 0)
        h_before = jnp.where(ti == 0, 0.0, hsth_ref[7:8, :])
        h_prev = jnp.where(row == 0, h_before, pltpu.roll(hs, 1, 0))
        rg = rg_ref[...].astype(f32)
        gl, tg = _gelu(rg)
        out = hs * gl
        gmix = gl_ref[...]
        _, rl = _rms_fwd(out, gmix)
        d_out, d_gl = _rms_bwd(dyl_ref[...], out, rl, gmix)
        drg_ref[...] = (d_out * hs * _gelu_grad(rg, tg)).astype(bf16)
        d_h = d_out * gl
        last = row == tm - 1
        a_next = jnp.where(last, 1.0, pltpu.roll(a, tm - 1, 0))
        lmb = _scan_bwd(a_next, d_h, carry_ref[0:1, :])
        carry_ref[...] = jnp.broadcast_to(a[0:1, :] * lmb[0:1, :], carry_ref.shape)
        d_a = lmb * h_prev
        d_mult = lmb * (ig * xc)
        d_ig = lmb * (mult * xc)
        d_la = d_a * a - d_mult * (a * a) / jnp.maximum(mult, 1e-30)
        d_pa = (d_la * (-LRU_C * sp)) * (r * (1.0 - r))
        d_px = d_ig * (ig * (1.0 - ig))
        d_pab = d_pa.astype(bf16)
        d_pxb = d_px.astype(bf16)
        d_xc = lmb * (mult * ig) + _dot_nt(d_pab, wa) + _dot_nt(d_pxb, wx)
        dxc_ref[...] = d_xc
        xcb = xc.astype(bf16)
        dwa_ref[...] += _dot_tn(xcb, d_pab)
        dwx_ref[...] += _dot_tn(xcb, d_pxb)
        d_lam = jnp.sum(d_la * (-LRU_C * r), axis=0, keepdims=True) * (-_sigmoid(-lam))

        def colsum(v):
            return jnp.sum(v, axis=0, keepdims=True)

        rows = [None] * _V_ROWS
        rows[_V_GC] = d_gc
        rows[_V_LNG] = colsum(d_ln * xhat)
        rows[_V_LNB] = colsum(d_ln)
        rows[_V_CB] = colsum(d_conv)
        rows[_V_GL] = d_gl
        rows[_V_BA] = colsum(d_pa)
        rows[_V_BX] = colsum(d_px)
        rows[_V_LAM] = d_lam
        rows[_V_LB] = colsum(d_xc)
        zero = jnp.zeros((1, CONV_W), f32)
        vec_ref[...] += jnp.concatenate([zero if v is None else v for v in rows], axis=0)

    def rev(c):
        return pl.BlockSpec((tm, CONV_W), lambda i: (nt - 1 - i, c))

    small = [p["lng"], p["lnb"], p["lw"], p["lb"], p["wa"], p["ba"], p["wx"], p["bx"], p["lam"], p["gc"], p["gl"]]
    return _pcall(
        body, pieces, name="branch_bwd_a", grid=(nt,),
        in_specs=[rev(0), rev(2), rev(3), rev(5),
                  pl.BlockSpec((HALO, CONV_W), lambda i: (jnp.maximum((nt - 1 - i) * hb - 1, 0), 5)),
                  rev(6), rev(0),
                  pl.BlockSpec((8, LRU_W), lambda i: (jnp.maximum((nt - 1 - i) * h8 - 1, 0), 0))]
                 + [_full(a.shape) for a in small],
        out_specs=[rev(0), rev(0), rev(0), _full((_V_ROWS, CONV_W)), _full((LRU_W, LRU_W)), _full((LRU_W, LRU_W))],
        out_shape=[SDS((s, CONV_W), f32), SDS((s, LRU_W), f32), SDS((s, LRU_W), bf16), SDS((_V_ROWS, CONV_W), f32),
                   SDS((LRU_W, LRU_W), f32), SDS((LRU_W, LRU_W), f32)],
        scratch_shapes=[pltpu.VMEM((8, LRU_W), f32)],
        operands=[conv, dy, dy, z, z, z, hst, hst, *small], sem=("arbitrary",))


def branch_bwd_b(z, d_conv, d_xc, p, pieces=None):
    s = z.shape[0]
    tm = _tile(s)
    nt = s // tm
    hb = tm // HALO

    def body(cv_ref, cg_ref, cvh_ref, cgh_ref, rx_ref, rxh_ref, dc_ref, dch_ref, dx_ref, dxh_ref, cw_ref, lw_ref,
             dzc_ref, dzr_ref, dcw_ref, dlw_ref):
        i = pl.program_id(0)

        @pl.when(i == 0)
        def _():
            dcw_ref[...] = jnp.zeros_like(dcw_ref)
            dlw_ref[...] = jnp.zeros_like(dlw_ref)

        first = i == 0
        last = i == nt - 1
        cval = cv_ref[...].astype(f32)
        sg = _sigmoid(cg_ref[...].astype(f32))
        u = cval * sg
        hu = jnp.where(first, 0.0, cvh_ref[...].astype(f32) * _sigmoid(cgh_ref[...].astype(f32)))
        dpad = jnp.concatenate([dc_ref[...], jnp.where(last, 0.0, dch_ref[...])], axis=0)
        d_u, dw_rows = _conv_taps_bwd(dpad, jnp.concatenate([hu, u], axis=0), cw_ref[...], CONV_K, tm)
        dcw_ref[...] += jnp.concatenate(dw_rows + [jnp.zeros((HALO - CONV_K, CONV_W), f32)], axis=0)
        dzc_ref[...] = jnp.concatenate([d_u * sg, d_u * cval * sg * (1.0 - sg)], axis=1).astype(bf16)

        rx = rx_ref[...].astype(f32)
        hrx = jnp.where(first, 0.0, rxh_ref[...].astype(f32))
        dxpad = jnp.concatenate([dx_ref[...], jnp.where(last, 0.0, dxh_ref[...])], axis=0)
        d_rx, dlw_rows = _conv_taps_bwd(dxpad, jnp.concatenate([hrx, rx], axis=0), lw_ref[...], LRU_K, tm)
        dlw_ref[...] += jnp.concatenate(dlw_rows + [jnp.zeros((8 - LRU_K, LRU_W), f32)], axis=0)
        dzr_ref[...] = d_rx.astype(bf16)

    def col(c):
        return pl.BlockSpec((tm, CONV_W), lambda i: (i, c))

    def prev(c):
        return pl.BlockSpec((HALO, CONV_W), lambda i: (jnp.maximum(i * hb - 1, 0), c))

    nxt = pl.BlockSpec((HALO, CONV_W), lambda i: (jnp.minimum((i + 1) * hb, nt * hb - 1), 0))
    return _pcall(
        body, pieces, name="branch_bwd_b", grid=(nt,),
        in_specs=[col(3), col(4), prev(3), prev(4), col(5), prev(5), col(0), nxt, col(0), nxt,
                  _full(p["cw"].shape), _full(p["lw"].shape)],
        out_specs=[pl.BlockSpec((tm, 2 * CONV_W), lambda i: (i, 0)), pl.BlockSpec((tm, LRU_W), lambda i: (i, 0)),
                   _full((HALO, CONV_W)), _full((8, LRU_W))],
        out_shape=[SDS((s, 2 * CONV_W), bf16), SDS((s, LRU_W), bf16), SDS((HALO, CONV_W), f32), SDS((8, LRU_W), f32)],
        operands=[z, z, z, z, z, z, d_conv, d_conv, d_xc, d_xc, p["cw"], p["lw"]], sem=("arbitrary",))


def inproj_bwd(dq, dkv, dzc, dzr, drg, h, g1, w_in, dh1, pieces=None):
    s = h.shape[0]
    tm = _tile(s)

    def body(dq_ref, dkv_ref, dzc_ref, dzr_ref, drg_ref, h_ref, g_ref, w_ref, dh1_ref, dh_ref, dw_ref, dg_ref):
        i = pl.program_id(0)

        @pl.when(i == 0)
        def _():
            dw_ref[...] = jnp.zeros_like(dw_ref)
            dg_ref[...] = jnp.zeros_like(dg_ref)

        dz = jnp.concatenate([dq_ref[...], dkv_ref[...], dzc_ref[...], dzr_ref[...], drg_ref[...]], axis=1)
        x = h_ref[...]
        g = g_ref[...]
        hn, r = _rms_fwd(x, g)
        d_hn = _dot_nt(dz, w_ref[...])
        dw_ref[...] += _dot_tn(hn.astype(bf16), dz)
        dx, dg = _rms_bwd(d_hn, x, r, g)
        dh_ref[...] = dh1_ref[...] + dx
        dg_ref[...] += dg

    def rowb(w):
        return pl.BlockSpec((tm, w), lambda i: (i, 0))

    return _pcall(
        body, pieces, name="inproj_bwd", grid=(s // tm,),
        in_specs=[rowb(ATTN_W), rowb(2 * KV_W), rowb(2 * CONV_W), rowb(LRU_W), rowb(LRU_W), rowb(D_MODEL),
                  _full((1, D_MODEL)), _full((D_MODEL, IN_W)), rowb(D_MODEL)],
        out_specs=[rowb(D_MODEL), _full((D_MODEL, IN_W)), _full((1, D_MODEL))],
        out_shape=[SDS((s, D_MODEL), f32), SDS((D_MODEL, IN_W), f32), SDS((1, D_MODEL), f32)],
        operands=[dq, dkv, dzc, dzr, drg, h, g1, w_in, dh1], sem=("arbitrary",), vmem=VMEM_LIMIT)


def _block_diag(w):
    out = jnp.zeros((LRU_W, LRU_W), w.dtype)
    hd = LRU_W // LRU_HEADS
    for hh in range(LRU_HEADS):
        out = out.at[hh * hd:(hh + 1) * hd, hh * hd:(hh + 1) * hd].set(w[hh])
    return out


def _diag_blocks(w):
    hd = LRU_W // LRU_HEADS
    return jnp.stack([w[hh * hd:(hh + 1) * hd, hh * hd:(hh + 1) * hd] for hh in range(LRU_HEADS)])


def _layer_params(sp, l):
    mix = sp["mix_norm"][l]
    return dict(
        g1=sp["norm1"][l][None, :], g2=sp["norm2"][l][None, :],
        sinks=jnp.repeat(sp["attn_sinks"][l], BLOCK)[None, :],
        ga=mix[None, :ATTN_W], gc=mix[None, ATTN_W:ATTN_W + CONV_W], gl=mix[None, ATTN_W + CONV_W:],
        cw=jnp.pad(sp["conv_dw_w"][l], ((0, HALO - CONV_K), (0, 0))), cb=sp["conv_dw_b"][l][None, :],
        lng=sp["conv_ln_g"][l][None, :], lnb=sp["conv_ln_b"][l][None, :],
        lw=jnp.pad(sp["lru_conv_w"][l], ((0, 8 - LRU_K), (0, 0))), lb=sp["lru_conv_b"][l][None, :],
        wa=_block_diag(sp["lru_wa"][l]).astype(bf16), ba=sp["lru_ba"][l].reshape(1, LRU_W),
        wx=_block_diag(sp["lru_wx"][l]).astype(bf16), bx=sp["lru_bx"][l].reshape(1, LRU_W),
        lam=sp["lru_lambda"][l][None, :],
    )


def local_step(x, tgt, big, sp):
    return train_local(x, tgt, sp, LocalWeights(big))


class LocalWeights:
    def __init__(self, big):
        self.big = big
        self.grads = [dict() for _ in range(DEPTH)]

    def weight(self, name, l):
        return self.big[l][name]

    def host(self, point, l):
        return None

    def grad(self, name, l, g):
        self.grads[l][name] = g

    def big_grads(self):
        return self.grads


def train_local(x, tgt, sp, cs):
    lp = [_layer_params(sp, l) for l in range(DEPTH)]
    bias = _attn_bias()
    saved = []
    h = x
    for l in range(DEPTH):
        p = lp[l]
        z = inproj_fwd(h, p["g1"], cs.weight("w_in", l), cs.host("inproj_fwd", l))
        o, ya = attn_fwd(z, p["sinks"], bias, p["ga"], cs.host("attn_fwd", l))
        conv, hst, yc, yl = branch_fwd(z, p, cs.host("branch_fwd", l))
        if l < DEPTH - 1:
            h1, hn2 = outproj_fwd(ya, yc, yl, h, cs.weight("w_out", l), p["g2"], cs.host("outproj_fwd", l))
            up, h_next = mlp_fwd(hn2, h1, cs.weight("w_up", l), cs.weight("w_dn", l), cs.host("mlp_fwd", l))
        else:
            h1, hn2, up, dh, loss, d_gf = last_layer_fwd(
                ya, yc, yl, h, cs.weight("w_out", l), p["g2"], cs.weight("w_up", l), cs.weight("w_dn", l),
                tgt, sp["final_norm"][None, :])
            h_next = None
        saved.append(dict(h=h, z=z, o=o, ya=ya, conv=conv, hst=hst, yc=yc, yl=yl, h1=h1, hn2=hn2, up=up))
        h = h_next
    small_g = [None] * DEPTH
    for l in reversed(range(DEPTH)):
        p, sv = lp[l], saved[l]
        w_up, w_dn = cs.weight("w_up", l), cs.weight("w_dn", l)
        d_up, dh1, d_g2 = mlp_bwd_act(dh, sv["up"], sv["h1"], p["g2"], w_up, w_dn, cs.host("mlp_bwd_act", l))
        dw_up, dw_dn = mlp_bwd_w(sv["hn2"], d_up, sv["up"], dh, cs.host("mlp_bwd_w", l))
        cs.grad("w_up", l, dw_up)
        cs.grad("w_dn", l, dw_dn)
        dy, dw_out = outproj_bwd(dh1, sv["ya"], sv["yc"], sv["yl"], cs.weight("w_out", l))
        cs.grad("w_out", l, dw_out)
        dq, dkv, d_sk, d_ga = attn_bwd(sv["z"], sv["o"], dy, p["sinks"], bias, p["ga"], cs.host("attn_bwd", l))
        d_conv, d_xc, d_rg, vec, dwa, dwx = branch_bwd_a(sv["z"], sv["conv"], sv["hst"], dy, p, cs.host("branch_bwd_a", l))
        dzc, dzr, dcw, dlw = branch_bwd_b(sv["z"], d_conv, d_xc, p, cs.host("branch_bwd_b", l))
        dh, dw_in, d_g1 = inproj_bwd(dq, dkv, dzc, dzr, d_rg, sv["h"], p["g1"], cs.weight("w_in", l), dh1,
                                     cs.host("inproj_bwd", l))
        cs.grad("w_in", l, dw_in)
        hd = LRU_W // LRU_HEADS
        small_g[l] = dict(
            norm1=d_g1[0], attn_sinks=d_sk[:, 0], conv_dw_w=dcw[:CONV_K], conv_dw_b=vec[_V_CB],
            conv_ln_g=vec[_V_LNG], conv_ln_b=vec[_V_LNB], lru_conv_w=dlw[:LRU_K], lru_conv_b=vec[_V_LB],
            lru_wa=_diag_blocks(dwa), lru_ba=vec[_V_BA].reshape(LRU_HEADS, hd),
            lru_wx=_diag_blocks(dwx), lru_bx=vec[_V_BX].reshape(LRU_HEADS, hd), lru_lambda=vec[_V_LAM],
            mix_norm=jnp.concatenate([d_ga[0], vec[_V_GC], vec[_V_GL]]), norm2=d_g2[0],
        )
    return loss, dh, cs.big_grads(), small_g, d_gf[0]


_HBM = pl.BlockSpec(memory_space=pl.ANY)


def _place():
    x, y, c = lax.axis_index("x"), lax.axis_index("y"), lax.axis_index("c")
    chips = [(1 - x, y), (x, 1 - y), (1 - x, 1 - y)]
    return x, y, c, chips


class Comm:
    def __init__(self, ins, out_shape, aliases, sems, start, finish, done):
        self.ins, self.out_shape, self.aliases, self.sems = list(ins), list(out_shape), dict(aliases), list(sems)
        self.start, self.finish, self.done = start, finish, done


def _pcall(body, pieces, *, name, grid, in_specs, out_specs, out_shape, operands, scratch_shapes=(), sem, vmem=None):
    in_specs, out_specs, out_shape, scratch_shapes = list(in_specs), list(out_specs), list(out_shape), list(scratch_shapes)
    if not pieces:
        return pl.pallas_call(body, name=name, grid=grid, in_specs=in_specs, out_specs=out_specs, out_shape=out_shape,
                              scratch_shapes=scratch_shapes, compiler_params=_params(sem, vmem))(*operands)
    n_in, n_out, n_scr = len(in_specs), len(out_specs), len(scratch_shapes)
    c_ins = [a for p in pieces for a in p.ins]
    c_outs = [s for p in pieces for s in p.out_shape]
    c_sems = [n for p in pieces for n in p.sems]
    aliases, spans, ki, ko, ks = {}, [], 0, 0, 0
    for p in pieces:
        spans.append((ki, ko, ks))
        for a, b in p.aliases.items():
            aliases[n_in + ki + a] = n_out + ko + b
        ki, ko, ks = ki + len(p.ins), ko + len(p.out_shape), ks + len(p.sems)

    def hosted(*refs):
        ins, cin = refs[:n_in], refs[n_in:n_in + ki]
        outs, cout = refs[n_in + ki:n_in + ki + n_out], refs[n_in + ki + n_out:n_in + ki + n_out + ko]
        scr, csem = refs[n_in + ki + n_out + ko:n_in + ki + n_out + ko + n_scr], refs[n_in + ki + n_out + ko + n_scr:]
        first = functools.reduce(jnp.logical_and, [pl.program_id(d) == 0 for d in range(len(grid))])
        last = functools.reduce(jnp.logical_and, [pl.program_id(d) == grid[d] - 1 for d in range(len(grid))])

        def each(which):
            for p, (a, b, s) in zip(pieces, spans):
                getattr(p, which)(cin[a:a + len(p.ins)], cout[b:b + len(p.out_shape)], csem[s:s + len(p.sems)])

        @pl.when(first)
        def _():
            each("start")

        body(*ins, *outs, *scr)

        @pl.when(last)
        def _():
            each("finish")

    res = pl.pallas_call(
        hosted, name=name + "_host", grid=grid, in_specs=in_specs + [_HBM] * ki, out_specs=out_specs + [_HBM] * ko,
        out_shape=out_shape + c_outs, scratch_shapes=scratch_shapes + [pltpu.SemaphoreType.DMA((n,)) for n in c_sems],
        input_output_aliases=aliases, compiler_params=_params(("arbitrary",) * len(grid), vmem),
    )(*operands, *c_ins)
    for p, (a, b, s) in zip(pieces, spans):
        p.done(res[n_out + b:n_out + b + len(p.out_shape)])
    return res[:n_out]


def standalone(pieces, name):
    ki = sum(len(p.ins) for p in pieces)
    ko = sum(len(p.out_shape) for p in pieces)
    spans, a, b, s = [], 0, 0, 0
    aliases = {}
    for p in pieces:
        spans.append((a, b, s))
        for i, o in p.aliases.items():
            aliases[a + i] = b + o
        a, b, s = a + len(p.ins), b + len(p.out_shape), s + len(p.sems)

    def body(*refs):
        cin, cout, csem = refs[:ki], refs[ki:ki + ko], refs[ki + ko:]
        for which in ("start", "finish"):
            for p, (a, b, s) in zip(pieces, spans):
                getattr(p, which)(cin[a:a + len(p.ins)], cout[b:b + len(p.out_shape)], csem[s:s + len(p.sems)])

    res = pl.pallas_call(
        body, name=name, in_specs=[_HBM] * ki, out_specs=[_HBM] * ko, out_shape=[s for p in pieces for s in p.out_shape],
        scratch_shapes=[pltpu.SemaphoreType.DMA((n,)) for p in pieces for n in p.sems], input_output_aliases=aliases,
    )(*[a for p in pieces for a in p.ins])
    for p, (a, b, s) in zip(pieces, spans):
        p.done(res[b:b + len(p.out_shape)])


def _rows_half(ref, which, rows):
    return ref.at[pl.ds(pl.multiple_of(which * rows, 8), rows)]


def gather_ici_piece(bufs, done):
    n = len(bufs)

    def copies(cout):
        x, y, c, chips = _place()
        out = []
        for j, (cx, cy) in enumerate(chips):
            for w in range(n):
                half = bufs[w].shape[1] // 2
                out.append((j, w, _rows_half(cout[w].at[2 * x + y], c, half), _rows_half(cout[w].at[2 * cx + cy], c, half),
                            (cx, cy, c)))
        return out

    def start(cin, cout, sems):
        for j, w, mine, _, to in copies(cout):
            pltpu.make_async_remote_copy(src_ref=mine, dst_ref=mine, send_sem=sems[0].at[n * j + w],
                                         recv_sem=sems[1].at[n * j + w], device_id=to, device_id_type=MESH).start()

    def finish(cin, cout, sems):
        for j, w, mine, landed, to in copies(cout):
            pltpu.make_async_remote_copy(src_ref=mine, dst_ref=landed, send_sem=sems[0].at[n * j + w],
                                         recv_sem=sems[1].at[n * j + w], device_id=to, device_id_type=MESH).wait()

    return Comm(bufs, [SDS(b.shape, b.dtype) for b in bufs], {w: w for w in range(n)}, [3 * n, 3 * n], start, finish, done)


def gather_full_piece(bufs, done):
    n = len(bufs)

    def copies(cout):
        x, y, c, chips = _place()
        return [(n * j + w, cout[w].at[2 * x + y], cout[w].at[2 * cx + cy], (cx, cy, c))
                for j, (cx, cy) in enumerate(chips) for w in range(n)]

    def start(cin, cout, sems):
        for k, mine, _, to in copies(cout):
            pltpu.make_async_remote_copy(src_ref=mine, dst_ref=mine, send_sem=sems[0].at[k], recv_sem=sems[1].at[k],
                                         device_id=to, device_id_type=MESH).start()

    def finish(cin, cout, sems):
        for k, mine, landed, to in copies(cout):
            pltpu.make_async_remote_copy(src_ref=mine, dst_ref=landed, send_sem=sems[0].at[k], recv_sem=sems[1].at[k],
                                         device_id=to, device_id_type=MESH).wait()

    return Comm(bufs, [SDS(b.shape, b.dtype) for b in bufs], {w: w for w in range(n)}, [3 * n, 3 * n], start, finish, done)


def gather_d2d_piece(bufs, done):
    n = len(bufs)

    def copies(cout):
        x, y, c, chips = _place()
        out = []
        for j, (cx, cy) in enumerate(chips):
            for w in range(n):
                half = bufs[w].shape[1] // 2
                slot = cout[w].at[2 * cx + cy]
                out.append((n * j + w, _rows_half(slot, c, half), _rows_half(slot, 1 - c, half), (x, y, 1 - c)))
        return out

    def start(cin, cout, sems):
        for k, mine, _, to in copies(cout):
            pltpu.make_async_remote_copy(src_ref=mine, dst_ref=mine, send_sem=sems[0].at[k], recv_sem=sems[1].at[k],
                                         device_id=to, device_id_type=MESH).start()

    def finish(cin, cout, sems):
        for k, mine, theirs, to in copies(cout):
            pltpu.make_async_remote_copy(src_ref=mine, dst_ref=theirs, send_sem=sems[0].at[k], recv_sem=sems[1].at[k],
                                         device_id=to, device_id_type=MESH).wait()

    return Comm(bufs, [SDS(b.shape, b.dtype) for b in bufs], {w: w for w in range(n)}, [3 * n, 3 * n], start, finish, done)


def pair_piece(parts, done):
    n = len(parts)

    def copies(cin, cout):
        x, y, c, _ = _place()
        out = []
        for w in range(n):
            half = parts[w].shape[1] // 2
            out.append((w, cin[w].at[:, pl.ds(pl.multiple_of((1 - c) * half, 8), half), :], cout[w], (x, y, 1 - c)))
        return out

    def start(cin, cout, sems):
        for w, src, dst, to in copies(cin, cout):
            pltpu.make_async_remote_copy(src_ref=src, dst_ref=dst, send_sem=sems[0].at[w], recv_sem=sems[1].at[w],
                                         device_id=to, device_id_type=MESH).start()

    def finish(cin, cout, sems):
        for w, src, dst, to in copies(cin, cout):
            pltpu.make_async_remote_copy(src_ref=src, dst_ref=dst, send_sem=sems[0].at[w], recv_sem=sems[1].at[w],
                                         device_id=to, device_id_type=MESH).wait()

    return Comm(parts, [SDS((N_SHARD, a.shape[1] // 2, a.shape[2]), f32) for a in parts], {}, [n, n], start, finish, done)


def shard_piece(sums16, done):
    n = len(sums16)

    def copies(cin, cout):
        x, y, c, chips = _place()
        return [(n * j + w, cin[w].at[2 * cx + cy], cout[w].at[j], (cx, cy, c))
                for j, (cx, cy) in enumerate(chips) for w in range(n)]

    def start(cin, cout, sems):
        for k, src, dst, to in copies(cin, cout):
            pltpu.make_async_remote_copy(src_ref=src, dst_ref=dst, send_sem=sems[0].at[k], recv_sem=sems[1].at[k],
                                         device_id=to, device_id_type=MESH).start()

    def finish(cin, cout, sems):
        for k, src, dst, to in copies(cin, cout):
            pltpu.make_async_remote_copy(src_ref=src, dst_ref=dst, send_sem=sems[0].at[k], recv_sem=sems[1].at[k],
                                         device_id=to, device_id_type=MESH).wait()

    return Comm(sums16, [SDS((3,) + a.shape[1:], bf16) for a in sums16], {}, [3 * n, 3 * n], start, finish, done)


def place_shard(a, layer, idx, dtype):
    _, r, cdim = a.shape
    tr = min(r, 512)

    def body(idx_ref, a_ref, o_ref):
        o_ref[0] = a_ref[0].astype(dtype)

    return pl.pallas_call(
        body, name="place_shard",
        grid_spec=pltpu.PrefetchScalarGridSpec(
            num_scalar_prefetch=1, grid=(r // tr,),
            in_specs=[pl.BlockSpec((1, tr, cdim), lambda i, idx_ref: (layer, i, 0))],
            out_specs=pl.BlockSpec((1, tr, cdim), lambda i, idx_ref: (idx_ref[1], i, 0))),
        out_shape=SDS((N_SHARD, r, cdim), dtype),
        compiler_params=_params(("arbitrary",)),
    )(idx, a)


_KEYS = ("w_in", "w_out", "w_up", "w_dn")


class MeshWeights:
    def __init__(self, w_in, w_out, w_up, w_down, conv_dw_w, lru_conv_w, idx):
        self.idx = idx
        src = dict(w_in=w_in, w_out=w_out, w_up=w_up, w_dn=w_down)
        self.buf = {(n, l): place_shard(src[n], l, idx, bf16) for n in _KEYS for l in range(DEPTH)}
        self.conv = {(n, l): place_shard(a, l, idx, f32)
                     for n, a in (("cw", conv_dw_w), ("lw", lru_conv_w)) for l in range(DEPTH)}
        self.cache, self.parts, self.sum32, self.sum16, self.got = {}, {}, {}, {}, {}
        first, small = [("w_in", 0)], list(self.conv)

        def store_conv(outs):
            self.conv.update(zip(small, outs))

        standalone([self._gather(gather_ici_piece, first), gather_full_piece([self.conv[k] for k in small], store_conv)],
                   "gather_first_ici")
        standalone([self._gather(gather_d2d_piece, first)], "gather_first_d2d")

    def _gather(self, piece, keys):
        def done(outs):
            self.buf.update(zip(keys, outs))
        return piece([self.buf[k] for k in keys], done)

    def _pair(self, keys):
        def done(outs):
            for k, recv in zip(keys, outs):
                self.sum32[k], self.sum16[k] = chip_sum(self.parts[k], recv, self.idx)
        return pair_piece([self.parts[k] for k in keys], done)

    def _shard(self, keys):
        def done(outs):
            self.got.update(zip(keys, outs))
        return shard_piece([self.sum16[k] for k in keys], done)

    def conv_weights(self):
        out = []
        for n in ("cw", "lw"):
            a = jnp.stack([self.conv[(n, l)] for l in range(DEPTH)])
            out.append(a.transpose(0, 2, 1, 3).reshape(DEPTH, a.shape[2], N_SHARD * a.shape[3]))
        return out

    def weight(self, name, l):
        if (name, l) not in self.cache:
            b = self.buf[(name, l)]
            if name == "w_in":
                b = b.transpose(1, 0, 2).reshape(D_MODEL, IN_W)
            elif name == "w_out":
                b = b.reshape(D_MODEL, D_MODEL)
            self.cache[(name, l)] = b
        return self.cache[(name, l)]

    def host(self, point, l):
        ici, d2d = gather_ici_piece, gather_d2d_piece
        rest1 = [("w_out", 1), ("w_up", 1), ("w_dn", 1)]
        plan = {
            ("inproj_fwd", 0): lambda: [self._gather(ici, [("w_out", 0)])],
            ("attn_fwd", 0): lambda: [self._gather(ici, [("w_up", 0)]), self._gather(d2d, [("w_out", 0)])],
            ("branch_fwd", 0): lambda: [self._gather(ici, [("w_dn", 0)]), self._gather(d2d, [("w_up", 0)])],
            ("outproj_fwd", 0): lambda: [self._gather(d2d, [("w_dn", 0)]), self._gather(ici, [("w_in", 1)])],
            ("mlp_fwd", 0): lambda: [self._gather(ici, rest1), self._gather(d2d, [("w_in", 1)])],
            ("attn_fwd", 1): lambda: [self._gather(d2d, rest1)],
            ("attn_bwd", 1): lambda: [self._pair([("w_up", 1), ("w_dn", 1)])],
            ("inproj_bwd", 1): lambda: [self._pair([("w_out", 1)])],
            ("mlp_bwd_act", 0): lambda: [self._shard([("w_up", 1), ("w_dn", 1)])],
            ("mlp_bwd_w", 0): lambda: [self._pair([("w_in", 1)])],
            ("attn_bwd", 0): lambda: [self._shard([("w_in", 1), ("w_out", 1)]),
                                      self._pair([("w_up", 0), ("w_dn", 0), ("w_out", 0)])],
            ("branch_bwd_a", 0): lambda: [self._shard([("w_up", 0)])],
            ("branch_bwd_b", 0): lambda: [self._shard([("w_dn", 0), ("w_out", 0)])],
        }
        make = plan.get((point, l))
        return make() if make else None

    def grad(self, name, l, g):
        if name == "w_in":
            g = g.reshape(D_MODEL, N_SHARD, IN_W // N_SHARD).transpose(1, 0, 2)
        elif name == "w_out":
            g = g.reshape(N_SHARD, D_MODEL // N_SHARD, D_MODEL)
        self.parts[(name, l)] = g

    def big_grads(self):
        last = [("w_in", 0)]
        standalone([self._pair(last)], "pair_last")
        standalone([self._shard(last)], "shard_last")
        tots = []
        for n in _KEYS:
            t = None
            for l in reversed(range(DEPTH)):
                t = shard_sum(self.sum32[(n, l)], self.got[(n, l)], self.idx, l, t)
            tots.append(t)
        return tots


def chip_sum(g, recv, idx):
    _, r, cdim = g.shape
    half = r // 2
    tr = min(half, 512)
    nh = half // tr

    def body(idx_ref, g_ref, r_ref, o32_ref, o16_ref):
        tot = g_ref[0] + r_ref[0]
        o16_ref[0] = tot.astype(bf16)

        @pl.when(pl.program_id(0) == N_SHARD - 1)
        def _():
            o32_ref[...] = tot

    def slab(k, idx_ref):
        return lax.rem(idx_ref[1] + 1 + k, N_SHARD)

    def o16_map(k, i, idx_ref):
        return (slab(k, idx_ref), i, 0)

    return pl.pallas_call(
        body, name="chip_sum",
        grid_spec=pltpu.PrefetchScalarGridSpec(
            num_scalar_prefetch=1, grid=(N_SHARD, nh),
            in_specs=[pl.BlockSpec((1, tr, cdim), lambda k, i, idx_ref: (slab(k, idx_ref), idx_ref[0] * nh + i, 0)),
                      pl.BlockSpec((1, tr, cdim), lambda k, i, idx_ref: (slab(k, idx_ref), i, 0))],
            out_specs=[pl.BlockSpec((tr, cdim), lambda k, i, idx_ref: (jnp.where(k == N_SHARD - 1, i, 0), 0)),
                       pl.BlockSpec((1, tr, cdim), o16_map)]),
        out_shape=[SDS((half, cdim), f32), SDS((N_SHARD, half, cdim), bf16)],
        compiler_params=_params(("arbitrary", "arbitrary")),
    )(idx, g, recv)


def shard_sum(sum32, got16, idx, layer, prev):
    half, cdim = sum32.shape
    tr = min(half, 512)

    def body(idx_ref, a_ref, r0_ref, r1_ref, r2_ref, *rest):
        o_ref = rest[-1]
        o_ref[0, 0] = ((a_ref[...] + r0_ref[0].astype(f32)) + r1_ref[0].astype(f32)) + r2_ref[0].astype(f32)

    def rel(j):
        return pl.BlockSpec((1, tr, cdim), lambda i, idx_ref: (j, i, 0))

    in_specs = [pl.BlockSpec((tr, cdim), lambda i, idx_ref: (i, 0)), rel(0), rel(1), rel(2)]
    operands = [idx, sum32, got16, got16, got16]
    aliases = {}
    if prev is not None:
        in_specs.append(_HBM)
        operands.append(prev)
        aliases = {5: 0}
    return pl.pallas_call(
        body, name="shard_sum",
        grid_spec=pltpu.PrefetchScalarGridSpec(
            num_scalar_prefetch=1, grid=(half // tr,), in_specs=in_specs,
            out_specs=pl.BlockSpec((1, 1, tr, cdim), lambda i, idx_ref: (layer, idx_ref[0], i, 0))),
        out_shape=SDS((DEPTH, 2, half, cdim), f32), input_output_aliases=aliases,
        compiler_params=_params(("arbitrary",)),
    )(*operands)


def halves_exchange(tots):
    nw = len(tots)

    def body(*refs):
        bufs = refs[nw:2 * nw]
        send_sem, recv_sem = refs[2 * nw:]
        x, y, c, _ = _place()

        def copy(w, l, half_idx):
            return pltpu.make_async_remote_copy(
                src_ref=bufs[w].at[l, half_idx], dst_ref=bufs[w].at[l, half_idx], send_sem=send_sem.at[DEPTH * w + l],
                recv_sem=recv_sem.at[DEPTH * w + l], device_id=(x, y, 1 - c), device_id_type=MESH)

        sends = [copy(w, l, c) for w in range(nw) for l in range(DEPTH)]
        for cp in sends:
            cp.start()
        for w in range(nw):
            for l in range(DEPTH):
                copy(w, l, 1 - c).wait_recv()
        for cp in sends:
            cp.wait_send()

    return pl.pallas_call(
        body, name="halves_exchange", in_specs=[_HBM] * nw, out_specs=[_HBM] * nw,
        out_shape=[SDS(a.shape, f32) for a in tots], input_output_aliases={w: w for w in range(nw)},
        scratch_shapes=[pltpu.SemaphoreType.DMA((DEPTH * nw,)), pltpu.SemaphoreType.DMA((DEPTH * nw,))],
    )(*tots)


N_DEV = 8


def small_allreduce(vec):
    r = vec.shape[0]

    def body(v_ref, o_ref, all_ref, send_sems, recv_sems):
        x, y, c, chips = _place()
        me, sib = (x, y, c), (x, y, 1 - c)

        def rows(px, py, pc):
            return all_ref.at[4 * px + 2 * py + pc]

        def copy(k, block, to, src=None):
            return pltpu.make_async_remote_copy(
                src_ref=rows(*block) if src is None else src, dst_ref=rows(*block), send_sem=send_sems.at[k],
                recv_sem=recv_sems.at[k], device_id=to, device_id_type=MESH)

        first = [copy(0, me, sib, src=v_ref)]
        first += [copy(1 + j, me, (*chip, c), src=v_ref) for j, chip in enumerate(chips)]
        for cp in first:
            cp.start()
        rows(*me)[...] = v_ref[...]
        passed = [copy(4 + j, (*chip, c), sib) for j, chip in enumerate(chips)]
        for j, chip in enumerate(chips):
            copy(1 + j, (*chip, c), me).wait_recv()
            passed[j].start()
        copy(0, sib, me).wait_recv()
        for j, chip in enumerate(chips):
            copy(4 + j, (*chip, 1 - c), me).wait_recv()
        for cp in first + passed:
            cp.wait_send()
        acc = all_ref[0]
        for d in range(1, N_DEV):
            acc = acc + all_ref[d]
        o_ref[...] = acc

    return pl.pallas_call(
        body, name="small_allreduce",
        in_specs=[pl.BlockSpec(memory_space=pltpu.VMEM)], out_specs=pl.BlockSpec(memory_space=pltpu.VMEM),
        out_shape=SDS((r, LANES), f32),
        scratch_shapes=[pltpu.VMEM((N_DEV, r, LANES), f32), pltpu.SemaphoreType.DMA((7,)), pltpu.SemaphoreType.DMA((7,))],
    )(vec)


def _adamw_math(w, g, m, v):
    m = ADAM_B1 * m + (1.0 - ADAM_B1) * g
    v = ADAM_B2 * v + (1.0 - ADAM_B2) * (g * g)
    m_hat = m / (1.0 - ADAM_B1 ** ADAM_STEP)
    v_hat = v / (1.0 - ADAM_B2 ** ADAM_STEP)
    delta = -ADAM_LR * (m_hat / (jnp.sqrt(v_hat) + ADAM_EPS) + ADAM_WD * w)
    return delta, m, v


def adamw_big(w, g, m, v):
    _, r, cdim = w.shape
    tr = min(r, 256)

    def body(w_ref, g_ref, m_ref, v_ref, go_ref, d_ref, mo_ref, vo_ref):
        g = g_ref[...]
        d, mm, vv = _adamw_math(w_ref[...], g, m_ref[...], v_ref[...])
        go_ref[...] = g
        d_ref[...] = d
        mo_ref[...] = mm
        vo_ref[...] = vv

    blk = pl.BlockSpec((1, tr, cdim), lambda l, i: (l, i, 0))
    return pl.pallas_call(
        body, name="adamw_big", grid=(DEPTH, r // tr), in_specs=[blk] * 4, out_specs=[blk] * 4,
        out_shape=[SDS(w.shape, f32)] * 4, compiler_params=_params(("parallel", "parallel")),
    )(w, g, m, v)


def adamw_small(ws, gs, ms, vs):
    n = len(ws)

    def body(*refs):
        w_r, g_r, m_r, v_r = refs[:n], refs[n:2 * n], refs[2 * n:3 * n], refs[3 * n:4 * n]
        d_o, m_o, v_o = refs[4 * n:5 * n], refs[5 * n:6 * n], refs[6 * n:7 * n]
        for k in range(n):
            d, mm, vv = _adamw_math(w_r[k][...], g_r[k][...], m_r[k][...], v_r[k][...])
            d_o[k][...] = d
            m_o[k][...] = mm
            v_o[k][...] = vv

    vm = pl.BlockSpec(memory_space=pltpu.VMEM)
    shapes = [SDS(a.shape, f32) for a in ws]
    outs = pl.pallas_call(
        body, name="adamw_small", in_specs=[vm] * (4 * n), out_specs=[vm] * (3 * n), out_shape=shapes * 3,
    )(*ws, *gs, *ms, *vs)
    return outs[:n], outs[n:2 * n], outs[2 * n:]


_BIG = ("w_in", "w_out", "w_up", "w_down")
_WEIGHTS = ("norm1", "w_in", "attn_sinks", "conv_dw_w", "conv_dw_b", "conv_ln_g", "conv_ln_b", "lru_conv_w", "lru_conv_b",
            "lru_wa", "lru_ba", "lru_wx", "lru_bx", "lru_lambda", "mix_norm", "w_out", "norm2", "w_up", "w_down", "final_norm")
_SMALL = tuple(n for n in _WEIGHTS if n not in _BIG)
_SMALL_FULL_SHAPE = dict(
    norm1=(DEPTH, D_MODEL), attn_sinks=(DEPTH, N_HEADS), conv_dw_w=(DEPTH, CONV_K, CONV_W), conv_dw_b=(DEPTH, CONV_W),
    conv_ln_g=(DEPTH, CONV_W), conv_ln_b=(DEPTH, CONV_W), lru_conv_w=(DEPTH, LRU_K, LRU_W), lru_conv_b=(DEPTH, LRU_W),
    lru_wa=(DEPTH, LRU_HEADS, 64, 64), lru_ba=(DEPTH, LRU_HEADS, 64), lru_wx=(DEPTH, LRU_HEADS, 64, 64),
    lru_bx=(DEPTH, LRU_HEADS, 64), lru_lambda=(DEPTH, LRU_W), mix_norm=(DEPTH, D_MODEL), norm2=(DEPTH, D_MODEL),
    final_norm=(D_MODEL,))
_CHANNEL_SHARDED = ("conv_dw_w", "lru_conv_w")


def _pad_lanes(n):
    return -(-n // LANES) * LANES


def _pack(named):
    flat = []
    for a in named:
        a = a.reshape(-1)
        flat.append(jnp.pad(a, (0, _pad_lanes(a.shape[0]) - a.shape[0])))
    v = jnp.concatenate(flat)
    rows = -(-v.shape[0] // (8 * LANES)) * 8
    return jnp.pad(v, (0, rows * LANES - v.shape[0])).reshape(rows, LANES)


def _unpack(vec, shapes):
    flat = vec.reshape(-1)
    out, off = [], 0
    for shp in shapes:
        n = math.prod(shp)
        out.append(flat[off:off + n].reshape(shp))
        off += _pad_lanes(n)
    return out


def _as2d(a):
    return a.reshape(-1, a.shape[-1]) if a.ndim > 1 else a.reshape(1, -1)


def kernel(x, norm1, w_in, attn_sinks, conv_dw_w, conv_dw_b, conv_ln_g, conv_ln_b, lru_conv_w, lru_conv_b, lru_wa, lru_ba, lru_wx, lru_bx, lru_lambda, mix_norm, w_out, norm2, w_up, w_down, final_norm, loss_target, m_norm1, m_w_in, m_attn_sinks, m_conv_dw_w, m_conv_dw_b, m_conv_ln_g, m_conv_ln_b, m_lru_conv_w, m_lru_conv_b, m_lru_wa, m_lru_ba, m_lru_wx, m_lru_bx, m_lru_lambda, m_mix_norm, m_w_out, m_norm2, m_w_up, m_w_down, m_final_norm, v_norm1, v_w_in, v_attn_sinks, v_conv_dw_w, v_conv_dw_b, v_conv_ln_g, v_conv_ln_b, v_lru_conv_w, v_lru_conv_b, v_lru_wa, v_lru_ba, v_lru_wx, v_lru_bx, v_lru_lambda, v_mix_norm, v_w_out, v_norm2, v_w_up, v_w_down, v_final_norm):
    wts = dict(norm1=norm1, w_in=w_in, attn_sinks=attn_sinks, conv_dw_w=conv_dw_w, conv_dw_b=conv_dw_b, conv_ln_g=conv_ln_g,
               conv_ln_b=conv_ln_b, lru_conv_w=lru_conv_w, lru_conv_b=lru_conv_b, lru_wa=lru_wa, lru_ba=lru_ba, lru_wx=lru_wx,
               lru_bx=lru_bx, lru_lambda=lru_lambda, mix_norm=mix_norm, w_out=w_out, norm2=norm2, w_up=w_up, w_down=w_down,
               final_norm=final_norm)
    mom = dict(norm1=m_norm1, w_in=m_w_in, attn_sinks=m_attn_sinks, conv_dw_w=m_conv_dw_w, conv_dw_b=m_conv_dw_b,
               conv_ln_g=m_conv_ln_g, conv_ln_b=m_conv_ln_b, lru_conv_w=m_lru_conv_w, lru_conv_b=m_lru_conv_b, lru_wa=m_lru_wa,
               lru_ba=m_lru_ba, lru_wx=m_lru_wx, lru_bx=m_lru_bx, lru_lambda=m_lru_lambda, mix_norm=m_mix_norm, w_out=m_w_out,
               norm2=m_norm2, w_up=m_w_up, w_down=m_w_down, final_norm=m_final_norm)
    var = dict(norm1=v_norm1, w_in=v_w_in, attn_sinks=v_attn_sinks, conv_dw_w=v_conv_dw_w, conv_dw_b=v_conv_dw_b,
               conv_ln_g=v_conv_ln_g, conv_ln_b=v_conv_ln_b, lru_conv_w=v_lru_conv_w, lru_conv_b=v_lru_conv_b, lru_wa=v_lru_wa,
               lru_ba=v_lru_ba, lru_wx=v_lru_wx, lru_bx=v_lru_bx, lru_lambda=v_lru_lambda, mix_norm=v_mix_norm, w_out=v_w_out,
               norm2=v_norm2, w_up=v_w_up, w_down=v_w_down, final_norm=v_final_norm)

    c_idx = lax.axis_index("c").astype(jnp.int32)
    s_idx = (2 * lax.axis_index("x") + lax.axis_index("y")).astype(jnp.int32)
    idx = jnp.stack([c_idx, s_idx])

    cs = MeshWeights(w_in, w_out, w_up, w_down, conv_dw_w, lru_conv_w, idx)
    sp = {n: wts[n] for n in _SMALL}
    sp["conv_dw_w"], sp["lru_conv_w"] = cs.conv_weights()
    loss_blk, grad_x, tots, small_g, d_gf = train_local(x[0], loss_target[0], sp, cs)
    grads_big = {n: a.reshape(wts[n].shape) for n, a in zip(_BIG, halves_exchange(tots))}

    stacked = [jnp.stack([small_g[l][n] for l in range(DEPTH)]) for n in _SMALL if n != "final_norm"]
    packed = _pack(stacked + [d_gf, loss_blk[0, 0:1]])
    summed = small_allreduce(packed)
    names = [n for n in _SMALL if n != "final_norm"] + ["final_norm"]
    unpacked = _unpack(summed, [_SMALL_FULL_SHAPE[n] for n in names] + [(1,)])
    loss = unpacked[-1][0]
    grads = dict(zip(names, unpacked[:-1]))
    for n in _CHANNEL_SHARDED:
        width = wts[n].shape[-1]
        grads[n] = lax.dynamic_slice_in_dim(grads[n], s_idx * width, width, axis=2)
    grads.update(grads_big)

    delta, new_m, new_v = {}, {}, {}
    for n in _BIG:
        grads[n], delta[n], new_m[n], new_v[n] = adamw_big(wts[n], grads[n], mom[n], var[n])
    sm = list(_SMALL)
    d_s, m_s, v_s = adamw_small([_as2d(wts[n]) for n in sm], [_as2d(grads[n]) for n in sm],
                                [_as2d(mom[n]) for n in sm], [_as2d(var[n]) for n in sm])
    for k, n in enumerate(sm):
        delta[n], new_m[n], new_v[n] = (a.reshape(wts[n].shape) for a in (d_s[k], m_s[k], v_s[k]))

    return (loss, grad_x[None], *[grads[n] for n in _WEIGHTS], *[delta[n] for n in _WEIGHTS],
            *[new_m[n] for n in _WEIGHTS], *[new_v[n] for n in _WEIGHTS])
```

```python
import functools
import math

import jax
import jax.numpy as jnp
from jax import lax
from jax.experimental import pallas as pl
from jax.experimental.pallas import tpu as pltpu

f32 = jnp.float32
bf16 = jnp.bfloat16
SDS = jax.ShapeDtypeStruct

D_MODEL = 1024
DEPTH = 2
ATTN_W = 512
KV_W = 128
HEAD_DIM = 64
N_HEADS = 8
BLOCK = 128
CONV_W = 256
CONV_K = 31
LRU_W = 256
LRU_K = 4
LRU_HEADS = 4
LRU_C = 8.0
IN_W = 1792
D_FF = 4096
N_SHARD = 4
FF_CHUNK = D_FF // N_SHARD
RMS_EPS = 1e-6
LN_EPS = 1e-5
MASK_VALUE = -1e30
HALO = 32
LANES = 128
VMEM_LIMIT = 56 * 1024 * 1024

ADAM_LR = 0.001
ADAM_B1 = 0.9
ADAM_B2 = 0.999
ADAM_EPS = 1e-08
ADAM_WD = 0.01
ADAM_STEP = 10

MESH = pl.DeviceIdType.MESH


def _dot(a, b):
    return jnp.dot(a, b, preferred_element_type=f32)


def _dot_nt(a, b):
    return lax.dot_general(a, b, (((1,), (1,)), ((), ())), preferred_element_type=f32)


def _dot_tn(a, b):
    return lax.dot_general(a, b, (((0,), (0,)), ((), ())), preferred_element_type=f32)


def _rms_fwd(x, g):
    r = lax.rsqrt(jnp.mean(x * x, axis=-1, keepdims=True) + RMS_EPS)
    return x * r * g, r


def _rms_bwd(dy, x, r, g):
    t = dy * g
    dx = r * t - x * (r * r * r) * jnp.mean(t * x, axis=-1, keepdims=True)
    dg = jnp.sum(dy * x * r, axis=0, keepdims=True)
    return dx, dg


def _sigmoid(x):
    return jax.nn.sigmoid(x)


_GELU_K = math.sqrt(2.0 / math.pi)


def _gelu(x):
    t = jnp.tanh(_GELU_K * (x + 0.044715 * x * x * x))
    return 0.5 * x * (1.0 + t), t


def _gelu_grad(x, t):
    return 0.5 * (1.0 + t) + 0.5 * x * (1.0 - t * t) * _GELU_K * (1.0 + 3.0 * 0.044715 * x * x)


def _log1p(x):
    return jnp.where(x < 1e-4, x - 0.5 * x * x, jnp.log(1.0 + x))


def _softplus(x):
    return jnp.maximum(x, 0.0) + _log1p(jnp.exp(-jnp.abs(x)))


def _neg_expm1(x):
    series = -x * (1.0 + 0.5 * x * (1.0 + x * (1.0 / 3.0) * (1.0 + 0.25 * x)))
    return jnp.where(x > -0.01, series, 1.0 - jnp.exp(x))


def _sublane_rolls(x, count, forward):
    n = x.shape[0]
    return [x if b == 0 else pltpu.roll(x, b if forward else n - b, 0) for b in range(count)]


def _conv_taps(xpad, w, k_width):
    t_rows = xpad.shape[0] - HALO
    rolled = _sublane_rolls(xpad, min(k_width, 8), forward=True)
    acc = None
    for k in range(k_width):
        hi, lo = divmod((k_width - 1) - k, 8)
        term = rolled[lo][HALO - 8 * hi:HALO - 8 * hi + t_rows] * w[k:k + 1, :]
        acc = term if acc is None else acc + term
    return acc


def _conv_taps_bwd(dpad, upad, w, k_width, t_rows):
    n_lo = min(k_width, 8)
    d_rolled = _sublane_rolls(dpad, n_lo, forward=False)
    u_rolled = _sublane_rolls(upad, n_lo, forward=True)
    d_in = None
    dw_rows = []
    d_out = dpad[:t_rows]
    for k in range(k_width):
        hi, lo = divmod((k_width - 1) - k, 8)
        term = d_rolled[lo][8 * hi:8 * hi + t_rows] * w[k:k + 1, :]
        d_in = term if d_in is None else d_in + term
        us = u_rolled[lo][HALO - 8 * hi:HALO - 8 * hi + t_rows]
        dw_rows.append(jnp.sum(d_out * us, axis=0, keepdims=True))
    return d_in, dw_rows


SUBLANES = 8


def _scan_fwd(a, b, h0):
    t_rows = a.shape[0]
    sub = jnp.bitwise_and(lax.broadcasted_iota(jnp.int32, a.shape, 0), SUBLANES - 1)
    for d in (1, 2, 4):
        a_sh = jnp.where(sub < d, 1.0, pltpu.roll(a, d, 0))
        b_sh = jnp.where(sub < d, 0.0, pltpu.roll(b, d, 0))
        b = a * b_sh + b
        a = a * a_sh
    out, carry = [], h0
    for g in range(t_rows // SUBLANES):
        rows = slice(g * SUBLANES, (g + 1) * SUBLANES)
        hg = a[rows] * carry + b[rows]
        out.append(hg)
        carry = hg[SUBLANES - 1:SUBLANES]
    return jnp.concatenate(out, axis=0)


def _scan_bwd(a, b, l_end):
    t_rows = a.shape[0]
    sub = jnp.bitwise_and(lax.broadcasted_iota(jnp.int32, a.shape, 0), SUBLANES - 1)
    for d in (1, 2, 4):
        a_sh = jnp.where(sub >= SUBLANES - d, 1.0, pltpu.roll(a, t_rows - d, 0))
        b_sh = jnp.where(sub >= SUBLANES - d, 0.0, pltpu.roll(b, t_rows - d, 0))
        b = b + a * b_sh
        a = a * a_sh
    out, carry = [], l_end
    for g in reversed(range(t_rows // SUBLANES)):
        rows = slice(g * SUBLANES, (g + 1) * SUBLANES)
        lg = b[rows] + a[rows] * carry
        out.append(lg)
        carry = lg[0:1]
    return jnp.concatenate(out[::-1], axis=0)


def _full(shape, single=False):
    nd = len(shape)
    if single:
        return pl.BlockSpec(shape, lambda *_: (0,) * nd, pipeline_mode=pl.Buffered(1))
    return pl.BlockSpec(shape, lambda *_: (0,) * nd)


def _params(sem, vmem=None):
    return pltpu.CompilerParams(dimension_semantics=sem, vmem_limit_bytes=vmem)


def _tile(s):
    return min(512, s)


def _row_tile(rows, cap):
    return next(t for t in range(min(cap, rows) // 8 * 8, 0, -8) if rows % t == 0)


def inproj_fwd(h, g1, w_in_t, pieces=None):
    s = h.shape[0]
    tm = _tile(s)

    def body(h_ref, g_ref, w_ref, z_ref):
        hn, _ = _rms_fwd(h_ref[...], g_ref[...])
        z_ref[...] = _dot_nt(hn.astype(bf16), w_ref[...]).astype(bf16)

    return _pcall(
        body, pieces, name="inproj_fwd", grid=(s // tm,),
        in_specs=[pl.BlockSpec((tm, D_MODEL), lambda i: (i, 0)), _full((1, D_MODEL)), _full((IN_W, D_MODEL))],
        out_specs=[pl.BlockSpec((tm, IN_W), lambda i: (i, 0))],
        out_shape=[SDS((s, IN_W), bf16)],
        operands=[h, g1, w_in_t], sem=("parallel",), vmem=VMEM_LIMIT)[0]


ATT_ROWS = N_HEADS * BLOCK
ATT_BLOCKS_PER_STEP = 8


def _attn_bias():
    qi = jnp.arange(BLOCK)[None, :]
    key = jnp.arange(2 * BLOCK)[:, None]
    band = (key > qi) & (key <= qi + BLOCK)
    first = band & (key >= BLOCK)
    mask = jnp.where(jnp.stack([first, band]), 0.0, MASK_VALUE).astype(f32)
    return jnp.tile(mask, (1, 1, N_HEADS // 2))


def _attn_band(kvc, kvp):
    kb = jnp.concatenate([kvp[:, :KV_W], kvc[:, :KV_W]], axis=0)
    vb = jnp.concatenate([kvp[:, KV_W:], kvc[:, KV_W:]], axis=0)
    lane = lax.broadcasted_iota(jnp.int32, kb.shape, 1)
    kb_sw = pltpu.roll(kb, HEAD_DIM, 1)
    vb_sw = pltpu.roll(vb, HEAD_DIM, 1)
    kx = [jnp.where(lane < HEAD_DIM, kb, kb_sw), jnp.where(lane >= HEAD_DIM, kb, kb_sw)]
    vx = [jnp.where(lane < HEAD_DIM, vb, vb_sw), jnp.where(lane >= HEAD_DIM, vb, vb_sw)]
    return kx, vx


def _stack_heads(x, mlo):
    zero = jnp.zeros((BLOCK, LANES), x.dtype)
    out = []
    for hk in range(2):
        parts = []
        for j in (2 * hk, 2 * hk + 1):
            xj = x[:, j * LANES:(j + 1) * LANES]
            parts += [jnp.where(mlo, xj, zero), jnp.where(mlo, zero, xj)]
        out.append(jnp.concatenate(parts, axis=0))
    return out


def _unstack_heads(y, mlo):
    cols = []
    for hk in range(2):
        for t in range(2):
            base = 2 * t * BLOCK
            cols.append(jnp.where(mlo, y[hk][base:base + BLOCK], y[hk][base + BLOCK:base + 2 * BLOCK]))
    return jnp.concatenate(cols, axis=1)


def _attn_probs(q4, kx, bias_t, sink_row):
    out = []
    half = ATT_ROWS // 2
    for hk in range(2):
        s = _dot_nt(kx[hk], q4[hk]) + bias_t
        sink = sink_row[:, hk * half:(hk + 1) * half]
        m = jnp.maximum(jnp.max(s, axis=0, keepdims=True), sink)
        p = jnp.exp(s - m)
        e_sink = jnp.exp(sink - m)
        inv = 1.0 / (jnp.sum(p, axis=0, keepdims=True) + e_sink)
        out.append((p * inv, e_sink * inv))
    return out


def attn_fwd(z, sink_col, bias, g_a, pieces=None):
    s = z.shape[0]
    per = min(ATT_BLOCKS_PER_STEP, s // BLOCK)
    tq = per * BLOCK

    def body(q_ref, kv_ref, kvp_ref, sk_ref, b_ref, g_ref, o_ref, y_ref):
        n = pl.program_id(0)
        mlo = lax.broadcasted_iota(jnp.int32, (BLOCK, LANES), 1) < HEAD_DIM
        for b in range(per):
            rows = slice(b * BLOCK, (b + 1) * BLOCK)
            kvp = kvp_ref[...] if b == 0 else kv_ref[(b - 1) * BLOCK:b * BLOCK, :]
            bias_b = b_ref[jnp.minimum(n, 1)] if b == 0 else b_ref[1]
            kx, vx = _attn_band(kv_ref[rows, :], kvp)
            q4 = _stack_heads(q_ref[rows, :] * (HEAD_DIM ** -0.5), mlo)
            probs = _attn_probs(q4, kx, bias_b, sk_ref[...])
            o = _unstack_heads([_dot_tn(probs[hk][0].astype(bf16), vx[hk]) for hk in range(2)], mlo)
            o_ref[rows, :] = o.astype(bf16)
            y, _ = _rms_fwd(o, g_ref[...])
            y_ref[rows, :] = y.astype(bf16)

    return _pcall(
        body, pieces, name="attn_fwd", grid=(s // tq,),
        in_specs=[pl.BlockSpec((tq, ATTN_W), lambda n: (n, 0)),
                  pl.BlockSpec((tq, 2 * KV_W), lambda n: (n, 2)),
                  pl.BlockSpec((BLOCK, 2 * KV_W), lambda n: (jnp.maximum(n * per - 1, 0), 2)),
                  _full((1, ATT_ROWS)), _full((2, 2 * BLOCK, ATT_ROWS // 2)), _full((1, ATTN_W))],
        out_specs=[pl.BlockSpec((tq, ATTN_W), lambda n: (n, 0)), pl.BlockSpec((tq, ATTN_W), lambda n: (n, 0))],
        out_shape=[SDS((s, ATTN_W), bf16), SDS((s, ATTN_W), bf16)],
        operands=[z, z, z, sink_col, bias, g_a], sem=("parallel",))


def _lru_gates(xc, wa, ba, wx, bx, lam):
    xcb = xc.astype(bf16)
    r = _sigmoid(_dot(xcb, wa) + ba)
    ig = _sigmoid(_dot(xcb, wx) + bx)
    sp = _softplus(-lam)
    la = (-LRU_C * r) * sp
    a = jnp.exp(la)
    mult = jnp.sqrt(_neg_expm1(2.0 * la))
    return r, ig, sp, la, a, mult


def branch_fwd(z, p, pieces=None):
    s = z.shape[0]
    tm = _tile(s)
    hb = tm // HALO

    def body(cv_ref, cg_ref, rx_ref, rg_ref, cvh_ref, cgh_ref, rxh_ref,
             cw_ref, cb_ref, lng_ref, lnb_ref, lw_ref, lb_ref, wa_ref, ba_ref, wx_ref, bx_ref, lam_ref, gc_ref, gl_ref,
             conv_ref, hst_ref, nc_ref, nl_ref, carry_ref):
        i = pl.program_id(0)
        first = i == 0

        @pl.when(first)
        def _():
            carry_ref[...] = jnp.zeros_like(carry_ref)

        cval = cv_ref[...].astype(f32)
        u = cval * _sigmoid(cg_ref[...].astype(f32))
        hu = jnp.where(first, 0.0, cvh_ref[...].astype(f32) * _sigmoid(cgh_ref[...].astype(f32)))
        conv = _conv_taps(jnp.concatenate([hu, u], axis=0), cw_ref[...], CONV_K) + cb_ref[...]
        conv_ref[...] = conv
        mu = jnp.mean(conv, axis=-1, keepdims=True)
        xm = conv - mu
        rstd = lax.rsqrt(jnp.mean(xm * xm, axis=-1, keepdims=True) + LN_EPS)
        ln = xm * rstd * lng_ref[...] + lnb_ref[...]
        yc = ln * _sigmoid(ln)
        nc, _ = _rms_fwd(yc, gc_ref[...])
        nc_ref[...] = nc.astype(bf16)

        rx = rx_ref[...].astype(f32)
        hrx = jnp.where(first, 0.0, rxh_ref[...].astype(f32))
        xc = _conv_taps(jnp.concatenate([hrx, rx], axis=0), lw_ref[...], LRU_K) + lb_ref[...]
        r, ig, sp, la, a, mult = _lru_gates(xc, wa_ref[...], ba_ref[...], wx_ref[...], bx_ref[...], lam_ref[...])
        gx = mult * (ig * xc)
        hs = _scan_fwd(a, gx, carry_ref[0:1, :])
        carry_ref[...] = jnp.broadcast_to(hs[tm - 1:tm, :], carry_ref.shape)
        hst_ref[...] = hs
        gl, _ = _gelu(rg_ref[...].astype(f32))
        nl, _ = _rms_fwd(hs * gl, gl_ref[...])
        nl_ref[...] = nl.astype(bf16)

    def col(c):
        return pl.BlockSpec((tm, CONV_W), lambda i: (i, c))

    def halo(c):
        return pl.BlockSpec((HALO, CONV_W), lambda i: (jnp.maximum(i * hb - 1, 0), c))

    small = [p["cw"], p["cb"], p["lng"], p["lnb"], p["lw"], p["lb"], p["wa"], p["ba"], p["wx"], p["bx"], p["lam"],
             p["gc"], p["gl"]]
    row = pl.BlockSpec((tm, CONV_W), lambda i: (i, 0))
    return _pcall(
        body, pieces, name="branch_fwd", grid=(s // tm,),
        in_specs=[col(3), col(4), col(5), col(6), halo(3), halo(4), halo(5)] + [_full(a.shape) for a in small],
        out_specs=[row, row, row, row],
        out_shape=[SDS((s, CONV_W), f32), SDS((s, LRU_W), f32), SDS((s, CONV_W), bf16), SDS((s, LRU_W), bf16)],
        scratch_shapes=[pltpu.VMEM((8, LRU_W), f32)],
        operands=[z, z, z, z, z, z, z, *small], sem=("arbitrary",))


def outproj_fwd(ya, yc, yl, h, w_out, g2, pieces=None):
    s = h.shape[0]
    tm = _tile(s)

    def body(ya_ref, yc_ref, yl_ref, h_ref, w_ref, g_ref, h1_ref, hn_ref):
        y = jnp.concatenate([ya_ref[...], yc_ref[...], yl_ref[...]], axis=1)
        h1 = h_ref[...] + _dot(y, w_ref[...])
        h1_ref[...] = h1
        hn, _ = _rms_fwd(h1, g_ref[...])
        hn_ref[...] = hn.astype(bf16)

    return _pcall(
        body, pieces, name="outproj_fwd", grid=(s // tm,),
        in_specs=[pl.BlockSpec((tm, ATTN_W), lambda i: (i, 0)), pl.BlockSpec((tm, CONV_W), lambda i: (i, 0)),
                  pl.BlockSpec((tm, LRU_W), lambda i: (i, 0)), pl.BlockSpec((tm, D_MODEL), lambda i: (i, 0)),
                  _full((D_MODEL, D_MODEL)), _full((1, D_MODEL))],
        out_specs=[pl.BlockSpec((tm, D_MODEL), lambda i: (i, 0)), pl.BlockSpec((tm, D_MODEL), lambda i: (i, 0))],
        out_shape=[SDS((s, D_MODEL), f32), SDS((s, D_MODEL), bf16)],
        operands=[ya, yc, yl, h, w_out, g2], sem=("parallel",), vmem=VMEM_LIMIT)


def mlp_fwd(hn2, h1, w_up, w_dn, pieces=None):
    s = h1.shape[0]
    tm = _tile(s)

    def body(x_ref, h_ref, wu_ref, wd_ref, up_ref, h2_ref):
        h2_ref[...] = _mlp_chunks(x_ref[...], h_ref[...], wu_ref, wd_ref, up_ref)

    return _pcall(
        body, pieces, name="mlp_fwd", grid=(s // tm,),
        in_specs=[pl.BlockSpec((tm, D_MODEL), lambda i: (i, 0)), pl.BlockSpec((tm, D_MODEL), lambda i: (i, 0)),
                  _full((N_SHARD, D_MODEL, FF_CHUNK), single=True), _full((N_SHARD, FF_CHUNK, D_MODEL), single=True)],
        out_specs=[pl.BlockSpec((tm, D_FF), lambda i: (i, 0)), pl.BlockSpec((tm, D_MODEL), lambda i: (i, 0))],
        out_shape=[SDS((s, D_FF), bf16), SDS((s, D_MODEL), f32)],
        operands=[hn2, h1, w_up, w_dn], sem=("parallel",), vmem=VMEM_LIMIT)


def _mlp_chunks(x, acc, wu_ref, wd_ref, up_ref):
    for c in range(N_SHARD):
        u = _dot(x, wu_ref[c])
        up_ref[:, c * FF_CHUNK:(c + 1) * FF_CHUNK] = u.astype(bf16)
        act = jnp.square(jnp.maximum(u, 0.0)).astype(bf16)
        acc = acc + _dot(act, wd_ref[c])
    return acc


def _final_tile(x, tgt, g, dh_ref, loss_ref, dg_ref):
    y, r = _rms_fwd(x, g)
    err = y - tgt
    part = 0.5 * jnp.sum(jnp.mean(err * err, axis=-1, keepdims=True), axis=0, keepdims=True)
    loss_ref[...] += jnp.broadcast_to(part, loss_ref.shape)
    dx, dg = _rms_bwd(err * (1.0 / D_MODEL), x, r, g)
    dh_ref[...] = dx
    dg_ref[...] += dg


def last_layer_fwd(ya, yc, yl, h, w_out, g2, w_up, w_dn, tgt, gf):
    s = h.shape[0]
    tm = min(256, s)

    def body(ya_ref, yc_ref, yl_ref, h_ref, wo_ref, g2_ref, wu_ref, wd_ref, t_ref, gf_ref,
             h1_ref, hn_ref, up_ref, dh_ref, loss_ref, dg_ref):
        i = pl.program_id(0)

        @pl.when(i == 0)
        def _():
            loss_ref[...] = jnp.zeros_like(loss_ref)
            dg_ref[...] = jnp.zeros_like(dg_ref)

        y = jnp.concatenate([ya_ref[...], yc_ref[...], yl_ref[...]], axis=1)
        h1 = h_ref[...] + _dot(y, wo_ref[...])
        h1_ref[...] = h1
        hn, _ = _rms_fwd(h1, g2_ref[...])
        hn = hn.astype(bf16)
        hn_ref[...] = hn
        h2 = _mlp_chunks(hn, h1, wu_ref, wd_ref, up_ref)
        _final_tile(h2, t_ref[...], gf_ref[...], dh_ref, loss_ref, dg_ref)

    def rowb(w):
        return pl.BlockSpec((tm, w), lambda i: (i, 0))

    return pl.pallas_call(
        body, name="last_layer_fwd", grid=(s // tm,),
        in_specs=[rowb(ATTN_W), rowb(CONV_W), rowb(LRU_W), rowb(D_MODEL), _full((D_MODEL, D_MODEL), single=True),
                  _full((1, D_MODEL)), _full((N_SHARD, D_MODEL, FF_CHUNK), single=True),
                  _full((N_SHARD, FF_CHUNK, D_MODEL), single=True), rowb(D_MODEL), _full((1, D_MODEL))],
        out_specs=[rowb(D_MODEL), rowb(D_MODEL), rowb(D_FF), rowb(D_MODEL), _full((8, LANES)), _full((1, D_MODEL))],
        out_shape=[SDS((s, D_MODEL), f32), SDS((s, D_MODEL), bf16), SDS((s, D_FF), bf16), SDS((s, D_MODEL), f32),
                   SDS((8, LANES), f32), SDS((1, D_MODEL), f32)],
        compiler_params=_params(("arbitrary",), VMEM_LIMIT),
    )(ya, yc, yl, h, w_out, g2, w_up, w_dn, tgt, gf)


def mlp_bwd_act(dh, up, h1, g2, w_up, w_dn, pieces=None):
    s = dh.shape[0]
    tm = _tile(s)

    def body(dh_ref, up_ref, h1_ref, g_ref, wu_ref, wd_ref, dup_ref, dh1_ref, dg_ref):
        i = pl.program_id(0)

        @pl.when(i == 0)
        def _():
            dg_ref[...] = jnp.zeros_like(dg_ref)

        dh = dh_ref[...]
        dhb = dh.astype(bf16)
        d_hn = jnp.zeros((tm, D_MODEL), f32)
        for c in range(N_SHARD):
            d_act = _dot_nt(dhb, wd_ref[c])
            u = up_ref[:, c * FF_CHUNK:(c + 1) * FF_CHUNK].astype(f32)
            d_u = (d_act * (2.0 * jnp.maximum(u, 0.0))).astype(bf16)
            dup_ref[:, c * FF_CHUNK:(c + 1) * FF_CHUNK] = d_u
            d_hn = d_hn + _dot_nt(d_u, wu_ref[c])
        x = h1_ref[...]
        g = g_ref[...]
        _, r = _rms_fwd(x, g)
        dx, dg = _rms_bwd(d_hn, x, r, g)
        dh1_ref[...] = dh + dx
        dg_ref[...] += dg

    return _pcall(
        body, pieces, name="mlp_bwd_act", grid=(s // tm,),
        in_specs=[pl.BlockSpec((tm, D_MODEL), lambda i: (i, 0)), pl.BlockSpec((tm, D_FF), lambda i: (i, 0)),
                  pl.BlockSpec((tm, D_MODEL), lambda i: (i, 0)), _full((1, D_MODEL)),
                  _full((N_SHARD, D_MODEL, FF_CHUNK), single=True), _full((N_SHARD, FF_CHUNK, D_MODEL), single=True)],
        out_specs=[pl.BlockSpec((tm, D_FF), lambda i: (i, 0)), pl.BlockSpec((tm, D_MODEL), lambda i: (i, 0)),
                   _full((1, D_MODEL))],
        out_shape=[SDS((s, D_FF), bf16), SDS((s, D_MODEL), f32), SDS((1, D_MODEL), f32)],
        operands=[dh, up, h1, g2, w_up, w_dn], sem=("arbitrary",), vmem=VMEM_LIMIT)


def mlp_bwd_w(hn2, d_up, up, dh, pieces=None):
    s = dh.shape[0]
    tk = min(1024, s)

    def body(x_ref, du_ref, up_ref, dh_ref, dwu_ref, dwd_ref):
        k = pl.program_id(1)

        @pl.when(k == 0)
        def _():
            dwu_ref[...] = jnp.zeros_like(dwu_ref)
            dwd_ref[...] = jnp.zeros_like(dwd_ref)

        dwu_ref[0] += _dot_tn(x_ref[...], du_ref[...])
        act = jnp.square(jnp.maximum(up_ref[...].astype(f32), 0.0)).astype(bf16)
        dwd_ref[0] += _dot_tn(act, dh_ref[...].astype(bf16))

    return _pcall(
        body, pieces, name="mlp_bwd_w", grid=(N_SHARD, s // tk),
        in_specs=[pl.BlockSpec((tk, D_MODEL), lambda c, k: (k, 0)), pl.BlockSpec((tk, FF_CHUNK), lambda c, k: (k, c)),
                  pl.BlockSpec((tk, FF_CHUNK), lambda c, k: (k, c)), pl.BlockSpec((tk, D_MODEL), lambda c, k: (k, 0))],
        out_specs=[pl.BlockSpec((1, D_MODEL, FF_CHUNK), lambda c, k: (c, 0, 0)),
                   pl.BlockSpec((1, FF_CHUNK, D_MODEL), lambda c, k: (c, 0, 0))],
        out_shape=[SDS((N_SHARD, D_MODEL, FF_CHUNK), f32), SDS((N_SHARD, FF_CHUNK, D_MODEL), f32)],
        operands=[hn2, d_up, up, dh], sem=("parallel", "arbitrary"), vmem=VMEM_LIMIT)


def outproj_bwd(dh1, ya, yc, yl, w_out):
    s = dh1.shape[0]
    tm = _tile(s)

    def body(dh_ref, ya_ref, yc_ref, yl_ref, w_ref, dy_ref, dw_ref):
        i = pl.program_id(0)

        @pl.when(i == 0)
        def _():
            dw_ref[...] = jnp.zeros_like(dw_ref)

        dhb = dh_ref[...].astype(bf16)
        dy_ref[...] = _dot_nt(dhb, w_ref[...])
        y = jnp.concatenate([ya_ref[...], yc_ref[...], yl_ref[...]], axis=1)
        dw_ref[...] += _dot_tn(y, dhb)

    return pl.pallas_call(
        body, name="outproj_bwd", grid=(s // tm,),
        in_specs=[pl.BlockSpec((tm, D_MODEL), lambda i: (i, 0)), pl.BlockSpec((tm, ATTN_W), lambda i: (i, 0)),
                  pl.BlockSpec((tm, CONV_W), lambda i: (i, 0)), pl.BlockSpec((tm, LRU_W), lambda i: (i, 0)),
                  _full((D_MODEL, D_MODEL))],
        out_specs=[pl.BlockSpec((tm, D_MODEL), lambda i: (i, 0)), _full((D_MODEL, D_MODEL))],
        out_shape=[SDS((s, D_MODEL), f32), SDS((D_MODEL, D_MODEL), f32)],
        compiler_params=_params(("arbitrary",), VMEM_LIMIT),
    )(dh1, ya, yc, yl, w_out)


def attn_bwd(z, o, dy, sink_col, bias, g_a, pieces=None):
    s = z.shape[0]
    per = min(ATT_BLOCKS_PER_STEP, s // BLOCK)
    tq = per * BLOCK
    nt = s // tq

    def body(q_ref, kv_ref, kvp_ref, o_ref, dy_ref, sk_ref, b_ref, g_ref, dq_ref, dkv_ref, dsk_ref, dg_ref,
             carry_ref, dsk_acc):
        i = pl.program_id(0)
        t = nt - 1 - i

        @pl.when(i == 0)
        def _():
            carry_ref[...] = jnp.zeros_like(carry_ref)
            dsk_acc[...] = jnp.zeros_like(dsk_acc)
            dg_ref[...] = jnp.zeros_like(dg_ref)

        mlo = lax.broadcasted_iota(jnp.int32, (BLOCK, LANES), 1) < HEAD_DIM
        lane = lax.broadcasted_iota(jnp.int32, (2 * BLOCK, LANES), 1)
        scale = HEAD_DIM ** -0.5
        half = ATT_ROWS // 2
        g = g_ref[...]
        bands = []
        for b in range(per):
            rows = slice(b * BLOCK, (b + 1) * BLOCK)
            kvp = kvp_ref[...] if b == 0 else kv_ref[(b - 1) * BLOCK:b * BLOCK, :]
            bias_b = b_ref[jnp.minimum(t, 1)] if b == 0 else b_ref[1]
            kx, vx = _attn_band(kv_ref[rows, :], kvp)
            q4 = _stack_heads(q_ref[rows, :] * scale, mlo)
            o_f = o_ref[rows, :].astype(f32)
            _, r = _rms_fwd(o_f, g)
            d_o, dg = _rms_bwd(dy_ref[rows, :], o_f, r, g)
            dg_ref[...] += dg
            do4 = _stack_heads(d_o.astype(bf16), mlo)
            probs = _attn_probs(q4, kx, bias_b, sk_ref[...])
            dq4, tk, tv = [], [], []
            for hk in range(2):
                pr, p_sink = probs[hk]
                d_p = _dot_nt(vx[hk], do4[hk])
                d_row = jnp.sum(pr * d_p, axis=0, keepdims=True)
                d_s = (pr * (d_p - d_row)).astype(bf16)
                dsk_acc[:, hk * half:(hk + 1) * half] -= p_sink * d_row
                dq4.append(_dot_tn(d_s, kx[hk]))
                tk.append(_dot(d_s, q4[hk]))
                tv.append(_dot(pr.astype(bf16), do4[hk]))
            dq_ref[rows, :] = (_unstack_heads(dq4, mlo) * scale).astype(bf16)
            fk = [x + pltpu.roll(x, HEAD_DIM, 1) for x in tk]
            fv = [x + pltpu.roll(x, HEAD_DIM, 1) for x in tv]
            bands.append(jnp.concatenate([jnp.where(lane < HEAD_DIM, fk[0], fk[1]),
                                          jnp.where(lane < HEAD_DIM, fv[0], fv[1])], axis=1))
        for b in range(per):
            after = bands[b + 1][:BLOCK] if b + 1 < per else carry_ref[...]
            dkv_ref[b * BLOCK:(b + 1) * BLOCK, :] = (bands[b][BLOCK:] + after).astype(bf16)
        carry_ref[...] = bands[0][:BLOCK]

        @pl.when(i == nt - 1)
        def _():
            for hh in range(N_HEADS):
                tot = jnp.sum(dsk_acc[:, hh * BLOCK:(hh + 1) * BLOCK], axis=1, keepdims=True)
                dsk_ref[hh:hh + 1, :] = jnp.broadcast_to(tot, (1, LANES))

    def rev(width, col):
        return pl.BlockSpec((tq, width), lambda i: (nt - 1 - i, col))

    return _pcall(
        body, pieces, name="attn_bwd", grid=(nt,),
        in_specs=[rev(ATTN_W, 0), rev(2 * KV_W, 2),
                  pl.BlockSpec((BLOCK, 2 * KV_W), lambda i: (jnp.maximum((nt - 1 - i) * per - 1, 0), 2)),
                  rev(ATTN_W, 0), rev(ATTN_W, 0),
                  _full((1, ATT_ROWS)), _full((2, 2 * BLOCK, ATT_ROWS // 2)), _full((1, ATTN_W))],
        out_specs=[rev(ATTN_W, 0), rev(2 * KV_W, 0), _full((N_HEADS, LANES)), _full((1, ATTN_W))],
        out_shape=[SDS((s, ATTN_W), bf16), SDS((s, 2 * KV_W), bf16), SDS((N_HEADS, LANES), f32), SDS((1, ATTN_W), f32)],
        scratch_shapes=[pltpu.VMEM((BLOCK, 2 * KV_W), f32), pltpu.VMEM((1, ATT_ROWS), f32)],
        operands=[z, z, z, o, dy, sink_col, bias, g_a], sem=("arbitrary",))


_V_GC, _V_LNG, _V_LNB, _V_CB, _V_GL, _V_BA, _V_BX, _V_LAM, _V_LB = range(9)
_V_ROWS = 16


def branch_bwd_a(z, conv, hst, dy, p, pieces=None):
    s = z.shape[0]
    tm = _tile(s)
    nt = s // tm
    hb = tm // HALO
    h8 = tm // 8

    def body(conv_ref, dyc_ref, dyl_ref, rx_ref, rxh_ref, rg_ref, hst_ref, hsth_ref,
             lng_ref, lnb_ref, lw_ref, lb_ref, wa_ref, ba_ref, wx_ref, bx_ref, lam_ref, gc_ref, gl_ref,
             dconv_ref, dxc_ref, drg_ref, vec_ref, dwa_ref, dwx_ref, carry_ref):
        i = pl.program_id(0)
        ti = nt - 1 - i

        @pl.when(i == 0)
        def _():
            carry_ref[...] = jnp.zeros_like(carry_ref)
            vec_ref[...] = jnp.zeros_like(vec_ref)
            dwa_ref[...] = jnp.zeros_like(dwa_ref)
            dwx_ref[...] = jnp.zeros_like(dwx_ref)

        conv = conv_ref[...]
        mu = jnp.mean(conv, axis=-1, keepdims=True)
        xm = conv - mu
        rstd = lax.rsqrt(jnp.mean(xm * xm, axis=-1, keepdims=True) + LN_EPS)
        xhat = xm * rstd
        lng = lng_ref[...]
        ln = xhat * lng + lnb_ref[...]
        sg = _sigmoid(ln)
        yc = ln * sg
        gc = gc_ref[...]
        _, rc = _rms_fwd(yc, gc)
        d_yc, d_gc = _rms_bwd(dyc_ref[...], yc, rc, gc)
        d_ln = d_yc * (sg * (1.0 + ln * (1.0 - sg)))
        d_xhat = d_ln * lng
        d_conv = rstd * (d_xhat - jnp.mean(d_xhat, axis=-1, keepdims=True)
                         - xhat * jnp.mean(d_xhat * xhat, axis=-1, keepdims=True))
        dconv_ref[...] = d_conv

        rx = rx_ref[...].astype(f32)
        hrx = jnp.where(ti == 0, 0.0, rxh_ref[...].astype(f32))
        xc = _conv_taps(jnp.concatenate([hrx, rx], axis=0), lw_ref[...], LRU_K) + lb_ref[...]
        wa = wa_ref[...]
        wx = wx_ref[...]
        lam = lam_ref[...]
        r, ig, sp, la, a, mult = _lru_gates(xc, wa, ba_ref[...], wx, bx_ref[...], lam)
        hs = hst_ref[...]
        row = lax.broadcasted_iota(jnp.int32, hs.shape, 0)
        h_before = jnp.where(ti == 0, 0.0, hsth_ref[7:8, :])
        h_prev = jnp.where(row == 0, h_before, pltpu.roll(hs, 1, 0))
        rg = rg_ref[...].astype(f32)
        gl, tg = _gelu(rg)
        out = hs * gl
        gmix = gl_ref[...]
        _, rl = _rms_fwd(out, gmix)
        d_out, d_gl = _rms_bwd(dyl_ref[...], out, rl, gmix)
        drg_ref[...] = (d_out * hs * _gelu_grad(rg, tg)).astype(bf16)
        d_h = d_out * gl
        last = row == tm - 1
        a_next = jnp.where(last, 1.0, pltpu.roll(a, tm - 1, 0))
        lmb = _scan_bwd(a_next, d_h, carry_ref[0:1, :])
        carry_ref[...] = jnp.broadcast_to(a[0:1, :] * lmb[0:1, :], carry_ref.shape)
        d_a = lmb * h_prev
        d_mult = lmb * (ig * xc)
        d_ig = lmb * (mult * xc)
        d_la = d_a * a - d_mult * (a * a) / jnp.maximum(mult, 1e-30)
        d_pa = (d_la * (-LRU_C * sp)) * (r * (1.0 - r))
        d_px = d_ig * (ig * (1.0 - ig))
        d_pab = d_pa.astype(bf16)
        d_pxb = d_px.astype(bf16)
        d_xc = lmb * (mult * ig) + _dot_nt(d_pab, wa) + _dot_nt(d_pxb, wx)
        dxc_ref[...] = d_xc
        xcb = xc.astype(bf16)
        dwa_ref[...] += _dot_tn(xcb, d_pab)
        dwx_ref[...] += _dot_tn(xcb, d_pxb)
        d_lam = jnp.sum(d_la * (-LRU_C * r), axis=0, keepdims=True) * (-_sigmoid(-lam))

        def colsum(v):
            return jnp.sum(v, axis=0, keepdims=True)

        rows = [None] * _V_ROWS
        rows[_V_GC] = d_gc
        rows[_V_LNG] = colsum(d_ln * xhat)
        rows[_V_LNB] = colsum(d_ln)
        rows[_V_CB] = colsum(d_conv)
        rows[_V_GL] = d_gl
        rows[_V_BA] = colsum(d_pa)
        rows[_V_BX] = colsum(d_px)
        rows[_V_LAM] = d_lam
        rows[_V_LB] = colsum(d_xc)
        zero = jnp.zeros((1, CONV_W), f32)
        vec_ref[...] += jnp.concatenate([zero if v is None else v for v in rows], axis=0)

    def rev(c):
        return pl.BlockSpec((tm, CONV_W), lambda i: (nt - 1 - i, c))

    small = [p["lng"], p["lnb"], p["lw"], p["lb"], p["wa"], p["ba"], p["wx"], p["bx"], p["lam"], p["gc"], p["gl"]]
    return _pcall(
        body, pieces, name="branch_bwd_a", grid=(nt,),
        in_specs=[rev(0), rev(2), rev(3), rev(5),
                  pl.BlockSpec((HALO, CONV_W), lambda i: (jnp.maximum((nt - 1 - i) * hb - 1, 0), 5)),
                  rev(6), rev(0),
                  pl.BlockSpec((8, LRU_W), lambda i: (jnp.maximum((nt - 1 - i) * h8 - 1, 0), 0))]
                 + [_full(a.shape) for a in small],
        out_specs=[rev(0), rev(0), rev(0), _full((_V_ROWS, CONV_W)), _full((LRU_W, LRU_W)), _full((LRU_W, LRU_W))],
        out_shape=[SDS((s, CONV_W), f32), SDS((s, LRU_W), f32), SDS((s, LRU_W), bf16), SDS((_V_ROWS, CONV_W), f32),
                   SDS((LRU_W, LRU_W), f32), SDS((LRU_W, LRU_W), f32)],
        scratch_shapes=[pltpu.VMEM((8, LRU_W), f32)],
        operands=[conv, dy, dy, z, z, z, hst, hst, *small], sem=("arbitrary",))


def branch_bwd_b(z, d_conv, d_xc, p, pieces=None):
    s = z.shape[0]
    tm = _tile(s)
    nt = s // tm
    hb = tm // HALO

    def body(cv_ref, cg_ref, cvh_ref, cgh_ref, rx_ref, rxh_ref, dc_ref, dch_ref, dx_ref, dxh_ref, cw_ref, lw_ref,
             dzc_ref, dzr_ref, dcw_ref, dlw_ref):
        i = pl.program_id(0)

        @pl.when(i == 0)
        def _():
            dcw_ref[...] = jnp.zeros_like(dcw_ref)
            dlw_ref[...] = jnp.zeros_like(dlw_ref)

        first = i == 0
        last = i == nt - 1
        cval = cv_ref[...].astype(f32)
        sg = _sigmoid(cg_ref[...].astype(f32))
        u = cval * sg
        hu = jnp.where(first, 0.0, cvh_ref[...].astype(f32) * _sigmoid(cgh_ref[...].astype(f32)))
        dpad = jnp.concatenate([dc_ref[...], jnp.where(last, 0.0, dch_ref[...])], axis=0)
        d_u, dw_rows = _conv_taps_bwd(dpad, jnp.concatenate([hu, u], axis=0), cw_ref[...], CONV_K, tm)
        dcw_ref[...] += jnp.concatenate(dw_rows + [jnp.zeros((HALO - CONV_K, CONV_W), f32)], axis=0)
        dzc_ref[...] = jnp.concatenate([d_u * sg, d_u * cval * sg * (1.0 - sg)], axis=1).astype(bf16)

        rx = rx_ref[...].astype(f32)
        hrx = jnp.where(first, 0.0, rxh_ref[...].astype(f32))
        dxpad = jnp.concatenate([dx_ref[...], jnp.where(last, 0.0, dxh_ref[...])], axis=0)
        d_rx, dlw_rows = _conv_taps_bwd(dxpad, jnp.concatenate([hrx, rx], axis=0), lw_ref[...], LRU_K, tm)
        dlw_ref[...] += jnp.concatenate(dlw_rows + [jnp.zeros((8 - LRU_K, LRU_W), f32)], axis=0)
        dzr_ref[...] = d_rx.astype(bf16)

    def col(c):
        return pl.BlockSpec((tm, CONV_W), lambda i: (i, c))

    def prev(c):
        return pl.BlockSpec((HALO, CONV_W), lambda i: (jnp.maximum(i * hb - 1, 0), c))

    nxt = pl.BlockSpec((HALO, CONV_W), lambda i: (jnp.minimum((i + 1) * hb, nt * hb - 1), 0))
    return _pcall(
        body, pieces, name="branch_bwd_b", grid=(nt,),
        in_specs=[col(3), col(4), prev(3), prev(4), col(5), prev(5), col(0), nxt, col(0), nxt,
                  _full(p["cw"].shape), _full(p["lw"].shape)],
        out_specs=[pl.BlockSpec((tm, 2 * CONV_W), lambda i: (i, 0)), pl.BlockSpec((tm, LRU_W), lambda i: (i, 0)),
                   _full((HALO, CONV_W)), _full((8, LRU_W))],
        out_shape=[SDS((s, 2 * CONV_W), bf16), SDS((s, LRU_W), bf16), SDS((HALO, CONV_W), f32), SDS((8, LRU_W), f32)],
        operands=[z, z, z, z, z, z, d_conv, d_conv, d_xc, d_xc, p["cw"], p["lw"]], sem=("arbitrary",))


def inproj_bwd(dq, dkv, dzc, dzr, drg, h, g1, w_in_t, dh1, pieces=None):
    s = h.shape[0]
    tm = _tile(s)

    def body(dq_ref, dkv_ref, dzc_ref, dzr_ref, drg_ref, h_ref, g_ref, w_ref, dh1_ref, dh_ref, dw_ref, dg_ref):
        i = pl.program_id(0)

        @pl.when(i == 0)
        def _():
            dw_ref[...] = jnp.zeros_like(dw_ref)
            dg_ref[...] = jnp.zeros_like(dg_ref)

        dz = jnp.concatenate([dq_ref[...], dkv_ref[...], dzc_ref[...], dzr_ref[...], drg_ref[...]], axis=1)
        x = h_ref[...]
        g = g_ref[...]
        hn, r = _rms_fwd(x, g)
        d_hn = _dot(dz, w_ref[...])
        dw_ref[...] += _dot_tn(dz, hn.astype(bf16))
        dx, dg = _rms_bwd(d_hn, x, r, g)
        dh_ref[...] = dh1_ref[...] + dx
        dg_ref[...] += dg

    def rowb(w):
        return pl.BlockSpec((tm, w), lambda i: (i, 0))

    return _pcall(
        body, pieces, name="inproj_bwd", grid=(s // tm,),
        in_specs=[rowb(ATTN_W), rowb(2 * KV_W), rowb(2 * CONV_W), rowb(LRU_W), rowb(LRU_W), rowb(D_MODEL),
                  _full((1, D_MODEL)), _full((IN_W, D_MODEL)), rowb(D_MODEL)],
        out_specs=[rowb(D_MODEL), _full((IN_W, D_MODEL)), _full((1, D_MODEL))],
        out_shape=[SDS((s, D_MODEL), f32), SDS((IN_W, D_MODEL), f32), SDS((1, D_MODEL), f32)],
        operands=[dq, dkv, dzc, dzr, drg, h, g1, w_in_t, dh1], sem=("arbitrary",), vmem=VMEM_LIMIT)


def _block_diag(w):
    out = jnp.zeros((LRU_W, LRU_W), w.dtype)
    hd = LRU_W // LRU_HEADS
    for hh in range(LRU_HEADS):
        out = out.at[hh * hd:(hh + 1) * hd, hh * hd:(hh + 1) * hd].set(w[hh])
    return out


def _diag_blocks(w):
    hd = LRU_W // LRU_HEADS
    return jnp.stack([w[hh * hd:(hh + 1) * hd, hh * hd:(hh + 1) * hd] for hh in range(LRU_HEADS)])


def _layer_params(sp, l):
    mix = sp["mix_norm"][l]
    return dict(
        g1=sp["norm1"][l][None, :], g2=sp["norm2"][l][None, :],
        sinks=jnp.repeat(sp["attn_sinks"][l], BLOCK)[None, :],
        ga=mix[None, :ATTN_W], gc=mix[None, ATTN_W:ATTN_W + CONV_W], gl=mix[None, ATTN_W + CONV_W:],
        cw=jnp.pad(sp["conv_dw_w"][l], ((0, HALO - CONV_K), (0, 0))), cb=sp["conv_dw_b"][l][None, :],
        lng=sp["conv_ln_g"][l][None, :], lnb=sp["conv_ln_b"][l][None, :],
        lw=jnp.pad(sp["lru_conv_w"][l], ((0, 8 - LRU_K), (0, 0))), lb=sp["lru_conv_b"][l][None, :],
        wa=_block_diag(sp["lru_wa"][l]).astype(bf16), ba=sp["lru_ba"][l].reshape(1, LRU_W),
        wx=_block_diag(sp["lru_wx"][l]).astype(bf16), bx=sp["lru_bx"][l].reshape(1, LRU_W),
        lam=sp["lru_lambda"][l][None, :],
    )


def local_step(x, tgt, big, sp):
    return train_local(x, tgt, sp, LocalWeights(big))


class LocalWeights:
    def __init__(self, big):
        self.big = big
        self.grads = [dict() for _ in range(DEPTH)]

    def weight(self, name, l):
        return self.big[l][name]

    def host(self, point, l):
        return None

    def grad(self, name, l, g):
        self.grads[l][name] = g

    def big_grads(self):
        return self.grads


def train_local(x, tgt, sp, cs):
    lp = [_layer_params(sp, l) for l in range(DEPTH)]
    bias = _attn_bias()
    saved = []
    h = x
    for l in range(DEPTH):
        p = lp[l]
        z = inproj_fwd(h, p["g1"], cs.weight("w_in", l), cs.host("inproj_fwd", l))
        o, ya = attn_fwd(z, p["sinks"], bias, p["ga"], cs.host("attn_fwd", l))
        conv, hst, yc, yl = branch_fwd(z, p, cs.host("branch_fwd", l))
        if l < DEPTH - 1:
            h1, hn2 = outproj_fwd(ya, yc, yl, h, cs.weight("w_out", l), p["g2"], cs.host("outproj_fwd", l))
            up, h_next = mlp_fwd(hn2, h1, cs.weight("w_up", l), cs.weight("w_dn", l), cs.host("mlp_fwd", l))
        else:
            h1, hn2, up, dh, loss, d_gf = last_layer_fwd(
                ya, yc, yl, h, cs.weight("w_out", l), p["g2"], cs.weight("w_up", l), cs.weight("w_dn", l),
                tgt, sp["final_norm"][None, :])
            h_next = None
        saved.append(dict(h=h, z=z, o=o, ya=ya, conv=conv, hst=hst, yc=yc, yl=yl, h1=h1, hn2=hn2, up=up))
        h = h_next
    small_g = [None] * DEPTH
    for l in reversed(range(DEPTH)):
        p, sv = lp[l], saved[l]
        w_up, w_dn = cs.weight("w_up", l), cs.weight("w_dn", l)
        d_up, dh1, d_g2 = mlp_bwd_act(dh, sv["up"], sv["h1"], p["g2"], w_up, w_dn, cs.host("mlp_bwd_act", l))
        dw_up, dw_dn = mlp_bwd_w(sv["hn2"], d_up, sv["up"], dh, cs.host("mlp_bwd_w", l))
        cs.grad("w_up", l, dw_up)
        cs.grad("w_dn", l, dw_dn)
        dy, dw_out = outproj_bwd(dh1, sv["ya"], sv["yc"], sv["yl"], cs.weight("w_out", l))
        cs.grad("w_out", l, dw_out)
        dq, dkv, d_sk, d_ga = attn_bwd(sv["z"], sv["o"], dy, p["sinks"], bias, p["ga"], cs.host("attn_bwd", l))
        d_conv, d_xc, d_rg, vec, dwa, dwx = branch_bwd_a(sv["z"], sv["conv"], sv["hst"], dy, p, cs.host("branch_bwd_a", l))
        dzc, dzr, dcw, dlw = branch_bwd_b(sv["z"], d_conv, d_xc, p, cs.host("branch_bwd_b", l))
        dh, dw_in, d_g1 = inproj_bwd(dq, dkv, dzc, dzr, d_rg, sv["h"], p["g1"], cs.weight("w_in", l), dh1,
                                     cs.host("inproj_bwd", l))
        cs.grad("w_in", l, dw_in)
        hd = LRU_W // LRU_HEADS
        small_g[l] = dict(
            norm1=d_g1[0], attn_sinks=d_sk[:, 0], conv_dw_w=dcw[:CONV_K], conv_dw_b=vec[_V_CB],
            conv_ln_g=vec[_V_LNG], conv_ln_b=vec[_V_LNB], lru_conv_w=dlw[:LRU_K], lru_conv_b=vec[_V_LB],
            lru_wa=_diag_blocks(dwa), lru_ba=vec[_V_BA].reshape(LRU_HEADS, hd),
            lru_wx=_diag_blocks(dwx), lru_bx=vec[_V_BX].reshape(LRU_HEADS, hd), lru_lambda=vec[_V_LAM],
            mix_norm=jnp.concatenate([d_ga[0], vec[_V_GC], vec[_V_GL]]), norm2=d_g2[0],
        )
    return loss, dh, cs.big_grads(), small_g, d_gf[0]


_HBM = pl.BlockSpec(memory_space=pl.ANY)


def _place():
    x, y, c = lax.axis_index("x"), lax.axis_index("y"), lax.axis_index("c")
    chips = [(1 - x, y), (x, 1 - y), (1 - x, 1 - y)]
    return x, y, c, chips


class Comm:
    def __init__(self, ins, out_shape, aliases, sems, start, finish, done):
        self.ins, self.out_shape, self.aliases, self.sems = list(ins), list(out_shape), dict(aliases), list(sems)
        self.start, self.finish, self.done = start, finish, done


def _pcall(body, pieces, *, name, grid, in_specs, out_specs, out_shape, operands, scratch_shapes=(), sem, vmem=None):
    in_specs, out_specs, out_shape, scratch_shapes = list(in_specs), list(out_specs), list(out_shape), list(scratch_shapes)
    if not pieces:
        return pl.pallas_call(body, name=name, grid=grid, in_specs=in_specs, out_specs=out_specs, out_shape=out_shape,
                              scratch_shapes=scratch_shapes, compiler_params=_params(sem, vmem))(*operands)
    n_in, n_out, n_scr = len(in_specs), len(out_specs), len(scratch_shapes)
    c_ins = [a for p in pieces for a in p.ins]
    c_outs = [s for p in pieces for s in p.out_shape]
    c_sems = [n for p in pieces for n in p.sems]
    aliases, spans, ki, ko, ks = {}, [], 0, 0, 0
    for p in pieces:
        spans.append((ki, ko, ks))
        for a, b in p.aliases.items():
            aliases[n_in + ki + a] = n_out + ko + b
        ki, ko, ks = ki + len(p.ins), ko + len(p.out_shape), ks + len(p.sems)

    def hosted(*refs):
        ins, cin = refs[:n_in], refs[n_in:n_in + ki]
        outs, cout = refs[n_in + ki:n_in + ki + n_out], refs[n_in + ki + n_out:n_in + ki + n_out + ko]
        scr, csem = refs[n_in + ki + n_out + ko:n_in + ki + n_out + ko + n_scr], refs[n_in + ki + n_out + ko + n_scr:]
        first = functools.reduce(jnp.logical_and, [pl.program_id(d) == 0 for d in range(len(grid))])
        last = functools.reduce(jnp.logical_and, [pl.program_id(d) == grid[d] - 1 for d in range(len(grid))])

        def each(which):
            for p, (a, b, s) in zip(pieces, spans):
                getattr(p, which)(cin[a:a + len(p.ins)], cout[b:b + len(p.out_shape)], csem[s:s + len(p.sems)])

        @pl.when(first)
        def _():
            each("start")

        body(*ins, *outs, *scr)

        @pl.when(last)
        def _():
            each("finish")

    res = pl.pallas_call(
        hosted, name=name + "_host", grid=grid, in_specs=in_specs + [_HBM] * ki, out_specs=out_specs + [_HBM] * ko,
        out_shape=out_shape + c_outs, scratch_shapes=scratch_shapes + [pltpu.SemaphoreType.DMA((n,)) for n in c_sems],
        input_output_aliases=aliases, compiler_params=_params(("arbitrary",) * len(grid), vmem),
    )(*operands, *c_ins)
    for p, (a, b, s) in zip(pieces, spans):
        p.done(res[n_out + b:n_out + b + len(p.out_shape)])
    return res[:n_out]


def standalone(pieces, name):
    ki = sum(len(p.ins) for p in pieces)
    ko = sum(len(p.out_shape) for p in pieces)
    spans, a, b, s = [], 0, 0, 0
    aliases = {}
    for p in pieces:
        spans.append((a, b, s))
        for i, o in p.aliases.items():
            aliases[a + i] = b + o
        a, b, s = a + len(p.ins), b + len(p.out_shape), s + len(p.sems)

    def body(*refs):
        cin, cout, csem = refs[:ki], refs[ki:ki + ko], refs[ki + ko:]
        for which in ("start", "finish"):
            for p, (a, b, s) in zip(pieces, spans):
                getattr(p, which)(cin[a:a + len(p.ins)], cout[b:b + len(p.out_shape)], csem[s:s + len(p.sems)])

    res = pl.pallas_call(
        body, name=name, in_specs=[_HBM] * ki, out_specs=[_HBM] * ko, out_shape=[s for p in pieces for s in p.out_shape],
        scratch_shapes=[pltpu.SemaphoreType.DMA((n,)) for p in pieces for n in p.sems], input_output_aliases=aliases,
    )(*[a for p in pieces for a in p.ins])
    for p, (a, b, s) in zip(pieces, spans):
        p.done(res[b:b + len(p.out_shape)])


def _rows_half(ref, which, rows):
    return ref.at[pl.ds(pl.multiple_of(which * rows, 8), rows)]


def gather_ici_piece(bufs, done):
    n = len(bufs)

    def copies(cout):
        x, y, c, chips = _place()
        out = []
        for j, (cx, cy) in enumerate(chips):
            for w in range(n):
                half = bufs[w].shape[1] // 2
                out.append((j, w, _rows_half(cout[w].at[2 * x + y], c, half), _rows_half(cout[w].at[2 * cx + cy], c, half),
                            (cx, cy, c)))
        return out

    def start(cin, cout, sems):
        for j, w, mine, _, to in copies(cout):
            pltpu.make_async_remote_copy(src_ref=mine, dst_ref=mine, send_sem=sems[0].at[n * j + w],
                                         recv_sem=sems[1].at[n * j + w], device_id=to, device_id_type=MESH).start()

    def finish(cin, cout, sems):
        for j, w, mine, landed, to in copies(cout):
            pltpu.make_async_remote_copy(src_ref=mine, dst_ref=landed, send_sem=sems[0].at[n * j + w],
                                         recv_sem=sems[1].at[n * j + w], device_id=to, device_id_type=MESH).wait()

    return Comm(bufs, [SDS(b.shape, b.dtype) for b in bufs], {w: w for w in range(n)}, [3 * n, 3 * n], start, finish, done)


def gather_full_piece(bufs, done):
    n = len(bufs)

    def copies(cout):
        x, y, c, chips = _place()
        return [(n * j + w, cout[w].at[2 * x + y], cout[w].at[2 * cx + cy], (cx, cy, c))
                for j, (cx, cy) in enumerate(chips) for w in range(n)]

    def start(cin, cout, sems):
        for k, mine, _, to in copies(cout):
            pltpu.make_async_remote_copy(src_ref=mine, dst_ref=mine, send_sem=sems[0].at[k], recv_sem=sems[1].at[k],
                                         device_id=to, device_id_type=MESH).start()

    def finish(cin, cout, sems):
        for k, mine, landed, to in copies(cout):
            pltpu.make_async_remote_copy(src_ref=mine, dst_ref=landed, send_sem=sems[0].at[k], recv_sem=sems[1].at[k],
                                         device_id=to, device_id_type=MESH).wait()

    return Comm(bufs, [SDS(b.shape, b.dtype) for b in bufs], {w: w for w in range(n)}, [3 * n, 3 * n], start, finish, done)


def gather_d2d_piece(bufs, done):
    n = len(bufs)

    def copies(cout):
        x, y, c, chips = _place()
        out = []
        for j, (cx, cy) in enumerate(chips):
            for w in range(n):
                half = bufs[w].shape[1] // 2
                slot = cout[w].at[2 * cx + cy]
                out.append((n * j + w, _rows_half(slot, c, half), _rows_half(slot, 1 - c, half), (x, y, 1 - c)))
        return out

    def start(cin, cout, sems):
        for k, mine, _, to in copies(cout):
            pltpu.make_async_remote_copy(src_ref=mine, dst_ref=mine, send_sem=sems[0].at[k], recv_sem=sems[1].at[k],
                                         device_id=to, device_id_type=MESH).start()

    def finish(cin, cout, sems):
        for k, mine, theirs, to in copies(cout):
            pltpu.make_async_remote_copy(src_ref=mine, dst_ref=theirs, send_sem=sems[0].at[k], recv_sem=sems[1].at[k],
                                         device_id=to, device_id_type=MESH).wait()

    return Comm(bufs, [SDS(b.shape, b.dtype) for b in bufs], {w: w for w in range(n)}, [3 * n, 3 * n], start, finish, done)


def pair_piece(parts, done):
    n = len(parts)

    def copies(cin, cout):
        x, y, c, _ = _place()
        out = []
        for w in range(n):
            half = parts[w].shape[1] // 2
            out.append((w, cin[w].at[:, pl.ds(pl.multiple_of((1 - c) * half, 8), half), :], cout[w], (x, y, 1 - c)))
        return out

    def start(cin, cout, sems):
        for w, src, dst, to in copies(cin, cout):
            pltpu.make_async_remote_copy(src_ref=src, dst_ref=dst, send_sem=sems[0].at[w], recv_sem=sems[1].at[w],
                                         device_id=to, device_id_type=MESH).start()

    def finish(cin, cout, sems):
        for w, src, dst, to in copies(cin, cout):
            pltpu.make_async_remote_copy(src_ref=src, dst_ref=dst, send_sem=sems[0].at[w], recv_sem=sems[1].at[w],
                                         device_id=to, device_id_type=MESH).wait()

    return Comm(parts, [SDS((N_SHARD, a.shape[1] // 2, a.shape[2]), f32) for a in parts], {}, [n, n], start, finish, done)


def shard_piece(sums16, done):
    n = len(sums16)

    def copies(cin, cout):
        x, y, c, chips = _place()
        return [(n * j + w, cin[w].at[2 * cx + cy], cout[w].at[j], (cx, cy, c))
                for j, (cx, cy) in enumerate(chips) for w in range(n)]

    def start(cin, cout, sems):
        for k, src, dst, to in copies(cin, cout):
            pltpu.make_async_remote_copy(src_ref=src, dst_ref=dst, send_sem=sems[0].at[k], recv_sem=sems[1].at[k],
                                         device_id=to, device_id_type=MESH).start()

    def finish(cin, cout, sems):
        for k, src, dst, to in copies(cin, cout):
            pltpu.make_async_remote_copy(src_ref=src, dst_ref=dst, send_sem=sems[0].at[k], recv_sem=sems[1].at[k],
                                         device_id=to, device_id_type=MESH).wait()

    return Comm(sums16, [SDS((3,) + a.shape[1:], bf16) for a in sums16], {}, [3 * n, 3 * n], start, finish, done)


def place_shard(a, layer, idx, dtype):
    _, r, cdim = a.shape
    tr = min(r, 512)

    def body(idx_ref, a_ref, o_ref):
        o_ref[0] = a_ref[0].astype(dtype)

    return pl.pallas_call(
        body, name="place_shard",
        grid_spec=pltpu.PrefetchScalarGridSpec(
            num_scalar_prefetch=1, grid=(r // tr,),
            in_specs=[pl.BlockSpec((1, tr, cdim), lambda i, idx_ref: (layer, i, 0))],
            out_specs=pl.BlockSpec((1, tr, cdim), lambda i, idx_ref: (idx_ref[1], i, 0))),
        out_shape=SDS((N_SHARD, r, cdim), dtype),
        compiler_params=_params(("arbitrary",)),
    )(idx, a)


_KEYS = ("w_in", "w_out", "w_up", "w_dn")


class MeshWeights:
    def __init__(self, w_in, w_out, w_up, w_down, conv_dw_w, lru_conv_w, idx):
        self.idx = idx
        src = dict(w_in=w_in, w_out=w_out, w_up=w_up, w_dn=w_down)
        self.buf = {(n, l): place_shard(src[n], l, idx, bf16) for n in _KEYS for l in range(DEPTH)}
        self.conv = {(n, l): place_shard(a, l, idx, f32)
                     for n, a in (("cw", conv_dw_w), ("lw", lru_conv_w)) for l in range(DEPTH)}
        self.cache, self.parts, self.sum32, self.sum16, self.got = {}, {}, {}, {}, {}
        first, small = [("w_in", 0)], list(self.conv)

        def store_conv(outs):
            self.conv.update(zip(small, outs))

        standalone([self._gather(gather_ici_piece, first), gather_full_piece([self.conv[k] for k in small], store_conv)],
                   "gather_first_ici")
        standalone([self._gather(gather_d2d_piece, first)], "gather_first_d2d")

    def _gather(self, piece, keys):
        def done(outs):
            self.buf.update(zip(keys, outs))
        return piece([self.buf[k] for k in keys], done)

    def _pair(self, keys):
        def done(outs):
            for k, recv in zip(keys, outs):
                self.sum32[k], self.sum16[k] = chip_sum(self.parts[k], recv, self.idx)
        return pair_piece([self.parts[k] for k in keys], done)

    def _shard(self, keys):
        def done(outs):
            self.got.update(zip(keys, outs))
        return shard_piece([self.sum16[k] for k in keys], done)

    def conv_weights(self):
        out = []
        for n in ("cw", "lw"):
            a = jnp.stack([self.conv[(n, l)] for l in range(DEPTH)])
            out.append(a.transpose(0, 2, 1, 3).reshape(DEPTH, a.shape[2], N_SHARD * a.shape[3]))
        return out

    def weight(self, name, l):
        if (name, l) not in self.cache:
            b = self.buf[(name, l)]
            if name in ("w_in", "w_out"):
                b = b.reshape(-1, D_MODEL)
            self.cache[(name, l)] = b
        return self.cache[(name, l)]

    def host(self, point, l):
        ici, d2d = gather_ici_piece, gather_d2d_piece
        rest1 = [("w_out", 1), ("w_up", 1), ("w_dn", 1)]
        plan = {
            ("inproj_fwd", 0): lambda: [self._gather(ici, [("w_out", 0)])],
            ("attn_fwd", 0): lambda: [self._gather(ici, [("w_up", 0)]), self._gather(d2d, [("w_out", 0)])],
            ("branch_fwd", 0): lambda: [self._gather(ici, [("w_dn", 0)]), self._gather(d2d, [("w_up", 0)])],
            ("outproj_fwd", 0): lambda: [self._gather(d2d, [("w_dn", 0)]), self._gather(ici, [("w_in", 1)])],
            ("mlp_fwd", 0): lambda: [self._gather(ici, rest1), self._gather(d2d, [("w_in", 1)])],
            ("attn_fwd", 1): lambda: [self._gather(d2d, rest1)],
            ("attn_bwd", 1): lambda: [self._pair([("w_up", 1), ("w_dn", 1)])],
            ("inproj_bwd", 1): lambda: [self._pair([("w_out", 1)])],
            ("mlp_bwd_act", 0): lambda: [self._shard([("w_up", 1), ("w_dn", 1)])],
            ("mlp_bwd_w", 0): lambda: [self._pair([("w_in", 1)])],
            ("attn_bwd", 0): lambda: [self._shard([("w_in", 1), ("w_out", 1)]),
                                      self._pair([("w_up", 0), ("w_dn", 0), ("w_out", 0)])],
            ("branch_bwd_a", 0): lambda: [self._shard([("w_up", 0)])],
            ("branch_bwd_b", 0): lambda: [self._shard([("w_dn", 0), ("w_out", 0)])],
        }
        make = plan.get((point, l))
        return make() if make else None

    def grad(self, name, l, g):
        if name in ("w_in", "w_out"):
            g = g.reshape(N_SHARD, -1, D_MODEL)
        self.parts[(name, l)] = g

    def big_grads(self):
        last = [("w_in", 0)]
        standalone([self._pair(last)], "pair_last")
        standalone([self._shard(last)], "shard_last")
        tots = []
        for n in _KEYS:
            t = None
            for l in reversed(range(DEPTH)):
                t = shard_sum(self.sum32[(n, l)], self.got[(n, l)], self.idx, l, t)
            tots.append(t)
        return tots


def chip_sum(g, recv, idx):
    _, r, cdim = g.shape
    half = r // 2
    tr = min(half, 512)
    nh = half // tr

    def body(idx_ref, g_ref, r_ref, o32_ref, o16_ref):
        tot = g_ref[0] + r_ref[0]
        o16_ref[0] = tot.astype(bf16)

        @pl.when(pl.program_id(0) == N_SHARD - 1)
        def _():
            o32_ref[...] = tot

    def slab(k, idx_ref):
        return lax.rem(idx_ref[1] + 1 + k, N_SHARD)

    def o16_map(k, i, idx_ref):
        return (slab(k, idx_ref), i, 0)

    return pl.pallas_call(
        body, name="chip_sum",
        grid_spec=pltpu.PrefetchScalarGridSpec(
            num_scalar_prefetch=1, grid=(N_SHARD, nh),
            in_specs=[pl.BlockSpec((1, tr, cdim), lambda k, i, idx_ref: (slab(k, idx_ref), idx_ref[0] * nh + i, 0)),
                      pl.BlockSpec((1, tr, cdim), lambda k, i, idx_ref: (slab(k, idx_ref), i, 0))],
            out_specs=[pl.BlockSpec((tr, cdim), lambda k, i, idx_ref: (jnp.where(k == N_SHARD - 1, i, 0), 0)),
                       pl.BlockSpec((1, tr, cdim), o16_map)]),
        out_shape=[SDS((half, cdim), f32), SDS((N_SHARD, half, cdim), bf16)],
        compiler_params=_params(("arbitrary", "arbitrary")),
    )(idx, g, recv)


def shard_sum(sum32, got16, idx, layer, prev):
    half, cdim = sum32.shape
    tr = min(half, 512)

    def body(idx_ref, a_ref, r0_ref, r1_ref, r2_ref, *rest):
        o_ref = rest[-1]
        o_ref[0, 0] = ((a_ref[...] + r0_ref[0].astype(f32)) + r1_ref[0].astype(f32)) + r2_ref[0].astype(f32)

    def rel(j):
        return pl.BlockSpec((1, tr, cdim), lambda i, idx_ref: (j, i, 0))

    in_specs = [pl.BlockSpec((tr, cdim), lambda i, idx_ref: (i, 0)), rel(0), rel(1), rel(2)]
    operands = [idx, sum32, got16, got16, got16]
    aliases = {}
    if prev is not None:
        in_specs.append(_HBM)
        operands.append(prev)
        aliases = {5: 0}
    return pl.pallas_call(
        body, name="shard_sum",
        grid_spec=pltpu.PrefetchScalarGridSpec(
            num_scalar_prefetch=1, grid=(half // tr,), in_specs=in_specs,
            out_specs=pl.BlockSpec((1, 1, tr, cdim), lambda i, idx_ref: (layer, idx_ref[0], i, 0))),
        out_shape=SDS((DEPTH, 2, half, cdim), f32), input_output_aliases=aliases,
        compiler_params=_params(("arbitrary",)),
    )(*operands)


def halves_exchange(tots):
    nw = len(tots)

    def body(*refs):
        bufs = refs[nw:2 * nw]
        send_sem, recv_sem = refs[2 * nw:]
        x, y, c, _ = _place()

        def copy(w, l, half_idx):
            return pltpu.make_async_remote_copy(
                src_ref=bufs[w].at[l, half_idx], dst_ref=bufs[w].at[l, half_idx], send_sem=send_sem.at[DEPTH * w + l],
                recv_sem=recv_sem.at[DEPTH * w + l], device_id=(x, y, 1 - c), device_id_type=MESH)

        sends = [copy(w, l, c) for w in range(nw) for l in range(DEPTH)]
        for cp in sends:
            cp.start()
        for w in range(nw):
            for l in range(DEPTH):
                copy(w, l, 1 - c).wait_recv()
        for cp in sends:
            cp.wait_send()

    return pl.pallas_call(
        body, name="halves_exchange", in_specs=[_HBM] * nw, out_specs=[_HBM] * nw,
        out_shape=[SDS(a.shape, f32) for a in tots], input_output_aliases={w: w for w in range(nw)},
        scratch_shapes=[pltpu.SemaphoreType.DMA((DEPTH * nw,)), pltpu.SemaphoreType.DMA((DEPTH * nw,))],
    )(*tots)


N_DEV = 8


def small_allreduce(vec):
    r = vec.shape[0]

    def body(v_ref, o_ref, all_ref, send_sems, recv_sems):
        x, y, c, chips = _place()
        me, sib = (x, y, c), (x, y, 1 - c)

        def rows(px, py, pc):
            return all_ref.at[4 * px + 2 * py + pc]

        def copy(k, block, to, src=None):
            return pltpu.make_async_remote_copy(
                src_ref=rows(*block) if src is None else src, dst_ref=rows(*block), send_sem=send_sems.at[k],
                recv_sem=recv_sems.at[k], device_id=to, device_id_type=MESH)

        first = [copy(0, me, sib, src=v_ref)]
        first += [copy(1 + j, me, (*chip, c), src=v_ref) for j, chip in enumerate(chips)]
        for cp in first:
            cp.start()
        rows(*me)[...] = v_ref[...]
        passed = [copy(4 + j, (*chip, c), sib) for j, chip in enumerate(chips)]
        for j, chip in enumerate(chips):
            copy(1 + j, (*chip, c), me).wait_recv()
            passed[j].start()
        copy(0, sib, me).wait_recv()
        for j, chip in enumerate(chips):
            copy(4 + j, (*chip, 1 - c), me).wait_recv()
        for cp in first + passed:
            cp.wait_send()
        acc = all_ref[0]
        for d in range(1, N_DEV):
            acc = acc + all_ref[d]
        o_ref[...] = acc

    return pl.pallas_call(
        body, name="small_allreduce",
        in_specs=[pl.BlockSpec(memory_space=pltpu.VMEM)], out_specs=pl.BlockSpec(memory_space=pltpu.VMEM),
        out_shape=SDS((r, LANES), f32),
        scratch_shapes=[pltpu.VMEM((N_DEV, r, LANES), f32), pltpu.SemaphoreType.DMA((7,)), pltpu.SemaphoreType.DMA((7,))],
    )(vec)


def _adamw_math(w, g, m, v):
    m = ADAM_B1 * m + (1.0 - ADAM_B1) * g
    v = ADAM_B2 * v + (1.0 - ADAM_B2) * (g * g)
    m_hat = m / (1.0 - ADAM_B1 ** ADAM_STEP)
    v_hat = v / (1.0 - ADAM_B2 ** ADAM_STEP)
    delta = -ADAM_LR * (m_hat / (jnp.sqrt(v_hat) + ADAM_EPS) + ADAM_WD * w)
    return delta, m, v


def adamw_big(w, g, m, v):
    _, r, cdim = w.shape
    tr = _row_tile(r, 256)

    def body(w_ref, g_ref, m_ref, v_ref, go_ref, d_ref, mo_ref, vo_ref):
        g = g_ref[...]
        d, mm, vv = _adamw_math(w_ref[...], g, m_ref[...], v_ref[...])
        go_ref[...] = g
        d_ref[...] = d
        mo_ref[...] = mm
        vo_ref[...] = vv

    blk = pl.BlockSpec((1, tr, cdim), lambda l, i: (l, i, 0))
    return pl.pallas_call(
        body, name="adamw_big", grid=(DEPTH, r // tr), in_specs=[blk] * 4, out_specs=[blk] * 4,
        out_shape=[SDS(w.shape, f32)] * 4, compiler_params=_params(("parallel", "parallel")),
    )(w, g, m, v)


def adamw_small(ws, gs, ms, vs):
    n = len(ws)

    def body(*refs):
        w_r, g_r, m_r, v_r = refs[:n], refs[n:2 * n], refs[2 * n:3 * n], refs[3 * n:4 * n]
        d_o, m_o, v_o = refs[4 * n:5 * n], refs[5 * n:6 * n], refs[6 * n:7 * n]
        for k in range(n):
            d, mm, vv = _adamw_math(w_r[k][...], g_r[k][...], m_r[k][...], v_r[k][...])
            d_o[k][...] = d
            m_o[k][...] = mm
            v_o[k][...] = vv

    vm = pl.BlockSpec(memory_space=pltpu.VMEM)
    shapes = [SDS(a.shape, f32) for a in ws]
    outs = pl.pallas_call(
        body, name="adamw_small", in_specs=[vm] * (4 * n), out_specs=[vm] * (3 * n), out_shape=shapes * 3,
    )(*ws, *gs, *ms, *vs)
    return outs[:n], outs[n:2 * n], outs[2 * n:]


_BIG = ("w_in", "w_out", "w_up", "w_down")
_WEIGHTS = ("norm1", "w_in", "attn_sinks", "conv_dw_w", "conv_dw_b", "conv_ln_g", "conv_ln_b", "lru_conv_w", "lru_conv_b",
            "lru_wa", "lru_ba", "lru_wx", "lru_bx", "lru_lambda", "mix_norm", "w_out", "norm2", "w_up", "w_down", "final_norm")
_SMALL = tuple(n for n in _WEIGHTS if n not in _BIG)
_SMALL_FULL_SHAPE = dict(
    norm1=(DEPTH, D_MODEL), attn_sinks=(DEPTH, N_HEADS), conv_dw_w=(DEPTH, CONV_K, CONV_W), conv_dw_b=(DEPTH, CONV_W),
    conv_ln_g=(DEPTH, CONV_W), conv_ln_b=(DEPTH, CONV_W), lru_conv_w=(DEPTH, LRU_K, LRU_W), lru_conv_b=(DEPTH, LRU_W),
    lru_wa=(DEPTH, LRU_HEADS, 64, 64), lru_ba=(DEPTH, LRU_HEADS, 64), lru_wx=(DEPTH, LRU_HEADS, 64, 64),
    lru_bx=(DEPTH, LRU_HEADS, 64), lru_lambda=(DEPTH, LRU_W), mix_norm=(DEPTH, D_MODEL), norm2=(DEPTH, D_MODEL),
    final_norm=(D_MODEL,))
_CHANNEL_SHARDED = ("conv_dw_w", "lru_conv_w")


def _pad_lanes(n):
    return -(-n // LANES) * LANES


def _pack(named):
    flat = []
    for a in named:
        a = a.reshape(-1)
        flat.append(jnp.pad(a, (0, _pad_lanes(a.shape[0]) - a.shape[0])))
    v = jnp.concatenate(flat)
    rows = -(-v.shape[0] // (8 * LANES)) * 8
    return jnp.pad(v, (0, rows * LANES - v.shape[0])).reshape(rows, LANES)


def _unpack(vec, shapes):
    flat = vec.reshape(-1)
    out, off = [], 0
    for shp in shapes:
        n = math.prod(shp)
        out.append(flat[off:off + n].reshape(shp))
        off += _pad_lanes(n)
    return out


def _as2d(a):
    return a.reshape(-1, a.shape[-1]) if a.ndim > 1 else a.reshape(1, -1)


def kernel(x, norm1, w_in, attn_sinks, conv_dw_w, conv_dw_b, conv_ln_g, conv_ln_b, lru_conv_w, lru_conv_b, lru_wa, lru_ba, lru_wx, lru_bx, lru_lambda, mix_norm, w_out, norm2, w_up, w_down, final_norm, loss_target, m_norm1, m_w_in, m_attn_sinks, m_conv_dw_w, m_conv_dw_b, m_conv_ln_g, m_conv_ln_b, m_lru_conv_w, m_lru_conv_b, m_lru_wa, m_lru_ba, m_lru_wx, m_lru_bx, m_lru_lambda, m_mix_norm, m_w_out, m_norm2, m_w_up, m_w_down, m_final_norm, v_norm1, v_w_in, v_attn_sinks, v_conv_dw_w, v_conv_dw_b, v_conv_ln_g, v_conv_ln_b, v_lru_conv_w, v_lru_conv_b, v_lru_wa, v_lru_ba, v_lru_wx, v_lru_bx, v_lru_lambda, v_mix_norm, v_w_out, v_norm2, v_w_up, v_w_down, v_final_norm):
    wts = dict(norm1=norm1, w_in=w_in, attn_sinks=attn_sinks, conv_dw_w=conv_dw_w, conv_dw_b=conv_dw_b, conv_ln_g=conv_ln_g,
               conv_ln_b=conv_ln_b, lru_conv_w=lru_conv_w, lru_conv_b=lru_conv_b, lru_wa=lru_wa, lru_ba=lru_ba, lru_wx=lru_wx,
               lru_bx=lru_bx, lru_lambda=lru_lambda, mix_norm=mix_norm, w_out=w_out, norm2=norm2, w_up=w_up, w_down=w_down,
               final_norm=final_norm)
    mom = dict(norm1=m_norm1, w_in=m_w_in, attn_sinks=m_attn_sinks, conv_dw_w=m_conv_dw_w, conv_dw_b=m_conv_dw_b,
               conv_ln_g=m_conv_ln_g, conv_ln_b=m_conv_ln_b, lru_conv_w=m_lru_conv_w, lru_conv_b=m_lru_conv_b, lru_wa=m_lru_wa,
               lru_ba=m_lru_ba, lru_wx=m_lru_wx, lru_bx=m_lru_bx, lru_lambda=m_lru_lambda, mix_norm=m_mix_norm, w_out=m_w_out,
               norm2=m_norm2, w_up=m_w_up, w_down=m_w_down, final_norm=m_final_norm)
    var = dict(norm1=v_norm1, w_in=v_w_in, attn_sinks=v_attn_sinks, conv_dw_w=v_conv_dw_w, conv_dw_b=v_conv_dw_b,
               conv_ln_g=v_conv_ln_g, conv_ln_b=v_conv_ln_b, lru_conv_w=v_lru_conv_w, lru_conv_b=v_lru_conv_b, lru_wa=v_lru_wa,
               lru_ba=v_lru_ba, lru_wx=v_lru_wx, lru_bx=v_lru_bx, lru_lambda=v_lru_lambda, mix_norm=v_mix_norm, w_out=v_w_out,
               norm2=v_norm2, w_up=v_w_up, w_down=v_w_down, final_norm=v_final_norm)

    c_idx = lax.axis_index("c").astype(jnp.int32)
    s_idx = (2 * lax.axis_index("x") + lax.axis_index("y")).astype(jnp.int32)
    idx = jnp.stack([c_idx, s_idx])

    for d in (wts, mom, var):
        d["w_in"] = d["w_in"].transpose(0, 2, 1)

    cs = MeshWeights(wts["w_in"], w_out, w_up, w_down, conv_dw_w, lru_conv_w, idx)
    sp = {n: wts[n] for n in _SMALL}
    sp["conv_dw_w"], sp["lru_conv_w"] = cs.conv_weights()
    loss_blk, grad_x, tots, small_g, d_gf = train_local(x[0], loss_target[0], sp, cs)
    grads_big = {n: a.reshape(wts[n].shape) for n, a in zip(_BIG, halves_exchange(tots))}

    stacked = [jnp.stack([small_g[l][n] for l in range(DEPTH)]) for n in _SMALL if n != "final_norm"]
    packed = _pack(stacked + [d_gf, loss_blk[0, 0:1]])
    summed = small_allreduce(packed)
    names = [n for n in _SMALL if n != "final_norm"] + ["final_norm"]
    unpacked = _unpack(summed, [_SMALL_FULL_SHAPE[n] for n in names] + [(1,)])
    loss = unpacked[-1][0]
    grads = dict(zip(names, unpacked[:-1]))
    for n in _CHANNEL_SHARDED:
        width = wts[n].shape[-1]
        grads[n] = lax.dynamic_slice_in_dim(grads[n], s_idx * width, width, axis=2)
    grads.update(grads_big)

    delta, new_m, new_v = {}, {}, {}
    for n in _BIG:
        grads[n], delta[n], new_m[n], new_v[n] = adamw_big(wts[n], grads[n], mom[n], var[n])
    for d in (grads, delta, new_m, new_v):
        d["w_in"] = d["w_in"].transpose(0, 2, 1)
    sm = list(_SMALL)
    d_s, m_s, v_s = adamw_small([_as2d(wts[n]) for n in sm], [_as2d(grads[n]) for n in sm],
                                [_as2d(mom[n]) for n in sm], [_as2d(var[n]) for n in sm])
    for k, n in enumerate(sm):
        delta[n], new_m[n], new_v[n] = (a.reshape(wts[n].shape) for a in (d_s[k], m_s[k], v_s[k]))

    return (loss, grad_x[None], *[grads[n] for n in _WEIGHTS], *[delta[n] for n in _WEIGHTS],
            *[new_m[n] for n in _WEIGHTS], *[new_v[n] for n in _WEIGHTS])
```

```python
import functools
import math

import jax
import jax.numpy as jnp
from jax import lax
from jax.experimental import pallas as pl
from jax.experimental.pallas import tpu as pltpu

f32 = jnp.float32
bf16 = jnp.bfloat16
SDS = jax.ShapeDtypeStruct

D_MODEL = 1024
DEPTH = 2
ATTN_W = 512
KV_W = 128
HEAD_DIM = 64
N_HEADS = 8
BLOCK = 128
CONV_W = 256
CONV_K = 31
LRU_W = 256
LRU_K = 4
LRU_HEADS = 4
LRU_C = 8.0
IN_W = 1792
D_FF = 4096
N_SHARD = 4
FF_CHUNK = D_FF // N_SHARD
RMS_EPS = 1e-6
LN_EPS = 1e-5
MASK_VALUE = -1e30
HALO = 32
LANES = 128
VMEM_LIMIT = 56 * 1024 * 1024

ADAM_LR = 0.001
ADAM_B1 = 0.9
ADAM_B2 = 0.999
ADAM_EPS = 1e-08
ADAM_WD = 0.01
ADAM_STEP = 10

MESH = pl.DeviceIdType.MESH


def _dot(a, b):
    return jnp.dot(a, b, preferred_element_type=f32)


def _dot_nt(a, b):
    return lax.dot_general(a, b, (((1,), (1,)), ((), ())), preferred_element_type=f32)


def _dot_tn(a, b):
    return lax.dot_general(a, b, (((0,), (0,)), ((), ())), preferred_element_type=f32)


def _rms_fwd(x, g):
    r = lax.rsqrt(jnp.mean(x * x, axis=-1, keepdims=True) + RMS_EPS)
    return x * r * g, r


def _rms_bwd(dy, x, r, g):
    t = dy * g
    dx = r * t - x * (r * r * r) * jnp.mean(t * x, axis=-1, keepdims=True)
    dg = jnp.sum(dy * x * r, axis=0, keepdims=True)
    return dx, dg


def _sigmoid(x):
    return jax.nn.sigmoid(x)


_GELU_K = math.sqrt(2.0 / math.pi)


def _gelu(x):
    t = jnp.tanh(_GELU_K * (x + 0.044715 * x * x * x))
    return 0.5 * x * (1.0 + t), t


def _gelu_grad(x, t):
    return 0.5 * (1.0 + t) + 0.5 * x * (1.0 - t * t) * _GELU_K * (1.0 + 3.0 * 0.044715 * x * x)


def _log1p(x):
    return jnp.where(x < 1e-4, x - 0.5 * x * x, jnp.log(1.0 + x))


def _softplus(x):
    return jnp.maximum(x, 0.0) + _log1p(jnp.exp(-jnp.abs(x)))


def _neg_expm1(x):
    series = -x * (1.0 + 0.5 * x * (1.0 + x * (1.0 / 3.0) * (1.0 + 0.25 * x)))
    return jnp.where(x > -0.01, series, 1.0 - jnp.exp(x))


def _sublane_rolls(x, count, forward):
    n = x.shape[0]
    return [x if b == 0 else pltpu.roll(x, b if forward else n - b, 0) for b in range(count)]


def _conv_taps(xpad, w, k_width):
    t_rows = xpad.shape[0] - HALO
    rolled = _sublane_rolls(xpad, min(k_width, 8), forward=True)
    acc = None
    for k in range(k_width):
        hi, lo = divmod((k_width - 1) - k, 8)
        term = rolled[lo][HALO - 8 * hi:HALO - 8 * hi + t_rows] * w[k:k + 1, :]
        acc = term if acc is None else acc + term
    return acc


def _conv_taps_bwd(dpad, upad, w, k_width, t_rows):
    n_lo = min(k_width, 8)
    d_rolled = _sublane_rolls(dpad, n_lo, forward=False)
    u_rolled = _sublane_rolls(upad, n_lo, forward=True)
    d_in = None
    dw_rows = []
    d_out = dpad[:t_rows]
    for k in range(k_width):
        hi, lo = divmod((k_width - 1) - k, 8)
        term = d_rolled[lo][8 * hi:8 * hi + t_rows] * w[k:k + 1, :]
        d_in = term if d_in is None else d_in + term
        us = u_rolled[lo][HALO - 8 * hi:HALO - 8 * hi + t_rows]
        dw_rows.append(jnp.sum(d_out * us, axis=0, keepdims=True))
    return d_in, dw_rows


SUBLANES = 8


def _scan_fwd(a, b, h0):
    t_rows = a.shape[0]
    sub = jnp.bitwise_and(lax.broadcasted_iota(jnp.int32, a.shape, 0), SUBLANES - 1)
    for d in (1, 2, 4):
        a_sh = jnp.where(sub < d, 1.0, pltpu.roll(a, d, 0))
        b_sh = jnp.where(sub < d, 0.0, pltpu.roll(b, d, 0))
        b = a * b_sh + b
        a = a * a_sh
    out, carry = [], h0
    for g in range(t_rows // SUBLANES):
        rows = slice(g * SUBLANES, (g + 1) * SUBLANES)
        hg = a[rows] * carry + b[rows]
        out.append(hg)
        carry = hg[SUBLANES - 1:SUBLANES]
    return jnp.concatenate(out, axis=0)


def _scan_bwd(a, b, l_end):
    t_rows = a.shape[0]
    sub = jnp.bitwise_and(lax.broadcasted_iota(jnp.int32, a.shape, 0), SUBLANES - 1)
    for d in (1, 2, 4):
        a_sh = jnp.where(sub >= SUBLANES - d, 1.0, pltpu.roll(a, t_rows - d, 0))
        b_sh = jnp.where(sub >= SUBLANES - d, 0.0, pltpu.roll(b, t_rows - d, 0))
        b = b + a * b_sh
        a = a * a_sh
    out, carry = [], l_end
    for g in reversed(range(t_rows // SUBLANES)):
        rows = slice(g * SUBLANES, (g + 1) * SUBLANES)
        lg = b[rows] + a[rows] * carry
        out.append(lg)
        carry = lg[0:1]
    return jnp.concatenate(out[::-1], axis=0)


def _full(shape, single=False):
    nd = len(shape)
    if single:
        return pl.BlockSpec(shape, lambda *_: (0,) * nd, pipeline_mode=pl.Buffered(1))
    return pl.BlockSpec(shape, lambda *_: (0,) * nd)


def _params(sem, vmem=None):
    return pltpu.CompilerParams(dimension_semantics=sem, vmem_limit_bytes=vmem)


def _tile(s):
    return min(512, s)


def _row_tile(rows, cap):
    return next(t for t in range(min(cap, rows) // 8 * 8, 0, -8) if rows % t == 0)


def inproj_fwd(h, g1, w_in_t, pieces=None):
    s = h.shape[0]
    tm = _tile(s)

    def body(h_ref, g_ref, w_ref, z_ref):
        hn, _ = _rms_fwd(h_ref[...], g_ref[...])
        z_ref[...] = _dot_nt(hn.astype(bf16), w_ref[...]).astype(bf16)

    return _pcall(
        body, pieces, name="inproj_fwd", grid=(s // tm,),
        in_specs=[pl.BlockSpec((tm, D_MODEL), lambda i: (i, 0)), _full((1, D_MODEL)), _full((IN_W, D_MODEL))],
        out_specs=[pl.BlockSpec((tm, IN_W), lambda i: (i, 0))],
        out_shape=[SDS((s, IN_W), bf16)],
        operands=[h, g1, w_in_t], sem=("parallel",), vmem=VMEM_LIMIT)[0]


ATT_ROWS = N_HEADS * BLOCK
ATT_BLOCKS_PER_STEP = 8


def _attn_bias():
    qi = jnp.arange(BLOCK)[None, :]
    key = jnp.arange(2 * BLOCK)[:, None]
    band = (key > qi) & (key <= qi + BLOCK)
    first = band & (key >= BLOCK)
    mask = jnp.where(jnp.stack([first, band]), 0.0, MASK_VALUE).astype(f32)
    return jnp.tile(mask, (1, 1, N_HEADS // 2))


def _attn_band(kvc, kvp):
    kb = jnp.concatenate([kvp[:, :KV_W], kvc[:, :KV_W]], axis=0)
    vb = jnp.concatenate([kvp[:, KV_W:], kvc[:, KV_W:]], axis=0)
    lane = lax.broadcasted_iota(jnp.int32, kb.shape, 1)
    kb_sw = pltpu.roll(kb, HEAD_DIM, 1)
    vb_sw = pltpu.roll(vb, HEAD_DIM, 1)
    kx = [jnp.where(lane < HEAD_DIM, kb, kb_sw), jnp.where(lane >= HEAD_DIM, kb, kb_sw)]
    vx = [jnp.where(lane < HEAD_DIM, vb, vb_sw), jnp.where(lane >= HEAD_DIM, vb, vb_sw)]
    return kx, vx


def _stack_heads(x, mlo):
    zero = jnp.zeros((BLOCK, LANES), x.dtype)
    out = []
    for hk in range(2):
        parts = []
        for j in (2 * hk, 2 * hk + 1):
            xj = x[:, j * LANES:(j + 1) * LANES]
            parts += [jnp.where(mlo, xj, zero), jnp.where(mlo, zero, xj)]
        out.append(jnp.concatenate(parts, axis=0))
    return out


def _unstack_heads(y, mlo):
    cols = []
    for hk in range(2):
        for t in range(2):
            base = 2 * t * BLOCK
            cols.append(jnp.where(mlo, y[hk][base:base + BLOCK], y[hk][base + BLOCK:base + 2 * BLOCK]))
    return jnp.concatenate(cols, axis=1)


def _attn_probs(q4, kx, bias_t, sink_row):
    out = []
    half = ATT_ROWS // 2
    for hk in range(2):
        s = _dot_nt(kx[hk], q4[hk]) + bias_t
        sink = sink_row[:, hk * half:(hk + 1) * half]
        m = jnp.maximum(jnp.max(s, axis=0, keepdims=True), sink)
        p = jnp.exp(s - m)
        e_sink = jnp.exp(sink - m)
        inv = 1.0 / (jnp.sum(p, axis=0, keepdims=True) + e_sink)
        out.append((p * inv, e_sink * inv))
    return out


def attn_fwd(z, sink_row, bias, g_a, pieces=None):
    s = z.shape[0]
    per = min(ATT_BLOCKS_PER_STEP, s // BLOCK)
    tq = per * BLOCK

    def body(q_ref, kv_ref, kvp_ref, sk_ref, b_ref, g_ref, o_ref, y_ref):
        n = pl.program_id(0)
        mlo = lax.broadcasted_iota(jnp.int32, (BLOCK, LANES), 1) < HEAD_DIM
        for b in range(per):
            rows = slice(b * BLOCK, (b + 1) * BLOCK)
            kvp = kvp_ref[...] if b == 0 else kv_ref[(b - 1) * BLOCK:b * BLOCK, :]
            bias_b = b_ref[jnp.minimum(n, 1)] if b == 0 else b_ref[1]
            kx, vx = _attn_band(kv_ref[rows, :], kvp)
            q4 = _stack_heads(q_ref[rows, :] * (HEAD_DIM ** -0.5), mlo)
            probs = _attn_probs(q4, kx, bias_b, sk_ref[...])
            o = _unstack_heads([_dot_tn(probs[hk][0].astype(bf16), vx[hk]) for hk in range(2)], mlo)
            o_ref[rows, :] = o.astype(bf16)
            y, _ = _rms_fwd(o, g_ref[...])
            y_ref[rows, :] = y.astype(bf16)

    return _pcall(
        body, pieces, name="attn_fwd", grid=(s // tq,),
        in_specs=[pl.BlockSpec((tq, ATTN_W), lambda n: (n, 0)),
                  pl.BlockSpec((tq, 2 * KV_W), lambda n: (n, 2)),
                  pl.BlockSpec((BLOCK, 2 * KV_W), lambda n: (jnp.maximum(n * per - 1, 0), 2)),
                  _full((1, ATT_ROWS)), _full((2, 2 * BLOCK, ATT_ROWS // 2)), _full((1, ATTN_W))],
        out_specs=[pl.BlockSpec((tq, ATTN_W), lambda n: (n, 0)), pl.BlockSpec((tq, ATTN_W), lambda n: (n, 0))],
        out_shape=[SDS((s, ATTN_W), bf16), SDS((s, ATTN_W), bf16)],
        operands=[z, z, z, sink_row, bias, g_a], sem=("parallel",))


def _lru_gates(xc, wa, ba, wx, bx, lam):
    xcb = xc.astype(bf16)
    r = _sigmoid(_dot(xcb, wa) + ba)
    ig = _sigmoid(_dot(xcb, wx) + bx)
    sp = _softplus(-lam)
    la = (-LRU_C * r) * sp
    a = jnp.exp(la)
    mult = jnp.sqrt(_neg_expm1(2.0 * la))
    return r, ig, sp, la, a, mult


def branch_fwd(z, p, pieces=None):
    s = z.shape[0]
    tm = _tile(s)
    hb = tm // HALO

    def body(cv_ref, cg_ref, rx_ref, rg_ref, cvh_ref, cgh_ref, rxh_ref,
             cw_ref, cb_ref, lng_ref, lnb_ref, lw_ref, lb_ref, wa_ref, ba_ref, wx_ref, bx_ref, lam_ref, gc_ref, gl_ref,
             conv_ref, hst_ref, nc_ref, nl_ref, carry_ref):
        i = pl.program_id(0)
        first = i == 0

        @pl.when(first)
        def _():
            carry_ref[...] = jnp.zeros_like(carry_ref)

        cval = cv_ref[...].astype(f32)
        u = cval * _sigmoid(cg_ref[...].astype(f32))
        hu = jnp.where(first, 0.0, cvh_ref[...].astype(f32) * _sigmoid(cgh_ref[...].astype(f32)))
        conv = _conv_taps(jnp.concatenate([hu, u], axis=0), cw_ref[...], CONV_K) + cb_ref[...]
        conv_ref[...] = conv
        mu = jnp.mean(conv, axis=-1, keepdims=True)
        xm = conv - mu
        rstd = lax.rsqrt(jnp.mean(xm * xm, axis=-1, keepdims=True) + LN_EPS)
        ln = xm * rstd * lng_ref[...] + lnb_ref[...]
        yc = ln * _sigmoid(ln)
        nc, _ = _rms_fwd(yc, gc_ref[...])
        nc_ref[...] = nc.astype(bf16)

        rx = rx_ref[...].astype(f32)
        hrx = jnp.where(first, 0.0, rxh_ref[...].astype(f32))
        xc = _conv_taps(jnp.concatenate([hrx, rx], axis=0), lw_ref[...], LRU_K) + lb_ref[...]
        r, ig, sp, la, a, mult = _lru_gates(xc, wa_ref[...], ba_ref[...], wx_ref[...], bx_ref[...], lam_ref[...])
        gx = mult * (ig * xc)
        hs = _scan_fwd(a, gx, carry_ref[0:1, :])
        carry_ref[...] = jnp.broadcast_to(hs[tm - 1:tm, :], carry_ref.shape)
        hst_ref[...] = hs
        gl, _ = _gelu(rg_ref[...].astype(f32))
        nl, _ = _rms_fwd(hs * gl, gl_ref[...])
        nl_ref[...] = nl.astype(bf16)

    def col(c):
        return pl.BlockSpec((tm, CONV_W), lambda i: (i, c))

    def halo(c):
        return pl.BlockSpec((HALO, CONV_W), lambda i: (jnp.maximum(i * hb - 1, 0), c))

    small = [p["cw"], p["cb"], p["lng"], p["lnb"], p["lw"], p["lb"], p["wa"], p["ba"], p["wx"], p["bx"], p["lam"],
             p["gc"], p["gl"]]
    row = pl.BlockSpec((tm, CONV_W), lambda i: (i, 0))
    return _pcall(
        body, pieces, name="branch_fwd", grid=(s // tm,),
        in_specs=[col(3), col(4), col(5), col(6), halo(3), halo(4), halo(5)] + [_full(a.shape) for a in small],
        out_specs=[row, row, row, row],
        out_shape=[SDS((s, CONV_W), f32), SDS((s, LRU_W), f32), SDS((s, CONV_W), bf16), SDS((s, LRU_W), bf16)],
        scratch_shapes=[pltpu.VMEM((8, LRU_W), f32)],
        operands=[z, z, z, z, z, z, z, *small], sem=("arbitrary",))


def outproj_fwd(ya, yc, yl, h, w_out, g2, pieces=None):
    s = h.shape[0]
    tm = _tile(s)

    def body(ya_ref, yc_ref, yl_ref, h_ref, w_ref, g_ref, h1_ref, hn_ref):
        y = jnp.concatenate([ya_ref[...], yc_ref[...], yl_ref[...]], axis=1)
        h1 = h_ref[...] + _dot(y, w_ref[...])
        h1_ref[...] = h1
        hn, _ = _rms_fwd(h1, g_ref[...])
        hn_ref[...] = hn.astype(bf16)

    return _pcall(
        body, pieces, name="outproj_fwd", grid=(s // tm,),
        in_specs=[pl.BlockSpec((tm, ATTN_W), lambda i: (i, 0)), pl.BlockSpec((tm, CONV_W), lambda i: (i, 0)),
                  pl.BlockSpec((tm, LRU_W), lambda i: (i, 0)), pl.BlockSpec((tm, D_MODEL), lambda i: (i, 0)),
                  _full((D_MODEL, D_MODEL)), _full((1, D_MODEL))],
        out_specs=[pl.BlockSpec((tm, D_MODEL), lambda i: (i, 0)), pl.BlockSpec((tm, D_MODEL), lambda i: (i, 0))],
        out_shape=[SDS((s, D_MODEL), f32), SDS((s, D_MODEL), bf16)],
        operands=[ya, yc, yl, h, w_out, g2], sem=("parallel",), vmem=VMEM_LIMIT)


def mlp_fwd(hn2, h1, w_up, w_dn, pieces=None):
    s = h1.shape[0]
    tm = _tile(s)

    def body(x_ref, h_ref, wu_ref, wd_ref, up_ref, h2_ref):
        h2_ref[...] = _mlp_chunks(x_ref[...], h_ref[...], wu_ref, wd_ref, up_ref)

    return _pcall(
        body, pieces, name="mlp_fwd", grid=(s // tm,),
        in_specs=[pl.BlockSpec((tm, D_MODEL), lambda i: (i, 0)), pl.BlockSpec((tm, D_MODEL), lambda i: (i, 0)),
                  _full((N_SHARD, D_MODEL, FF_CHUNK), single=True), _full((N_SHARD, FF_CHUNK, D_MODEL), single=True)],
        out_specs=[pl.BlockSpec((tm, D_FF), lambda i: (i, 0)), pl.BlockSpec((tm, D_MODEL), lambda i: (i, 0))],
        out_shape=[SDS((s, D_FF), bf16), SDS((s, D_MODEL), f32)],
        operands=[hn2, h1, w_up, w_dn], sem=("parallel",), vmem=VMEM_LIMIT)


def _mlp_chunks(x, acc, wu_ref, wd_ref, up_ref):
    for c in range(N_SHARD):
        u = _dot(x, wu_ref[c])
        up_ref[:, c * FF_CHUNK:(c + 1) * FF_CHUNK] = u.astype(bf16)
        act = jnp.square(jnp.maximum(u, 0.0)).astype(bf16)
        acc = acc + _dot(act, wd_ref[c])
    return acc


def _final_tile(x, tgt, g, dh_ref, loss_ref, dg_ref):
    y, r = _rms_fwd(x, g)
    err = y - tgt
    part = 0.5 * jnp.sum(jnp.mean(err * err, axis=-1, keepdims=True), axis=0, keepdims=True)
    loss_ref[...] += jnp.broadcast_to(part, loss_ref.shape)
    dx, dg = _rms_bwd(err * (1.0 / D_MODEL), x, r, g)
    dh_ref[...] = dx
    dg_ref[...] += dg


def last_layer_fwd(ya, yc, yl, h, w_out, g2, w_up, w_dn, tgt, gf):
    s = h.shape[0]
    tm = min(256, s)

    def body(ya_ref, yc_ref, yl_ref, h_ref, wo_ref, g2_ref, wu_ref, wd_ref, t_ref, gf_ref,
             h1_ref, hn_ref, up_ref, dh_ref, loss_ref, dg_ref):
        i = pl.program_id(0)

        @pl.when(i == 0)
        def _():
            loss_ref[...] = jnp.zeros_like(loss_ref)
            dg_ref[...] = jnp.zeros_like(dg_ref)

        y = jnp.concatenate([ya_ref[...], yc_ref[...], yl_ref[...]], axis=1)
        h1 = h_ref[...] + _dot(y, wo_ref[...])
        h1_ref[...] = h1
        hn, _ = _rms_fwd(h1, g2_ref[...])
        hn = hn.astype(bf16)
        hn_ref[...] = hn
        h2 = _mlp_chunks(hn, h1, wu_ref, wd_ref, up_ref)
        _final_tile(h2, t_ref[...], gf_ref[...], dh_ref, loss_ref, dg_ref)

    def rowb(w):
        return pl.BlockSpec((tm, w), lambda i: (i, 0))

    return pl.pallas_call(
        body, name="last_layer_fwd", grid=(s // tm,),
        in_specs=[rowb(ATTN_W), rowb(CONV_W), rowb(LRU_W), rowb(D_MODEL), _full((D_MODEL, D_MODEL), single=True),
                  _full((1, D_MODEL)), _full((N_SHARD, D_MODEL, FF_CHUNK), single=True),
                  _full((N_SHARD, FF_CHUNK, D_MODEL), single=True), rowb(D_MODEL), _full((1, D_MODEL))],
        out_specs=[rowb(D_MODEL), rowb(D_MODEL), rowb(D_FF), rowb(D_MODEL), _full((8, LANES)), _full((1, D_MODEL))],
        out_shape=[SDS((s, D_MODEL), f32), SDS((s, D_MODEL), bf16), SDS((s, D_FF), bf16), SDS((s, D_MODEL), f32),
                   SDS((8, LANES), f32), SDS((1, D_MODEL), f32)],
        compiler_params=_params(("arbitrary",), VMEM_LIMIT),
    )(ya, yc, yl, h, w_out, g2, w_up, w_dn, tgt, gf)


def mlp_bwd_act(dh, up, h1, g2, w_up, w_dn, pieces=None):
    s = dh.shape[0]
    tm = _tile(s)

    def body(dh_ref, up_ref, h1_ref, g_ref, wu_ref, wd_ref, dup_ref, dh1_ref, dg_ref):
        i = pl.program_id(0)

        @pl.when(i == 0)
        def _():
            dg_ref[...] = jnp.zeros_like(dg_ref)

        dh = dh_ref[...]
        dhb = dh.astype(bf16)
        d_hn = jnp.zeros((tm, D_MODEL), f32)
        for c in range(N_SHARD):
            d_act = _dot_nt(dhb, wd_ref[c])
            u = up_ref[:, c * FF_CHUNK:(c + 1) * FF_CHUNK].astype(f32)
            d_u = (d_act * (2.0 * jnp.maximum(u, 0.0))).astype(bf16)
            dup_ref[:, c * FF_CHUNK:(c + 1) * FF_CHUNK] = d_u
            d_hn = d_hn + _dot_nt(d_u, wu_ref[c])
        x = h1_ref[...]
        g = g_ref[...]
        _, r = _rms_fwd(x, g)
        dx, dg = _rms_bwd(d_hn, x, r, g)
        dh1_ref[...] = dh + dx
        dg_ref[...] += dg

    return _pcall(
        body, pieces, name="mlp_bwd_act", grid=(s // tm,),
        in_specs=[pl.BlockSpec((tm, D_MODEL), lambda i: (i, 0)), pl.BlockSpec((tm, D_FF), lambda i: (i, 0)),
                  pl.BlockSpec((tm, D_MODEL), lambda i: (i, 0)), _full((1, D_MODEL)),
                  _full((N_SHARD, D_MODEL, FF_CHUNK), single=True), _full((N_SHARD, FF_CHUNK, D_MODEL), single=True)],
        out_specs=[pl.BlockSpec((tm, D_FF), lambda i: (i, 0)), pl.BlockSpec((tm, D_MODEL), lambda i: (i, 0)),
                   _full((1, D_MODEL))],
        out_shape=[SDS((s, D_FF), bf16), SDS((s, D_MODEL), f32), SDS((1, D_MODEL), f32)],
        operands=[dh, up, h1, g2, w_up, w_dn], sem=("arbitrary",), vmem=VMEM_LIMIT)


def mlp_bwd_w(hn2, d_up, up, dh, pieces=None):
    s = dh.shape[0]
    tk = min(1024, s)

    def body(x_ref, du_ref, up_ref, dh_ref, dwu_ref, dwd_ref):
        k = pl.program_id(1)

        @pl.when(k == 0)
        def _():
            dwu_ref[...] = jnp.zeros_like(dwu_ref)
            dwd_ref[...] = jnp.zeros_like(dwd_ref)

        dwu_ref[0] += _dot_tn(x_ref[...], du_ref[...])
        act = jnp.square(jnp.maximum(up_ref[...].astype(f32), 0.0)).astype(bf16)
        dwd_ref[0] += _dot_tn(act, dh_ref[...].astype(bf16))

    return _pcall(
        body, pieces, name="mlp_bwd_w", grid=(N_SHARD, s // tk),
        in_specs=[pl.BlockSpec((tk, D_MODEL), lambda c, k: (k, 0)), pl.BlockSpec((tk, FF_CHUNK), lambda c, k: (k, c)),
                  pl.BlockSpec((tk, FF_CHUNK), lambda c, k: (k, c)), pl.BlockSpec((tk, D_MODEL), lambda c, k: (k, 0))],
        out_specs=[pl.BlockSpec((1, D_MODEL, FF_CHUNK), lambda c, k: (c, 0, 0)),
                   pl.BlockSpec((1, FF_CHUNK, D_MODEL), lambda c, k: (c, 0, 0))],
        out_shape=[SDS((N_SHARD, D_MODEL, FF_CHUNK), f32), SDS((N_SHARD, FF_CHUNK, D_MODEL), f32)],
        operands=[hn2, d_up, up, dh], sem=("parallel", "arbitrary"), vmem=VMEM_LIMIT)


def outproj_bwd(dh1, ya, yc, yl, w_out):
    s = dh1.shape[0]
    tm = _tile(s)

    def body(dh_ref, ya_ref, yc_ref, yl_ref, w_ref, dy_ref, dw_ref):
        i = pl.program_id(0)

        @pl.when(i == 0)
        def _():
            dw_ref[...] = jnp.zeros_like(dw_ref)

        dhb = dh_ref[...].astype(bf16)
        dy_ref[...] = _dot_nt(dhb, w_ref[...])
        y = jnp.concatenate([ya_ref[...], yc_ref[...], yl_ref[...]], axis=1)
        dw_ref[...] += _dot_tn(y, dhb)

    return pl.pallas_call(
        body, name="outproj_bwd", grid=(s // tm,),
        in_specs=[pl.BlockSpec((tm, D_MODEL), lambda i: (i, 0)), pl.BlockSpec((tm, ATTN_W), lambda i: (i, 0)),
                  pl.BlockSpec((tm, CONV_W), lambda i: (i, 0)), pl.BlockSpec((tm, LRU_W), lambda i: (i, 0)),
                  _full((D_MODEL, D_MODEL))],
        out_specs=[pl.BlockSpec((tm, D_MODEL), lambda i: (i, 0)), _full((D_MODEL, D_MODEL))],
        out_shape=[SDS((s, D_MODEL), f32), SDS((D_MODEL, D_MODEL), f32)],
        compiler_params=_params(("arbitrary",), VMEM_LIMIT),
    )(dh1, ya, yc, yl, w_out)


def attn_bwd(z, o, dy, sink_row, bias, g_a, pieces=None):
    s = z.shape[0]
    per = min(ATT_BLOCKS_PER_STEP, s // BLOCK)
    tq = per * BLOCK
    nt = s // tq

    def body(q_ref, kv_ref, kvp_ref, o_ref, dy_ref, sk_ref, b_ref, g_ref, dq_ref, dkv_ref, dsk_ref, dg_ref,
             carry_ref, dsk_acc):
        i = pl.program_id(0)
        t = nt - 1 - i

        @pl.when(i == 0)
        def _():
            carry_ref[...] = jnp.zeros_like(carry_ref)
            dsk_acc[...] = jnp.zeros_like(dsk_acc)
            dg_ref[...] = jnp.zeros_like(dg_ref)

        mlo = lax.broadcasted_iota(jnp.int32, (BLOCK, LANES), 1) < HEAD_DIM
        lane = lax.broadcasted_iota(jnp.int32, (2 * BLOCK, LANES), 1)
        scale = HEAD_DIM ** -0.5
        half = ATT_ROWS // 2
        g = g_ref[...]
        bands = []
        for b in range(per):
            rows = slice(b * BLOCK, (b + 1) * BLOCK)
            kvp = kvp_ref[...] if b == 0 else kv_ref[(b - 1) * BLOCK:b * BLOCK, :]
            bias_b = b_ref[jnp.minimum(t, 1)] if b == 0 else b_ref[1]
            kx, vx = _attn_band(kv_ref[rows, :], kvp)
            q4 = _stack_heads(q_ref[rows, :] * scale, mlo)
            o_f = o_ref[rows, :].astype(f32)
            _, r = _rms_fwd(o_f, g)
            d_o, dg = _rms_bwd(dy_ref[rows, :], o_f, r, g)
            dg_ref[...] += dg
            do4 = _stack_heads(d_o.astype(bf16), mlo)
            probs = _attn_probs(q4, kx, bias_b, sk_ref[...])
            dq4, tk, tv = [], [], []
            for hk in range(2):
                pr, p_sink = probs[hk]
                d_p = _dot_nt(vx[hk], do4[hk])
                d_row = jnp.sum(pr * d_p, axis=0, keepdims=True)
                d_s = (pr * (d_p - d_row)).astype(bf16)
                dsk_acc[:, hk * half:(hk + 1) * half] -= p_sink * d_row
                dq4.append(_dot_tn(d_s, kx[hk]))
                tk.append(_dot(d_s, q4[hk]))
                tv.append(_dot(pr.astype(bf16), do4[hk]))
            dq_ref[rows, :] = (_unstack_heads(dq4, mlo) * scale).astype(bf16)
            fk = [x + pltpu.roll(x, HEAD_DIM, 1) for x in tk]
            fv = [x + pltpu.roll(x, HEAD_DIM, 1) for x in tv]
            bands.append(jnp.concatenate([jnp.where(lane < HEAD_DIM, fk[0], fk[1]),
                                          jnp.where(lane < HEAD_DIM, fv[0], fv[1])], axis=1))
        for b in range(per):
            after = bands[b + 1][:BLOCK] if b + 1 < per else carry_ref[...]
            dkv_ref[b * BLOCK:(b + 1) * BLOCK, :] = (bands[b][BLOCK:] + after).astype(bf16)
        carry_ref[...] = bands[0][:BLOCK]

        @pl.when(i == nt - 1)
        def _():
            for hh in range(N_HEADS):
                tot = jnp.sum(dsk_acc[:, hh * BLOCK:(hh + 1) * BLOCK], axis=1, keepdims=True)
                dsk_ref[hh:hh + 1, :] = jnp.broadcast_to(tot, (1, LANES))

    def rev(width, col):
        return pl.BlockSpec((tq, width), lambda i: (nt - 1 - i, col))

    return _pcall(
        body, pieces, name="attn_bwd", grid=(nt,),
        in_specs=[rev(ATTN_W, 0), rev(2 * KV_W, 2),
                  pl.BlockSpec((BLOCK, 2 * KV_W), lambda i: (jnp.maximum((nt - 1 - i) * per - 1, 0), 2)),
                  rev(ATTN_W, 0), rev(ATTN_W, 0),
                  _full((1, ATT_ROWS)), _full((2, 2 * BLOCK, ATT_ROWS // 2)), _full((1, ATTN_W))],
        out_specs=[rev(ATTN_W, 0), rev(2 * KV_W, 0), _full((N_HEADS, LANES)), _full((1, ATTN_W))],
        out_shape=[SDS((s, ATTN_W), bf16), SDS((s, 2 * KV_W), bf16), SDS((N_HEADS, LANES), f32), SDS((1, ATTN_W), f32)],
        scratch_shapes=[pltpu.VMEM((BLOCK, 2 * KV_W), f32), pltpu.VMEM((1, ATT_ROWS), f32)],
        operands=[z, z, z, o, dy, sink_row, bias, g_a], sem=("arbitrary",))


_V_GC, _V_LNG, _V_LNB, _V_CB, _V_GL, _V_BA, _V_BX, _V_LAM, _V_LB = range(9)
_V_ROWS = 16


def branch_bwd_a(z, conv, hst, dy, p, pieces=None):
    s = z.shape[0]
    tm = _tile(s)
    nt = s // tm
    hb = tm // HALO
    h8 = tm // 8

    def body(conv_ref, dyc_ref, dyl_ref, rx_ref, rxh_ref, rg_ref, hst_ref, hsth_ref,
             lng_ref, lnb_ref, lw_ref, lb_ref, wa_ref, ba_ref, wx_ref, bx_ref, lam_ref, gc_ref, gl_ref,
             dconv_ref, dxc_ref, drg_ref, vec_ref, dwa_ref, dwx_ref, carry_ref):
        i = pl.program_id(0)
        ti = nt - 1 - i

        @pl.when(i == 0)
        def _():
            carry_ref[...] = jnp.zeros_like(carry_ref)
            vec_ref[...] = jnp.zeros_like(vec_ref)
            dwa_ref[...] = jnp.zeros_like(dwa_ref)
            dwx_ref[...] = jnp.zeros_like(dwx_ref)

        conv = conv_ref[...]
        mu = jnp.mean(conv, axis=-1, keepdims=True)
        xm = conv - mu
        rstd = lax.rsqrt(jnp.mean(xm * xm, axis=-1, keepdims=True) + LN_EPS)
        xhat = xm * rstd
        lng = lng_ref[...]
        ln = xhat * lng + lnb_ref[...]
        sg = _sigmoid(ln)
        yc = ln * sg
        gc = gc_ref[...]
        _, rc = _rms_fwd(yc, gc)
        d_yc, d_gc = _rms_bwd(dyc_ref[...], yc, rc, gc)
        d_ln = d_yc * (sg * (1.0 + ln * (1.0 - sg)))
        d_xhat = d_ln * lng
        d_conv = rstd * (d_xhat - jnp.mean(d_xhat, axis=-1, keepdims=True)
                         - xhat * jnp.mean(d_xhat * xhat, axis=-1, keepdims=True))
        dconv_ref[...] = d_conv

        rx = rx_ref[...].astype(f32)
        hrx = jnp.where(ti == 0, 0.0, rxh_ref[...].astype(f32))
        xc = _conv_taps(jnp.concatenate([hrx, rx], axis=0), lw_ref[...], LRU_K) + lb_ref[...]
        wa = wa_ref[...]
        wx = wx_ref[...]
        lam = lam_ref[...]
        r, ig, sp, la, a, mult = _lru_gates(xc, wa, ba_ref[...], wx, bx_ref[...], lam)
        hs = hst_ref[...]
        row = lax.broadcasted_iota(jnp.int32, hs.shape, 0)
        h_before = jnp.where(ti == 0, 0.0, hsth_ref[7:8, :])
        h_prev = jnp.where(row == 0, h_before, pltpu.roll(hs, 1, 0))
        rg = rg_ref[...].astype(f32)
        gl, tg = _gelu(rg)
        out = hs * gl
        gmix = gl_ref[...]
        _, rl = _rms_fwd(out, gmix)
        d_out, d_gl = _rms_bwd(dyl_ref[...], out, rl, gmix)
        drg_ref[...] = (d_out * hs * _gelu_grad(rg, tg)).astype(bf16)
        d_h = d_out * gl
        last = row == tm - 1
        a_next = jnp.where(last, 1.0, pltpu.roll(a, tm - 1, 0))
        lmb = _scan_bwd(a_next, d_h, carry_ref[0:1, :])
        carry_ref[...] = jnp.broadcast_to(a[0:1, :] * lmb[0:1, :], carry_ref.shape)
        d_a = lmb * h_prev
        d_mult = lmb * (ig * xc)
        d_ig = lmb * (mult * xc)
        d_la = d_a * a - d_mult * (a * a) / jnp.maximum(mult, 1e-30)
        d_pa = (d_la * (-LRU_C * sp)) * (r * (1.0 - r))
        d_px = d_ig * (ig * (1.0 - ig))
        d_pab = d_pa.astype(bf16)
        d_pxb = d_px.astype(bf16)
        d_xc = lmb * (mult * ig) + _dot_nt(d_pab, wa) + _dot_nt(d_pxb, wx)
        dxc_ref[...] = d_xc
        xcb = xc.astype(bf16)
        dwa_ref[...] += _dot_tn(xcb, d_pab)
        dwx_ref[...] += _dot_tn(xcb, d_pxb)
        d_lam = jnp.sum(d_la * (-LRU_C * r), axis=0, keepdims=True) * (-_sigmoid(-lam))

        def colsum(v):
            return jnp.sum(v, axis=0, keepdims=True)

        rows = [None] * _V_ROWS
        rows[_V_GC] = d_gc
        rows[_V_LNG] = colsum(d_ln * xhat)
        rows[_V_LNB] = colsum(d_ln)
        rows[_V_CB] = colsum(d_conv)
        rows[_V_GL] = d_gl
        rows[_V_BA] = colsum(d_pa)
        rows[_V_BX] = colsum(d_px)
        rows[_V_LAM] = d_lam
        rows[_V_LB] = colsum(d_xc)
        zero = jnp.zeros((1, CONV_W), f32)
        vec_ref[...] += jnp.concatenate([zero if v is None else v for v in rows], axis=0)

    def rev(c):
        return pl.BlockSpec((tm, CONV_W), lambda i: (nt - 1 - i, c))

    small = [p["lng"], p["lnb"], p["lw"], p["lb"], p["wa"], p["ba"], p["wx"], p["bx"], p["lam"], p["gc"], p["gl"]]
    return _pcall(
        body, pieces, name="branch_bwd_a", grid=(nt,),
        in_specs=[rev(0), rev(2), rev(3), rev(5),
                  pl.BlockSpec((HALO, CONV_W), lambda i: (jnp.maximum((nt - 1 - i) * hb - 1, 0), 5)),
                  rev(6), rev(0),
                  pl.BlockSpec((8, LRU_W), lambda i: (jnp.maximum((nt - 1 - i) * h8 - 1, 0), 0))]
                 + [_full(a.shape) for a in small],
        out_specs=[rev(0), rev(0), rev(0), _full((_V_ROWS, CONV_W)), _full((LRU_W, LRU_W)), _full((LRU_W, LRU_W))],
        out_shape=[SDS((s, CONV_W), f32), SDS((s, LRU_W), f32), SDS((s, LRU_W), bf16), SDS((_V_ROWS, CONV_W), f32),
                   SDS((LRU_W, LRU_W), f32), SDS((LRU_W, LRU_W), f32)],
        scratch_shapes=[pltpu.VMEM((8, LRU_W), f32)],
        operands=[conv, dy, dy, z, z, z, hst, hst, *small], sem=("arbitrary",))


def branch_bwd_b(z, d_conv, d_xc, p, pieces=None):
    s = z.shape[0]
    tm = _tile(s)
    nt = s // tm
    hb = tm // HALO

    def body(cv_ref, cg_ref, cvh_ref, cgh_ref, rx_ref, rxh_ref, dc_ref, dch_ref, dx_ref, dxh_ref, cw_ref, lw_ref,
             dzc_ref, dzr_ref, dcw_ref, dlw_ref):
        i = pl.program_id(0)

        @pl.when(i == 0)
        def _():
            dcw_ref[...] = jnp.zeros_like(dcw_ref)
            dlw_ref[...] = jnp.zeros_like(dlw_ref)

        first = i == 0
        last = i == nt - 1
        cval = cv_ref[...].astype(f32)
        sg = _sigmoid(cg_ref[...].astype(f32))
        u = cval * sg
        hu = jnp.where(first, 0.0, cvh_ref[...].astype(f32) * _sigmoid(cgh_ref[...].astype(f32)))
        dpad = jnp.concatenate([dc_ref[...], jnp.where(last, 0.0, dch_ref[...])], axis=0)
        d_u, dw_rows = _conv_taps_bwd(dpad, jnp.concatenate([hu, u], axis=0), cw_ref[...], CONV_K, tm)
        dcw_ref[...] += jnp.concatenate(dw_rows + [jnp.zeros((HALO - CONV_K, CONV_W), f32)], axis=0)
        dzc_ref[...] = jnp.concatenate([d_u * sg, d_u * cval * sg * (1.0 - sg)], axis=1).astype(bf16)

        rx = rx_ref[...].astype(f32)
        hrx = jnp.where(first, 0.0, rxh_ref[...].astype(f32))
        dxpad = jnp.concatenate([dx_ref[...], jnp.where(last, 0.0, dxh_ref[...])], axis=0)
        d_rx, dlw_rows = _conv_taps_bwd(dxpad, jnp.concatenate([hrx, rx], axis=0), lw_ref[...], LRU_K, tm)
        dlw_ref[...] += jnp.concatenate(dlw_rows + [jnp.zeros((8 - LRU_K, LRU_W), f32)], axis=0)
        dzr_ref[...] = d_rx.astype(bf16)

    def col(c):
        return pl.BlockSpec((tm, CONV_W), lambda i: (i, c))

    def prev(c):
        return pl.BlockSpec((HALO, CONV_W), lambda i: (jnp.maximum(i * hb - 1, 0), c))

    nxt = pl.BlockSpec((HALO, CONV_W), lambda i: (jnp.minimum((i + 1) * hb, nt * hb - 1), 0))
    return _pcall(
        body, pieces, name="branch_bwd_b", grid=(nt,),
        in_specs=[col(3), col(4), prev(3), prev(4), col(5), prev(5), col(0), nxt, col(0), nxt,
                  _full(p["cw"].shape), _full(p["lw"].shape)],
        out_specs=[pl.BlockSpec((tm, 2 * CONV_W), lambda i: (i, 0)), pl.BlockSpec((tm, LRU_W), lambda i: (i, 0)),
                   _full((HALO, CONV_W)), _full((8, LRU_W))],
        out_shape=[SDS((s, 2 * CONV_W), bf16), SDS((s, LRU_W), bf16), SDS((HALO, CONV_W), f32), SDS((8, LRU_W), f32)],
        operands=[z, z, z, z, z, z, d_conv, d_conv, d_xc, d_xc, p["cw"], p["lw"]], sem=("arbitrary",))


def inproj_bwd(dq, dkv, dzc, dzr, drg, h, g1, w_in_t, dh1, pieces=None):
    s = h.shape[0]
    tm = _tile(s)

    def body(dq_ref, dkv_ref, dzc_ref, dzr_ref, drg_ref, h_ref, g_ref, w_ref, dh1_ref, dh_ref, dw_ref, dg_ref):
        i = pl.program_id(0)

        @pl.when(i == 0)
        def _():
            dw_ref[...] = jnp.zeros_like(dw_ref)
            dg_ref[...] = jnp.zeros_like(dg_ref)

        dz = jnp.concatenate([dq_ref[...], dkv_ref[...], dzc_ref[...], dzr_ref[...], drg_ref[...]], axis=1)
        x = h_ref[...]
        g = g_ref[...]
        hn, r = _rms_fwd(x, g)
        d_hn = _dot(dz, w_ref[...])
        dw_ref[...] += _dot_tn(dz, hn.astype(bf16))
        dx, dg = _rms_bwd(d_hn, x, r, g)
        dh_ref[...] = dh1_ref[...] + dx
        dg_ref[...] += dg

    def rowb(w):
        return pl.BlockSpec((tm, w), lambda i: (i, 0))

    return _pcall(
        body, pieces, name="inproj_bwd", grid=(s // tm,),
        in_specs=[rowb(ATTN_W), rowb(2 * KV_W), rowb(2 * CONV_W), rowb(LRU_W), rowb(LRU_W), rowb(D_MODEL),
                  _full((1, D_MODEL)), _full((IN_W, D_MODEL)), rowb(D_MODEL)],
        out_specs=[rowb(D_MODEL), _full((IN_W, D_MODEL)), _full((1, D_MODEL))],
        out_shape=[SDS((s, D_MODEL), f32), SDS((IN_W, D_MODEL), f32), SDS((1, D_MODEL), f32)],
        operands=[dq, dkv, dzc, dzr, drg, h, g1, w_in_t, dh1], sem=("arbitrary",), vmem=VMEM_LIMIT)


def _block_diag(w):
    out = jnp.zeros((LRU_W, LRU_W), w.dtype)
    hd = LRU_W // LRU_HEADS
    for hh in range(LRU_HEADS):
        out = out.at[hh * hd:(hh + 1) * hd, hh * hd:(hh + 1) * hd].set(w[hh])
    return out


def _diag_blocks(w):
    hd = LRU_W // LRU_HEADS
    return jnp.stack([w[hh * hd:(hh + 1) * hd, hh * hd:(hh + 1) * hd] for hh in range(LRU_HEADS)])


def _layer_params(sp, l):
    mix = sp["mix_norm"][l]
    return dict(
        g1=sp["norm1"][l][None, :], g2=sp["norm2"][l][None, :],
        sinks=jnp.repeat(sp["attn_sinks"][l], BLOCK)[None, :],
        ga=mix[None, :ATTN_W], gc=mix[None, ATTN_W:ATTN_W + CONV_W], gl=mix[None, ATTN_W + CONV_W:],
        cw=jnp.pad(sp["conv_dw_w"][l], ((0, HALO - CONV_K), (0, 0))), cb=sp["conv_dw_b"][l][None, :],
        lng=sp["conv_ln_g"][l][None, :], lnb=sp["conv_ln_b"][l][None, :],
        lw=jnp.pad(sp["lru_conv_w"][l], ((0, 8 - LRU_K), (0, 0))), lb=sp["lru_conv_b"][l][None, :],
        wa=_block_diag(sp["lru_wa"][l]).astype(bf16), ba=sp["lru_ba"][l].reshape(1, LRU_W),
        wx=_block_diag(sp["lru_wx"][l]).astype(bf16), bx=sp["lru_bx"][l].reshape(1, LRU_W),
        lam=sp["lru_lambda"][l][None, :],
    )


def local_step(x, tgt, big, sp):
    return train_local(x, tgt, sp, LocalWeights(big))


class LocalWeights:
    def __init__(self, big):
        self.big = big
        self.grads = [dict() for _ in range(DEPTH)]

    def weight(self, name, l):
        return self.big[l][name]

    def host(self, point, l):
        return None

    def grad(self, name, l, g):
        self.grads[l][name] = g

    def big_grads(self):
        return self.grads


def train_local(x, tgt, sp, cs):
    lp = [_layer_params(sp, l) for l in range(DEPTH)]
    bias = _attn_bias()
    saved = []
    h = x
    for l in range(DEPTH):
        p = lp[l]
        z = inproj_fwd(h, p["g1"], cs.weight("w_in", l), cs.host("inproj_fwd", l))
        o, ya = attn_fwd(z, p["sinks"], bias, p["ga"], cs.host("attn_fwd", l))
        conv, hst, yc, yl = branch_fwd(z, p, cs.host("branch_fwd", l))
        if l < DEPTH - 1:
            h1, hn2 = outproj_fwd(ya, yc, yl, h, cs.weight("w_out", l), p["g2"], cs.host("outproj_fwd", l))
            up, h_next = mlp_fwd(hn2, h1, cs.weight("w_up", l), cs.weight("w_dn", l), cs.host("mlp_fwd", l))
        else:
            h1, hn2, up, dh, loss, d_gf = last_layer_fwd(
                ya, yc, yl, h, cs.weight("w_out", l), p["g2"], cs.weight("w_up", l), cs.weight("w_dn", l),
                tgt, sp["final_norm"][None, :])
            h_next = None
        saved.append(dict(h=h, z=z, o=o, ya=ya, conv=conv, hst=hst, yc=yc, yl=yl, h1=h1, hn2=hn2, up=up))
        h = h_next
    small_g = [None] * DEPTH
    for l in reversed(range(DEPTH)):
        p, sv = lp[l], saved[l]
        w_up, w_dn = cs.weight("w_up", l), cs.weight("w_dn", l)
        d_up, dh1, d_g2 = mlp_bwd_act(dh, sv["up"], sv["h1"], p["g2"], w_up, w_dn, cs.host("mlp_bwd_act", l))
        dw_up, dw_dn = mlp_bwd_w(sv["hn2"], d_up, sv["up"], dh, cs.host("mlp_bwd_w", l))
        cs.grad("w_up", l, dw_up)
        cs.grad("w_dn", l, dw_dn)
        dy, dw_out = outproj_bwd(dh1, sv["ya"], sv["yc"], sv["yl"], cs.weight("w_out", l))
        cs.grad("w_out", l, dw_out)
        dq, dkv, d_sk, d_ga = attn_bwd(sv["z"], sv["o"], dy, p["sinks"], bias, p["ga"], cs.host("attn_bwd", l))
        d_conv, d_xc, d_rg, vec, dwa, dwx = branch_bwd_a(sv["z"], sv["conv"], sv["hst"], dy, p, cs.host("branch_bwd_a", l))
        dzc, dzr, dcw, dlw = branch_bwd_b(sv["z"], d_conv, d_xc, p, cs.host("branch_bwd_b", l))
        dh, dw_in, d_g1 = inproj_bwd(dq, dkv, dzc, dzr, d_rg, sv["h"], p["g1"], cs.weight("w_in", l), dh1,
                                     cs.host("inproj_bwd", l))
        cs.grad("w_in", l, dw_in)
        hd = LRU_W // LRU_HEADS
        small_g[l] = dict(
            norm1=d_g1[0], attn_sinks=d_sk[:, 0], conv_dw_w=dcw[:CONV_K], conv_dw_b=vec[_V_CB],
            conv_ln_g=vec[_V_LNG], conv_ln_b=vec[_V_LNB], lru_conv_w=dlw[:LRU_K], lru_conv_b=vec[_V_LB],
            lru_wa=_diag_blocks(dwa), lru_ba=vec[_V_BA].reshape(LRU_HEADS, hd),
            lru_wx=_diag_blocks(dwx), lru_bx=vec[_V_BX].reshape(LRU_HEADS, hd), lru_lambda=vec[_V_LAM],
            mix_norm=jnp.concatenate([d_ga[0], vec[_V_GC], vec[_V_GL]]), norm2=d_g2[0],
        )
    return loss, dh, cs.big_grads(), small_g, d_gf[0]


_HBM = pl.BlockSpec(memory_space=pl.ANY)


def _place():
    x, y, c = lax.axis_index("x"), lax.axis_index("y"), lax.axis_index("c")
    chips = [(1 - x, y), (x, 1 - y), (1 - x, 1 - y)]
    return x, y, c, chips


class Comm:
    def __init__(self, ins, out_shape, aliases, sems, start, finish, done):
        self.ins, self.out_shape, self.aliases, self.sems = list(ins), list(out_shape), dict(aliases), list(sems)
        self.start, self.finish, self.done = start, finish, done


def _pcall(body, pieces, *, name, grid, in_specs, out_specs, out_shape, operands, scratch_shapes=(), sem, vmem=None):
    in_specs, out_specs, out_shape, scratch_shapes = list(in_specs), list(out_specs), list(out_shape), list(scratch_shapes)
    if not pieces:
        return pl.pallas_call(body, name=name, grid=grid, in_specs=in_specs, out_specs=out_specs, out_shape=out_shape,
                              scratch_shapes=scratch_shapes, compiler_params=_params(sem, vmem))(*operands)
    n_in, n_out, n_scr = len(in_specs), len(out_specs), len(scratch_shapes)
    c_ins = [a for p in pieces for a in p.ins]
    c_outs = [s for p in pieces for s in p.out_shape]
    c_sems = [n for p in pieces for n in p.sems]
    aliases, spans, ki, ko, ks = {}, [], 0, 0, 0
    for p in pieces:
        spans.append((ki, ko, ks))
        for a, b in p.aliases.items():
            aliases[n_in + ki + a] = n_out + ko + b
        ki, ko, ks = ki + len(p.ins), ko + len(p.out_shape), ks + len(p.sems)

    def hosted(*refs):
        ins, cin = refs[:n_in], refs[n_in:n_in + ki]
        outs, cout = refs[n_in + ki:n_in + ki + n_out], refs[n_in + ki + n_out:n_in + ki + n_out + ko]
        scr, csem = refs[n_in + ki + n_out + ko:n_in + ki + n_out + ko + n_scr], refs[n_in + ki + n_out + ko + n_scr:]
        first = functools.reduce(jnp.logical_and, [pl.program_id(d) == 0 for d in range(len(grid))])
        last = functools.reduce(jnp.logical_and, [pl.program_id(d) == grid[d] - 1 for d in range(len(grid))])

        def each(which):
            for p, (a, b, s) in zip(pieces, spans):
                getattr(p, which)(cin[a:a + len(p.ins)], cout[b:b + len(p.out_shape)], csem[s:s + len(p.sems)])

        @pl.when(first)
        def _():
            each("start")

        body(*ins, *outs, *scr)

        @pl.when(last)
        def _():
            each("finish")

    res = pl.pallas_call(
        hosted, name=name + "_host", grid=grid, in_specs=in_specs + [_HBM] * ki, out_specs=out_specs + [_HBM] * ko,
        out_shape=out_shape + c_outs, scratch_shapes=scratch_shapes + [pltpu.SemaphoreType.DMA((n,)) for n in c_sems],
        input_output_aliases=aliases, compiler_params=_params(("arbitrary",) * len(grid), vmem),
    )(*operands, *c_ins)
    for p, (a, b, s) in zip(pieces, spans):
        p.done(res[n_out + b:n_out + b + len(p.out_shape)])
    return res[:n_out]


def standalone(pieces, name):
    ki = sum(len(p.ins) for p in pieces)
    ko = sum(len(p.out_shape) for p in pieces)
    spans, a, b, s = [], 0, 0, 0
    aliases = {}
    for p in pieces:
        spans.append((a, b, s))
        for i, o in p.aliases.items():
            aliases[a + i] = b + o
        a, b, s = a + len(p.ins), b + len(p.out_shape), s + len(p.sems)

    def body(*refs):
        cin, cout, csem = refs[:ki], refs[ki:ki + ko], refs[ki + ko:]
        for which in ("start", "finish"):
            for p, (a, b, s) in zip(pieces, spans):
                getattr(p, which)(cin[a:a + len(p.ins)], cout[b:b + len(p.out_shape)], csem[s:s + len(p.sems)])

    res = pl.pallas_call(
        body, name=name, in_specs=[_HBM] * ki, out_specs=[_HBM] * ko, out_shape=[s for p in pieces for s in p.out_shape],
        scratch_shapes=[pltpu.SemaphoreType.DMA((n,)) for p in pieces for n in p.sems], input_output_aliases=aliases,
    )(*[a for p in pieces for a in p.ins])
    for p, (a, b, s) in zip(pieces, spans):
        p.done(res[b:b + len(p.out_shape)])


def _rows_half(ref, which, rows):
    return ref.at[pl.ds(pl.multiple_of(which * rows, 8), rows)]


def gather_ici_piece(bufs, done):
    n = len(bufs)

    def copies(cout):
        x, y, c, chips = _place()
        out = []
        for j, (cx, cy) in enumerate(chips):
            for w in range(n):
                half = bufs[w].shape[1] // 2
                out.append((j, w, _rows_half(cout[w].at[2 * x + y], c, half), _rows_half(cout[w].at[2 * cx + cy], c, half),
                            (cx, cy, c)))
        return out

    def start(cin, cout, sems):
        for j, w, mine, _, to in copies(cout):
            pltpu.make_async_remote_copy(src_ref=mine, dst_ref=mine, send_sem=sems[0].at[n * j + w],
                                         recv_sem=sems[1].at[n * j + w], device_id=to, device_id_type=MESH).start()

    def finish(cin, cout, sems):
        for j, w, mine, landed, to in copies(cout):
            pltpu.make_async_remote_copy(src_ref=mine, dst_ref=landed, send_sem=sems[0].at[n * j + w],
                                         recv_sem=sems[1].at[n * j + w], device_id=to, device_id_type=MESH).wait()

    return Comm(bufs, [SDS(b.shape, b.dtype) for b in bufs], {w: w for w in range(n)}, [3 * n, 3 * n], start, finish, done)


def gather_full_piece(bufs, done):
    n = len(bufs)

    def copies(cout):
        x, y, c, chips = _place()
        return [(n * j + w, cout[w].at[2 * x + y], cout[w].at[2 * cx + cy], (cx, cy, c))
                for j, (cx, cy) in enumerate(chips) for w in range(n)]

    def start(cin, cout, sems):
        for k, mine, _, to in copies(cout):
            pltpu.make_async_remote_copy(src_ref=mine, dst_ref=mine, send_sem=sems[0].at[k], recv_sem=sems[1].at[k],
                                         device_id=to, device_id_type=MESH).start()

    def finish(cin, cout, sems):
        for k, mine, landed, to in copies(cout):
            pltpu.make_async_remote_copy(src_ref=mine, dst_ref=landed, send_sem=sems[0].at[k], recv_sem=sems[1].at[k],
                                         device_id=to, device_id_type=MESH).wait()

    return Comm(bufs, [SDS(b.shape, b.dtype) for b in bufs], {w: w for w in range(n)}, [3 * n, 3 * n], start, finish, done)


def gather_d2d_piece(bufs, done):
    n = len(bufs)

    def copies(cout):
        x, y, c, chips = _place()
        out = []
        for j, (cx, cy) in enumerate(chips):
            for w in range(n):
                half = bufs[w].shape[1] // 2
                slot = cout[w].at[2 * cx + cy]
                out.append((n * j + w, _rows_half(slot, c, half), _rows_half(slot, 1 - c, half), (x, y, 1 - c)))
        return out

    def start(cin, cout, sems):
        for k, mine, _, to in copies(cout):
            pltpu.make_async_remote_copy(src_ref=mine, dst_ref=mine, send_sem=sems[0].at[k], recv_sem=sems[1].at[k],
                                         device_id=to, device_id_type=MESH).start()

    def finish(cin, cout, sems):
        for k, mine, theirs, to in copies(cout):
            pltpu.make_async_remote_copy(src_ref=mine, dst_ref=theirs, send_sem=sems[0].at[k], recv_sem=sems[1].at[k],
                                         device_id=to, device_id_type=MESH).wait()

    return Comm(bufs, [SDS(b.shape, b.dtype) for b in bufs], {w: w for w in range(n)}, [3 * n, 3 * n], start, finish, done)


def pair_piece(parts, done):
    n = len(parts)

    def copies(cin, cout):
        x, y, c, _ = _place()
        out = []
        for w in range(n):
            half = parts[w].shape[1] // 2
            out.append((w, cin[w].at[:, pl.ds(pl.multiple_of((1 - c) * half, 8), half), :], cout[w], (x, y, 1 - c)))
        return out

    def start(cin, cout, sems):
        for w, src, dst, to in copies(cin, cout):
            pltpu.make_async_remote_copy(src_ref=src, dst_ref=dst, send_sem=sems[0].at[w], recv_sem=sems[1].at[w],
                                         device_id=to, device_id_type=MESH).start()

    def finish(cin, cout, sems):
        for w, src, dst, to in copies(cin, cout):
            pltpu.make_async_remote_copy(src_ref=src, dst_ref=dst, send_sem=sems[0].at[w], recv_sem=sems[1].at[w],
                                         device_id=to, device_id_type=MESH).wait()

    return Comm(parts, [SDS((N_SHARD, a.shape[1] // 2, a.shape[2]), f32) for a in parts], {}, [n, n], start, finish, done)


def shard_piece(sums16, done):
    n = len(sums16)

    def copies(cin, cout):
        x, y, c, chips = _place()
        return [(n * j + w, cin[w].at[2 * cx + cy], cout[w].at[j], (cx, cy, c))
                for j, (cx, cy) in enumerate(chips) for w in range(n)]

    def start(cin, cout, sems):
        for k, src, dst, to in copies(cin, cout):
            pltpu.make_async_remote_copy(src_ref=src, dst_ref=dst, send_sem=sems[0].at[k], recv_sem=sems[1].at[k],
                                         device_id=to, device_id_type=MESH).start()

    def finish(cin, cout, sems):
        for k, src, dst, to in copies(cin, cout):
            pltpu.make_async_remote_copy(src_ref=src, dst_ref=dst, send_sem=sems[0].at[k], recv_sem=sems[1].at[k],
                                         device_id=to, device_id_type=MESH).wait()

    return Comm(sums16, [SDS((3,) + a.shape[1:], bf16) for a in sums16], {}, [3 * n, 3 * n], start, finish, done)


def place_shard(a, layer, idx, dtype):
    _, r, cdim = a.shape
    tr = min(r, 512)

    def body(idx_ref, a_ref, o_ref):
        o_ref[0] = a_ref[0].astype(dtype)

    return pl.pallas_call(
        body, name="place_shard",
        grid_spec=pltpu.PrefetchScalarGridSpec(
            num_scalar_prefetch=1, grid=(r // tr,),
            in_specs=[pl.BlockSpec((1, tr, cdim), lambda i, idx_ref: (layer, i, 0))],
            out_specs=pl.BlockSpec((1, tr, cdim), lambda i, idx_ref: (idx_ref[1], i, 0))),
        out_shape=SDS((N_SHARD, r, cdim), dtype),
        compiler_params=_params(("arbitrary",)),
    )(idx, a)


_KEYS = ("w_in", "w_out", "w_up", "w_dn")


class MeshWeights:
    def __init__(self, w_in, w_out, w_up, w_down, conv_dw_w, lru_conv_w, idx):
        self.idx = idx
        src = dict(w_in=w_in, w_out=w_out, w_up=w_up, w_dn=w_down)
        self.buf = {(n, l): place_shard(src[n], l, idx, bf16) for n in _KEYS for l in range(DEPTH)}
        self.conv = {(n, l): place_shard(a, l, idx, f32)
                     for n, a in (("cw", conv_dw_w), ("lw", lru_conv_w)) for l in range(DEPTH)}
        self.cache, self.parts, self.sum32, self.sum16, self.got = {}, {}, {}, {}, {}
        first, small = [("w_in", 0)], list(self.conv)

        def store_conv(outs):
            self.conv.update(zip(small, outs))

        standalone([self._gather(gather_ici_piece, first), gather_full_piece([self.conv[k] for k in small], store_conv)],
                   "gather_first_ici")
        standalone([self._gather(gather_d2d_piece, first)], "gather_first_d2d")

    def _gather(self, piece, keys):
        def done(outs):
            self.buf.update(zip(keys, outs))
        return piece([self.buf[k] for k in keys], done)

    def _pair(self, keys):
        def done(outs):
            for k, recv in zip(keys, outs):
                self.sum32[k], self.sum16[k] = chip_sum(self.parts[k], recv, self.idx)
        return pair_piece([self.parts[k] for k in keys], done)

    def _shard(self, keys):
        def done(outs):
            self.got.update(zip(keys, outs))
        return shard_piece([self.sum16[k] for k in keys], done)

    def conv_weights(self):
        out = []
        for n in ("cw", "lw"):
            a = jnp.stack([self.conv[(n, l)] for l in range(DEPTH)])
            out.append(a.transpose(0, 2, 1, 3).reshape(DEPTH, a.shape[2], N_SHARD * a.shape[3]))
        return out

    def weight(self, name, l):
        if (name, l) not in self.cache:
            b = self.buf[(name, l)]
            if name in ("w_in", "w_out"):
                b = b.reshape(-1, D_MODEL)
            self.cache[(name, l)] = b
        return self.cache[(name, l)]

    def host(self, point, l):
        ici, d2d = gather_ici_piece, gather_d2d_piece
        rest1 = [("w_out", 1), ("w_up", 1), ("w_dn", 1)]
        plan = {
            ("inproj_fwd", 0): lambda: [self._gather(ici, [("w_out", 0)])],
            ("attn_fwd", 0): lambda: [self._gather(ici, [("w_up", 0)]), self._gather(d2d, [("w_out", 0)])],
            ("branch_fwd", 0): lambda: [self._gather(ici, [("w_dn", 0)]), self._gather(d2d, [("w_up", 0)])],
            ("outproj_fwd", 0): lambda: [self._gather(d2d, [("w_dn", 0)]), self._gather(ici, [("w_in", 1)])],
            ("mlp_fwd", 0): lambda: [self._gather(ici, rest1), self._gather(d2d, [("w_in", 1)])],
            ("attn_fwd", 1): lambda: [self._gather(d2d, rest1)],
            ("attn_bwd", 1): lambda: [self._pair([("w_up", 1), ("w_dn", 1)])],
            ("mlp_bwd_act", 0): lambda: [self._shard([("w_up", 1), ("w_dn", 1)])],
            ("mlp_bwd_w", 0): lambda: [self._pair([("w_in", 1), ("w_out", 1)])],
            ("attn_bwd", 0): lambda: [self._shard([("w_in", 1), ("w_out", 1)]),
                                      self._pair([("w_up", 0), ("w_dn", 0), ("w_out", 0)])],
            ("branch_bwd_a", 0): lambda: [self._shard([("w_up", 0)])],
            ("branch_bwd_b", 0): lambda: [self._shard([("w_dn", 0), ("w_out", 0)])],
        }
        make = plan.get((point, l))
        return make() if make else None

    def grad(self, name, l, g):
        if name in ("w_in", "w_out"):
            g = g.reshape(N_SHARD, -1, D_MODEL)
        self.parts[(name, l)] = g

    def big_grads(self):
        return None

    def finish(self, small_vec):
        last = [("w_in", 0)]
        small = {}

        def keep(outs):
            small["buf"] = outs[0]

        standalone([self._pair(last), small_first_piece(small_vec, keep)], "pair_last")
        standalone([self._shard(last), small_second_piece(small["buf"], keep)], "shard_last")
        return self._totals(), small["buf"]

    def _totals(self):
        tots = []
        for n in _KEYS:
            t = None
            for l in reversed(range(DEPTH)):
                t = shard_sum(self.sum32[(n, l)], self.got[(n, l)], self.idx, l, t)
            tots.append(t)
        return tots


def chip_sum(g, recv, idx):
    _, r, cdim = g.shape
    half = r // 2
    tr = min(half, 512)
    nh = half // tr

    def body(idx_ref, g_ref, r_ref, o32_ref, o16_ref):
        tot = g_ref[0] + r_ref[0]
        o16_ref[0] = tot.astype(bf16)

        @pl.when(pl.program_id(0) == N_SHARD - 1)
        def _():
            o32_ref[...] = tot

    def slab(k, idx_ref):
        return lax.rem(idx_ref[1] + 1 + k, N_SHARD)

    def o16_map(k, i, idx_ref):
        return (slab(k, idx_ref), i, 0)

    return pl.pallas_call(
        body, name="chip_sum",
        grid_spec=pltpu.PrefetchScalarGridSpec(
            num_scalar_prefetch=1, grid=(N_SHARD, nh),
            in_specs=[pl.BlockSpec((1, tr, cdim), lambda k, i, idx_ref: (slab(k, idx_ref), idx_ref[0] * nh + i, 0)),
                      pl.BlockSpec((1, tr, cdim), lambda k, i, idx_ref: (slab(k, idx_ref), i, 0))],
            out_specs=[pl.BlockSpec((tr, cdim), lambda k, i, idx_ref: (jnp.where(k == N_SHARD - 1, i, 0), 0)),
                       pl.BlockSpec((1, tr, cdim), o16_map)]),
        out_shape=[SDS((half, cdim), f32), SDS((N_SHARD, half, cdim), bf16)],
        compiler_params=_params(("arbitrary", "arbitrary")),
    )(idx, g, recv)


def shard_sum(sum32, got16, idx, layer, prev):
    half, cdim = sum32.shape
    tr = min(half, 512)

    def body(idx_ref, a_ref, r0_ref, r1_ref, r2_ref, *rest):
        o_ref = rest[-1]
        o_ref[0, 0] = ((a_ref[...] + r0_ref[0].astype(f32)) + r1_ref[0].astype(f32)) + r2_ref[0].astype(f32)

    def rel(j):
        return pl.BlockSpec((1, tr, cdim), lambda i, idx_ref: (j, i, 0))

    in_specs = [pl.BlockSpec((tr, cdim), lambda i, idx_ref: (i, 0)), rel(0), rel(1), rel(2)]
    operands = [idx, sum32, got16, got16, got16]
    aliases = {}
    if prev is not None:
        in_specs.append(_HBM)
        operands.append(prev)
        aliases = {5: 0}
    return pl.pallas_call(
        body, name="shard_sum",
        grid_spec=pltpu.PrefetchScalarGridSpec(
            num_scalar_prefetch=1, grid=(half // tr,), in_specs=in_specs,
            out_specs=pl.BlockSpec((1, 1, tr, cdim), lambda i, idx_ref: (layer, idx_ref[0], i, 0))),
        out_shape=SDS((DEPTH, 2, half, cdim), f32), input_output_aliases=aliases,
        compiler_params=_params(("arbitrary",)),
    )(*operands)


def halves_exchange(tots):
    nw = len(tots)

    def body(*refs):
        bufs = refs[nw:2 * nw]
        send_sem, recv_sem = refs[2 * nw:]
        x, y, c, _ = _place()

        def copy(w, l, half_idx):
            return pltpu.make_async_remote_copy(
                src_ref=bufs[w].at[l, half_idx], dst_ref=bufs[w].at[l, half_idx], send_sem=send_sem.at[DEPTH * w + l],
                recv_sem=recv_sem.at[DEPTH * w + l], device_id=(x, y, 1 - c), device_id_type=MESH)

        sends = [copy(w, l, c) for w in range(nw) for l in range(DEPTH)]
        for cp in sends:
            cp.start()
        for w in range(nw):
            for l in range(DEPTH):
                copy(w, l, 1 - c).wait_recv()
        for cp in sends:
            cp.wait_send()

    return pl.pallas_call(
        body, name="halves_exchange", in_specs=[_HBM] * nw, out_specs=[_HBM] * nw,
        out_shape=[SDS(a.shape, f32) for a in tots], input_output_aliases={w: w for w in range(nw)},
        scratch_shapes=[pltpu.SemaphoreType.DMA((DEPTH * nw,)), pltpu.SemaphoreType.DMA((DEPTH * nw,))],
    )(*tots)


N_DEV = 8


def _dev_index(px, py, pc):
    return 4 * px + 2 * py + pc


def small_first_piece(vec, done):
    def copies(cin, cout):
        x, y, c, chips = _place()
        mine = cout[0].at[_dev_index(x, y, c)]
        peers = [(x, y, 1 - c)] + [(cx, cy, c) for cx, cy in chips]
        return mine, [(k, cout[0].at[_dev_index(*p)], p) for k, p in enumerate(peers)]

    def start(cin, cout, sems):
        mine, peers = copies(cin, cout)
        pltpu.make_async_copy(cin[0], mine, sems[2].at[0]).start()
        for k, _, to in peers:
            pltpu.make_async_remote_copy(src_ref=cin[0], dst_ref=mine, send_sem=sems[0].at[k], recv_sem=sems[1].at[k],
                                         device_id=to, device_id_type=MESH).start()

    def finish(cin, cout, sems):
        mine, peers = copies(cin, cout)
        for k, theirs, to in peers:
            pltpu.make_async_remote_copy(src_ref=cin[0], dst_ref=theirs, send_sem=sems[0].at[k], recv_sem=sems[1].at[k],
                                         device_id=to, device_id_type=MESH).wait()
        pltpu.make_async_copy(cin[0], mine, sems[2].at[0]).wait()

    return Comm([vec], [SDS((N_DEV,) + vec.shape, f32)], {}, [4, 4, 1], start, finish, done)


def small_second_piece(buf, done):
    def copies(cout):
        x, y, c, chips = _place()
        return [(j, cout[0].at[_dev_index(cx, cy, c)], cout[0].at[_dev_index(cx, cy, 1 - c)], (x, y, 1 - c))
                for j, (cx, cy) in enumerate(chips)]

    def start(cin, cout, sems):
        for j, mine, _, to in copies(cout):
            pltpu.make_async_remote_copy(src_ref=mine, dst_ref=mine, send_sem=sems[0].at[j], recv_sem=sems[1].at[j],
                                         device_id=to, device_id_type=MESH).start()

    def finish(cin, cout, sems):
        for j, mine, theirs, to in copies(cout):
            pltpu.make_async_remote_copy(src_ref=mine, dst_ref=theirs, send_sem=sems[0].at[j], recv_sem=sems[1].at[j],
                                         device_id=to, device_id_type=MESH).wait()

    return Comm([buf], [SDS(buf.shape, f32)], {0: 0}, [3, 3], start, finish, done)


def small_sum(buf):
    def body(b_ref, o_ref):
        acc = b_ref[0]
        for d in range(1, N_DEV):
            acc = acc + b_ref[d]
        o_ref[...] = acc

    vm = pl.BlockSpec(memory_space=pltpu.VMEM)
    return pl.pallas_call(body, name="small_sum", in_specs=[vm], out_specs=vm, out_shape=SDS(buf.shape[1:], f32))(buf)


def _adamw_math(w, g, m, v):
    m = ADAM_B1 * m + (1.0 - ADAM_B1) * g
    v = ADAM_B2 * v + (1.0 - ADAM_B2) * (g * g)
    m_hat = m / (1.0 - ADAM_B1 ** ADAM_STEP)
    v_hat = v / (1.0 - ADAM_B2 ** ADAM_STEP)
    delta = -ADAM_LR * (m_hat / (jnp.sqrt(v_hat) + ADAM_EPS) + ADAM_WD * w)
    return delta, m, v


def adamw_big(w, g, m, v):
    _, r, cdim = w.shape
    tr = _row_tile(r, 256)

    def body(w_ref, g_ref, m_ref, v_ref, go_ref, d_ref, mo_ref, vo_ref):
        g = g_ref[...]
        d, mm, vv = _adamw_math(w_ref[...], g, m_ref[...], v_ref[...])
        go_ref[...] = g
        d_ref[...] = d
        mo_ref[...] = mm
        vo_ref[...] = vv

    blk = pl.BlockSpec((1, tr, cdim), lambda l, i: (l, i, 0))
    return pl.pallas_call(
        body, name="adamw_big", grid=(DEPTH, r // tr), in_specs=[blk] * 4, out_specs=[blk] * 4,
        out_shape=[SDS(w.shape, f32)] * 4, compiler_params=_params(("parallel", "parallel")),
    )(w, g, m, v)


def adamw_small(ws, gs, ms, vs):
    n = len(ws)

    def body(*refs):
        w_r, g_r, m_r, v_r = refs[:n], refs[n:2 * n], refs[2 * n:3 * n], refs[3 * n:4 * n]
        d_o, m_o, v_o = refs[4 * n:5 * n], refs[5 * n:6 * n], refs[6 * n:7 * n]
        for k in range(n):
            d, mm, vv = _adamw_math(w_r[k][...], g_r[k][...], m_r[k][...], v_r[k][...])
            d_o[k][...] = d
            m_o[k][...] = mm
            v_o[k][...] = vv

    vm = pl.BlockSpec(memory_space=pltpu.VMEM)
    shapes = [SDS(a.shape, f32) for a in ws]
    outs = pl.pallas_call(
        body, name="adamw_small", in_specs=[vm] * (4 * n), out_specs=[vm] * (3 * n), out_shape=shapes * 3,
    )(*ws, *gs, *ms, *vs)
    return outs[:n], outs[n:2 * n], outs[2 * n:]


_BIG = ("w_in", "w_out", "w_up", "w_down")
_WEIGHTS = ("norm1", "w_in", "attn_sinks", "conv_dw_w", "conv_dw_b", "conv_ln_g", "conv_ln_b", "lru_conv_w", "lru_conv_b",
            "lru_wa", "lru_ba", "lru_wx", "lru_bx", "lru_lambda", "mix_norm", "w_out", "norm2", "w_up", "w_down", "final_norm")
_SMALL = tuple(n for n in _WEIGHTS if n not in _BIG)
_SMALL_FULL_SHAPE = dict(
    norm1=(DEPTH, D_MODEL), attn_sinks=(DEPTH, N_HEADS), conv_dw_w=(DEPTH, CONV_K, CONV_W), conv_dw_b=(DEPTH, CONV_W),
    conv_ln_g=(DEPTH, CONV_W), conv_ln_b=(DEPTH, CONV_W), lru_conv_w=(DEPTH, LRU_K, LRU_W), lru_conv_b=(DEPTH, LRU_W),
    lru_wa=(DEPTH, LRU_HEADS, 64, 64), lru_ba=(DEPTH, LRU_HEADS, 64), lru_wx=(DEPTH, LRU_HEADS, 64, 64),
    lru_bx=(DEPTH, LRU_HEADS, 64), lru_lambda=(DEPTH, LRU_W), mix_norm=(DEPTH, D_MODEL), norm2=(DEPTH, D_MODEL),
    final_norm=(D_MODEL,))
_CHANNEL_SHARDED = ("conv_dw_w", "lru_conv_w")


def _pad_lanes(n):
    return -(-n // LANES) * LANES


def _pack(named):
    flat = []
    for a in named:
        a = a.reshape(-1)
        flat.append(jnp.pad(a, (0, _pad_lanes(a.shape[0]) - a.shape[0])))
    v = jnp.concatenate(flat)
    rows = -(-v.shape[0] // (8 * LANES)) * 8
    return jnp.pad(v, (0, rows * LANES - v.shape[0])).reshape(rows, LANES)


def _unpack(vec, shapes):
    flat = vec.reshape(-1)
    out, off = [], 0
    for shp in shapes:
        n = math.prod(shp)
        out.append(flat[off:off + n].reshape(shp))
        off += _pad_lanes(n)
    return out


def _as2d(a):
    return a.reshape(-1, a.shape[-1]) if a.ndim > 1 else a.reshape(1, -1)


def kernel(x, norm1, w_in, attn_sinks, conv_dw_w, conv_dw_b, conv_ln_g, conv_ln_b, lru_conv_w, lru_conv_b, lru_wa, lru_ba, lru_wx, lru_bx, lru_lambda, mix_norm, w_out, norm2, w_up, w_down, final_norm, loss_target, m_norm1, m_w_in, m_attn_sinks, m_conv_dw_w, m_conv_dw_b, m_conv_ln_g, m_conv_ln_b, m_lru_conv_w, m_lru_conv_b, m_lru_wa, m_lru_ba, m_lru_wx, m_lru_bx, m_lru_lambda, m_mix_norm, m_w_out, m_norm2, m_w_up, m_w_down, m_final_norm, v_norm1, v_w_in, v_attn_sinks, v_conv_dw_w, v_conv_dw_b, v_conv_ln_g, v_conv_ln_b, v_lru_conv_w, v_lru_conv_b, v_lru_wa, v_lru_ba, v_lru_wx, v_lru_bx, v_lru_lambda, v_mix_norm, v_w_out, v_norm2, v_w_up, v_w_down, v_final_norm):
    wts = dict(norm1=norm1, w_in=w_in, attn_sinks=attn_sinks, conv_dw_w=conv_dw_w, conv_dw_b=conv_dw_b, conv_ln_g=conv_ln_g,
               conv_ln_b=conv_ln_b, lru_conv_w=lru_conv_w, lru_conv_b=lru_conv_b, lru_wa=lru_wa, lru_ba=lru_ba, lru_wx=lru_wx,
               lru_bx=lru_bx, lru_lambda=lru_lambda, mix_norm=mix_norm, w_out=w_out, norm2=norm2, w_up=w_up, w_down=w_down,
               final_norm=final_norm)
    mom = dict(norm1=m_norm1, w_in=m_w_in, attn_sinks=m_attn_sinks, conv_dw_w=m_conv_dw_w, conv_dw_b=m_conv_dw_b,
               conv_ln_g=m_conv_ln_g, conv_ln_b=m_conv_ln_b, lru_conv_w=m_lru_conv_w, lru_conv_b=m_lru_conv_b, lru_wa=m_lru_wa,
               lru_ba=m_lru_ba, lru_wx=m_lru_wx, lru_bx=m_lru_bx, lru_lambda=m_lru_lambda, mix_norm=m_mix_norm, w_out=m_w_out,
               norm2=m_norm2, w_up=m_w_up, w_down=m_w_down, final_norm=m_final_norm)
    var = dict(norm1=v_norm1, w_in=v_w_in, attn_sinks=v_attn_sinks, conv_dw_w=v_conv_dw_w, conv_dw_b=v_conv_dw_b,
               conv_ln_g=v_conv_ln_g, conv_ln_b=v_conv_ln_b, lru_conv_w=v_lru_conv_w, lru_conv_b=v_lru_conv_b, lru_wa=v_lru_wa,
               lru_ba=v_lru_ba, lru_wx=v_lru_wx, lru_bx=v_lru_bx, lru_lambda=v_lru_lambda, mix_norm=v_mix_norm, w_out=v_w_out,
               norm2=v_norm2, w_up=v_w_up, w_down=v_w_down, final_norm=v_final_norm)

    c_idx = lax.axis_index("c").astype(jnp.int32)
    s_idx = (2 * lax.axis_index("x") + lax.axis_index("y")).astype(jnp.int32)
    idx = jnp.stack([c_idx, s_idx])

    for d in (wts, mom, var):
        d["w_in"] = d["w_in"].transpose(0, 2, 1)

    cs = MeshWeights(wts["w_in"], w_out, w_up, w_down, conv_dw_w, lru_conv_w, idx)
    sp = {n: wts[n] for n in _SMALL}
    sp["conv_dw_w"], sp["lru_conv_w"] = cs.conv_weights()
    loss_blk, grad_x, _, small_g, d_gf = train_local(x[0], loss_target[0], sp, cs)

    stacked = [jnp.stack([small_g[l][n] for l in range(DEPTH)]) for n in _SMALL if n != "final_norm"]
    tots, gathered = cs.finish(_pack(stacked + [d_gf, loss_blk[0, 0:1]]))
    grads_big = {n: a.reshape(wts[n].shape) for n, a in zip(_BIG, halves_exchange(tots))}
    summed = small_sum(gathered)
    names = [n for n in _SMALL if n != "final_norm"] + ["final_norm"]
    unpacked = _unpack(summed, [_SMALL_FULL_SHAPE[n] for n in names] + [(1,)])
    loss = unpacked[-1][0]
    grads = dict(zip(names, unpacked[:-1]))
    for n in _CHANNEL_SHARDED:
        width = wts[n].shape[-1]
        grads[n] = lax.dynamic_slice_in_dim(grads[n], s_idx * width, width, axis=2)
    grads.update(grads_big)

    delta, new_m, new_v = {}, {}, {}
    for n in _BIG:
        grads[n], delta[n], new_m[n], new_v[n] = adamw_big(wts[n], grads[n], mom[n], var[n])
    for d in (grads, delta, new_m, new_v):
        d["w_in"] = d["w_in"].transpose(0, 2, 1)
    sm = list(_SMALL)
    d_s, m_s, v_s = adamw_small([_as2d(wts[n]) for n in sm], [_as2d(grads[n]) for n in sm],
                                [_as2d(mom[n]) for n in sm], [_as2d(var[n]) for n in sm])
    for k, n in enumerate(sm):
        delta[n], new_m[n], new_v[n] = (a.reshape(wts[n].shape) for a in (d_s[k], m_s[k], v_s[k]))

    return (loss, grad_x[None], *[grads[n] for n in _WEIGHTS], *[delta[n] for n in _WEIGHTS],
            *[new_m[n] for n in _WEIGHTS], *[new_v[n] for n in _WEIGHTS])
```

```python
import functools
import math

import jax
import jax.numpy as jnp
from jax import lax
from jax.experimental import pallas as pl
from jax.experimental.pallas import tpu as pltpu

f32 = jnp.float32
bf16 = jnp.bfloat16
SDS = jax.ShapeDtypeStruct

D_MODEL = 1024
DEPTH = 2
ATTN_W = 512
KV_W = 128
HEAD_DIM = 64
N_HEADS = 8
BLOCK = 128
CONV_W = 256
CONV_K = 31
LRU_W = 256
LRU_K = 4
LRU_HEADS = 4
LRU_C = 8.0
IN_W = 1792
D_FF = 4096
N_SHARD = 4
FF_CHUNK = D_FF // N_SHARD
RMS_EPS = 1e-6
LN_EPS = 1e-5
MASK_VALUE = -1e30
HALO = 32
LANES = 128
VMEM_LIMIT = 56 * 1024 * 1024

ADAM_LR = 0.001
ADAM_B1 = 0.9
ADAM_B2 = 0.999
ADAM_EPS = 1e-08
ADAM_WD = 0.01
ADAM_STEP = 10

MESH = pl.DeviceIdType.MESH


def _dot(a, b):
    return jnp.dot(a, b, preferred_element_type=f32)


def _dot_nt(a, b):
    return lax.dot_general(a, b, (((1,), (1,)), ((), ())), preferred_element_type=f32)


def _dot_tn(a, b):
    return lax.dot_general(a, b, (((0,), (0,)), ((), ())), preferred_element_type=f32)


def _rms_fwd(x, g):
    r = lax.rsqrt(jnp.mean(x * x, axis=-1, keepdims=True) + RMS_EPS)
    return x * r * g, r


def _rms_bwd(dy, x, r, g):
    t = dy * g
    dx = r * t - x * (r * r * r) * jnp.mean(t * x, axis=-1, keepdims=True)
    dg = jnp.sum(dy * x * r, axis=0, keepdims=True)
    return dx, dg


def _sigmoid(x):
    return jax.nn.sigmoid(x)


_GELU_K = math.sqrt(2.0 / math.pi)


def _gelu(x):
    t = jnp.tanh(_GELU_K * (x + 0.044715 * x * x * x))
    return 0.5 * x * (1.0 + t), t


def _gelu_grad(x, t):
    return 0.5 * (1.0 + t) + 0.5 * x * (1.0 - t * t) * _GELU_K * (1.0 + 3.0 * 0.044715 * x * x)


def _log1p(x):
    return jnp.where(x < 1e-4, x - 0.5 * x * x, jnp.log(1.0 + x))


def _softplus(x):
    return jnp.maximum(x, 0.0) + _log1p(jnp.exp(-jnp.abs(x)))


def _neg_expm1(x):
    series = -x * (1.0 + 0.5 * x * (1.0 + x * (1.0 / 3.0) * (1.0 + 0.25 * x)))
    return jnp.where(x > -0.01, series, 1.0 - jnp.exp(x))


def _sublane_rolls(x, count, forward):
    n = x.shape[0]
    return [x if b == 0 else pltpu.roll(x, b if forward else n - b, 0) for b in range(count)]


def _conv_taps(xpad, w, k_width):
    t_rows = xpad.shape[0] - HALO
    rolled = _sublane_rolls(xpad, min(k_width, 8), forward=True)
    acc = None
    for k in range(k_width):
        hi, lo = divmod((k_width - 1) - k, 8)
        term = rolled[lo][HALO - 8 * hi:HALO - 8 * hi + t_rows] * w[k:k + 1, :]
        acc = term if acc is None else acc + term
    return acc


def _conv_taps_bwd(dpad, upad, w, k_width, t_rows):
    n_lo = min(k_width, 8)
    d_rolled = _sublane_rolls(dpad, n_lo, forward=False)
    u_rolled = _sublane_rolls(upad, n_lo, forward=True)
    d_in = None
    dw_rows = []
    d_out = dpad[:t_rows]
    for k in range(k_width):
        hi, lo = divmod((k_width - 1) - k, 8)
        term = d_rolled[lo][8 * hi:8 * hi + t_rows] * w[k:k + 1, :]
        d_in = term if d_in is None else d_in + term
        us = u_rolled[lo][HALO - 8 * hi:HALO - 8 * hi + t_rows]
        dw_rows.append(jnp.sum(d_out * us, axis=0, keepdims=True))
    return d_in, dw_rows


SUBLANES = 8


def _scan_fwd(a, b, h0):
    t_rows = a.shape[0]
    sub = jnp.bitwise_and(lax.broadcasted_iota(jnp.int32, a.shape, 0), SUBLANES - 1)
    for d in (1, 2, 4):
        a_sh = jnp.where(sub < d, 1.0, pltpu.roll(a, d, 0))
        b_sh = jnp.where(sub < d, 0.0, pltpu.roll(b, d, 0))
        b = a * b_sh + b
        a = a * a_sh
    out, carry = [], h0
    for g in range(t_rows // SUBLANES):
        rows = slice(g * SUBLANES, (g + 1) * SUBLANES)
        hg = a[rows] * carry + b[rows]
        out.append(hg)
        carry = hg[SUBLANES - 1:SUBLANES]
    return jnp.concatenate(out, axis=0)


def _scan_bwd(a, b, l_end):
    t_rows = a.shape[0]
    sub = jnp.bitwise_and(lax.broadcasted_iota(jnp.int32, a.shape, 0), SUBLANES - 1)
    for d in (1, 2, 4):
        a_sh = jnp.where(sub >= SUBLANES - d, 1.0, pltpu.roll(a, t_rows - d, 0))
        b_sh = jnp.where(sub >= SUBLANES - d, 0.0, pltpu.roll(b, t_rows - d, 0))
        b = b + a * b_sh
        a = a * a_sh
    out, carry = [], l_end
    for g in reversed(range(t_rows // SUBLANES)):
        rows = slice(g * SUBLANES, (g + 1) * SUBLANES)
        lg = b[rows] + a[rows] * carry
        out.append(lg)
        carry = lg[0:1]
    return jnp.concatenate(out[::-1], axis=0)


def _full(shape, single=False):
    nd = len(shape)
    if single:
        return pl.BlockSpec(shape, lambda *_: (0,) * nd, pipeline_mode=pl.Buffered(1))
    return pl.BlockSpec(shape, lambda *_: (0,) * nd)


def _params(sem, vmem=None):
    return pltpu.CompilerParams(dimension_semantics=sem, vmem_limit_bytes=vmem)


def _tile(s):
    return min(512, s)


def _row_tile(rows, cap):
    return next(t for t in range(min(cap, rows) // 8 * 8, 0, -8) if rows % t == 0)


def inproj_fwd(h, g1, w_in_t, pieces=None):
    s = h.shape[0]
    tm = _tile(s)

    def body(h_ref, g_ref, w_ref, z_ref):
        hn, _ = _rms_fwd(h_ref[...], g_ref[...])
        z_ref[...] = _dot_nt(hn.astype(bf16), w_ref[...]).astype(bf16)

    return _pcall(
        body, pieces, name="inproj_fwd", grid=(s // tm,),
        in_specs=[pl.BlockSpec((tm, D_MODEL), lambda i: (i, 0)), _full((1, D_MODEL)), _full((IN_W, D_MODEL))],
        out_specs=[pl.BlockSpec((tm, IN_W), lambda i: (i, 0))],
        out_shape=[SDS((s, IN_W), bf16)],
        operands=[h, g1, w_in_t], sem=("parallel",), vmem=VMEM_LIMIT)[0]


ATT_ROWS = N_HEADS * BLOCK
ATT_BLOCKS_PER_STEP = 8


def _attn_bias():
    qi = jnp.arange(BLOCK)[None, :]
    key = jnp.arange(2 * BLOCK)[:, None]
    band = (key > qi) & (key <= qi + BLOCK)
    first = band & (key >= BLOCK)
    mask = jnp.where(jnp.stack([first, band]), 0.0, MASK_VALUE).astype(f32)
    return jnp.tile(mask, (1, 1, N_HEADS // 2))


def _attn_band(kvc, kvp):
    kb = jnp.concatenate([kvp[:, :KV_W], kvc[:, :KV_W]], axis=0)
    vb = jnp.concatenate([kvp[:, KV_W:], kvc[:, KV_W:]], axis=0)
    lane = lax.broadcasted_iota(jnp.int32, kb.shape, 1)
    kb_sw = pltpu.roll(kb, HEAD_DIM, 1)
    vb_sw = pltpu.roll(vb, HEAD_DIM, 1)
    kx = [jnp.where(lane < HEAD_DIM, kb, kb_sw), jnp.where(lane >= HEAD_DIM, kb, kb_sw)]
    vx = [jnp.where(lane < HEAD_DIM, vb, vb_sw), jnp.where(lane >= HEAD_DIM, vb, vb_sw)]
    return kx, vx


def _stack_heads(x, mlo):
    zero = jnp.zeros((BLOCK, LANES), x.dtype)
    out = []
    for hk in range(2):
        parts = []
        for j in (2 * hk, 2 * hk + 1):
            xj = x[:, j * LANES:(j + 1) * LANES]
            parts += [jnp.where(mlo, xj, zero), jnp.where(mlo, zero, xj)]
        out.append(jnp.concatenate(parts, axis=0))
    return out


def _unstack_heads(y, mlo):
    cols = []
    for hk in range(2):
        for t in range(2):
            base = 2 * t * BLOCK
            cols.append(jnp.where(mlo, y[hk][base:base + BLOCK], y[hk][base + BLOCK:base + 2 * BLOCK]))
    return jnp.concatenate(cols, axis=1)


def _attn_probs(q4, kx, bias_t, sink_row):
    out = []
    half = ATT_ROWS // 2
    for hk in range(2):
        s = _dot_nt(kx[hk], q4[hk]) + bias_t
        sink = sink_row[:, hk * half:(hk + 1) * half]
        m = jnp.maximum(jnp.max(s, axis=0, keepdims=True), sink)
        p = jnp.exp(s - m)
        e_sink = jnp.exp(sink - m)
        inv = 1.0 / (jnp.sum(p, axis=0, keepdims=True) + e_sink)
        out.append((p * inv, e_sink * inv))
    return out


def attn_fwd(z, sink_row, bias, g_a, pieces=None):
    s = z.shape[0]
    per = min(ATT_BLOCKS_PER_STEP, s // BLOCK)
    tq = per * BLOCK

    def body(q_ref, kv_ref, kvp_ref, sk_ref, b_ref, g_ref, o_ref, y_ref):
        n = pl.program_id(0)
        mlo = lax.broadcasted_iota(jnp.int32, (BLOCK, LANES), 1) < HEAD_DIM
        for b in range(per):
            rows = slice(b * BLOCK, (b + 1) * BLOCK)
            kvp = kvp_ref[...] if b == 0 else kv_ref[(b - 1) * BLOCK:b * BLOCK, :]
            bias_b = b_ref[jnp.minimum(n, 1)] if b == 0 else b_ref[1]
            kx, vx = _attn_band(kv_ref[rows, :], kvp)
            q4 = _stack_heads(q_ref[rows, :] * (HEAD_DIM ** -0.5), mlo)
            probs = _attn_probs(q4, kx, bias_b, sk_ref[...])
            o = _unstack_heads([_dot_tn(probs[hk][0].astype(bf16), vx[hk]) for hk in range(2)], mlo)
            o_ref[rows, :] = o.astype(bf16)
            y, _ = _rms_fwd(o, g_ref[...])
            y_ref[rows, :] = y.astype(bf16)

    return _pcall(
        body, pieces, name="attn_fwd", grid=(s // tq,),
        in_specs=[pl.BlockSpec((tq, ATTN_W), lambda n: (n, 0)),
                  pl.BlockSpec((tq, 2 * KV_W), lambda n: (n, 2)),
                  pl.BlockSpec((BLOCK, 2 * KV_W), lambda n: (jnp.maximum(n * per - 1, 0), 2)),
                  _full((1, ATT_ROWS)), _full((2, 2 * BLOCK, ATT_ROWS // 2)), _full((1, ATTN_W))],
        out_specs=[pl.BlockSpec((tq, ATTN_W), lambda n: (n, 0)), pl.BlockSpec((tq, ATTN_W), lambda n: (n, 0))],
        out_shape=[SDS((s, ATTN_W), bf16), SDS((s, ATTN_W), bf16)],
        operands=[z, z, z, sink_row, bias, g_a], sem=("parallel",))


def _lru_gates(xc, wa, ba, wx, bx, lam):
    xcb = xc.astype(bf16)
    r = _sigmoid(_dot(xcb, wa) + ba)
    ig = _sigmoid(_dot(xcb, wx) + bx)
    sp = _softplus(-lam)
    la = (-LRU_C * r) * sp
    a = jnp.exp(la)
    mult = jnp.sqrt(_neg_expm1(2.0 * la))
    return r, ig, sp, la, a, mult


def branch_fwd(z, p, pieces=None):
    s = z.shape[0]
    tm = _tile(s)
    hb = tm // HALO

    def body(cv_ref, cg_ref, rx_ref, rg_ref, cvh_ref, cgh_ref, rxh_ref,
             cw_ref, cb_ref, lng_ref, lnb_ref, lw_ref, lb_ref, wa_ref, ba_ref, wx_ref, bx_ref, lam_ref, gc_ref, gl_ref,
             conv_ref, hst_ref, nc_ref, nl_ref, carry_ref):
        i = pl.program_id(0)
        first = i == 0

        @pl.when(first)
        def _():
            carry_ref[...] = jnp.zeros_like(carry_ref)

        cval = cv_ref[...].astype(f32)
        u = cval * _sigmoid(cg_ref[...].astype(f32))
        hu = jnp.where(first, 0.0, cvh_ref[...].astype(f32) * _sigmoid(cgh_ref[...].astype(f32)))
        conv = _conv_taps(jnp.concatenate([hu, u], axis=0), cw_ref[...], CONV_K) + cb_ref[...]
        conv_ref[...] = conv
        mu = jnp.mean(conv, axis=-1, keepdims=True)
        xm = conv - mu
        rstd = lax.rsqrt(jnp.mean(xm * xm, axis=-1, keepdims=True) + LN_EPS)
        ln = xm * rstd * lng_ref[...] + lnb_ref[...]
        yc = ln * _sigmoid(ln)
        nc, _ = _rms_fwd(yc, gc_ref[...])
        nc_ref[...] = nc.astype(bf16)

        rx = rx_ref[...].astype(f32)
        hrx = jnp.where(first, 0.0, rxh_ref[...].astype(f32))
        xc = _conv_taps(jnp.concatenate([hrx, rx], axis=0), lw_ref[...], LRU_K) + lb_ref[...]
        r, ig, sp, la, a, mult = _lru_gates(xc, wa_ref[...], ba_ref[...], wx_ref[...], bx_ref[...], lam_ref[...])
        gx = mult * (ig * xc)
        hs = _scan_fwd(a, gx, carry_ref[0:1, :])
        carry_ref[...] = jnp.broadcast_to(hs[tm - 1:tm, :], carry_ref.shape)
        hst_ref[...] = hs
        gl, _ = _gelu(rg_ref[...].astype(f32))
        nl, _ = _rms_fwd(hs * gl, gl_ref[...])
        nl_ref[...] = nl.astype(bf16)

    def col(c):
        return pl.BlockSpec((tm, CONV_W), lambda i: (i, c))

    def halo(c):
        return pl.BlockSpec((HALO, CONV_W), lambda i: (jnp.maximum(i * hb - 1, 0), c))

    small = [p["cw"], p["cb"], p["lng"], p["lnb"], p["lw"], p["lb"], p["wa"], p["ba"], p["wx"], p["bx"], p["lam"],
             p["gc"], p["gl"]]
    row = pl.BlockSpec((tm, CONV_W), lambda i: (i, 0))
    return _pcall(
        body, pieces, name="branch_fwd", grid=(s // tm,),
        in_specs=[col(3), col(4), col(5), col(6), halo(3), halo(4), halo(5)] + [_full(a.shape) for a in small],
        out_specs=[row, row, row, row],
        out_shape=[SDS((s, CONV_W), f32), SDS((s, LRU_W), f32), SDS((s, CONV_W), bf16), SDS((s, LRU_W), bf16)],
        scratch_shapes=[pltpu.VMEM((8, LRU_W), f32)],
        operands=[z, z, z, z, z, z, z, *small], sem=("arbitrary",))


def outproj_fwd(ya, yc, yl, h, w_out, g2, pieces=None):
    s = h.shape[0]
    tm = _tile(s)

    def body(ya_ref, yc_ref, yl_ref, h_ref, w_ref, g_ref, h1_ref, hn_ref):
        y = jnp.concatenate([ya_ref[...], yc_ref[...], yl_ref[...]], axis=1)
        h1 = h_ref[...] + _dot(y, w_ref[...])
        h1_ref[...] = h1
        hn, _ = _rms_fwd(h1, g_ref[...])
        hn_ref[...] = hn.astype(bf16)

    return _pcall(
        body, pieces, name="outproj_fwd", grid=(s // tm,),
        in_specs=[pl.BlockSpec((tm, ATTN_W), lambda i: (i, 0)), pl.BlockSpec((tm, CONV_W), lambda i: (i, 0)),
                  pl.BlockSpec((tm, LRU_W), lambda i: (i, 0)), pl.BlockSpec((tm, D_MODEL), lambda i: (i, 0)),
                  _full((D_MODEL, D_MODEL)), _full((1, D_MODEL))],
        out_specs=[pl.BlockSpec((tm, D_MODEL), lambda i: (i, 0)), pl.BlockSpec((tm, D_MODEL), lambda i: (i, 0))],
        out_shape=[SDS((s, D_MODEL), f32), SDS((s, D_MODEL), bf16)],
        operands=[ya, yc, yl, h, w_out, g2], sem=("parallel",), vmem=VMEM_LIMIT)


def mlp_fwd(hn2, h1, w_up, w_dn, pieces=None):
    s = h1.shape[0]
    tm = _tile(s)

    def body(x_ref, h_ref, wu_ref, wd_ref, up_ref, h2_ref):
        h2_ref[...] = _mlp_chunks(x_ref[...], h_ref[...], wu_ref, wd_ref, up_ref)

    return _pcall(
        body, pieces, name="mlp_fwd", grid=(s // tm,),
        in_specs=[pl.BlockSpec((tm, D_MODEL), lambda i: (i, 0)), pl.BlockSpec((tm, D_MODEL), lambda i: (i, 0)),
                  _full((N_SHARD, D_MODEL, FF_CHUNK), single=True), _full((N_SHARD, FF_CHUNK, D_MODEL), single=True)],
        out_specs=[pl.BlockSpec((tm, D_FF), lambda i: (i, 0)), pl.BlockSpec((tm, D_MODEL), lambda i: (i, 0))],
        out_shape=[SDS((s, D_FF), bf16), SDS((s, D_MODEL), f32)],
        operands=[hn2, h1, w_up, w_dn], sem=("parallel",), vmem=VMEM_LIMIT)


def _mlp_chunks(x, acc, wu_ref, wd_ref, up_ref):
    for c in range(N_SHARD):
        u = _dot(x, wu_ref[c])
        up_ref[:, c * FF_CHUNK:(c + 1) * FF_CHUNK] = u.astype(bf16)
        act = jnp.square(jnp.maximum(u, 0.0)).astype(bf16)
        acc = acc + _dot(act, wd_ref[c])
    return acc


def _final_tile(x, tgt, g, dh_ref, loss_ref, dg_ref):
    y, r = _rms_fwd(x, g)
    err = y - tgt
    part = 0.5 * jnp.sum(jnp.mean(err * err, axis=-1, keepdims=True), axis=0, keepdims=True)
    loss_ref[...] += jnp.broadcast_to(part, loss_ref.shape)
    dx, dg = _rms_bwd(err * (1.0 / D_MODEL), x, r, g)
    dh_ref[...] = dx
    dg_ref[...] += dg


def last_layer_fwd(ya, yc, yl, h, w_out, g2, w_up, w_dn, tgt, gf):
    s = h.shape[0]
    tm = min(256, s)

    def body(ya_ref, yc_ref, yl_ref, h_ref, wo_ref, g2_ref, wu_ref, wd_ref, t_ref, gf_ref,
             h1_ref, hn_ref, up_ref, dh_ref, loss_ref, dg_ref):
        i = pl.program_id(0)

        @pl.when(i == 0)
        def _():
            loss_ref[...] = jnp.zeros_like(loss_ref)
            dg_ref[...] = jnp.zeros_like(dg_ref)

        y = jnp.concatenate([ya_ref[...], yc_ref[...], yl_ref[...]], axis=1)
        h1 = h_ref[...] + _dot(y, wo_ref[...])
        h1_ref[...] = h1
        hn, _ = _rms_fwd(h1, g2_ref[...])
        hn = hn.astype(bf16)
        hn_ref[...] = hn
        h2 = _mlp_chunks(hn, h1, wu_ref, wd_ref, up_ref)
        _final_tile(h2, t_ref[...], gf_ref[...], dh_ref, loss_ref, dg_ref)

    def rowb(w):
        return pl.BlockSpec((tm, w), lambda i: (i, 0))

    return pl.pallas_call(
        body, name="last_layer_fwd", grid=(s // tm,),
        in_specs=[rowb(ATTN_W), rowb(CONV_W), rowb(LRU_W), rowb(D_MODEL), _full((D_MODEL, D_MODEL), single=True),
                  _full((1, D_MODEL)), _full((N_SHARD, D_MODEL, FF_CHUNK), single=True),
                  _full((N_SHARD, FF_CHUNK, D_MODEL), single=True), rowb(D_MODEL), _full((1, D_MODEL))],
        out_specs=[rowb(D_MODEL), rowb(D_MODEL), rowb(D_FF), rowb(D_MODEL), _full((8, LANES)), _full((1, D_MODEL))],
        out_shape=[SDS((s, D_MODEL), f32), SDS((s, D_MODEL), bf16), SDS((s, D_FF), bf16), SDS((s, D_MODEL), f32),
                   SDS((8, LANES), f32), SDS((1, D_MODEL), f32)],
        compiler_params=_params(("arbitrary",), VMEM_LIMIT),
    )(ya, yc, yl, h, w_out, g2, w_up, w_dn, tgt, gf)


def mlp_bwd_act(dh, up, h1, g2, w_up, w_dn, pieces=None):
    s = dh.shape[0]
    tm = _tile(s)

    def body(dh_ref, up_ref, h1_ref, g_ref, wu_ref, wd_ref, dup_ref, dh1_ref, dg_ref):
        i = pl.program_id(0)

        @pl.when(i == 0)
        def _():
            dg_ref[...] = jnp.zeros_like(dg_ref)

        dh = dh_ref[...]
        dhb = dh.astype(bf16)
        d_hn = jnp.zeros((tm, D_MODEL), f32)
        for c in range(N_SHARD):
            d_act = _dot_nt(dhb, wd_ref[c])
            u = up_ref[:, c * FF_CHUNK:(c + 1) * FF_CHUNK].astype(f32)
            d_u = (d_act * (2.0 * jnp.maximum(u, 0.0))).astype(bf16)
            dup_ref[:, c * FF_CHUNK:(c + 1) * FF_CHUNK] = d_u
            d_hn = d_hn + _dot_nt(d_u, wu_ref[c])
        x = h1_ref[...]
        g = g_ref[...]
        _, r = _rms_fwd(x, g)
        dx, dg = _rms_bwd(d_hn, x, r, g)
        dh1_ref[...] = dh + dx
        dg_ref[...] += dg

    return _pcall(
        body, pieces, name="mlp_bwd_act", grid=(s // tm,),
        in_specs=[pl.BlockSpec((tm, D_MODEL), lambda i: (i, 0)), pl.BlockSpec((tm, D_FF), lambda i: (i, 0)),
                  pl.BlockSpec((tm, D_MODEL), lambda i: (i, 0)), _full((1, D_MODEL)),
                  _full((N_SHARD, D_MODEL, FF_CHUNK), single=True), _full((N_SHARD, FF_CHUNK, D_MODEL), single=True)],
        out_specs=[pl.BlockSpec((tm, D_FF), lambda i: (i, 0)), pl.BlockSpec((tm, D_MODEL), lambda i: (i, 0)),
                   _full((1, D_MODEL))],
        out_shape=[SDS((s, D_FF), bf16), SDS((s, D_MODEL), f32), SDS((1, D_MODEL), f32)],
        operands=[dh, up, h1, g2, w_up, w_dn], sem=("arbitrary",), vmem=VMEM_LIMIT)


def mlp_bwd_w(hn2, d_up, up, dh, pieces=None):
    s = dh.shape[0]
    tk = min(1024, s)

    def body(x_ref, du_ref, up_ref, dh_ref, dwu_ref, dwd_ref):
        k = pl.program_id(1)

        @pl.when(k == 0)
        def _():
            dwu_ref[...] = jnp.zeros_like(dwu_ref)
            dwd_ref[...] = jnp.zeros_like(dwd_ref)

        dwu_ref[0] += _dot_tn(x_ref[...], du_ref[...])
        act = jnp.square(jnp.maximum(up_ref[...].astype(f32), 0.0)).astype(bf16)
        dwd_ref[0] += _dot_tn(act, dh_ref[...].astype(bf16))

    return _pcall(
        body, pieces, name="mlp_bwd_w", grid=(N_SHARD, s // tk),
        in_specs=[pl.BlockSpec((tk, D_MODEL), lambda c, k: (k, 0)), pl.BlockSpec((tk, FF_CHUNK), lambda c, k: (k, c)),
                  pl.BlockSpec((tk, FF_CHUNK), lambda c, k: (k, c)), pl.BlockSpec((tk, D_MODEL), lambda c, k: (k, 0))],
        out_specs=[pl.BlockSpec((1, D_MODEL, FF_CHUNK), lambda c, k: (c, 0, 0)),
                   pl.BlockSpec((1, FF_CHUNK, D_MODEL), lambda c, k: (c, 0, 0))],
        out_shape=[SDS((N_SHARD, D_MODEL, FF_CHUNK), f32), SDS((N_SHARD, FF_CHUNK, D_MODEL), f32)],
        operands=[hn2, d_up, up, dh], sem=("parallel", "arbitrary"), vmem=VMEM_LIMIT)


def outproj_bwd(dh1, ya, yc, yl, w_out):
    s = dh1.shape[0]
    tm = _tile(s)

    def body(dh_ref, ya_ref, yc_ref, yl_ref, w_ref, dy_ref, dw_ref):
        i = pl.program_id(0)

        @pl.when(i == 0)
        def _():
            dw_ref[...] = jnp.zeros_like(dw_ref)

        dhb = dh_ref[...].astype(bf16)
        dy_ref[...] = _dot_nt(dhb, w_ref[...])
        y = jnp.concatenate([ya_ref[...], yc_ref[...], yl_ref[...]], axis=1)
        dw_ref[...] += _dot_tn(y, dhb)

    return pl.pallas_call(
        body, name="outproj_bwd", grid=(s // tm,),
        in_specs=[pl.BlockSpec((tm, D_MODEL), lambda i: (i, 0)), pl.BlockSpec((tm, ATTN_W), lambda i: (i, 0)),
                  pl.BlockSpec((tm, CONV_W), lambda i: (i, 0)), pl.BlockSpec((tm, LRU_W), lambda i: (i, 0)),
                  _full((D_MODEL, D_MODEL))],
        out_specs=[pl.BlockSpec((tm, D_MODEL), lambda i: (i, 0)), _full((D_MODEL, D_MODEL))],
        out_shape=[SDS((s, D_MODEL), f32), SDS((D_MODEL, D_MODEL), f32)],
        compiler_params=_params(("arbitrary",), VMEM_LIMIT),
    )(dh1, ya, yc, yl, w_out)


def attn_bwd(z, o, dy, sink_row, bias, g_a, pieces=None):
    s = z.shape[0]
    per = min(ATT_BLOCKS_PER_STEP, s // BLOCK)
    tq = per * BLOCK
    nt = s // tq

    def body(q_ref, kv_ref, kvp_ref, o_ref, dy_ref, sk_ref, b_ref, g_ref, dq_ref, dkv_ref, dsk_ref, dg_ref,
             carry_ref, dsk_acc):
        i = pl.program_id(0)
        t = nt - 1 - i

        @pl.when(i == 0)
        def _():
            carry_ref[...] = jnp.zeros_like(carry_ref)
            dsk_acc[...] = jnp.zeros_like(dsk_acc)
            dg_ref[...] = jnp.zeros_like(dg_ref)

        mlo = lax.broadcasted_iota(jnp.int32, (BLOCK, LANES), 1) < HEAD_DIM
        lane = lax.broadcasted_iota(jnp.int32, (2 * BLOCK, LANES), 1)
        scale = HEAD_DIM ** -0.5
        half = ATT_ROWS // 2
        g = g_ref[...]
        bands = []
        for b in range(per):
            rows = slice(b * BLOCK, (b + 1) * BLOCK)
            kvp = kvp_ref[...] if b == 0 else kv_ref[(b - 1) * BLOCK:b * BLOCK, :]
            bias_b = b_ref[jnp.minimum(t, 1)] if b == 0 else b_ref[1]
            kx, vx = _attn_band(kv_ref[rows, :], kvp)
            q4 = _stack_heads(q_ref[rows, :] * scale, mlo)
            o_f = o_ref[rows, :].astype(f32)
            _, r = _rms_fwd(o_f, g)
            d_o, dg = _rms_bwd(dy_ref[rows, :], o_f, r, g)
            dg_ref[...] += dg
            do4 = _stack_heads(d_o.astype(bf16), mlo)
            probs = _attn_probs(q4, kx, bias_b, sk_ref[...])
            dq4, tk, tv = [], [], []
            for hk in range(2):
                pr, p_sink = probs[hk]
                d_p = _dot_nt(vx[hk], do4[hk])
                d_row = jnp.sum(pr * d_p, axis=0, keepdims=True)
                d_s = (pr * (d_p - d_row)).astype(bf16)
                dsk_acc[:, hk * half:(hk + 1) * half] -= p_sink * d_row
                dq4.append(_dot_tn(d_s, kx[hk]))
                tk.append(_dot(d_s, q4[hk]))
                tv.append(_dot(pr.astype(bf16), do4[hk]))
            dq_ref[rows, :] = (_unstack_heads(dq4, mlo) * scale).astype(bf16)
            fk = [x + pltpu.roll(x, HEAD_DIM, 1) for x in tk]
            fv = [x + pltpu.roll(x, HEAD_DIM, 1) for x in tv]
            bands.append(jnp.concatenate([jnp.where(lane < HEAD_DIM, fk[0], fk[1]),
                                          jnp.where(lane < HEAD_DIM, fv[0], fv[1])], axis=1))
        for b in range(per):
            after = bands[b + 1][:BLOCK] if b + 1 < per else carry_ref[...]
            dkv_ref[b * BLOCK:(b + 1) * BLOCK, :] = (bands[b][BLOCK:] + after).astype(bf16)
        carry_ref[...] = bands[0][:BLOCK]

        @pl.when(i == nt - 1)
        def _():
            for hh in range(N_HEADS):
                tot = jnp.sum(dsk_acc[:, hh * BLOCK:(hh + 1) * BLOCK], axis=1, keepdims=True)
                dsk_ref[hh:hh + 1, :] = jnp.broadcast_to(tot, (1, LANES))

    def rev(width, col):
        return pl.BlockSpec((tq, width), lambda i: (nt - 1 - i, col))

    return _pcall(
        body, pieces, name="attn_bwd", grid=(nt,),
        in_specs=[rev(ATTN_W, 0), rev(2 * KV_W, 2),
                  pl.BlockSpec((BLOCK, 2 * KV_W), lambda i: (jnp.maximum((nt - 1 - i) * per - 1, 0), 2)),
                  rev(ATTN_W, 0), rev(ATTN_W, 0),
                  _full((1, ATT_ROWS)), _full((2, 2 * BLOCK, ATT_ROWS // 2)), _full((1, ATTN_W))],
        out_specs=[rev(ATTN_W, 0), rev(2 * KV_W, 0), _full((N_HEADS, LANES)), _full((1, ATTN_W))],
        out_shape=[SDS((s, ATTN_W), bf16), SDS((s, 2 * KV_W), bf16), SDS((N_HEADS, LANES), f32), SDS((1, ATTN_W), f32)],
        scratch_shapes=[pltpu.VMEM((BLOCK, 2 * KV_W), f32), pltpu.VMEM((1, ATT_ROWS), f32)],
        operands=[z, z, z, o, dy, sink_row, bias, g_a], sem=("arbitrary",))


_V_GC, _V_LNG, _V_LNB, _V_CB, _V_GL, _V_BA, _V_BX, _V_LAM, _V_LB = range(9)
_V_ROWS = 16


def branch_bwd_a(z, conv, hst, dy, p, pieces=None):
    s = z.shape[0]
    tm = _tile(s)
    nt = s // tm
    hb = tm // HALO
    h8 = tm // 8

    def body(conv_ref, dyc_ref, dyl_ref, rx_ref, rxh_ref, rg_ref, hst_ref, hsth_ref,
             lng_ref, lnb_ref, lw_ref, lb_ref, wa_ref, ba_ref, wx_ref, bx_ref, lam_ref, gc_ref, gl_ref,
             dconv_ref, dxc_ref, drg_ref, vec_ref, dwa_ref, dwx_ref, carry_ref):
        i = pl.program_id(0)
        ti = nt - 1 - i

        @pl.when(i == 0)
        def _():
            carry_ref[...] = jnp.zeros_like(carry_ref)
            vec_ref[...] = jnp.zeros_like(vec_ref)
            dwa_ref[...] = jnp.zeros_like(dwa_ref)
            dwx_ref[...] = jnp.zeros_like(dwx_ref)

        conv = conv_ref[...]
        mu = jnp.mean(conv, axis=-1, keepdims=True)
        xm = conv - mu
        rstd = lax.rsqrt(jnp.mean(xm * xm, axis=-1, keepdims=True) + LN_EPS)
        xhat = xm * rstd
        lng = lng_ref[...]
        ln = xhat * lng + lnb_ref[...]
        sg = _sigmoid(ln)
        yc = ln * sg
        gc = gc_ref[...]
        _, rc = _rms_fwd(yc, gc)
        d_yc, d_gc = _rms_bwd(dyc_ref[...], yc, rc, gc)
        d_ln = d_yc * (sg * (1.0 + ln * (1.0 - sg)))
        d_xhat = d_ln * lng
        d_conv = rstd * (d_xhat - jnp.mean(d_xhat, axis=-1, keepdims=True)
                         - xhat * jnp.mean(d_xhat * xhat, axis=-1, keepdims=True))
        dconv_ref[...] = d_conv

        rx = rx_ref[...].astype(f32)
        hrx = jnp.where(ti == 0, 0.0, rxh_ref[...].astype(f32))
        xc = _conv_taps(jnp.concatenate([hrx, rx], axis=0), lw_ref[...], LRU_K) + lb_ref[...]
        wa = wa_ref[...]
        wx = wx_ref[...]
        lam = lam_ref[...]
        r, ig, sp, la, a, mult = _lru_gates(xc, wa, ba_ref[...], wx, bx_ref[...], lam)
        hs = hst_ref[...]
        row = lax.broadcasted_iota(jnp.int32, hs.shape, 0)
        h_before = jnp.where(ti == 0, 0.0, hsth_ref[7:8, :])
        h_prev = jnp.where(row == 0, h_before, pltpu.roll(hs, 1, 0))
        rg = rg_ref[...].astype(f32)
        gl, tg = _gelu(rg)
        out = hs * gl
        gmix = gl_ref[...]
        _, rl = _rms_fwd(out, gmix)
        d_out, d_gl = _rms_bwd(dyl_ref[...], out, rl, gmix)
        drg_ref[...] = (d_out * hs * _gelu_grad(rg, tg)).astype(bf16)
        d_h = d_out * gl
        last = row == tm - 1
        a_next = jnp.where(last, 1.0, pltpu.roll(a, tm - 1, 0))
        lmb = _scan_bwd(a_next, d_h, carry_ref[0:1, :])
        carry_ref[...] = jnp.broadcast_to(a[0:1, :] * lmb[0:1, :], carry_ref.shape)
        d_a = lmb * h_prev
        d_mult = lmb * (ig * xc)
        d_ig = lmb * (mult * xc)
        d_la = d_a * a - d_mult * (a * a) / jnp.maximum(mult, 1e-30)
        d_pa = (d_la * (-LRU_C * sp)) * (r * (1.0 - r))
        d_px = d_ig * (ig * (1.0 - ig))
        d_pab = d_pa.astype(bf16)
        d_pxb = d_px.astype(bf16)
        d_xc = lmb * (mult * ig) + _dot_nt(d_pab, wa) + _dot_nt(d_pxb, wx)
        dxc_ref[...] = d_xc
        xcb = xc.astype(bf16)
        dwa_ref[...] += _dot_tn(xcb, d_pab)
        dwx_ref[...] += _dot_tn(xcb, d_pxb)
        d_lam = jnp.sum(d_la * (-LRU_C * r), axis=0, keepdims=True) * (-_sigmoid(-lam))

        def colsum(v):
            return jnp.sum(v, axis=0, keepdims=True)

        rows = [None] * _V_ROWS
        rows[_V_GC] = d_gc
        rows[_V_LNG] = colsum(d_ln * xhat)
        rows[_V_LNB] = colsum(d_ln)
        rows[_V_CB] = colsum(d_conv)
        rows[_V_GL] = d_gl
        rows[_V_BA] = colsum(d_pa)
        rows[_V_BX] = colsum(d_px)
        rows[_V_LAM] = d_lam
        rows[_V_LB] = colsum(d_xc)
        zero = jnp.zeros((1, CONV_W), f32)
        vec_ref[...] += jnp.concatenate([zero if v is None else v for v in rows], axis=0)

    def rev(c):
        return pl.BlockSpec((tm, CONV_W), lambda i: (nt - 1 - i, c))

    small = [p["lng"], p["lnb"], p["lw"], p["lb"], p["wa"], p["ba"], p["wx"], p["bx"], p["lam"], p["gc"], p["gl"]]
    return _pcall(
        body, pieces, name="branch_bwd_a", grid=(nt,),
        in_specs=[rev(0), rev(2), rev(3), rev(5),
                  pl.BlockSpec((HALO, CONV_W), lambda i: (jnp.maximum((nt - 1 - i) * hb - 1, 0), 5)),
                  rev(6), rev(0),
                  pl.BlockSpec((8, LRU_W), lambda i: (jnp.maximum((nt - 1 - i) * h8 - 1, 0), 0))]
                 + [_full(a.shape) for a in small],
        out_specs=[rev(0), rev(0), rev(0), _full((_V_ROWS, CONV_W)), _full((LRU_W, LRU_W)), _full((LRU_W, LRU_W))],
        out_shape=[SDS((s, CONV_W), f32), SDS((s, LRU_W), f32), SDS((s, LRU_W), bf16), SDS((_V_ROWS, CONV_W), f32),
                   SDS((LRU_W, LRU_W), f32), SDS((LRU_W, LRU_W), f32)],
        scratch_shapes=[pltpu.VMEM((8, LRU_W), f32)],
        operands=[conv, dy, dy, z, z, z, hst, hst, *small], sem=("arbitrary",))


def branch_bwd_b(z, d_conv, d_xc, p, pieces=None):
    s = z.shape[0]
    tm = _tile(s)
    nt = s // tm
    hb = tm // HALO

    def body(cv_ref, cg_ref, cvh_ref, cgh_ref, rx_ref, rxh_ref, dc_ref, dch_ref, dx_ref, dxh_ref, cw_ref, lw_ref,
             dzc_ref, dzr_ref, dcw_ref, dlw_ref):
        i = pl.program_id(0)

        @pl.when(i == 0)
        def _():
            dcw_ref[...] = jnp.zeros_like(dcw_ref)
            dlw_ref[...] = jnp.zeros_like(dlw_ref)

        first = i == 0
        last = i == nt - 1
        cval = cv_ref[...].astype(f32)
        sg = _sigmoid(cg_ref[...].astype(f32))
        u = cval * sg
        hu = jnp.where(first, 0.0, cvh_ref[...].astype(f32) * _sigmoid(cgh_ref[...].astype(f32)))
        dpad = jnp.concatenate([dc_ref[...], jnp.where(last, 0.0, dch_ref[...])], axis=0)
        d_u, dw_rows = _conv_taps_bwd(dpad, jnp.concatenate([hu, u], axis=0), cw_ref[...], CONV_K, tm)
        dcw_ref[...] += jnp.concatenate(dw_rows + [jnp.zeros((HALO - CONV_K, CONV_W), f32)], axis=0)
        dzc_ref[...] = jnp.concatenate([d_u * sg, d_u * cval * sg * (1.0 - sg)], axis=1).astype(bf16)

        rx = rx_ref[...].astype(f32)
        hrx = jnp.where(first, 0.0, rxh_ref[...].astype(f32))
        dxpad = jnp.concatenate([dx_ref[...], jnp.where(last, 0.0, dxh_ref[...])], axis=0)
        d_rx, dlw_rows = _conv_taps_bwd(dxpad, jnp.concatenate([hrx, rx], axis=0), lw_ref[...], LRU_K, tm)
        dlw_ref[...] += jnp.concatenate(dlw_rows + [jnp.zeros((8 - LRU_K, LRU_W), f32)], axis=0)
        dzr_ref[...] = d_rx.astype(bf16)

    def col(c):
        return pl.BlockSpec((tm, CONV_W), lambda i: (i, c))

    def prev(c):
        return pl.BlockSpec((HALO, CONV_W), lambda i: (jnp.maximum(i * hb - 1, 0), c))

    nxt = pl.BlockSpec((HALO, CONV_W), lambda i: (jnp.minimum((i + 1) * hb, nt * hb - 1), 0))
    return _pcall(
        body, pieces, name="branch_bwd_b", grid=(nt,),
        in_specs=[col(3), col(4), prev(3), prev(4), col(5), prev(5), col(0), nxt, col(0), nxt,
                  _full(p["cw"].shape), _full(p["lw"].shape)],
        out_specs=[pl.BlockSpec((tm, 2 * CONV_W), lambda i: (i, 0)), pl.BlockSpec((tm, LRU_W), lambda i: (i, 0)),
                   _full((HALO, CONV_W)), _full((8, LRU_W))],
        out_shape=[SDS((s, 2 * CONV_W), bf16), SDS((s, LRU_W), bf16), SDS((HALO, CONV_W), f32), SDS((8, LRU_W), f32)],
        operands=[z, z, z, z, z, z, d_conv, d_conv, d_xc, d_xc, p["cw"], p["lw"]], sem=("arbitrary",))


def inproj_bwd(dq, dkv, dzc, dzr, drg, h, g1, w_in_t, dh1, pieces=None):
    s = h.shape[0]
    tm = _tile(s)

    def body(dq_ref, dkv_ref, dzc_ref, dzr_ref, drg_ref, h_ref, g_ref, w_ref, dh1_ref, dh_ref, dw_ref, dg_ref):
        i = pl.program_id(0)

        @pl.when(i == 0)
        def _():
            dw_ref[...] = jnp.zeros_like(dw_ref)
            dg_ref[...] = jnp.zeros_like(dg_ref)

        dz = jnp.concatenate([dq_ref[...], dkv_ref[...], dzc_ref[...], dzr_ref[...], drg_ref[...]], axis=1)
        x = h_ref[...]
        g = g_ref[...]
        hn, r = _rms_fwd(x, g)
        d_hn = _dot(dz, w_ref[...])
        dw_ref[...] += _dot_tn(dz, hn.astype(bf16))
        dx, dg = _rms_bwd(d_hn, x, r, g)
        dh_ref[...] = dh1_ref[...] + dx
        dg_ref[...] += dg

    def rowb(w):
        return pl.BlockSpec((tm, w), lambda i: (i, 0))

    return _pcall(
        body, pieces, name="inproj_bwd", grid=(s // tm,),
        in_specs=[rowb(ATTN_W), rowb(2 * KV_W), rowb(2 * CONV_W), rowb(LRU_W), rowb(LRU_W), rowb(D_MODEL),
                  _full((1, D_MODEL)), _full((IN_W, D_MODEL)), rowb(D_MODEL)],
        out_specs=[rowb(D_MODEL), _full((IN_W, D_MODEL)), _full((1, D_MODEL))],
        out_shape=[SDS((s, D_MODEL), f32), SDS((IN_W, D_MODEL), f32), SDS((1, D_MODEL), f32)],
        operands=[dq, dkv, dzc, dzr, drg, h, g1, w_in_t, dh1], sem=("arbitrary",), vmem=VMEM_LIMIT)


def _block_diag(w):
    out = jnp.zeros((LRU_W, LRU_W), w.dtype)
    hd = LRU_W // LRU_HEADS
    for hh in range(LRU_HEADS):
        out = out.at[hh * hd:(hh + 1) * hd, hh * hd:(hh + 1) * hd].set(w[hh])
    return out


def _diag_blocks(w):
    hd = LRU_W // LRU_HEADS
    return jnp.stack([w[hh * hd:(hh + 1) * hd, hh * hd:(hh + 1) * hd] for hh in range(LRU_HEADS)])


def _layer_params(sp, l):
    mix = sp["mix_norm"][l]
    return dict(
        g1=sp["norm1"][l][None, :], g2=sp["norm2"][l][None, :],
        sinks=jnp.repeat(sp["attn_sinks"][l], BLOCK)[None, :],
        ga=mix[None, :ATTN_W], gc=mix[None, ATTN_W:ATTN_W + CONV_W], gl=mix[None, ATTN_W + CONV_W:],
        cw=jnp.pad(sp["conv_dw_w"][l], ((0, HALO - CONV_K), (0, 0))), cb=sp["conv_dw_b"][l][None, :],
        lng=sp["conv_ln_g"][l][None, :], lnb=sp["conv_ln_b"][l][None, :],
        lw=jnp.pad(sp["lru_conv_w"][l], ((0, 8 - LRU_K), (0, 0))), lb=sp["lru_conv_b"][l][None, :],
        wa=_block_diag(sp["lru_wa"][l]).astype(bf16), ba=sp["lru_ba"][l].reshape(1, LRU_W),
        wx=_block_diag(sp["lru_wx"][l]).astype(bf16), bx=sp["lru_bx"][l].reshape(1, LRU_W),
        lam=sp["lru_lambda"][l][None, :],
    )


def local_step(x, tgt, big, sp):
    return train_local(x, tgt, sp, LocalWeights(big))


class LocalWeights:
    def __init__(self, big):
        self.big = big
        self.grads = [dict() for _ in range(DEPTH)]

    def weight(self, name, l):
        return self.big[l][name]

    def host(self, point, l):
        return None

    def grad(self, name, l, g):
        self.grads[l][name] = g

    def big_grads(self):
        return self.grads


def train_local(x, tgt, sp, cs):
    lp = [_layer_params(sp, l) for l in range(DEPTH)]
    bias = _attn_bias()
    saved = []
    h = x
    for l in range(DEPTH):
        p = lp[l]
        z = inproj_fwd(h, p["g1"], cs.weight("w_in", l), cs.host("inproj_fwd", l))
        o, ya = attn_fwd(z, p["sinks"], bias, p["ga"], cs.host("attn_fwd", l))
        conv, hst, yc, yl = branch_fwd(z, p, cs.host("branch_fwd", l))
        if l < DEPTH - 1:
            h1, hn2 = outproj_fwd(ya, yc, yl, h, cs.weight("w_out", l), p["g2"], cs.host("outproj_fwd", l))
            up, h_next = mlp_fwd(hn2, h1, cs.weight("w_up", l), cs.weight("w_dn", l), cs.host("mlp_fwd", l))
        else:
            h1, hn2, up, dh, loss, d_gf = last_layer_fwd(
                ya, yc, yl, h, cs.weight("w_out", l), p["g2"], cs.weight("w_up", l), cs.weight("w_dn", l),
                tgt, sp["final_norm"][None, :])
            h_next = None
        saved.append(dict(h=h, z=z, o=o, ya=ya, conv=conv, hst=hst, yc=yc, yl=yl, h1=h1, hn2=hn2, up=up))
        h = h_next
    small_g = [None] * DEPTH
    for l in reversed(range(DEPTH)):
        p, sv = lp[l], saved[l]
        w_up, w_dn = cs.weight("w_up", l), cs.weight("w_dn", l)
        d_up, dh1, d_g2 = mlp_bwd_act(dh, sv["up"], sv["h1"], p["g2"], w_up, w_dn, cs.host("mlp_bwd_act", l))
        dw_up, dw_dn = mlp_bwd_w(sv["hn2"], d_up, sv["up"], dh, cs.host("mlp_bwd_w", l))
        cs.grad("w_up", l, dw_up)
        cs.grad("w_dn", l, dw_dn)
        dy, dw_out = outproj_bwd(dh1, sv["ya"], sv["yc"], sv["yl"], cs.weight("w_out", l))
        cs.grad("w_out", l, dw_out)
        dq, dkv, d_sk, d_ga = attn_bwd(sv["z"], sv["o"], dy, p["sinks"], bias, p["ga"], cs.host("attn_bwd", l))
        d_conv, d_xc, d_rg, vec, dwa, dwx = branch_bwd_a(sv["z"], sv["conv"], sv["hst"], dy, p, cs.host("branch_bwd_a", l))
        dzc, dzr, dcw, dlw = branch_bwd_b(sv["z"], d_conv, d_xc, p, cs.host("branch_bwd_b", l))
        dh, dw_in, d_g1 = inproj_bwd(dq, dkv, dzc, dzr, d_rg, sv["h"], p["g1"], cs.weight("w_in", l), dh1,
                                     cs.host("inproj_bwd", l))
        cs.grad("w_in", l, dw_in)
        hd = LRU_W // LRU_HEADS
        small_g[l] = dict(
            norm1=d_g1[0], attn_sinks=d_sk[:, 0], conv_dw_w=dcw[:CONV_K], conv_dw_b=vec[_V_CB],
            conv_ln_g=vec[_V_LNG], conv_ln_b=vec[_V_LNB], lru_conv_w=dlw[:LRU_K], lru_conv_b=vec[_V_LB],
            lru_wa=_diag_blocks(dwa), lru_ba=vec[_V_BA].reshape(LRU_HEADS, hd),
            lru_wx=_diag_blocks(dwx), lru_bx=vec[_V_BX].reshape(LRU_HEADS, hd), lru_lambda=vec[_V_LAM],
            mix_norm=jnp.concatenate([d_ga[0], vec[_V_GC], vec[_V_GL]]), norm2=d_g2[0],
        )
    return loss, dh, cs.big_grads(), small_g, d_gf[0]


_HBM = pl.BlockSpec(memory_space=pl.ANY)


def _place():
    x, y, c = lax.axis_index("x"), lax.axis_index("y"), lax.axis_index("c")
    chips = [(1 - x, y), (x, 1 - y), (1 - x, 1 - y)]
    return x, y, c, chips


class Comm:
    def __init__(self, ins, out_shape, aliases, sems, start, finish, done):
        self.ins, self.out_shape, self.aliases, self.sems = list(ins), list(out_shape), dict(aliases), list(sems)
        self.start, self.finish, self.done = start, finish, done


def _pcall(body, pieces, *, name, grid, in_specs, out_specs, out_shape, operands, scratch_shapes=(), sem, vmem=None,
           prefetch=None):
    in_specs, out_specs, out_shape, scratch_shapes = list(in_specs), list(out_specs), list(out_shape), list(scratch_shapes)
    lead = 0 if prefetch is None else 1

    def call(fn, call_name, ins, outs, shapes, scratch, aliases, semantics, args):
        params = _params(semantics, vmem)
        if prefetch is None:
            return pl.pallas_call(fn, name=call_name, grid=grid, in_specs=ins, out_specs=outs, out_shape=shapes,
                                  scratch_shapes=scratch, input_output_aliases=aliases, compiler_params=params)(*args)
        spec = pltpu.PrefetchScalarGridSpec(num_scalar_prefetch=1, grid=grid, in_specs=ins, out_specs=outs,
                                            scratch_shapes=scratch)
        return pl.pallas_call(fn, name=call_name, grid_spec=spec, out_shape=shapes, input_output_aliases=aliases,
                              compiler_params=params)(prefetch, *args)

    if not pieces:
        return call(body, name, in_specs, out_specs, out_shape, scratch_shapes, {}, sem, operands)
    n_in, n_out, n_scr = len(in_specs), len(out_specs), len(scratch_shapes)
    c_ins = [a for p in pieces for a in p.ins]
    c_outs = [s for p in pieces for s in p.out_shape]
    c_sems = [n for p in pieces for n in p.sems]
    aliases, spans, ki, ko, ks = {}, [], 0, 0, 0
    for p in pieces:
        spans.append((ki, ko, ks))
        for a, b in p.aliases.items():
            aliases[lead + n_in + ki + a] = n_out + ko + b
        ki, ko, ks = ki + len(p.ins), ko + len(p.out_shape), ks + len(p.sems)

    def hosted(*refs):
        pre, refs = refs[:lead], refs[lead:]
        ins, cin = refs[:n_in], refs[n_in:n_in + ki]
        outs, cout = refs[n_in + ki:n_in + ki + n_out], refs[n_in + ki + n_out:n_in + ki + n_out + ko]
        scr, csem = refs[n_in + ki + n_out + ko:n_in + ki + n_out + ko + n_scr], refs[n_in + ki + n_out + ko + n_scr:]
        first = functools.reduce(jnp.logical_and, [pl.program_id(d) == 0 for d in range(len(grid))])
        last = functools.reduce(jnp.logical_and, [pl.program_id(d) == grid[d] - 1 for d in range(len(grid))])

        def each(which):
            for p, (a, b, s) in zip(pieces, spans):
                getattr(p, which)(cin[a:a + len(p.ins)], cout[b:b + len(p.out_shape)], csem[s:s + len(p.sems)])

        @pl.when(first)
        def _():
            each("start")

        body(*pre, *ins, *outs, *scr)

        @pl.when(last)
        def _():
            each("finish")

    res = call(hosted, name + "_host", in_specs + [_HBM] * ki, out_specs + [_HBM] * ko, out_shape + c_outs,
               scratch_shapes + [pltpu.SemaphoreType.DMA((n,)) for n in c_sems], aliases, ("arbitrary",) * len(grid),
               [*operands, *c_ins])
    for p, (a, b, s) in zip(pieces, spans):
        p.done(res[n_out + b:n_out + b + len(p.out_shape)])
    return res[:n_out]


def standalone(pieces, name):
    ki = sum(len(p.ins) for p in pieces)
    ko = sum(len(p.out_shape) for p in pieces)
    spans, a, b, s = [], 0, 0, 0
    aliases = {}
    for p in pieces:
        spans.append((a, b, s))
        for i, o in p.aliases.items():
            aliases[a + i] = b + o
        a, b, s = a + len(p.ins), b + len(p.out_shape), s + len(p.sems)

    def body(*refs):
        cin, cout, csem = refs[:ki], refs[ki:ki + ko], refs[ki + ko:]
        for which in ("start", "finish"):
            for p, (a, b, s) in zip(pieces, spans):
                getattr(p, which)(cin[a:a + len(p.ins)], cout[b:b + len(p.out_shape)], csem[s:s + len(p.sems)])

    res = pl.pallas_call(
        body, name=name, in_specs=[_HBM] * ki, out_specs=[_HBM] * ko, out_shape=[s for p in pieces for s in p.out_shape],
        scratch_shapes=[pltpu.SemaphoreType.DMA((n,)) for p in pieces for n in p.sems], input_output_aliases=aliases,
    )(*[a for p in pieces for a in p.ins])
    for p, (a, b, s) in zip(pieces, spans):
        p.done(res[b:b + len(p.out_shape)])


def _rows_half(ref, which, rows):
    return ref.at[pl.ds(pl.multiple_of(which * rows, 8), rows)]


def gather_ici_piece(bufs, done):
    n = len(bufs)

    def copies(cout):
        x, y, c, chips = _place()
        out = []
        for j, (cx, cy) in enumerate(chips):
            for w in range(n):
                half = bufs[w].shape[1] // 2
                out.append((j, w, _rows_half(cout[w].at[2 * x + y], c, half), _rows_half(cout[w].at[2 * cx + cy], c, half),
                            (cx, cy, c)))
        return out

    def start(cin, cout, sems):
        for j, w, mine, _, to in copies(cout):
            pltpu.make_async_remote_copy(src_ref=mine, dst_ref=mine, send_sem=sems[0].at[n * j + w],
                                         recv_sem=sems[1].at[n * j + w], device_id=to, device_id_type=MESH).start()

    def finish(cin, cout, sems):
        for j, w, mine, landed, to in copies(cout):
            pltpu.make_async_remote_copy(src_ref=mine, dst_ref=landed, send_sem=sems[0].at[n * j + w],
                                         recv_sem=sems[1].at[n * j + w], device_id=to, device_id_type=MESH).wait()

    return Comm(bufs, [SDS(b.shape, b.dtype) for b in bufs], {w: w for w in range(n)}, [3 * n, 3 * n], start, finish, done)


def gather_full_piece(bufs, done):
    n = len(bufs)

    def copies(cout):
        x, y, c, chips = _place()
        return [(n * j + w, cout[w].at[2 * x + y], cout[w].at[2 * cx + cy], (cx, cy, c))
                for j, (cx, cy) in enumerate(chips) for w in range(n)]

    def start(cin, cout, sems):
        for k, mine, _, to in copies(cout):
            pltpu.make_async_remote_copy(src_ref=mine, dst_ref=mine, send_sem=sems[0].at[k], recv_sem=sems[1].at[k],
                                         device_id=to, device_id_type=MESH).start()

    def finish(cin, cout, sems):
        for k, mine, landed, to in copies(cout):
            pltpu.make_async_remote_copy(src_ref=mine, dst_ref=landed, send_sem=sems[0].at[k], recv_sem=sems[1].at[k],
                                         device_id=to, device_id_type=MESH).wait()

    return Comm(bufs, [SDS(b.shape, b.dtype) for b in bufs], {w: w for w in range(n)}, [3 * n, 3 * n], start, finish, done)


def gather_d2d_piece(bufs, done):
    n = len(bufs)

    def copies(cout):
        x, y, c, chips = _place()
        out = []
        for j, (cx, cy) in enumerate(chips):
            for w in range(n):
                half = bufs[w].shape[1] // 2
                slot = cout[w].at[2 * cx + cy]
                out.append((n * j + w, _rows_half(slot, c, half), _rows_half(slot, 1 - c, half), (x, y, 1 - c)))
        return out

    def start(cin, cout, sems):
        for k, mine, _, to in copies(cout):
            pltpu.make_async_remote_copy(src_ref=mine, dst_ref=mine, send_sem=sems[0].at[k], recv_sem=sems[1].at[k],
                                         device_id=to, device_id_type=MESH).start()

    def finish(cin, cout, sems):
        for k, mine, theirs, to in copies(cout):
            pltpu.make_async_remote_copy(src_ref=mine, dst_ref=theirs, send_sem=sems[0].at[k], recv_sem=sems[1].at[k],
                                         device_id=to, device_id_type=MESH).wait()

    return Comm(bufs, [SDS(b.shape, b.dtype) for b in bufs], {w: w for w in range(n)}, [3 * n, 3 * n], start, finish, done)


def pair_piece(parts, done):
    n = len(parts)

    def copies(cin, cout):
        x, y, c, _ = _place()
        out = []
        for w in range(n):
            half = parts[w].shape[1] // 2
            out.append((w, cin[w].at[:, pl.ds(pl.multiple_of((1 - c) * half, 8), half), :], cout[w], (x, y, 1 - c)))
        return out

    def start(cin, cout, sems):
        for w, src, dst, to in copies(cin, cout):
            pltpu.make_async_remote_copy(src_ref=src, dst_ref=dst, send_sem=sems[0].at[w], recv_sem=sems[1].at[w],
                                         device_id=to, device_id_type=MESH).start()

    def finish(cin, cout, sems):
        for w, src, dst, to in copies(cin, cout):
            pltpu.make_async_remote_copy(src_ref=src, dst_ref=dst, send_sem=sems[0].at[w], recv_sem=sems[1].at[w],
                                         device_id=to, device_id_type=MESH).wait()

    return Comm(parts, [SDS((N_SHARD, a.shape[1] // 2, a.shape[2]), f32) for a in parts], {}, [n, n], start, finish, done)


def shard_piece(sums16, done):
    n = len(sums16)

    def copies(cin, cout):
        x, y, c, chips = _place()
        return [(n * j + w, cin[w].at[2 * cx + cy], cout[w].at[j], (cx, cy, c))
                for j, (cx, cy) in enumerate(chips) for w in range(n)]

    def start(cin, cout, sems):
        for k, src, dst, to in copies(cin, cout):
            pltpu.make_async_remote_copy(src_ref=src, dst_ref=dst, send_sem=sems[0].at[k], recv_sem=sems[1].at[k],
                                         device_id=to, device_id_type=MESH).start()

    def finish(cin, cout, sems):
        for k, src, dst, to in copies(cin, cout):
            pltpu.make_async_remote_copy(src_ref=src, dst_ref=dst, send_sem=sems[0].at[k], recv_sem=sems[1].at[k],
                                         device_id=to, device_id_type=MESH).wait()

    return Comm(sums16, [SDS((3,) + a.shape[1:], bf16) for a in sums16], {}, [3 * n, 3 * n], start, finish, done)


def place_shard(a, layer, idx, dtype):
    _, r, cdim = a.shape
    tr = min(r, 512)

    def body(idx_ref, a_ref, o_ref):
        o_ref[0] = a_ref[0].astype(dtype)

    return pl.pallas_call(
        body, name="place_shard",
        grid_spec=pltpu.PrefetchScalarGridSpec(
            num_scalar_prefetch=1, grid=(r // tr,),
            in_specs=[pl.BlockSpec((1, tr, cdim), lambda i, idx_ref: (layer, i, 0))],
            out_specs=pl.BlockSpec((1, tr, cdim), lambda i, idx_ref: (idx_ref[1], i, 0))),
        out_shape=SDS((N_SHARD, r, cdim), dtype),
        compiler_params=_params(("arbitrary",)),
    )(idx, a)


def place_shards(items, idx, pieces):
    steps = 4
    n = len(items)

    def body(idx_ref, *refs):
        for k in range(n):
            refs[n + k][0] = refs[k][0].astype(bf16)

    in_specs, out_specs, out_shape = [], [], []
    for a, layer in items:
        _, r, cdim = a.shape
        in_specs.append(pl.BlockSpec((1, r // steps, cdim), lambda i, idx_ref, layer=layer: (layer, i, 0)))
        out_specs.append(pl.BlockSpec((1, r // steps, cdim), lambda i, idx_ref: (idx_ref[1], i, 0)))
        out_shape.append(SDS((N_SHARD, r, cdim), bf16))
    return _pcall(body, pieces, name="place_shards", grid=(steps,), in_specs=in_specs, out_specs=out_specs,
                  out_shape=out_shape, operands=[a for a, _ in items], sem=("arbitrary",), prefetch=idx)


_KEYS = ("w_in", "w_out", "w_up", "w_dn")


class MeshWeights:
    def __init__(self, w_in, w_out, w_up, w_down, conv_dw_w, lru_conv_w, idx):
        self.idx = idx
        src = dict(w_in=w_in, w_out=w_out, w_up=w_up, w_dn=w_down)
        self.conv = {(n, l): place_shard(a, l, idx, f32)
                     for n, a in (("cw", conv_dw_w), ("lw", lru_conv_w)) for l in range(DEPTH)}
        self.cache, self.parts, self.sum32, self.sum16, self.got = {}, {}, {}, {}, {}
        first, small = [("w_in", 0)], list(self.conv)
        rest = [(n, l) for n in _KEYS for l in range(DEPTH) if (n, l) not in first]

        def store_conv(outs):
            self.conv.update(zip(small, outs))

        self.buf = {k: place_shard(src[k[0]], k[1], idx, bf16) for k in first}
        placed = place_shards(
            [(src[n], l) for n, l in rest], idx,
            [self._gather(gather_ici_piece, first), gather_full_piece([self.conv[k] for k in small], store_conv)])
        self.buf.update(zip(rest, placed))
        standalone([self._gather(gather_d2d_piece, first)], "gather_first_d2d")

    def _gather(self, piece, keys):
        def done(outs):
            self.buf.update(zip(keys, outs))
        return piece([self.buf[k] for k in keys], done)

    def _pair(self, keys):
        def done(outs):
            for k, recv in zip(keys, outs):
                self.sum32[k], self.sum16[k] = chip_sum(self.parts[k], recv, self.idx)
        return pair_piece([self.parts[k] for k in keys], done)

    def _shard(self, keys):
        def done(outs):
            self.got.update(zip(keys, outs))
        return shard_piece([self.sum16[k] for k in keys], done)

    def conv_weights(self):
        out = []
        for n in ("cw", "lw"):
            a = jnp.stack([self.conv[(n, l)] for l in range(DEPTH)])
            out.append(a.transpose(0, 2, 1, 3).reshape(DEPTH, a.shape[2], N_SHARD * a.shape[3]))
        return out

    def weight(self, name, l):
        if (name, l) not in self.cache:
            b = self.buf[(name, l)]
            if name in ("w_in", "w_out"):
                b = b.reshape(-1, D_MODEL)
            self.cache[(name, l)] = b
        return self.cache[(name, l)]

    def host(self, point, l):
        ici, d2d = gather_ici_piece, gather_d2d_piece
        rest1 = [("w_out", 1), ("w_up", 1), ("w_dn", 1)]
        plan = {
            ("inproj_fwd", 0): lambda: [self._gather(ici, [("w_out", 0)])],
            ("attn_fwd", 0): lambda: [self._gather(ici, [("w_up", 0)]), self._gather(d2d, [("w_out", 0)])],
            ("branch_fwd", 0): lambda: [self._gather(ici, [("w_dn", 0)]), self._gather(d2d, [("w_up", 0)])],
            ("outproj_fwd", 0): lambda: [self._gather(d2d, [("w_dn", 0)]), self._gather(ici, [("w_in", 1)])],
            ("mlp_fwd", 0): lambda: [self._gather(ici, rest1), self._gather(d2d, [("w_in", 1)])],
            ("attn_fwd", 1): lambda: [self._gather(d2d, rest1)],
            ("attn_bwd", 1): lambda: [self._pair([("w_up", 1), ("w_dn", 1)])],
            ("mlp_bwd_act", 0): lambda: [self._shard([("w_up", 1), ("w_dn", 1)])],
            ("mlp_bwd_w", 0): lambda: [self._pair([("w_in", 1), ("w_out", 1)])],
            ("attn_bwd", 0): lambda: [self._shard([("w_in", 1), ("w_out", 1)]),
                                      self._pair([("w_up", 0), ("w_dn", 0), ("w_out", 0)])],
            ("branch_bwd_a", 0): lambda: [self._shard([("w_up", 0)])],
            ("branch_bwd_b", 0): lambda: [self._shard([("w_dn", 0), ("w_out", 0)])],
        }
        make = plan.get((point, l))
        return make() if make else None

    def grad(self, name, l, g):
        if name in ("w_in", "w_out"):
            g = g.reshape(N_SHARD, -1, D_MODEL)
        self.parts[(name, l)] = g

    def big_grads(self):
        return None

    def finish(self, small_vec):
        last = [("w_in", 0)]
        small = {}

        def keep(outs):
            small["buf"] = outs[0]

        standalone([self._pair(last), small_first_piece(small_vec, keep)], "pair_last")
        standalone([self._shard(last), small_second_piece(small["buf"], keep)], "shard_last")
        return self._totals(), small["buf"]

    def _totals(self):
        tots = []
        for n in _KEYS:
            t = None
            for l in reversed(range(DEPTH)):
                t = shard_sum(self.sum32[(n, l)], self.got[(n, l)], self.idx, l, t)
            tots.append(t)
        return tots


def chip_sum(g, recv, idx):
    _, r, cdim = g.shape
    half = r // 2
    tr = min(half, 512)
    nh = half // tr

    def body(idx_ref, g_ref, r_ref, o32_ref, o16_ref):
        tot = g_ref[0] + r_ref[0]
        o16_ref[0] = tot.astype(bf16)

        @pl.when(pl.program_id(0) == N_SHARD - 1)
        def _():
            o32_ref[...] = tot

    def slab(k, idx_ref):
        return lax.rem(idx_ref[1] + 1 + k, N_SHARD)

    def o16_map(k, i, idx_ref):
        return (slab(k, idx_ref), i, 0)

    return pl.pallas_call(
        body, name="chip_sum",
        grid_spec=pltpu.PrefetchScalarGridSpec(
            num_scalar_prefetch=1, grid=(N_SHARD, nh),
            in_specs=[pl.BlockSpec((1, tr, cdim), lambda k, i, idx_ref: (slab(k, idx_ref), idx_ref[0] * nh + i, 0)),
                      pl.BlockSpec((1, tr, cdim), lambda k, i, idx_ref: (slab(k, idx_ref), i, 0))],
            out_specs=[pl.BlockSpec((tr, cdim), lambda k, i, idx_ref: (jnp.where(k == N_SHARD - 1, i, 0), 0)),
                       pl.BlockSpec((1, tr, cdim), o16_map)]),
        out_shape=[SDS((half, cdim), f32), SDS((N_SHARD, half, cdim), bf16)],
        compiler_params=_params(("arbitrary", "arbitrary")),
    )(idx, g, recv)


def shard_sum(sum32, got16, idx, layer, prev):
    half, cdim = sum32.shape
    tr = min(half, 512)

    def body(idx_ref, a_ref, r0_ref, r1_ref, r2_ref, *rest):
        o_ref = rest[-1]
        o_ref[0, 0] = ((a_ref[...] + r0_ref[0].astype(f32)) + r1_ref[0].astype(f32)) + r2_ref[0].astype(f32)

    def rel(j):
        return pl.BlockSpec((1, tr, cdim), lambda i, idx_ref: (j, i, 0))

    in_specs = [pl.BlockSpec((tr, cdim), lambda i, idx_ref: (i, 0)), rel(0), rel(1), rel(2)]
    operands = [idx, sum32, got16, got16, got16]
    aliases = {}
    if prev is not None:
        in_specs.append(_HBM)
        operands.append(prev)
        aliases = {5: 0}
    return pl.pallas_call(
        body, name="shard_sum",
        grid_spec=pltpu.PrefetchScalarGridSpec(
            num_scalar_prefetch=1, grid=(half // tr,), in_specs=in_specs,
            out_specs=pl.BlockSpec((1, 1, tr, cdim), lambda i, idx_ref: (layer, idx_ref[0], i, 0))),
        out_shape=SDS((DEPTH, 2, half, cdim), f32), input_output_aliases=aliases,
        compiler_params=_params(("arbitrary",)),
    )(*operands)


def halves_exchange(tots):
    nw = len(tots)

    def body(*refs):
        bufs = refs[nw:2 * nw]
        send_sem, recv_sem = refs[2 * nw:]
        x, y, c, _ = _place()

        def copy(w, l, half_idx):
            return pltpu.make_async_remote_copy(
                src_ref=bufs[w].at[l, half_idx], dst_ref=bufs[w].at[l, half_idx], send_sem=send_sem.at[DEPTH * w + l],
                recv_sem=recv_sem.at[DEPTH * w + l], device_id=(x, y, 1 - c), device_id_type=MESH)

        sends = [copy(w, l, c) for w in range(nw) for l in range(DEPTH)]
        for cp in sends:
            cp.start()
        for w in range(nw):
            for l in range(DEPTH):
                copy(w, l, 1 - c).wait_recv()
        for cp in sends:
            cp.wait_send()

    return pl.pallas_call(
        body, name="halves_exchange", in_specs=[_HBM] * nw, out_specs=[_HBM] * nw,
        out_shape=[SDS(a.shape, f32) for a in tots], input_output_aliases={w: w for w in range(nw)},
        scratch_shapes=[pltpu.SemaphoreType.DMA((DEPTH * nw,)), pltpu.SemaphoreType.DMA((DEPTH * nw,))],
    )(*tots)


N_DEV = 8


def _dev_index(px, py, pc):
    return 4 * px + 2 * py + pc


def small_first_piece(vec, done):
    def copies(cin, cout):
        x, y, c, chips = _place()
        mine = cout[0].at[_dev_index(x, y, c)]
        peers = [(x, y, 1 - c)] + [(cx, cy, c) for cx, cy in chips]
        return mine, [(k, cout[0].at[_dev_index(*p)], p) for k, p in enumerate(peers)]

    def start(cin, cout, sems):
        mine, peers = copies(cin, cout)
        pltpu.make_async_copy(cin[0], mine, sems[2].at[0]).start()
        for k, _, to in peers:
            pltpu.make_async_remote_copy(src_ref=cin[0], dst_ref=mine, send_sem=sems[0].at[k], recv_sem=sems[1].at[k],
                                         device_id=to, device_id_type=MESH).start()

    def finish(cin, cout, sems):
        mine, peers = copies(cin, cout)
        for k, theirs, to in peers:
            pltpu.make_async_remote_copy(src_ref=cin[0], dst_ref=theirs, send_sem=sems[0].at[k], recv_sem=sems[1].at[k],
                                         device_id=to, device_id_type=MESH).wait()
        pltpu.make_async_copy(cin[0], mine, sems[2].at[0]).wait()

    return Comm([vec], [SDS((N_DEV,) + vec.shape, f32)], {}, [4, 4, 1], start, finish, done)


def small_second_piece(buf, done):
    def copies(cout):
        x, y, c, chips = _place()
        return [(j, cout[0].at[_dev_index(cx, cy, c)], cout[0].at[_dev_index(cx, cy, 1 - c)], (x, y, 1 - c))
                for j, (cx, cy) in enumerate(chips)]

    def start(cin, cout, sems):
        for j, mine, _, to in copies(cout):
            pltpu.make_async_remote_copy(src_ref=mine, dst_ref=mine, send_sem=sems[0].at[j], recv_sem=sems[1].at[j],
                                         device_id=to, device_id_type=MESH).start()

    def finish(cin, cout, sems):
        for j, mine, theirs, to in copies(cout):
            pltpu.make_async_remote_copy(src_ref=mine, dst_ref=theirs, send_sem=sems[0].at[j], recv_sem=sems[1].at[j],
                                         device_id=to, device_id_type=MESH).wait()

    return Comm([buf], [SDS(buf.shape, f32)], {0: 0}, [3, 3], start, finish, done)


def small_sum(buf):
    def body(b_ref, o_ref):
        acc = b_ref[0]
        for d in range(1, N_DEV):
            acc = acc + b_ref[d]
        o_ref[...] = acc

    vm = pl.BlockSpec(memory_space=pltpu.VMEM)
    return pl.pallas_call(body, name="small_sum", in_specs=[vm], out_specs=vm, out_shape=SDS(buf.shape[1:], f32))(buf)


def _adamw_math(w, g, m, v):
    m = ADAM_B1 * m + (1.0 - ADAM_B1) * g
    v = ADAM_B2 * v + (1.0 - ADAM_B2) * (g * g)
    m_hat = m / (1.0 - ADAM_B1 ** ADAM_STEP)
    v_hat = v / (1.0 - ADAM_B2 ** ADAM_STEP)
    delta = -ADAM_LR * (m_hat / (jnp.sqrt(v_hat) + ADAM_EPS) + ADAM_WD * w)
    return delta, m, v


def adamw_big(w, g, m, v):
    _, r, cdim = w.shape
    tr = _row_tile(r, 256)

    def body(w_ref, g_ref, m_ref, v_ref, go_ref, d_ref, mo_ref, vo_ref):
        g = g_ref[...]
        d, mm, vv = _adamw_math(w_ref[...], g, m_ref[...], v_ref[...])
        go_ref[...] = g
        d_ref[...] = d
        mo_ref[...] = mm
        vo_ref[...] = vv

    blk = pl.BlockSpec((1, tr, cdim), lambda l, i: (l, i, 0))
    return pl.pallas_call(
        body, name="adamw_big", grid=(DEPTH, r // tr), in_specs=[blk] * 4, out_specs=[blk] * 4,
        out_shape=[SDS(w.shape, f32)] * 4, compiler_params=_params(("parallel", "parallel")),
    )(w, g, m, v)


def adamw_small(ws, gs, ms, vs):
    n = len(ws)

    def body(*refs):
        w_r, g_r, m_r, v_r = refs[:n], refs[n:2 * n], refs[2 * n:3 * n], refs[3 * n:4 * n]
        d_o, m_o, v_o = refs[4 * n:5 * n], refs[5 * n:6 * n], refs[6 * n:7 * n]
        for k in range(n):
            d, mm, vv = _adamw_math(w_r[k][...], g_r[k][...], m_r[k][...], v_r[k][...])
            d_o[k][...] = d
            m_o[k][...] = mm
            v_o[k][...] = vv

    vm = pl.BlockSpec(memory_space=pltpu.VMEM)
    shapes = [SDS(a.shape, f32) for a in ws]
    outs = pl.pallas_call(
        body, name="adamw_small", in_specs=[vm] * (4 * n), out_specs=[vm] * (3 * n), out_shape=shapes * 3,
    )(*ws, *gs, *ms, *vs)
    return outs[:n], outs[n:2 * n], outs[2 * n:]


_BIG = ("w_in", "w_out", "w_up", "w_down")
_WEIGHTS = ("norm1", "w_in", "attn_sinks", "conv_dw_w", "conv_dw_b", "conv_ln_g", "conv_ln_b", "lru_conv_w", "lru_conv_b",
            "lru_wa", "lru_ba", "lru_wx", "lru_bx", "lru_lambda", "mix_norm", "w_out", "norm2", "w_up", "w_down", "final_norm")
_SMALL = tuple(n for n in _WEIGHTS if n not in _BIG)
_SMALL_FULL_SHAPE = dict(
    norm1=(DEPTH, D_MODEL), attn_sinks=(DEPTH, N_HEADS), conv_dw_w=(DEPTH, CONV_K, CONV_W), conv_dw_b=(DEPTH, CONV_W),
    conv_ln_g=(DEPTH, CONV_W), conv_ln_b=(DEPTH, CONV_W), lru_conv_w=(DEPTH, LRU_K, LRU_W), lru_conv_b=(DEPTH, LRU_W),
    lru_wa=(DEPTH, LRU_HEADS, 64, 64), lru_ba=(DEPTH, LRU_HEADS, 64), lru_wx=(DEPTH, LRU_HEADS, 64, 64),
    lru_bx=(DEPTH, LRU_HEADS, 64), lru_lambda=(DEPTH, LRU_W), mix_norm=(DEPTH, D_MODEL), norm2=(DEPTH, D_MODEL),
    final_norm=(D_MODEL,))
_CHANNEL_SHARDED = ("conv_dw_w", "lru_conv_w")


def _pad_lanes(n):
    return -(-n // LANES) * LANES


def _pack(named):
    flat = []
    for a in named:
        a = a.reshape(-1)
        flat.append(jnp.pad(a, (0, _pad_lanes(a.shape[0]) - a.shape[0])))
    v = jnp.concatenate(flat)
    rows = -(-v.shape[0] // (8 * LANES)) * 8
    return jnp.pad(v, (0, rows * LANES - v.shape[0])).reshape(rows, LANES)


def _unpack(vec, shapes):
    flat = vec.reshape(-1)
    out, off = [], 0
    for shp in shapes:
        n = math.prod(shp)
        out.append(flat[off:off + n].reshape(shp))
        off += _pad_lanes(n)
    return out


def _as2d(a):
    return a.reshape(-1, a.shape[-1]) if a.ndim > 1 else a.reshape(1, -1)


def kernel(x, norm1, w_in, attn_sinks, conv_dw_w, conv_dw_b, conv_ln_g, conv_ln_b, lru_conv_w, lru_conv_b, lru_wa, lru_ba, lru_wx, lru_bx, lru_lambda, mix_norm, w_out, norm2, w_up, w_down, final_norm, loss_target, m_norm1, m_w_in, m_attn_sinks, m_conv_dw_w, m_conv_dw_b, m_conv_ln_g, m_conv_ln_b, m_lru_conv_w, m_lru_conv_b, m_lru_wa, m_lru_ba, m_lru_wx, m_lru_bx, m_lru_lambda, m_mix_norm, m_w_out, m_norm2, m_w_up, m_w_down, m_final_norm, v_norm1, v_w_in, v_attn_sinks, v_conv_dw_w, v_conv_dw_b, v_conv_ln_g, v_conv_ln_b, v_lru_conv_w, v_lru_conv_b, v_lru_wa, v_lru_ba, v_lru_wx, v_lru_bx, v_lru_lambda, v_mix_norm, v_w_out, v_norm2, v_w_up, v_w_down, v_final_norm):
    wts = dict(norm1=norm1, w_in=w_in, attn_sinks=attn_sinks, conv_dw_w=conv_dw_w, conv_dw_b=conv_dw_b, conv_ln_g=conv_ln_g,
               conv_ln_b=conv_ln_b, lru_conv_w=lru_conv_w, lru_conv_b=lru_conv_b, lru_wa=lru_wa, lru_ba=lru_ba, lru_wx=lru_wx,
               lru_bx=lru_bx, lru_lambda=lru_lambda, mix_norm=mix_norm, w_out=w_out, norm2=norm2, w_up=w_up, w_down=w_down,
               final_norm=final_norm)
    mom = dict(norm1=m_norm1, w_in=m_w_in, attn_sinks=m_attn_sinks, conv_dw_w=m_conv_dw_w, conv_dw_b=m_conv_dw_b,
               conv_ln_g=m_conv_ln_g, conv_ln_b=m_conv_ln_b, lru_conv_w=m_lru_conv_w, lru_conv_b=m_lru_conv_b, lru_wa=m_lru_wa,
               lru_ba=m_lru_ba, lru_wx=m_lru_wx, lru_bx=m_lru_bx, lru_lambda=m_lru_lambda, mix_norm=m_mix_norm, w_out=m_w_out,
               norm2=m_norm2, w_up=m_w_up, w_down=m_w_down, final_norm=m_final_norm)
    var = dict(norm1=v_norm1, w_in=v_w_in, attn_sinks=v_attn_sinks, conv_dw_w=v_conv_dw_w, conv_dw_b=v_conv_dw_b,
               conv_ln_g=v_conv_ln_g, conv_ln_b=v_conv_ln_b, lru_conv_w=v_lru_conv_w, lru_conv_b=v_lru_conv_b, lru_wa=v_lru_wa,
               lru_ba=v_lru_ba, lru_wx=v_lru_wx, lru_bx=v_lru_bx, lru_lambda=v_lru_lambda, mix_norm=v_mix_norm, w_out=v_w_out,
               norm2=v_norm2, w_up=v_w_up, w_down=v_w_down, final_norm=v_final_norm)

    c_idx = lax.axis_index("c").astype(jnp.int32)
    s_idx = (2 * lax.axis_index("x") + lax.axis_index("y")).astype(jnp.int32)
    idx = jnp.stack([c_idx, s_idx])

    for d in (wts, mom, var):
        d["w_in"] = d["w_in"].transpose(0, 2, 1)

    cs = MeshWeights(wts["w_in"], w_out, w_up, w_down, conv_dw_w, lru_conv_w, idx)
    sp = {n: wts[n] for n in _SMALL}
    sp["conv_dw_w"], sp["lru_conv_w"] = cs.conv_weights()
    loss_blk, grad_x, _, small_g, d_gf = train_local(x[0], loss_target[0], sp, cs)

    stacked = [jnp.stack([small_g[l][n] for l in range(DEPTH)]) for n in _SMALL if n != "final_norm"]
    tots, gathered = cs.finish(_pack(stacked + [d_gf, loss_blk[0, 0:1]]))
    grads_big = {n: a.reshape(wts[n].shape) for n, a in zip(_BIG, halves_exchange(tots))}
    summed = small_sum(gathered)
    names = [n for n in _SMALL if n != "final_norm"] + ["final_norm"]
    unpacked = _unpack(summed, [_SMALL_FULL_SHAPE[n] for n in names] + [(1,)])
    loss = unpacked[-1][0]
    grads = dict(zip(names, unpacked[:-1]))
    for n in _CHANNEL_SHARDED:
        width = wts[n].shape[-1]
        grads[n] = lax.dynamic_slice_in_dim(grads[n], s_idx * width, width, axis=2)
    grads.update(grads_big)

    delta, new_m, new_v = {}, {}, {}
    for n in _BIG:
        grads[n], delta[n], new_m[n], new_v[n] = adamw_big(wts[n], grads[n], mom[n], var[n])
    for d in (grads, delta, new_m, new_v):
        d["w_in"] = d["w_in"].transpose(0, 2, 1)
    sm = list(_SMALL)
    d_s, m_s, v_s = adamw_small([_as2d(wts[n]) for n in sm], [_as2d(grads[n]) for n in sm],
                                [_as2d(mom[n]) for n in sm], [_as2d(var[n]) for n in sm])
    for k, n in enumerate(sm):
        delta[n], new_m[n], new_v[n] = (a.reshape(wts[n].shape) for a in (d_s[k], m_s[k], v_s[k]))

    return (loss, grad_x[None], *[grads[n] for n in _WEIGHTS], *[delta[n] for n in _WEIGHTS],
            *[new_m[n] for n in _WEIGHTS], *[new_v[n] for n in _WEIGHTS])
```

```python
import functools
import math

import jax
import jax.numpy as jnp
from jax import lax
from jax.experimental import pallas as pl
from jax.experimental.pallas import tpu as pltpu

f32 = jnp.float32
bf16 = jnp.bfloat16
SDS = jax.ShapeDtypeStruct

D_MODEL = 1024
DEPTH = 2
ATTN_W = 512
KV_W = 128
HEAD_DIM = 64
N_HEADS = 8
BLOCK = 128
CONV_W = 256
CONV_K = 31
LRU_W = 256
LRU_K = 4
LRU_HEADS = 4
LRU_C = 8.0
IN_W = 1792
D_FF = 4096
N_SHARD = 4
FF_CHUNK = D_FF // N_SHARD
RMS_EPS = 1e-6
LN_EPS = 1e-5
MASK_VALUE = -1e30
HALO = 32
LANES = 128
VMEM_LIMIT = 56 * 1024 * 1024

ADAM_LR = 0.001
ADAM_B1 = 0.9
ADAM_B2 = 0.999
ADAM_EPS = 1e-08
ADAM_WD = 0.01
ADAM_STEP = 10

MESH = pl.DeviceIdType.MESH


def _dot(a, b):
    return jnp.dot(a, b, preferred_element_type=f32)


def _dot_nt(a, b):
    return lax.dot_general(a, b, (((1,), (1,)), ((), ())), preferred_element_type=f32)


def _dot_tn(a, b):
    return lax.dot_general(a, b, (((0,), (0,)), ((), ())), preferred_element_type=f32)


def _rms_fwd(x, g):
    r = lax.rsqrt(jnp.mean(x * x, axis=-1, keepdims=True) + RMS_EPS)
    return x * r * g, r


def _rms_bwd(dy, x, r, g):
    t = dy * g
    dx = r * t - x * (r * r * r) * jnp.mean(t * x, axis=-1, keepdims=True)
    dg = jnp.sum(dy * x * r, axis=0, keepdims=True)
    return dx, dg


def _sigmoid(x):
    return jax.nn.sigmoid(x)


_GELU_K = math.sqrt(2.0 / math.pi)


def _gelu(x):
    t = jnp.tanh(_GELU_K * (x + 0.044715 * x * x * x))
    return 0.5 * x * (1.0 + t), t


def _gelu_grad(x, t):
    return 0.5 * (1.0 + t) + 0.5 * x * (1.0 - t * t) * _GELU_K * (1.0 + 3.0 * 0.044715 * x * x)


def _log1p(x):
    return jnp.where(x < 1e-4, x - 0.5 * x * x, jnp.log(1.0 + x))


def _softplus(x):
    return jnp.maximum(x, 0.0) + _log1p(jnp.exp(-jnp.abs(x)))


def _neg_expm1(x):
    series = -x * (1.0 + 0.5 * x * (1.0 + x * (1.0 / 3.0) * (1.0 + 0.25 * x)))
    return jnp.where(x > -0.01, series, 1.0 - jnp.exp(x))


def _sublane_rolls(x, count, forward):
    n = x.shape[0]
    return [x if b == 0 else pltpu.roll(x, b if forward else n - b, 0) for b in range(count)]


def _conv_taps(xpad, w, k_width):
    t_rows = xpad.shape[0] - HALO
    rolled = _sublane_rolls(xpad, min(k_width, 8), forward=True)
    acc = None
    for k in range(k_width):
        hi, lo = divmod((k_width - 1) - k, 8)
        term = rolled[lo][HALO - 8 * hi:HALO - 8 * hi + t_rows] * w[k:k + 1, :]
        acc = term if acc is None else acc + term
    return acc


def _conv_taps_bwd(dpad, upad, w, k_width, t_rows):
    n_lo = min(k_width, 8)
    d_rolled = _sublane_rolls(dpad, n_lo, forward=False)
    u_rolled = _sublane_rolls(upad, n_lo, forward=True)
    d_in = None
    dw_rows = []
    d_out = dpad[:t_rows]
    for k in range(k_width):
        hi, lo = divmod((k_width - 1) - k, 8)
        term = d_rolled[lo][8 * hi:8 * hi + t_rows] * w[k:k + 1, :]
        d_in = term if d_in is None else d_in + term
        us = u_rolled[lo][HALO - 8 * hi:HALO - 8 * hi + t_rows]
        dw_rows.append(jnp.sum(d_out * us, axis=0, keepdims=True))
    return d_in, dw_rows


SUBLANES = 8


def _scan_fwd(a, b, h0):
    t_rows = a.shape[0]
    sub = jnp.bitwise_and(lax.broadcasted_iota(jnp.int32, a.shape, 0), SUBLANES - 1)
    for d in (1, 2, 4):
        a_sh = jnp.where(sub < d, 1.0, pltpu.roll(a, d, 0))
        b_sh = jnp.where(sub < d, 0.0, pltpu.roll(b, d, 0))
        b = a * b_sh + b
        a = a * a_sh
    out, carry = [], h0
    for g in range(t_rows // SUBLANES):
        rows = slice(g * SUBLANES, (g + 1) * SUBLANES)
        hg = a[rows] * carry + b[rows]
        out.append(hg)
        carry = hg[SUBLANES - 1:SUBLANES]
    return jnp.concatenate(out, axis=0)


def _scan_bwd(a, b, l_end):
    t_rows = a.shape[0]
    sub = jnp.bitwise_and(lax.broadcasted_iota(jnp.int32, a.shape, 0), SUBLANES - 1)
    for d in (1, 2, 4):
        a_sh = jnp.where(sub >= SUBLANES - d, 1.0, pltpu.roll(a, t_rows - d, 0))
        b_sh = jnp.where(sub >= SUBLANES - d, 0.0, pltpu.roll(b, t_rows - d, 0))
        b = b + a * b_sh
        a = a * a_sh
    out, carry = [], l_end
    for g in reversed(range(t_rows // SUBLANES)):
        rows = slice(g * SUBLANES, (g + 1) * SUBLANES)
        lg = b[rows] + a[rows] * carry
        out.append(lg)
        carry = lg[0:1]
    return jnp.concatenate(out[::-1], axis=0)


def _full(shape, single=False):
    nd = len(shape)
    if single:
        return pl.BlockSpec(shape, lambda *_: (0,) * nd, pipeline_mode=pl.Buffered(1))
    return pl.BlockSpec(shape, lambda *_: (0,) * nd)


def _params(sem, vmem=None):
    return pltpu.CompilerParams(dimension_semantics=sem, vmem_limit_bytes=vmem)


def _tile(s):
    return min(512, s)


def _row_tile(rows, cap):
    return next(t for t in range(min(cap, rows) // 8 * 8, 0, -8) if rows % t == 0)


def inproj_fwd(h, g1, w_in_t, pieces=None):
    s = h.shape[0]
    tm = _tile(s)

    def body(h_ref, g_ref, w_ref, z_ref):
        hn, _ = _rms_fwd(h_ref[...], g_ref[...])
        z_ref[...] = _dot_nt(hn.astype(bf16), w_ref[...]).astype(bf16)

    return _pcall(
        body, pieces, name="inproj_fwd", grid=(s // tm,),
        in_specs=[pl.BlockSpec((tm, D_MODEL), lambda i: (i, 0)), _full((1, D_MODEL)), _full((IN_W, D_MODEL))],
        out_specs=[pl.BlockSpec((tm, IN_W), lambda i: (i, 0))],
        out_shape=[SDS((s, IN_W), bf16)],
        operands=[h, g1, w_in_t], sem=("parallel",), vmem=VMEM_LIMIT)[0]


ATT_ROWS = N_HEADS * BLOCK
ATT_BLOCKS_PER_STEP = 8


def _attn_bias():
    qi = jnp.arange(BLOCK)[None, :]
    key = jnp.arange(2 * BLOCK)[:, None]
    band = (key > qi) & (key <= qi + BLOCK)
    first = band & (key >= BLOCK)
    mask = jnp.where(jnp.stack([first, band]), 0.0, MASK_VALUE).astype(f32)
    return jnp.tile(mask, (1, 1, N_HEADS // 2))


def _attn_band(kvc, kvp):
    kb = jnp.concatenate([kvp[:, :KV_W], kvc[:, :KV_W]], axis=0)
    vb = jnp.concatenate([kvp[:, KV_W:], kvc[:, KV_W:]], axis=0)
    lane = lax.broadcasted_iota(jnp.int32, kb.shape, 1)
    kb_sw = pltpu.roll(kb, HEAD_DIM, 1)
    vb_sw = pltpu.roll(vb, HEAD_DIM, 1)
    kx = [jnp.where(lane < HEAD_DIM, kb, kb_sw), jnp.where(lane >= HEAD_DIM, kb, kb_sw)]
    vx = [jnp.where(lane < HEAD_DIM, vb, vb_sw), jnp.where(lane >= HEAD_DIM, vb, vb_sw)]
    return kx, vx


def _stack_heads(x, mlo):
    zero = jnp.zeros((BLOCK, LANES), x.dtype)
    out = []
    for hk in range(2):
        parts = []
        for j in (2 * hk, 2 * hk + 1):
            xj = x[:, j * LANES:(j + 1) * LANES]
            parts += [jnp.where(mlo, xj, zero), jnp.where(mlo, zero, xj)]
        out.append(jnp.concatenate(parts, axis=0))
    return out


def _unstack_heads(y, mlo):
    cols = []
    for hk in range(2):
        for t in range(2):
            base = 2 * t * BLOCK
            cols.append(jnp.where(mlo, y[hk][base:base + BLOCK], y[hk][base + BLOCK:base + 2 * BLOCK]))
    return jnp.concatenate(cols, axis=1)


def _attn_probs(q4, kx, bias_t, sink_row):
    out = []
    half = ATT_ROWS // 2
    for hk in range(2):
        s = _dot_nt(kx[hk], q4[hk]) + bias_t
        sink = sink_row[:, hk * half:(hk + 1) * half]
        m = jnp.maximum(jnp.max(s, axis=0, keepdims=True), sink)
        p = jnp.exp(s - m)
        e_sink = jnp.exp(sink - m)
        inv = 1.0 / (jnp.sum(p, axis=0, keepdims=True) + e_sink)
        out.append((p * inv, e_sink * inv))
    return out


def attn_fwd(z, sink_row, bias, g_a, pieces=None):
    s = z.shape[0]
    per = min(ATT_BLOCKS_PER_STEP, s // BLOCK)
    tq = per * BLOCK

    def body(q_ref, kv_ref, kvp_ref, sk_ref, b_ref, g_ref, o_ref, y_ref):
        n = pl.program_id(0)
        mlo = lax.broadcasted_iota(jnp.int32, (BLOCK, LANES), 1) < HEAD_DIM
        for b in range(per):
            rows = slice(b * BLOCK, (b + 1) * BLOCK)
            kvp = kvp_ref[...] if b == 0 else kv_ref[(b - 1) * BLOCK:b * BLOCK, :]
            bias_b = b_ref[jnp.minimum(n, 1)] if b == 0 else b_ref[1]
            kx, vx = _attn_band(kv_ref[rows, :], kvp)
            q4 = _stack_heads(q_ref[rows, :] * (HEAD_DIM ** -0.5), mlo)
            probs = _attn_probs(q4, kx, bias_b, sk_ref[...])
            o = _unstack_heads([_dot_tn(probs[hk][0].astype(bf16), vx[hk]) for hk in range(2)], mlo)
            o_ref[rows, :] = o.astype(bf16)
            y, _ = _rms_fwd(o, g_ref[...])
            y_ref[rows, :] = y.astype(bf16)

    return _pcall(
        body, pieces, name="attn_fwd", grid=(s // tq,),
        in_specs=[pl.BlockSpec((tq, ATTN_W), lambda n: (n, 0)),
                  pl.BlockSpec((tq, 2 * KV_W), lambda n: (n, 2)),
                  pl.BlockSpec((BLOCK, 2 * KV_W), lambda n: (jnp.maximum(n * per - 1, 0), 2)),
                  _full((1, ATT_ROWS)), _full((2, 2 * BLOCK, ATT_ROWS // 2)), _full((1, ATTN_W))],
        out_specs=[pl.BlockSpec((tq, ATTN_W), lambda n: (n, 0)), pl.BlockSpec((tq, ATTN_W), lambda n: (n, 0))],
        out_shape=[SDS((s, ATTN_W), bf16), SDS((s, ATTN_W), bf16)],
        operands=[z, z, z, sink_row, bias, g_a], sem=("parallel",))


def _lru_gates(xc, wa, ba, wx, bx, lam):
    xcb = xc.astype(bf16)
    r = _sigmoid(_dot(xcb, wa) + ba)
    ig = _sigmoid(_dot(xcb, wx) + bx)
    sp = _softplus(-lam)
    la = (-LRU_C * r) * sp
    a = jnp.exp(la)
    mult = jnp.sqrt(_neg_expm1(2.0 * la))
    return r, ig, sp, la, a, mult


def branch_fwd(z, p, pieces=None):
    s = z.shape[0]
    tm = _tile(s)
    hb = tm // HALO

    def body(cv_ref, cg_ref, rx_ref, rg_ref, cvh_ref, cgh_ref, rxh_ref,
             cw_ref, cb_ref, lng_ref, lnb_ref, lw_ref, lb_ref, wa_ref, ba_ref, wx_ref, bx_ref, lam_ref, gc_ref, gl_ref,
             conv_ref, hst_ref, nc_ref, nl_ref, carry_ref):
        i = pl.program_id(0)
        first = i == 0

        @pl.when(first)
        def _():
            carry_ref[...] = jnp.zeros_like(carry_ref)

        cval = cv_ref[...].astype(f32)
        u = cval * _sigmoid(cg_ref[...].astype(f32))
        hu = jnp.where(first, 0.0, cvh_ref[...].astype(f32) * _sigmoid(cgh_ref[...].astype(f32)))
        conv = _conv_taps(jnp.concatenate([hu, u], axis=0), cw_ref[...], CONV_K) + cb_ref[...]
        conv_ref[...] = conv
        mu = jnp.mean(conv, axis=-1, keepdims=True)
        xm = conv - mu
        rstd = lax.rsqrt(jnp.mean(xm * xm, axis=-1, keepdims=True) + LN_EPS)
        ln = xm * rstd * lng_ref[...] + lnb_ref[...]
        yc = ln * _sigmoid(ln)
        nc, _ = _rms_fwd(yc, gc_ref[...])
        nc_ref[...] = nc.astype(bf16)

        rx = rx_ref[...].astype(f32)
        hrx = jnp.where(first, 0.0, rxh_ref[...].astype(f32))
        xc = _conv_taps(jnp.concatenate([hrx, rx], axis=0), lw_ref[...], LRU_K) + lb_ref[...]
        r, ig, sp, la, a, mult = _lru_gates(xc, wa_ref[...], ba_ref[...], wx_ref[...], bx_ref[...], lam_ref[...])
        gx = mult * (ig * xc)
        hs = _scan_fwd(a, gx, carry_ref[0:1, :])
        carry_ref[...] = jnp.broadcast_to(hs[tm - 1:tm, :], carry_ref.shape)
        hst_ref[...] = hs
        gl, _ = _gelu(rg_ref[...].astype(f32))
        nl, _ = _rms_fwd(hs * gl, gl_ref[...])
        nl_ref[...] = nl.astype(bf16)

    def col(c):
        return pl.BlockSpec((tm, CONV_W), lambda i: (i, c))

    def halo(c):
        return pl.BlockSpec((HALO, CONV_W), lambda i: (jnp.maximum(i * hb - 1, 0), c))

    small = [p["cw"], p["cb"], p["lng"], p["lnb"], p["lw"], p["lb"], p["wa"], p["ba"], p["wx"], p["bx"], p["lam"],
             p["gc"], p["gl"]]
    row = pl.BlockSpec((tm, CONV_W), lambda i: (i, 0))
    return _pcall(
        body, pieces, name="branch_fwd", grid=(s // tm,),
        in_specs=[col(3), col(4), col(5), col(6), halo(3), halo(4), halo(5)] + [_full(a.shape) for a in small],
        out_specs=[row, row, row, row],
        out_shape=[SDS((s, CONV_W), f32), SDS((s, LRU_W), f32), SDS((s, CONV_W), bf16), SDS((s, LRU_W), bf16)],
        scratch_shapes=[pltpu.VMEM((8, LRU_W), f32)],
        operands=[z, z, z, z, z, z, z, *small], sem=("arbitrary",))


def outproj_fwd(ya, yc, yl, h, w_out, g2, pieces=None):
    s = h.shape[0]
    tm = _tile(s)

    def body(ya_ref, yc_ref, yl_ref, h_ref, w_ref, g_ref, h1_ref, hn_ref):
        y = jnp.concatenate([ya_ref[...], yc_ref[...], yl_ref[...]], axis=1)
        h1 = h_ref[...] + _dot(y, w_ref[...])
        h1_ref[...] = h1
        hn, _ = _rms_fwd(h1, g_ref[...])
        hn_ref[...] = hn.astype(bf16)

    return _pcall(
        body, pieces, name="outproj_fwd", grid=(s // tm,),
        in_specs=[pl.BlockSpec((tm, ATTN_W), lambda i: (i, 0)), pl.BlockSpec((tm, CONV_W), lambda i: (i, 0)),
                  pl.BlockSpec((tm, LRU_W), lambda i: (i, 0)), pl.BlockSpec((tm, D_MODEL), lambda i: (i, 0)),
                  _full((D_MODEL, D_MODEL)), _full((1, D_MODEL))],
        out_specs=[pl.BlockSpec((tm, D_MODEL), lambda i: (i, 0)), pl.BlockSpec((tm, D_MODEL), lambda i: (i, 0))],
        out_shape=[SDS((s, D_MODEL), f32), SDS((s, D_MODEL), bf16)],
        operands=[ya, yc, yl, h, w_out, g2], sem=("parallel",), vmem=VMEM_LIMIT)


def mlp_fwd(hn2, h1, w_up, w_dn, pieces=None):
    s = h1.shape[0]
    tm = _tile(s)

    def body(x_ref, h_ref, wu_ref, wd_ref, up_ref, h2_ref):
        h2_ref[...] = _mlp_chunks(x_ref[...], h_ref[...], wu_ref, wd_ref, up_ref)

    return _pcall(
        body, pieces, name="mlp_fwd", grid=(s // tm,),
        in_specs=[pl.BlockSpec((tm, D_MODEL), lambda i: (i, 0)), pl.BlockSpec((tm, D_MODEL), lambda i: (i, 0)),
                  _full((N_SHARD, D_MODEL, FF_CHUNK), single=True), _full((N_SHARD, FF_CHUNK, D_MODEL), single=True)],
        out_specs=[pl.BlockSpec((tm, D_FF), lambda i: (i, 0)), pl.BlockSpec((tm, D_MODEL), lambda i: (i, 0))],
        out_shape=[SDS((s, D_FF), bf16), SDS((s, D_MODEL), f32)],
        operands=[hn2, h1, w_up, w_dn], sem=("parallel",), vmem=VMEM_LIMIT)


def _mlp_chunks(x, acc, wu_ref, wd_ref, up_ref):
    for c in range(N_SHARD):
        u = _dot(x, wu_ref[c])
        up_ref[:, c * FF_CHUNK:(c + 1) * FF_CHUNK] = u.astype(bf16)
        act = jnp.square(jnp.maximum(u, 0.0)).astype(bf16)
        acc = acc + _dot(act, wd_ref[c])
    return acc


def _final_tile(x, tgt, g, dh_ref, loss_ref, dg_ref):
    y, r = _rms_fwd(x, g)
    err = y - tgt
    part = 0.5 * jnp.sum(jnp.mean(err * err, axis=-1, keepdims=True), axis=0, keepdims=True)
    loss_ref[...] += jnp.broadcast_to(part, loss_ref.shape)
    dx, dg = _rms_bwd(err * (1.0 / D_MODEL), x, r, g)
    dh_ref[...] = dx
    dg_ref[...] += dg


def last_layer_fwd(ya, yc, yl, h, w_out, g2, w_up, w_dn, tgt, gf):
    s = h.shape[0]
    tm = min(256, s)

    def body(ya_ref, yc_ref, yl_ref, h_ref, wo_ref, g2_ref, wu_ref, wd_ref, t_ref, gf_ref,
             h1_ref, hn_ref, up_ref, dh_ref, loss_ref, dg_ref):
        i = pl.program_id(0)

        @pl.when(i == 0)
        def _():
            loss_ref[...] = jnp.zeros_like(loss_ref)
            dg_ref[...] = jnp.zeros_like(dg_ref)

        y = jnp.concatenate([ya_ref[...], yc_ref[...], yl_ref[...]], axis=1)
        h1 = h_ref[...] + _dot(y, wo_ref[...])
        h1_ref[...] = h1
        hn, _ = _rms_fwd(h1, g2_ref[...])
        hn = hn.astype(bf16)
        hn_ref[...] = hn
        h2 = _mlp_chunks(hn, h1, wu_ref, wd_ref, up_ref)
        _final_tile(h2, t_ref[...], gf_ref[...], dh_ref, loss_ref, dg_ref)

    def rowb(w):
        return pl.BlockSpec((tm, w), lambda i: (i, 0))

    return pl.pallas_call(
        body, name="last_layer_fwd", grid=(s // tm,),
        in_specs=[rowb(ATTN_W), rowb(CONV_W), rowb(LRU_W), rowb(D_MODEL), _full((D_MODEL, D_MODEL), single=True),
                  _full((1, D_MODEL)), _full((N_SHARD, D_MODEL, FF_CHUNK), single=True),
                  _full((N_SHARD, FF_CHUNK, D_MODEL), single=True), rowb(D_MODEL), _full((1, D_MODEL))],
        out_specs=[rowb(D_MODEL), rowb(D_MODEL), rowb(D_FF), rowb(D_MODEL), _full((8, LANES)), _full((1, D_MODEL))],
        out_shape=[SDS((s, D_MODEL), f32), SDS((s, D_MODEL), bf16), SDS((s, D_FF), bf16), SDS((s, D_MODEL), f32),
                   SDS((8, LANES), f32), SDS((1, D_MODEL), f32)],
        compiler_params=_params(("arbitrary",), VMEM_LIMIT),
    )(ya, yc, yl, h, w_out, g2, w_up, w_dn, tgt, gf)


def mlp_bwd_act(dh, up, h1, g2, w_up, w_dn, pieces=None):
    s = dh.shape[0]
    tm = _tile(s)

    def body(dh_ref, up_ref, h1_ref, g_ref, wu_ref, wd_ref, dup_ref, dh1_ref, dg_ref):
        i = pl.program_id(0)

        @pl.when(i == 0)
        def _():
            dg_ref[...] = jnp.zeros_like(dg_ref)

        dh = dh_ref[...]
        dhb = dh.astype(bf16)
        d_hn = jnp.zeros((tm, D_MODEL), f32)
        for c in range(N_SHARD):
            d_act = _dot_nt(dhb, wd_ref[c])
            u = up_ref[:, c * FF_CHUNK:(c + 1) * FF_CHUNK].astype(f32)
            d_u = (d_act * (2.0 * jnp.maximum(u, 0.0))).astype(bf16)
            dup_ref[:, c * FF_CHUNK:(c + 1) * FF_CHUNK] = d_u
            d_hn = d_hn + _dot_nt(d_u, wu_ref[c])
        x = h1_ref[...]
        g = g_ref[...]
        _, r = _rms_fwd(x, g)
        dx, dg = _rms_bwd(d_hn, x, r, g)
        dh1_ref[...] = dh + dx
        dg_ref[...] += dg

    return _pcall(
        body, pieces, name="mlp_bwd_act", grid=(s // tm,),
        in_specs=[pl.BlockSpec((tm, D_MODEL), lambda i: (i, 0)), pl.BlockSpec((tm, D_FF), lambda i: (i, 0)),
                  pl.BlockSpec((tm, D_MODEL), lambda i: (i, 0)), _full((1, D_MODEL)),
                  _full((N_SHARD, D_MODEL, FF_CHUNK), single=True), _full((N_SHARD, FF_CHUNK, D_MODEL), single=True)],
        out_specs=[pl.BlockSpec((tm, D_FF), lambda i: (i, 0)), pl.BlockSpec((tm, D_MODEL), lambda i: (i, 0)),
                   _full((1, D_MODEL))],
        out_shape=[SDS((s, D_FF), bf16), SDS((s, D_MODEL), f32), SDS((1, D_MODEL), f32)],
        operands=[dh, up, h1, g2, w_up, w_dn], sem=("arbitrary",), vmem=VMEM_LIMIT)


def mlp_bwd_w(hn2, d_up, up, dh, pieces=None):
    s = dh.shape[0]
    tk = min(1024, s)

    def body(x_ref, du_ref, up_ref, dh_ref, dwu_ref, dwd_ref):
        k = pl.program_id(1)

        @pl.when(k == 0)
        def _():
            dwu_ref[...] = jnp.zeros_like(dwu_ref)
            dwd_ref[...] = jnp.zeros_like(dwd_ref)

        dwu_ref[0] += _dot_tn(x_ref[...], du_ref[...])
        act = jnp.square(jnp.maximum(up_ref[...].astype(f32), 0.0)).astype(bf16)
        dwd_ref[0] += _dot_tn(act, dh_ref[...].astype(bf16))

    return _pcall(
        body, pieces, name="mlp_bwd_w", grid=(N_SHARD, s // tk),
        in_specs=[pl.BlockSpec((tk, D_MODEL), lambda c, k: (k, 0)), pl.BlockSpec((tk, FF_CHUNK), lambda c, k: (k, c)),
                  pl.BlockSpec((tk, FF_CHUNK), lambda c, k: (k, c)), pl.BlockSpec((tk, D_MODEL), lambda c, k: (k, 0))],
        out_specs=[pl.BlockSpec((1, D_MODEL, FF_CHUNK), lambda c, k: (c, 0, 0)),
                   pl.BlockSpec((1, FF_CHUNK, D_MODEL), lambda c, k: (c, 0, 0))],
        out_shape=[SDS((N_SHARD, D_MODEL, FF_CHUNK), f32), SDS((N_SHARD, FF_CHUNK, D_MODEL), f32)],
        operands=[hn2, d_up, up, dh], sem=("parallel", "arbitrary"), vmem=VMEM_LIMIT)


def outproj_bwd(dh1, ya, yc, yl, w_out):
    s = dh1.shape[0]
    tm = min(1024, s)

    def body(dh_ref, ya_ref, yc_ref, yl_ref, w_ref, dy_ref, dw_ref):
        i = pl.program_id(0)

        @pl.when(i == 0)
        def _():
            dw_ref[...] = jnp.zeros_like(dw_ref)

        dhb = dh_ref[...].astype(bf16)
        dy_ref[...] = _dot_nt(dhb, w_ref[...])
        y = jnp.concatenate([ya_ref[...], yc_ref[...], yl_ref[...]], axis=1)
        dw_ref[...] += _dot_tn(y, dhb)

    return pl.pallas_call(
        body, name="outproj_bwd", grid=(s // tm,),
        in_specs=[pl.BlockSpec((tm, D_MODEL), lambda i: (i, 0)), pl.BlockSpec((tm, ATTN_W), lambda i: (i, 0)),
                  pl.BlockSpec((tm, CONV_W), lambda i: (i, 0)), pl.BlockSpec((tm, LRU_W), lambda i: (i, 0)),
                  _full((D_MODEL, D_MODEL))],
        out_specs=[pl.BlockSpec((tm, D_MODEL), lambda i: (i, 0)), _full((D_MODEL, D_MODEL))],
        out_shape=[SDS((s, D_MODEL), f32), SDS((D_MODEL, D_MODEL), f32)],
        compiler_params=_params(("arbitrary",), VMEM_LIMIT),
    )(dh1, ya, yc, yl, w_out)


def attn_bwd(z, o, dy, sink_row, bias, g_a, pieces=None):
    s = z.shape[0]
    per = min(ATT_BLOCKS_PER_STEP, s // BLOCK)
    tq = per * BLOCK
    nt = s // tq

    def body(q_ref, kv_ref, kvp_ref, o_ref, dy_ref, sk_ref, b_ref, g_ref, dq_ref, dkv_ref, dsk_ref, dg_ref,
             carry_ref, dsk_acc):
        i = pl.program_id(0)
        t = nt - 1 - i

        @pl.when(i == 0)
        def _():
            carry_ref[...] = jnp.zeros_like(carry_ref)
            dsk_acc[...] = jnp.zeros_like(dsk_acc)
            dg_ref[...] = jnp.zeros_like(dg_ref)

        mlo = lax.broadcasted_iota(jnp.int32, (BLOCK, LANES), 1) < HEAD_DIM
        lane = lax.broadcasted_iota(jnp.int32, (2 * BLOCK, LANES), 1)
        scale = HEAD_DIM ** -0.5
        half = ATT_ROWS // 2
        g = g_ref[...]
        bands = []
        for b in range(per):
            rows = slice(b * BLOCK, (b + 1) * BLOCK)
            kvp = kvp_ref[...] if b == 0 else kv_ref[(b - 1) * BLOCK:b * BLOCK, :]
            bias_b = b_ref[jnp.minimum(t, 1)] if b == 0 else b_ref[1]
            kx, vx = _attn_band(kv_ref[rows, :], kvp)
            q4 = _stack_heads(q_ref[rows, :] * scale, mlo)
            o_f = o_ref[rows, :].astype(f32)
            _, r = _rms_fwd(o_f, g)
            d_o, dg = _rms_bwd(dy_ref[rows, :], o_f, r, g)
            dg_ref[...] += dg
            do4 = _stack_heads(d_o.astype(bf16), mlo)
            probs = _attn_probs(q4, kx, bias_b, sk_ref[...])
            dq4, tk, tv = [], [], []
            for hk in range(2):
                pr, p_sink = probs[hk]
                d_p = _dot_nt(vx[hk], do4[hk])
                d_row = jnp.sum(pr * d_p, axis=0, keepdims=True)
                d_s = (pr * (d_p - d_row)).astype(bf16)
                dsk_acc[:, hk * half:(hk + 1) * half] -= p_sink * d_row
                dq4.append(_dot_tn(d_s, kx[hk]))
                tk.append(_dot(d_s, q4[hk]))
                tv.append(_dot(pr.astype(bf16), do4[hk]))
            dq_ref[rows, :] = (_unstack_heads(dq4, mlo) * scale).astype(bf16)
            fk = [x + pltpu.roll(x, HEAD_DIM, 1) for x in tk]
            fv = [x + pltpu.roll(x, HEAD_DIM, 1) for x in tv]
            bands.append(jnp.concatenate([jnp.where(lane < HEAD_DIM, fk[0], fk[1]),
                                          jnp.where(lane < HEAD_DIM, fv[0], fv[1])], axis=1))
        for b in range(per):
            after = bands[b + 1][:BLOCK] if b + 1 < per else carry_ref[...]
            dkv_ref[b * BLOCK:(b + 1) * BLOCK, :] = (bands[b][BLOCK:] + after).astype(bf16)
        carry_ref[...] = bands[0][:BLOCK]

        @pl.when(i == nt - 1)
        def _():
            for hh in range(N_HEADS):
                tot = jnp.sum(dsk_acc[:, hh * BLOCK:(hh + 1) * BLOCK], axis=1, keepdims=True)
                dsk_ref[hh:hh + 1, :] = jnp.broadcast_to(tot, (1, LANES))

    def rev(width, col):
        return pl.BlockSpec((tq, width), lambda i: (nt - 1 - i, col))

    return _pcall(
        body, pieces, name="attn_bwd", grid=(nt,),
        in_specs=[rev(ATTN_W, 0), rev(2 * KV_W, 2),
                  pl.BlockSpec((BLOCK, 2 * KV_W), lambda i: (jnp.maximum((nt - 1 - i) * per - 1, 0), 2)),
                  rev(ATTN_W, 0), rev(ATTN_W, 0),
                  _full((1, ATT_ROWS)), _full((2, 2 * BLOCK, ATT_ROWS // 2)), _full((1, ATTN_W))],
        out_specs=[rev(ATTN_W, 0), rev(2 * KV_W, 0), _full((N_HEADS, LANES)), _full((1, ATTN_W))],
        out_shape=[SDS((s, ATTN_W), bf16), SDS((s, 2 * KV_W), bf16), SDS((N_HEADS, LANES), f32), SDS((1, ATTN_W), f32)],
        scratch_shapes=[pltpu.VMEM((BLOCK, 2 * KV_W), f32), pltpu.VMEM((1, ATT_ROWS), f32)],
        operands=[z, z, z, o, dy, sink_row, bias, g_a], sem=("arbitrary",))


_V_GC, _V_LNG, _V_LNB, _V_CB, _V_GL, _V_BA, _V_BX, _V_LAM, _V_LB = range(9)
_V_ROWS = 16


def branch_bwd_a(z, conv, hst, dy, p, pieces=None):
    s = z.shape[0]
    tm = _tile(s)
    nt = s // tm
    hb = tm // HALO
    h8 = tm // 8

    def body(conv_ref, dyc_ref, dyl_ref, rx_ref, rxh_ref, rg_ref, hst_ref, hsth_ref,
             lng_ref, lnb_ref, lw_ref, lb_ref, wa_ref, ba_ref, wx_ref, bx_ref, lam_ref, gc_ref, gl_ref,
             dconv_ref, dxc_ref, drg_ref, vec_ref, dwa_ref, dwx_ref, carry_ref):
        i = pl.program_id(0)
        ti = nt - 1 - i

        @pl.when(i == 0)
        def _():
            carry_ref[...] = jnp.zeros_like(carry_ref)
            vec_ref[...] = jnp.zeros_like(vec_ref)
            dwa_ref[...] = jnp.zeros_like(dwa_ref)
            dwx_ref[...] = jnp.zeros_like(dwx_ref)

        conv = conv_ref[...]
        mu = jnp.mean(conv, axis=-1, keepdims=True)
        xm = conv - mu
        rstd = lax.rsqrt(jnp.mean(xm * xm, axis=-1, keepdims=True) + LN_EPS)
        xhat = xm * rstd
        lng = lng_ref[...]
        ln = xhat * lng + lnb_ref[...]
        sg = _sigmoid(ln)
        yc = ln * sg
        gc = gc_ref[...]
        _, rc = _rms_fwd(yc, gc)
        d_yc, d_gc = _rms_bwd(dyc_ref[...], yc, rc, gc)
        d_ln = d_yc * (sg * (1.0 + ln * (1.0 - sg)))
        d_xhat = d_ln * lng
        d_conv = rstd * (d_xhat - jnp.mean(d_xhat, axis=-1, keepdims=True)
                         - xhat * jnp.mean(d_xhat * xhat, axis=-1, keepdims=True))
        dconv_ref[...] = d_conv

        rx = rx_ref[...].astype(f32)
        hrx = jnp.where(ti == 0, 0.0, rxh_ref[...].astype(f32))
        xc = _conv_taps(jnp.concatenate([hrx, rx], axis=0), lw_ref[...], LRU_K) + lb_ref[...]
        wa = wa_ref[...]
        wx = wx_ref[...]
        lam = lam_ref[...]
        r, ig, sp, la, a, mult = _lru_gates(xc, wa, ba_ref[...], wx, bx_ref[...], lam)
        hs = hst_ref[...]
        row = lax.broadcasted_iota(jnp.int32, hs.shape, 0)
        h_before = jnp.where(ti == 0, 0.0, hsth_ref[7:8, :])
        h_prev = jnp.where(row == 0, h_before, pltpu.roll(hs, 1, 0))
        rg = rg_ref[...].astype(f32)
        gl, tg = _gelu(rg)
        out = hs * gl
        gmix = gl_ref[...]
        _, rl = _rms_fwd(out, gmix)
        d_out, d_gl = _rms_bwd(dyl_ref[...], out, rl, gmix)
        drg_ref[...] = (d_out * hs * _gelu_grad(rg, tg)).astype(bf16)
        d_h = d_out * gl
        last = row == tm - 1
        a_next = jnp.where(last, 1.0, pltpu.roll(a, tm - 1, 0))
        lmb = _scan_bwd(a_next, d_h, carry_ref[0:1, :])
        carry_ref[...] = jnp.broadcast_to(a[0:1, :] * lmb[0:1, :], carry_ref.shape)
        d_a = lmb * h_prev
        d_mult = lmb * (ig * xc)
        d_ig = lmb * (mult * xc)
        d_la = d_a * a - d_mult * (a * a) / jnp.maximum(mult, 1e-30)
        d_pa = (d_la * (-LRU_C * sp)) * (r * (1.0 - r))
        d_px = d_ig * (ig * (1.0 - ig))
        d_pab = d_pa.astype(bf16)
        d_pxb = d_px.astype(bf16)
        d_xc = lmb * (mult * ig) + _dot_nt(d_pab, wa) + _dot_nt(d_pxb, wx)
        dxc_ref[...] = d_xc
        xcb = xc.astype(bf16)
        dwa_ref[...] += _dot_tn(xcb, d_pab)
        dwx_ref[...] += _dot_tn(xcb, d_pxb)
        d_lam = jnp.sum(d_la * (-LRU_C * r), axis=0, keepdims=True) * (-_sigmoid(-lam))

        def colsum(v):
            return jnp.sum(v, axis=0, keepdims=True)

        rows = [None] * _V_ROWS
        rows[_V_GC] = d_gc
        rows[_V_LNG] = colsum(d_ln * xhat)
        rows[_V_LNB] = colsum(d_ln)
        rows[_V_CB] = colsum(d_conv)
        rows[_V_GL] = d_gl
        rows[_V_BA] = colsum(d_pa)
        rows[_V_BX] = colsum(d_px)
        rows[_V_LAM] = d_lam
        rows[_V_LB] = colsum(d_xc)
        zero = jnp.zeros((1, CONV_W), f32)
        vec_ref[...] += jnp.concatenate([zero if v is None else v for v in rows], axis=0)

    def rev(c):
        return pl.BlockSpec((tm, CONV_W), lambda i: (nt - 1 - i, c))

    small = [p["lng"], p["lnb"], p["lw"], p["lb"], p["wa"], p["ba"], p["wx"], p["bx"], p["lam"], p["gc"], p["gl"]]
    return _pcall(
        body, pieces, name="branch_bwd_a", grid=(nt,),
        in_specs=[rev(0), rev(2), rev(3), rev(5),
                  pl.BlockSpec((HALO, CONV_W), lambda i: (jnp.maximum((nt - 1 - i) * hb - 1, 0), 5)),
                  rev(6), rev(0),
                  pl.BlockSpec((8, LRU_W), lambda i: (jnp.maximum((nt - 1 - i) * h8 - 1, 0), 0))]
                 + [_full(a.shape) for a in small],
        out_specs=[rev(0), rev(0), rev(0), _full((_V_ROWS, CONV_W)), _full((LRU_W, LRU_W)), _full((LRU_W, LRU_W))],
        out_shape=[SDS((s, CONV_W), f32), SDS((s, LRU_W), f32), SDS((s, LRU_W), bf16), SDS((_V_ROWS, CONV_W), f32),
                   SDS((LRU_W, LRU_W), f32), SDS((LRU_W, LRU_W), f32)],
        scratch_shapes=[pltpu.VMEM((8, LRU_W), f32)],
        operands=[conv, dy, dy, z, z, z, hst, hst, *small], sem=("arbitrary",))


def branch_bwd_b(z, d_conv, d_xc, p, pieces=None):
    s = z.shape[0]
    tm = _tile(s)
    nt = s // tm
    hb = tm // HALO

    def body(cv_ref, cg_ref, cvh_ref, cgh_ref, rx_ref, rxh_ref, dc_ref, dch_ref, dx_ref, dxh_ref, cw_ref, lw_ref,
             dzc_ref, dzr_ref, dcw_ref, dlw_ref):
        i = pl.program_id(0)

        @pl.when(i == 0)
        def _():
            dcw_ref[...] = jnp.zeros_like(dcw_ref)
            dlw_ref[...] = jnp.zeros_like(dlw_ref)

        first = i == 0
        last = i == nt - 1
        cval = cv_ref[...].astype(f32)
        sg = _sigmoid(cg_ref[...].astype(f32))
        u = cval * sg
        hu = jnp.where(first, 0.0, cvh_ref[...].astype(f32) * _sigmoid(cgh_ref[...].astype(f32)))
        dpad = jnp.concatenate([dc_ref[...], jnp.where(last, 0.0, dch_ref[...])], axis=0)
        d_u, dw_rows = _conv_taps_bwd(dpad, jnp.concatenate([hu, u], axis=0), cw_ref[...], CONV_K, tm)
        dcw_ref[...] += jnp.concatenate(dw_rows + [jnp.zeros((HALO - CONV_K, CONV_W), f32)], axis=0)
        dzc_ref[...] = jnp.concatenate([d_u * sg, d_u * cval * sg * (1.0 - sg)], axis=1).astype(bf16)

        rx = rx_ref[...].astype(f32)
        hrx = jnp.where(first, 0.0, rxh_ref[...].astype(f32))
        dxpad = jnp.concatenate([dx_ref[...], jnp.where(last, 0.0, dxh_ref[...])], axis=0)
        d_rx, dlw_rows = _conv_taps_bwd(dxpad, jnp.concatenate([hrx, rx], axis=0), lw_ref[...], LRU_K, tm)
        dlw_ref[...] += jnp.concatenate(dlw_rows + [jnp.zeros((8 - LRU_K, LRU_W), f32)], axis=0)
        dzr_ref[...] = d_rx.astype(bf16)

    def col(c):
        return pl.BlockSpec((tm, CONV_W), lambda i: (i, c))

    def prev(c):
        return pl.BlockSpec((HALO, CONV_W), lambda i: (jnp.maximum(i * hb - 1, 0), c))

    nxt = pl.BlockSpec((HALO, CONV_W), lambda i: (jnp.minimum((i + 1) * hb, nt * hb - 1), 0))
    return _pcall(
        body, pieces, name="branch_bwd_b", grid=(nt,),
        in_specs=[col(3), col(4), prev(3), prev(4), col(5), prev(5), col(0), nxt, col(0), nxt,
                  _full(p["cw"].shape), _full(p["lw"].shape)],
        out_specs=[pl.BlockSpec((tm, 2 * CONV_W), lambda i: (i, 0)), pl.BlockSpec((tm, LRU_W), lambda i: (i, 0)),
                   _full((HALO, CONV_W)), _full((8, LRU_W))],
        out_shape=[SDS((s, 2 * CONV_W), bf16), SDS((s, LRU_W), bf16), SDS((HALO, CONV_W), f32), SDS((8, LRU_W), f32)],
        operands=[z, z, z, z, z, z, d_conv, d_conv, d_xc, d_xc, p["cw"], p["lw"]], sem=("arbitrary",))


def inproj_bwd(dq, dkv, dzc, dzr, drg, h, g1, w_in_t, dh1, pieces=None):
    s = h.shape[0]
    tm = _tile(s)

    def body(dq_ref, dkv_ref, dzc_ref, dzr_ref, drg_ref, h_ref, g_ref, w_ref, dh1_ref, dh_ref, dw_ref, dg_ref):
        i = pl.program_id(0)

        @pl.when(i == 0)
        def _():
            dw_ref[...] = jnp.zeros_like(dw_ref)
            dg_ref[...] = jnp.zeros_like(dg_ref)

        dz = jnp.concatenate([dq_ref[...], dkv_ref[...], dzc_ref[...], dzr_ref[...], drg_ref[...]], axis=1)
        x = h_ref[...]
        g = g_ref[...]
        hn, r = _rms_fwd(x, g)
        d_hn = _dot(dz, w_ref[...])
        dw_ref[...] += _dot_tn(dz, hn.astype(bf16))
        dx, dg = _rms_bwd(d_hn, x, r, g)
        dh_ref[...] = dh1_ref[...] + dx
        dg_ref[...] += dg

    def rowb(w):
        return pl.BlockSpec((tm, w), lambda i: (i, 0))

    return _pcall(
        body, pieces, name="inproj_bwd", grid=(s // tm,),
        in_specs=[rowb(ATTN_W), rowb(2 * KV_W), rowb(2 * CONV_W), rowb(LRU_W), rowb(LRU_W), rowb(D_MODEL),
                  _full((1, D_MODEL)), _full((IN_W, D_MODEL)), rowb(D_MODEL)],
        out_specs=[rowb(D_MODEL), _full((IN_W, D_MODEL)), _full((1, D_MODEL))],
        out_shape=[SDS((s, D_MODEL), f32), SDS((IN_W, D_MODEL), f32), SDS((1, D_MODEL), f32)],
        operands=[dq, dkv, dzc, dzr, drg, h, g1, w_in_t, dh1], sem=("arbitrary",), vmem=VMEM_LIMIT)


def _block_diag(w):
    out = jnp.zeros((LRU_W, LRU_W), w.dtype)
    hd = LRU_W // LRU_HEADS
    for hh in range(LRU_HEADS):
        out = out.at[hh * hd:(hh + 1) * hd, hh * hd:(hh + 1) * hd].set(w[hh])
    return out


def _diag_blocks(w):
    hd = LRU_W // LRU_HEADS
    return jnp.stack([w[hh * hd:(hh + 1) * hd, hh * hd:(hh + 1) * hd] for hh in range(LRU_HEADS)])


def _layer_params(sp, l):
    mix = sp["mix_norm"][l]
    return dict(
        g1=sp["norm1"][l][None, :], g2=sp["norm2"][l][None, :],
        sinks=jnp.repeat(sp["attn_sinks"][l], BLOCK)[None, :],
        ga=mix[None, :ATTN_W], gc=mix[None, ATTN_W:ATTN_W + CONV_W], gl=mix[None, ATTN_W + CONV_W:],
        cw=jnp.pad(sp["conv_dw_w"][l], ((0, HALO - CONV_K), (0, 0))), cb=sp["conv_dw_b"][l][None, :],
        lng=sp["conv_ln_g"][l][None, :], lnb=sp["conv_ln_b"][l][None, :],
        lw=jnp.pad(sp["lru_conv_w"][l], ((0, 8 - LRU_K), (0, 0))), lb=sp["lru_conv_b"][l][None, :],
        wa=_block_diag(sp["lru_wa"][l]).astype(bf16), ba=sp["lru_ba"][l].reshape(1, LRU_W),
        wx=_block_diag(sp["lru_wx"][l]).astype(bf16), bx=sp["lru_bx"][l].reshape(1, LRU_W),
        lam=sp["lru_lambda"][l][None, :],
    )


def local_step(x, tgt, big, sp):
    return train_local(x, tgt, sp, LocalWeights(big))


class LocalWeights:
    def __init__(self, big):
        self.big = big
        self.grads = [dict() for _ in range(DEPTH)]

    def weight(self, name, l):
        return self.big[l][name]

    def host(self, point, l):
        return None

    def grad(self, name, l, g):
        self.grads[l][name] = g

    def big_grads(self):
        return self.grads


def train_local(x, tgt, sp, cs):
    lp = [_layer_params(sp, l) for l in range(DEPTH)]
    bias = _attn_bias()
    saved = []
    h = x
    for l in range(DEPTH):
        p = lp[l]
        z = inproj_fwd(h, p["g1"], cs.weight("w_in", l), cs.host("inproj_fwd", l))
        o, ya = attn_fwd(z, p["sinks"], bias, p["ga"], cs.host("attn_fwd", l))
        conv, hst, yc, yl = branch_fwd(z, p, cs.host("branch_fwd", l))
        if l < DEPTH - 1:
            h1, hn2 = outproj_fwd(ya, yc, yl, h, cs.weight("w_out", l), p["g2"], cs.host("outproj_fwd", l))
            up, h_next = mlp_fwd(hn2, h1, cs.weight("w_up", l), cs.weight("w_dn", l), cs.host("mlp_fwd", l))
        else:
            h1, hn2, up, dh, loss, d_gf = last_layer_fwd(
                ya, yc, yl, h, cs.weight("w_out", l), p["g2"], cs.weight("w_up", l), cs.weight("w_dn", l),
                tgt, sp["final_norm"][None, :])
            h_next = None
        saved.append(dict(h=h, z=z, o=o, ya=ya, conv=conv, hst=hst, yc=yc, yl=yl, h1=h1, hn2=hn2, up=up))
        h = h_next
    small_g = [None] * DEPTH
    for l in reversed(range(DEPTH)):
        p, sv = lp[l], saved[l]
        w_up, w_dn = cs.weight("w_up", l), cs.weight("w_dn", l)
        d_up, dh1, d_g2 = mlp_bwd_act(dh, sv["up"], sv["h1"], p["g2"], w_up, w_dn, cs.host("mlp_bwd_act", l))
        dw_up, dw_dn = mlp_bwd_w(sv["hn2"], d_up, sv["up"], dh, cs.host("mlp_bwd_w", l))
        cs.grad("w_up", l, dw_up)
        cs.grad("w_dn", l, dw_dn)
        dy, dw_out = outproj_bwd(dh1, sv["ya"], sv["yc"], sv["yl"], cs.weight("w_out", l))
        cs.grad("w_out", l, dw_out)
        dq, dkv, d_sk, d_ga = attn_bwd(sv["z"], sv["o"], dy, p["sinks"], bias, p["ga"], cs.host("attn_bwd", l))
        d_conv, d_xc, d_rg, vec, dwa, dwx = branch_bwd_a(sv["z"], sv["conv"], sv["hst"], dy, p, cs.host("branch_bwd_a", l))
        dzc, dzr, dcw, dlw = branch_bwd_b(sv["z"], d_conv, d_xc, p, cs.host("branch_bwd_b", l))
        dh, dw_in, d_g1 = inproj_bwd(dq, dkv, dzc, dzr, d_rg, sv["h"], p["g1"], cs.weight("w_in", l), dh1,
                                     cs.host("inproj_bwd", l))
        cs.grad("w_in", l, dw_in)
        hd = LRU_W // LRU_HEADS
        small_g[l] = dict(
            norm1=d_g1[0], attn_sinks=d_sk[:, 0], conv_dw_w=dcw[:CONV_K], conv_dw_b=vec[_V_CB],
            conv_ln_g=vec[_V_LNG], conv_ln_b=vec[_V_LNB], lru_conv_w=dlw[:LRU_K], lru_conv_b=vec[_V_LB],
            lru_wa=_diag_blocks(dwa), lru_ba=vec[_V_BA].reshape(LRU_HEADS, hd),
            lru_wx=_diag_blocks(dwx), lru_bx=vec[_V_BX].reshape(LRU_HEADS, hd), lru_lambda=vec[_V_LAM],
            mix_norm=jnp.concatenate([d_ga[0], vec[_V_GC], vec[_V_GL]]), norm2=d_g2[0],
        )
    return loss, dh, cs.big_grads(), small_g, d_gf[0]


_HBM = pl.BlockSpec(memory_space=pl.ANY)


def _place():
    x, y, c = lax.axis_index("x"), lax.axis_index("y"), lax.axis_index("c")
    chips = [(1 - x, y), (x, 1 - y), (1 - x, 1 - y)]
    return x, y, c, chips


class Comm:
    def __init__(self, ins, out_shape, aliases, sems, start, finish, done):
        self.ins, self.out_shape, self.aliases, self.sems = list(ins), list(out_shape), dict(aliases), list(sems)
        self.start, self.finish, self.done = start, finish, done


def _pcall(body, pieces, *, name, grid, in_specs, out_specs, out_shape, operands, scratch_shapes=(), sem, vmem=None,
           prefetch=None):
    in_specs, out_specs, out_shape, scratch_shapes = list(in_specs), list(out_specs), list(out_shape), list(scratch_shapes)
    lead = 0 if prefetch is None else 1

    def call(fn, call_name, ins, outs, shapes, scratch, aliases, semantics, args):
        params = _params(semantics, vmem)
        if prefetch is None:
            return pl.pallas_call(fn, name=call_name, grid=grid, in_specs=ins, out_specs=outs, out_shape=shapes,
                                  scratch_shapes=scratch, input_output_aliases=aliases, compiler_params=params)(*args)
        spec = pltpu.PrefetchScalarGridSpec(num_scalar_prefetch=1, grid=grid, in_specs=ins, out_specs=outs,
                                            scratch_shapes=scratch)
        return pl.pallas_call(fn, name=call_name, grid_spec=spec, out_shape=shapes, input_output_aliases=aliases,
                              compiler_params=params)(prefetch, *args)

    if not pieces:
        return call(body, name, in_specs, out_specs, out_shape, scratch_shapes, {}, sem, operands)
    n_in, n_out, n_scr = len(in_specs), len(out_specs), len(scratch_shapes)
    c_ins = [a for p in pieces for a in p.ins]
    c_outs = [s for p in pieces for s in p.out_shape]
    c_sems = [n for p in pieces for n in p.sems]
    aliases, spans, ki, ko, ks = {}, [], 0, 0, 0
    for p in pieces:
        spans.append((ki, ko, ks))
        for a, b in p.aliases.items():
            aliases[lead + n_in + ki + a] = n_out + ko + b
        ki, ko, ks = ki + len(p.ins), ko + len(p.out_shape), ks + len(p.sems)

    def hosted(*refs):
        pre, refs = refs[:lead], refs[lead:]
        ins, cin = refs[:n_in], refs[n_in:n_in + ki]
        outs, cout = refs[n_in + ki:n_in + ki + n_out], refs[n_in + ki + n_out:n_in + ki + n_out + ko]
        scr, csem = refs[n_in + ki + n_out + ko:n_in + ki + n_out + ko + n_scr], refs[n_in + ki + n_out + ko + n_scr:]
        first = functools.reduce(jnp.logical_and, [pl.program_id(d) == 0 for d in range(len(grid))])
        last = functools.reduce(jnp.logical_and, [pl.program_id(d) == grid[d] - 1 for d in range(len(grid))])

        def each(which):
            for p, (a, b, s) in zip(pieces, spans):
                getattr(p, which)(cin[a:a + len(p.ins)], cout[b:b + len(p.out_shape)], csem[s:s + len(p.sems)])

        @pl.when(first)
        def _():
            each("start")

        body(*pre, *ins, *outs, *scr)

        @pl.when(last)
        def _():
            each("finish")

    res = call(hosted, name + "_host", in_specs + [_HBM] * ki, out_specs + [_HBM] * ko, out_shape + c_outs,
               scratch_shapes + [pltpu.SemaphoreType.DMA((n,)) for n in c_sems], aliases, ("arbitrary",) * len(grid),
               [*operands, *c_ins])
    for p, (a, b, s) in zip(pieces, spans):
        p.done(res[n_out + b:n_out + b + len(p.out_shape)])
    return res[:n_out]


def standalone(pieces, name):
    ki = sum(len(p.ins) for p in pieces)
    ko = sum(len(p.out_shape) for p in pieces)
    spans, a, b, s = [], 0, 0, 0
    aliases = {}
    for p in pieces:
        spans.append((a, b, s))
        for i, o in p.aliases.items():
            aliases[a + i] = b + o
        a, b, s = a + len(p.ins), b + len(p.out_shape), s + len(p.sems)

    def body(*refs):
        cin, cout, csem = refs[:ki], refs[ki:ki + ko], refs[ki + ko:]
        for which in ("start", "finish"):
            for p, (a, b, s) in zip(pieces, spans):
                getattr(p, which)(cin[a:a + len(p.ins)], cout[b:b + len(p.out_shape)], csem[s:s + len(p.sems)])

    res = pl.pallas_call(
        body, name=name, in_specs=[_HBM] * ki, out_specs=[_HBM] * ko, out_shape=[s for p in pieces for s in p.out_shape],
        scratch_shapes=[pltpu.SemaphoreType.DMA((n,)) for p in pieces for n in p.sems], input_output_aliases=aliases,
    )(*[a for p in pieces for a in p.ins])
    for p, (a, b, s) in zip(pieces, spans):
        p.done(res[b:b + len(p.out_shape)])


def _rows_half(ref, which, rows):
    return ref.at[pl.ds(pl.multiple_of(which * rows, 8), rows)]


def gather_ici_piece(bufs, done):
    n = len(bufs)

    def copies(cout):
        x, y, c, chips = _place()
        out = []
        for j, (cx, cy) in enumerate(chips):
            for w in range(n):
                half = bufs[w].shape[1] // 2
                out.append((j, w, _rows_half(cout[w].at[2 * x + y], c, half), _rows_half(cout[w].at[2 * cx + cy], c, half),
                            (cx, cy, c)))
        return out

    def start(cin, cout, sems):
        for j, w, mine, _, to in copies(cout):
            pltpu.make_async_remote_copy(src_ref=mine, dst_ref=mine, send_sem=sems[0].at[n * j + w],
                                         recv_sem=sems[1].at[n * j + w], device_id=to, device_id_type=MESH).start()

    def finish(cin, cout, sems):
        for j, w, mine, landed, to in copies(cout):
            pltpu.make_async_remote_copy(src_ref=mine, dst_ref=landed, send_sem=sems[0].at[n * j + w],
                                         recv_sem=sems[1].at[n * j + w], device_id=to, device_id_type=MESH).wait()

    return Comm(bufs, [SDS(b.shape, b.dtype) for b in bufs], {w: w for w in range(n)}, [3 * n, 3 * n], start, finish, done)


def gather_full_piece(bufs, done):
    n = len(bufs)

    def copies(cout):
        x, y, c, chips = _place()
        return [(n * j + w, cout[w].at[2 * x + y], cout[w].at[2 * cx + cy], (cx, cy, c))
                for j, (cx, cy) in enumerate(chips) for w in range(n)]

    def start(cin, cout, sems):
        for k, mine, _, to in copies(cout):
            pltpu.make_async_remote_copy(src_ref=mine, dst_ref=mine, send_sem=sems[0].at[k], recv_sem=sems[1].at[k],
                                         device_id=to, device_id_type=MESH).start()

    def finish(cin, cout, sems):
        for k, mine, landed, to in copies(cout):
            pltpu.make_async_remote_copy(src_ref=mine, dst_ref=landed, send_sem=sems[0].at[k], recv_sem=sems[1].at[k],
                                         device_id=to, device_id_type=MESH).wait()

    return Comm(bufs, [SDS(b.shape, b.dtype) for b in bufs], {w: w for w in range(n)}, [3 * n, 3 * n], start, finish, done)


def gather_d2d_piece(bufs, done):
    n = len(bufs)

    def copies(cout):
        x, y, c, chips = _place()
        out = []
        for j, (cx, cy) in enumerate(chips):
            for w in range(n):
                half = bufs[w].shape[1] // 2
                slot = cout[w].at[2 * cx + cy]
                out.append((n * j + w, _rows_half(slot, c, half), _rows_half(slot, 1 - c, half), (x, y, 1 - c)))
        return out

    def start(cin, cout, sems):
        for k, mine, _, to in copies(cout):
            pltpu.make_async_remote_copy(src_ref=mine, dst_ref=mine, send_sem=sems[0].at[k], recv_sem=sems[1].at[k],
                                         device_id=to, device_id_type=MESH).start()

    def finish(cin, cout, sems):
        for k, mine, theirs, to in copies(cout):
            pltpu.make_async_remote_copy(src_ref=mine, dst_ref=theirs, send_sem=sems[0].at[k], recv_sem=sems[1].at[k],
                                         device_id=to, device_id_type=MESH).wait()

    return Comm(bufs, [SDS(b.shape, b.dtype) for b in bufs], {w: w for w in range(n)}, [3 * n, 3 * n], start, finish, done)


def pair_piece(parts, done):
    n = len(parts)

    def copies(cin, cout):
        x, y, c, _ = _place()
        out = []
        for w in range(n):
            half = parts[w].shape[1] // 2
            out.append((w, cin[w].at[:, pl.ds(pl.multiple_of((1 - c) * half, 8), half), :], cout[w], (x, y, 1 - c)))
        return out

    def start(cin, cout, sems):
        for w, src, dst, to in copies(cin, cout):
            pltpu.make_async_remote_copy(src_ref=src, dst_ref=dst, send_sem=sems[0].at[w], recv_sem=sems[1].at[w],
                                         device_id=to, device_id_type=MESH).start()

    def finish(cin, cout, sems):
        for w, src, dst, to in copies(cin, cout):
            pltpu.make_async_remote_copy(src_ref=src, dst_ref=dst, send_sem=sems[0].at[w], recv_sem=sems[1].at[w],
                                         device_id=to, device_id_type=MESH).wait()

    return Comm(parts, [SDS((N_SHARD, a.shape[1] // 2, a.shape[2]), f32) for a in parts], {}, [n, n], start, finish, done)


def shard_piece(sums16, done):
    n = len(sums16)

    def copies(cin, cout):
        x, y, c, chips = _place()
        return [(n * j + w, cin[w].at[2 * cx + cy], cout[w].at[j], (cx, cy, c))
                for j, (cx, cy) in enumerate(chips) for w in range(n)]

    def start(cin, cout, sems):
        for k, src, dst, to in copies(cin, cout):
            pltpu.make_async_remote_copy(src_ref=src, dst_ref=dst, send_sem=sems[0].at[k], recv_sem=sems[1].at[k],
                                         device_id=to, device_id_type=MESH).start()

    def finish(cin, cout, sems):
        for k, src, dst, to in copies(cin, cout):
            pltpu.make_async_remote_copy(src_ref=src, dst_ref=dst, send_sem=sems[0].at[k], recv_sem=sems[1].at[k],
                                         device_id=to, device_id_type=MESH).wait()

    return Comm(sums16, [SDS((3,) + a.shape[1:], bf16) for a in sums16], {}, [3 * n, 3 * n], start, finish, done)


def place_shard(a, layer, idx, dtype):
    _, r, cdim = a.shape
    tr = min(r, 512)

    def body(idx_ref, a_ref, o_ref):
        o_ref[0] = a_ref[0].astype(dtype)

    return pl.pallas_call(
        body, name="place_shard",
        grid_spec=pltpu.PrefetchScalarGridSpec(
            num_scalar_prefetch=1, grid=(r // tr,),
            in_specs=[pl.BlockSpec((1, tr, cdim), lambda i, idx_ref: (layer, i, 0))],
            out_specs=pl.BlockSpec((1, tr, cdim), lambda i, idx_ref: (idx_ref[1], i, 0))),
        out_shape=SDS((N_SHARD, r, cdim), dtype),
        compiler_params=_params(("arbitrary",)),
    )(idx, a)


def place_shards(items, idx, pieces):
    steps = 4
    n = len(items)

    def body(idx_ref, *refs):
        for k in range(n):
            refs[n + k][0] = refs[k][0].astype(bf16)

    in_specs, out_specs, out_shape = [], [], []
    for a, layer in items:
        _, r, cdim = a.shape
        in_specs.append(pl.BlockSpec((1, r // steps, cdim), lambda i, idx_ref, layer=layer: (layer, i, 0)))
        out_specs.append(pl.BlockSpec((1, r // steps, cdim), lambda i, idx_ref: (idx_ref[1], i, 0)))
        out_shape.append(SDS((N_SHARD, r, cdim), bf16))
    return _pcall(body, pieces, name="place_shards", grid=(steps,), in_specs=in_specs, out_specs=out_specs,
                  out_shape=out_shape, operands=[a for a, _ in items], sem=("arbitrary",), prefetch=idx)


_KEYS = ("w_in", "w_out", "w_up", "w_dn")


class MeshWeights:
    def __init__(self, w_in, w_out, w_up, w_down, conv_dw_w, lru_conv_w, idx):
        self.idx = idx
        src = dict(w_in=w_in, w_out=w_out, w_up=w_up, w_dn=w_down)
        self.conv = {(n, l): place_shard(a, l, idx, f32)
                     for n, a in (("cw", conv_dw_w), ("lw", lru_conv_w)) for l in range(DEPTH)}
        self.cache, self.parts, self.sum32, self.sum16, self.got = {}, {}, {}, {}, {}
        first, small = [("w_in", 0)], list(self.conv)
        rest = [(n, l) for n in _KEYS for l in range(DEPTH) if (n, l) not in first]

        def store_conv(outs):
            self.conv.update(zip(small, outs))

        self.buf = {k: place_shard(src[k[0]], k[1], idx, bf16) for k in first}
        placed = place_shards(
            [(src[n], l) for n, l in rest], idx,
            [self._gather(gather_ici_piece, first), gather_full_piece([self.conv[k] for k in small], store_conv)])
        self.buf.update(zip(rest, placed))
        standalone([self._gather(gather_d2d_piece, first)], "gather_first_d2d")

    def _gather(self, piece, keys):
        def done(outs):
            self.buf.update(zip(keys, outs))
        return piece([self.buf[k] for k in keys], done)

    def _pair(self, keys):
        def done(outs):
            for k, recv in zip(keys, outs):
                self.sum32[k], self.sum16[k] = chip_sum(self.parts[k], recv, self.idx)
        return pair_piece([self.parts[k] for k in keys], done)

    def _shard(self, keys):
        def done(outs):
            self.got.update(zip(keys, outs))
        return shard_piece([self.sum16[k] for k in keys], done)

    def conv_weights(self):
        out = []
        for n in ("cw", "lw"):
            a = jnp.stack([self.conv[(n, l)] for l in range(DEPTH)])
            out.append(a.transpose(0, 2, 1, 3).reshape(DEPTH, a.shape[2], N_SHARD * a.shape[3]))
        return out

    def weight(self, name, l):
        if (name, l) not in self.cache:
            b = self.buf[(name, l)]
            if name in ("w_in", "w_out"):
                b = b.reshape(-1, D_MODEL)
            self.cache[(name, l)] = b
        return self.cache[(name, l)]

    def host(self, point, l):
        ici, d2d = gather_ici_piece, gather_d2d_piece
        rest1 = [("w_out", 1), ("w_up", 1), ("w_dn", 1)]
        plan = {
            ("inproj_fwd", 0): lambda: [self._gather(ici, [("w_out", 0)])],
            ("attn_fwd", 0): lambda: [self._gather(ici, [("w_up", 0)]), self._gather(d2d, [("w_out", 0)])],
            ("branch_fwd", 0): lambda: [self._gather(ici, [("w_dn", 0)]), self._gather(d2d, [("w_up", 0)])],
            ("outproj_fwd", 0): lambda: [self._gather(d2d, [("w_dn", 0)]), self._gather(ici, [("w_in", 1)])],
            ("mlp_fwd", 0): lambda: [self._gather(ici, rest1), self._gather(d2d, [("w_in", 1)])],
            ("attn_fwd", 1): lambda: [self._gather(d2d, rest1)],
            ("attn_bwd", 1): lambda: [self._pair([("w_up", 1), ("w_dn", 1)])],
            ("mlp_bwd_act", 0): lambda: [self._shard([("w_up", 1), ("w_dn", 1)])],
            ("mlp_bwd_w", 0): lambda: [self._pair([("w_in", 1), ("w_out", 1)])],
            ("attn_bwd", 0): lambda: [self._shard([("w_in", 1), ("w_out", 1)]),
                                      self._pair([("w_up", 0), ("w_dn", 0), ("w_out", 0)])],
            ("branch_bwd_a", 0): lambda: [self._shard([("w_up", 0)])],
            ("branch_bwd_b", 0): lambda: [self._shard([("w_dn", 0), ("w_out", 0)])],
        }
        make = plan.get((point, l))
        return make() if make else None

    def grad(self, name, l, g):
        if name in ("w_in", "w_out"):
            g = g.reshape(N_SHARD, -1, D_MODEL)
        self.parts[(name, l)] = g

    def big_grads(self):
        return None

    def finish(self, small_vec):
        last = [("w_in", 0)]
        small = {}

        def keep(outs):
            small["buf"] = outs[0]

        standalone([self._pair(last), small_first_piece(small_vec, keep)], "pair_last")
        standalone([self._shard(last), small_second_piece(small["buf"], keep)], "shard_last")
        return self._totals(), small["buf"]

    def _totals(self):
        tots = []
        for n in _KEYS:
            t = None
            for l in reversed(range(DEPTH)):
                t = shard_sum(self.sum32[(n, l)], self.got[(n, l)], self.idx, l, t)
            tots.append(t)
        return tots


def chip_sum(g, recv, idx):
    _, r, cdim = g.shape
    half = r // 2
    tr = min(half, 512)
    nh = half // tr

    def body(idx_ref, g_ref, r_ref, o32_ref, o16_ref):
        tot = g_ref[0] + r_ref[0]
        o16_ref[0] = tot.astype(bf16)

        @pl.when(pl.program_id(0) == N_SHARD - 1)
        def _():
            o32_ref[...] = tot

    def slab(k, idx_ref):
        return lax.rem(idx_ref[1] + 1 + k, N_SHARD)

    def o16_map(k, i, idx_ref):
        return (slab(k, idx_ref), i, 0)

    return pl.pallas_call(
        body, name="chip_sum",
        grid_spec=pltpu.PrefetchScalarGridSpec(
            num_scalar_prefetch=1, grid=(N_SHARD, nh),
            in_specs=[pl.BlockSpec((1, tr, cdim), lambda k, i, idx_ref: (slab(k, idx_ref), idx_ref[0] * nh + i, 0)),
                      pl.BlockSpec((1, tr, cdim), lambda k, i, idx_ref: (slab(k, idx_ref), i, 0))],
            out_specs=[pl.BlockSpec((tr, cdim), lambda k, i, idx_ref: (jnp.where(k == N_SHARD - 1, i, 0), 0)),
                       pl.BlockSpec((1, tr, cdim), o16_map)]),
        out_shape=[SDS((half, cdim), f32), SDS((N_SHARD, half, cdim), bf16)],
        compiler_params=_params(("arbitrary", "arbitrary")),
    )(idx, g, recv)


def shard_sum(sum32, got16, idx, layer, prev):
    half, cdim = sum32.shape
    tr = min(half, 512)

    def body(idx_ref, a_ref, r0_ref, r1_ref, r2_ref, *rest):
        o_ref = rest[-1]
        o_ref[0, 0] = ((a_ref[...] + r0_ref[0].astype(f32)) + r1_ref[0].astype(f32)) + r2_ref[0].astype(f32)

    def rel(j):
        return pl.BlockSpec((1, tr, cdim), lambda i, idx_ref: (j, i, 0))

    in_specs = [pl.BlockSpec((tr, cdim), lambda i, idx_ref: (i, 0)), rel(0), rel(1), rel(2)]
    operands = [idx, sum32, got16, got16, got16]
    aliases = {}
    if prev is not None:
        in_specs.append(_HBM)
        operands.append(prev)
        aliases = {5: 0}
    return pl.pallas_call(
        body, name="shard_sum",
        grid_spec=pltpu.PrefetchScalarGridSpec(
            num_scalar_prefetch=1, grid=(half // tr,), in_specs=in_specs,
            out_specs=pl.BlockSpec((1, 1, tr, cdim), lambda i, idx_ref: (layer, idx_ref[0], i, 0))),
        out_shape=SDS((DEPTH, 2, half, cdim), f32), input_output_aliases=aliases,
        compiler_params=_params(("arbitrary",)),
    )(*operands)


def halves_exchange(tots):
    nw = len(tots)

    def body(*refs):
        bufs = refs[nw:2 * nw]
        send_sem, recv_sem = refs[2 * nw:]
        x, y, c, _ = _place()

        def copy(w, l, half_idx):
            return pltpu.make_async_remote_copy(
                src_ref=bufs[w].at[l, half_idx], dst_ref=bufs[w].at[l, half_idx], send_sem=send_sem.at[DEPTH * w + l],
                recv_sem=recv_sem.at[DEPTH * w + l], device_id=(x, y, 1 - c), device_id_type=MESH)

        sends = [copy(w, l, c) for w in range(nw) for l in range(DEPTH)]
        for cp in sends:
            cp.start()
        for w in range(nw):
            for l in range(DEPTH):
                copy(w, l, 1 - c).wait_recv()
        for cp in sends:
            cp.wait_send()

    return pl.pallas_call(
        body, name="halves_exchange", in_specs=[_HBM] * nw, out_specs=[_HBM] * nw,
        out_shape=[SDS(a.shape, f32) for a in tots], input_output_aliases={w: w for w in range(nw)},
        scratch_shapes=[pltpu.SemaphoreType.DMA((DEPTH * nw,)), pltpu.SemaphoreType.DMA((DEPTH * nw,))],
    )(*tots)


N_DEV = 8


def _dev_index(px, py, pc):
    return 4 * px + 2 * py + pc


def small_first_piece(vec, done):
    def copies(cin, cout):
        x, y, c, chips = _place()
        mine = cout[0].at[_dev_index(x, y, c)]
        peers = [(x, y, 1 - c)] + [(cx, cy, c) for cx, cy in chips]
        return mine, [(k, cout[0].at[_dev_index(*p)], p) for k, p in enumerate(peers)]

    def start(cin, cout, sems):
        mine, peers = copies(cin, cout)
        pltpu.make_async_copy(cin[0], mine, sems[2].at[0]).start()
        for k, _, to in peers:
            pltpu.make_async_remote_copy(src_ref=cin[0], dst_ref=mine, send_sem=sems[0].at[k], recv_sem=sems[1].at[k],
                                         device_id=to, device_id_type=MESH).start()

    def finish(cin, cout, sems):
        mine, peers = copies(cin, cout)
        for k, theirs, to in peers:
            pltpu.make_async_remote_copy(src_ref=cin[0], dst_ref=theirs, send_sem=sems[0].at[k], recv_sem=sems[1].at[k],
                                         device_id=to, device_id_type=MESH).wait()
        pltpu.make_async_copy(cin[0], mine, sems[2].at[0]).wait()

    return Comm([vec], [SDS((N_DEV,) + vec.shape, f32)], {}, [4, 4, 1], start, finish, done)


def small_second_piece(buf, done):
    def copies(cout):
        x, y, c, chips = _place()
        return [(j, cout[0].at[_dev_index(cx, cy, c)], cout[0].at[_dev_index(cx, cy, 1 - c)], (x, y, 1 - c))
                for j, (cx, cy) in enumerate(chips)]

    def start(cin, cout, sems):
        for j, mine, _, to in copies(cout):
            pltpu.make_async_remote_copy(src_ref=mine, dst_ref=mine, send_sem=sems[0].at[j], recv_sem=sems[1].at[j],
                                         device_id=to, device_id_type=MESH).start()

    def finish(cin, cout, sems):
        for j, mine, theirs, to in copies(cout):
            pltpu.make_async_remote_copy(src_ref=mine, dst_ref=theirs, send_sem=sems[0].at[j], recv_sem=sems[1].at[j],
                                         device_id=to, device_id_type=MESH).wait()

    return Comm([buf], [SDS(buf.shape, f32)], {0: 0}, [3, 3], start, finish, done)


def small_sum(buf):
    def body(b_ref, o_ref):
        acc = b_ref[0]
        for d in range(1, N_DEV):
            acc = acc + b_ref[d]
        o_ref[...] = acc

    vm = pl.BlockSpec(memory_space=pltpu.VMEM)
    return pl.pallas_call(body, name="small_sum", in_specs=[vm], out_specs=vm, out_shape=SDS(buf.shape[1:], f32))(buf)


def _adamw_math(w, g, m, v):
    m = ADAM_B1 * m + (1.0 - ADAM_B1) * g
    v = ADAM_B2 * v + (1.0 - ADAM_B2) * (g * g)
    m_hat = m / (1.0 - ADAM_B1 ** ADAM_STEP)
    v_hat = v / (1.0 - ADAM_B2 ** ADAM_STEP)
    delta = -ADAM_LR * (m_hat / (jnp.sqrt(v_hat) + ADAM_EPS) + ADAM_WD * w)
    return delta, m, v


def adamw_big(w, g, m, v):
    _, r, cdim = w.shape
    tr = _row_tile(r, 512)

    def body(w_ref, g_ref, m_ref, v_ref, go_ref, d_ref, mo_ref, vo_ref):
        g = g_ref[...]
        d, mm, vv = _adamw_math(w_ref[...], g, m_ref[...], v_ref[...])
        go_ref[...] = g
        d_ref[...] = d
        mo_ref[...] = mm
        vo_ref[...] = vv

    blk = pl.BlockSpec((1, tr, cdim), lambda l, i: (l, i, 0))
    return pl.pallas_call(
        body, name="adamw_big", grid=(DEPTH, r // tr), in_specs=[blk] * 4, out_specs=[blk] * 4,
        out_shape=[SDS(w.shape, f32)] * 4, compiler_params=_params(("parallel", "parallel"), VMEM_LIMIT),
    )(w, g, m, v)


def adamw_small(ws, gs, ms, vs):
    n = len(ws)

    def body(*refs):
        w_r, g_r, m_r, v_r = refs[:n], refs[n:2 * n], refs[2 * n:3 * n], refs[3 * n:4 * n]
        d_o, m_o, v_o = refs[4 * n:5 * n], refs[5 * n:6 * n], refs[6 * n:7 * n]
        for k in range(n):
            d, mm, vv = _adamw_math(w_r[k][...], g_r[k][...], m_r[k][...], v_r[k][...])
            d_o[k][...] = d
            m_o[k][...] = mm
            v_o[k][...] = vv

    vm = pl.BlockSpec(memory_space=pltpu.VMEM)
    shapes = [SDS(a.shape, f32) for a in ws]
    outs = pl.pallas_call(
        body, name="adamw_small", in_specs=[vm] * (4 * n), out_specs=[vm] * (3 * n), out_shape=shapes * 3,
    )(*ws, *gs, *ms, *vs)
    return outs[:n], outs[n:2 * n], outs[2 * n:]


_BIG = ("w_in", "w_out", "w_up", "w_down")
_WEIGHTS = ("norm1", "w_in", "attn_sinks", "conv_dw_w", "conv_dw_b", "conv_ln_g", "conv_ln_b", "lru_conv_w", "lru_conv_b",
            "lru_wa", "lru_ba", "lru_wx", "lru_bx", "lru_lambda", "mix_norm", "w_out", "norm2", "w_up", "w_down", "final_norm")
_SMALL = tuple(n for n in _WEIGHTS if n not in _BIG)
_SMALL_FULL_SHAPE = dict(
    norm1=(DEPTH, D_MODEL), attn_sinks=(DEPTH, N_HEADS), conv_dw_w=(DEPTH, CONV_K, CONV_W), conv_dw_b=(DEPTH, CONV_W),
    conv_ln_g=(DEPTH, CONV_W), conv_ln_b=(DEPTH, CONV_W), lru_conv_w=(DEPTH, LRU_K, LRU_W), lru_conv_b=(DEPTH, LRU_W),
    lru_wa=(DEPTH, LRU_HEADS, 64, 64), lru_ba=(DEPTH, LRU_HEADS, 64), lru_wx=(DEPTH, LRU_HEADS, 64, 64),
    lru_bx=(DEPTH, LRU_HEADS, 64), lru_lambda=(DEPTH, LRU_W), mix_norm=(DEPTH, D_MODEL), norm2=(DEPTH, D_MODEL),
    final_norm=(D_MODEL,))
_CHANNEL_SHARDED = ("conv_dw_w", "lru_conv_w")


def _pad_lanes(n):
    return -(-n // LANES) * LANES


def _pack(named):
    flat = []
    for a in named:
        a = a.reshape(-1)
        flat.append(jnp.pad(a, (0, _pad_lanes(a.shape[0]) - a.shape[0])))
    v = jnp.concatenate(flat)
    rows = -(-v.shape[0] // (8 * LANES)) * 8
    return jnp.pad(v, (0, rows * LANES - v.shape[0])).reshape(rows, LANES)


def _unpack(vec, shapes):
    flat = vec.reshape(-1)
    out, off = [], 0
    for shp in shapes:
        n = math.prod(shp)
        out.append(flat[off:off + n].reshape(shp))
        off += _pad_lanes(n)
    return out


def _as2d(a):
    return a.reshape(-1, a.shape[-1]) if a.ndim > 1 else a.reshape(1, -1)


def kernel(x, norm1, w_in, attn_sinks, conv_dw_w, conv_dw_b, conv_ln_g, conv_ln_b, lru_conv_w, lru_conv_b, lru_wa, lru_ba, lru_wx, lru_bx, lru_lambda, mix_norm, w_out, norm2, w_up, w_down, final_norm, loss_target, m_norm1, m_w_in, m_attn_sinks, m_conv_dw_w, m_conv_dw_b, m_conv_ln_g, m_conv_ln_b, m_lru_conv_w, m_lru_conv_b, m_lru_wa, m_lru_ba, m_lru_wx, m_lru_bx, m_lru_lambda, m_mix_norm, m_w_out, m_norm2, m_w_up, m_w_down, m_final_norm, v_norm1, v_w_in, v_attn_sinks, v_conv_dw_w, v_conv_dw_b, v_conv_ln_g, v_conv_ln_b, v_lru_conv_w, v_lru_conv_b, v_lru_wa, v_lru_ba, v_lru_wx, v_lru_bx, v_lru_lambda, v_mix_norm, v_w_out, v_norm2, v_w_up, v_w_down, v_final_norm):
    wts = dict(norm1=norm1, w_in=w_in, attn_sinks=attn_sinks, conv_dw_w=conv_dw_w, conv_dw_b=conv_dw_b, conv_ln_g=conv_ln_g,
               conv_ln_b=conv_ln_b, lru_conv_w=lru_conv_w, lru_conv_b=lru_conv_b, lru_wa=lru_wa, lru_ba=lru_ba, lru_wx=lru_wx,
               lru_bx=lru_bx, lru_lambda=lru_lambda, mix_norm=mix_norm, w_out=w_out, norm2=norm2, w_up=w_up, w_down=w_down,
               final_norm=final_norm)
    mom = dict(norm1=m_norm1, w_in=m_w_in, attn_sinks=m_attn_sinks, conv_dw_w=m_conv_dw_w, conv_dw_b=m_conv_dw_b,
               conv_ln_g=m_conv_ln_g, conv_ln_b=m_conv_ln_b, lru_conv_w=m_lru_conv_w, lru_conv_b=m_lru_conv_b, lru_wa=m_lru_wa,
               lru_ba=m_lru_ba, lru_wx=m_lru_wx, lru_bx=m_lru_bx, lru_lambda=m_lru_lambda, mix_norm=m_mix_norm, w_out=m_w_out,
               norm2=m_norm2, w_up=m_w_up, w_down=m_w_down, final_norm=m_final_norm)
    var = dict(norm1=v_norm1, w_in=v_w_in, attn_sinks=v_attn_sinks, conv_dw_w=v_conv_dw_w, conv_dw_b=v_conv_dw_b,
               conv_ln_g=v_conv_ln_g, conv_ln_b=v_conv_ln_b, lru_conv_w=v_lru_conv_w, lru_conv_b=v_lru_conv_b, lru_wa=v_lru_wa,
               lru_ba=v_lru_ba, lru_wx=v_lru_wx, lru_bx=v_lru_bx, lru_lambda=v_lru_lambda, mix_norm=v_mix_norm, w_out=v_w_out,
               norm2=v_norm2, w_up=v_w_up, w_down=v_w_down, final_norm=v_final_norm)

    c_idx = lax.axis_index("c").astype(jnp.int32)
    s_idx = (2 * lax.axis_index("x") + lax.axis_index("y")).astype(jnp.int32)
    idx = jnp.stack([c_idx, s_idx])

    for d in (wts, mom, var):
        d["w_in"] = d["w_in"].transpose(0, 2, 1)

    cs = MeshWeights(wts["w_in"], w_out, w_up, w_down, conv_dw_w, lru_conv_w, idx)
    sp = {n: wts[n] for n in _SMALL}
    sp["conv_dw_w"], sp["lru_conv_w"] = cs.conv_weights()
    loss_blk, grad_x, _, small_g, d_gf = train_local(x[0], loss_target[0], sp, cs)

    stacked = [jnp.stack([small_g[l][n] for l in range(DEPTH)]) for n in _SMALL if n != "final_norm"]
    tots, gathered = cs.finish(_pack(stacked + [d_gf, loss_blk[0, 0:1]]))
    grads_big = {n: a.reshape(wts[n].shape) for n, a in zip(_BIG, halves_exchange(tots))}
    summed = small_sum(gathered)
    names = [n for n in _SMALL if n != "final_norm"] + ["final_norm"]
    unpacked = _unpack(summed, [_SMALL_FULL_SHAPE[n] for n in names] + [(1,)])
    loss = unpacked[-1][0]
    grads = dict(zip(names, unpacked[:-1]))
    for n in _CHANNEL_SHARDED:
        width = wts[n].shape[-1]
        grads[n] = lax.dynamic_slice_in_dim(grads[n], s_idx * width, width, axis=2)
    grads.update(grads_big)

    delta, new_m, new_v = {}, {}, {}
    for n in _BIG:
        grads[n], delta[n], new_m[n], new_v[n] = adamw_big(wts[n], grads[n], mom[n], var[n])
    for d in (grads, delta, new_m, new_v):
        d["w_in"] = d["w_in"].transpose(0, 2, 1)
    sm = list(_SMALL)
    d_s, m_s, v_s = adamw_small([_as2d(wts[n]) for n in sm], [_as2d(grads[n]) for n in sm],
                                [_as2d(mom[n]) for n in sm], [_as2d(var[n]) for n in sm])
    for k, n in enumerate(sm):
        delta[n], new_m[n], new_v[n] = (a.reshape(wts[n].shape) for a in (d_s[k], m_s[k], v_s[k]))

    return (loss, grad_x[None], *[grads[n] for n in _WEIGHTS], *[delta[n] for n in _WEIGHTS],
            *[new_m[n] for n in _WEIGHTS], *[new_v[n] for n in _WEIGHTS])
```

```python
import functools
import math

import jax
import jax.numpy as jnp
from jax import lax
from jax.experimental import pallas as pl
from jax.experimental.pallas import tpu as pltpu

f32 = jnp.float32
bf16 = jnp.bfloat16
SDS = jax.ShapeDtypeStruct

D_MODEL = 1024
DEPTH = 2
ATTN_W = 512
KV_W = 128
HEAD_DIM = 64
N_HEADS = 8
BLOCK = 128
CONV_W = 256
CONV_K = 31
LRU_W = 256
LRU_K = 4
LRU_HEADS = 4
LRU_C = 8.0
IN_W = 1792
D_FF = 4096
N_SHARD = 4
FF_CHUNK = D_FF // N_SHARD
RMS_EPS = 1e-6
LN_EPS = 1e-5
MASK_VALUE = -1e30
HALO = 32
LANES = 128
VMEM_LIMIT = 56 * 1024 * 1024

ADAM_LR = 0.001
ADAM_B1 = 0.9
ADAM_B2 = 0.999
ADAM_EPS = 1e-08
ADAM_WD = 0.01
ADAM_STEP = 10

MESH = pl.DeviceIdType.MESH


def _dot(a, b):
    return jnp.dot(a, b, preferred_element_type=f32)


def _dot_nt(a, b):
    return lax.dot_general(a, b, (((1,), (1,)), ((), ())), preferred_element_type=f32)


def _dot_tn(a, b):
    return lax.dot_general(a, b, (((0,), (0,)), ((), ())), preferred_element_type=f32)


def _rms_fwd(x, g):
    r = lax.rsqrt(jnp.mean(x * x, axis=-1, keepdims=True) + RMS_EPS)
    return x * r * g, r


def _rms_bwd(dy, x, r, g):
    t = dy * g
    dx = r * t - x * (r * r * r) * jnp.mean(t * x, axis=-1, keepdims=True)
    dg = jnp.sum(dy * x * r, axis=0, keepdims=True)
    return dx, dg


def _sigmoid(x):
    return jax.nn.sigmoid(x)


_GELU_K = math.sqrt(2.0 / math.pi)


def _gelu(x):
    t = jnp.tanh(_GELU_K * (x + 0.044715 * x * x * x))
    return 0.5 * x * (1.0 + t), t


def _gelu_grad(x, t):
    return 0.5 * (1.0 + t) + 0.5 * x * (1.0 - t * t) * _GELU_K * (1.0 + 3.0 * 0.044715 * x * x)


def _log1p(x):
    return jnp.where(x < 1e-4, x - 0.5 * x * x, jnp.log(1.0 + x))


def _softplus(x):
    return jnp.maximum(x, 0.0) + _log1p(jnp.exp(-jnp.abs(x)))


def _neg_expm1(x):
    series = -x * (1.0 + 0.5 * x * (1.0 + x * (1.0 / 3.0) * (1.0 + 0.25 * x)))
    return jnp.where(x > -0.01, series, 1.0 - jnp.exp(x))


def _sublane_rolls(x, count, forward):
    n = x.shape[0]
    return [x if b == 0 else pltpu.roll(x, b if forward else n - b, 0) for b in range(count)]


def _conv_taps(xpad, w, k_width):
    t_rows = xpad.shape[0] - HALO
    rolled = _sublane_rolls(xpad, min(k_width, 8), forward=True)
    acc = None
    for k in range(k_width):
        hi, lo = divmod((k_width - 1) - k, 8)
        term = rolled[lo][HALO - 8 * hi:HALO - 8 * hi + t_rows] * w[k:k + 1, :]
        acc = term if acc is None else acc + term
    return acc


def _conv_taps_bwd(dpad, upad, w, k_width, t_rows):
    n_lo = min(k_width, 8)
    d_rolled = _sublane_rolls(dpad, n_lo, forward=False)
    u_rolled = _sublane_rolls(upad, n_lo, forward=True)
    d_in = None
    dw_rows = []
    d_out = dpad[:t_rows]
    for k in range(k_width):
        hi, lo = divmod((k_width - 1) - k, 8)
        term = d_rolled[lo][8 * hi:8 * hi + t_rows] * w[k:k + 1, :]
        d_in = term if d_in is None else d_in + term
        us = u_rolled[lo][HALO - 8 * hi:HALO - 8 * hi + t_rows]
        dw_rows.append(jnp.sum(d_out * us, axis=0, keepdims=True))
    return d_in, dw_rows


SUBLANES = 8


def _scan_fwd(a, b, h0):
    t_rows = a.shape[0]
    sub = jnp.bitwise_and(lax.broadcasted_iota(jnp.int32, a.shape, 0), SUBLANES - 1)
    for d in (1, 2, 4):
        a_sh = jnp.where(sub < d, 1.0, pltpu.roll(a, d, 0))
        b_sh = jnp.where(sub < d, 0.0, pltpu.roll(b, d, 0))
        b = a * b_sh + b
        a = a * a_sh
    out, carry = [], h0
    for g in range(t_rows // SUBLANES):
        rows = slice(g * SUBLANES, (g + 1) * SUBLANES)
        hg = a[rows] * carry + b[rows]
        out.append(hg)
        carry = hg[SUBLANES - 1:SUBLANES]
    return jnp.concatenate(out, axis=0)


def _scan_bwd(a, b, l_end):
    t_rows = a.shape[0]
    sub = jnp.bitwise_and(lax.broadcasted_iota(jnp.int32, a.shape, 0), SUBLANES - 1)
    for d in (1, 2, 4):
        a_sh = jnp.where(sub >= SUBLANES - d, 1.0, pltpu.roll(a, t_rows - d, 0))
        b_sh = jnp.where(sub >= SUBLANES - d, 0.0, pltpu.roll(b, t_rows - d, 0))
        b = b + a * b_sh
        a = a * a_sh
    out, carry = [], l_end
    for g in reversed(range(t_rows // SUBLANES)):
        rows = slice(g * SUBLANES, (g + 1) * SUBLANES)
        lg = b[rows] + a[rows] * carry
        out.append(lg)
        carry = lg[0:1]
    return jnp.concatenate(out[::-1], axis=0)


def _full(shape, single=False):
    nd = len(shape)
    if single:
        return pl.BlockSpec(shape, lambda *_: (0,) * nd, pipeline_mode=pl.Buffered(1))
    return pl.BlockSpec(shape, lambda *_: (0,) * nd)


def _params(sem, vmem=None):
    return pltpu.CompilerParams(dimension_semantics=sem, vmem_limit_bytes=vmem)


def _tile(s):
    return min(512, s)


def _row_tile(rows, cap):
    return next(t for t in range(min(cap, rows) // 8 * 8, 0, -8) if rows % t == 0)


def inproj_fwd(h, g1, w_in_t, pieces=None):
    s = h.shape[0]
    tm = min(1024, s)

    def body(h_ref, g_ref, w_ref, z_ref):
        hn, _ = _rms_fwd(h_ref[...], g_ref[...])
        z_ref[...] = _dot_nt(hn.astype(bf16), w_ref[...]).astype(bf16)

    return _pcall(
        body, pieces, name="inproj_fwd", grid=(s // tm,),
        in_specs=[pl.BlockSpec((tm, D_MODEL), lambda i: (i, 0)), _full((1, D_MODEL)), _full((IN_W, D_MODEL))],
        out_specs=[pl.BlockSpec((tm, IN_W), lambda i: (i, 0))],
        out_shape=[SDS((s, IN_W), bf16)],
        operands=[h, g1, w_in_t], sem=("parallel",), vmem=VMEM_LIMIT)[0]


ATT_ROWS = N_HEADS * BLOCK
ATT_BLOCKS_PER_STEP = 8


def _attn_bias():
    qi = jnp.arange(BLOCK)[None, :]
    key = jnp.arange(2 * BLOCK)[:, None]
    band = (key > qi) & (key <= qi + BLOCK)
    first = band & (key >= BLOCK)
    mask = jnp.where(jnp.stack([first, band]), 0.0, MASK_VALUE).astype(f32)
    return jnp.tile(mask, (1, 1, N_HEADS // 2))


def _attn_band(kvc, kvp):
    kb = jnp.concatenate([kvp[:, :KV_W], kvc[:, :KV_W]], axis=0)
    vb = jnp.concatenate([kvp[:, KV_W:], kvc[:, KV_W:]], axis=0)
    lane = lax.broadcasted_iota(jnp.int32, kb.shape, 1)
    kb_sw = pltpu.roll(kb, HEAD_DIM, 1)
    vb_sw = pltpu.roll(vb, HEAD_DIM, 1)
    kx = [jnp.where(lane < HEAD_DIM, kb, kb_sw), jnp.where(lane >= HEAD_DIM, kb, kb_sw)]
    vx = [jnp.where(lane < HEAD_DIM, vb, vb_sw), jnp.where(lane >= HEAD_DIM, vb, vb_sw)]
    return kx, vx


def _stack_heads(x, mlo):
    zero = jnp.zeros((BLOCK, LANES), x.dtype)
    out = []
    for hk in range(2):
        parts = []
        for j in (2 * hk, 2 * hk + 1):
            xj = x[:, j * LANES:(j + 1) * LANES]
            parts += [jnp.where(mlo, xj, zero), jnp.where(mlo, zero, xj)]
        out.append(jnp.concatenate(parts, axis=0))
    return out


def _unstack_heads(y, mlo):
    cols = []
    for hk in range(2):
        for t in range(2):
            base = 2 * t * BLOCK
            cols.append(jnp.where(mlo, y[hk][base:base + BLOCK], y[hk][base + BLOCK:base + 2 * BLOCK]))
    return jnp.concatenate(cols, axis=1)


def _attn_probs(q4, kx, bias_t, sink_row):
    out = []
    half = ATT_ROWS // 2
    for hk in range(2):
        s = _dot_nt(kx[hk], q4[hk]) + bias_t
        sink = sink_row[:, hk * half:(hk + 1) * half]
        m = jnp.maximum(jnp.max(s, axis=0, keepdims=True), sink)
        p = jnp.exp(s - m)
        e_sink = jnp.exp(sink - m)
        inv = 1.0 / (jnp.sum(p, axis=0, keepdims=True) + e_sink)
        out.append((p * inv, e_sink * inv))
    return out


def attn_fwd(z, sink_row, bias, g_a, pieces=None):
    s = z.shape[0]
    per = min(ATT_BLOCKS_PER_STEP, s // BLOCK)
    tq = per * BLOCK

    def body(q_ref, kv_ref, kvp_ref, sk_ref, b_ref, g_ref, o_ref, y_ref):
        n = pl.program_id(0)
        mlo = lax.broadcasted_iota(jnp.int32, (BLOCK, LANES), 1) < HEAD_DIM
        for b in range(per):
            rows = slice(b * BLOCK, (b + 1) * BLOCK)
            kvp = kvp_ref[...] if b == 0 else kv_ref[(b - 1) * BLOCK:b * BLOCK, :]
            bias_b = b_ref[jnp.minimum(n, 1)] if b == 0 else b_ref[1]
            kx, vx = _attn_band(kv_ref[rows, :], kvp)
            q4 = _stack_heads(q_ref[rows, :] * (HEAD_DIM ** -0.5), mlo)
            probs = _attn_probs(q4, kx, bias_b, sk_ref[...])
            o = _unstack_heads([_dot_tn(probs[hk][0].astype(bf16), vx[hk]) for hk in range(2)], mlo)
            o_ref[rows, :] = o.astype(bf16)
            y, _ = _rms_fwd(o, g_ref[...])
            y_ref[rows, :] = y.astype(bf16)

    return _pcall(
        body, pieces, name="attn_fwd", grid=(s // tq,),
        in_specs=[pl.BlockSpec((tq, ATTN_W), lambda n: (n, 0)),
                  pl.BlockSpec((tq, 2 * KV_W), lambda n: (n, 2)),
                  pl.BlockSpec((BLOCK, 2 * KV_W), lambda n: (jnp.maximum(n * per - 1, 0), 2)),
                  _full((1, ATT_ROWS)), _full((2, 2 * BLOCK, ATT_ROWS // 2)), _full((1, ATTN_W))],
        out_specs=[pl.BlockSpec((tq, ATTN_W), lambda n: (n, 0)), pl.BlockSpec((tq, ATTN_W), lambda n: (n, 0))],
        out_shape=[SDS((s, ATTN_W), bf16), SDS((s, ATTN_W), bf16)],
        operands=[z, z, z, sink_row, bias, g_a], sem=("parallel",))


def _lru_gates(xc, wa, ba, wx, bx, lam):
    xcb = xc.astype(bf16)
    r = _sigmoid(_dot(xcb, wa) + ba)
    ig = _sigmoid(_dot(xcb, wx) + bx)
    sp = _softplus(-lam)
    la = (-LRU_C * r) * sp
    a = jnp.exp(la)
    mult = jnp.sqrt(_neg_expm1(2.0 * la))
    return r, ig, sp, la, a, mult


def branch_fwd(z, p, pieces=None):
    s = z.shape[0]
    tm = _tile(s)
    hb = tm // HALO

    def body(cv_ref, cg_ref, rx_ref, rg_ref, cvh_ref, cgh_ref, rxh_ref,
             cw_ref, cb_ref, lng_ref, lnb_ref, lw_ref, lb_ref, wa_ref, ba_ref, wx_ref, bx_ref, lam_ref, gc_ref, gl_ref,
             conv_ref, hst_ref, nc_ref, nl_ref, carry_ref):
        i = pl.program_id(0)
        first = i == 0

        @pl.when(first)
        def _():
            carry_ref[...] = jnp.zeros_like(carry_ref)

        cval = cv_ref[...].astype(f32)
        u = cval * _sigmoid(cg_ref[...].astype(f32))
        hu = jnp.where(first, 0.0, cvh_ref[...].astype(f32) * _sigmoid(cgh_ref[...].astype(f32)))
        conv = _conv_taps(jnp.concatenate([hu, u], axis=0), cw_ref[...], CONV_K) + cb_ref[...]
        conv_ref[...] = conv
        mu = jnp.mean(conv, axis=-1, keepdims=True)
        xm = conv - mu
        rstd = lax.rsqrt(jnp.mean(xm * xm, axis=-1, keepdims=True) + LN_EPS)
        ln = xm * rstd * lng_ref[...] + lnb_ref[...]
        yc = ln * _sigmoid(ln)
        nc, _ = _rms_fwd(yc, gc_ref[...])
        nc_ref[...] = nc.astype(bf16)

        rx = rx_ref[...].astype(f32)
        hrx = jnp.where(first, 0.0, rxh_ref[...].astype(f32))
        xc = _conv_taps(jnp.concatenate([hrx, rx], axis=0), lw_ref[...], LRU_K) + lb_ref[...]
        r, ig, sp, la, a, mult = _lru_gates(xc, wa_ref[...], ba_ref[...], wx_ref[...], bx_ref[...], lam_ref[...])
        gx = mult * (ig * xc)
        hs = _scan_fwd(a, gx, carry_ref[0:1, :])
        carry_ref[...] = jnp.broadcast_to(hs[tm - 1:tm, :], carry_ref.shape)
        hst_ref[...] = hs
        gl, _ = _gelu(rg_ref[...].astype(f32))
        nl, _ = _rms_fwd(hs * gl, gl_ref[...])
        nl_ref[...] = nl.astype(bf16)

    def col(c):
        return pl.BlockSpec((tm, CONV_W), lambda i: (i, c))

    def halo(c):
        return pl.BlockSpec((HALO, CONV_W), lambda i: (jnp.maximum(i * hb - 1, 0), c))

    small = [p["cw"], p["cb"], p["lng"], p["lnb"], p["lw"], p["lb"], p["wa"], p["ba"], p["wx"], p["bx"], p["lam"],
             p["gc"], p["gl"]]
    row = pl.BlockSpec((tm, CONV_W), lambda i: (i, 0))
    return _pcall(
        body, pieces, name="branch_fwd", grid=(s // tm,),
        in_specs=[col(3), col(4), col(5), col(6), halo(3), halo(4), halo(5)] + [_full(a.shape) for a in small],
        out_specs=[row, row, row, row],
        out_shape=[SDS((s, CONV_W), f32), SDS((s, LRU_W), f32), SDS((s, CONV_W), bf16), SDS((s, LRU_W), bf16)],
        scratch_shapes=[pltpu.VMEM((8, LRU_W), f32)],
        operands=[z, z, z, z, z, z, z, *small], sem=("arbitrary",))


def outproj_fwd(ya, yc, yl, h, w_out, g2, pieces=None):
    s = h.shape[0]
    tm = _tile(s)

    def body(ya_ref, yc_ref, yl_ref, h_ref, w_ref, g_ref, h1_ref, hn_ref):
        y = jnp.concatenate([ya_ref[...], yc_ref[...], yl_ref[...]], axis=1)
        h1 = h_ref[...] + _dot(y, w_ref[...])
        h1_ref[...] = h1
        hn, _ = _rms_fwd(h1, g_ref[...])
        hn_ref[...] = hn.astype(bf16)

    return _pcall(
        body, pieces, name="outproj_fwd", grid=(s // tm,),
        in_specs=[pl.BlockSpec((tm, ATTN_W), lambda i: (i, 0)), pl.BlockSpec((tm, CONV_W), lambda i: (i, 0)),
                  pl.BlockSpec((tm, LRU_W), lambda i: (i, 0)), pl.BlockSpec((tm, D_MODEL), lambda i: (i, 0)),
                  _full((D_MODEL, D_MODEL)), _full((1, D_MODEL))],
        out_specs=[pl.BlockSpec((tm, D_MODEL), lambda i: (i, 0)), pl.BlockSpec((tm, D_MODEL), lambda i: (i, 0))],
        out_shape=[SDS((s, D_MODEL), f32), SDS((s, D_MODEL), bf16)],
        operands=[ya, yc, yl, h, w_out, g2], sem=("parallel",), vmem=VMEM_LIMIT)


def mlp_fwd(hn2, h1, w_up, w_dn, pieces=None):
    s = h1.shape[0]
    tm = _tile(s)

    def body(x_ref, h_ref, wu_ref, wd_ref, up_ref, h2_ref):
        h2_ref[...] = _mlp_chunks(x_ref[...], h_ref[...], wu_ref, wd_ref, up_ref)

    return _pcall(
        body, pieces, name="mlp_fwd", grid=(s // tm,),
        in_specs=[pl.BlockSpec((tm, D_MODEL), lambda i: (i, 0)), pl.BlockSpec((tm, D_MODEL), lambda i: (i, 0)),
                  _full((N_SHARD, D_MODEL, FF_CHUNK), single=True), _full((N_SHARD, FF_CHUNK, D_MODEL), single=True)],
        out_specs=[pl.BlockSpec((tm, D_FF), lambda i: (i, 0)), pl.BlockSpec((tm, D_MODEL), lambda i: (i, 0))],
        out_shape=[SDS((s, D_FF), bf16), SDS((s, D_MODEL), f32)],
        operands=[hn2, h1, w_up, w_dn], sem=("parallel",), vmem=VMEM_LIMIT)


def _mlp_chunks(x, acc, wu_ref, wd_ref, up_ref):
    for c in range(N_SHARD):
        u = _dot(x, wu_ref[c])
        up_ref[:, c * FF_CHUNK:(c + 1) * FF_CHUNK] = u.astype(bf16)
        act = jnp.square(jnp.maximum(u, 0.0)).astype(bf16)
        acc = acc + _dot(act, wd_ref[c])
    return acc


def _final_tile(x, tgt, g, dh_ref, loss_ref, dg_ref):
    y, r = _rms_fwd(x, g)
    err = y - tgt
    part = 0.5 * jnp.sum(jnp.mean(err * err, axis=-1, keepdims=True), axis=0, keepdims=True)
    loss_ref[...] += jnp.broadcast_to(part, loss_ref.shape)
    dx, dg = _rms_bwd(err * (1.0 / D_MODEL), x, r, g)
    dh_ref[...] = dx
    dg_ref[...] += dg


def last_layer_fwd(ya, yc, yl, h, w_out, g2, w_up, w_dn, tgt, gf):
    s = h.shape[0]
    tm = min(256, s)

    def body(ya_ref, yc_ref, yl_ref, h_ref, wo_ref, g2_ref, wu_ref, wd_ref, t_ref, gf_ref,
             h1_ref, hn_ref, up_ref, dh_ref, loss_ref, dg_ref):
        i = pl.program_id(0)

        @pl.when(i == 0)
        def _():
            loss_ref[...] = jnp.zeros_like(loss_ref)
            dg_ref[...] = jnp.zeros_like(dg_ref)

        y = jnp.concatenate([ya_ref[...], yc_ref[...], yl_ref[...]], axis=1)
        h1 = h_ref[...] + _dot(y, wo_ref[...])
        h1_ref[...] = h1
        hn, _ = _rms_fwd(h1, g2_ref[...])
        hn = hn.astype(bf16)
        hn_ref[...] = hn
        h2 = _mlp_chunks(hn, h1, wu_ref, wd_ref, up_ref)
        _final_tile(h2, t_ref[...], gf_ref[...], dh_ref, loss_ref, dg_ref)

    def rowb(w):
        return pl.BlockSpec((tm, w), lambda i: (i, 0))

    return pl.pallas_call(
        body, name="last_layer_fwd", grid=(s // tm,),
        in_specs=[rowb(ATTN_W), rowb(CONV_W), rowb(LRU_W), rowb(D_MODEL), _full((D_MODEL, D_MODEL), single=True),
                  _full((1, D_MODEL)), _full((N_SHARD, D_MODEL, FF_CHUNK), single=True),
                  _full((N_SHARD, FF_CHUNK, D_MODEL), single=True), rowb(D_MODEL), _full((1, D_MODEL))],
        out_specs=[rowb(D_MODEL), rowb(D_MODEL), rowb(D_FF), rowb(D_MODEL), _full((8, LANES)), _full((1, D_MODEL))],
        out_shape=[SDS((s, D_MODEL), f32), SDS((s, D_MODEL), bf16), SDS((s, D_FF), bf16), SDS((s, D_MODEL), f32),
                   SDS((8, LANES), f32), SDS((1, D_MODEL), f32)],
        compiler_params=_params(("arbitrary",), VMEM_LIMIT),
    )(ya, yc, yl, h, w_out, g2, w_up, w_dn, tgt, gf)


def mlp_bwd_act(dh, up, h1, g2, w_up, w_dn, pieces=None):
    s = dh.shape[0]
    tm = _tile(s)

    def body(dh_ref, up_ref, h1_ref, g_ref, wu_ref, wd_ref, dup_ref, dh1_ref, dg_ref):
        i = pl.program_id(0)

        @pl.when(i == 0)
        def _():
            dg_ref[...] = jnp.zeros_like(dg_ref)

        dh = dh_ref[...]
        dhb = dh.astype(bf16)
        d_hn = jnp.zeros((tm, D_MODEL), f32)
        for c in range(N_SHARD):
            d_act = _dot_nt(dhb, wd_ref[c])
            u = up_ref[:, c * FF_CHUNK:(c + 1) * FF_CHUNK].astype(f32)
            d_u = (d_act * (2.0 * jnp.maximum(u, 0.0))).astype(bf16)
            dup_ref[:, c * FF_CHUNK:(c + 1) * FF_CHUNK] = d_u
            d_hn = d_hn + _dot_nt(d_u, wu_ref[c])
        x = h1_ref[...]
        g = g_ref[...]
        _, r = _rms_fwd(x, g)
        dx, dg = _rms_bwd(d_hn, x, r, g)
        dh1_ref[...] = dh + dx
        dg_ref[...] += dg

    return _pcall(
        body, pieces, name="mlp_bwd_act", grid=(s // tm,),
        in_specs=[pl.BlockSpec((tm, D_MODEL), lambda i: (i, 0)), pl.BlockSpec((tm, D_FF), lambda i: (i, 0)),
                  pl.BlockSpec((tm, D_MODEL), lambda i: (i, 0)), _full((1, D_MODEL)),
                  _full((N_SHARD, D_MODEL, FF_CHUNK), single=True), _full((N_SHARD, FF_CHUNK, D_MODEL), single=True)],
        out_specs=[pl.BlockSpec((tm, D_FF), lambda i: (i, 0)), pl.BlockSpec((tm, D_MODEL), lambda i: (i, 0)),
                   _full((1, D_MODEL))],
        out_shape=[SDS((s, D_FF), bf16), SDS((s, D_MODEL), f32), SDS((1, D_MODEL), f32)],
        operands=[dh, up, h1, g2, w_up, w_dn], sem=("arbitrary",), vmem=VMEM_LIMIT)


def mlp_bwd_w(hn2, d_up, up, dh, pieces=None):
    s = dh.shape[0]
    tk = min(1024, s)

    def body(x_ref, du_ref, up_ref, dh_ref, dwu_ref, dwd_ref):
        k = pl.program_id(1)

        @pl.when(k == 0)
        def _():
            dwu_ref[...] = jnp.zeros_like(dwu_ref)
            dwd_ref[...] = jnp.zeros_like(dwd_ref)

        dwu_ref[0] += _dot_tn(x_ref[...], du_ref[...])
        act = jnp.square(jnp.maximum(up_ref[...].astype(f32), 0.0)).astype(bf16)
        dwd_ref[0] += _dot_tn(act, dh_ref[...].astype(bf16))

    return _pcall(
        body, pieces, name="mlp_bwd_w", grid=(N_SHARD, s // tk),
        in_specs=[pl.BlockSpec((tk, D_MODEL), lambda c, k: (k, 0)), pl.BlockSpec((tk, FF_CHUNK), lambda c, k: (k, c)),
                  pl.BlockSpec((tk, FF_CHUNK), lambda c, k: (k, c)), pl.BlockSpec((tk, D_MODEL), lambda c, k: (k, 0))],
        out_specs=[pl.BlockSpec((1, D_MODEL, FF_CHUNK), lambda c, k: (c, 0, 0)),
                   pl.BlockSpec((1, FF_CHUNK, D_MODEL), lambda c, k: (c, 0, 0))],
        out_shape=[SDS((N_SHARD, D_MODEL, FF_CHUNK), f32), SDS((N_SHARD, FF_CHUNK, D_MODEL), f32)],
        operands=[hn2, d_up, up, dh], sem=("parallel", "arbitrary"), vmem=VMEM_LIMIT)


def outproj_bwd(dh1, ya, yc, yl, w_out):
    s = dh1.shape[0]
    tm = min(1024, s)

    def body(dh_ref, ya_ref, yc_ref, yl_ref, w_ref, dy_ref, dw_ref):
        i = pl.program_id(0)

        @pl.when(i == 0)
        def _():
            dw_ref[...] = jnp.zeros_like(dw_ref)

        dhb = dh_ref[...].astype(bf16)
        dy_ref[...] = _dot_nt(dhb, w_ref[...])
        y = jnp.concatenate([ya_ref[...], yc_ref[...], yl_ref[...]], axis=1)
        dw_ref[...] += _dot_tn(y, dhb)

    return pl.pallas_call(
        body, name="outproj_bwd", grid=(s // tm,),
        in_specs=[pl.BlockSpec((tm, D_MODEL), lambda i: (i, 0)), pl.BlockSpec((tm, ATTN_W), lambda i: (i, 0)),
                  pl.BlockSpec((tm, CONV_W), lambda i: (i, 0)), pl.BlockSpec((tm, LRU_W), lambda i: (i, 0)),
                  _full((D_MODEL, D_MODEL))],
        out_specs=[pl.BlockSpec((tm, D_MODEL), lambda i: (i, 0)), _full((D_MODEL, D_MODEL))],
        out_shape=[SDS((s, D_MODEL), f32), SDS((D_MODEL, D_MODEL), f32)],
        compiler_params=_params(("arbitrary",), VMEM_LIMIT),
    )(dh1, ya, yc, yl, w_out)


def attn_bwd(z, o, dy, sink_row, bias, g_a, pieces=None):
    s = z.shape[0]
    per = min(ATT_BLOCKS_PER_STEP, s // BLOCK)
    tq = per * BLOCK
    nt = s // tq

    def body(q_ref, kv_ref, kvp_ref, o_ref, dy_ref, sk_ref, b_ref, g_ref, dq_ref, dkv_ref, dsk_ref, dg_ref,
             carry_ref, dsk_acc):
        i = pl.program_id(0)
        t = nt - 1 - i

        @pl.when(i == 0)
        def _():
            carry_ref[...] = jnp.zeros_like(carry_ref)
            dsk_acc[...] = jnp.zeros_like(dsk_acc)
            dg_ref[...] = jnp.zeros_like(dg_ref)

        mlo = lax.broadcasted_iota(jnp.int32, (BLOCK, LANES), 1) < HEAD_DIM
        lane = lax.broadcasted_iota(jnp.int32, (2 * BLOCK, LANES), 1)
        scale = HEAD_DIM ** -0.5
        half = ATT_ROWS // 2
        g = g_ref[...]
        bands = []
        for b in range(per):
            rows = slice(b * BLOCK, (b + 1) * BLOCK)
            kvp = kvp_ref[...] if b == 0 else kv_ref[(b - 1) * BLOCK:b * BLOCK, :]
            bias_b = b_ref[jnp.minimum(t, 1)] if b == 0 else b_ref[1]
            kx, vx = _attn_band(kv_ref[rows, :], kvp)
            q4 = _stack_heads(q_ref[rows, :] * scale, mlo)
            o_f = o_ref[rows, :].astype(f32)
            _, r = _rms_fwd(o_f, g)
            d_o, dg = _rms_bwd(dy_ref[rows, :], o_f, r, g)
            dg_ref[...] += dg
            do4 = _stack_heads(d_o.astype(bf16), mlo)
            probs = _attn_probs(q4, kx, bias_b, sk_ref[...])
            dq4, tk, tv = [], [], []
            for hk in range(2):
                pr, p_sink = probs[hk]
                d_p = _dot_nt(vx[hk], do4[hk])
                d_row = jnp.sum(pr * d_p, axis=0, keepdims=True)
                d_s = (pr * (d_p - d_row)).astype(bf16)
                dsk_acc[:, hk * half:(hk + 1) * half] -= p_sink * d_row
                dq4.append(_dot_tn(d_s, kx[hk]))
                tk.append(_dot(d_s, q4[hk]))
                tv.append(_dot(pr.astype(bf16), do4[hk]))
            dq_ref[rows, :] = (_unstack_heads(dq4, mlo) * scale).astype(bf16)
            fk = [x + pltpu.roll(x, HEAD_DIM, 1) for x in tk]
            fv = [x + pltpu.roll(x, HEAD_DIM, 1) for x in tv]
            bands.append(jnp.concatenate([jnp.where(lane < HEAD_DIM, fk[0], fk[1]),
                                          jnp.where(lane < HEAD_DIM, fv[0], fv[1])], axis=1))
        for b in range(per):
            after = bands[b + 1][:BLOCK] if b + 1 < per else carry_ref[...]
            dkv_ref[b * BLOCK:(b + 1) * BLOCK, :] = (bands[b][BLOCK:] + after).astype(bf16)
        carry_ref[...] = bands[0][:BLOCK]

        @pl.when(i == nt - 1)
        def _():
            for hh in range(N_HEADS):
                tot = jnp.sum(dsk_acc[:, hh * BLOCK:(hh + 1) * BLOCK], axis=1, keepdims=True)
                dsk_ref[hh:hh + 1, :] = jnp.broadcast_to(tot, (1, LANES))

    def rev(width, col):
        return pl.BlockSpec((tq, width), lambda i: (nt - 1 - i, col))

    return _pcall(
        body, pieces, name="attn_bwd", grid=(nt,),
        in_specs=[rev(ATTN_W, 0), rev(2 * KV_W, 2),
                  pl.BlockSpec((BLOCK, 2 * KV_W), lambda i: (jnp.maximum((nt - 1 - i) * per - 1, 0), 2)),
                  rev(ATTN_W, 0), rev(ATTN_W, 0),
                  _full((1, ATT_ROWS)), _full((2, 2 * BLOCK, ATT_ROWS // 2)), _full((1, ATTN_W))],
        out_specs=[rev(ATTN_W, 0), rev(2 * KV_W, 0), _full((N_HEADS, LANES)), _full((1, ATTN_W))],
        out_shape=[SDS((s, ATTN_W), bf16), SDS((s, 2 * KV_W), bf16), SDS((N_HEADS, LANES), f32), SDS((1, ATTN_W), f32)],
        scratch_shapes=[pltpu.VMEM((BLOCK, 2 * KV_W), f32), pltpu.VMEM((1, ATT_ROWS), f32)],
        operands=[z, z, z, o, dy, sink_row, bias, g_a], sem=("arbitrary",))


_V_GC, _V_LNG, _V_LNB, _V_CB, _V_GL, _V_BA, _V_BX, _V_LAM, _V_LB = range(9)
_V_ROWS = 16


def branch_bwd_a(z, conv, hst, dy, p, pieces=None):
    s = z.shape[0]
    tm = _tile(s)
    nt = s // tm
    hb = tm // HALO
    h8 = tm // 8

    def body(conv_ref, dyc_ref, dyl_ref, rx_ref, rxh_ref, rg_ref, hst_ref, hsth_ref,
             lng_ref, lnb_ref, lw_ref, lb_ref, wa_ref, ba_ref, wx_ref, bx_ref, lam_ref, gc_ref, gl_ref,
             dconv_ref, dxc_ref, drg_ref, vec_ref, dwa_ref, dwx_ref, carry_ref):
        i = pl.program_id(0)
        ti = nt - 1 - i

        @pl.when(i == 0)
        def _():
            carry_ref[...] = jnp.zeros_like(carry_ref)
            vec_ref[...] = jnp.zeros_like(vec_ref)
            dwa_ref[...] = jnp.zeros_like(dwa_ref)
            dwx_ref[...] = jnp.zeros_like(dwx_ref)

        conv = conv_ref[...]
        mu = jnp.mean(conv, axis=-1, keepdims=True)
        xm = conv - mu
        rstd = lax.rsqrt(jnp.mean(xm * xm, axis=-1, keepdims=True) + LN_EPS)
        xhat = xm * rstd
        lng = lng_ref[...]
        ln = xhat * lng + lnb_ref[...]
        sg = _sigmoid(ln)
        yc = ln * sg
        gc = gc_ref[...]
        _, rc = _rms_fwd(yc, gc)
        d_yc, d_gc = _rms_bwd(dyc_ref[...], yc, rc, gc)
        d_ln = d_yc * (sg * (1.0 + ln * (1.0 - sg)))
        d_xhat = d_ln * lng
        d_conv = rstd * (d_xhat - jnp.mean(d_xhat, axis=-1, keepdims=True)
                         - xhat * jnp.mean(d_xhat * xhat, axis=-1, keepdims=True))
        dconv_ref[...] = d_conv

        rx = rx_ref[...].astype(f32)
        hrx = jnp.where(ti == 0, 0.0, rxh_ref[...].astype(f32))
        xc = _conv_taps(jnp.concatenate([hrx, rx], axis=0), lw_ref[...], LRU_K) + lb_ref[...]
        wa = wa_ref[...]
        wx = wx_ref[...]
        lam = lam_ref[...]
        r, ig, sp, la, a, mult = _lru_gates(xc, wa, ba_ref[...], wx, bx_ref[...], lam)
        hs = hst_ref[...]
        row = lax.broadcasted_iota(jnp.int32, hs.shape, 0)
        h_before = jnp.where(ti == 0, 0.0, hsth_ref[7:8, :])
        h_prev = jnp.where(row == 0, h_before, pltpu.roll(hs, 1, 0))
        rg = rg_ref[...].astype(f32)
        gl, tg = _gelu(rg)
        out = hs * gl
        gmix = gl_ref[...]
        _, rl = _rms_fwd(out, gmix)
        d_out, d_gl = _rms_bwd(dyl_ref[...], out, rl, gmix)
        drg_ref[...] = (d_out * hs * _gelu_grad(rg, tg)).astype(bf16)
        d_h = d_out * gl
        last = row == tm - 1
        a_next = jnp.where(last, 1.0, pltpu.roll(a, tm - 1, 0))
        lmb = _scan_bwd(a_next, d_h, carry_ref[0:1, :])
        carry_ref[...] = jnp.broadcast_to(a[0:1, :] * lmb[0:1, :], carry_ref.shape)
        d_a = lmb * h_prev
        d_mult = lmb * (ig * xc)
        d_ig = lmb * (mult * xc)
        d_la = d_a * a - d_mult * (a * a) / jnp.maximum(mult, 1e-30)
        d_pa = (d_la * (-LRU_C * sp)) * (r * (1.0 - r))
        d_px = d_ig * (ig * (1.0 - ig))
        d_pab = d_pa.astype(bf16)
        d_pxb = d_px.astype(bf16)
        d_xc = lmb * (mult * ig) + _dot_nt(d_pab, wa) + _dot_nt(d_pxb, wx)
        dxc_ref[...] = d_xc
        xcb = xc.astype(bf16)
        dwa_ref[...] += _dot_tn(xcb, d_pab)
        dwx_ref[...] += _dot_tn(xcb, d_pxb)
        d_lam = jnp.sum(d_la * (-LRU_C * r), axis=0, keepdims=True) * (-_sigmoid(-lam))

        def colsum(v):
            return jnp.sum(v, axis=0, keepdims=True)

        rows = [None] * _V_ROWS
        rows[_V_GC] = d_gc
        rows[_V_LNG] = colsum(d_ln * xhat)
        rows[_V_LNB] = colsum(d_ln)
        rows[_V_CB] = colsum(d_conv)
        rows[_V_GL] = d_gl
        rows[_V_BA] = colsum(d_pa)
        rows[_V_BX] = colsum(d_px)
        rows[_V_LAM] = d_lam
        rows[_V_LB] = colsum(d_xc)
        zero = jnp.zeros((1, CONV_W), f32)
        vec_ref[...] += jnp.concatenate([zero if v is None else v for v in rows], axis=0)

    def rev(c):
        return pl.BlockSpec((tm, CONV_W), lambda i: (nt - 1 - i, c))

    small = [p["lng"], p["lnb"], p["lw"], p["lb"], p["wa"], p["ba"], p["wx"], p["bx"], p["lam"], p["gc"], p["gl"]]
    return _pcall(
        body, pieces, name="branch_bwd_a", grid=(nt,),
        in_specs=[rev(0), rev(2), rev(3), rev(5),
                  pl.BlockSpec((HALO, CONV_W), lambda i: (jnp.maximum((nt - 1 - i) * hb - 1, 0), 5)),
                  rev(6), rev(0),
                  pl.BlockSpec((8, LRU_W), lambda i: (jnp.maximum((nt - 1 - i) * h8 - 1, 0), 0))]
                 + [_full(a.shape) for a in small],
        out_specs=[rev(0), rev(0), rev(0), _full((_V_ROWS, CONV_W)), _full((LRU_W, LRU_W)), _full((LRU_W, LRU_W))],
        out_shape=[SDS((s, CONV_W), f32), SDS((s, LRU_W), f32), SDS((s, LRU_W), bf16), SDS((_V_ROWS, CONV_W), f32),
                   SDS((LRU_W, LRU_W), f32), SDS((LRU_W, LRU_W), f32)],
        scratch_shapes=[pltpu.VMEM((8, LRU_W), f32)],
        operands=[conv, dy, dy, z, z, z, hst, hst, *small], sem=("arbitrary",))


def branch_bwd_b(z, d_conv, d_xc, p, pieces=None):
    s = z.shape[0]
    tm = _tile(s)
    nt = s // tm
    hb = tm // HALO

    def body(cv_ref, cg_ref, cvh_ref, cgh_ref, rx_ref, rxh_ref, dc_ref, dch_ref, dx_ref, dxh_ref, cw_ref, lw_ref,
             dzc_ref, dzr_ref, dcw_ref, dlw_ref):
        i = pl.program_id(0)

        @pl.when(i == 0)
        def _():
            dcw_ref[...] = jnp.zeros_like(dcw_ref)
            dlw_ref[...] = jnp.zeros_like(dlw_ref)

        first = i == 0
        last = i == nt - 1
        cval = cv_ref[...].astype(f32)
        sg = _sigmoid(cg_ref[...].astype(f32))
        u = cval * sg
        hu = jnp.where(first, 0.0, cvh_ref[...].astype(f32) * _sigmoid(cgh_ref[...].astype(f32)))
        dpad = jnp.concatenate([dc_ref[...], jnp.where(last, 0.0, dch_ref[...])], axis=0)
        d_u, dw_rows = _conv_taps_bwd(dpad, jnp.concatenate([hu, u], axis=0), cw_ref[...], CONV_K, tm)
        dcw_ref[...] += jnp.concatenate(dw_rows + [jnp.zeros((HALO - CONV_K, CONV_W), f32)], axis=0)
        dzc_ref[...] = jnp.concatenate([d_u * sg, d_u * cval * sg * (1.0 - sg)], axis=1).astype(bf16)

        rx = rx_ref[...].astype(f32)
        hrx = jnp.where(first, 0.0, rxh_ref[...].astype(f32))
        dxpad = jnp.concatenate([dx_ref[...], jnp.where(last, 0.0, dxh_ref[...])], axis=0)
        d_rx, dlw_rows = _conv_taps_bwd(dxpad, jnp.concatenate([hrx, rx], axis=0), lw_ref[...], LRU_K, tm)
        dlw_ref[...] += jnp.concatenate(dlw_rows + [jnp.zeros((8 - LRU_K, LRU_W), f32)], axis=0)
        dzr_ref[...] = d_rx.astype(bf16)

    def col(c):
        return pl.BlockSpec((tm, CONV_W), lambda i: (i, c))

    def prev(c):
        return pl.BlockSpec((HALO, CONV_W), lambda i: (jnp.maximum(i * hb - 1, 0), c))

    nxt = pl.BlockSpec((HALO, CONV_W), lambda i: (jnp.minimum((i + 1) * hb, nt * hb - 1), 0))
    return _pcall(
        body, pieces, name="branch_bwd_b", grid=(nt,),
        in_specs=[col(3), col(4), prev(3), prev(4), col(5), prev(5), col(0), nxt, col(0), nxt,
                  _full(p["cw"].shape), _full(p["lw"].shape)],
        out_specs=[pl.BlockSpec((tm, 2 * CONV_W), lambda i: (i, 0)), pl.BlockSpec((tm, LRU_W), lambda i: (i, 0)),
                   _full((HALO, CONV_W)), _full((8, LRU_W))],
        out_shape=[SDS((s, 2 * CONV_W), bf16), SDS((s, LRU_W), bf16), SDS((HALO, CONV_W), f32), SDS((8, LRU_W), f32)],
        operands=[z, z, z, z, z, z, d_conv, d_conv, d_xc, d_xc, p["cw"], p["lw"]], sem=("arbitrary",))


def inproj_bwd(dq, dkv, dzc, dzr, drg, h, g1, w_in_t, dh1, pieces=None):
    s = h.shape[0]
    tm = _tile(s)

    def body(dq_ref, dkv_ref, dzc_ref, dzr_ref, drg_ref, h_ref, g_ref, w_ref, dh1_ref, dh_ref, dw_ref, dg_ref):
        i = pl.program_id(0)

        @pl.when(i == 0)
        def _():
            dw_ref[...] = jnp.zeros_like(dw_ref)
            dg_ref[...] = jnp.zeros_like(dg_ref)

        dz = jnp.concatenate([dq_ref[...], dkv_ref[...], dzc_ref[...], dzr_ref[...], drg_ref[...]], axis=1)
        x = h_ref[...]
        g = g_ref[...]
        hn, r = _rms_fwd(x, g)
        d_hn = _dot(dz, w_ref[...])
        dw_ref[...] += _dot_tn(dz, hn.astype(bf16))
        dx, dg = _rms_bwd(d_hn, x, r, g)
        dh_ref[...] = dh1_ref[...] + dx
        dg_ref[...] += dg

    def rowb(w):
        return pl.BlockSpec((tm, w), lambda i: (i, 0))

    return _pcall(
        body, pieces, name="inproj_bwd", grid=(s // tm,),
        in_specs=[rowb(ATTN_W), rowb(2 * KV_W), rowb(2 * CONV_W), rowb(LRU_W), rowb(LRU_W), rowb(D_MODEL),
                  _full((1, D_MODEL)), _full((IN_W, D_MODEL)), rowb(D_MODEL)],
        out_specs=[rowb(D_MODEL), _full((IN_W, D_MODEL)), _full((1, D_MODEL))],
        out_shape=[SDS((s, D_MODEL), f32), SDS((IN_W, D_MODEL), f32), SDS((1, D_MODEL), f32)],
        operands=[dq, dkv, dzc, dzr, drg, h, g1, w_in_t, dh1], sem=("arbitrary",), vmem=VMEM_LIMIT)


def _block_diag(w):
    out = jnp.zeros((LRU_W, LRU_W), w.dtype)
    hd = LRU_W // LRU_HEADS
    for hh in range(LRU_HEADS):
        out = out.at[hh * hd:(hh + 1) * hd, hh * hd:(hh + 1) * hd].set(w[hh])
    return out


def _diag_blocks(w):
    hd = LRU_W // LRU_HEADS
    return jnp.stack([w[hh * hd:(hh + 1) * hd, hh * hd:(hh + 1) * hd] for hh in range(LRU_HEADS)])


def _layer_params(sp, l):
    mix = sp["mix_norm"][l]
    return dict(
        g1=sp["norm1"][l][None, :], g2=sp["norm2"][l][None, :],
        sinks=jnp.repeat(sp["attn_sinks"][l], BLOCK)[None, :],
        ga=mix[None, :ATTN_W], gc=mix[None, ATTN_W:ATTN_W + CONV_W], gl=mix[None, ATTN_W + CONV_W:],
        cw=jnp.pad(sp["conv_dw_w"][l], ((0, HALO - CONV_K), (0, 0))), cb=sp["conv_dw_b"][l][None, :],
        lng=sp["conv_ln_g"][l][None, :], lnb=sp["conv_ln_b"][l][None, :],
        lw=jnp.pad(sp["lru_conv_w"][l], ((0, 8 - LRU_K), (0, 0))), lb=sp["lru_conv_b"][l][None, :],
        wa=_block_diag(sp["lru_wa"][l]).astype(bf16), ba=sp["lru_ba"][l].reshape(1, LRU_W),
        wx=_block_diag(sp["lru_wx"][l]).astype(bf16), bx=sp["lru_bx"][l].reshape(1, LRU_W),
        lam=sp["lru_lambda"][l][None, :],
    )


def local_step(x, tgt, big, sp):
    return train_local(x, tgt, sp, LocalWeights(big))


class LocalWeights:
    def __init__(self, big):
        self.big = big
        self.grads = [dict() for _ in range(DEPTH)]

    def weight(self, name, l):
        return self.big[l][name]

    def host(self, point, l):
        return None

    def grad(self, name, l, g):
        self.grads[l][name] = g

    def big_grads(self):
        return self.grads


def train_local(x, tgt, sp, cs):
    lp = [_layer_params(sp, l) for l in range(DEPTH)]
    bias = _attn_bias()
    saved = []
    h = x
    for l in range(DEPTH):
        p = lp[l]
        z = inproj_fwd(h, p["g1"], cs.weight("w_in", l), cs.host("inproj_fwd", l))
        o, ya = attn_fwd(z, p["sinks"], bias, p["ga"], cs.host("attn_fwd", l))
        conv, hst, yc, yl = branch_fwd(z, p, cs.host("branch_fwd", l))
        if l < DEPTH - 1:
            h1, hn2 = outproj_fwd(ya, yc, yl, h, cs.weight("w_out", l), p["g2"], cs.host("outproj_fwd", l))
            up, h_next = mlp_fwd(hn2, h1, cs.weight("w_up", l), cs.weight("w_dn", l), cs.host("mlp_fwd", l))
        else:
            h1, hn2, up, dh, loss, d_gf = last_layer_fwd(
                ya, yc, yl, h, cs.weight("w_out", l), p["g2"], cs.weight("w_up", l), cs.weight("w_dn", l),
                tgt, sp["final_norm"][None, :])
            h_next = None
        saved.append(dict(h=h, z=z, o=o, ya=ya, conv=conv, hst=hst, yc=yc, yl=yl, h1=h1, hn2=hn2, up=up))
        h = h_next
    small_g = [None] * DEPTH
    for l in reversed(range(DEPTH)):
        p, sv = lp[l], saved[l]
        w_up, w_dn = cs.weight("w_up", l), cs.weight("w_dn", l)
        d_up, dh1, d_g2 = mlp_bwd_act(dh, sv["up"], sv["h1"], p["g2"], w_up, w_dn, cs.host("mlp_bwd_act", l))
        dw_up, dw_dn = mlp_bwd_w(sv["hn2"], d_up, sv["up"], dh, cs.host("mlp_bwd_w", l))
        cs.grad("w_up", l, dw_up)
        cs.grad("w_dn", l, dw_dn)
        dy, dw_out = outproj_bwd(dh1, sv["ya"], sv["yc"], sv["yl"], cs.weight("w_out", l))
        cs.grad("w_out", l, dw_out)
        dq, dkv, d_sk, d_ga = attn_bwd(sv["z"], sv["o"], dy, p["sinks"], bias, p["ga"], cs.host("attn_bwd", l))
        d_conv, d_xc, d_rg, vec, dwa, dwx = branch_bwd_a(sv["z"], sv["conv"], sv["hst"], dy, p, cs.host("branch_bwd_a", l))
        dzc, dzr, dcw, dlw = branch_bwd_b(sv["z"], d_conv, d_xc, p, cs.host("branch_bwd_b", l))
        dh, dw_in, d_g1 = inproj_bwd(dq, dkv, dzc, dzr, d_rg, sv["h"], p["g1"], cs.weight("w_in", l), dh1,
                                     cs.host("inproj_bwd", l))
        cs.grad("w_in", l, dw_in)
        hd = LRU_W // LRU_HEADS
        small_g[l] = dict(
            norm1=d_g1[0], attn_sinks=d_sk[:, 0], conv_dw_w=dcw[:CONV_K], conv_dw_b=vec[_V_CB],
            conv_ln_g=vec[_V_LNG], conv_ln_b=vec[_V_LNB], lru_conv_w=dlw[:LRU_K], lru_conv_b=vec[_V_LB],
            lru_wa=_diag_blocks(dwa), lru_ba=vec[_V_BA].reshape(LRU_HEADS, hd),
            lru_wx=_diag_blocks(dwx), lru_bx=vec[_V_BX].reshape(LRU_HEADS, hd), lru_lambda=vec[_V_LAM],
            mix_norm=jnp.concatenate([d_ga[0], vec[_V_GC], vec[_V_GL]]), norm2=d_g2[0],
        )
    return loss, dh, cs.big_grads(), small_g, d_gf[0]


_HBM = pl.BlockSpec(memory_space=pl.ANY)


def _place():
    x, y, c = lax.axis_index("x"), lax.axis_index("y"), lax.axis_index("c")
    chips = [(1 - x, y), (x, 1 - y), (1 - x, 1 - y)]
    return x, y, c, chips


class Comm:
    def __init__(self, ins, out_shape, aliases, sems, start, finish, done):
        self.ins, self.out_shape, self.aliases, self.sems = list(ins), list(out_shape), dict(aliases), list(sems)
        self.start, self.finish, self.done = start, finish, done


def _pcall(body, pieces, *, name, grid, in_specs, out_specs, out_shape, operands, scratch_shapes=(), sem, vmem=None,
           prefetch=None):
    in_specs, out_specs, out_shape, scratch_shapes = list(in_specs), list(out_specs), list(out_shape), list(scratch_shapes)
    lead = 0 if prefetch is None else 1

    def call(fn, call_name, ins, outs, shapes, scratch, aliases, semantics, args):
        params = _params(semantics, vmem)
        if prefetch is None:
            return pl.pallas_call(fn, name=call_name, grid=grid, in_specs=ins, out_specs=outs, out_shape=shapes,
                                  scratch_shapes=scratch, input_output_aliases=aliases, compiler_params=params)(*args)
        spec = pltpu.PrefetchScalarGridSpec(num_scalar_prefetch=1, grid=grid, in_specs=ins, out_specs=outs,
                                            scratch_shapes=scratch)
        return pl.pallas_call(fn, name=call_name, grid_spec=spec, out_shape=shapes, input_output_aliases=aliases,
                              compiler_params=params)(prefetch, *args)

    if not pieces:
        return call(body, name, in_specs, out_specs, out_shape, scratch_shapes, {}, sem, operands)
    n_in, n_out, n_scr = len(in_specs), len(out_specs), len(scratch_shapes)
    c_ins = [a for p in pieces for a in p.ins]
    c_outs = [s for p in pieces for s in p.out_shape]
    c_sems = [n for p in pieces for n in p.sems]
    aliases, spans, ki, ko, ks = {}, [], 0, 0, 0
    for p in pieces:
        spans.append((ki, ko, ks))
        for a, b in p.aliases.items():
            aliases[lead + n_in + ki + a] = n_out + ko + b
        ki, ko, ks = ki + len(p.ins), ko + len(p.out_shape), ks + len(p.sems)

    def hosted(*refs):
        pre, refs = refs[:lead], refs[lead:]
        ins, cin = refs[:n_in], refs[n_in:n_in + ki]
        outs, cout = refs[n_in + ki:n_in + ki + n_out], refs[n_in + ki + n_out:n_in + ki + n_out + ko]
        scr, csem = refs[n_in + ki + n_out + ko:n_in + ki + n_out + ko + n_scr], refs[n_in + ki + n_out + ko + n_scr:]
        first = functools.reduce(jnp.logical_and, [pl.program_id(d) == 0 for d in range(len(grid))])
        last = functools.reduce(jnp.logical_and, [pl.program_id(d) == grid[d] - 1 for d in range(len(grid))])

        def each(which):
            for p, (a, b, s) in zip(pieces, spans):
                getattr(p, which)(cin[a:a + len(p.ins)], cout[b:b + len(p.out_shape)], csem[s:s + len(p.sems)])

        @pl.when(first)
        def _():
            each("start")

        body(*pre, *ins, *outs, *scr)

        @pl.when(last)
        def _():
            each("finish")

    res = call(hosted, name + "_host", in_specs + [_HBM] * ki, out_specs + [_HBM] * ko, out_shape + c_outs,
               scratch_shapes + [pltpu.SemaphoreType.DMA((n,)) for n in c_sems], aliases, ("arbitrary",) * len(grid),
               [*operands, *c_ins])
    for p, (a, b, s) in zip(pieces, spans):
        p.done(res[n_out + b:n_out + b + len(p.out_shape)])
    return res[:n_out]


def standalone(pieces, name):
    ki = sum(len(p.ins) for p in pieces)
    ko = sum(len(p.out_shape) for p in pieces)
    spans, a, b, s = [], 0, 0, 0
    aliases = {}
    for p in pieces:
        spans.append((a, b, s))
        for i, o in p.aliases.items():
            aliases[a + i] = b + o
        a, b, s = a + len(p.ins), b + len(p.out_shape), s + len(p.sems)

    def body(*refs):
        cin, cout, csem = refs[:ki], refs[ki:ki + ko], refs[ki + ko:]
        for which in ("start", "finish"):
            for p, (a, b, s) in zip(pieces, spans):
                getattr(p, which)(cin[a:a + len(p.ins)], cout[b:b + len(p.out_shape)], csem[s:s + len(p.sems)])

    res = pl.pallas_call(
        body, name=name, in_specs=[_HBM] * ki, out_specs=[_HBM] * ko, out_shape=[s for p in pieces for s in p.out_shape],
        scratch_shapes=[pltpu.SemaphoreType.DMA((n,)) for p in pieces for n in p.sems], input_output_aliases=aliases,
    )(*[a for p in pieces for a in p.ins])
    for p, (a, b, s) in zip(pieces, spans):
        p.done(res[b:b + len(p.out_shape)])


def _rows_half(ref, which, rows):
    return ref.at[pl.ds(pl.multiple_of(which * rows, 8), rows)]


def gather_ici_piece(bufs, done):
    n = len(bufs)

    def copies(cout):
        x, y, c, chips = _place()
        out = []
        for j, (cx, cy) in enumerate(chips):
            for w in range(n):
                half = bufs[w].shape[1] // 2
                out.append((j, w, _rows_half(cout[w].at[2 * x + y], c, half), _rows_half(cout[w].at[2 * cx + cy], c, half),
                            (cx, cy, c)))
        return out

    def start(cin, cout, sems):
        for j, w, mine, _, to in copies(cout):
            pltpu.make_async_remote_copy(src_ref=mine, dst_ref=mine, send_sem=sems[0].at[n * j + w],
                                         recv_sem=sems[1].at[n * j + w], device_id=to, device_id_type=MESH).start()

    def finish(cin, cout, sems):
        for j, w, mine, landed, to in copies(cout):
            pltpu.make_async_remote_copy(src_ref=mine, dst_ref=landed, send_sem=sems[0].at[n * j + w],
                                         recv_sem=sems[1].at[n * j + w], device_id=to, device_id_type=MESH).wait()

    return Comm(bufs, [SDS(b.shape, b.dtype) for b in bufs], {w: w for w in range(n)}, [3 * n, 3 * n], start, finish, done)


def gather_full_piece(bufs, done):
    n = len(bufs)

    def copies(cout):
        x, y, c, chips = _place()
        return [(n * j + w, cout[w].at[2 * x + y], cout[w].at[2 * cx + cy], (cx, cy, c))
                for j, (cx, cy) in enumerate(chips) for w in range(n)]

    def start(cin, cout, sems):
        for k, mine, _, to in copies(cout):
            pltpu.make_async_remote_copy(src_ref=mine, dst_ref=mine, send_sem=sems[0].at[k], recv_sem=sems[1].at[k],
                                         device_id=to, device_id_type=MESH).start()

    def finish(cin, cout, sems):
        for k, mine, landed, to in copies(cout):
            pltpu.make_async_remote_copy(src_ref=mine, dst_ref=landed, send_sem=sems[0].at[k], recv_sem=sems[1].at[k],
                                         device_id=to, device_id_type=MESH).wait()

    return Comm(bufs, [SDS(b.shape, b.dtype) for b in bufs], {w: w for w in range(n)}, [3 * n, 3 * n], start, finish, done)


def gather_d2d_piece(bufs, done):
    n = len(bufs)

    def copies(cout):
        x, y, c, chips = _place()
        out = []
        for j, (cx, cy) in enumerate(chips):
            for w in range(n):
                half = bufs[w].shape[1] // 2
                slot = cout[w].at[2 * cx + cy]
                out.append((n * j + w, _rows_half(slot, c, half), _rows_half(slot, 1 - c, half), (x, y, 1 - c)))
        return out

    def start(cin, cout, sems):
        for k, mine, _, to in copies(cout):
            pltpu.make_async_remote_copy(src_ref=mine, dst_ref=mine, send_sem=sems[0].at[k], recv_sem=sems[1].at[k],
                                         device_id=to, device_id_type=MESH).start()

    def finish(cin, cout, sems):
        for k, mine, theirs, to in copies(cout):
            pltpu.make_async_remote_copy(src_ref=mine, dst_ref=theirs, send_sem=sems[0].at[k], recv_sem=sems[1].at[k],
                                         device_id=to, device_id_type=MESH).wait()

    return Comm(bufs, [SDS(b.shape, b.dtype) for b in bufs], {w: w for w in range(n)}, [3 * n, 3 * n], start, finish, done)


def pair_piece(parts, done):
    n = len(parts)

    def copies(cin, cout):
        x, y, c, _ = _place()
        out = []
        for w in range(n):
            half = parts[w].shape[1] // 2
            out.append((w, cin[w].at[:, pl.ds(pl.multiple_of((1 - c) * half, 8), half), :], cout[w], (x, y, 1 - c)))
        return out

    def start(cin, cout, sems):
        for w, src, dst, to in copies(cin, cout):
            pltpu.make_async_remote_copy(src_ref=src, dst_ref=dst, send_sem=sems[0].at[w], recv_sem=sems[1].at[w],
                                         device_id=to, device_id_type=MESH).start()

    def finish(cin, cout, sems):
        for w, src, dst, to in copies(cin, cout):
            pltpu.make_async_remote_copy(src_ref=src, dst_ref=dst, send_sem=sems[0].at[w], recv_sem=sems[1].at[w],
                                         device_id=to, device_id_type=MESH).wait()

    return Comm(parts, [SDS((N_SHARD, a.shape[1] // 2, a.shape[2]), f32) for a in parts], {}, [n, n], start, finish, done)


def shard_piece(sums16, done):
    n = len(sums16)

    def copies(cin, cout):
        x, y, c, chips = _place()
        return [(n * j + w, cin[w].at[2 * cx + cy], cout[w].at[j], (cx, cy, c))
                for j, (cx, cy) in enumerate(chips) for w in range(n)]

    def start(cin, cout, sems):
        for k, src, dst, to in copies(cin, cout):
            pltpu.make_async_remote_copy(src_ref=src, dst_ref=dst, send_sem=sems[0].at[k], recv_sem=sems[1].at[k],
                                         device_id=to, device_id_type=MESH).start()

    def finish(cin, cout, sems):
        for k, src, dst, to in copies(cin, cout):
            pltpu.make_async_remote_copy(src_ref=src, dst_ref=dst, send_sem=sems[0].at[k], recv_sem=sems[1].at[k],
                                         device_id=to, device_id_type=MESH).wait()

    return Comm(sums16, [SDS((3,) + a.shape[1:], bf16) for a in sums16], {}, [3 * n, 3 * n], start, finish, done)


def place_shard(a, layer, idx, dtype):
    _, r, cdim = a.shape
    tr = min(r, 512)

    def body(idx_ref, a_ref, o_ref):
        o_ref[0] = a_ref[0].astype(dtype)

    return pl.pallas_call(
        body, name="place_shard",
        grid_spec=pltpu.PrefetchScalarGridSpec(
            num_scalar_prefetch=1, grid=(r // tr,),
            in_specs=[pl.BlockSpec((1, tr, cdim), lambda i, idx_ref: (layer, i, 0))],
            out_specs=pl.BlockSpec((1, tr, cdim), lambda i, idx_ref: (idx_ref[1], i, 0))),
        out_shape=SDS((N_SHARD, r, cdim), dtype),
        compiler_params=_params(("arbitrary",)),
    )(idx, a)


def place_shards(items, idx, pieces):
    steps = 4
    n = len(items)

    def body(idx_ref, *refs):
        for k in range(n):
            refs[n + k][0] = refs[k][0].astype(bf16)

    in_specs, out_specs, out_shape = [], [], []
    for a, layer in items:
        _, r, cdim = a.shape
        in_specs.append(pl.BlockSpec((1, r // steps, cdim), lambda i, idx_ref, layer=layer: (layer, i, 0)))
        out_specs.append(pl.BlockSpec((1, r // steps, cdim), lambda i, idx_ref: (idx_ref[1], i, 0)))
        out_shape.append(SDS((N_SHARD, r, cdim), bf16))
    return _pcall(body, pieces, name="place_shards", grid=(steps,), in_specs=in_specs, out_specs=out_specs,
                  out_shape=out_shape, operands=[a for a, _ in items], sem=("arbitrary",), prefetch=idx)


_KEYS = ("w_in", "w_out", "w_up", "w_dn")


class MeshWeights:
    def __init__(self, w_in, w_out, w_up, w_down, conv_dw_w, lru_conv_w, idx):
        self.idx = idx
        src = dict(w_in=w_in, w_out=w_out, w_up=w_up, w_dn=w_down)
        self.conv = {(n, l): place_shard(a, l, idx, f32)
                     for n, a in (("cw", conv_dw_w), ("lw", lru_conv_w)) for l in range(DEPTH)}
        self.cache, self.parts, self.sum32, self.sum16, self.got = {}, {}, {}, {}, {}
        first, small = [("w_in", 0)], list(self.conv)
        rest = [(n, l) for n in _KEYS for l in range(DEPTH) if (n, l) not in first]

        def store_conv(outs):
            self.conv.update(zip(small, outs))

        self.buf = {k: place_shard(src[k[0]], k[1], idx, bf16) for k in first}
        placed = place_shards(
            [(src[n], l) for n, l in rest], idx,
            [self._gather(gather_ici_piece, first), gather_full_piece([self.conv[k] for k in small], store_conv)])
        self.buf.update(zip(rest, placed))
        standalone([self._gather(gather_d2d_piece, first)], "gather_first_d2d")

    def _gather(self, piece, keys):
        def done(outs):
            self.buf.update(zip(keys, outs))
        return piece([self.buf[k] for k in keys], done)

    def _pair(self, keys):
        def done(outs):
            for k, recv in zip(keys, outs):
                self.sum32[k], self.sum16[k] = chip_sum(self.parts[k], recv, self.idx)
        return pair_piece([self.parts[k] for k in keys], done)

    def _shard(self, keys):
        def done(outs):
            self.got.update(zip(keys, outs))
        return shard_piece([self.sum16[k] for k in keys], done)

    def conv_weights(self):
        out = []
        for n in ("cw", "lw"):
            a = jnp.stack([self.conv[(n, l)] for l in range(DEPTH)])
            out.append(a.transpose(0, 2, 1, 3).reshape(DEPTH, a.shape[2], N_SHARD * a.shape[3]))
        return out

    def weight(self, name, l):
        if (name, l) not in self.cache:
            b = self.buf[(name, l)]
            if name in ("w_in", "w_out"):
                b = b.reshape(-1, D_MODEL)
            self.cache[(name, l)] = b
        return self.cache[(name, l)]

    def host(self, point, l):
        ici, d2d = gather_ici_piece, gather_d2d_piece
        rest1 = [("w_out", 1), ("w_up", 1), ("w_dn", 1)]
        plan = {
            ("inproj_fwd", 0): lambda: [self._gather(ici, [("w_out", 0)])],
            ("attn_fwd", 0): lambda: [self._gather(ici, [("w_up", 0)]), self._gather(d2d, [("w_out", 0)])],
            ("branch_fwd", 0): lambda: [self._gather(ici, [("w_dn", 0)]), self._gather(d2d, [("w_up", 0)])],
            ("outproj_fwd", 0): lambda: [self._gather(d2d, [("w_dn", 0)]), self._gather(ici, [("w_in", 1)])],
            ("mlp_fwd", 0): lambda: [self._gather(ici, rest1), self._gather(d2d, [("w_in", 1)])],
            ("attn_fwd", 1): lambda: [self._gather(d2d, rest1)],
            ("attn_bwd", 1): lambda: [self._pair([("w_up", 1), ("w_dn", 1)])],
            ("mlp_bwd_act", 0): lambda: [self._shard([("w_up", 1), ("w_dn", 1)])],
            ("mlp_bwd_w", 0): lambda: [self._pair([("w_in", 1), ("w_out", 1)])],
            ("attn_bwd", 0): lambda: [self._shard([("w_in", 1), ("w_out", 1)]),
                                      self._pair([("w_up", 0), ("w_dn", 0), ("w_out", 0)])],
            ("branch_bwd_a", 0): lambda: [self._shard([("w_up", 0)])],
            ("branch_bwd_b", 0): lambda: [self._shard([("w_dn", 0), ("w_out", 0)])],
        }
        make = plan.get((point, l))
        return make() if make else None

    def grad(self, name, l, g):
        if name in ("w_in", "w_out"):
            g = g.reshape(N_SHARD, -1, D_MODEL)
        self.parts[(name, l)] = g

    def big_grads(self):
        return None

    def finish(self, small_vec):
        last = [("w_in", 0)]
        small = {}

        def keep(outs):
            small["buf"] = outs[0]

        standalone([self._pair(last), small_first_piece(small_vec, keep)], "pair_last")
        standalone([self._shard(last), small_second_piece(small["buf"], keep)], "shard_last")
        return self._totals(), small["buf"]

    def _totals(self):
        tots = []
        for n in _KEYS:
            t = None
            for l in reversed(range(DEPTH)):
                t = shard_sum(self.sum32[(n, l)], self.got[(n, l)], self.idx, l, t)
            tots.append(t)
        return tots


def chip_sum(g, recv, idx):
    _, r, cdim = g.shape
    half = r // 2
    tr = min(half, 512)
    nh = half // tr

    def body(idx_ref, g_ref, r_ref, o32_ref, o16_ref):
        tot = g_ref[0] + r_ref[0]
        o16_ref[0] = tot.astype(bf16)

        @pl.when(pl.program_id(0) == N_SHARD - 1)
        def _():
            o32_ref[...] = tot

    def slab(k, idx_ref):
        return lax.rem(idx_ref[1] + 1 + k, N_SHARD)

    def o16_map(k, i, idx_ref):
        return (slab(k, idx_ref), i, 0)

    return pl.pallas_call(
        body, name="chip_sum",
        grid_spec=pltpu.PrefetchScalarGridSpec(
            num_scalar_prefetch=1, grid=(N_SHARD, nh),
            in_specs=[pl.BlockSpec((1, tr, cdim), lambda k, i, idx_ref: (slab(k, idx_ref), idx_ref[0] * nh + i, 0)),
                      pl.BlockSpec((1, tr, cdim), lambda k, i, idx_ref: (slab(k, idx_ref), i, 0))],
            out_specs=[pl.BlockSpec((tr, cdim), lambda k, i, idx_ref: (jnp.where(k == N_SHARD - 1, i, 0), 0)),
                       pl.BlockSpec((1, tr, cdim), o16_map)]),
        out_shape=[SDS((half, cdim), f32), SDS((N_SHARD, half, cdim), bf16)],
        compiler_params=_params(("arbitrary", "arbitrary")),
    )(idx, g, recv)


def shard_sum(sum32, got16, idx, layer, prev):
    half, cdim = sum32.shape
    tr = min(half, 512)

    def body(idx_ref, a_ref, r0_ref, r1_ref, r2_ref, *rest):
        o_ref = rest[-1]
        o_ref[0, 0] = ((a_ref[...] + r0_ref[0].astype(f32)) + r1_ref[0].astype(f32)) + r2_ref[0].astype(f32)

    def rel(j):
        return pl.BlockSpec((1, tr, cdim), lambda i, idx_ref: (j, i, 0))

    in_specs = [pl.BlockSpec((tr, cdim), lambda i, idx_ref: (i, 0)), rel(0), rel(1), rel(2)]
    operands = [idx, sum32, got16, got16, got16]
    aliases = {}
    if prev is not None:
        in_specs.append(_HBM)
        operands.append(prev)
        aliases = {5: 0}
    return pl.pallas_call(
        body, name="shard_sum",
        grid_spec=pltpu.PrefetchScalarGridSpec(
            num_scalar_prefetch=1, grid=(half // tr,), in_specs=in_specs,
            out_specs=pl.BlockSpec((1, 1, tr, cdim), lambda i, idx_ref: (layer, idx_ref[0], i, 0))),
        out_shape=SDS((DEPTH, 2, half, cdim), f32), input_output_aliases=aliases,
        compiler_params=_params(("arbitrary",)),
    )(*operands)


def halves_exchange(tots):
    nw = len(tots)

    def body(*refs):
        bufs = refs[nw:2 * nw]
        send_sem, recv_sem = refs[2 * nw:]
        x, y, c, _ = _place()

        def copy(w, l, half_idx):
            return pltpu.make_async_remote_copy(
                src_ref=bufs[w].at[l, half_idx], dst_ref=bufs[w].at[l, half_idx], send_sem=send_sem.at[DEPTH * w + l],
                recv_sem=recv_sem.at[DEPTH * w + l], device_id=(x, y, 1 - c), device_id_type=MESH)

        sends = [copy(w, l, c) for w in range(nw) for l in range(DEPTH)]
        for cp in sends:
            cp.start()
        for w in range(nw):
            for l in range(DEPTH):
                copy(w, l, 1 - c).wait_recv()
        for cp in sends:
            cp.wait_send()

    return pl.pallas_call(
        body, name="halves_exchange", in_specs=[_HBM] * nw, out_specs=[_HBM] * nw,
        out_shape=[SDS(a.shape, f32) for a in tots], input_output_aliases={w: w for w in range(nw)},
        scratch_shapes=[pltpu.SemaphoreType.DMA((DEPTH * nw,)), pltpu.SemaphoreType.DMA((DEPTH * nw,))],
    )(*tots)


N_DEV = 8


def _dev_index(px, py, pc):
    return 4 * px + 2 * py + pc


def small_first_piece(vec, done):
    def copies(cin, cout):
        x, y, c, chips = _place()
        mine = cout[0].at[_dev_index(x, y, c)]
        peers = [(x, y, 1 - c)] + [(cx, cy, c) for cx, cy in chips]
        return mine, [(k, cout[0].at[_dev_index(*p)], p) for k, p in enumerate(peers)]

    def start(cin, cout, sems):
        mine, peers = copies(cin, cout)
        pltpu.make_async_copy(cin[0], mine, sems[2].at[0]).start()
        for k, _, to in peers:
            pltpu.make_async_remote_copy(src_ref=cin[0], dst_ref=mine, send_sem=sems[0].at[k], recv_sem=sems[1].at[k],
                                         device_id=to, device_id_type=MESH).start()

    def finish(cin, cout, sems):
        mine, peers = copies(cin, cout)
        for k, theirs, to in peers:
            pltpu.make_async_remote_copy(src_ref=cin[0], dst_ref=theirs, send_sem=sems[0].at[k], recv_sem=sems[1].at[k],
                                         device_id=to, device_id_type=MESH).wait()
        pltpu.make_async_copy(cin[0], mine, sems[2].at[0]).wait()

    return Comm([vec], [SDS((N_DEV,) + vec.shape, f32)], {}, [4, 4, 1], start, finish, done)


def small_second_piece(buf, done):
    def copies(cout):
        x, y, c, chips = _place()
        return [(j, cout[0].at[_dev_index(cx, cy, c)], cout[0].at[_dev_index(cx, cy, 1 - c)], (x, y, 1 - c))
                for j, (cx, cy) in enumerate(chips)]

    def start(cin, cout, sems):
        for j, mine, _, to in copies(cout):
            pltpu.make_async_remote_copy(src_ref=mine, dst_ref=mine, send_sem=sems[0].at[j], recv_sem=sems[1].at[j],
                                         device_id=to, device_id_type=MESH).start()

    def finish(cin, cout, sems):
        for j, mine, theirs, to in copies(cout):
            pltpu.make_async_remote_copy(src_ref=mine, dst_ref=theirs, send_sem=sems[0].at[j], recv_sem=sems[1].at[j],
                                         device_id=to, device_id_type=MESH).wait()

    return Comm([buf], [SDS(buf.shape, f32)], {0: 0}, [3, 3], start, finish, done)


def small_sum(buf):
    def body(b_ref, o_ref):
        acc = b_ref[0]
        for d in range(1, N_DEV):
            acc = acc + b_ref[d]
        o_ref[...] = acc

    vm = pl.BlockSpec(memory_space=pltpu.VMEM)
    return pl.pallas_call(body, name="small_sum", in_specs=[vm], out_specs=vm, out_shape=SDS(buf.shape[1:], f32))(buf)


def _adamw_math(w, g, m, v):
    m = ADAM_B1 * m + (1.0 - ADAM_B1) * g
    v = ADAM_B2 * v + (1.0 - ADAM_B2) * (g * g)
    m_hat = m / (1.0 - ADAM_B1 ** ADAM_STEP)
    v_hat = v / (1.0 - ADAM_B2 ** ADAM_STEP)
    delta = -ADAM_LR * (m_hat / (jnp.sqrt(v_hat) + ADAM_EPS) + ADAM_WD * w)
    return delta, m, v


def adamw_big(w, g, m, v):
    _, r, cdim = w.shape
    tr = _row_tile(r, 512)

    def body(w_ref, g_ref, m_ref, v_ref, go_ref, d_ref, mo_ref, vo_ref):
        g = g_ref[...]
        d, mm, vv = _adamw_math(w_ref[...], g, m_ref[...], v_ref[...])
        go_ref[...] = g
        d_ref[...] = d
        mo_ref[...] = mm
        vo_ref[...] = vv

    blk = pl.BlockSpec((1, tr, cdim), lambda l, i: (l, i, 0))
    return pl.pallas_call(
        body, name="adamw_big", grid=(DEPTH, r // tr), in_specs=[blk] * 4, out_specs=[blk] * 4,
        out_shape=[SDS(w.shape, f32)] * 4, compiler_params=_params(("parallel", "parallel"), VMEM_LIMIT),
    )(w, g, m, v)


def adamw_small(ws, gs, ms, vs):
    n = len(ws)

    def body(*refs):
        w_r, g_r, m_r, v_r = refs[:n], refs[n:2 * n], refs[2 * n:3 * n], refs[3 * n:4 * n]
        d_o, m_o, v_o = refs[4 * n:5 * n], refs[5 * n:6 * n], refs[6 * n:7 * n]
        for k in range(n):
            d, mm, vv = _adamw_math(w_r[k][...], g_r[k][...], m_r[k][...], v_r[k][...])
            d_o[k][...] = d
            m_o[k][...] = mm
            v_o[k][...] = vv

    vm = pl.BlockSpec(memory_space=pltpu.VMEM)
    shapes = [SDS(a.shape, f32) for a in ws]
    outs = pl.pallas_call(
        body, name="adamw_small", in_specs=[vm] * (4 * n), out_specs=[vm] * (3 * n), out_shape=shapes * 3,
    )(*ws, *gs, *ms, *vs)
    return outs[:n], outs[n:2 * n], outs[2 * n:]


_BIG = ("w_in", "w_out", "w_up", "w_down")
_WEIGHTS = ("norm1", "w_in", "attn_sinks", "conv_dw_w", "conv_dw_b", "conv_ln_g", "conv_ln_b", "lru_conv_w", "lru_conv_b",
            "lru_wa", "lru_ba", "lru_wx", "lru_bx", "lru_lambda", "mix_norm", "w_out", "norm2", "w_up", "w_down", "final_norm")
_SMALL = tuple(n for n in _WEIGHTS if n not in _BIG)
_SMALL_FULL_SHAPE = dict(
    norm1=(DEPTH, D_MODEL), attn_sinks=(DEPTH, N_HEADS), conv_dw_w=(DEPTH, CONV_K, CONV_W), conv_dw_b=(DEPTH, CONV_W),
    conv_ln_g=(DEPTH, CONV_W), conv_ln_b=(DEPTH, CONV_W), lru_conv_w=(DEPTH, LRU_K, LRU_W), lru_conv_b=(DEPTH, LRU_W),
    lru_wa=(DEPTH, LRU_HEADS, 64, 64), lru_ba=(DEPTH, LRU_HEADS, 64), lru_wx=(DEPTH, LRU_HEADS, 64, 64),
    lru_bx=(DEPTH, LRU_HEADS, 64), lru_lambda=(DEPTH, LRU_W), mix_norm=(DEPTH, D_MODEL), norm2=(DEPTH, D_MODEL),
    final_norm=(D_MODEL,))
_CHANNEL_SHARDED = ("conv_dw_w", "lru_conv_w")


def _pad_lanes(n):
    return -(-n // LANES) * LANES


def _pack(named):
    flat = []
    for a in named:
        a = a.reshape(-1)
        flat.append(jnp.pad(a, (0, _pad_lanes(a.shape[0]) - a.shape[0])))
    v = jnp.concatenate(flat)
    rows = -(-v.shape[0] // (8 * LANES)) * 8
    return jnp.pad(v, (0, rows * LANES - v.shape[0])).reshape(rows, LANES)


def _unpack(vec, shapes):
    flat = vec.reshape(-1)
    out, off = [], 0
    for shp in shapes:
        n = math.prod(shp)
        out.append(flat[off:off + n].reshape(shp))
        off += _pad_lanes(n)
    return out


def _as2d(a):
    return a.reshape(-1, a.shape[-1]) if a.ndim > 1 else a.reshape(1, -1)


def kernel(x, norm1, w_in, attn_sinks, conv_dw_w, conv_dw_b, conv_ln_g, conv_ln_b, lru_conv_w, lru_conv_b, lru_wa, lru_ba, lru_wx, lru_bx, lru_lambda, mix_norm, w_out, norm2, w_up, w_down, final_norm, loss_target, m_norm1, m_w_in, m_attn_sinks, m_conv_dw_w, m_conv_dw_b, m_conv_ln_g, m_conv_ln_b, m_lru_conv_w, m_lru_conv_b, m_lru_wa, m_lru_ba, m_lru_wx, m_lru_bx, m_lru_lambda, m_mix_norm, m_w_out, m_norm2, m_w_up, m_w_down, m_final_norm, v_norm1, v_w_in, v_attn_sinks, v_conv_dw_w, v_conv_dw_b, v_conv_ln_g, v_conv_ln_b, v_lru_conv_w, v_lru_conv_b, v_lru_wa, v_lru_ba, v_lru_wx, v_lru_bx, v_lru_lambda, v_mix_norm, v_w_out, v_norm2, v_w_up, v_w_down, v_final_norm):
    wts = dict(norm1=norm1, w_in=w_in, attn_sinks=attn_sinks, conv_dw_w=conv_dw_w, conv_dw_b=conv_dw_b, conv_ln_g=conv_ln_g,
               conv_ln_b=conv_ln_b, lru_conv_w=lru_conv_w, lru_conv_b=lru_conv_b, lru_wa=lru_wa, lru_ba=lru_ba, lru_wx=lru_wx,
               lru_bx=lru_bx, lru_lambda=lru_lambda, mix_norm=mix_norm, w_out=w_out, norm2=norm2, w_up=w_up, w_down=w_down,
               final_norm=final_norm)
    mom = dict(norm1=m_norm1, w_in=m_w_in, attn_sinks=m_attn_sinks, conv_dw_w=m_conv_dw_w, conv_dw_b=m_conv_dw_b,
               conv_ln_g=m_conv_ln_g, conv_ln_b=m_conv_ln_b, lru_conv_w=m_lru_conv_w, lru_conv_b=m_lru_conv_b, lru_wa=m_lru_wa,
               lru_ba=m_lru_ba, lru_wx=m_lru_wx, lru_bx=m_lru_bx, lru_lambda=m_lru_lambda, mix_norm=m_mix_norm, w_out=m_w_out,
               norm2=m_norm2, w_up=m_w_up, w_down=m_w_down, final_norm=m_final_norm)
    var = dict(norm1=v_norm1, w_in=v_w_in, attn_sinks=v_attn_sinks, conv_dw_w=v_conv_dw_w, conv_dw_b=v_conv_dw_b,
               conv_ln_g=v_conv_ln_g, conv_ln_b=v_conv_ln_b, lru_conv_w=v_lru_conv_w, lru_conv_b=v_lru_conv_b, lru_wa=v_lru_wa,
               lru_ba=v_lru_ba, lru_wx=v_lru_wx, lru_bx=v_lru_bx, lru_lambda=v_lru_lambda, mix_norm=v_mix_norm, w_out=v_w_out,
               norm2=v_norm2, w_up=v_w_up, w_down=v_w_down, final_norm=v_final_norm)

    c_idx = lax.axis_index("c").astype(jnp.int32)
    s_idx = (2 * lax.axis_index("x") + lax.axis_index("y")).astype(jnp.int32)
    idx = jnp.stack([c_idx, s_idx])

    for d in (wts, mom, var):
        d["w_in"] = d["w_in"].transpose(0, 2, 1)

    cs = MeshWeights(wts["w_in"], w_out, w_up, w_down, conv_dw_w, lru_conv_w, idx)
    sp = {n: wts[n] for n in _SMALL}
    sp["conv_dw_w"], sp["lru_conv_w"] = cs.conv_weights()
    loss_blk, grad_x, _, small_g, d_gf = train_local(x[0], loss_target[0], sp, cs)

    stacked = [jnp.stack([small_g[l][n] for l in range(DEPTH)]) for n in _SMALL if n != "final_norm"]
    tots, gathered = cs.finish(_pack(stacked + [d_gf, loss_blk[0, 0:1]]))
    grads_big = {n: a.reshape(wts[n].shape) for n, a in zip(_BIG, halves_exchange(tots))}
    summed = small_sum(gathered)
    names = [n for n in _SMALL if n != "final_norm"] + ["final_norm"]
    unpacked = _unpack(summed, [_SMALL_FULL_SHAPE[n] for n in names] + [(1,)])
    loss = unpacked[-1][0]
    grads = dict(zip(names, unpacked[:-1]))
    for n in _CHANNEL_SHARDED:
        width = wts[n].shape[-1]
        grads[n] = lax.dynamic_slice_in_dim(grads[n], s_idx * width, width, axis=2)
    grads.update(grads_big)

    delta, new_m, new_v = {}, {}, {}
    for n in _BIG:
        grads[n], delta[n], new_m[n], new_v[n] = adamw_big(wts[n], grads[n], mom[n], var[n])
    for d in (grads, delta, new_m, new_v):
        d["w_in"] = d["w_in"].transpose(0, 2, 1)
    sm = list(_SMALL)
    d_s, m_s, v_s = adamw_small([_as2d(wts[n]) for n in sm], [_as2d(grads[n]) for n in sm],
                                [_as2d(mom[n]) for n in sm], [_as2d(var[n]) for n in sm])
    for k, n in enumerate(sm):
        delta[n], new_m[n], new_v[n] = (a.reshape(wts[n].shape) for a in (d_s[k], m_s[k], v_s[k]))

    return (loss, grad_x[None], *[grads[n] for n in _WEIGHTS], *[delta[n] for n in _WEIGHTS],
            *[new_m[n] for n in _WEIGHTS], *[new_v[n] for n in _WEIGHTS])
```

```python
import functools
import math

import jax
import jax.numpy as jnp
from jax import lax
from jax.experimental import pallas as pl
from jax.experimental.pallas import tpu as pltpu

f32 = jnp.float32
bf16 = jnp.bfloat16
SDS = jax.ShapeDtypeStruct

D_MODEL = 1024
DEPTH = 2
ATTN_W = 512
KV_W = 128
HEAD_DIM = 64
N_HEADS = 8
BLOCK = 128
CONV_W = 256
CONV_K = 31
LRU_W = 256
LRU_K = 4
LRU_HEADS = 4
LRU_C = 8.0
IN_W = 1792
D_FF = 4096
N_SHARD = 4
FF_CHUNK = D_FF // N_SHARD
RMS_EPS = 1e-6
LN_EPS = 1e-5
MASK_VALUE = -1e30
HALO = 32
LANES = 128
VMEM_LIMIT = 56 * 1024 * 1024

ADAM_LR = 0.001
ADAM_B1 = 0.9
ADAM_B2 = 0.999
ADAM_EPS = 1e-08
ADAM_WD = 0.01
ADAM_STEP = 10

MESH = pl.DeviceIdType.MESH


def _dot(a, b):
    return jnp.dot(a, b, preferred_element_type=f32)


def _dot_nt(a, b):
    return lax.dot_general(a, b, (((1,), (1,)), ((), ())), preferred_element_type=f32)


def _dot_tn(a, b):
    return lax.dot_general(a, b, (((0,), (0,)), ((), ())), preferred_element_type=f32)


def _rms_fwd(x, g):
    r = lax.rsqrt(jnp.mean(x * x, axis=-1, keepdims=True) + RMS_EPS)
    return x * r * g, r


def _rms_bwd(dy, x, r, g):
    t = dy * g
    dx = r * t - x * (r * r * r) * jnp.mean(t * x, axis=-1, keepdims=True)
    dg = jnp.sum(dy * x * r, axis=0, keepdims=True)
    return dx, dg


def _sigmoid(x):
    return jax.nn.sigmoid(x)


_GELU_K = math.sqrt(2.0 / math.pi)


def _gelu(x):
    t = jnp.tanh(_GELU_K * (x + 0.044715 * x * x * x))
    return 0.5 * x * (1.0 + t), t


def _gelu_grad(x, t):
    return 0.5 * (1.0 + t) + 0.5 * x * (1.0 - t * t) * _GELU_K * (1.0 + 3.0 * 0.044715 * x * x)


def _log1p(x):
    return jnp.where(x < 1e-4, x - 0.5 * x * x, jnp.log(1.0 + x))


def _softplus(x):
    return jnp.maximum(x, 0.0) + _log1p(jnp.exp(-jnp.abs(x)))


def _sublane_rolls(x, count, forward):
    n = x.shape[0]
    return [x if b == 0 else pltpu.roll(x, b if forward else n - b, 0) for b in range(count)]


def _conv_taps(xpad, w, k_width):
    t_rows = xpad.shape[0] - HALO
    rolled = _sublane_rolls(xpad, min(k_width, 8), forward=True)
    acc = None
    for k in range(k_width):
        hi, lo = divmod((k_width - 1) - k, 8)
        term = rolled[lo][HALO - 8 * hi:HALO - 8 * hi + t_rows] * w[k:k + 1, :]
        acc = term if acc is None else acc + term
    return acc


def _conv_taps_bwd(dpad, upad, w, k_width, t_rows):
    n_lo = min(k_width, 8)
    d_rolled = _sublane_rolls(dpad, n_lo, forward=False)
    u_rolled = _sublane_rolls(upad, n_lo, forward=True)
    d_in = None
    dw_rows = []
    d_out = dpad[:t_rows]
    for k in range(k_width):
        hi, lo = divmod((k_width - 1) - k, 8)
        term = d_rolled[lo][8 * hi:8 * hi + t_rows] * w[k:k + 1, :]
        d_in = term if d_in is None else d_in + term
        us = u_rolled[lo][HALO - 8 * hi:HALO - 8 * hi + t_rows]
        dw_rows.append(jnp.sum(d_out * us, axis=0, keepdims=True))
    return d_in, dw_rows


SUBLANES = 8


def _scan_fwd(a, b, h0):
    t_rows = a.shape[0]
    sub = jnp.bitwise_and(lax.broadcasted_iota(jnp.int32, a.shape, 0), SUBLANES - 1)
    for d in (1, 2, 4):
        a_sh = jnp.where(sub < d, 1.0, pltpu.roll(a, d, 0))
        b_sh = jnp.where(sub < d, 0.0, pltpu.roll(b, d, 0))
        b = a * b_sh + b
        a = a * a_sh
    out, carry = [], h0
    for g in range(t_rows // SUBLANES):
        rows = slice(g * SUBLANES, (g + 1) * SUBLANES)
        hg = a[rows] * carry + b[rows]
        out.append(hg)
        carry = hg[SUBLANES - 1:SUBLANES]
    return jnp.concatenate(out, axis=0)


def _scan_bwd(a, b, l_end):
    t_rows = a.shape[0]
    sub = jnp.bitwise_and(lax.broadcasted_iota(jnp.int32, a.shape, 0), SUBLANES - 1)
    for d in (1, 2, 4):
        a_sh = jnp.where(sub >= SUBLANES - d, 1.0, pltpu.roll(a, t_rows - d, 0))
        b_sh = jnp.where(sub >= SUBLANES - d, 0.0, pltpu.roll(b, t_rows - d, 0))
        b = b + a * b_sh
        a = a * a_sh
    out, carry = [], l_end
    for g in reversed(range(t_rows // SUBLANES)):
        rows = slice(g * SUBLANES, (g + 1) * SUBLANES)
        lg = b[rows] + a[rows] * carry
        out.append(lg)
        carry = lg[0:1]
    return jnp.concatenate(out[::-1], axis=0)


def _full(shape, single=False):
    nd = len(shape)
    if single:
        return pl.BlockSpec(shape, lambda *_: (0,) * nd, pipeline_mode=pl.Buffered(1))
    return pl.BlockSpec(shape, lambda *_: (0,) * nd)


def _params(sem, vmem=None):
    return pltpu.CompilerParams(dimension_semantics=sem, vmem_limit_bytes=vmem)


def _tile(s):
    return min(512, s)


def _row_tile(rows, cap):
    return next(t for t in range(min(cap, rows) // 8 * 8, 0, -8) if rows % t == 0)


def inproj_fwd(h, g1, w_in_t, pieces=None):
    s = h.shape[0]
    tm = _tile(s)

    def body(h_ref, g_ref, w_ref, z_ref):
        hn, _ = _rms_fwd(h_ref[...], g_ref[...])
        z_ref[...] = _dot_nt(hn.astype(bf16), w_ref[...]).astype(bf16)

    return _pcall(
        body, pieces, name="inproj_fwd", grid=(s // tm,),
        in_specs=[pl.BlockSpec((tm, D_MODEL), lambda i: (i, 0)), _full((1, D_MODEL)), _full((IN_W, D_MODEL))],
        out_specs=[pl.BlockSpec((tm, IN_W), lambda i: (i, 0))],
        out_shape=[SDS((s, IN_W), bf16)],
        operands=[h, g1, w_in_t], sem=("parallel",), vmem=VMEM_LIMIT)[0]


ATT_ROWS = N_HEADS * BLOCK
ATT_BLOCKS_PER_STEP = 8


def _attn_bias():
    qi = jnp.arange(BLOCK)[None, :]
    key = jnp.arange(2 * BLOCK)[:, None]
    band = (key > qi) & (key <= qi + BLOCK)
    first = band & (key >= BLOCK)
    mask = jnp.where(jnp.stack([first, band]), 0.0, MASK_VALUE).astype(f32)
    return jnp.tile(mask, (1, 1, N_HEADS // 2))


def _attn_band(kvc, kvp):
    kb = jnp.concatenate([kvp[:, :KV_W], kvc[:, :KV_W]], axis=0)
    vb = jnp.concatenate([kvp[:, KV_W:], kvc[:, KV_W:]], axis=0)
    lane = lax.broadcasted_iota(jnp.int32, kb.shape, 1)
    kb_sw = pltpu.roll(kb, HEAD_DIM, 1)
    vb_sw = pltpu.roll(vb, HEAD_DIM, 1)
    kx = [jnp.where(lane < HEAD_DIM, kb, kb_sw), jnp.where(lane >= HEAD_DIM, kb, kb_sw)]
    vx = [jnp.where(lane < HEAD_DIM, vb, vb_sw), jnp.where(lane >= HEAD_DIM, vb, vb_sw)]
    return kx, vx


def _stack_heads(x, mlo):
    zero = jnp.zeros((BLOCK, LANES), x.dtype)
    out = []
    for hk in range(2):
        parts = []
        for j in (2 * hk, 2 * hk + 1):
            xj = x[:, j * LANES:(j + 1) * LANES]
            parts += [jnp.where(mlo, xj, zero), jnp.where(mlo, zero, xj)]
        out.append(jnp.concatenate(parts, axis=0))
    return out


def _unstack_heads(y, mlo):
    cols = []
    for hk in range(2):
        for t in range(2):
            base = 2 * t * BLOCK
            cols.append(jnp.where(mlo, y[hk][base:base + BLOCK], y[hk][base + BLOCK:base + 2 * BLOCK]))
    return jnp.concatenate(cols, axis=1)


def _attn_probs(q4, kx, bias_t, sink_row):
    out = []
    half = ATT_ROWS // 2
    for hk in range(2):
        s = _dot_nt(kx[hk], q4[hk]) + bias_t
        sink = sink_row[:, hk * half:(hk + 1) * half]
        m = jnp.maximum(jnp.max(s, axis=0, keepdims=True), sink)
        p = jnp.exp(s - m)
        e_sink = jnp.exp(sink - m)
        inv = 1.0 / (jnp.sum(p, axis=0, keepdims=True) + e_sink)
        out.append((p * inv, e_sink * inv))
    return out


def attn_fwd(z, sink_row, bias, g_a, pieces=None):
    s = z.shape[0]
    per = min(ATT_BLOCKS_PER_STEP, s // BLOCK)
    tq = per * BLOCK

    def body(q_ref, kv_ref, kvp_ref, sk_ref, b_ref, g_ref, o_ref, y_ref):
        n = pl.program_id(0)
        mlo = lax.broadcasted_iota(jnp.int32, (BLOCK, LANES), 1) < HEAD_DIM
        for b in range(per):
            rows = slice(b * BLOCK, (b + 1) * BLOCK)
            kvp = kvp_ref[...] if b == 0 else kv_ref[(b - 1) * BLOCK:b * BLOCK, :]
            bias_b = b_ref[jnp.minimum(n, 1)] if b == 0 else b_ref[1]
            kx, vx = _attn_band(kv_ref[rows, :], kvp)
            q4 = _stack_heads(q_ref[rows, :] * (HEAD_DIM ** -0.5), mlo)
            probs = _attn_probs(q4, kx, bias_b, sk_ref[...])
            o = _unstack_heads([_dot_tn(probs[hk][0].astype(bf16), vx[hk]) for hk in range(2)], mlo)
            o_ref[rows, :] = o.astype(bf16)
            y, _ = _rms_fwd(o, g_ref[...])
            y_ref[rows, :] = y.astype(bf16)

    return _pcall(
        body, pieces, name="attn_fwd", grid=(s // tq,),
        in_specs=[pl.BlockSpec((tq, ATTN_W), lambda n: (n, 0)),
                  pl.BlockSpec((tq, 2 * KV_W), lambda n: (n, 2)),
                  pl.BlockSpec((BLOCK, 2 * KV_W), lambda n: (jnp.maximum(n * per - 1, 0), 2)),
                  _full((1, ATT_ROWS)), _full((2, 2 * BLOCK, ATT_ROWS // 2)), _full((1, ATTN_W))],
        out_specs=[pl.BlockSpec((tq, ATTN_W), lambda n: (n, 0)), pl.BlockSpec((tq, ATTN_W), lambda n: (n, 0))],
        out_shape=[SDS((s, ATTN_W), bf16), SDS((s, ATTN_W), bf16)],
        operands=[z, z, z, sink_row, bias, g_a], sem=("parallel",))


def _lru_gates(xc, wa, ba, wx, bx, lam):
    xcb = xc.astype(bf16)
    r = _sigmoid(_dot(xcb, wa) + ba)
    ig = _sigmoid(_dot(xcb, wx) + bx)
    sp = _softplus(-lam)
    la = (-LRU_C * r) * sp
    a = jnp.exp(la)
    t = jnp.tanh(la)
    mult = jnp.sqrt(-2.0 * t / (1.0 - t))
    return r, ig, sp, la, a, mult


def branch_fwd(z, p, pieces=None):
    s = z.shape[0]
    tm = _tile(s)
    hb = tm // HALO

    def body(cv_ref, cg_ref, rx_ref, rg_ref, cvh_ref, cgh_ref, rxh_ref,
             cw_ref, cb_ref, lng_ref, lnb_ref, lw_ref, lb_ref, wa_ref, ba_ref, wx_ref, bx_ref, lam_ref, gc_ref, gl_ref,
             conv_ref, hst_ref, nc_ref, nl_ref, carry_ref):
        i = pl.program_id(0)
        first = i == 0

        @pl.when(first)
        def _():
            carry_ref[...] = jnp.zeros_like(carry_ref)

        cval = cv_ref[...].astype(f32)
        u = cval * _sigmoid(cg_ref[...].astype(f32))
        hu = jnp.where(first, 0.0, cvh_ref[...].astype(f32) * _sigmoid(cgh_ref[...].astype(f32)))
        conv = _conv_taps(jnp.concatenate([hu, u], axis=0), cw_ref[...], CONV_K) + cb_ref[...]
        conv_ref[...] = conv
        mu = jnp.mean(conv, axis=-1, keepdims=True)
        xm = conv - mu
        rstd = lax.rsqrt(jnp.mean(xm * xm, axis=-1, keepdims=True) + LN_EPS)
        ln = xm * rstd * lng_ref[...] + lnb_ref[...]
        yc = ln * _sigmoid(ln)
        nc, _ = _rms_fwd(yc, gc_ref[...])
        nc_ref[...] = nc.astype(bf16)

        rx = rx_ref[...].astype(f32)
        hrx = jnp.where(first, 0.0, rxh_ref[...].astype(f32))
        xc = _conv_taps(jnp.concatenate([hrx, rx], axis=0), lw_ref[...], LRU_K) + lb_ref[...]
        r, ig, sp, la, a, mult = _lru_gates(xc, wa_ref[...], ba_ref[...], wx_ref[...], bx_ref[...], lam_ref[...])
        gx = mult * (ig * xc)
        hs = _scan_fwd(a, gx, carry_ref[0:1, :])
        carry_ref[...] = jnp.broadcast_to(hs[tm - 1:tm, :], carry_ref.shape)
        hst_ref[...] = hs
        gl, _ = _gelu(rg_ref[...].astype(f32))
        nl, _ = _rms_fwd(hs * gl, gl_ref[...])
        nl_ref[...] = nl.astype(bf16)

    def col(c):
        return pl.BlockSpec((tm, CONV_W), lambda i: (i, c))

    def halo(c):
        return pl.BlockSpec((HALO, CONV_W), lambda i: (jnp.maximum(i * hb - 1, 0), c))

    small = [p["cw"], p["cb"], p["lng"], p["lnb"], p["lw"], p["lb"], p["wa"], p["ba"], p["wx"], p["bx"], p["lam"],
             p["gc"], p["gl"]]
    row = pl.BlockSpec((tm, CONV_W), lambda i: (i, 0))
    return _pcall(
        body, pieces, name="branch_fwd", grid=(s // tm,),
        in_specs=[col(3), col(4), col(5), col(6), halo(3), halo(4), halo(5)] + [_full(a.shape) for a in small],
        out_specs=[row, row, row, row],
        out_shape=[SDS((s, CONV_W), f32), SDS((s, LRU_W), f32), SDS((s, CONV_W), bf16), SDS((s, LRU_W), bf16)],
        scratch_shapes=[pltpu.VMEM((8, LRU_W), f32)],
        operands=[z, z, z, z, z, z, z, *small], sem=("arbitrary",))


def outproj_fwd(ya, yc, yl, h, w_out, g2, pieces=None):
    s = h.shape[0]
    tm = _tile(s)

    def body(ya_ref, yc_ref, yl_ref, h_ref, w_ref, g_ref, h1_ref, hn_ref):
        y = jnp.concatenate([ya_ref[...], yc_ref[...], yl_ref[...]], axis=1)
        h1 = h_ref[...] + _dot(y, w_ref[...])
        h1_ref[...] = h1
        hn, _ = _rms_fwd(h1, g_ref[...])
        hn_ref[...] = hn.astype(bf16)

    return _pcall(
        body, pieces, name="outproj_fwd", grid=(s // tm,),
        in_specs=[pl.BlockSpec((tm, ATTN_W), lambda i: (i, 0)), pl.BlockSpec((tm, CONV_W), lambda i: (i, 0)),
                  pl.BlockSpec((tm, LRU_W), lambda i: (i, 0)), pl.BlockSpec((tm, D_MODEL), lambda i: (i, 0)),
                  _full((D_MODEL, D_MODEL)), _full((1, D_MODEL))],
        out_specs=[pl.BlockSpec((tm, D_MODEL), lambda i: (i, 0)), pl.BlockSpec((tm, D_MODEL), lambda i: (i, 0))],
        out_shape=[SDS((s, D_MODEL), f32), SDS((s, D_MODEL), bf16)],
        operands=[ya, yc, yl, h, w_out, g2], sem=("parallel",), vmem=VMEM_LIMIT)


def mlp_fwd(hn2, h1, w_up, w_dn, pieces=None):
    s = h1.shape[0]
    tm = _tile(s)

    def body(x_ref, h_ref, wu_ref, wd_ref, up_ref, h2_ref):
        h2_ref[...] = _mlp_chunks(x_ref[...], h_ref[...], wu_ref, wd_ref, up_ref)

    return _pcall(
        body, pieces, name="mlp_fwd", grid=(s // tm,),
        in_specs=[pl.BlockSpec((tm, D_MODEL), lambda i: (i, 0)), pl.BlockSpec((tm, D_MODEL), lambda i: (i, 0)),
                  _full((N_SHARD, D_MODEL, FF_CHUNK), single=True), _full((N_SHARD, FF_CHUNK, D_MODEL), single=True)],
        out_specs=[pl.BlockSpec((tm, D_FF), lambda i: (i, 0)), pl.BlockSpec((tm, D_MODEL), lambda i: (i, 0))],
        out_shape=[SDS((s, D_FF), bf16), SDS((s, D_MODEL), f32)],
        operands=[hn2, h1, w_up, w_dn], sem=("parallel",), vmem=VMEM_LIMIT)


def _mlp_chunks(x, acc, wu_ref, wd_ref, up_ref):
    for c in range(N_SHARD):
        u = _dot(x, wu_ref[c])
        up_ref[:, c * FF_CHUNK:(c + 1) * FF_CHUNK] = u.astype(bf16)
        act = jnp.square(jnp.maximum(u, 0.0)).astype(bf16)
        acc = acc + _dot(act, wd_ref[c])
    return acc


def _final_tile(x, tgt, g, dh_ref, loss_ref, dg_ref):
    y, r = _rms_fwd(x, g)
    err = y - tgt
    part = 0.5 * jnp.sum(jnp.mean(err * err, axis=-1, keepdims=True), axis=0, keepdims=True)
    loss_ref[...] += jnp.broadcast_to(part, loss_ref.shape)
    dx, dg = _rms_bwd(err * (1.0 / D_MODEL), x, r, g)
    dh_ref[...] = dx
    dg_ref[...] += dg


def last_layer_fwd(ya, yc, yl, h, w_out, g2, w_up, w_dn, tgt, gf):
    s = h.shape[0]
    tm = min(256, s)

    def body(ya_ref, yc_ref, yl_ref, h_ref, wo_ref, g2_ref, wu_ref, wd_ref, t_ref, gf_ref,
             h1_ref, hn_ref, up_ref, dh_ref, loss_ref, dg_ref):
        i = pl.program_id(0)

        @pl.when(i == 0)
        def _():
            loss_ref[...] = jnp.zeros_like(loss_ref)
            dg_ref[...] = jnp.zeros_like(dg_ref)

        y = jnp.concatenate([ya_ref[...], yc_ref[...], yl_ref[...]], axis=1)
        h1 = h_ref[...] + _dot(y, wo_ref[...])
        h1_ref[...] = h1
        hn, _ = _rms_fwd(h1, g2_ref[...])
        hn = hn.astype(bf16)
        hn_ref[...] = hn
        h2 = _mlp_chunks(hn, h1, wu_ref, wd_ref, up_ref)
        _final_tile(h2, t_ref[...], gf_ref[...], dh_ref, loss_ref, dg_ref)

    def rowb(w):
        return pl.BlockSpec((tm, w), lambda i: (i, 0))

    return pl.pallas_call(
        body, name="last_layer_fwd", grid=(s // tm,),
        in_specs=[rowb(ATTN_W), rowb(CONV_W), rowb(LRU_W), rowb(D_MODEL), _full((D_MODEL, D_MODEL), single=True),
                  _full((1, D_MODEL)), _full((N_SHARD, D_MODEL, FF_CHUNK), single=True),
                  _full((N_SHARD, FF_CHUNK, D_MODEL), single=True), rowb(D_MODEL), _full((1, D_MODEL))],
        out_specs=[rowb(D_MODEL), rowb(D_MODEL), rowb(D_FF), rowb(D_MODEL), _full((8, LANES)), _full((1, D_MODEL))],
        out_shape=[SDS((s, D_MODEL), f32), SDS((s, D_MODEL), bf16), SDS((s, D_FF), bf16), SDS((s, D_MODEL), f32),
                   SDS((8, LANES), f32), SDS((1, D_MODEL), f32)],
        compiler_params=_params(("arbitrary",), VMEM_LIMIT),
    )(ya, yc, yl, h, w_out, g2, w_up, w_dn, tgt, gf)


def mlp_bwd_act(dh, up, h1, g2, w_up, w_dn, pieces=None):
    s = dh.shape[0]
    tm = _tile(s)

    def body(dh_ref, up_ref, h1_ref, g_ref, wu_ref, wd_ref, dup_ref, dh1_ref, dg_ref):
        i = pl.program_id(0)

        @pl.when(i == 0)
        def _():
            dg_ref[...] = jnp.zeros_like(dg_ref)

        dh = dh_ref[...]
        dhb = dh.astype(bf16)
        d_hn = jnp.zeros((tm, D_MODEL), f32)
        for c in range(N_SHARD):
            d_act = _dot_nt(dhb, wd_ref[c])
            u = up_ref[:, c * FF_CHUNK:(c + 1) * FF_CHUNK].astype(f32)
            d_u = (d_act * (2.0 * jnp.maximum(u, 0.0))).astype(bf16)
            dup_ref[:, c * FF_CHUNK:(c + 1) * FF_CHUNK] = d_u
            d_hn = d_hn + _dot_nt(d_u, wu_ref[c])
        x = h1_ref[...]
        g = g_ref[...]
        _, r = _rms_fwd(x, g)
        dx, dg = _rms_bwd(d_hn, x, r, g)
        dh1_ref[...] = dh + dx
        dg_ref[...] += dg

    return _pcall(
        body, pieces, name="mlp_bwd_act", grid=(s // tm,),
        in_specs=[pl.BlockSpec((tm, D_MODEL), lambda i: (i, 0)), pl.BlockSpec((tm, D_FF), lambda i: (i, 0)),
                  pl.BlockSpec((tm, D_MODEL), lambda i: (i, 0)), _full((1, D_MODEL)),
                  _full((N_SHARD, D_MODEL, FF_CHUNK), single=True), _full((N_SHARD, FF_CHUNK, D_MODEL), single=True)],
        out_specs=[pl.BlockSpec((tm, D_FF), lambda i: (i, 0)), pl.BlockSpec((tm, D_MODEL), lambda i: (i, 0)),
                   _full((1, D_MODEL))],
        out_shape=[SDS((s, D_FF), bf16), SDS((s, D_MODEL), f32), SDS((1, D_MODEL), f32)],
        operands=[dh, up, h1, g2, w_up, w_dn], sem=("arbitrary",), vmem=VMEM_LIMIT)


def mlp_bwd_w(hn2, d_up, up, dh, pieces=None):
    s = dh.shape[0]
    tk = min(1024, s)

    def body(x_ref, du_ref, up_ref, dh_ref, dwu_ref, dwd_ref):
        k = pl.program_id(1)

        @pl.when(k == 0)
        def _():
            dwu_ref[...] = jnp.zeros_like(dwu_ref)
            dwd_ref[...] = jnp.zeros_like(dwd_ref)

        dwu_ref[0] += _dot_tn(x_ref[...], du_ref[...])
        act = jnp.square(jnp.maximum(up_ref[...].astype(f32), 0.0)).astype(bf16)
        dwd_ref[0] += _dot_tn(act, dh_ref[...].astype(bf16))

    return _pcall(
        body, pieces, name="mlp_bwd_w", grid=(N_SHARD, s // tk),
        in_specs=[pl.BlockSpec((tk, D_MODEL), lambda c, k: (k, 0)), pl.BlockSpec((tk, FF_CHUNK), lambda c, k: (k, c)),
                  pl.BlockSpec((tk, FF_CHUNK), lambda c, k: (k, c)), pl.BlockSpec((tk, D_MODEL), lambda c, k: (k, 0))],
        out_specs=[pl.BlockSpec((1, D_MODEL, FF_CHUNK), lambda c, k: (c, 0, 0)),
                   pl.BlockSpec((1, FF_CHUNK, D_MODEL), lambda c, k: (c, 0, 0))],
        out_shape=[SDS((N_SHARD, D_MODEL, FF_CHUNK), f32), SDS((N_SHARD, FF_CHUNK, D_MODEL), f32)],
        operands=[hn2, d_up, up, dh], sem=("parallel", "arbitrary"), vmem=VMEM_LIMIT)


def outproj_bwd(dh1, ya, yc, yl, w_out):
    s = dh1.shape[0]
    tm = min(1024, s)

    def body(dh_ref, ya_ref, yc_ref, yl_ref, w_ref, dy_ref, dw_ref):
        i = pl.program_id(0)

        @pl.when(i == 0)
        def _():
            dw_ref[...] = jnp.zeros_like(dw_ref)

        dhb = dh_ref[...].astype(bf16)
        dy_ref[...] = _dot_nt(dhb, w_ref[...])
        y = jnp.concatenate([ya_ref[...], yc_ref[...], yl_ref[...]], axis=1)
        dw_ref[...] += _dot_tn(y, dhb)

    return pl.pallas_call(
        body, name="outproj_bwd", grid=(s // tm,),
        in_specs=[pl.BlockSpec((tm, D_MODEL), lambda i: (i, 0)), pl.BlockSpec((tm, ATTN_W), lambda i: (i, 0)),
                  pl.BlockSpec((tm, CONV_W), lambda i: (i, 0)), pl.BlockSpec((tm, LRU_W), lambda i: (i, 0)),
                  _full((D_MODEL, D_MODEL))],
        out_specs=[pl.BlockSpec((tm, D_MODEL), lambda i: (i, 0)), _full((D_MODEL, D_MODEL))],
        out_shape=[SDS((s, D_MODEL), f32), SDS((D_MODEL, D_MODEL), f32)],
        compiler_params=_params(("arbitrary",), VMEM_LIMIT),
    )(dh1, ya, yc, yl, w_out)


def attn_bwd(z, o, dy, sink_row, bias, g_a, pieces=None):
    s = z.shape[0]
    per = min(ATT_BLOCKS_PER_STEP, s // BLOCK)
    tq = per * BLOCK
    nt = s // tq

    def body(q_ref, kv_ref, kvp_ref, o_ref, dy_ref, sk_ref, b_ref, g_ref, dq_ref, dkv_ref, dsk_ref, dg_ref,
             carry_ref, dsk_acc):
        i = pl.program_id(0)
        t = nt - 1 - i

        @pl.when(i == 0)
        def _():
            carry_ref[...] = jnp.zeros_like(carry_ref)
            dsk_acc[...] = jnp.zeros_like(dsk_acc)
            dg_ref[...] = jnp.zeros_like(dg_ref)

        mlo = lax.broadcasted_iota(jnp.int32, (BLOCK, LANES), 1) < HEAD_DIM
        lane = lax.broadcasted_iota(jnp.int32, (2 * BLOCK, LANES), 1)
        scale = HEAD_DIM ** -0.5
        half = ATT_ROWS // 2
        g = g_ref[...]
        bands = []
        for b in range(per):
            rows = slice(b * BLOCK, (b + 1) * BLOCK)
            kvp = kvp_ref[...] if b == 0 else kv_ref[(b - 1) * BLOCK:b * BLOCK, :]
            bias_b = b_ref[jnp.minimum(t, 1)] if b == 0 else b_ref[1]
            kx, vx = _attn_band(kv_ref[rows, :], kvp)
            q4 = _stack_heads(q_ref[rows, :] * scale, mlo)
            o_f = o_ref[rows, :].astype(f32)
            _, r = _rms_fwd(o_f, g)
            d_o, dg = _rms_bwd(dy_ref[rows, :], o_f, r, g)
            dg_ref[...] += dg
            do4 = _stack_heads(d_o.astype(bf16), mlo)
            probs = _attn_probs(q4, kx, bias_b, sk_ref[...])
            dq4, tk, tv = [], [], []
            for hk in range(2):
                pr, p_sink = probs[hk]
                d_p = _dot_nt(vx[hk], do4[hk])
                d_row = jnp.sum(pr * d_p, axis=0, keepdims=True)
                d_s = (pr * (d_p - d_row)).astype(bf16)
                dsk_acc[:, hk * half:(hk + 1) * half] -= p_sink * d_row
                dq4.append(_dot_tn(d_s, kx[hk]))
                tk.append(_dot(d_s, q4[hk]))
                tv.append(_dot(pr.astype(bf16), do4[hk]))
            dq_ref[rows, :] = (_unstack_heads(dq4, mlo) * scale).astype(bf16)
            fk = [x + pltpu.roll(x, HEAD_DIM, 1) for x in tk]
            fv = [x + pltpu.roll(x, HEAD_DIM, 1) for x in tv]
            bands.append(jnp.concatenate([jnp.where(lane < HEAD_DIM, fk[0], fk[1]),
                                          jnp.where(lane < HEAD_DIM, fv[0], fv[1])], axis=1))
        for b in range(per):
            after = bands[b + 1][:BLOCK] if b + 1 < per else carry_ref[...]
            dkv_ref[b * BLOCK:(b + 1) * BLOCK, :] = (bands[b][BLOCK:] + after).astype(bf16)
        carry_ref[...] = bands[0][:BLOCK]

        @pl.when(i == nt - 1)
        def _():
            for hh in range(N_HEADS):
                tot = jnp.sum(dsk_acc[:, hh * BLOCK:(hh + 1) * BLOCK], axis=1, keepdims=True)
                dsk_ref[hh:hh + 1, :] = jnp.broadcast_to(tot, (1, LANES))

    def rev(width, col):
        return pl.BlockSpec((tq, width), lambda i: (nt - 1 - i, col))

    return _pcall(
        body, pieces, name="attn_bwd", grid=(nt,),
        in_specs=[rev(ATTN_W, 0), rev(2 * KV_W, 2),
                  pl.BlockSpec((BLOCK, 2 * KV_W), lambda i: (jnp.maximum((nt - 1 - i) * per - 1, 0), 2)),
                  rev(ATTN_W, 0), rev(ATTN_W, 0),
                  _full((1, ATT_ROWS)), _full((2, 2 * BLOCK, ATT_ROWS // 2)), _full((1, ATTN_W))],
        out_specs=[rev(ATTN_W, 0), rev(2 * KV_W, 0), _full((N_HEADS, LANES)), _full((1, ATTN_W))],
        out_shape=[SDS((s, ATTN_W), bf16), SDS((s, 2 * KV_W), bf16), SDS((N_HEADS, LANES), f32), SDS((1, ATTN_W), f32)],
        scratch_shapes=[pltpu.VMEM((BLOCK, 2 * KV_W), f32), pltpu.VMEM((1, ATT_ROWS), f32)],
        operands=[z, z, z, o, dy, sink_row, bias, g_a], sem=("arbitrary",))


_V_GC, _V_LNG, _V_LNB, _V_CB, _V_GL, _V_BA, _V_BX, _V_LAM, _V_LB = range(9)
_V_ROWS = 16


def branch_bwd_a(z, conv, hst, dy, p, pieces=None):
    s = z.shape[0]
    tm = _tile(s)
    nt = s // tm
    hb = tm // HALO
    h8 = tm // 8

    def body(conv_ref, dyc_ref, dyl_ref, rx_ref, rxh_ref, rg_ref, hst_ref, hsth_ref,
             lng_ref, lnb_ref, lw_ref, lb_ref, wa_ref, ba_ref, wx_ref, bx_ref, lam_ref, gc_ref, gl_ref,
             dconv_ref, dxc_ref, drg_ref, vec_ref, dwa_ref, dwx_ref, carry_ref):
        i = pl.program_id(0)
        ti = nt - 1 - i

        @pl.when(i == 0)
        def _():
            carry_ref[...] = jnp.zeros_like(carry_ref)
            vec_ref[...] = jnp.zeros_like(vec_ref)
            dwa_ref[...] = jnp.zeros_like(dwa_ref)
            dwx_ref[...] = jnp.zeros_like(dwx_ref)

        conv = conv_ref[...]
        mu = jnp.mean(conv, axis=-1, keepdims=True)
        xm = conv - mu
        rstd = lax.rsqrt(jnp.mean(xm * xm, axis=-1, keepdims=True) + LN_EPS)
        xhat = xm * rstd
        lng = lng_ref[...]
        ln = xhat * lng + lnb_ref[...]
        sg = _sigmoid(ln)
        yc = ln * sg
        gc = gc_ref[...]
        _, rc = _rms_fwd(yc, gc)
        d_yc, d_gc = _rms_bwd(dyc_ref[...], yc, rc, gc)
        d_ln = d_yc * (sg * (1.0 + ln * (1.0 - sg)))
        d_xhat = d_ln * lng
        d_conv = rstd * (d_xhat - jnp.mean(d_xhat, axis=-1, keepdims=True)
                         - xhat * jnp.mean(d_xhat * xhat, axis=-1, keepdims=True))
        dconv_ref[...] = d_conv

        rx = rx_ref[...].astype(f32)
        hrx = jnp.where(ti == 0, 0.0, rxh_ref[...].astype(f32))
        xc = _conv_taps(jnp.concatenate([hrx, rx], axis=0), lw_ref[...], LRU_K) + lb_ref[...]
        wa = wa_ref[...]
        wx = wx_ref[...]
        lam = lam_ref[...]
        r, ig, sp, la, a, mult = _lru_gates(xc, wa, ba_ref[...], wx, bx_ref[...], lam)
        hs = hst_ref[...]
        row = lax.broadcasted_iota(jnp.int32, hs.shape, 0)
        h_before = jnp.where(ti == 0, 0.0, hsth_ref[7:8, :])
        h_prev = jnp.where(row == 0, h_before, pltpu.roll(hs, 1, 0))
        rg = rg_ref[...].astype(f32)
        gl, tg = _gelu(rg)
        out = hs * gl
        gmix = gl_ref[...]
        _, rl = _rms_fwd(out, gmix)
        d_out, d_gl = _rms_bwd(dyl_ref[...], out, rl, gmix)
        drg_ref[...] = (d_out * hs * _gelu_grad(rg, tg)).astype(bf16)
        d_h = d_out * gl
        last = row == tm - 1
        a_next = jnp.where(last, 1.0, pltpu.roll(a, tm - 1, 0))
        lmb = _scan_bwd(a_next, d_h, carry_ref[0:1, :])
        carry_ref[...] = jnp.broadcast_to(a[0:1, :] * lmb[0:1, :], carry_ref.shape)
        d_a = lmb * h_prev
        d_mult = lmb * (ig * xc)
        d_ig = lmb * (mult * xc)
        d_la = d_a * a - d_mult * (a * a) / jnp.maximum(mult, 1e-30)
        d_pa = (d_la * (-LRU_C * sp)) * (r * (1.0 - r))
        d_px = d_ig * (ig * (1.0 - ig))
        d_pab = d_pa.astype(bf16)
        d_pxb = d_px.astype(bf16)
        d_xc = lmb * (mult * ig) + _dot_nt(d_pab, wa) + _dot_nt(d_pxb, wx)
        dxc_ref[...] = d_xc
        xcb = xc.astype(bf16)
        dwa_ref[...] += _dot_tn(xcb, d_pab)
        dwx_ref[...] += _dot_tn(xcb, d_pxb)
        d_lam = jnp.sum(d_la * (-LRU_C * r), axis=0, keepdims=True) * (-_sigmoid(-lam))

        def colsum(v):
            return jnp.sum(v, axis=0, keepdims=True)

        rows = [None] * _V_ROWS
        rows[_V_GC] = d_gc
        rows[_V_LNG] = colsum(d_ln * xhat)
        rows[_V_LNB] = colsum(d_ln)
        rows[_V_CB] = colsum(d_conv)
        rows[_V_GL] = d_gl
        rows[_V_BA] = colsum(d_pa)
        rows[_V_BX] = colsum(d_px)
        rows[_V_LAM] = d_lam
        rows[_V_LB] = colsum(d_xc)
        zero = jnp.zeros((1, CONV_W), f32)
        vec_ref[...] += jnp.concatenate([zero if v is None else v for v in rows], axis=0)

    def rev(c):
        return pl.BlockSpec((tm, CONV_W), lambda i: (nt - 1 - i, c))

    small = [p["lng"], p["lnb"], p["lw"], p["lb"], p["wa"], p["ba"], p["wx"], p["bx"], p["lam"], p["gc"], p["gl"]]
    return _pcall(
        body, pieces, name="branch_bwd_a", grid=(nt,),
        in_specs=[rev(0), rev(2), rev(3), rev(5),
                  pl.BlockSpec((HALO, CONV_W), lambda i: (jnp.maximum((nt - 1 - i) * hb - 1, 0), 5)),
                  rev(6), rev(0),
                  pl.BlockSpec((8, LRU_W), lambda i: (jnp.maximum((nt - 1 - i) * h8 - 1, 0), 0))]
                 + [_full(a.shape) for a in small],
        out_specs=[rev(0), rev(0), rev(0), _full((_V_ROWS, CONV_W)), _full((LRU_W, LRU_W)), _full((LRU_W, LRU_W))],
        out_shape=[SDS((s, CONV_W), f32), SDS((s, LRU_W), f32), SDS((s, LRU_W), bf16), SDS((_V_ROWS, CONV_W), f32),
                   SDS((LRU_W, LRU_W), f32), SDS((LRU_W, LRU_W), f32)],
        scratch_shapes=[pltpu.VMEM((8, LRU_W), f32)],
        operands=[conv, dy, dy, z, z, z, hst, hst, *small], sem=("arbitrary",))


def branch_bwd_b(z, d_conv, d_xc, p, pieces=None):
    s = z.shape[0]
    tm = _tile(s)
    nt = s // tm
    hb = tm // HALO

    def body(cv_ref, cg_ref, cvh_ref, cgh_ref, rx_ref, rxh_ref, dc_ref, dch_ref, dx_ref, dxh_ref, cw_ref, lw_ref,
             dzc_ref, dzr_ref, dcw_ref, dlw_ref):
        i = pl.program_id(0)

        @pl.when(i == 0)
        def _():
            dcw_ref[...] = jnp.zeros_like(dcw_ref)
            dlw_ref[...] = jnp.zeros_like(dlw_ref)

        first = i == 0
        last = i == nt - 1
        cval = cv_ref[...].astype(f32)
        sg = _sigmoid(cg_ref[...].astype(f32))
        u = cval * sg
        hu = jnp.where(first, 0.0, cvh_ref[...].astype(f32) * _sigmoid(cgh_ref[...].astype(f32)))
        dpad = jnp.concatenate([dc_ref[...], jnp.where(last, 0.0, dch_ref[...])], axis=0)
        d_u, dw_rows = _conv_taps_bwd(dpad, jnp.concatenate([hu, u], axis=0), cw_ref[...], CONV_K, tm)
        dcw_ref[...] += jnp.concatenate(dw_rows + [jnp.zeros((HALO - CONV_K, CONV_W), f32)], axis=0)
        dzc_ref[...] = jnp.concatenate([d_u * sg, d_u * cval * sg * (1.0 - sg)], axis=1).astype(bf16)

        rx = rx_ref[...].astype(f32)
        hrx = jnp.where(first, 0.0, rxh_ref[...].astype(f32))
        dxpad = jnp.concatenate([dx_ref[...], jnp.where(last, 0.0, dxh_ref[...])], axis=0)
        d_rx, dlw_rows = _conv_taps_bwd(dxpad, jnp.concatenate([hrx, rx], axis=0), lw_ref[...], LRU_K, tm)
        dlw_ref[...] += jnp.concatenate(dlw_rows + [jnp.zeros((8 - LRU_K, LRU_W), f32)], axis=0)
        dzr_ref[...] = d_rx.astype(bf16)

    def col(c):
        return pl.BlockSpec((tm, CONV_W), lambda i: (i, c))

    def prev(c):
        return pl.BlockSpec((HALO, CONV_W), lambda i: (jnp.maximum(i * hb - 1, 0), c))

    nxt = pl.BlockSpec((HALO, CONV_W), lambda i: (jnp.minimum((i + 1) * hb, nt * hb - 1), 0))
    return _pcall(
        body, pieces, name="branch_bwd_b", grid=(nt,),
        in_specs=[col(3), col(4), prev(3), prev(4), col(5), prev(5), col(0), nxt, col(0), nxt,
                  _full(p["cw"].shape), _full(p["lw"].shape)],
        out_specs=[pl.BlockSpec((tm, 2 * CONV_W), lambda i: (i, 0)), pl.BlockSpec((tm, LRU_W), lambda i: (i, 0)),
                   _full((HALO, CONV_W)), _full((8, LRU_W))],
        out_shape=[SDS((s, 2 * CONV_W), bf16), SDS((s, LRU_W), bf16), SDS((HALO, CONV_W), f32), SDS((8, LRU_W), f32)],
        operands=[z, z, z, z, z, z, d_conv, d_conv, d_xc, d_xc, p["cw"], p["lw"]], sem=("arbitrary",))


def inproj_bwd(dq, dkv, dzc, dzr, drg, h, g1, w_in_t, dh1, pieces=None):
    s = h.shape[0]
    tm = _tile(s)

    def body(dq_ref, dkv_ref, dzc_ref, dzr_ref, drg_ref, h_ref, g_ref, w_ref, dh1_ref, dh_ref, dw_ref, dg_ref):
        i = pl.program_id(0)

        @pl.when(i == 0)
        def _():
            dw_ref[...] = jnp.zeros_like(dw_ref)
            dg_ref[...] = jnp.zeros_like(dg_ref)

        dz = jnp.concatenate([dq_ref[...], dkv_ref[...], dzc_ref[...], dzr_ref[...], drg_ref[...]], axis=1)
        x = h_ref[...]
        g = g_ref[...]
        hn, r = _rms_fwd(x, g)
        d_hn = _dot(dz, w_ref[...])
        dw_ref[...] += _dot_tn(dz, hn.astype(bf16))
        dx, dg = _rms_bwd(d_hn, x, r, g)
        dh_ref[...] = dh1_ref[...] + dx
        dg_ref[...] += dg

    def rowb(w):
        return pl.BlockSpec((tm, w), lambda i: (i, 0))

    return _pcall(
        body, pieces, name="inproj_bwd", grid=(s // tm,),
        in_specs=[rowb(ATTN_W), rowb(2 * KV_W), rowb(2 * CONV_W), rowb(LRU_W), rowb(LRU_W), rowb(D_MODEL),
                  _full((1, D_MODEL)), _full((IN_W, D_MODEL)), rowb(D_MODEL)],
        out_specs=[rowb(D_MODEL), _full((IN_W, D_MODEL)), _full((1, D_MODEL))],
        out_shape=[SDS((s, D_MODEL), f32), SDS((IN_W, D_MODEL), f32), SDS((1, D_MODEL), f32)],
        operands=[dq, dkv, dzc, dzr, drg, h, g1, w_in_t, dh1], sem=("arbitrary",), vmem=VMEM_LIMIT)


def _block_diag(w):
    out = jnp.zeros((LRU_W, LRU_W), w.dtype)
    hd = LRU_W // LRU_HEADS
    for hh in range(LRU_HEADS):
        out = out.at[hh * hd:(hh + 1) * hd, hh * hd:(hh + 1) * hd].set(w[hh])
    return out


def _diag_blocks(w):
    hd = LRU_W // LRU_HEADS
    return jnp.stack([w[hh * hd:(hh + 1) * hd, hh * hd:(hh + 1) * hd] for hh in range(LRU_HEADS)])


def _layer_params(sp, l):
    mix = sp["mix_norm"][l]
    return dict(
        g1=sp["norm1"][l][None, :], g2=sp["norm2"][l][None, :],
        sinks=jnp.repeat(sp["attn_sinks"][l], BLOCK)[None, :],
        ga=mix[None, :ATTN_W], gc=mix[None, ATTN_W:ATTN_W + CONV_W], gl=mix[None, ATTN_W + CONV_W:],
        cw=jnp.pad(sp["conv_dw_w"][l], ((0, HALO - CONV_K), (0, 0))), cb=sp["conv_dw_b"][l][None, :],
        lng=sp["conv_ln_g"][l][None, :], lnb=sp["conv_ln_b"][l][None, :],
        lw=jnp.pad(sp["lru_conv_w"][l], ((0, 8 - LRU_K), (0, 0))), lb=sp["lru_conv_b"][l][None, :],
        wa=_block_diag(sp["lru_wa"][l]).astype(bf16), ba=sp["lru_ba"][l].reshape(1, LRU_W),
        wx=_block_diag(sp["lru_wx"][l]).astype(bf16), bx=sp["lru_bx"][l].reshape(1, LRU_W),
        lam=sp["lru_lambda"][l][None, :],
    )


def local_step(x, tgt, big, sp):
    return train_local(x, tgt, sp, LocalWeights(big))


class LocalWeights:
    def __init__(self, big):
        self.big = big
        self.grads = [dict() for _ in range(DEPTH)]

    def weight(self, name, l):
        return self.big[l][name]

    def host(self, point, l):
        return None

    def grad(self, name, l, g):
        self.grads[l][name] = g

    def big_grads(self):
        return self.grads


def train_local(x, tgt, sp, cs):
    lp = [_layer_params(sp, l) for l in range(DEPTH)]
    bias = _attn_bias()
    saved = []
    h = x
    for l in range(DEPTH):
        p = lp[l]
        z = inproj_fwd(h, p["g1"], cs.weight("w_in", l), cs.host("inproj_fwd", l))
        o, ya = attn_fwd(z, p["sinks"], bias, p["ga"], cs.host("attn_fwd", l))
        conv, hst, yc, yl = branch_fwd(z, p, cs.host("branch_fwd", l))
        if l < DEPTH - 1:
            h1, hn2 = outproj_fwd(ya, yc, yl, h, cs.weight("w_out", l), p["g2"], cs.host("outproj_fwd", l))
            up, h_next = mlp_fwd(hn2, h1, cs.weight("w_up", l), cs.weight("w_dn", l), cs.host("mlp_fwd", l))
        else:
            h1, hn2, up, dh, loss, d_gf = last_layer_fwd(
                ya, yc, yl, h, cs.weight("w_out", l), p["g2"], cs.weight("w_up", l), cs.weight("w_dn", l),
                tgt, sp["final_norm"][None, :])
            h_next = None
        saved.append(dict(h=h, z=z, o=o, ya=ya, conv=conv, hst=hst, yc=yc, yl=yl, h1=h1, hn2=hn2, up=up))
        h = h_next
    small_g = [None] * DEPTH
    for l in reversed(range(DEPTH)):
        p, sv = lp[l], saved[l]
        w_up, w_dn = cs.weight("w_up", l), cs.weight("w_dn", l)
        d_up, dh1, d_g2 = mlp_bwd_act(dh, sv["up"], sv["h1"], p["g2"], w_up, w_dn, cs.host("mlp_bwd_act", l))
        dw_up, dw_dn = mlp_bwd_w(sv["hn2"], d_up, sv["up"], dh, cs.host("mlp_bwd_w", l))
        cs.grad("w_up", l, dw_up)
        cs.grad("w_dn", l, dw_dn)
        dy, dw_out = outproj_bwd(dh1, sv["ya"], sv["yc"], sv["yl"], cs.weight("w_out", l))
        cs.grad("w_out", l, dw_out)
        dq, dkv, d_sk, d_ga = attn_bwd(sv["z"], sv["o"], dy, p["sinks"], bias, p["ga"], cs.host("attn_bwd", l))
        d_conv, d_xc, d_rg, vec, dwa, dwx = branch_bwd_a(sv["z"], sv["conv"], sv["hst"], dy, p, cs.host("branch_bwd_a", l))
        dzc, dzr, dcw, dlw = branch_bwd_b(sv["z"], d_conv, d_xc, p, cs.host("branch_bwd_b", l))
        dh, dw_in, d_g1 = inproj_bwd(dq, dkv, dzc, dzr, d_rg, sv["h"], p["g1"], cs.weight("w_in", l), dh1,
                                     cs.host("inproj_bwd", l))
        cs.grad("w_in", l, dw_in)
        hd = LRU_W // LRU_HEADS
        small_g[l] = dict(
            norm1=d_g1[0], attn_sinks=d_sk[:, 0], conv_dw_w=dcw[:CONV_K], conv_dw_b=vec[_V_CB],
            conv_ln_g=vec[_V_LNG], conv_ln_b=vec[_V_LNB], lru_conv_w=dlw[:LRU_K], lru_conv_b=vec[_V_LB],
            lru_wa=_diag_blocks(dwa), lru_ba=vec[_V_BA].reshape(LRU_HEADS, hd),
            lru_wx=_diag_blocks(dwx), lru_bx=vec[_V_BX].reshape(LRU_HEADS, hd), lru_lambda=vec[_V_LAM],
            mix_norm=jnp.concatenate([d_ga[0], vec[_V_GC], vec[_V_GL]]), norm2=d_g2[0],
        )
    return loss, dh, cs.big_grads(), small_g, d_gf[0]


_HBM = pl.BlockSpec(memory_space=pl.ANY)


def _place():
    x, y, c = lax.axis_index("x"), lax.axis_index("y"), lax.axis_index("c")
    chips = [(1 - x, y), (x, 1 - y), (1 - x, 1 - y)]
    return x, y, c, chips


class Comm:
    def __init__(self, ins, out_shape, aliases, sems, start, finish, done):
        self.ins, self.out_shape, self.aliases, self.sems = list(ins), list(out_shape), dict(aliases), list(sems)
        self.start, self.finish, self.done = start, finish, done


def _pcall(body, pieces, *, name, grid, in_specs, out_specs, out_shape, operands, scratch_shapes=(), sem, vmem=None,
           prefetch=None):
    in_specs, out_specs, out_shape, scratch_shapes = list(in_specs), list(out_specs), list(out_shape), list(scratch_shapes)
    lead = 0 if prefetch is None else 1

    def call(fn, call_name, ins, outs, shapes, scratch, aliases, semantics, args):
        params = _params(semantics, vmem)
        if prefetch is None:
            return pl.pallas_call(fn, name=call_name, grid=grid, in_specs=ins, out_specs=outs, out_shape=shapes,
                                  scratch_shapes=scratch, input_output_aliases=aliases, compiler_params=params)(*args)
        spec = pltpu.PrefetchScalarGridSpec(num_scalar_prefetch=1, grid=grid, in_specs=ins, out_specs=outs,
                                            scratch_shapes=scratch)
        return pl.pallas_call(fn, name=call_name, grid_spec=spec, out_shape=shapes, input_output_aliases=aliases,
                              compiler_params=params)(prefetch, *args)

    if not pieces:
        return call(body, name, in_specs, out_specs, out_shape, scratch_shapes, {}, sem, operands)
    n_in, n_out, n_scr = len(in_specs), len(out_specs), len(scratch_shapes)
    c_ins = [a for p in pieces for a in p.ins]
    c_outs = [s for p in pieces for s in p.out_shape]
    c_sems = [n for p in pieces for n in p.sems]
    aliases, spans, ki, ko, ks = {}, [], 0, 0, 0
    for p in pieces:
        spans.append((ki, ko, ks))
        for a, b in p.aliases.items():
            aliases[lead + n_in + ki + a] = n_out + ko + b
        ki, ko, ks = ki + len(p.ins), ko + len(p.out_shape), ks + len(p.sems)

    def hosted(*refs):
        pre, refs = refs[:lead], refs[lead:]
        ins, cin = refs[:n_in], refs[n_in:n_in + ki]
        outs, cout = refs[n_in + ki:n_in + ki + n_out], refs[n_in + ki + n_out:n_in + ki + n_out + ko]
        scr, csem = refs[n_in + ki + n_out + ko:n_in + ki + n_out + ko + n_scr], refs[n_in + ki + n_out + ko + n_scr:]
        first = functools.reduce(jnp.logical_and, [pl.program_id(d) == 0 for d in range(len(grid))])
        last = functools.reduce(jnp.logical_and, [pl.program_id(d) == grid[d] - 1 for d in range(len(grid))])

        def each(which):
            for p, (a, b, s) in zip(pieces, spans):
                getattr(p, which)(cin[a:a + len(p.ins)], cout[b:b + len(p.out_shape)], csem[s:s + len(p.sems)])

        @pl.when(first)
        def _():
            each("start")

        body(*pre, *ins, *outs, *scr)

        @pl.when(last)
        def _():
            each("finish")

    res = call(hosted, name + "_host", in_specs + [_HBM] * ki, out_specs + [_HBM] * ko, out_shape + c_outs,
               scratch_shapes + [pltpu.SemaphoreType.DMA((n,)) for n in c_sems], aliases, ("arbitrary",) * len(grid),
               [*operands, *c_ins])
    for p, (a, b, s) in zip(pieces, spans):
        p.done(res[n_out + b:n_out + b + len(p.out_shape)])
    return res[:n_out]


def standalone(pieces, name):
    ki = sum(len(p.ins) for p in pieces)
    ko = sum(len(p.out_shape) for p in pieces)
    spans, a, b, s = [], 0, 0, 0
    aliases = {}
    for p in pieces:
        spans.append((a, b, s))
        for i, o in p.aliases.items():
            aliases[a + i] = b + o
        a, b, s = a + len(p.ins), b + len(p.out_shape), s + len(p.sems)

    def body(*refs):
        cin, cout, csem = refs[:ki], refs[ki:ki + ko], refs[ki + ko:]
        for which in ("start", "finish"):
            for p, (a, b, s) in zip(pieces, spans):
                getattr(p, which)(cin[a:a + len(p.ins)], cout[b:b + len(p.out_shape)], csem[s:s + len(p.sems)])

    res = pl.pallas_call(
        body, name=name, in_specs=[_HBM] * ki, out_specs=[_HBM] * ko, out_shape=[s for p in pieces for s in p.out_shape],
        scratch_shapes=[pltpu.SemaphoreType.DMA((n,)) for p in pieces for n in p.sems], input_output_aliases=aliases,
    )(*[a for p in pieces for a in p.ins])
    for p, (a, b, s) in zip(pieces, spans):
        p.done(res[b:b + len(p.out_shape)])


def _rows_half(ref, which, rows):
    return ref.at[pl.ds(pl.multiple_of(which * rows, 8), rows)]


def gather_ici_piece(bufs, done):
    n = len(bufs)

    def copies(cout):
        x, y, c, chips = _place()
        out = []
        for j, (cx, cy) in enumerate(chips):
            for w in range(n):
                half = bufs[w].shape[1] // 2
                out.append((j, w, _rows_half(cout[w].at[2 * x + y], c, half), _rows_half(cout[w].at[2 * cx + cy], c, half),
                            (cx, cy, c)))
        return out

    def start(cin, cout, sems):
        for j, w, mine, _, to in copies(cout):
            pltpu.make_async_remote_copy(src_ref=mine, dst_ref=mine, send_sem=sems[0].at[n * j + w],
                                         recv_sem=sems[1].at[n * j + w], device_id=to, device_id_type=MESH).start()

    def finish(cin, cout, sems):
        for j, w, mine, landed, to in copies(cout):
            pltpu.make_async_remote_copy(src_ref=mine, dst_ref=landed, send_sem=sems[0].at[n * j + w],
                                         recv_sem=sems[1].at[n * j + w], device_id=to, device_id_type=MESH).wait()

    return Comm(bufs, [SDS(b.shape, b.dtype) for b in bufs], {w: w for w in range(n)}, [3 * n, 3 * n], start, finish, done)


def gather_full_piece(bufs, done):
    n = len(bufs)

    def copies(cout):
        x, y, c, chips = _place()
        return [(n * j + w, cout[w].at[2 * x + y], cout[w].at[2 * cx + cy], (cx, cy, c))
                for j, (cx, cy) in enumerate(chips) for w in range(n)]

    def start(cin, cout, sems):
        for k, mine, _, to in copies(cout):
            pltpu.make_async_remote_copy(src_ref=mine, dst_ref=mine, send_sem=sems[0].at[k], recv_sem=sems[1].at[k],
                                         device_id=to, device_id_type=MESH).start()

    def finish(cin, cout, sems):
        for k, mine, landed, to in copies(cout):
            pltpu.make_async_remote_copy(src_ref=mine, dst_ref=landed, send_sem=sems[0].at[k], recv_sem=sems[1].at[k],
                                         device_id=to, device_id_type=MESH).wait()

    return Comm(bufs, [SDS(b.shape, b.dtype) for b in bufs], {w: w for w in range(n)}, [3 * n, 3 * n], start, finish, done)


def gather_d2d_piece(bufs, done):
    n = len(bufs)

    def copies(cout):
        x, y, c, chips = _place()
        out = []
        for j, (cx, cy) in enumerate(chips):
            for w in range(n):
                half = bufs[w].shape[1] // 2
                slot = cout[w].at[2 * cx + cy]
                out.append((n * j + w, _rows_half(slot, c, half), _rows_half(slot, 1 - c, half), (x, y, 1 - c)))
        return out

    def start(cin, cout, sems):
        for k, mine, _, to in copies(cout):
            pltpu.make_async_remote_copy(src_ref=mine, dst_ref=mine, send_sem=sems[0].at[k], recv_sem=sems[1].at[k],
                                         device_id=to, device_id_type=MESH).start()

    def finish(cin, cout, sems):
        for k, mine, theirs, to in copies(cout):
            pltpu.make_async_remote_copy(src_ref=mine, dst_ref=theirs, send_sem=sems[0].at[k], recv_sem=sems[1].at[k],
                                         device_id=to, device_id_type=MESH).wait()

    return Comm(bufs, [SDS(b.shape, b.dtype) for b in bufs], {w: w for w in range(n)}, [3 * n, 3 * n], start, finish, done)


def pair_piece(parts, done):
    n = len(parts)

    def copies(cin, cout):
        x, y, c, _ = _place()
        out = []
        for w in range(n):
            half = parts[w].shape[1] // 2
            out.append((w, cin[w].at[:, pl.ds(pl.multiple_of((1 - c) * half, 8), half), :], cout[w], (x, y, 1 - c)))
        return out

    def start(cin, cout, sems):
        for w, src, dst, to in copies(cin, cout):
            pltpu.make_async_remote_copy(src_ref=src, dst_ref=dst, send_sem=sems[0].at[w], recv_sem=sems[1].at[w],
                                         device_id=to, device_id_type=MESH).start()

    def finish(cin, cout, sems):
        for w, src, dst, to in copies(cin, cout):
            pltpu.make_async_remote_copy(src_ref=src, dst_ref=dst, send_sem=sems[0].at[w], recv_sem=sems[1].at[w],
                                         device_id=to, device_id_type=MESH).wait()

    return Comm(parts, [SDS((N_SHARD, a.shape[1] // 2, a.shape[2]), f32) for a in parts], {}, [n, n], start, finish, done)


def shard_piece(sums16, done):
    n = len(sums16)

    def copies(cin, cout):
        x, y, c, chips = _place()
        return [(n * j + w, cin[w].at[2 * cx + cy], cout[w].at[j], (cx, cy, c))
                for j, (cx, cy) in enumerate(chips) for w in range(n)]

    def start(cin, cout, sems):
        for k, src, dst, to in copies(cin, cout):
            pltpu.make_async_remote_copy(src_ref=src, dst_ref=dst, send_sem=sems[0].at[k], recv_sem=sems[1].at[k],
                                         device_id=to, device_id_type=MESH).start()

    def finish(cin, cout, sems):
        for k, src, dst, to in copies(cin, cout):
            pltpu.make_async_remote_copy(src_ref=src, dst_ref=dst, send_sem=sems[0].at[k], recv_sem=sems[1].at[k],
                                         device_id=to, device_id_type=MESH).wait()

    return Comm(sums16, [SDS((3,) + a.shape[1:], bf16) for a in sums16], {}, [3 * n, 3 * n], start, finish, done)


def place_shard(a, layer, idx, dtype):
    _, r, cdim = a.shape
    tr = min(r, 512)

    def body(idx_ref, a_ref, o_ref):
        o_ref[0] = a_ref[0].astype(dtype)

    return pl.pallas_call(
        body, name="place_shard",
        grid_spec=pltpu.PrefetchScalarGridSpec(
            num_scalar_prefetch=1, grid=(r // tr,),
            in_specs=[pl.BlockSpec((1, tr, cdim), lambda i, idx_ref: (layer, i, 0))],
            out_specs=pl.BlockSpec((1, tr, cdim), lambda i, idx_ref: (idx_ref[1], i, 0))),
        out_shape=SDS((N_SHARD, r, cdim), dtype),
        compiler_params=_params(("arbitrary",)),
    )(idx, a)


def place_shards(items, idx, pieces):
    steps = 4
    n = len(items)

    def body(idx_ref, *refs):
        for k in range(n):
            refs[n + k][0] = refs[k][0].astype(bf16)

    in_specs, out_specs, out_shape = [], [], []
    for a, layer in items:
        _, r, cdim = a.shape
        in_specs.append(pl.BlockSpec((1, r // steps, cdim), lambda i, idx_ref, layer=layer: (layer, i, 0)))
        out_specs.append(pl.BlockSpec((1, r // steps, cdim), lambda i, idx_ref: (idx_ref[1], i, 0)))
        out_shape.append(SDS((N_SHARD, r, cdim), bf16))
    return _pcall(body, pieces, name="place_shards", grid=(steps,), in_specs=in_specs, out_specs=out_specs,
                  out_shape=out_shape, operands=[a for a, _ in items], sem=("arbitrary",), prefetch=idx)


_KEYS = ("w_in", "w_out", "w_up", "w_dn")


class MeshWeights:
    def __init__(self, w_in, w_out, w_up, w_down, conv_dw_w, lru_conv_w, idx):
        self.idx = idx
        src = dict(w_in=w_in, w_out=w_out, w_up=w_up, w_dn=w_down)
        self.conv = {(n, l): place_shard(a, l, idx, f32)
                     for n, a in (("cw", conv_dw_w), ("lw", lru_conv_w)) for l in range(DEPTH)}
        self.cache, self.parts, self.sum32, self.sum16, self.got = {}, {}, {}, {}, {}
        first, small = [("w_in", 0)], list(self.conv)
        rest = [(n, l) for n in _KEYS for l in range(DEPTH) if (n, l) not in first]

        def store_conv(outs):
            self.conv.update(zip(small, outs))

        self.buf = {k: place_shard(src[k[0]], k[1], idx, bf16) for k in first}
        placed = place_shards(
            [(src[n], l) for n, l in rest], idx,
            [self._gather(gather_ici_piece, first), gather_full_piece([self.conv[k] for k in small], store_conv)])
        self.buf.update(zip(rest, placed))
        standalone([self._gather(gather_d2d_piece, first)], "gather_first_d2d")

    def _gather(self, piece, keys):
        def done(outs):
            self.buf.update(zip(keys, outs))
        return piece([self.buf[k] for k in keys], done)

    def _pair(self, keys):
        def done(outs):
            for k, recv in zip(keys, outs):
                self.sum32[k], self.sum16[k] = chip_sum(self.parts[k], recv, self.idx)
        return pair_piece([self.parts[k] for k in keys], done)

    def _shard(self, keys):
        def done(outs):
            self.got.update(zip(keys, outs))
        return shard_piece([self.sum16[k] for k in keys], done)

    def conv_weights(self):
        out = []
        for n in ("cw", "lw"):
            a = jnp.stack([self.conv[(n, l)] for l in range(DEPTH)])
            out.append(a.transpose(0, 2, 1, 3).reshape(DEPTH, a.shape[2], N_SHARD * a.shape[3]))
        return out

    def weight(self, name, l):
        if (name, l) not in self.cache:
            b = self.buf[(name, l)]
            if name in ("w_in", "w_out"):
                b = b.reshape(-1, D_MODEL)
            self.cache[(name, l)] = b
        return self.cache[(name, l)]

    def host(self, point, l):
        ici, d2d = gather_ici_piece, gather_d2d_piece
        rest1 = [("w_out", 1), ("w_up", 1), ("w_dn", 1)]
        plan = {
            ("inproj_fwd", 0): lambda: [self._gather(ici, [("w_out", 0)])],
            ("attn_fwd", 0): lambda: [self._gather(ici, [("w_up", 0)]), self._gather(d2d, [("w_out", 0)])],
            ("branch_fwd", 0): lambda: [self._gather(ici, [("w_dn", 0)]), self._gather(d2d, [("w_up", 0)])],
            ("outproj_fwd", 0): lambda: [self._gather(d2d, [("w_dn", 0)]), self._gather(ici, [("w_in", 1)])],
            ("mlp_fwd", 0): lambda: [self._gather(ici, rest1), self._gather(d2d, [("w_in", 1)])],
            ("attn_fwd", 1): lambda: [self._gather(d2d, rest1)],
            ("attn_bwd", 1): lambda: [self._pair([("w_up", 1), ("w_dn", 1)])],
            ("mlp_bwd_act", 0): lambda: [self._shard([("w_up", 1), ("w_dn", 1)])],
            ("mlp_bwd_w", 0): lambda: [self._pair([("w_in", 1), ("w_out", 1)])],
            ("attn_bwd", 0): lambda: [self._shard([("w_in", 1), ("w_out", 1)]),
                                      self._pair([("w_up", 0), ("w_dn", 0), ("w_out", 0)])],
            ("branch_bwd_a", 0): lambda: [self._shard([("w_up", 0)])],
            ("branch_bwd_b", 0): lambda: [self._shard([("w_dn", 0), ("w_out", 0)])],
        }
        make = plan.get((point, l))
        return make() if make else None

    def grad(self, name, l, g):
        if name in ("w_in", "w_out"):
            g = g.reshape(N_SHARD, -1, D_MODEL)
        self.parts[(name, l)] = g

    def big_grads(self):
        return None

    def finish(self, small_vec):
        last = [("w_in", 0)]
        small = {}

        def keep(outs):
            small["buf"] = outs[0]

        standalone([self._pair(last), small_first_piece(small_vec, keep)], "pair_last")
        standalone([self._shard(last), small_second_piece(small["buf"], keep)], "shard_last")
        return self._totals(), small["buf"]

    def _totals(self):
        tots = []
        for n in _KEYS:
            t = None
            for l in reversed(range(DEPTH)):
                t = shard_sum(self.sum32[(n, l)], self.got[(n, l)], self.idx, l, t)
            tots.append(t)
        return tots


def chip_sum(g, recv, idx):
    _, r, cdim = g.shape
    half = r // 2
    tr = min(half, 512)
    nh = half // tr

    def body(idx_ref, g_ref, r_ref, o32_ref, o16_ref):
        tot = g_ref[0] + r_ref[0]
        o16_ref[0] = tot.astype(bf16)

        @pl.when(pl.program_id(0) == N_SHARD - 1)
        def _():
            o32_ref[...] = tot

    def slab(k, idx_ref):
        return lax.rem(idx_ref[1] + 1 + k, N_SHARD)

    def o16_map(k, i, idx_ref):
        return (slab(k, idx_ref), i, 0)

    return pl.pallas_call(
        body, name="chip_sum",
        grid_spec=pltpu.PrefetchScalarGridSpec(
            num_scalar_prefetch=1, grid=(N_SHARD, nh),
            in_specs=[pl.BlockSpec((1, tr, cdim), lambda k, i, idx_ref: (slab(k, idx_ref), idx_ref[0] * nh + i, 0)),
                      pl.BlockSpec((1, tr, cdim), lambda k, i, idx_ref: (slab(k, idx_ref), i, 0))],
            out_specs=[pl.BlockSpec((tr, cdim), lambda k, i, idx_ref: (jnp.where(k == N_SHARD - 1, i, 0), 0)),
                       pl.BlockSpec((1, tr, cdim), o16_map)]),
        out_shape=[SDS((half, cdim), f32), SDS((N_SHARD, half, cdim), bf16)],
        compiler_params=_params(("arbitrary", "arbitrary")),
    )(idx, g, recv)


def shard_sum(sum32, got16, idx, layer, prev):
    half, cdim = sum32.shape
    tr = min(half, 512)

    def body(idx_ref, a_ref, r0_ref, r1_ref, r2_ref, *rest):
        o_ref = rest[-1]
        o_ref[0, 0] = ((a_ref[...] + r0_ref[0].astype(f32)) + r1_ref[0].astype(f32)) + r2_ref[0].astype(f32)

    def rel(j):
        return pl.BlockSpec((1, tr, cdim), lambda i, idx_ref: (j, i, 0))

    in_specs = [pl.BlockSpec((tr, cdim), lambda i, idx_ref: (i, 0)), rel(0), rel(1), rel(2)]
    operands = [idx, sum32, got16, got16, got16]
    aliases = {}
    if prev is not None:
        in_specs.append(_HBM)
        operands.append(prev)
        aliases = {5: 0}
    return pl.pallas_call(
        body, name="shard_sum",
        grid_spec=pltpu.PrefetchScalarGridSpec(
            num_scalar_prefetch=1, grid=(half // tr,), in_specs=in_specs,
            out_specs=pl.BlockSpec((1, 1, tr, cdim), lambda i, idx_ref: (layer, idx_ref[0], i, 0))),
        out_shape=SDS((DEPTH, 2, half, cdim), f32), input_output_aliases=aliases,
        compiler_params=_params(("arbitrary",)),
    )(*operands)


def halves_exchange(tots):
    nw = len(tots)

    def body(*refs):
        bufs = refs[nw:2 * nw]
        send_sem, recv_sem = refs[2 * nw:]
        x, y, c, _ = _place()

        def copy(w, l, half_idx):
            return pltpu.make_async_remote_copy(
                src_ref=bufs[w].at[l, half_idx], dst_ref=bufs[w].at[l, half_idx], send_sem=send_sem.at[DEPTH * w + l],
                recv_sem=recv_sem.at[DEPTH * w + l], device_id=(x, y, 1 - c), device_id_type=MESH)

        sends = [copy(w, l, c) for w in range(nw) for l in range(DEPTH)]
        for cp in sends:
            cp.start()
        for w in range(nw):
            for l in range(DEPTH):
                copy(w, l, 1 - c).wait_recv()
        for cp in sends:
            cp.wait_send()

    return pl.pallas_call(
        body, name="halves_exchange", in_specs=[_HBM] * nw, out_specs=[_HBM] * nw,
        out_shape=[SDS(a.shape, f32) for a in tots], input_output_aliases={w: w for w in range(nw)},
        scratch_shapes=[pltpu.SemaphoreType.DMA((DEPTH * nw,)), pltpu.SemaphoreType.DMA((DEPTH * nw,))],
    )(*tots)


N_DEV = 8


def _dev_index(px, py, pc):
    return 4 * px + 2 * py + pc


def small_first_piece(vec, done):
    def copies(cin, cout):
        x, y, c, chips = _place()
        mine = cout[0].at[_dev_index(x, y, c)]
        peers = [(x, y, 1 - c)] + [(cx, cy, c) for cx, cy in chips]
        return mine, [(k, cout[0].at[_dev_index(*p)], p) for k, p in enumerate(peers)]

    def start(cin, cout, sems):
        mine, peers = copies(cin, cout)
        pltpu.make_async_copy(cin[0], mine, sems[2].at[0]).start()
        for k, _, to in peers:
            pltpu.make_async_remote_copy(src_ref=cin[0], dst_ref=mine, send_sem=sems[0].at[k], recv_sem=sems[1].at[k],
                                         device_id=to, device_id_type=MESH).start()

    def finish(cin, cout, sems):
        mine, peers = copies(cin, cout)
        for k, theirs, to in peers:
            pltpu.make_async_remote_copy(src_ref=cin[0], dst_ref=theirs, send_sem=sems[0].at[k], recv_sem=sems[1].at[k],
                                         device_id=to, device_id_type=MESH).wait()
        pltpu.make_async_copy(cin[0], mine, sems[2].at[0]).wait()

    return Comm([vec], [SDS((N_DEV,) + vec.shape, f32)], {}, [4, 4, 1], start, finish, done)


def small_second_piece(buf, done):
    def copies(cout):
        x, y, c, chips = _place()
        return [(j, cout[0].at[_dev_index(cx, cy, c)], cout[0].at[_dev_index(cx, cy, 1 - c)], (x, y, 1 - c))
                for j, (cx, cy) in enumerate(chips)]

    def start(cin, cout, sems):
        for j, mine, _, to in copies(cout):
            pltpu.make_async_remote_copy(src_ref=mine, dst_ref=mine, send_sem=sems[0].at[j], recv_sem=sems[1].at[j],
                                         device_id=to, device_id_type=MESH).start()

    def finish(cin, cout, sems):
        for j, mine, theirs, to in copies(cout):
            pltpu.make_async_remote_copy(src_ref=mine, dst_ref=theirs, send_sem=sems[0].at[j], recv_sem=sems[1].at[j],
                                         device_id=to, device_id_type=MESH).wait()

    return Comm([buf], [SDS(buf.shape, f32)], {0: 0}, [3, 3], start, finish, done)


def small_sum(buf):
    def body(b_ref, o_ref):
        acc = b_ref[0]
        for d in range(1, N_DEV):
            acc = acc + b_ref[d]
        o_ref[...] = acc

    vm = pl.BlockSpec(memory_space=pltpu.VMEM)
    return pl.pallas_call(body, name="small_sum", in_specs=[vm], out_specs=vm, out_shape=SDS(buf.shape[1:], f32))(buf)


def _adamw_math(w, g, m, v):
    m = ADAM_B1 * m + (1.0 - ADAM_B1) * g
    v = ADAM_B2 * v + (1.0 - ADAM_B2) * (g * g)
    m_hat = m / (1.0 - ADAM_B1 ** ADAM_STEP)
    v_hat = v / (1.0 - ADAM_B2 ** ADAM_STEP)
    delta = -ADAM_LR * (m_hat / (jnp.sqrt(v_hat) + ADAM_EPS) + ADAM_WD * w)
    return delta, m, v


def adamw_big(w, g, m, v):
    _, r, cdim = w.shape
    tr = _row_tile(r, 512)

    def body(w_ref, g_ref, m_ref, v_ref, go_ref, d_ref, mo_ref, vo_ref):
        g = g_ref[...]
        d, mm, vv = _adamw_math(w_ref[...], g, m_ref[...], v_ref[...])
        go_ref[...] = g
        d_ref[...] = d
        mo_ref[...] = mm
        vo_ref[...] = vv

    blk = pl.BlockSpec((1, tr, cdim), lambda l, i: (l, i, 0))
    return pl.pallas_call(
        body, name="adamw_big", grid=(DEPTH, r // tr), in_specs=[blk] * 4, out_specs=[blk] * 4,
        out_shape=[SDS(w.shape, f32)] * 4, compiler_params=_params(("parallel", "parallel"), VMEM_LIMIT),
    )(w, g, m, v)


def adamw_small(ws, gs, ms, vs):
    n = len(ws)

    def body(*refs):
        w_r, g_r, m_r, v_r = refs[:n], refs[n:2 * n], refs[2 * n:3 * n], refs[3 * n:4 * n]
        d_o, m_o, v_o = refs[4 * n:5 * n], refs[5 * n:6 * n], refs[6 * n:7 * n]
        for k in range(n):
            d, mm, vv = _adamw_math(w_r[k][...], g_r[k][...], m_r[k][...], v_r[k][...])
            d_o[k][...] = d
            m_o[k][...] = mm
            v_o[k][...] = vv

    vm = pl.BlockSpec(memory_space=pltpu.VMEM)
    shapes = [SDS(a.shape, f32) for a in ws]
    outs = pl.pallas_call(
        body, name="adamw_small", in_specs=[vm] * (4 * n), out_specs=[vm] * (3 * n), out_shape=shapes * 3,
    )(*ws, *gs, *ms, *vs)
    return outs[:n], outs[n:2 * n], outs[2 * n:]


_BIG = ("w_in", "w_out", "w_up", "w_down")
_WEIGHTS = ("norm1", "w_in", "attn_sinks", "conv_dw_w", "conv_dw_b", "conv_ln_g", "conv_ln_b", "lru_conv_w", "lru_conv_b",
            "lru_wa", "lru_ba", "lru_wx", "lru_bx", "lru_lambda", "mix_norm", "w_out", "norm2", "w_up", "w_down", "final_norm")
_SMALL = tuple(n for n in _WEIGHTS if n not in _BIG)
_SMALL_FULL_SHAPE = dict(
    norm1=(DEPTH, D_MODEL), attn_sinks=(DEPTH, N_HEADS), conv_dw_w=(DEPTH, CONV_K, CONV_W), conv_dw_b=(DEPTH, CONV_W),
    conv_ln_g=(DEPTH, CONV_W), conv_ln_b=(DEPTH, CONV_W), lru_conv_w=(DEPTH, LRU_K, LRU_W), lru_conv_b=(DEPTH, LRU_W),
    lru_wa=(DEPTH, LRU_HEADS, 64, 64), lru_ba=(DEPTH, LRU_HEADS, 64), lru_wx=(DEPTH, LRU_HEADS, 64, 64),
    lru_bx=(DEPTH, LRU_HEADS, 64), lru_lambda=(DEPTH, LRU_W), mix_norm=(DEPTH, D_MODEL), norm2=(DEPTH, D_MODEL),
    final_norm=(D_MODEL,))
_CHANNEL_SHARDED = ("conv_dw_w", "lru_conv_w")


def _pad_lanes(n):
    return -(-n // LANES) * LANES


def _pack(named):
    flat = []
    for a in named:
        a = a.reshape(-1)
        flat.append(jnp.pad(a, (0, _pad_lanes(a.shape[0]) - a.shape[0])))
    v = jnp.concatenate(flat)
    rows = -(-v.shape[0] // (8 * LANES)) * 8
    return jnp.pad(v, (0, rows * LANES - v.shape[0])).reshape(rows, LANES)


def _unpack(vec, shapes):
    flat = vec.reshape(-1)
    out, off = [], 0
    for shp in shapes:
        n = math.prod(shp)
        out.append(flat[off:off + n].reshape(shp))
        off += _pad_lanes(n)
    return out


def _as2d(a):
    return a.reshape(-1, a.shape[-1]) if a.ndim > 1 else a.reshape(1, -1)


def kernel(x, norm1, w_in, attn_sinks, conv_dw_w, conv_dw_b, conv_ln_g, conv_ln_b, lru_conv_w, lru_conv_b, lru_wa, lru_ba, lru_wx, lru_bx, lru_lambda, mix_norm, w_out, norm2, w_up, w_down, final_norm, loss_target, m_norm1, m_w_in, m_attn_sinks, m_conv_dw_w, m_conv_dw_b, m_conv_ln_g, m_conv_ln_b, m_lru_conv_w, m_lru_conv_b, m_lru_wa, m_lru_ba, m_lru_wx, m_lru_bx, m_lru_lambda, m_mix_norm, m_w_out, m_norm2, m_w_up, m_w_down, m_final_norm, v_norm1, v_w_in, v_attn_sinks, v_conv_dw_w, v_conv_dw_b, v_conv_ln_g, v_conv_ln_b, v_lru_conv_w, v_lru_conv_b, v_lru_wa, v_lru_ba, v_lru_wx, v_lru_bx, v_lru_lambda, v_mix_norm, v_w_out, v_norm2, v_w_up, v_w_down, v_final_norm):
    wts = dict(norm1=norm1, w_in=w_in, attn_sinks=attn_sinks, conv_dw_w=conv_dw_w, conv_dw_b=conv_dw_b, conv_ln_g=conv_ln_g,
               conv_ln_b=conv_ln_b, lru_conv_w=lru_conv_w, lru_conv_b=lru_conv_b, lru_wa=lru_wa, lru_ba=lru_ba, lru_wx=lru_wx,
               lru_bx=lru_bx, lru_lambda=lru_lambda, mix_norm=mix_norm, w_out=w_out, norm2=norm2, w_up=w_up, w_down=w_down,
               final_norm=final_norm)
    mom = dict(norm1=m_norm1, w_in=m_w_in, attn_sinks=m_attn_sinks, conv_dw_w=m_conv_dw_w, conv_dw_b=m_conv_dw_b,
               conv_ln_g=m_conv_ln_g, conv_ln_b=m_conv_ln_b, lru_conv_w=m_lru_conv_w, lru_conv_b=m_lru_conv_b, lru_wa=m_lru_wa,
               lru_ba=m_lru_ba, lru_wx=m_lru_wx, lru_bx=m_lru_bx, lru_lambda=m_lru_lambda, mix_norm=m_mix_norm, w_out=m_w_out,
               norm2=m_norm2, w_up=m_w_up, w_down=m_w_down, final_norm=m_final_norm)
    var = dict(norm1=v_norm1, w_in=v_w_in, attn_sinks=v_attn_sinks, conv_dw_w=v_conv_dw_w, conv_dw_b=v_conv_dw_b,
               conv_ln_g=v_conv_ln_g, conv_ln_b=v_conv_ln_b, lru_conv_w=v_lru_conv_w, lru_conv_b=v_lru_conv_b, lru_wa=v_lru_wa,
               lru_ba=v_lru_ba, lru_wx=v_lru_wx, lru_bx=v_lru_bx, lru_lambda=v_lru_lambda, mix_norm=v_mix_norm, w_out=v_w_out,
               norm2=v_norm2, w_up=v_w_up, w_down=v_w_down, final_norm=v_final_norm)

    c_idx = lax.axis_index("c").astype(jnp.int32)
    s_idx = (2 * lax.axis_index("x") + lax.axis_index("y")).astype(jnp.int32)
    idx = jnp.stack([c_idx, s_idx])

    for d in (wts, mom, var):
        d["w_in"] = d["w_in"].transpose(0, 2, 1)

    cs = MeshWeights(wts["w_in"], w_out, w_up, w_down, conv_dw_w, lru_conv_w, idx)
    sp = {n: wts[n] for n in _SMALL}
    sp["conv_dw_w"], sp["lru_conv_w"] = cs.conv_weights()
    loss_blk, grad_x, _, small_g, d_gf = train_local(x[0], loss_target[0], sp, cs)

    stacked = [jnp.stack([small_g[l][n] for l in range(DEPTH)]) for n in _SMALL if n != "final_norm"]
    tots, gathered = cs.finish(_pack(stacked + [d_gf, loss_blk[0, 0:1]]))
    grads_big = {n: a.reshape(wts[n].shape) for n, a in zip(_BIG, halves_exchange(tots))}
    summed = small_sum(gathered)
    names = [n for n in _SMALL if n != "final_norm"] + ["final_norm"]
    unpacked = _unpack(summed, [_SMALL_FULL_SHAPE[n] for n in names] + [(1,)])
    loss = unpacked[-1][0]
    grads = dict(zip(names, unpacked[:-1]))
    for n in _CHANNEL_SHARDED:
        width = wts[n].shape[-1]
        grads[n] = lax.dynamic_slice_in_dim(grads[n], s_idx * width, width, axis=2)
    grads.update(grads_big)

    delta, new_m, new_v = {}, {}, {}
    for n in _BIG:
        grads[n], delta[n], new_m[n], new_v[n] = adamw_big(wts[n], grads[n], mom[n], var[n])
    for d in (grads, delta, new_m, new_v):
        d["w_in"] = d["w_in"].transpose(0, 2, 1)
    sm = list(_SMALL)
    d_s, m_s, v_s = adamw_small([_as2d(wts[n]) for n in sm], [_as2d(grads[n]) for n in sm],
                                [_as2d(mom[n]) for n in sm], [_as2d(var[n]) for n in sm])
    for k, n in enumerate(sm):
        delta[n], new_m[n], new_v[n] = (a.reshape(wts[n].shape) for a in (d_s[k], m_s[k], v_s[k]))

    return (loss, grad_x[None], *[grads[n] for n in _WEIGHTS], *[delta[n] for n in _WEIGHTS],
            *[new_m[n] for n in _WEIGHTS], *[new_v[n] for n in _WEIGHTS])
```
